```python
import math
import jax, jax.numpy as jnp
from jax import lax
import numpy as np

D_MODEL = 1024
BATCH = 8
SEQ = 4096
DEPTH = 2

N_MEM = 256
EPS = 1e-6

RET_WIDTH = D_MODEL // 2
RET_HEADS = 4
RET_HEAD_DIM = RET_WIDTH // RET_HEADS
RET_CHUNK = 128
ROPE_BASE = 10000.0
S5_WIDTH = D_MODEL - RET_WIDTH
S5_GROUP = 16
S5_GROUPS = S5_WIDTH // S5_GROUP
S5_STATE = 64
EVEN_IN = 4 * RET_WIDTH + S5_WIDTH

GDN_HEADS = 8
GDN_HEAD_DIM = D_MODEL // GDN_HEADS
GDN_WIDTH = GDN_HEADS * GDN_HEAD_DIM
GDN_CONV = 4
GDN_CHUNK = 64
ODD_IN = 4 * GDN_WIDTH + 2 * GDN_HEADS

XA_HEADS = 4
XA_HEAD_DIM = D_MODEL // XA_HEADS

FFN_DIM = ((8 * D_MODEL) // 3 + 255) // 256 * 256
FFN_CONV = 3

kernel_name = "hybrid_retention_s5_gdn_convffn"

F32 = jnp.float32


def rmsnorm(x, g):
    xf = x.astype(F32)
    y = xf * lax.rsqrt(jnp.mean(xf * xf, axis=-1, keepdims=True) + EPS)
    return (y * g.astype(F32)).astype(x.dtype)


def causal_dwconv(x, w):
    k_w, ch = w.shape
    return lax.conv_general_dilated(x, w[:, None, :].astype(x.dtype), window_strides=(1,),
                                    padding=[(k_w - 1, 0)],
                                    dimension_numbers=('NWC', 'WIO', 'NWC'),
                                    feature_group_count=ch)


def rotary(x, positions):
    half = x.shape[-1] // 2
    inv = jnp.exp(-math.log(ROPE_BASE) * jnp.arange(half, dtype=F32) / half)
    ang = positions.astype(F32)[:, None] * inv[None, :]
    cos = jnp.cos(ang)[None, :, None, :]
    sin = jnp.sin(ang)[None, :, None, :]
    x1, x2 = x[..., :half], x[..., half:]
    return jnp.concatenate([x1 * cos - x2 * sin, x1 * sin + x2 * cos], axis=-1)


def retention_chunkwise(q, k, v):
    b, h, s, dh = q.shape
    c = RET_CHUNK
    n = s // c
    log_gamma = jnp.log1p(-jnp.exp2(-5.0 - jnp.arange(h, dtype=F32)))
    idx = jnp.arange(c, dtype=F32)
    diff = idx[:, None] - idx[None, :]
    causal = diff >= 0
    intra = jnp.where(causal, jnp.exp(log_gamma[:, None, None] * jnp.where(causal, diff, 0.0)), 0.0)
    q = q.reshape(b, h, n, c, dh)
    k = k.reshape(b, h, n, c, dh)
    v = v.reshape(b, h, n, c, dh)
    scores = jnp.einsum('bhnid,bhnjd->bhnij', q, k) * intra[None, :, None]
    inner = jnp.einsum('bhnij,bhnjd->bhnid', scores, v)
    k_dec = k * jnp.exp(log_gamma[:, None] * (c - 1 - idx))[None, :, None, :, None]
    kv = jnp.einsum('bhnjd,bhnje->nbhde', k_dec, v)
    chunk_decay = jnp.exp(log_gamma * c)[None, :, None, None]

    def step(state, kv_n):
        return state * chunk_decay + kv_n, state

    _, prev = lax.scan(step, jnp.zeros((b, h, dh, dh), F32), kv)
    q_dec = q * jnp.exp(log_gamma[:, None] * (idx + 1))[None, :, None, :, None]
    cross = jnp.einsum('bhnid,nbhde->bhnie', q_dec, prev)
    return (inner + cross).reshape(b, h, s, dh)


def complex_affine_combine(e1, e2):
    a1r, a1i, b1r, b1i = e1
    a2r, a2i, b2r, b2i = e2
    return (a2r * a1r - a2i * a1i,
            a2r * a1i + a2i * a1r,
            a2r * b1r - a2i * b1i + b2r,
            a2r * b1i + a2i * b1r + b2i)


def s5_ssm(u, lam_re, lam_im, b_re, b_im, c_re, c_im, d, log_dt):
    bsz, s, _ = u.shape
    uf = u.astype(F32).reshape(bsz, s, S5_GROUPS, S5_GROUP)
    lr, li = lam_re.astype(F32), lam_im.astype(F32)
    dt = jnp.exp(log_dt.astype(F32))[:, None]
    mag = jnp.exp(lr * dt)
    a_re = mag * jnp.cos(li * dt)
    a_im = mag * jnp.sin(li * dt)
    den = lr * lr + li * li
    z_re = ((a_re - 1.0) * lr + a_im * li) / den
    z_im = (a_im * lr - (a_re - 1.0) * li) / den
    br, bi = b_re.astype(F32), b_im.astype(F32)
    bb_re = z_re[:, None, :] * br - z_im[:, None, :] * bi
    bb_im = z_re[:, None, :] * bi + z_im[:, None, :] * br
    bu_re = jnp.einsum('bsgh,ghp->bsgp', uf, bb_re)
    bu_im = jnp.einsum('bsgh,ghp->bsgp', uf, bb_im)
    elems = (jnp.broadcast_to(a_re, bu_re.shape), jnp.broadcast_to(a_im, bu_re.shape), bu_re, bu_im)
    _, _, st_re, st_im = lax.associative_scan(complex_affine_combine, elems, axis=1)
    y = (jnp.einsum('bsgp,gph->bsgh', st_re, c_re.astype(F32))
         - jnp.einsum('bsgp,gph->bsgh', st_im, c_im.astype(F32))
         + d.astype(F32) * uf)
    return y.reshape(bsz, s, S5_WIDTH)


def even_mixer(h, w_in, ret_norm, lam_re, lam_im, b_re, b_im, c_re, c_im, s5_d, s5_log_dt,
               w_glu, b_glu, w_out):
    bsz, s, _ = h.shape
    proj = h @ w_in
    q, k, v, gate, u = jnp.split(proj, [RET_WIDTH, 2 * RET_WIDTH, 3 * RET_WIDTH, 4 * RET_WIDTH], axis=-1)
    pos = jnp.arange(s)

    def heads(t):
        return t.astype(F32).reshape(bsz, s, RET_HEADS, RET_HEAD_DIM)

    qh = rotary(heads(q), pos)
    kh = rotary(heads(k), pos) * (RET_HEAD_DIM ** -0.5)
    o = retention_chunkwise(qh.transpose(0, 2, 1, 3), kh.transpose(0, 2, 1, 3),
                            heads(v).transpose(0, 2, 1, 3)).transpose(0, 2, 1, 3)
    o = o * lax.rsqrt(jnp.mean(o * o, axis=-1, keepdims=True) + EPS)
    o = o.reshape(bsz, s, RET_WIDTH) * ret_norm.astype(F32) * jax.nn.silu(gate.astype(F32))
    y = s5_ssm(u, lam_re, lam_im, b_re, b_im, c_re, c_im, s5_d, s5_log_dt)
    y = jax.nn.gelu(y)
    y = y * jax.nn.sigmoid(y @ w_glu.astype(F32) + b_glu.astype(F32))
    merged = jnp.concatenate([o, y], axis=-1).astype(h.dtype)
    return merged @ w_out


def gated_delta_chunkwise(q, k, v, g, beta):
    b, h, s, dk = q.shape
    dv = v.shape[-1]
    c = GDN_CHUNK
    n = s // c
    q = q.reshape(b, h, n, c, dk)
    k = k.reshape(b, h, n, c, dk)
    v = v.reshape(b, h, n, c, dv)
    gc = jnp.cumsum(g.reshape(b, h, n, c), axis=-1)
    beta = beta.reshape(b, h, n, c)
    kb = k * beta[..., None]
    vb = v * beta[..., None]
    incl = jnp.tril(jnp.ones((c, c), bool))
    strict = jnp.tril(jnp.ones((c, c), bool), -1)
    gdiff = gc[..., :, None] - gc[..., None, :]
    decay = jnp.where(incl, jnp.exp(jnp.where(incl, gdiff, 0.0)), 0.0)
    a_mat = jnp.where(strict, jnp.einsum('bhnid,bhnjd->bhnij', kb, k) * decay, 0.0)
    eye = jnp.eye(c, dtype=F32)
    t_mat = lax.linalg.triangular_solve(a_mat + eye, jnp.broadcast_to(eye, a_mat.shape),
                                        left_side=True, lower=True)
    w = jnp.einsum('bhnij,bhnjd->bhnid', t_mat, kb * jnp.exp(gc)[..., None])
    u = jnp.einsum('bhnij,bhnjd->bhnid', t_mat, vb)
    qk = jnp.where(incl, jnp.einsum('bhnid,bhnjd->bhnij', q, k) * decay, 0.0)
    q_dec = q * jnp.exp(gc)[..., None]
    k_dec = k * jnp.exp(gc[..., -1:] - gc)[..., None]
    g_last = jnp.exp(gc[..., -1])
    xs = tuple(jnp.moveaxis(t, 2, 0) for t in (q_dec, k_dec, u, w, qk, g_last))

    def step(state, inp):
        qd, kd, un, wn, qkn, gl = inp
        v_new = un - jnp.einsum('bhcd,bhde->bhce', wn, state)
        o = jnp.einsum('bhcd,bhde->bhce', qd, state) + jnp.einsum('bhij,bhje->bhie', qkn, v_new)
        state = state * gl[..., None, None] + jnp.einsum('bhcd,bhce->bhde', kd, v_new)
        return state, o

    _, o = lax.scan(step, jnp.zeros((b, h, dk, dv), F32), xs)
    return jnp.moveaxis(o, 0, 2).reshape(b, h, s, dv)


def odd_mixer(h, w_in, conv_w, a_log, dt_bias, o_norm, w_out):
    bsz, s, _ = h.shape
    proj = h @ w_in
    qkv, z, b_in, a_in = jnp.split(proj, [3 * GDN_WIDTH, 4 * GDN_WIDTH, 4 * GDN_WIDTH + GDN_HEADS], axis=-1)
    qkv = jax.nn.silu(causal_dwconv(qkv, conv_w)).astype(F32)
    q, k, v = jnp.split(qkv, 3, axis=-1)

    def heads(t):
        return t.reshape(bsz, s, GDN_HEADS, GDN_HEAD_DIM).transpose(0, 2, 1, 3)

    def l2n(t):
        return t * lax.rsqrt(jnp.sum(t * t, axis=-1, keepdims=True) + EPS)

    q = l2n(heads(q)) * (GDN_HEAD_DIM ** -0.5)
    k = l2n(heads(k))
    v = heads(v)
    beta = jax.nn.sigmoid(b_in.astype(F32)).transpose(0, 2, 1)
    g = -(jnp.exp(a_log.astype(F32)) * jax.nn.softplus(a_in.astype(F32) + dt_bias.astype(F32)))
    g = g.transpose(0, 2, 1)
    o = gated_delta_chunkwise(q, k, v, g, beta).transpose(0, 2, 1, 3)
    o = o * lax.rsqrt(jnp.mean(o * o, axis=-1, keepdims=True) + EPS) * o_norm.astype(F32)
    o = o * jax.nn.silu(z.astype(F32).reshape(bsz, s, GDN_HEADS, GDN_HEAD_DIM))
    return o.reshape(bsz, s, GDN_WIDTH).astype(h.dtype) @ w_out


def memory_cross_attention(h, mem_n, wq, wkv, wo):
    bsz, s, _ = h.shape
    m = mem_n.shape[1]
    q = (h @ wq).reshape(bsz, s, XA_HEADS, XA_HEAD_DIM)
    k, v = jnp.split(mem_n @ wkv, 2, axis=-1)
    k = k.reshape(bsz, m, XA_HEADS, XA_HEAD_DIM)
    v = v.reshape(bsz, m, XA_HEADS, XA_HEAD_DIM)
    scores = jnp.einsum('bshd,bmhd->bhsm', q, k).astype(F32) * (XA_HEAD_DIM ** -0.5)
    p = jax.nn.softmax(scores, axis=-1).astype(h.dtype)
    o = jnp.einsum('bhsm,bmhd->bshd', p, v).reshape(bsz, s, D_MODEL)
    return o @ wo


def conv_ffn(h, w_up, conv_w, w_down):
    hu = causal_dwconv(h @ w_up, conv_w)
    up, gate = jnp.split(hu, 2, axis=-1)
    return (jax.nn.silu(gate) * up) @ w_down


def _fwd_setup_inputs(seed: int = 0) -> dict:
    key = jax.random.key(seed)
    it = iter(jax.random.split(key, 64))

    def nrm(shape, scale):
        return jax.random.normal(next(it), shape, F32) * scale

    def dense(fan_in, fan_out):
        return nrm((fan_in, fan_out), fan_in ** -0.5)

    def gain(n):
        return 1.0 + nrm((n,), 0.02)

    def log_uniform(shape, lo, hi):
        return jax.random.uniform(next(it), shape, F32, math.log(lo), math.log(hi))

    def common(p):
        return {
            p + "xa_norm": gain(D_MODEL),
            p + "mem_norm": gain(D_MODEL),
            p + "xa_wq": dense(D_MODEL, D_MODEL),
            p + "xa_wkv": dense(D_MODEL, 2 * D_MODEL),
            p + "xa_wo": dense(D_MODEL, D_MODEL),
            p + "ffn_norm": gain(D_MODEL),
            p + "ffn_w_up": dense(D_MODEL, 2 * FFN_DIM),
            p + "ffn_conv": nrm((FFN_CONV, 2 * FFN_DIM), FFN_CONV ** -0.5),
            p + "ffn_w_down": dense(FFN_DIM, D_MODEL),
        }

    out = {
        "x": nrm((BATCH, SEQ, D_MODEL), 1.0),
        "mem": nrm((BATCH, N_MEM, D_MODEL), 1.0),
        "l0_mix_norm": gain(D_MODEL),
        "l0_w_in": dense(D_MODEL, EVEN_IN),
        "l0_ret_norm": gain(RET_WIDTH),
        "l0_s5_lambda_re": -0.5 + nrm((S5_GROUPS, S5_STATE), 0.01),
        "l0_s5_lambda_im": math.pi * jnp.broadcast_to(jnp.arange(S5_STATE, dtype=F32), (S5_GROUPS, S5_STATE))
                           + nrm((S5_GROUPS, S5_STATE), 0.01),
        "l0_s5_b_re": nrm((S5_GROUPS, S5_GROUP, S5_STATE), (2 * S5_GROUP) ** -0.5),
        "l0_s5_b_im": nrm((S5_GROUPS, S5_GROUP, S5_STATE), (2 * S5_GROUP) ** -0.5),
        "l0_s5_c_re": nrm((S5_GROUPS, S5_STATE, S5_GROUP), (2 * S5_STATE) ** -0.5),
        "l0_s5_c_im": nrm((S5_GROUPS, S5_STATE, S5_GROUP), (2 * S5_STATE) ** -0.5),
        "l0_s5_d": nrm((S5_GROUPS, S5_GROUP), 1.0),
        "l0_s5_log_dt": log_uniform((S5_GROUPS,), 1e-3, 1e-1),
        "l0_s5_w_glu": dense(S5_WIDTH, S5_WIDTH),
        "l0_s5_b_glu": nrm((S5_WIDTH,), 0.01),
        "l0_w_out": dense(D_MODEL, D_MODEL),
    }
    out.update(common("l0_"))
    dt = jnp.exp(log_uniform((GDN_HEADS,), 1e-3, 1e-1))
    out.update({
        "l1_mix_norm": gain(D_MODEL),
        "l1_w_in": dense(D_MODEL, ODD_IN),
        "l1_conv": nrm((GDN_CONV, 3 * GDN_WIDTH), GDN_CONV ** -0.5),
        "l1_a_log": jnp.log(jax.random.uniform(next(it), (GDN_HEADS,), F32, 1.0, 16.0)),
        "l1_dt_bias": dt + jnp.log(-jnp.expm1(-dt)),
        "l1_o_norm": gain(GDN_HEAD_DIM),
        "l1_w_out": dense(GDN_WIDTH, D_MODEL),
    })
    out.update(common("l1_"))
    out["final_norm"] = gain(D_MODEL)
    return out


def _fwd_reference(x, mem,
              l0_mix_norm, l0_w_in, l0_ret_norm, l0_s5_lambda_re, l0_s5_lambda_im, l0_s5_b_re, l0_s5_b_im,
              l0_s5_c_re, l0_s5_c_im, l0_s5_d, l0_s5_log_dt, l0_s5_w_glu, l0_s5_b_glu, l0_w_out,
              l0_xa_norm, l0_mem_norm, l0_xa_wq, l0_xa_wkv, l0_xa_wo,
              l0_ffn_norm, l0_ffn_w_up, l0_ffn_conv, l0_ffn_w_down,
              l1_mix_norm, l1_w_in, l1_conv, l1_a_log, l1_dt_bias, l1_o_norm, l1_w_out,
              l1_xa_norm, l1_mem_norm, l1_xa_wq, l1_xa_wkv, l1_xa_wo,
              l1_ffn_norm, l1_ffn_w_up, l1_ffn_conv, l1_ffn_w_down,
              final_norm):
    mixers = (
        lambda h: even_mixer(h, l0_w_in, l0_ret_norm, l0_s5_lambda_re, l0_s5_lambda_im, l0_s5_b_re,
                             l0_s5_b_im, l0_s5_c_re, l0_s5_c_im, l0_s5_d, l0_s5_log_dt,
                             l0_s5_w_glu, l0_s5_b_glu, l0_w_out),
        lambda h: odd_mixer(h, l1_w_in, l1_conv, l1_a_log, l1_dt_bias, l1_o_norm, l1_w_out),
    )
    commons = (
        (l0_mix_norm, l0_xa_norm, l0_mem_norm, l0_xa_wq, l0_xa_wkv, l0_xa_wo,
         l0_ffn_norm, l0_ffn_w_up, l0_ffn_conv, l0_ffn_w_down),
        (l1_mix_norm, l1_xa_norm, l1_mem_norm, l1_xa_wq, l1_xa_wkv, l1_xa_wo,
         l1_ffn_norm, l1_ffn_w_up, l1_ffn_conv, l1_ffn_w_down),
    )
    for i in range(DEPTH):
        (mix_norm, xa_norm, mem_norm, xa_wq, xa_wkv, xa_wo,
         ffn_norm, ffn_w_up, ffn_conv, ffn_w_down) = commons[i]
        x = x + mixers[i](rmsnorm(x, mix_norm))
        x = x + memory_cross_attention(rmsnorm(x, xa_norm), rmsnorm(mem, mem_norm), xa_wq, xa_wkv, xa_wo)
        x = x + conv_ffn(rmsnorm(x, ffn_norm), ffn_w_up, ffn_conv, ffn_w_down)
    return rmsnorm(x, final_norm)


import jax as _jax
import jax.numpy as _jnp

TWIN_FORMAT = 'train_step'
FWD_PARAMS = ['x', 'mem', 'l0_mix_norm', 'l0_w_in', 'l0_ret_norm', 'l0_s5_lambda_re', 'l0_s5_lambda_im', 'l0_s5_b_re', 'l0_s5_b_im', 'l0_s5_c_re', 'l0_s5_c_im', 'l0_s5_d', 'l0_s5_log_dt', 'l0_s5_w_glu', 'l0_s5_b_glu', 'l0_w_out', 'l0_xa_norm', 'l0_mem_norm', 'l0_xa_wq', 'l0_xa_wkv', 'l0_xa_wo', 'l0_ffn_norm', 'l0_ffn_w_up', 'l0_ffn_conv', 'l0_ffn_w_down', 'l1_mix_norm', 'l1_w_in', 'l1_conv', 'l1_a_log', 'l1_dt_bias', 'l1_o_norm', 'l1_w_out', 'l1_xa_norm', 'l1_mem_norm', 'l1_xa_wq', 'l1_xa_wkv', 'l1_xa_wo', 'l1_ffn_norm', 'l1_ffn_w_up', 'l1_ffn_conv', 'l1_ffn_w_down', 'final_norm']
TWIN_WEIGHTS = ['l0_mix_norm', 'l0_w_in', 'l0_ret_norm', 'l0_s5_lambda_re', 'l0_s5_lambda_im', 'l0_s5_b_re', 'l0_s5_b_im', 'l0_s5_c_re', 'l0_s5_c_im', 'l0_s5_d', 'l0_s5_log_dt', 'l0_s5_w_glu', 'l0_s5_b_glu', 'l0_w_out', 'l0_xa_norm', 'l0_mem_norm', 'l0_xa_wq', 'l0_xa_wkv', 'l0_xa_wo', 'l0_ffn_norm', 'l0_ffn_w_up', 'l0_ffn_conv', 'l0_ffn_w_down', 'l1_mix_norm', 'l1_w_in', 'l1_conv', 'l1_a_log', 'l1_dt_bias', 'l1_o_norm', 'l1_w_out', 'l1_xa_norm', 'l1_mem_norm', 'l1_xa_wq', 'l1_xa_wkv', 'l1_xa_wo', 'l1_ffn_norm', 'l1_ffn_w_up', 'l1_ffn_conv', 'l1_ffn_w_down', 'final_norm']
TWIN_DIFF_INPUT = 'x'
TWIN_INPUTS = ['x', 'mem', 'l0_mix_norm', 'l0_w_in', 'l0_ret_norm', 'l0_s5_lambda_re', 'l0_s5_lambda_im', 'l0_s5_b_re', 'l0_s5_b_im', 'l0_s5_c_re', 'l0_s5_c_im', 'l0_s5_d', 'l0_s5_log_dt', 'l0_s5_w_glu', 'l0_s5_b_glu', 'l0_w_out', 'l0_xa_norm', 'l0_mem_norm', 'l0_xa_wq', 'l0_xa_wkv', 'l0_xa_wo', 'l0_ffn_norm', 'l0_ffn_w_up', 'l0_ffn_conv', 'l0_ffn_w_down', 'l1_mix_norm', 'l1_w_in', 'l1_conv', 'l1_a_log', 'l1_dt_bias', 'l1_o_norm', 'l1_w_out', 'l1_xa_norm', 'l1_mem_norm', 'l1_xa_wq', 'l1_xa_wkv', 'l1_xa_wo', 'l1_ffn_norm', 'l1_ffn_w_up', 'l1_ffn_conv', 'l1_ffn_w_down', 'final_norm', 'loss_target', 'm_l0_mix_norm', 'm_l0_w_in', 'm_l0_ret_norm', 'm_l0_s5_lambda_re', 'm_l0_s5_lambda_im', 'm_l0_s5_b_re', 'm_l0_s5_b_im', 'm_l0_s5_c_re', 'm_l0_s5_c_im', 'm_l0_s5_d', 'm_l0_s5_log_dt', 'm_l0_s5_w_glu', 'm_l0_s5_b_glu', 'm_l0_w_out', 'm_l0_xa_norm', 'm_l0_mem_norm', 'm_l0_xa_wq', 'm_l0_xa_wkv', 'm_l0_xa_wo', 'm_l0_ffn_norm', 'm_l0_ffn_w_up', 'm_l0_ffn_conv', 'm_l0_ffn_w_down', 'm_l1_mix_norm', 'm_l1_w_in', 'm_l1_conv', 'm_l1_a_log', 'm_l1_dt_bias', 'm_l1_o_norm', 'm_l1_w_out', 'm_l1_xa_norm', 'm_l1_mem_norm', 'm_l1_xa_wq', 'm_l1_xa_wkv', 'm_l1_xa_wo', 'm_l1_ffn_norm', 'm_l1_ffn_w_up', 'm_l1_ffn_conv', 'm_l1_ffn_w_down', 'm_final_norm', 'v_l0_mix_norm', 'v_l0_w_in', 'v_l0_ret_norm', 'v_l0_s5_lambda_re', 'v_l0_s5_lambda_im', 'v_l0_s5_b_re', 'v_l0_s5_b_im', 'v_l0_s5_c_re', 'v_l0_s5_c_im', 'v_l0_s5_d', 'v_l0_s5_log_dt', 'v_l0_s5_w_glu', 'v_l0_s5_b_glu', 'v_l0_w_out', 'v_l0_xa_norm', 'v_l0_mem_norm', 'v_l0_xa_wq', 'v_l0_xa_wkv', 'v_l0_xa_wo', 'v_l0_ffn_norm', 'v_l0_ffn_w_up', 'v_l0_ffn_conv', 'v_l0_ffn_w_down', 'v_l1_mix_norm', 'v_l1_w_in', 'v_l1_conv', 'v_l1_a_log', 'v_l1_dt_bias', 'v_l1_o_norm', 'v_l1_w_out', 'v_l1_xa_norm', 'v_l1_mem_norm', 'v_l1_xa_wq', 'v_l1_xa_wkv', 'v_l1_xa_wo', 'v_l1_ffn_norm', 'v_l1_ffn_w_up', 'v_l1_ffn_conv', 'v_l1_ffn_w_down', 'v_final_norm']
TWIN_OUTPUTS = ['loss', 'grad_x', 'grad_l0_mix_norm', 'grad_l0_w_in', 'grad_l0_ret_norm', 'grad_l0_s5_lambda_re', 'grad_l0_s5_lambda_im', 'grad_l0_s5_b_re', 'grad_l0_s5_b_im', 'grad_l0_s5_c_re', 'grad_l0_s5_c_im', 'grad_l0_s5_d', 'grad_l0_s5_log_dt', 'grad_l0_s5_w_glu', 'grad_l0_s5_b_glu', 'grad_l0_w_out', 'grad_l0_xa_norm', 'grad_l0_mem_norm', 'grad_l0_xa_wq', 'grad_l0_xa_wkv', 'grad_l0_xa_wo', 'grad_l0_ffn_norm', 'grad_l0_ffn_w_up', 'grad_l0_ffn_conv', 'grad_l0_ffn_w_down', 'grad_l1_mix_norm', 'grad_l1_w_in', 'grad_l1_conv', 'grad_l1_a_log', 'grad_l1_dt_bias', 'grad_l1_o_norm', 'grad_l1_w_out', 'grad_l1_xa_norm', 'grad_l1_mem_norm', 'grad_l1_xa_wq', 'grad_l1_xa_wkv', 'grad_l1_xa_wo', 'grad_l1_ffn_norm', 'grad_l1_ffn_w_up', 'grad_l1_ffn_conv', 'grad_l1_ffn_w_down', 'grad_final_norm', 'delta_l0_mix_norm', 'delta_l0_w_in', 'delta_l0_ret_norm', 'delta_l0_s5_lambda_re', 'delta_l0_s5_lambda_im', 'delta_l0_s5_b_re', 'delta_l0_s5_b_im', 'delta_l0_s5_c_re', 'delta_l0_s5_c_im', 'delta_l0_s5_d', 'delta_l0_s5_log_dt', 'delta_l0_s5_w_glu', 'delta_l0_s5_b_glu', 'delta_l0_w_out', 'delta_l0_xa_norm', 'delta_l0_mem_norm', 'delta_l0_xa_wq', 'delta_l0_xa_wkv', 'delta_l0_xa_wo', 'delta_l0_ffn_norm', 'delta_l0_ffn_w_up', 'delta_l0_ffn_conv', 'delta_l0_ffn_w_down', 'delta_l1_mix_norm', 'delta_l1_w_in', 'delta_l1_conv', 'delta_l1_a_log', 'delta_l1_dt_bias', 'delta_l1_o_norm', 'delta_l1_w_out', 'delta_l1_xa_norm', 'delta_l1_mem_norm', 'delta_l1_xa_wq', 'delta_l1_xa_wkv', 'delta_l1_xa_wo', 'delta_l1_ffn_norm', 'delta_l1_ffn_w_up', 'delta_l1_ffn_conv', 'delta_l1_ffn_w_down', 'delta_final_norm', 'new_m_l0_mix_norm', 'new_m_l0_w_in', 'new_m_l0_ret_norm', 'new_m_l0_s5_lambda_re', 'new_m_l0_s5_lambda_im', 'new_m_l0_s5_b_re', 'new_m_l0_s5_b_im', 'new_m_l0_s5_c_re', 'new_m_l0_s5_c_im', 'new_m_l0_s5_d', 'new_m_l0_s5_log_dt', 'new_m_l0_s5_w_glu', 'new_m_l0_s5_b_glu', 'new_m_l0_w_out', 'new_m_l0_xa_norm', 'new_m_l0_mem_norm', 'new_m_l0_xa_wq', 'new_m_l0_xa_wkv', 'new_m_l0_xa_wo', 'new_m_l0_ffn_norm', 'new_m_l0_ffn_w_up', 'new_m_l0_ffn_conv', 'new_m_l0_ffn_w_down', 'new_m_l1_mix_norm', 'new_m_l1_w_in', 'new_m_l1_conv', 'new_m_l1_a_log', 'new_m_l1_dt_bias', 'new_m_l1_o_norm', 'new_m_l1_w_out', 'new_m_l1_xa_norm', 'new_m_l1_mem_norm', 'new_m_l1_xa_wq', 'new_m_l1_xa_wkv', 'new_m_l1_xa_wo', 'new_m_l1_ffn_norm', 'new_m_l1_ffn_w_up', 'new_m_l1_ffn_conv', 'new_m_l1_ffn_w_down', 'new_m_final_norm', 'new_v_l0_mix_norm', 'new_v_l0_w_in', 'new_v_l0_ret_norm', 'new_v_l0_s5_lambda_re', 'new_v_l0_s5_lambda_im', 'new_v_l0_s5_b_re', 'new_v_l0_s5_b_im', 'new_v_l0_s5_c_re', 'new_v_l0_s5_c_im', 'new_v_l0_s5_d', 'new_v_l0_s5_log_dt', 'new_v_l0_s5_w_glu', 'new_v_l0_s5_b_glu', 'new_v_l0_w_out', 'new_v_l0_xa_norm', 'new_v_l0_mem_norm', 'new_v_l0_xa_wq', 'new_v_l0_xa_wkv', 'new_v_l0_xa_wo', 'new_v_l0_ffn_norm', 'new_v_l0_ffn_w_up', 'new_v_l0_ffn_conv', 'new_v_l0_ffn_w_down', 'new_v_l1_mix_norm', 'new_v_l1_w_in', 'new_v_l1_conv', 'new_v_l1_a_log', 'new_v_l1_dt_bias', 'new_v_l1_o_norm', 'new_v_l1_w_out', 'new_v_l1_xa_norm', 'new_v_l1_mem_norm', 'new_v_l1_xa_wq', 'new_v_l1_xa_wkv', 'new_v_l1_xa_wo', 'new_v_l1_ffn_norm', 'new_v_l1_ffn_w_up', 'new_v_l1_ffn_conv', 'new_v_l1_ffn_w_down', 'new_v_final_norm']
TWIN_LEAF_KINDS = {'loss': 'loss', 'grad_x': 'grad_x', 'grad_l0_mix_norm': 'grad_w', 'grad_l0_w_in': 'grad_w', 'grad_l0_ret_norm': 'grad_w', 'grad_l0_s5_lambda_re': 'grad_w', 'grad_l0_s5_lambda_im': 'grad_w', 'grad_l0_s5_b_re': 'grad_w', 'grad_l0_s5_b_im': 'grad_w', 'grad_l0_s5_c_re': 'grad_w', 'grad_l0_s5_c_im': 'grad_w', 'grad_l0_s5_d': 'grad_w', 'grad_l0_s5_log_dt': 'grad_w', 'grad_l0_s5_w_glu': 'grad_w', 'grad_l0_s5_b_glu': 'grad_w', 'grad_l0_w_out': 'grad_w', 'grad_l0_xa_norm': 'grad_w', 'grad_l0_mem_norm': 'grad_w', 'grad_l0_xa_wq': 'grad_w', 'grad_l0_xa_wkv': 'grad_w', 'grad_l0_xa_wo': 'grad_w', 'grad_l0_ffn_norm': 'grad_w', 'grad_l0_ffn_w_up': 'grad_w', 'grad_l0_ffn_conv': 'grad_w', 'grad_l0_ffn_w_down': 'grad_w', 'grad_l1_mix_norm': 'grad_w', 'grad_l1_w_in': 'grad_w', 'grad_l1_conv': 'grad_w', 'grad_l1_a_log': 'grad_w', 'grad_l1_dt_bias': 'grad_w', 'grad_l1_o_norm': 'grad_w', 'grad_l1_w_out': 'grad_w', 'grad_l1_xa_norm': 'grad_w', 'grad_l1_mem_norm': 'grad_w', 'grad_l1_xa_wq': 'grad_w', 'grad_l1_xa_wkv': 'grad_w', 'grad_l1_xa_wo': 'grad_w', 'grad_l1_ffn_norm': 'grad_w', 'grad_l1_ffn_w_up': 'grad_w', 'grad_l1_ffn_conv': 'grad_w', 'grad_l1_ffn_w_down': 'grad_w', 'grad_final_norm': 'grad_w', 'delta_l0_mix_norm': 'delta_w', 'delta_l0_w_in': 'delta_w', 'delta_l0_ret_norm': 'delta_w', 'delta_l0_s5_lambda_re': 'delta_w', 'delta_l0_s5_lambda_im': 'delta_w', 'delta_l0_s5_b_re': 'delta_w', 'delta_l0_s5_b_im': 'delta_w', 'delta_l0_s5_c_re': 'delta_w', 'delta_l0_s5_c_im': 'delta_w', 'delta_l0_s5_d': 'delta_w', 'delta_l0_s5_log_dt': 'delta_w', 'delta_l0_s5_w_glu': 'delta_w', 'delta_l0_s5_b_glu': 'delta_w', 'delta_l0_w_out': 'delta_w', 'delta_l0_xa_norm': 'delta_w', 'delta_l0_mem_norm': 'delta_w', 'delta_l0_xa_wq': 'delta_w', 'delta_l0_xa_wkv': 'delta_w', 'delta_l0_xa_wo': 'delta_w', 'delta_l0_ffn_norm': 'delta_w', 'delta_l0_ffn_w_up': 'delta_w', 'delta_l0_ffn_conv': 'delta_w', 'delta_l0_ffn_w_down': 'delta_w', 'delta_l1_mix_norm': 'delta_w', 'delta_l1_w_in': 'delta_w', 'delta_l1_conv': 'delta_w', 'delta_l1_a_log': 'delta_w', 'delta_l1_dt_bias': 'delta_w', 'delta_l1_o_norm': 'delta_w', 'delta_l1_w_out': 'delta_w', 'delta_l1_xa_norm': 'delta_w', 'delta_l1_mem_norm': 'delta_w', 'delta_l1_xa_wq': 'delta_w', 'delta_l1_xa_wkv': 'delta_w', 'delta_l1_xa_wo': 'delta_w', 'delta_l1_ffn_norm': 'delta_w', 'delta_l1_ffn_w_up': 'delta_w', 'delta_l1_ffn_conv': 'delta_w', 'delta_l1_ffn_w_down': 'delta_w', 'delta_final_norm': 'delta_w', 'new_m_l0_mix_norm': 'new_m', 'new_m_l0_w_in': 'new_m', 'new_m_l0_ret_norm': 'new_m', 'new_m_l0_s5_lambda_re': 'new_m', 'new_m_l0_s5_lambda_im': 'new_m', 'new_m_l0_s5_b_re': 'new_m', 'new_m_l0_s5_b_im': 'new_m', 'new_m_l0_s5_c_re': 'new_m', 'new_m_l0_s5_c_im': 'new_m', 'new_m_l0_s5_d': 'new_m', 'new_m_l0_s5_log_dt': 'new_m', 'new_m_l0_s5_w_glu': 'new_m', 'new_m_l0_s5_b_glu': 'new_m', 'new_m_l0_w_out': 'new_m', 'new_m_l0_xa_norm': 'new_m', 'new_m_l0_mem_norm': 'new_m', 'new_m_l0_xa_wq': 'new_m', 'new_m_l0_xa_wkv': 'new_m', 'new_m_l0_xa_wo': 'new_m', 'new_m_l0_ffn_norm': 'new_m', 'new_m_l0_ffn_w_up': 'new_m', 'new_m_l0_ffn_conv': 'new_m', 'new_m_l0_ffn_w_down': 'new_m', 'new_m_l1_mix_norm': 'new_m', 'new_m_l1_w_in': 'new_m', 'new_m_l1_conv': 'new_m', 'new_m_l1_a_log': 'new_m', 'new_m_l1_dt_bias': 'new_m', 'new_m_l1_o_norm': 'new_m', 'new_m_l1_w_out': 'new_m', 'new_m_l1_xa_norm': 'new_m', 'new_m_l1_mem_norm': 'new_m', 'new_m_l1_xa_wq': 'new_m', 'new_m_l1_xa_wkv': 'new_m', 'new_m_l1_xa_wo': 'new_m', 'new_m_l1_ffn_norm': 'new_m', 'new_m_l1_ffn_w_up': 'new_m', 'new_m_l1_ffn_conv': 'new_m', 'new_m_l1_ffn_w_down': 'new_m', 'new_m_final_norm': 'new_m', 'new_v_l0_mix_norm': 'new_v', 'new_v_l0_w_in': 'new_v', 'new_v_l0_ret_norm': 'new_v', 'new_v_l0_s5_lambda_re': 'new_v', 'new_v_l0_s5_lambda_im': 'new_v', 'new_v_l0_s5_b_re': 'new_v', 'new_v_l0_s5_b_im': 'new_v', 'new_v_l0_s5_c_re': 'new_v', 'new_v_l0_s5_c_im': 'new_v', 'new_v_l0_s5_d': 'new_v', 'new_v_l0_s5_log_dt': 'new_v', 'new_v_l0_s5_w_glu': 'new_v', 'new_v_l0_s5_b_glu': 'new_v', 'new_v_l0_w_out': 'new_v', 'new_v_l0_xa_norm': 'new_v', 'new_v_l0_mem_norm': 'new_v', 'new_v_l0_xa_wq': 'new_v', 'new_v_l0_xa_wkv': 'new_v', 'new_v_l0_xa_wo': 'new_v', 'new_v_l0_ffn_norm': 'new_v', 'new_v_l0_ffn_w_up': 'new_v', 'new_v_l0_ffn_conv': 'new_v', 'new_v_l0_ffn_w_down': 'new_v', 'new_v_l1_mix_norm': 'new_v', 'new_v_l1_w_in': 'new_v', 'new_v_l1_conv': 'new_v', 'new_v_l1_a_log': 'new_v', 'new_v_l1_dt_bias': 'new_v', 'new_v_l1_o_norm': 'new_v', 'new_v_l1_w_out': 'new_v', 'new_v_l1_xa_norm': 'new_v', 'new_v_l1_mem_norm': 'new_v', 'new_v_l1_xa_wq': 'new_v', 'new_v_l1_xa_wkv': 'new_v', 'new_v_l1_xa_wo': 'new_v', 'new_v_l1_ffn_norm': 'new_v', 'new_v_l1_ffn_w_up': 'new_v', 'new_v_l1_ffn_conv': 'new_v', 'new_v_l1_ffn_w_down': 'new_v', 'new_v_final_norm': 'new_v'}


def _forward(args):
    return _fwd_reference(*[args[k] for k in FWD_PARAMS])


def _output_shape():
    out = _jax.eval_shape(lambda: _forward(_fwd_setup_inputs(0)))
    return out.shape, out.dtype

N_MICROBATCH = 1
ADAM_LR = 0.001
ADAM_B1 = 0.9
ADAM_B2 = 0.999
ADAM_EPS = 1e-08
ADAM_WD = 0.01
ADAM_STEP = 10
PER_EXAMPLE_BATCH_AXIS = {'x': 0, 'mem': 0, 'loss_target': 0}
SHARED_INPUTS = []
_WEIGHT_DTYPES = {'l0_mix_norm': _jnp.float32, 'l0_w_in': _jnp.float32, 'l0_ret_norm': _jnp.float32, 'l0_s5_lambda_re': _jnp.float32, 'l0_s5_lambda_im': _jnp.float32, 'l0_s5_b_re': _jnp.float32, 'l0_s5_b_im': _jnp.float32, 'l0_s5_c_re': _jnp.float32, 'l0_s5_c_im': _jnp.float32, 'l0_s5_d': _jnp.float32, 'l0_s5_log_dt': _jnp.float32, 'l0_s5_w_glu': _jnp.float32, 'l0_s5_b_glu': _jnp.float32, 'l0_w_out': _jnp.float32, 'l0_xa_norm': _jnp.float32, 'l0_mem_norm': _jnp.float32, 'l0_xa_wq': _jnp.float32, 'l0_xa_wkv': _jnp.float32, 'l0_xa_wo': _jnp.float32, 'l0_ffn_norm': _jnp.float32, 'l0_ffn_w_up': _jnp.float32, 'l0_ffn_conv': _jnp.float32, 'l0_ffn_w_down': _jnp.float32, 'l1_mix_norm': _jnp.float32, 'l1_w_in': _jnp.float32, 'l1_conv': _jnp.float32, 'l1_a_log': _jnp.float32, 'l1_dt_bias': _jnp.float32, 'l1_o_norm': _jnp.float32, 'l1_w_out': _jnp.float32, 'l1_xa_norm': _jnp.float32, 'l1_mem_norm': _jnp.float32, 'l1_xa_wq': _jnp.float32, 'l1_xa_wkv': _jnp.float32, 'l1_xa_wo': _jnp.float32, 'l1_ffn_norm': _jnp.float32, 'l1_ffn_w_up': _jnp.float32, 'l1_ffn_conv': _jnp.float32, 'l1_ffn_w_down': _jnp.float32, 'final_norm': _jnp.float32}
MOMENT_SCALE = {'l0_mix_norm': 2.073210e-01, 'l0_w_in': 1.268005e-01, 'l0_ret_norm': 1.324558e-01, 'l0_s5_lambda_re': 4.614154e-03, 'l0_s5_lambda_im': 4.207371e-03, 'l0_s5_b_re': 2.783260e-03, 'l0_s5_b_im': 2.940600e-03, 'l0_s5_c_re': 5.717788e-03, 'l0_s5_c_im': 5.669187e-03, 'l0_s5_d': 1.064832e-01, 'l0_s5_log_dt': 3.491859e+00, 'l0_s5_w_glu': 2.523259e-02, 'l0_s5_b_glu': 4.238004e-02, 'l0_w_out': 1.104300e-01, 'l0_xa_norm': 2.331302e-02, 'l0_mem_norm': 3.525259e-02, 'l0_xa_wq': 2.351315e-02, 'l0_xa_wkv': 2.393267e-02, 'l0_xa_wo': 2.449457e-02, 'l0_ffn_norm': 1.605032e-01, 'l0_ffn_w_up': 6.674500e-02, 'l0_ffn_conv': 6.703723e-02, 'l0_ffn_w_down': 1.087624e-01, 'l1_mix_norm': 1.359600e-01, 'l1_w_in': 6.881611e-02, 'l1_conv': 6.372752e-02, 'l1_a_log': 6.692861e-01, 'l1_dt_bias': 6.610981e-01, 'l1_o_norm': 2.371155e-01, 'l1_w_out': 7.967398e-02, 'l1_xa_norm': 1.467067e-02, 'l1_mem_norm': 2.094123e-02, 'l1_xa_wq': 1.421090e-02, 'l1_xa_wkv': 1.438137e-02, 'l1_xa_wo': 1.457238e-02, 'l1_ffn_norm': 1.082958e-01, 'l1_ffn_w_up': 4.427697e-02, 'l1_ffn_conv': 4.332316e-02, 'l1_ffn_w_down': 7.275469e-02, 'final_norm': 3.199237e+01}


def _to_microbatches(a, axis):
    t = _jnp.moveaxis(a, axis, 0)
    t = t.reshape((N_MICROBATCH, t.shape[0] // N_MICROBATCH) + t.shape[1:])
    return _jnp.moveaxis(t, 1, axis + 1)


def setup_inputs(seed: int = 0) -> dict:
    inp = _fwd_setup_inputs(seed)
    key = _jax.random.fold_in(_jax.random.key(seed), 7919)
    shape, _ = _output_shape()
    out = dict(inp)
    out["loss_target"] = _jax.random.normal(_jax.random.fold_in(key, 0), shape, _jnp.float32)
    for i, name in enumerate(TWIN_WEIGHTS):
        w = inp[name].astype(_jnp.float32)
        if MOMENT_SCALE is None:
            s = _jnp.sqrt(_jnp.mean(_jnp.square(w)) + 1e-30)
        else:
            s = MOMENT_SCALE[name]
        km, kv = _jax.random.split(_jax.random.fold_in(key, i + 1))
        out[name] = w
        out["m_" + name] = s * _jax.random.normal(km, w.shape, _jnp.float32)
        out["v_" + name] = (s * s) * _jax.random.uniform(kv, w.shape, _jnp.float32, 0.5, 1.5)
    if N_MICROBATCH > 1:
        for name, axis in PER_EXAMPLE_BATCH_AXIS.items():
            out[name] = _to_microbatches(out[name], axis)
    return {'x': out['x'], 'mem': out['mem'], 'l0_mix_norm': out['l0_mix_norm'], 'l0_w_in': out['l0_w_in'], 'l0_ret_norm': out['l0_ret_norm'], 'l0_s5_lambda_re': out['l0_s5_lambda_re'], 'l0_s5_lambda_im': out['l0_s5_lambda_im'], 'l0_s5_b_re': out['l0_s5_b_re'], 'l0_s5_b_im': out['l0_s5_b_im'], 'l0_s5_c_re': out['l0_s5_c_re'], 'l0_s5_c_im': out['l0_s5_c_im'], 'l0_s5_d': out['l0_s5_d'], 'l0_s5_log_dt': out['l0_s5_log_dt'], 'l0_s5_w_glu': out['l0_s5_w_glu'], 'l0_s5_b_glu': out['l0_s5_b_glu'], 'l0_w_out': out['l0_w_out'], 'l0_xa_norm': out['l0_xa_norm'], 'l0_mem_norm': out['l0_mem_norm'], 'l0_xa_wq': out['l0_xa_wq'], 'l0_xa_wkv': out['l0_xa_wkv'], 'l0_xa_wo': out['l0_xa_wo'], 'l0_ffn_norm': out['l0_ffn_norm'], 'l0_ffn_w_up': out['l0_ffn_w_up'], 'l0_ffn_conv': out['l0_ffn_conv'], 'l0_ffn_w_down': out['l0_ffn_w_down'], 'l1_mix_norm': out['l1_mix_norm'], 'l1_w_in': out['l1_w_in'], 'l1_conv': out['l1_conv'], 'l1_a_log': out['l1_a_log'], 'l1_dt_bias': out['l1_dt_bias'], 'l1_o_norm': out['l1_o_norm'], 'l1_w_out': out['l1_w_out'], 'l1_xa_norm': out['l1_xa_norm'], 'l1_mem_norm': out['l1_mem_norm'], 'l1_xa_wq': out['l1_xa_wq'], 'l1_xa_wkv': out['l1_xa_wkv'], 'l1_xa_wo': out['l1_xa_wo'], 'l1_ffn_norm': out['l1_ffn_norm'], 'l1_ffn_w_up': out['l1_ffn_w_up'], 'l1_ffn_conv': out['l1_ffn_conv'], 'l1_ffn_w_down': out['l1_ffn_w_down'], 'final_norm': out['final_norm'], 'loss_target': out['loss_target'], 'm_l0_mix_norm': out['m_l0_mix_norm'], 'm_l0_w_in': out['m_l0_w_in'], 'm_l0_ret_norm': out['m_l0_ret_norm'], 'm_l0_s5_lambda_re': out['m_l0_s5_lambda_re'], 'm_l0_s5_lambda_im': out['m_l0_s5_lambda_im'], 'm_l0_s5_b_re': out['m_l0_s5_b_re'], 'm_l0_s5_b_im': out['m_l0_s5_b_im'], 'm_l0_s5_c_re': out['m_l0_s5_c_re'], 'm_l0_s5_c_im': out['m_l0_s5_c_im'], 'm_l0_s5_d': out['m_l0_s5_d'], 'm_l0_s5_log_dt': out['m_l0_s5_log_dt'], 'm_l0_s5_w_glu': out['m_l0_s5_w_glu'], 'm_l0_s5_b_glu': out['m_l0_s5_b_glu'], 'm_l0_w_out': out['m_l0_w_out'], 'm_l0_xa_norm': out['m_l0_xa_norm'], 'm_l0_mem_norm': out['m_l0_mem_norm'], 'm_l0_xa_wq': out['m_l0_xa_wq'], 'm_l0_xa_wkv': out['m_l0_xa_wkv'], 'm_l0_xa_wo': out['m_l0_xa_wo'], 'm_l0_ffn_norm': out['m_l0_ffn_norm'], 'm_l0_ffn_w_up': out['m_l0_ffn_w_up'], 'm_l0_ffn_conv': out['m_l0_ffn_conv'], 'm_l0_ffn_w_down': out['m_l0_ffn_w_down'], 'm_l1_mix_norm': out['m_l1_mix_norm'], 'm_l1_w_in': out['m_l1_w_in'], 'm_l1_conv': out['m_l1_conv'], 'm_l1_a_log': out['m_l1_a_log'], 'm_l1_dt_bias': out['m_l1_dt_bias'], 'm_l1_o_norm': out['m_l1_o_norm'], 'm_l1_w_out': out['m_l1_w_out'], 'm_l1_xa_norm': out['m_l1_xa_norm'], 'm_l1_mem_norm': out['m_l1_mem_norm'], 'm_l1_xa_wq': out['m_l1_xa_wq'], 'm_l1_xa_wkv': out['m_l1_xa_wkv'], 'm_l1_xa_wo': out['m_l1_xa_wo'], 'm_l1_ffn_norm': out['m_l1_ffn_norm'], 'm_l1_ffn_w_up': out['m_l1_ffn_w_up'], 'm_l1_ffn_conv': out['m_l1_ffn_conv'], 'm_l1_ffn_w_down': out['m_l1_ffn_w_down'], 'm_final_norm': out['m_final_norm'], 'v_l0_mix_norm': out['v_l0_mix_norm'], 'v_l0_w_in': out['v_l0_w_in'], 'v_l0_ret_norm': out['v_l0_ret_norm'], 'v_l0_s5_lambda_re': out['v_l0_s5_lambda_re'], 'v_l0_s5_lambda_im': out['v_l0_s5_lambda_im'], 'v_l0_s5_b_re': out['v_l0_s5_b_re'], 'v_l0_s5_b_im': out['v_l0_s5_b_im'], 'v_l0_s5_c_re': out['v_l0_s5_c_re'], 'v_l0_s5_c_im': out['v_l0_s5_c_im'], 'v_l0_s5_d': out['v_l0_s5_d'], 'v_l0_s5_log_dt': out['v_l0_s5_log_dt'], 'v_l0_s5_w_glu': out['v_l0_s5_w_glu'], 'v_l0_s5_b_glu': out['v_l0_s5_b_glu'], 'v_l0_w_out': out['v_l0_w_out'], 'v_l0_xa_norm': out['v_l0_xa_norm'], 'v_l0_mem_norm': out['v_l0_mem_norm'], 'v_l0_xa_wq': out['v_l0_xa_wq'], 'v_l0_xa_wkv': out['v_l0_xa_wkv'], 'v_l0_xa_wo': out['v_l0_xa_wo'], 'v_l0_ffn_norm': out['v_l0_ffn_norm'], 'v_l0_ffn_w_up': out['v_l0_ffn_w_up'], 'v_l0_ffn_conv': out['v_l0_ffn_conv'], 'v_l0_ffn_w_down': out['v_l0_ffn_w_down'], 'v_l1_mix_norm': out['v_l1_mix_norm'], 'v_l1_w_in': out['v_l1_w_in'], 'v_l1_conv': out['v_l1_conv'], 'v_l1_a_log': out['v_l1_a_log'], 'v_l1_dt_bias': out['v_l1_dt_bias'], 'v_l1_o_norm': out['v_l1_o_norm'], 'v_l1_w_out': out['v_l1_w_out'], 'v_l1_xa_norm': out['v_l1_xa_norm'], 'v_l1_mem_norm': out['v_l1_mem_norm'], 'v_l1_xa_wq': out['v_l1_xa_wq'], 'v_l1_xa_wkv': out['v_l1_xa_wkv'], 'v_l1_xa_wo': out['v_l1_xa_wo'], 'v_l1_ffn_norm': out['v_l1_ffn_norm'], 'v_l1_ffn_w_up': out['v_l1_ffn_w_up'], 'v_l1_ffn_conv': out['v_l1_ffn_conv'], 'v_l1_ffn_w_down': out['v_l1_ffn_w_down'], 'v_final_norm': out['v_final_norm']}


def _loss(weights, diff, rest, loss_target):
    with _jax.named_scope("forward"):
        args = {**rest, TWIN_DIFF_INPUT: diff, **{k: w.astype(_WEIGHT_DTYPES[k]) for k, w in weights.items()}}
        y = _forward(args)
    with _jax.named_scope("loss_head"):
        err = _jnp.square(y.astype(_jnp.float32) - loss_target)
        return 0.5 * _jnp.sum(_jnp.mean(err, axis=-1)) if err.ndim else 0.5 * err


def _adamw(w, g, m, v):
    m = ADAM_B1 * m + (1.0 - ADAM_B1) * g
    v = ADAM_B2 * v + (1.0 - ADAM_B2) * _jnp.square(g)
    m_hat = m / (1.0 - ADAM_B1 ** ADAM_STEP)
    v_hat = v / (1.0 - ADAM_B2 ** ADAM_STEP)
    delta = -ADAM_LR * (m_hat / (_jnp.sqrt(v_hat) + ADAM_EPS) + ADAM_WD * w)
    return delta, m, v


def reference(x, mem, l0_mix_norm, l0_w_in, l0_ret_norm, l0_s5_lambda_re, l0_s5_lambda_im, l0_s5_b_re, l0_s5_b_im, l0_s5_c_re, l0_s5_c_im, l0_s5_d, l0_s5_log_dt, l0_s5_w_glu, l0_s5_b_glu, l0_w_out, l0_xa_norm, l0_mem_norm, l0_xa_wq, l0_xa_wkv, l0_xa_wo, l0_ffn_norm, l0_ffn_w_up, l0_ffn_conv, l0_ffn_w_down, l1_mix_norm, l1_w_in, l1_conv, l1_a_log, l1_dt_bias, l1_o_norm, l1_w_out, l1_xa_norm, l1_mem_norm, l1_xa_wq, l1_xa_wkv, l1_xa_wo, l1_ffn_norm, l1_ffn_w_up, l1_ffn_conv, l1_ffn_w_down, final_norm, loss_target, m_l0_mix_norm, m_l0_w_in, m_l0_ret_norm, m_l0_s5_lambda_re, m_l0_s5_lambda_im, m_l0_s5_b_re, m_l0_s5_b_im, m_l0_s5_c_re, m_l0_s5_c_im, m_l0_s5_d, m_l0_s5_log_dt, m_l0_s5_w_glu, m_l0_s5_b_glu, m_l0_w_out, m_l0_xa_norm, m_l0_mem_norm, m_l0_xa_wq, m_l0_xa_wkv, m_l0_xa_wo, m_l0_ffn_norm, m_l0_ffn_w_up, m_l0_ffn_conv, m_l0_ffn_w_down, m_l1_mix_norm, m_l1_w_in, m_l1_conv, m_l1_a_log, m_l1_dt_bias, m_l1_o_norm, m_l1_w_out, m_l1_xa_norm, m_l1_mem_norm, m_l1_xa_wq, m_l1_xa_wkv, m_l1_xa_wo, m_l1_ffn_norm, m_l1_ffn_w_up, m_l1_ffn_conv, m_l1_ffn_w_down, m_final_norm, v_l0_mix_norm, v_l0_w_in, v_l0_ret_norm, v_l0_s5_lambda_re, v_l0_s5_lambda_im, v_l0_s5_b_re, v_l0_s5_b_im, v_l0_s5_c_re, v_l0_s5_c_im, v_l0_s5_d, v_l0_s5_log_dt, v_l0_s5_w_glu, v_l0_s5_b_glu, v_l0_w_out, v_l0_xa_norm, v_l0_mem_norm, v_l0_xa_wq, v_l0_xa_wkv, v_l0_xa_wo, v_l0_ffn_norm, v_l0_ffn_w_up, v_l0_ffn_conv, v_l0_ffn_w_down, v_l1_mix_norm, v_l1_w_in, v_l1_conv, v_l1_a_log, v_l1_dt_bias, v_l1_o_norm, v_l1_w_out, v_l1_xa_norm, v_l1_mem_norm, v_l1_xa_wq, v_l1_xa_wkv, v_l1_xa_wo, v_l1_ffn_norm, v_l1_ffn_w_up, v_l1_ffn_conv, v_l1_ffn_w_down, v_final_norm):
    given = dict(x=x, mem=mem, l0_mix_norm=l0_mix_norm, l0_w_in=l0_w_in, l0_ret_norm=l0_ret_norm, l0_s5_lambda_re=l0_s5_lambda_re, l0_s5_lambda_im=l0_s5_lambda_im, l0_s5_b_re=l0_s5_b_re, l0_s5_b_im=l0_s5_b_im, l0_s5_c_re=l0_s5_c_re, l0_s5_c_im=l0_s5_c_im, l0_s5_d=l0_s5_d, l0_s5_log_dt=l0_s5_log_dt, l0_s5_w_glu=l0_s5_w_glu, l0_s5_b_glu=l0_s5_b_glu, l0_w_out=l0_w_out, l0_xa_norm=l0_xa_norm, l0_mem_norm=l0_mem_norm, l0_xa_wq=l0_xa_wq, l0_xa_wkv=l0_xa_wkv, l0_xa_wo=l0_xa_wo, l0_ffn_norm=l0_ffn_norm, l0_ffn_w_up=l0_ffn_w_up, l0_ffn_conv=l0_ffn_conv, l0_ffn_w_down=l0_ffn_w_down, l1_mix_norm=l1_mix_norm, l1_w_in=l1_w_in, l1_conv=l1_conv, l1_a_log=l1_a_log, l1_dt_bias=l1_dt_bias, l1_o_norm=l1_o_norm, l1_w_out=l1_w_out, l1_xa_norm=l1_xa_norm, l1_mem_norm=l1_mem_norm, l1_xa_wq=l1_xa_wq, l1_xa_wkv=l1_xa_wkv, l1_xa_wo=l1_xa_wo, l1_ffn_norm=l1_ffn_norm, l1_ffn_w_up=l1_ffn_w_up, l1_ffn_conv=l1_ffn_conv, l1_ffn_w_down=l1_ffn_w_down, final_norm=final_norm, loss_target=loss_target, m_l0_mix_norm=m_l0_mix_norm, m_l0_w_in=m_l0_w_in, m_l0_ret_norm=m_l0_ret_norm, m_l0_s5_lambda_re=m_l0_s5_lambda_re, m_l0_s5_lambda_im=m_l0_s5_lambda_im, m_l0_s5_b_re=m_l0_s5_b_re, m_l0_s5_b_im=m_l0_s5_b_im, m_l0_s5_c_re=m_l0_s5_c_re, m_l0_s5_c_im=m_l0_s5_c_im, m_l0_s5_d=m_l0_s5_d, m_l0_s5_log_dt=m_l0_s5_log_dt, m_l0_s5_w_glu=m_l0_s5_w_glu, m_l0_s5_b_glu=m_l0_s5_b_glu, m_l0_w_out=m_l0_w_out, m_l0_xa_norm=m_l0_xa_norm, m_l0_mem_norm=m_l0_mem_norm, m_l0_xa_wq=m_l0_xa_wq, m_l0_xa_wkv=m_l0_xa_wkv, m_l0_xa_wo=m_l0_xa_wo, m_l0_ffn_norm=m_l0_ffn_norm, m_l0_ffn_w_up=m_l0_ffn_w_up, m_l0_ffn_conv=m_l0_ffn_conv, m_l0_ffn_w_down=m_l0_ffn_w_down, m_l1_mix_norm=m_l1_mix_norm, m_l1_w_in=m_l1_w_in, m_l1_conv=m_l1_conv, m_l1_a_log=m_l1_a_log, m_l1_dt_bias=m_l1_dt_bias, m_l1_o_norm=m_l1_o_norm, m_l1_w_out=m_l1_w_out, m_l1_xa_norm=m_l1_xa_norm, m_l1_mem_norm=m_l1_mem_norm, m_l1_xa_wq=m_l1_xa_wq, m_l1_xa_wkv=m_l1_xa_wkv, m_l1_xa_wo=m_l1_xa_wo, m_l1_ffn_norm=m_l1_ffn_norm, m_l1_ffn_w_up=m_l1_ffn_w_up, m_l1_ffn_conv=m_l1_ffn_conv, m_l1_ffn_w_down=m_l1_ffn_w_down, m_final_norm=m_final_norm, v_l0_mix_norm=v_l0_mix_norm, v_l0_w_in=v_l0_w_in, v_l0_ret_norm=v_l0_ret_norm, v_l0_s5_lambda_re=v_l0_s5_lambda_re, v_l0_s5_lambda_im=v_l0_s5_lambda_im, v_l0_s5_b_re=v_l0_s5_b_re, v_l0_s5_b_im=v_l0_s5_b_im, v_l0_s5_c_re=v_l0_s5_c_re, v_l0_s5_c_im=v_l0_s5_c_im, v_l0_s5_d=v_l0_s5_d, v_l0_s5_log_dt=v_l0_s5_log_dt, v_l0_s5_w_glu=v_l0_s5_w_glu, v_l0_s5_b_glu=v_l0_s5_b_glu, v_l0_w_out=v_l0_w_out, v_l0_xa_norm=v_l0_xa_norm, v_l0_mem_norm=v_l0_mem_norm, v_l0_xa_wq=v_l0_xa_wq, v_l0_xa_wkv=v_l0_xa_wkv, v_l0_xa_wo=v_l0_xa_wo, v_l0_ffn_norm=v_l0_ffn_norm, v_l0_ffn_w_up=v_l0_ffn_w_up, v_l0_ffn_conv=v_l0_ffn_conv, v_l0_ffn_w_down=v_l0_ffn_w_down, v_l1_mix_norm=v_l1_mix_norm, v_l1_w_in=v_l1_w_in, v_l1_conv=v_l1_conv, v_l1_a_log=v_l1_a_log, v_l1_dt_bias=v_l1_dt_bias, v_l1_o_norm=v_l1_o_norm, v_l1_w_out=v_l1_w_out, v_l1_xa_norm=v_l1_xa_norm, v_l1_mem_norm=v_l1_mem_norm, v_l1_xa_wq=v_l1_xa_wq, v_l1_xa_wkv=v_l1_xa_wkv, v_l1_xa_wo=v_l1_xa_wo, v_l1_ffn_norm=v_l1_ffn_norm, v_l1_ffn_w_up=v_l1_ffn_w_up, v_l1_ffn_conv=v_l1_ffn_conv, v_l1_ffn_w_down=v_l1_ffn_w_down, v_final_norm=v_final_norm)
    weights = {n: given[n] for n in TWIN_WEIGHTS}
    shared = {n: given[n] for n in SHARED_INPUTS}
    per_example = {n: given[n] for n in ['x', 'mem']}
    grad_fn = _jax.value_and_grad(_loss, argnums=(0, 1))

    def one_microbatch(ex, loss_target):
        ex = dict(ex)
        diff = ex.pop(TWIN_DIFF_INPUT)
        return grad_fn(weights, diff, {**shared, **ex}, loss_target)

    if N_MICROBATCH == 1:
        loss, (grad_w, grad_x) = one_microbatch(per_example, given["loss_target"])
    else:
        def body(carry, xs):
            loss_sum, grad_sum = carry
            l_k, (gw_k, gx_k) = one_microbatch(xs[0], xs[1])
            with _jax.named_scope("update"):
                return (loss_sum + l_k, _jax.tree.map(_jnp.add, grad_sum, gw_k)), gx_k

        init = (_jnp.zeros((), _jnp.float32), _jax.tree.map(_jnp.zeros_like, weights))
        (loss, grad_w), grad_x = _jax.lax.scan(body, init, (per_example, given["loss_target"]))
    with _jax.named_scope("update"):
        delta_w, new_m, new_v = {}, {}, {}
        for n in TWIN_WEIGHTS:
            delta_w[n], new_m[n], new_v[n] = _adamw(weights[n], grad_w[n], given["m_" + n], given["v_" + n])
    return (loss, grad_x, *[grad_w[n] for n in TWIN_WEIGHTS], *[delta_w[n] for n in TWIN_WEIGHTS],
            *[new_m[n] for n in TWIN_WEIGHTS], *[new_v[n] for n in TWIN_WEIGHTS])
```

```python
import functools
import math

import numpy as np
import jax
import jax.numpy as jnp
from jax import lax
from jax.experimental import pallas as pl
from jax.experimental.pallas import tpu as pltpu

F32 = jnp.float32
BF16 = jnp.bfloat16
EPS = 1e-6
MESH = pl.DeviceIdType.MESH

ADAM_LR = 0.001
ADAM_B1 = 0.9
ADAM_B2 = 0.999
ADAM_EPS = 1e-08
ADAM_WD = 0.01
ADAM_STEP = 10

VMEM_LIMIT_BYTES = 56 * 1024 * 1024
LANES = 1024

WEIGHTS = ['l0_mix_norm', 'l0_w_in', 'l0_ret_norm', 'l0_s5_lambda_re', 'l0_s5_lambda_im', 'l0_s5_b_re', 'l0_s5_b_im',
           'l0_s5_c_re', 'l0_s5_c_im', 'l0_s5_d', 'l0_s5_log_dt', 'l0_s5_w_glu', 'l0_s5_b_glu', 'l0_w_out',
           'l0_xa_norm', 'l0_mem_norm', 'l0_xa_wq', 'l0_xa_wkv', 'l0_xa_wo', 'l0_ffn_norm', 'l0_ffn_w_up',
           'l0_ffn_conv', 'l0_ffn_w_down', 'l1_mix_norm', 'l1_w_in', 'l1_conv', 'l1_a_log', 'l1_dt_bias',
           'l1_o_norm', 'l1_w_out', 'l1_xa_norm', 'l1_mem_norm', 'l1_xa_wq', 'l1_xa_wkv', 'l1_xa_wo',
           'l1_ffn_norm', 'l1_ffn_w_up', 'l1_ffn_conv', 'l1_ffn_w_down', 'final_norm']
ARG_NAMES = (['x', 'mem'] + WEIGHTS + ['loss_target'] + ['m_' + w for w in WEIGHTS] + ['v_' + w for w in WEIGHTS])

SHARDED = {
    'l0_w_in': ((1024, 2560), 1, False), 'l0_s5_w_glu': ((512, 512), 0, False), 'l0_w_out': ((1024, 1024), 0, False),
    'l0_xa_wq': ((1024, 1024), 0, False), 'l0_xa_wkv': ((1024, 2048), 1, False), 'l0_xa_wo': ((1024, 1024), 0, False),
    'l0_ffn_w_up': ((1024, 5632), 1, False), 'l0_ffn_conv': ((3, 5632), 1, True),
    'l0_ffn_w_down': ((2816, 1024), 0, False),
    'l1_w_in': ((1024, 4112), 1, False), 'l1_conv': ((4, 3072), 1, True), 'l1_w_out': ((1024, 1024), 0, False),
    'l1_xa_wq': ((1024, 1024), 0, False), 'l1_xa_wkv': ((1024, 2048), 1, False), 'l1_xa_wo': ((1024, 1024), 0, False),
    'l1_ffn_w_up': ((1024, 5632), 1, False), 'l1_ffn_conv': ((3, 5632), 1, True),
    'l1_ffn_w_down': ((2816, 1024), 0, False),
}
SHARDED_NAMES = [w for w in WEIGHTS if w in SHARDED]
SMALL_NAMES = [w for w in WEIGHTS if w not in SHARDED]


def _cparams(sem=None):
    return pltpu.CompilerParams(dimension_semantics=sem, vmem_limit_bytes=VMEM_LIMIT_BYTES)


def _pick(n, cands):
    for c in cands:
        if n % c == 0:
            return c
    return n


_NN = ((1,), (0,))
_NT = ((1,), (1,))
_TN = ((0,), (0,))


def _dot(a, b, dims, hi):
    if hi:
        return lax.dot_general(a.astype(F32), b.astype(F32), (dims, ((), ())), precision=lax.Precision.HIGHEST,
                               preferred_element_type=F32)
    return lax.dot_general(a.astype(BF16), b.astype(BF16), (dims, ((), ())), preferred_element_type=F32)


def _make_mm(hi):
    @jax.custom_vjp
    def nn(a, b):
        return _dot(a, b, _NN, hi)

    def nn_f(a, b):
        return nn(a, b), (a, b)

    def nn_b(r, g):
        a, b = r
        return _dot(g, b, _NT, hi), _dot(a, g, _TN, hi)

    nn.defvjp(nn_f, nn_b)

    @jax.custom_vjp
    def nt(a, b):
        return _dot(a, b, _NT, hi)

    def nt_f(a, b):
        return nt(a, b), (a, b)

    def nt_b(r, g):
        a, b = r
        return _dot(g, b, _NN, hi), _dot(g, a, _TN, hi)

    nt.defvjp(nt_f, nt_b)

    @jax.custom_vjp
    def tn(a, b):
        return _dot(a, b, _TN, hi)

    def tn_f(a, b):
        return tn(a, b), (a, b)

    def tn_b(r, g):
        a, b = r
        return _dot(b, g, _NT, hi), _dot(a, g, _NN, hi)

    tn.defvjp(tn_f, tn_b)
    return nn, nt, tn


mm, mm_nt, mm_tn = _make_mm(False)
mmh, mmh_nt, mmh_tn = _make_mm(True)


@jax.custom_vjp
def _swap_halves(x):
    return pltpu.roll(x, 64, 1)


def _swap_f(x):
    return pltpu.roll(x, 64, 1), None


def _swap_b(_, g):
    return (pltpu.roll(g, 64, 1),)


_swap_halves.defvjp(_swap_f, _swap_b)


def _silu(x):
    return x * jax.nn.sigmoid(x)


def _rms(x, g):
    return x * lax.rsqrt(jnp.mean(x * x, axis=-1, keepdims=True) + EPS) * g


def _iota(shape, dim):
    return lax.broadcasted_iota(jnp.int32, shape, dim)


def _matmul(a, b, mode="nn", res=None, name="mm"):
    if mode == "nn":
        (M, K), (K2, N) = a.shape, b.shape
    elif mode == "nt":
        (M, K), (N, K2) = a.shape, b.shape
    else:
        (K, M), (K2, N) = a.shape, b.shape
    assert K == K2, (a.shape, b.shape, mode)
    tm = _pick(M, (512, 256, 128))
    tn = _pick(N, (512, 256, 128))
    tk = _pick(K, (1024, 512, 256, 128))
    nk = K // tk
    dims = {"nn": _NN, "nt": _NT, "tn": _TN}[mode]
    if mode == "nn":
        a_spec = pl.BlockSpec((tm, tk), lambda i, j, k: (i, k))
        b_spec = pl.BlockSpec((tk, tn), lambda i, j, k: (k, j))
    elif mode == "nt":
        a_spec = pl.BlockSpec((tm, tk), lambda i, j, k: (i, k))
        b_spec = pl.BlockSpec((tn, tk), lambda i, j, k: (j, k))
    else:
        a_spec = pl.BlockSpec((tk, tm), lambda i, j, k: (k, i))
        b_spec = pl.BlockSpec((tk, tn), lambda i, j, k: (k, j))
    o_spec = pl.BlockSpec((tm, tn), lambda i, j, k: (i, j))
    has_res = res is not None

    def kern(*refs):
        if has_res:
            a_ref, b_ref, r_ref, o_ref, acc_ref = refs
        else:
            a_ref, b_ref, o_ref, acc_ref = refs
        k = pl.program_id(2)

        @pl.when(k == 0)
        def _():
            acc_ref[...] = jnp.zeros_like(acc_ref)

        acc_ref[...] += lax.dot_general(a_ref[...].astype(BF16), b_ref[...].astype(BF16), (dims, ((), ())),
                                        preferred_element_type=F32)

        @pl.when(k == nk - 1)
        def _():
            if has_res:
                o_ref[...] = acc_ref[...] + r_ref[...]
            else:
                o_ref[...] = acc_ref[...]

    in_specs = [a_spec, b_spec] + ([o_spec] if has_res else [])
    ops = (a, b) + ((res,) if has_res else ())
    return pl.pallas_call(
        kern, name=name, grid=(M // tm, N // tn, nk), in_specs=in_specs, out_specs=o_spec,
        out_shape=jax.ShapeDtypeStruct((M, N), F32), scratch_shapes=[pltpu.VMEM((tm, tn), F32)],
        compiler_params=_cparams(("parallel", "parallel", "arbitrary")))(*ops)


def _blk(a, width=None, colblk=0):
    return (a, a.shape[1] if width is None else width, colblk)


def _row_specs(blocked, params, ts):
    specs = []
    for (_, w, cb) in blocked:
        specs.append(pl.BlockSpec((ts, w), functools.partial(lambda i, cb: (i, cb), cb=cb)))
    for p in params:
        specs.append(pl.BlockSpec(p.shape, lambda i: (0, 0)))
    return specs


def _rowwise(fn, blocked, params, out_widths, name, ts=256):
    S = blocked[0][0].shape[0]
    ts = min(ts, S)
    nb, npar = len(blocked), len(params)

    def kern(*refs):
        vals = [r[...] for r in refs[:nb + npar]]
        outs = fn(*vals)
        for o_ref, o in zip(refs[nb + npar:], outs):
            o_ref[...] = o

    return pl.pallas_call(
        kern, name=name, grid=(S // ts,), in_specs=_row_specs(blocked, params, ts),
        out_specs=[pl.BlockSpec((ts, w), lambda i: (i, 0)) for w in out_widths],
        out_shape=[jax.ShapeDtypeStruct((S, w), F32) for w in out_widths],
        compiler_params=_cparams(("parallel",)))(*[b[0] for b in blocked], *params)


def _rowwise_bwd(fn, blocked, params, cots, name, blocked_grad=None, param_grad=None, adds=None, ts=256):
    S = blocked[0][0].shape[0]
    ts = min(ts, S)
    nb, npar, nc = len(blocked), len(params), len(cots)
    blocked_grad = [True] * nb if blocked_grad is None else blocked_grad
    param_grad = [True] * npar if param_grad is None else param_grad
    adds = {} if adds is None else adds
    bidx = [i for i in range(nb) if blocked_grad[i]]
    pidx = [i for i in range(npar) if param_grad[i]]
    add_keys = sorted(adds)
    n_in = nb + npar + nc + len(add_keys)

    def kern(*refs):
        i = pl.program_id(0)
        xs = [r[...] for r in refs[:nb]]
        ps = [r[...] for r in refs[nb:nb + npar]]
        gs = [r[...] for r in refs[nb + npar:nb + npar + nc]]
        add_vals = {k: refs[nb + npar + nc + n][...] for n, k in enumerate(add_keys)}
        outs = refs[n_in:]

        def f(*diff):
            full_x = list(xs)
            full_p = list(ps)
            for n, ix in enumerate(bidx):
                full_x[ix] = diff[n]
            for n, ix in enumerate(pidx):
                full_p[ix] = diff[len(bidx) + n]
            return tuple(fn(*full_x, *full_p))

        _, vjp = jax.vjp(f, *[xs[ix] for ix in bidx], *[ps[ix] for ix in pidx])
        grads = vjp(tuple(gs))
        for n, ix in enumerate(bidx):
            g = grads[n]
            if ix in add_vals:
                g = g + add_vals[ix]
            outs[n][...] = g
        for n in range(len(pidx)):
            o_ref = outs[len(bidx) + n]

            @pl.when(i == 0)
            def _(o_ref=o_ref):
                o_ref[...] = jnp.zeros_like(o_ref)

            o_ref[...] += grads[len(bidx) + n]

    in_specs = _row_specs(blocked, params, ts)
    in_specs += [pl.BlockSpec((ts, c.shape[1]), lambda i: (i, 0)) for c in cots]
    in_specs += [pl.BlockSpec((ts, adds[k].shape[1]), lambda i: (i, 0)) for k in add_keys]
    out_specs = [pl.BlockSpec((ts, blocked[ix][1]), lambda i: (i, 0)) for ix in bidx]
    out_specs += [pl.BlockSpec(params[ix].shape, lambda i: (0, 0)) for ix in pidx]
    out_shape = [jax.ShapeDtypeStruct((S, blocked[ix][1]), F32) for ix in bidx]
    out_shape += [jax.ShapeDtypeStruct(params[ix].shape, F32) for ix in pidx]
    return pl.pallas_call(
        kern, name=name, grid=(S // ts,), in_specs=in_specs, out_specs=out_specs, out_shape=out_shape,
        compiler_params=_cparams(("arbitrary",)))(*[b[0] for b in blocked], *params, *cots,
                                                    *[adds[k] for k in add_keys])


def _rms_fn(x, g):
    return (_rms(x, g),)


def _head_norm(o, n_heads, dh):
    outs = []
    for h in range(n_heads):
        oh = o[:, h * dh:(h + 1) * dh]
        outs.append(oh * lax.rsqrt(jnp.mean(oh * oh, axis=-1, keepdims=True) + EPS))
    return outs


def _ret_post_fn(o_raw, gate, ret_norm):
    o = jnp.concatenate(_head_norm(o_raw, 4, 128), axis=1)
    return (o * ret_norm * _silu(gate),)


def _s5_post_fn(y1, y2, u, d, w_glu, b_glu):
    y = y1 - y2 + d * u
    y = jax.nn.gelu(y)
    return (y * jax.nn.sigmoid(mm(y, w_glu) + b_glu),)


def _xattn_fn(q, kv):
    outs = []
    for h in range(4):
        qh = q[:, h * 256:(h + 1) * 256]
        kh = kv[:, h * 256:(h + 1) * 256]
        vh = kv[:, 1024 + h * 256:1024 + (h + 1) * 256]
        s = mm_nt(qh, kh) * (256 ** -0.5)
        s = s - lax.stop_gradient(jnp.max(s, axis=-1, keepdims=True))
        p = jnp.exp(s)
        p = p / jnp.sum(p, axis=-1, keepdims=True)
        outs.append(mm(p, vh))
    return (jnp.concatenate(outs, axis=1),)


def _softplus(x):
    return jnp.maximum(x, 0.0) + jnp.log1p(jnp.exp(-jnp.abs(x)))


def _gdn_prep_fn(pt, a_log_p, dtb_p):
    rows, cols = _iota((128, 1024), 0), _iota((128, 1024), 1)
    e_b = (rows == (cols >> 7)).astype(F32)
    e_a = (rows == (cols >> 7) + 8).astype(F32)
    beta = jax.nn.sigmoid(pt)
    g = -(jnp.exp(a_log_p) * _softplus(pt + dtb_p))
    return mmh(g, e_a), mmh(beta, e_b)


def _gdn_post_fn(o_raw, z, o_norm):
    outs = _head_norm(o_raw, 8, 128)
    o = jnp.concatenate([oh * o_norm for oh in outs], axis=1)
    return (o * _silu(z),)


def _ffn_post(up, gate):
    return _silu(gate) * up


def _shift_down(cur, prev8, sh, row8):
    if sh == 0:
        return cur
    r = pltpu.roll(cur, sh, 0)
    p = pltpu.roll(prev8, sh, 0)
    top = jnp.where(row8 < sh, p, r[0:8])
    return jnp.concatenate([top, r[8:]], axis=0)


def _shift_up(cur, next8, sh, row8):
    if sh == 0:
        return cur
    ts = cur.shape[0]
    r = pltpu.roll(cur, ts - sh, 0)
    p = pltpu.roll(next8, 8 - sh, 0)
    bot = jnp.where(row8 >= 8 - sh, p, r[ts - 8:])
    return jnp.concatenate([r[:ts - 8], bot], axis=0)


def _conv_rows(cur, prev8, wrows, row8):
    k_w = len(wrows)
    out = None
    for j in range(k_w):
        t = _shift_down(cur, prev8, k_w - 1 - j, row8) * wrows[j]
        out = t if out is None else out + t
    return out


def _conv_specs(x, xoff, w, woff, ts, tc):
    r8 = ts // 8
    return [pl.BlockSpec((ts, tc), functools.partial(lambda i, j, o: (i, j + o), o=xoff)),
            pl.BlockSpec((8, tc), functools.partial(lambda i, j, o: (jnp.maximum(i * r8 - 1, 0), j + o), o=xoff)),
            pl.BlockSpec((w.shape[0], tc), functools.partial(lambda i, j, o: (0, j + o), o=woff))]


def _conv_post(srcs, post, ncol, tc, name, cots=None, ts=256):
    S = srcs[0][0].shape[0]
    ns = len(srcs)
    bwd = cots is not None

    def kern(*refs):
        first = pl.program_id(0) == 0
        row8 = _iota((8, tc), 0)
        cs = []
        for s in range(ns):
            cur_ref, prev_ref, w_ref = refs[3 * s:3 * s + 3]
            prev = jnp.where(first, 0.0, prev_ref[...])
            wrows = [w_ref[j:j + 1, :] for j in range(w_ref.shape[0])]
            cs.append(_conv_rows(cur_ref[...], prev, wrows, row8))
        if bwd:
            g = refs[3 * ns][...]
            _, vjp = jax.vjp(lambda *c: post(*c), *cs)
            for o_ref, d in zip(refs[3 * ns + 1:], vjp(g)):
                o_ref[...] = d
        else:
            refs[3 * ns][...] = post(*cs)

    in_specs = []
    ops = []
    for (x, xoff, w, woff) in srcs:
        in_specs += _conv_specs(x, xoff, w, woff, ts, tc)
        ops += [x, x, w]
    o_spec = pl.BlockSpec((ts, tc), lambda i, j: (i, j))
    o_shape = jax.ShapeDtypeStruct((S, ncol * tc), F32)
    if bwd:
        in_specs.append(o_spec)
        ops.append(cots)
        out_specs, out_shape = [o_spec] * ns, [o_shape] * ns
    else:
        out_specs, out_shape = o_spec, o_shape
    return pl.pallas_call(
        kern, name=name, grid=(S // ts, ncol), in_specs=in_specs, out_specs=out_specs, out_shape=out_shape,
        compiler_params=_cparams(("parallel", "parallel")))(*ops)


def _conv_bwd(dc, x, xoff, w, woff, ncol, tc, name, ts=256):
    S = x.shape[0]
    k_w = w.shape[0]
    r8 = ts // 8
    nblk8 = S // 8
    nrow = S // ts

    def kern(dc_ref, dn_ref, x_ref, xp_ref, w_ref, dx_ref, dw_ref):
        i = pl.program_id(1)
        row8 = _iota((8, tc), 0)
        dcur = dc_ref[...]
        dnext = jnp.where(i == nrow - 1, 0.0, dn_ref[...])
        xcur = x_ref[...]
        xprev = jnp.where(i == 0, 0.0, xp_ref[...])

        @pl.when(i == 0)
        def _():
            dw_ref[...] = jnp.zeros_like(dw_ref)

        dx = None
        for j in range(k_w):
            sh = k_w - 1 - j
            wj = w_ref[j:j + 1, :]
            t = _shift_up(dcur, dnext, sh, row8) * wj
            dx = t if dx is None else dx + t
            dw_ref[j:j + 1, :] += jnp.sum(dcur * _shift_down(xcur, xprev, sh, row8), axis=0, keepdims=True)
        dx_ref[...] = dx

    in_specs = [pl.BlockSpec((ts, tc), lambda j, i: (i, j)),
                pl.BlockSpec((8, tc), lambda j, i: (jnp.minimum((i + 1) * r8, nblk8 - 1), j)),
                pl.BlockSpec((ts, tc), functools.partial(lambda j, i, o: (i, j + o), o=xoff)),
                pl.BlockSpec((8, tc), functools.partial(lambda j, i, o: (jnp.maximum(i * r8 - 1, 0), j + o), o=xoff)),
                pl.BlockSpec((k_w, tc), functools.partial(lambda j, i, o: (0, j + o), o=woff))]
    out_specs = [pl.BlockSpec((ts, tc), lambda j, i: (i, j)), pl.BlockSpec((k_w, tc), lambda j, i: (0, j))]
    out_shape = [jax.ShapeDtypeStruct((S, ncol * tc), F32), jax.ShapeDtypeStruct((k_w, ncol * tc), F32)]
    return pl.pallas_call(
        kern, name=name, grid=(ncol, nrow), in_specs=in_specs, out_specs=out_specs, out_shape=out_shape,
        compiler_params=_cparams(("parallel", "arbitrary")))(dc, dc, x, x, w)


def _ret_tables(S):
    H, C, dh = 4, 128, 128
    lg = jnp.log1p(-jnp.exp2(-5.0 - jnp.arange(H, dtype=F32)))
    idx = jnp.arange(C, dtype=F32)
    diff = idx[:, None] - idx[None, :]
    causal = diff >= 0
    intra = jnp.where(causal, jnp.exp(lg[:, None, None] * jnp.where(causal, diff, 0.0)), 0.0)
    kdec = jnp.broadcast_to(jnp.exp(lg[:, None] * (C - 1 - idx))[:, :, None], (H, C, dh))
    qdec = jnp.broadcast_to(jnp.exp(lg[:, None] * (idx + 1))[:, :, None], (H, C, dh))
    cdec = jnp.broadcast_to(jnp.exp(lg * C)[:, None, None], (H, dh, dh))
    half = dh // 2
    inv = jnp.exp(-math.log(10000.0) * jnp.arange(half, dtype=F32) / half)
    ang = jnp.arange(S).astype(F32)[:, None] * inv[None, :]
    cos, sin = jnp.cos(ang), jnp.sin(ang)
    cosf = jnp.concatenate([cos, cos], axis=1)
    sinf = jnp.concatenate([-sin, sin], axis=1)
    return cosf, sinf, intra, kdec, qdec, cdec


def _ret_chunk(q, k, v, cosf, sinf, intra, kdec, qdec, cdec, state):
    qr = q * cosf + _swap_halves(q) * sinf
    kr = (k * cosf + _swap_halves(k) * sinf) * (128 ** -0.5)
    scores = mm_nt(qr, kr) * intra
    inner = mm(scores, v)
    kv = mm_tn(kr * kdec, v)
    cross = mm(qr * qdec, state)
    return inner + cross, state * cdec + kv


def _ret_specs(N, rev):
    def nn(n):
        return N - 1 - n if rev else n
    chunk = [pl.BlockSpec((128, 128), functools.partial(lambda h, n, o: (nn(n), h + o), o=o)) for o in (0, 4, 8)]
    pos = [pl.BlockSpec((128, 128), lambda h, n: (nn(n), 0))] * 2
    tabs = [pl.BlockSpec((None, 128, 128), lambda h, n: (h, 0, 0))] * 4
    st = pl.BlockSpec((None, None, 128, 128), lambda h, n: (h, nn(n), 0, 0))
    o = pl.BlockSpec((128, 128), lambda h, n: (nn(n), h))
    return chunk, pos, tabs, st, o


def _ret_fwd(proj, tabs):
    S = proj.shape[0]
    N = S // 128
    chunk, pos, tsp, st, o = _ret_specs(N, False)

    def kern(q_ref, k_ref, v_ref, c_ref, s_ref, i_ref, kd_ref, qd_ref, cd_ref, o_ref, sp_ref, st_ref):
        @pl.when(pl.program_id(1) == 0)
        def _():
            st_ref[...] = jnp.zeros_like(st_ref)

        state = st_ref[...]
        sp_ref[...] = state
        out, new = _ret_chunk(q_ref[...], k_ref[...], v_ref[...], c_ref[...], s_ref[...], i_ref[...], kd_ref[...],
                              qd_ref[...], cd_ref[...], state)
        o_ref[...] = out
        st_ref[...] = new

    return pl.pallas_call(
        kern, name="ret_fwd", grid=(4, N), in_specs=chunk + pos + tsp, out_specs=[o, st],
        out_shape=[jax.ShapeDtypeStruct((S, 512), F32), jax.ShapeDtypeStruct((4, N, 128, 128), F32)],
        scratch_shapes=[pltpu.VMEM((128, 128), F32)],
        compiler_params=_cparams(("parallel", "arbitrary")))(proj, proj, proj, *tabs)


def _ret_bwd(proj, tabs, states, do):
    S = proj.shape[0]
    N = S // 128
    chunk, pos, tsp, st, o = _ret_specs(N, True)

    def kern(q_ref, k_ref, v_ref, c_ref, s_ref, i_ref, kd_ref, qd_ref, cd_ref, sp_ref, do_ref,
             dq_ref, dk_ref, dv_ref, ds_ref):
        @pl.when(pl.program_id(1) == 0)
        def _():
            ds_ref[...] = jnp.zeros_like(ds_ref)

        consts = (c_ref[...], s_ref[...], i_ref[...], kd_ref[...], qd_ref[...], cd_ref[...])
        _, vjp = jax.vjp(lambda q, k, v, s: _ret_chunk(q, k, v, *consts, s), q_ref[...], k_ref[...], v_ref[...],
                         sp_ref[...])
        dq, dk, dv, ds = vjp((do_ref[...], ds_ref[...]))
        dq_ref[...] = dq
        dk_ref[...] = dk
        dv_ref[...] = dv
        ds_ref[...] = ds

    return pl.pallas_call(
        kern, name="ret_bwd", grid=(4, N), in_specs=chunk + pos + tsp + [st, o], out_specs=[o, o, o],
        out_shape=[jax.ShapeDtypeStruct((S, 512), F32)] * 3, scratch_shapes=[pltpu.VMEM((128, 128), F32)],
        compiler_params=_cparams(("parallel", "arbitrary")))(proj, proj, proj, *tabs, states, do)


def _gdn_chunk(q, k, v, g_b, beta_b, state):
    c = 64
    q = q * lax.rsqrt(jnp.sum(q * q, axis=-1, keepdims=True) + EPS) * (128 ** -0.5)
    k = k * lax.rsqrt(jnp.sum(k * k, axis=-1, keepdims=True) + EPS)
    ri, ci = _iota((c, c), 0), _iota((c, c), 1)
    incl = ri >= ci
    strict = ri > ci
    eye = (ri == ci).astype(F32)
    gc_b = mmh(incl.astype(F32), g_b)
    gl_b = mmh(jnp.ones((c, c), F32), g_b)
    gl_s = mmh(jnp.ones((128, c), F32), g_b)
    kb = k * beta_b
    vb = v * beta_b
    gcc = gc_b[:, :c]
    gdiff = gcc - gcc.T
    decay = jnp.where(incl, jnp.exp(jnp.where(incl, gdiff, 0.0)), 0.0)
    a_mat = jnp.where(strict, mm_nt(kb, k) * decay, 0.0)
    p = -a_mat
    t_mat = eye + p
    for _ in range(5):
        p = mmh(p, p)
        t_mat = mmh(t_mat, eye + p)
    egc = jnp.exp(gc_b)
    w = mm(t_mat, kb * egc)
    u = mm(t_mat, vb)
    qk = jnp.where(incl, mm_nt(q, k) * decay, 0.0)
    q_dec = q * egc
    k_dec = k * jnp.exp(gl_b - gc_b)
    v_new = u - mm(w, state)
    o = mm(q_dec, state) + mm(qk, v_new)
    new_state = state * jnp.exp(gl_s) + mm_tn(k_dec, v_new)
    return o, new_state


def _gdn_specs(N, rev):
    def nn(n):
        return N - 1 - n if rev else n
    qkv = [pl.BlockSpec((64, 128), functools.partial(lambda h, n, o: (nn(n), h + o), o=o)) for o in (0, 8, 16)]
    hd = pl.BlockSpec((64, 128), lambda h, n: (nn(n), h))
    st = pl.BlockSpec((None, None, 128, 128), lambda h, n: (h, nn(n), 0, 0))
    return qkv, hd, st


def _gdn_fwd(qkv, g_e, beta_e):
    S = qkv.shape[0]
    N = S // 64
    qs, hd, st = _gdn_specs(N, False)

    def kern(q_ref, k_ref, v_ref, g_ref, b_ref, o_ref, sp_ref, st_ref):
        @pl.when(pl.program_id(1) == 0)
        def _():
            st_ref[...] = jnp.zeros_like(st_ref)

        state = st_ref[...]
        sp_ref[...] = state
        out, new = _gdn_chunk(q_ref[...], k_ref[...], v_ref[...], g_ref[...], b_ref[...], state)
        o_ref[...] = out
        st_ref[...] = new

    return pl.pallas_call(
        kern, name="gdn_fwd", grid=(8, N), in_specs=qs + [hd, hd], out_specs=[hd, st],
        out_shape=[jax.ShapeDtypeStruct((S, 1024), F32), jax.ShapeDtypeStruct((8, N, 128, 128), F32)],
        scratch_shapes=[pltpu.VMEM((128, 128), F32)],
        compiler_params=_cparams(("parallel", "arbitrary")))(qkv, qkv, qkv, g_e, beta_e)


def _gdn_bwd(qkv, g_e, beta_e, states, do):
    S = qkv.shape[0]
    N = S // 64
    qs, hd, st = _gdn_specs(N, True)

    def kern(q_ref, k_ref, v_ref, g_ref, b_ref, sp_ref, do_ref, dq_ref, dk_ref, dv_ref, dg_ref, db_ref, ds_ref):
        @pl.when(pl.program_id(1) == 0)
        def _():
            ds_ref[...] = jnp.zeros_like(ds_ref)

        _, vjp = jax.vjp(_gdn_chunk, q_ref[...], k_ref[...], v_ref[...], g_ref[...], b_ref[...], sp_ref[...])
        dq, dk, dv, dg, db, ds = vjp((do_ref[...], ds_ref[...]))
        dq_ref[...] = dq
        dk_ref[...] = dk
        dv_ref[...] = dv
        dg_ref[...] = dg
        db_ref[...] = db
        ds_ref[...] = ds

    return pl.pallas_call(
        kern, name="gdn_bwd", grid=(8, N), in_specs=qs + [hd, hd, st, hd], out_specs=[hd] * 5,
        out_shape=[jax.ShapeDtypeStruct((S, 1024), F32)] * 5, scratch_shapes=[pltpu.VMEM((128, 128), F32)],
        compiler_params=_cparams(("parallel", "arbitrary")))(qkv, qkv, qkv, g_e, beta_e, states, do)


def _s5_prep_fn(lr, li, ldt, br, bi, cr, ci):
    dt = jnp.exp(ldt)
    mag = jnp.exp(lr * dt)
    a_re = mag * jnp.cos(li * dt)
    a_im = mag * jnp.sin(li * dt)
    den = lr * lr + li * li
    z_re = ((a_re - 1.0) * lr + a_im * li) / den
    z_im = (a_im * lr - (a_re - 1.0) * li) / den
    e1 = ((_iota((512, 32), 0) >> 4) == _iota((512, 32), 1)).astype(F32)
    zr_e = mmh(e1, z_re)
    zi_e = mmh(e1, z_im)
    bb_re = zr_e * br - zi_e * bi
    bb_im = zr_e * bi + zi_e * br
    t1 = ((_iota((64, 2048), 1) & 63) == _iota((64, 2048), 0)).astype(F32)
    m1 = (_iota((512, 2048), 0) >> 4) == (_iota((512, 2048), 1) >> 6)
    bd_re = jnp.where(m1, mmh(bb_re, t1), 0.0)
    bd_im = jnp.where(m1, mmh(bb_im, t1), 0.0)
    t2 = ((_iota((16, 512), 1) & 15) == _iota((16, 512), 0)).astype(F32)
    m2 = (_iota((2048, 512), 0) >> 6) == (_iota((2048, 512), 1) >> 4)
    cd_re = jnp.where(m2, mmh(cr, t2), 0.0)
    cd_im = jnp.where(m2, mmh(ci, t2), 0.0)
    return a_re, a_im, bd_re, bd_im, cd_re, cd_im


_PREP_OUT = [(32, 64), (32, 64), (512, 2048), (512, 2048), (2048, 512), (2048, 512)]


def _s5_prep(params, cots=None):
    bwd = cots is not None

    def kern(*refs):
        vals = [r[...] for r in refs[:7]]
        if bwd:
            gs = tuple(r[...] for r in refs[7:13])
            _, vjp = jax.vjp(_s5_prep_fn, *vals)
            for o_ref, d in zip(refs[13:], vjp(gs)):
                o_ref[...] = d
        else:
            for o_ref, o in zip(refs[7:], _s5_prep_fn(*vals)):
                o_ref[...] = o

    if bwd:
        out_shape = [jax.ShapeDtypeStruct(p.shape, F32) for p in params]
        ops = list(params) + list(cots)
    else:
        out_shape = [jax.ShapeDtypeStruct(s, F32) for s in _PREP_OUT]
        ops = list(params)
    return pl.pallas_call(kern, name="s5_prep_bwd" if bwd else "s5_prep", out_shape=out_shape,
                          compiler_params=_cparams())(*ops)


def _cmul(ar, ai, br, bi):
    return ar * br - ai * bi, ar * bi + ai * br


def _power_table(ar, ai, row8, descending):
    pr, pi = ar, ai
    tr = jnp.zeros(row8.shape, F32)
    ti = jnp.zeros(row8.shape, F32)
    for n in range(8):
        r = 7 - n if descending else n
        tr = jnp.where(row8 == r, pr, tr)
        ti = jnp.where(row8 == r, pi, ti)
        if n < 7:
            pr, pi = _cmul(pr, pi, ar, ai)
    return tr, ti


def _tile_scan(xr, xi, pows, row8, up):
    for d, (pr, pi) in zip((1, 2, 4), pows):
        if up:
            sr = jnp.where(row8 < 8 - d, pltpu.roll(xr, 8 - d, 0), 0.0)
            si = jnp.where(row8 < 8 - d, pltpu.roll(xi, 8 - d, 0), 0.0)
        else:
            sr = jnp.where(row8 >= d, pltpu.roll(xr, d, 0), 0.0)
            si = jnp.where(row8 >= d, pltpu.roll(xi, d, 0), 0.0)
        mr, mi = _cmul(pr, pi, sr, si)
        xr, xi = xr + mr, xi + mi
    return xr, xi


def _pick_row(x, row8, r):
    return jnp.sum(jnp.where(row8 == r, x, 0.0), axis=0, keepdims=True)


SCAN_LB = 512
SCAN_TS = 512


def _scan_fwd(bu_re, bu_im, a_re, a_im):
    S, L = bu_re.shape
    ts, lb = min(SCAN_TS, S), SCAN_LB
    nt = ts // 8

    def kern(br_ref, bi_ref, ar_ref, ai_ref, or_ref, oi_ref, cr_ref, ci_ref):
        @pl.when(pl.program_id(1) == 0)
        def _():
            cr_ref[...] = jnp.zeros_like(cr_ref)
            ci_ref[...] = jnp.zeros_like(ci_ref)

        row8 = _iota((8, lb), 0)
        ar, ai = ar_ref[...], ai_ref[...]
        a2 = _cmul(ar, ai, ar, ai)
        a4 = _cmul(*a2, *a2)
        pows = ((ar, ai), a2, a4)
        tr, ti = _power_table(ar, ai, row8, False)

        def body(i, carry):
            cr, ci = carry
            off = pl.multiple_of(i * 8, 8)
            xr, xi = _tile_scan(br_ref[pl.ds(off, 8), :], bi_ref[pl.ds(off, 8), :], pows, row8, False)
            mr, mi = _cmul(tr, ti, cr, ci)
            xr, xi = xr + mr, xi + mi
            or_ref[pl.ds(off, 8), :] = xr
            oi_ref[pl.ds(off, 8), :] = xi
            return _pick_row(xr, row8, 7), _pick_row(xi, row8, 7)

        cr, ci = lax.fori_loop(0, nt, body, (cr_ref[...], ci_ref[...]))
        cr_ref[...] = cr
        ci_ref[...] = ci

    blk = pl.BlockSpec((ts, lb), lambda j, i: (i, j))
    par = pl.BlockSpec((1, lb), lambda j, i: (0, j))
    return pl.pallas_call(
        kern, name="s5_scan_fwd", grid=(L // lb, S // ts), in_specs=[blk, blk, par, par], out_specs=[blk, blk],
        out_shape=[jax.ShapeDtypeStruct((S, L), F32)] * 2,
        scratch_shapes=[pltpu.VMEM((1, lb), F32), pltpu.VMEM((1, lb), F32)],
        compiler_params=_cparams(("parallel", "arbitrary")))(bu_re, bu_im, a_re, a_im)


def _scan_bwd(dst_re, dst_im, st_re, st_im, a_re, a_im):
    S, L = dst_re.shape
    ts, lb = min(SCAN_TS, S), SCAN_LB
    nt = ts // 8
    nb = S // ts
    r8 = ts // 8

    def kern(dr_ref, di_ref, sr_ref, si_ref, pr_ref, pi_ref, ar_ref, ai_ref, gr_ref, gi_ref, dar_ref, dai_ref,
             cr_ref, ci_ref):
        step = pl.program_id(1)
        blk = nb - 1 - step

        @pl.when(step == 0)
        def _():
            cr_ref[...] = jnp.zeros_like(cr_ref)
            ci_ref[...] = jnp.zeros_like(ci_ref)
            dar_ref[...] = jnp.zeros_like(dar_ref)
            dai_ref[...] = jnp.zeros_like(dai_ref)

        row8 = _iota((8, lb), 0)
        ar, ai = ar_ref[...], ai_ref[...]
        nai = -ai
        a2 = _cmul(ar, nai, ar, nai)
        a4 = _cmul(*a2, *a2)
        pows = ((ar, nai), a2, a4)
        tr, ti = _power_table(ar, nai, row8, True)
        halo_r = jnp.where(blk == 0, 0.0, pr_ref[...])
        halo_i = jnp.where(blk == 0, 0.0, pi_ref[...])

        def body(n, carry):
            cr, ci, acc_r, acc_i = carry
            i = nt - 1 - n
            off = pl.multiple_of(i * 8, 8)
            gr, gi = _tile_scan(dr_ref[pl.ds(off, 8), :], di_ref[pl.ds(off, 8), :], pows, row8, True)
            mr, mi = _cmul(tr, ti, cr, ci)
            gr, gi = gr + mr, gi + mi
            gr_ref[pl.ds(off, 8), :] = gr
            gi_ref[pl.ds(off, 8), :] = gi
            poff = pl.multiple_of(jnp.maximum(i - 1, 0) * 8, 8)
            before_r = jnp.where(i == 0, halo_r, sr_ref[pl.ds(poff, 8), :])
            before_i = jnp.where(i == 0, halo_i, si_ref[pl.ds(poff, 8), :])
            last_r = _pick_row(before_r, row8, 7)
            last_i = _pick_row(before_i, row8, 7)
            spr = jnp.where(row8 >= 1, pltpu.roll(sr_ref[pl.ds(off, 8), :], 1, 0), last_r)
            spi = jnp.where(row8 >= 1, pltpu.roll(si_ref[pl.ds(off, 8), :], 1, 0), last_i)
            acc_r = acc_r + gr * spr + gi * spi
            acc_i = acc_i + gi * spr - gr * spi
            return _pick_row(gr, row8, 0), _pick_row(gi, row8, 0), acc_r, acc_i

        zero = jnp.zeros((8, lb), F32)
        cr, ci, acc_r, acc_i = lax.fori_loop(0, nt, body, (cr_ref[...], ci_ref[...], zero, zero))
        cr_ref[...] = cr
        ci_ref[...] = ci
        dar_ref[...] += jnp.sum(acc_r, axis=0, keepdims=True)
        dai_ref[...] += jnp.sum(acc_i, axis=0, keepdims=True)

    blk = pl.BlockSpec((ts, lb), lambda j, i: (nb - 1 - i, j))
    halo = pl.BlockSpec((8, lb), lambda j, i: (jnp.maximum((nb - 1 - i) * r8 - 1, 0), j))
    par = pl.BlockSpec((1, lb), lambda j, i: (0, j))
    return pl.pallas_call(
        kern, name="s5_scan_bwd", grid=(L // lb, nb), in_specs=[blk, blk, blk, blk, halo, halo, par, par],
        out_specs=[blk, blk, par, par],
        out_shape=[jax.ShapeDtypeStruct((S, L), F32)] * 2 + [jax.ShapeDtypeStruct((1, L), F32)] * 2,
        scratch_shapes=[pltpu.VMEM((1, lb), F32), pltpu.VMEM((1, lb), F32)],
        compiler_params=_cparams(("parallel", "arbitrary")))(dst_re, dst_im, st_re, st_im, st_re, st_im, a_re, a_im)


def _loss_grad(x, target, gain, ts=256):
    S, D = x.shape

    def kern(x_ref, t_ref, g_ref, loss_ref, dx_ref, dg_ref):
        i = pl.program_id(0)
        tgt = t_ref[...]

        def f(xv, gv):
            err = _rms(xv, gv) - tgt
            return 0.5 * jnp.mean(err * err, axis=-1, keepdims=True)

        rowloss, vjp = jax.vjp(f, x_ref[...], g_ref[...])
        dx, dg = vjp(jnp.ones_like(rowloss))
        dx_ref[...] = dx

        @pl.when(i == 0)
        def _():
            loss_ref[...] = jnp.zeros_like(loss_ref)
            dg_ref[...] = jnp.zeros_like(dg_ref)

        loss_ref[...] += jnp.broadcast_to(jnp.sum(rowloss, axis=0, keepdims=True), loss_ref.shape)
        dg_ref[...] += dg

    row = pl.BlockSpec((ts, D), lambda i: (i, 0))
    return pl.pallas_call(
        kern, name="loss_grad", grid=(S // ts,), in_specs=[row, row, pl.BlockSpec((1, D), lambda i: (0, 0))],
        out_specs=[pl.BlockSpec((8, 128), lambda i: (0, 0)), row, pl.BlockSpec((1, D), lambda i: (0, 0))],
        out_shape=[jax.ShapeDtypeStruct((8, 128), F32), jax.ShapeDtypeStruct((S, D), F32),
                   jax.ShapeDtypeStruct((1, D), F32)],
        compiler_params=_cparams(("arbitrary",)))(x, target, gain)


def _rms_fwd(x, g, name):
    return _rowwise(_rms_fn, [_blk(x)], [g], [x.shape[1]], name)[0]


def _rms_bwd(x, g, dy, name, add=None):
    return _rowwise_bwd(_rms_fn, [_blk(x)], [g], [dy], name, adds=None if add is None else {0: add})


def _common_fwd(x, mem, P, L):
    hx = _rms_fwd(x, P['xa_norm'], L + "xa_norm")
    q = _matmul(hx, P['xa_wq'], name=L + "xa_q")
    memn = _rms_fwd(mem, P['mem_norm'], L + "mem_norm")
    kv = _matmul(memn, P['xa_wkv'], name=L + "xa_kv")
    att = _rowwise(_xattn_fn, [_blk(q)], [kv], [1024], L + "xattn")[0]
    x2 = _matmul(att, P['xa_wo'], res=x, name=L + "xa_o")
    hf = _rms_fwd(x2, P['ffn_norm'], L + "ffn_norm")
    hu = _matmul(hf, P['ffn_w_up'], name=L + "ffn_up")
    cw = P['ffn_conv']
    act = _conv_post([(hu, 0, cw, 0), (hu, 11, cw, 11)], _ffn_post, 11, 256, L + "ffn_conv")
    x3 = _matmul(act, P['ffn_w_down'], res=x2, name=L + "ffn_down")
    return x3, (x, mem, hx, q, memn, kv, att, x2, hf, hu, act)


def _common_bwd(saved, dx3, P, L):
    x, mem, hx, q, memn, kv, att, x2, hf, hu, act = saved
    G = {}
    dact = _matmul(dx3, P['ffn_w_down'], "nt", name=L + "ffn_down_dx")
    G['ffn_w_down'] = _matmul(act, dx3, "tn", name=L + "ffn_down_dw")
    cw = P['ffn_conv']
    dcu, dcg = _conv_post([(hu, 0, cw, 0), (hu, 11, cw, 11)], _ffn_post, 11, 256, L + "ffn_conv_dpost", cots=dact)
    dhu, G['ffn_conv'] = _conv_bwd(jnp.concatenate([dcu, dcg], axis=1), hu, 0, cw, 0, 22, 256, L + "ffn_conv_bwd")
    dhf = _matmul(dhu, P['ffn_w_up'], "nt", name=L + "ffn_up_dx")
    G['ffn_w_up'] = _matmul(hf, dhu, "tn", name=L + "ffn_up_dw")
    dx2, G['ffn_norm'] = _rms_bwd(x2, P['ffn_norm'], dhf, L + "ffn_norm_bwd", add=dx3)
    datt = _matmul(dx2, P['xa_wo'], "nt", name=L + "xa_o_dx")
    G['xa_wo'] = _matmul(att, dx2, "tn", name=L + "xa_o_dw")
    dq, dkv = _rowwise_bwd(_xattn_fn, [_blk(q)], [kv], [datt], L + "xattn_bwd")
    dhx = _matmul(dq, P['xa_wq'], "nt", name=L + "xa_q_dx")
    G['xa_wq'] = _matmul(hx, dq, "tn", name=L + "xa_q_dw")
    dmemn = _matmul(dkv, P['xa_wkv'], "nt", name=L + "xa_kv_dx")
    G['xa_wkv'] = _matmul(memn, dkv, "tn", name=L + "xa_kv_dw")
    _, G['mem_norm'] = _rms_bwd(mem, P['mem_norm'], dmemn, L + "mem_norm_bwd")
    dx, G['xa_norm'] = _rms_bwd(x, P['xa_norm'], dhx, L + "xa_norm_bwd", add=dx2)
    return dx, G


def _even_fwd(x, P):
    S = x.shape[0]
    h0 = _rms_fwd(x, P['mix_norm'], "l0_mix_norm")
    proj = _matmul(h0, P['w_in'], name="l0_in")
    tabs = _ret_tables(S)
    o_raw, rstates = _ret_fwd(proj, tabs)
    o = _rowwise(_ret_post_fn, [_blk(o_raw), _blk(proj, 512, 3)], [P['ret_norm']], [512], "l0_ret_post")[0]
    prep_in = (P['s5_lambda_re'], P['s5_lambda_im'], P['s5_log_dt'], P['s5_b_re'], P['s5_b_im'], P['s5_c_re'],
               P['s5_c_im'])
    a_re, a_im, bd_re, bd_im, cd_re, cd_im = _s5_prep(prep_in)
    a_re_f, a_im_f = a_re.reshape(1, 2048), a_im.reshape(1, 2048)
    u = proj[:, 2048:2560]
    bu_re = _matmul(u, bd_re, name="l0_s5_bu_re")
    bu_im = _matmul(u, bd_im, name="l0_s5_bu_im")
    st_re, st_im = _scan_fwd(bu_re, bu_im, a_re_f, a_im_f)
    y1 = _matmul(st_re, cd_re, name="l0_s5_y_re")
    y2 = _matmul(st_im, cd_im, name="l0_s5_y_im")
    yg = _rowwise(_s5_post_fn, [_blk(y1), _blk(y2), _blk(u)], [P['s5_d'], P['s5_w_glu'], P['s5_b_glu']], [512],
                  "l0_s5_post")[0]
    merged = jnp.concatenate([o, yg], axis=1)
    x1 = _matmul(merged, P['w_out'], res=x, name="l0_out")
    saved = (x, h0, proj, tabs, o_raw, rstates, prep_in, a_re_f, a_im_f, bd_re, bd_im, cd_re, cd_im, u, st_re, st_im,
             y1, y2, merged)
    return x1, saved


def _even_bwd(saved, dx1, P):
    (x, h0, proj, tabs, o_raw, rstates, prep_in, a_re_f, a_im_f, bd_re, bd_im, cd_re, cd_im, u, st_re, st_im, y1, y2,
     merged) = saved
    G = {}
    dmerged = _matmul(dx1, P['w_out'], "nt", name="l0_out_dx")
    G['w_out'] = _matmul(merged, dx1, "tn", name="l0_out_dw")
    do_raw, dgate, G['ret_norm'] = _rowwise_bwd(
        _ret_post_fn, [_blk(o_raw), _blk(proj, 512, 3)], [P['ret_norm']], [dmerged[:, :512]], "l0_ret_post_bwd")
    dq, dk, dv = _ret_bwd(proj, tabs, rstates, do_raw)
    dy1, dy2, du_a, G['s5_d'], G['s5_w_glu'], G['s5_b_glu'] = _rowwise_bwd(
        _s5_post_fn, [_blk(y1), _blk(y2), _blk(u)], [P['s5_d'], P['s5_w_glu'], P['s5_b_glu']], [dmerged[:, 512:]],
        "l0_s5_post_bwd")
    dst_re = _matmul(dy1, cd_re, "nt", name="l0_s5_y_re_dx")
    dcd_re = _matmul(st_re, dy1, "tn", name="l0_s5_y_re_dw")
    dst_im = _matmul(dy2, cd_im, "nt", name="l0_s5_y_im_dx")
    dcd_im = _matmul(st_im, dy2, "tn", name="l0_s5_y_im_dw")
    dbu_re, dbu_im, da_re, da_im = _scan_bwd(dst_re, dst_im, st_re, st_im, a_re_f, a_im_f)
    du = _matmul(dbu_re, bd_re, "nt", res=du_a, name="l0_s5_bu_re_dx")
    du = _matmul(dbu_im, bd_im, "nt", res=du, name="l0_s5_bu_im_dx")
    dbd_re = _matmul(u, dbu_re, "tn", name="l0_s5_bu_re_dw")
    dbd_im = _matmul(u, dbu_im, "tn", name="l0_s5_bu_im_dw")
    dprep = _s5_prep(prep_in, cots=(da_re.reshape(32, 64), da_im.reshape(32, 64), dbd_re, dbd_im, dcd_re, dcd_im))
    for n, d in zip(('s5_lambda_re', 's5_lambda_im', 's5_log_dt', 's5_b_re', 's5_b_im', 's5_c_re', 's5_c_im'), dprep):
        G[n] = d
    dproj = jnp.concatenate([dq, dk, dv, dgate, du], axis=1)
    dh0 = _matmul(dproj, P['w_in'], "nt", name="l0_in_dx")
    G['w_in'] = _matmul(h0, dproj, "tn", name="l0_in_dw")
    dx, G['mix_norm'] = _rms_bwd(x, P['mix_norm'], dh0, "l0_mix_norm_bwd", add=dx1)
    return dx, G


def _odd_fwd(x, P):
    h1 = _rms_fwd(x, P['mix_norm'], "l1_mix_norm")
    pm = _matmul(h1, P['w_main'], name="l1_in_main")
    pt = _matmul(h1, P['w_tail'], name="l1_in_tail")
    qkv = _conv_post([(pm, 0, P['conv'], 0)], _silu, 6, 512, "l1_conv")
    g_e, beta_e = _rowwise(_gdn_prep_fn, [_blk(pt)], [P['a_log_p'], P['dtb_p']], [1024, 1024], "l1_gdn_prep")
    o_raw, gstates = _gdn_fwd(qkv, g_e, beta_e)
    og = _rowwise(_gdn_post_fn, [_blk(o_raw), _blk(pm, 1024, 3)], [P['o_norm']], [1024], "l1_gdn_post")[0]
    x1 = _matmul(og, P['w_out'], res=x, name="l1_out")
    return x1, (x, h1, pm, pt, qkv, g_e, beta_e, o_raw, gstates, og)


def _odd_bwd(saved, dx1, P):
    x, h1, pm, pt, qkv, g_e, beta_e, o_raw, gstates, og = saved
    G = {}
    dog = _matmul(dx1, P['w_out'], "nt", name="l1_out_dx")
    G['w_out'] = _matmul(og, dx1, "tn", name="l1_out_dw")
    do_raw, dz, G['o_norm'] = _rowwise_bwd(_gdn_post_fn, [_blk(o_raw), _blk(pm, 1024, 3)], [P['o_norm']], [dog],
                                           "l1_gdn_post_bwd")
    dq, dk, dv, dg_e, dbeta_e = _gdn_bwd(qkv, g_e, beta_e, gstates, do_raw)
    dpt, G['a_log_p'], G['dtb_p'] = _rowwise_bwd(_gdn_prep_fn, [_blk(pt)], [P['a_log_p'], P['dtb_p']],
                                                 [dg_e, dbeta_e], "l1_gdn_prep_bwd")
    dqkv = jnp.concatenate([dq, dk, dv], axis=1)
    (dc,) = _conv_post([(pm, 0, P['conv'], 0)], _silu, 6, 512, "l1_conv_dpost", cots=dqkv)
    dqkv_pre, G['conv'] = _conv_bwd(dc, pm, 0, P['conv'], 0, 6, 512, "l1_conv_bwd")
    dpm = jnp.concatenate([dqkv_pre, dz], axis=1)
    dh1 = _matmul(dpm, P['w_main'], "nt", name="l1_in_main_dx")
    dh1 = _matmul(dpt, P['w_tail'], "nt", res=dh1, name="l1_in_tail_dx")
    G['w_main'] = _matmul(h1, dpm, "tn", name="l1_in_main_dw")
    G['w_tail'] = _matmul(h1, dpt, "tn", name="l1_in_tail_dw")
    dx, G['mix_norm'] = _rms_bwd(x, P['mix_norm'], dh1, "l1_mix_norm_bwd", add=dx1)
    return dx, G


def _row(v):
    return v.reshape(1, -1)


def _local_step(x, mem, target, W):
    P0 = {
        'mix_norm': _row(W['l0_mix_norm']), 'w_in': W['l0_w_in'], 'ret_norm': _row(W['l0_ret_norm']),
        's5_lambda_re': W['l0_s5_lambda_re'], 's5_lambda_im': W['l0_s5_lambda_im'],
        's5_log_dt': W['l0_s5_log_dt'].reshape(32, 1),
        's5_b_re': W['l0_s5_b_re'].reshape(512, 64), 's5_b_im': W['l0_s5_b_im'].reshape(512, 64),
        's5_c_re': W['l0_s5_c_re'].reshape(2048, 16), 's5_c_im': W['l0_s5_c_im'].reshape(2048, 16),
        's5_d': _row(W['l0_s5_d']), 's5_w_glu': W['l0_s5_w_glu'].astype(F32), 's5_b_glu': _row(W['l0_s5_b_glu']),
        'w_out': W['l0_w_out'],
    }
    w_in1 = W['l1_w_in']
    pad16 = jnp.zeros((8,), F32)
    P1 = {
        'mix_norm': _row(W['l1_mix_norm']), 'w_main': w_in1[:, :4096],
        'w_tail': jnp.pad(w_in1[:, 4096:], ((0, 0), (0, 112))), 'conv': W['l1_conv'],
        'a_log_p': _row(jnp.concatenate([pad16, W['l1_a_log'], jnp.zeros((112,), F32)])),
        'dtb_p': _row(jnp.concatenate([pad16, W['l1_dt_bias'], jnp.zeros((112,), F32)])),
        'o_norm': _row(W['l1_o_norm']), 'w_out': W['l1_w_out'],
    }
    C = []
    for L in ('l0_', 'l1_'):
        C.append({'xa_norm': _row(W[L + 'xa_norm']), 'mem_norm': _row(W[L + 'mem_norm']), 'xa_wq': W[L + 'xa_wq'],
                  'xa_wkv': W[L + 'xa_wkv'], 'xa_wo': W[L + 'xa_wo'], 'ffn_norm': _row(W[L + 'ffn_norm']),
                  'ffn_w_up': W[L + 'ffn_w_up'], 'ffn_conv': W[L + 'ffn_conv'], 'ffn_w_down': W[L + 'ffn_w_down']})

    x1, s_even = _even_fwd(x, P0)
    x3, s_c0 = _common_fwd(x1, mem, C[0], "l0_")
    x4, s_odd = _odd_fwd(x3, P1)
    x6, s_c1 = _common_fwd(x4, mem, C[1], "l1_")
    loss_tile, dx6, d_final = _loss_grad(x6, target, _row(W['final_norm']))

    G = {'final_norm': d_final.reshape(-1)}
    dx4, g = _common_bwd(s_c1, dx6, C[1], "l1_")
    for k, v in g.items():
        G['l1_' + k] = v
    dx3, g = _odd_bwd(s_odd, dx4, P1)
    G['l1_mix_norm'] = g['mix_norm']
    G['l1_w_in'] = jnp.concatenate([g['w_main'], g['w_tail'][:, :16]], axis=1)
    G['l1_conv'] = g['conv']
    G['l1_a_log'] = g['a_log_p'][0, 8:16]
    G['l1_dt_bias'] = g['dtb_p'][0, 8:16]
    G['l1_o_norm'] = g['o_norm']
    G['l1_w_out'] = g['w_out']
    dx1, g = _common_bwd(s_c0, dx3, C[0], "l0_")
    for k, v in g.items():
        G['l0_' + k] = v
    dx0, g = _even_bwd(s_even, dx1, P0)
    for k, v in g.items():
        G['l0_' + k] = v
    return loss_tile, dx0, G


ANY = pl.BlockSpec(memory_space=pl.ANY)


def _pad_rows(flat, mult):
    n = flat.shape[0]
    rows = -(-n // LANES)
    rows = -(-rows // mult) * mult
    return jnp.pad(flat, (0, rows * LANES - n)).reshape(rows, LANES)


def _pack_weight_shards(shards):
    parts = []
    for n in SHARDED_NAMES:
        w = shards[n]
        hi = w.astype(BF16)
        parts.append(hi.reshape(-1))
        if SHARDED[n][2]:
            parts.append((w - hi.astype(F32)).astype(BF16).reshape(-1))
    return _pad_rows(jnp.concatenate(parts), 16)


def _shard_shape(n):
    shape, axis, _ = SHARDED[n]
    return tuple(s // 4 if a == axis else s for a, s in enumerate(shape))


def _unpack_weights(gathered):
    flat = gathered.reshape(4, -1)
    out = {}
    off = 0
    for n in SHARDED_NAMES:
        shp = _shard_shape(n)
        size = shp[0] * shp[1]
        axis = SHARDED[n][1]
        pieces = flat[:, off:off + size].reshape((4,) + shp)
        off += size
        if SHARDED[n][2]:
            lo = flat[:, off:off + size].reshape((4,) + shp)
            off += size
            pieces = pieces.astype(F32) + lo.astype(F32)
        out[n] = jnp.concatenate([pieces[j] for j in range(4)], axis=axis)
    return out


def _pack_grads(G):
    slabs = []
    for j in range(4):
        parts = []
        for n in SHARDED_NAMES:
            shp = _shard_shape(n)
            axis = SHARDED[n][1]
            g = G[n]
            piece = lax.slice_in_dim(g, j * shp[axis], (j + 1) * shp[axis], axis=axis)
            parts.append(piece.reshape(-1))
        slabs.append(_pad_rows(jnp.concatenate(parts), 16))
    return jnp.stack(slabs)


def _unpack_grad_shards(packed):
    flat = packed.reshape(-1)
    out = {}
    off = 0
    for n in SHARDED_NAMES:
        shp = _shard_shape(n)
        size = shp[0] * shp[1]
        out[n] = flat[off:off + size].reshape(shp)
        off += size
    return out


def _gather_xy(mine):
    def body(x_ref, out_ref, send_sems, recv_sems, local_sem):
        x, y, c = lax.axis_index("x"), lax.axis_index("y"), lax.axis_index("c")
        peers = [(1 - x, y), (x, 1 - y), (1 - x, 1 - y)]

        def copy(k, slab, to):
            return pltpu.make_async_remote_copy(src_ref=x_ref, dst_ref=out_ref.at[slab], send_sem=send_sems.at[k],
                                                recv_sem=recv_sems.at[k], device_id=(to[0], to[1], c),
                                                device_id_type=MESH)

        own = pltpu.make_async_copy(x_ref, out_ref.at[2 * x + y], local_sem)
        own.start()
        sends = [copy(k, 2 * x + y, p) for k, p in enumerate(peers)]
        for s in sends:
            s.start()
        for k, p in enumerate(peers):
            copy(k, 2 * p[0] + p[1], p).wait_recv()
        for s in sends:
            s.wait_send()
        own.wait()

    return pl.pallas_call(
        body, name="gather_xy", in_specs=[ANY], out_specs=ANY,
        out_shape=jax.ShapeDtypeStruct((4,) + mine.shape, mine.dtype),
        scratch_shapes=[pltpu.SemaphoreType.DMA((3,)), pltpu.SemaphoreType.DMA((3,)), pltpu.SemaphoreType.DMA],
    )(mine)


def _scatter_xy(slabs):
    def body(x_ref, out_ref, send_sems, recv_sems, local_sem):
        x, y, c = lax.axis_index("x"), lax.axis_index("y"), lax.axis_index("c")
        me = 2 * x + y
        peers = [(1 - x, y), (x, 1 - y), (1 - x, 1 - y)]

        def copy(k, src_slab, dst_slab, to):
            return pltpu.make_async_remote_copy(src_ref=x_ref.at[src_slab], dst_ref=out_ref.at[dst_slab],
                                                send_sem=send_sems.at[k], recv_sem=recv_sems.at[k],
                                                device_id=(to[0], to[1], c), device_id_type=MESH)

        own = pltpu.make_async_copy(x_ref.at[me], out_ref.at[me], local_sem)
        own.start()
        sends = [copy(k, 2 * p[0] + p[1], me, p) for k, p in enumerate(peers)]
        for s in sends:
            s.start()
        for k, p in enumerate(peers):
            copy(k, me, 2 * p[0] + p[1], p).wait_recv()
        for s in sends:
            s.wait_send()
        own.wait()

    return pl.pallas_call(
        body, name="scatter_xy", in_specs=[ANY], out_specs=ANY, out_shape=jax.ShapeDtypeStruct(slabs.shape, slabs.dtype),
        scratch_shapes=[pltpu.SemaphoreType.DMA((3,)), pltpu.SemaphoreType.DMA((3,)), pltpu.SemaphoreType.DMA],
    )(slabs)


def _swap_halves_c(halves):
    def body(x_ref, out_ref, send_sem, recv_sem):
        x, y, c = lax.axis_index("x"), lax.axis_index("y"), lax.axis_index("c")
        cp = pltpu.make_async_remote_copy(src_ref=x_ref.at[1 - c], dst_ref=out_ref, send_sem=send_sem,
                                          recv_sem=recv_sem, device_id=(x, y, 1 - c), device_id_type=MESH)
        cp.start()
        cp.wait()

    return pl.pallas_call(
        body, name="swap_halves_c", in_specs=[ANY], out_specs=ANY,
        out_shape=jax.ShapeDtypeStruct(halves.shape[1:], halves.dtype),
        scratch_shapes=[pltpu.SemaphoreType.DMA, pltpu.SemaphoreType.DMA],
    )(halves)


def _gather_c(mine):
    def body(x_ref, out_ref, send_sem, recv_sem, local_sem):
        x, y, c = lax.axis_index("x"), lax.axis_index("y"), lax.axis_index("c")
        own = pltpu.make_async_copy(x_ref, out_ref.at[c], local_sem)
        own.start()
        pltpu.make_async_remote_copy(src_ref=x_ref, dst_ref=out_ref.at[c], send_sem=send_sem, recv_sem=recv_sem,
                                     device_id=(x, y, 1 - c), device_id_type=MESH).start()
        pltpu.make_async_remote_copy(src_ref=x_ref, dst_ref=out_ref.at[1 - c], send_sem=send_sem, recv_sem=recv_sem,
                                     device_id=(x, y, 1 - c), device_id_type=MESH).wait()
        own.wait()

    return pl.pallas_call(
        body, name="gather_c", in_specs=[ANY], out_specs=ANY,
        out_shape=jax.ShapeDtypeStruct((2,) + mine.shape, mine.dtype),
        scratch_shapes=[pltpu.SemaphoreType.DMA, pltpu.SemaphoreType.DMA, pltpu.SemaphoreType.DMA],
    )(mine)


def _gather_all(mine):
    flips = [(dx, dy, dc) for dx in (0, 1) for dy in (0, 1) for dc in (0, 1) if (dx, dy, dc) != (0, 0, 0)]

    def body(x_ref, out_ref, send_sems, recv_sems, local_sem):
        x, y, c = lax.axis_index("x"), lax.axis_index("y"), lax.axis_index("c")
        me = 4 * x + 2 * y + c

        def peer(f):
            return (x ^ f[0], y ^ f[1], c ^ f[2])

        def copy(k, slab, to):
            return pltpu.make_async_remote_copy(src_ref=x_ref, dst_ref=out_ref.at[slab], send_sem=send_sems.at[k],
                                                recv_sem=recv_sems.at[k], device_id=to, device_id_type=MESH)

        own = pltpu.make_async_copy(x_ref, out_ref.at[me], local_sem)
        own.start()
        sends = [copy(k, me, peer(f)) for k, f in enumerate(flips)]
        for s in sends:
            s.start()
        for k, f in enumerate(flips):
            p = peer(f)
            copy(k, 4 * p[0] + 2 * p[1] + p[2], p).wait_recv()
        for s in sends:
            s.wait_send()
        own.wait()

    return pl.pallas_call(
        body, name="gather_all", in_specs=[ANY], out_specs=ANY,
        out_shape=jax.ShapeDtypeStruct((8,) + mine.shape, mine.dtype),
        scratch_shapes=[pltpu.SemaphoreType.DMA((7,)), pltpu.SemaphoreType.DMA((7,)), pltpu.SemaphoreType.DMA],
    )(mine)


def _sum_slabs(slabs, name, extra=None, tr=512):
    n, R, _ = slabs.shape
    tr = _pick(R, (tr, 256, 128, 64, 32, 16, 8))
    has_extra = extra is not None

    def kern(*refs):
        s_ref = refs[0]
        o_ref = refs[-1]
        acc = refs[1][...] if has_extra else s_ref[0]
        for k in range(0 if has_extra else 1, n):
            acc = acc + s_ref[k]
        o_ref[...] = acc

    in_specs = [pl.BlockSpec((n, tr, LANES), lambda i: (0, i, 0))]
    ops = [slabs]
    if has_extra:
        in_specs.append(pl.BlockSpec((tr, LANES), lambda i: (i, 0)))
        ops.append(extra)
    return pl.pallas_call(
        kern, name=name, grid=(R // tr,), in_specs=in_specs, out_specs=pl.BlockSpec((tr, LANES), lambda i: (i, 0)),
        out_shape=jax.ShapeDtypeStruct((R, LANES), F32), compiler_params=_cparams(("parallel",)))(*ops)


def _adamw(w, g, m, v, name):
    R, C = w.shape
    tr = _pick(R, (256, 128, 64, 32, 16, 8))

    def kern(w_ref, g_ref, m_ref, v_ref, d_ref, nm_ref, nv_ref):
        gv = g_ref[...]
        m2 = ADAM_B1 * m_ref[...] + (1.0 - ADAM_B1) * gv
        v2 = ADAM_B2 * v_ref[...] + (1.0 - ADAM_B2) * jnp.square(gv)
        m_hat = m2 / (1.0 - ADAM_B1 ** ADAM_STEP)
        v_hat = v2 / (1.0 - ADAM_B2 ** ADAM_STEP)
        d_ref[...] = -ADAM_LR * (m_hat / (jnp.sqrt(v_hat) + ADAM_EPS) + ADAM_WD * w_ref[...])
        nm_ref[...] = m2
        nv_ref[...] = v2

    spec = pl.BlockSpec((tr, C), lambda i: (i, 0))
    return pl.pallas_call(
        kern, name=name, grid=(R // tr,), in_specs=[spec] * 4, out_specs=[spec] * 3,
        out_shape=[jax.ShapeDtypeStruct((R, C), F32)] * 3, compiler_params=_cparams(("parallel",)))(w, g, m, v)


def _pack_small(vals):
    return _pad_rows(jnp.concatenate([vals[n].astype(F32).reshape(-1) for n in SMALL_NAMES]), 8)


def _unpack_small(packed, shapes):
    flat = packed.reshape(-1)
    out = {}
    off = 0
    for n in SMALL_NAMES:
        size = int(np.prod(shapes[n]))
        out[n] = flat[off:off + size].reshape(shapes[n])
        off += size
    return out


def kernel(*args):
    A = dict(zip(ARG_NAMES, args, strict=True))
    x, mem, target = A['x'][0], A['mem'][0], A['loss_target'][0]

    gathered = _gather_xy(_pack_weight_shards({n: A[n] for n in SHARDED_NAMES}))
    W = _unpack_weights(gathered)
    for n in SMALL_NAMES:
        W[n] = A[n]

    loss_tile, grad_x, G = _local_step(x, mem, target, W)
    loss = lax.psum(loss_tile[0, 0], ("x", "y", "c"))

    gp = _pack_grads(G)
    rg = gp.shape[1]
    halves = gp.reshape(4, 2, rg // 2, LANES).transpose(1, 0, 2, 3)
    c = lax.axis_index("c")
    got = _swap_halves_c(halves)
    mine = lax.dynamic_index_in_dim(halves, c, axis=0, keepdims=False)
    pair = _sum_slabs(jnp.stack([mine, got]).reshape(2, 4 * (rg // 2), LANES), "sum_pair")
    recv = _scatter_xy(pair.reshape(4, rg // 2, LANES))
    half = _sum_slabs(recv, "sum_chips")
    full = _gather_c(half).reshape(rg, LANES)
    g_sh = _unpack_grad_shards(full)

    small_shapes = {n: A[n].shape for n in SMALL_NAMES}
    g_small = _sum_slabs(_gather_all(_pack_small({n: G[n] for n in SMALL_NAMES})), "sum_small")
    d_s, m_s, v_s = _adamw(_pack_small({n: A[n] for n in SMALL_NAMES}), g_small,
                           _pack_small({n: A['m_' + n] for n in SMALL_NAMES}),
                           _pack_small({n: A['v_' + n] for n in SMALL_NAMES}), "adamw_small")
    g_small, d_s, m_s, v_s = (_unpack_small(p, small_shapes) for p in (g_small, d_s, m_s, v_s))

    grads, deltas, new_m, new_v = {}, {}, {}, {}
    for n in WEIGHTS:
        if n in SHARDED:
            grads[n] = g_sh[n]
            deltas[n], new_m[n], new_v[n] = _adamw(A[n], g_sh[n], A['m_' + n], A['v_' + n], "adamw_" + n)
        else:
            grads[n], deltas[n], new_m[n], new_v[n] = g_small[n], d_s[n], m_s[n], v_s[n]
    return (loss, grad_x[None], *[grads[n] for n in WEIGHTS], *[deltas[n] for n in WEIGHTS],
            *[new_m[n] for n in WEIGHTS], *[new_v[n] for n in WEIGHTS])
```

```python
import functools
import math

import numpy as np
import jax
import jax.numpy as jnp
from jax import lax
from jax.experimental import pallas as pl
from jax.experimental.pallas import tpu as pltpu

F32 = jnp.float32
BF16 = jnp.bfloat16
EPS = 1e-6
MESH = pl.DeviceIdType.MESH

ADAM_LR = 0.001
ADAM_B1 = 0.9
ADAM_B2 = 0.999
ADAM_EPS = 1e-08
ADAM_WD = 0.01
ADAM_STEP = 10

VMEM_LIMIT_BYTES = 56 * 1024 * 1024
LANES = 1024

WEIGHTS = ['l0_mix_norm', 'l0_w_in', 'l0_ret_norm', 'l0_s5_lambda_re', 'l0_s5_lambda_im', 'l0_s5_b_re', 'l0_s5_b_im',
           'l0_s5_c_re', 'l0_s5_c_im', 'l0_s5_d', 'l0_s5_log_dt', 'l0_s5_w_glu', 'l0_s5_b_glu', 'l0_w_out',
           'l0_xa_norm', 'l0_mem_norm', 'l0_xa_wq', 'l0_xa_wkv', 'l0_xa_wo', 'l0_ffn_norm', 'l0_ffn_w_up',
           'l0_ffn_conv', 'l0_ffn_w_down', 'l1_mix_norm', 'l1_w_in', 'l1_conv', 'l1_a_log', 'l1_dt_bias',
           'l1_o_norm', 'l1_w_out', 'l1_xa_norm', 'l1_mem_norm', 'l1_xa_wq', 'l1_xa_wkv', 'l1_xa_wo',
           'l1_ffn_norm', 'l1_ffn_w_up', 'l1_ffn_conv', 'l1_ffn_w_down', 'final_norm']
ARG_NAMES = (['x', 'mem'] + WEIGHTS + ['loss_target'] + ['m_' + w for w in WEIGHTS] + ['v_' + w for w in WEIGHTS])

MATRICES = {
    'l0_w_in': 1, 'l0_s5_w_glu': 0, 'l0_w_out': 0, 'l0_xa_wq': 0, 'l0_xa_wkv': 1, 'l0_xa_wo': 0, 'l0_ffn_w_up': 1,
    'l0_ffn_w_down': 0, 'l1_w_in': 1, 'l1_w_out': 0, 'l1_xa_wq': 0, 'l1_xa_wkv': 1, 'l1_xa_wo': 0,
    'l1_ffn_w_up': 1, 'l1_ffn_w_down': 0,
}
CONVS = ('l0_ffn_conv', 'l1_conv', 'l1_ffn_conv')
MATRIX_NAMES = [w for w in WEIGHTS if w in MATRICES]
SMALL_NAMES = [w for w in WEIGHTS if w not in MATRICES]


def _cparams(sem=None):
    return pltpu.CompilerParams(dimension_semantics=sem, vmem_limit_bytes=VMEM_LIMIT_BYTES)


def _pick(n, cands):
    for c in cands:
        if n % c == 0:
            return c
    return n


_NN = ((1,), (0,))
_NT = ((1,), (1,))
_TN = ((0,), (0,))


def _dot(a, b, dims, hi):
    if hi:
        return lax.dot_general(a.astype(F32), b.astype(F32), (dims, ((), ())), precision=lax.Precision.HIGHEST,
                               preferred_element_type=F32)
    return lax.dot_general(a.astype(BF16), b.astype(BF16), (dims, ((), ())), preferred_element_type=F32)


def _make_mm(hi):
    @jax.custom_vjp
    def nn(a, b):
        return _dot(a, b, _NN, hi)

    def nn_f(a, b):
        return nn(a, b), (a, b)

    def nn_b(r, g):
        a, b = r
        return _dot(g, b, _NT, hi), _dot(a, g, _TN, hi)

    nn.defvjp(nn_f, nn_b)

    @jax.custom_vjp
    def nt(a, b):
        return _dot(a, b, _NT, hi)

    def nt_f(a, b):
        return nt(a, b), (a, b)

    def nt_b(r, g):
        a, b = r
        return _dot(g, b, _NN, hi), _dot(g, a, _TN, hi)

    nt.defvjp(nt_f, nt_b)

    @jax.custom_vjp
    def tn(a, b):
        return _dot(a, b, _TN, hi)

    def tn_f(a, b):
        return tn(a, b), (a, b)

    def tn_b(r, g):
        a, b = r
        return _dot(b, g, _NT, hi), _dot(a, g, _NN, hi)

    tn.defvjp(tn_f, tn_b)
    return nn, nt, tn


mm, mm_nt, mm_tn = _make_mm(False)
mmh, mmh_nt, mmh_tn = _make_mm(True)


@jax.custom_vjp
def _swap_halves(x):
    return pltpu.roll(x, 64, 1)


def _swap_f(x):
    return pltpu.roll(x, 64, 1), None


def _swap_b(_, g):
    return (pltpu.roll(g, 64, 1),)


_swap_halves.defvjp(_swap_f, _swap_b)


def _silu(x):
    return x * jax.nn.sigmoid(x)


def _rms(x, g):
    return x * lax.rsqrt(jnp.mean(x * x, axis=-1, keepdims=True) + EPS) * g


def _iota(shape, dim):
    return lax.broadcasted_iota(jnp.int32, shape, dim)


def _matmul(a, b, mode="nn", res=None, name="mm", a_cols=None):
    a_off, a_w = (0, a.shape[1]) if a_cols is None else a_cols
    if mode == "nn":
        (M, K), (K2, N) = (a.shape[0], a_w), b.shape
    elif mode == "nt":
        (M, K), (N, K2) = (a.shape[0], a_w), b.shape
    else:
        (K, M), (K2, N) = (a.shape[0], a_w), b.shape
    assert K == K2, (a.shape, b.shape, mode)
    tm = _pick(M, (512, 256, 128))
    tn = _pick(N, (1024, 512, 256, 128))
    tk = _pick(K, (1024, 512, 256, 128))
    nk = K // tk
    dims = {"nn": _NN, "nt": _NT, "tn": _TN}[mode]
    ao = a_off // (tm if mode == "tn" else tk)
    assert ao * (tm if mode == "tn" else tk) == a_off
    if mode == "nn":
        a_spec = pl.BlockSpec((tm, tk), lambda i, j, k: (i, k + ao))
        b_spec = pl.BlockSpec((tk, tn), lambda i, j, k: (k, j))
    elif mode == "nt":
        a_spec = pl.BlockSpec((tm, tk), lambda i, j, k: (i, k + ao))
        b_spec = pl.BlockSpec((tn, tk), lambda i, j, k: (j, k))
    else:
        a_spec = pl.BlockSpec((tk, tm), lambda i, j, k: (k, i + ao))
        b_spec = pl.BlockSpec((tk, tn), lambda i, j, k: (k, j))
    o_spec = pl.BlockSpec((tm, tn), lambda i, j, k: (i, j))
    has_res = res is not None

    def kern(*refs):
        if has_res:
            a_ref, b_ref, r_ref, o_ref, acc_ref = refs
        else:
            a_ref, b_ref, o_ref, acc_ref = refs
        k = pl.program_id(2)

        @pl.when(k == 0)
        def _():
            acc_ref[...] = jnp.zeros_like(acc_ref)

        acc_ref[...] += lax.dot_general(a_ref[...].astype(BF16), b_ref[...].astype(BF16), (dims, ((), ())),
                                        preferred_element_type=F32)

        @pl.when(k == nk - 1)
        def _():
            if has_res:
                o_ref[...] = acc_ref[...] + r_ref[...]
            else:
                o_ref[...] = acc_ref[...]

    in_specs = [a_spec, b_spec] + ([o_spec] if has_res else [])
    ops = (a, b) + ((res,) if has_res else ())
    return pl.pallas_call(
        kern, name=name, grid=(M // tm, N // tn, nk), in_specs=in_specs, out_specs=o_spec,
        out_shape=jax.ShapeDtypeStruct((M, N), F32), scratch_shapes=[pltpu.VMEM((tm, tn), F32)],
        compiler_params=_cparams(("parallel", "parallel", "arbitrary")))(*ops)


def _matmul_cat(pieces, b, mode="nn", res=None, name="mmcat"):
    M = pieces[0].shape[0]
    widths = [p.shape[1] for p in pieces]
    K = sum(widths)
    N = b.shape[1] if mode == "nn" else b.shape[0]
    assert (b.shape[0] if mode == "nn" else b.shape[1]) == K
    tm = _pick(M, (256, 128))
    tn = _pick(N, (1024, 512, 256, 128))
    npc = len(pieces)
    has_res = res is not None
    dims = _NN if mode == "nn" else _NT

    def kern(*refs):
        b_ref = refs[npc]
        o_ref = refs[-1]
        acc = refs[npc + 1][...] if has_res else None
        off = 0
        for p in range(npc):
            bp = b_ref[off:off + widths[p], :] if mode == "nn" else b_ref[:, off:off + widths[p]]
            t = lax.dot_general(refs[p][...].astype(BF16), bp.astype(BF16), (dims, ((), ())),
                                preferred_element_type=F32)
            acc = t if acc is None else acc + t
            off += widths[p]
        o_ref[...] = acc

    in_specs = [pl.BlockSpec((tm, w), lambda j, i: (i, 0)) for w in widths]
    in_specs.append(pl.BlockSpec((K, tn), lambda j, i: (0, j)) if mode == "nn"
                    else pl.BlockSpec((tn, K), lambda j, i: (j, 0)))
    o_spec = pl.BlockSpec((tm, tn), lambda j, i: (i, j))
    if has_res:
        in_specs.append(o_spec)
    ops = list(pieces) + [b] + ([res] if has_res else [])
    return pl.pallas_call(
        kern, name=name, grid=(N // tn, M // tm), in_specs=in_specs, out_specs=o_spec,
        out_shape=jax.ShapeDtypeStruct((M, N), F32), compiler_params=_cparams(("parallel", "parallel")))(*ops)


def _blk(a, width=None, colblk=0):
    return (a, a.shape[1] if width is None else width, colblk)


def _row_specs(blocked, params, ts):
    specs = []
    for (_, w, cb) in blocked:
        specs.append(pl.BlockSpec((ts, w), functools.partial(lambda i, cb: (i, cb), cb=cb)))
    for p in params:
        specs.append(pl.BlockSpec(p.shape, lambda i: (0, 0)))
    return specs


def _rowwise(fn, blocked, params, out_widths, name, ts=256):
    S = blocked[0][0].shape[0]
    ts = min(ts, S)
    nb, npar = len(blocked), len(params)

    def kern(*refs):
        vals = [r[...] for r in refs[:nb + npar]]
        outs = fn(*vals)
        for o_ref, o in zip(refs[nb + npar:], outs):
            o_ref[...] = o

    return pl.pallas_call(
        kern, name=name, grid=(S // ts,), in_specs=_row_specs(blocked, params, ts),
        out_specs=[pl.BlockSpec((ts, w), lambda i: (i, 0)) for w in out_widths],
        out_shape=[jax.ShapeDtypeStruct((S, w), F32) for w in out_widths],
        compiler_params=_cparams(("parallel",)))(*[b[0] for b in blocked], *params)


def _rowwise_bwd(fn, blocked, params, cots, name, blocked_grad=None, param_grad=None, adds=None, ts=256):
    S = blocked[0][0].shape[0]
    ts = min(ts, S)
    cots = [c if isinstance(c, tuple) else _blk(c) for c in cots]
    nb, npar, nc = len(blocked), len(params), len(cots)
    blocked_grad = [True] * nb if blocked_grad is None else blocked_grad
    param_grad = [True] * npar if param_grad is None else param_grad
    adds = {} if adds is None else adds
    bidx = [i for i in range(nb) if blocked_grad[i]]
    pidx = [i for i in range(npar) if param_grad[i]]
    add_keys = sorted(adds)
    n_in = nb + npar + nc + len(add_keys)

    def kern(*refs):
        i = pl.program_id(0)
        xs = [r[...] for r in refs[:nb]]
        ps = [r[...] for r in refs[nb:nb + npar]]
        gs = [r[...] for r in refs[nb + npar:nb + npar + nc]]
        add_vals = {k: refs[nb + npar + nc + n][...] for n, k in enumerate(add_keys)}
        outs = refs[n_in:]

        def f(*diff):
            full_x = list(xs)
            full_p = list(ps)
            for n, ix in enumerate(bidx):
                full_x[ix] = diff[n]
            for n, ix in enumerate(pidx):
                full_p[ix] = diff[len(bidx) + n]
            return tuple(fn(*full_x, *full_p))

        _, vjp = jax.vjp(f, *[xs[ix] for ix in bidx], *[ps[ix] for ix in pidx])
        grads = vjp(tuple(gs))
        for n, ix in enumerate(bidx):
            g = grads[n]
            if ix in add_vals:
                g = g + add_vals[ix]
            outs[n][...] = g
        for n in range(len(pidx)):
            o_ref = outs[len(bidx) + n]

            @pl.when(i == 0)
            def _(o_ref=o_ref):
                o_ref[...] = jnp.zeros_like(o_ref)

            o_ref[...] += grads[len(bidx) + n]

    in_specs = _row_specs(blocked, params, ts)
    in_specs += _row_specs(cots, [], ts)
    in_specs += [pl.BlockSpec((ts, adds[k].shape[1]), lambda i: (i, 0)) for k in add_keys]
    out_specs = [pl.BlockSpec((ts, blocked[ix][1]), lambda i: (i, 0)) for ix in bidx]
    out_specs += [pl.BlockSpec(params[ix].shape, lambda i: (0, 0)) for ix in pidx]
    out_shape = [jax.ShapeDtypeStruct((S, blocked[ix][1]), F32) for ix in bidx]
    out_shape += [jax.ShapeDtypeStruct(params[ix].shape, F32) for ix in pidx]
    return pl.pallas_call(
        kern, name=name, grid=(S // ts,), in_specs=in_specs, out_specs=out_specs, out_shape=out_shape,
        compiler_params=_cparams(("arbitrary",)))(*[b[0] for b in blocked], *params, *[c[0] for c in cots],
                                                    *[adds[k] for k in add_keys])


def _rms_fn(x, g):
    return (_rms(x, g),)


def _head_norm(o, n_heads, dh):
    outs = []
    for h in range(n_heads):
        oh = o[:, h * dh:(h + 1) * dh]
        outs.append(oh * lax.rsqrt(jnp.mean(oh * oh, axis=-1, keepdims=True) + EPS))
    return outs


def _ret_post_fn(o_raw, gate, ret_norm):
    o = jnp.concatenate(_head_norm(o_raw, 4, 128), axis=1)
    return (o * ret_norm * _silu(gate),)


def _s5_post_fn(y1, y2, u, d, w_glu, b_glu):
    y = y1 - y2 + d * u
    y = jax.nn.gelu(y)
    return (y * jax.nn.sigmoid(mm(y, w_glu) + b_glu),)


def _xattn_fn(q, kv):
    outs = []
    for h in range(4):
        qh = q[:, h * 256:(h + 1) * 256]
        kh = kv[:, h * 256:(h + 1) * 256]
        vh = kv[:, 1024 + h * 256:1024 + (h + 1) * 256]
        s = mm_nt(qh, kh) * (256 ** -0.5)
        s = s - lax.stop_gradient(jnp.max(s, axis=-1, keepdims=True))
        p = jnp.exp(s)
        p = p / jnp.sum(p, axis=-1, keepdims=True)
        outs.append(mm(p, vh))
    return (jnp.concatenate(outs, axis=1),)


def _softplus(x):
    return jnp.maximum(x, 0.0) + jnp.log1p(jnp.exp(-jnp.abs(x)))


def _gdn_gates_fn(pt, a_log_p, dtb_p):
    rows, cols = _iota((128, 1024), 0), _iota((128, 1024), 1)
    e_b = (rows == (cols >> 7)).astype(F32)
    e_a = (rows == (cols >> 7) + 8).astype(F32)
    beta = jax.nn.sigmoid(pt)
    g = -(jnp.exp(a_log_p) * _softplus(pt + dtb_p))
    return mmh(g, e_a), mmh(beta, e_b)


def _gdn_post_fn(o_raw, z, o_norm):
    outs = _head_norm(o_raw, 8, 128)
    o = jnp.concatenate([oh * o_norm for oh in outs], axis=1)
    return (o * _silu(z),)


def _ffn_post(up, gate):
    return _silu(gate) * up


def _shift_down(cur, prev8, sh, row8):
    if sh == 0:
        return cur
    r = pltpu.roll(cur, sh, 0)
    p = pltpu.roll(prev8, sh, 0)
    top = jnp.where(row8 < sh, p, r[0:8])
    return jnp.concatenate([top, r[8:]], axis=0)


def _shift_up(cur, next8, sh, row8):
    if sh == 0:
        return cur
    ts = cur.shape[0]
    r = pltpu.roll(cur, ts - sh, 0)
    p = pltpu.roll(next8, 8 - sh, 0)
    bot = jnp.where(row8 >= 8 - sh, p, r[ts - 8:])
    return jnp.concatenate([r[:ts - 8], bot], axis=0)


def _conv_rows(cur, prev8, wrows, row8):
    k_w = len(wrows)
    out = None
    for j in range(k_w):
        t = _shift_down(cur, prev8, k_w - 1 - j, row8) * wrows[j]
        out = t if out is None else out + t
    return out


def _conv_specs(x, xoff, w, woff, ts, tc):
    r8 = ts // 8
    return [pl.BlockSpec((ts, tc), functools.partial(lambda i, j, o: (i, j + o), o=xoff)),
            pl.BlockSpec((8, tc), functools.partial(lambda i, j, o: (jnp.maximum(i * r8 - 1, 0), j + o), o=xoff)),
            pl.BlockSpec((w.shape[0], tc), functools.partial(lambda i, j, o: (0, j + o), o=woff))]


def _conv_post(srcs, post, ncol, tc, name, cots=None, ts=256):
    S = srcs[0][0].shape[0]
    ns = len(srcs)
    bwd = cots is not None

    def kern(*refs):
        first = pl.program_id(0) == 0
        row8 = _iota((8, tc), 0)
        cs = []
        for s in range(ns):
            cur_ref, prev_ref, w_ref = refs[3 * s:3 * s + 3]
            prev = jnp.where(first, 0.0, prev_ref[...])
            wrows = [w_ref[j:j + 1, :] for j in range(w_ref.shape[0])]
            cs.append(_conv_rows(cur_ref[...], prev, wrows, row8))
        if bwd:
            g = refs[3 * ns][...]
            _, vjp = jax.vjp(lambda *c: post(*c), *cs)
            for o_ref, d in zip(refs[3 * ns + 1:], vjp(g)):
                o_ref[...] = d
        else:
            refs[3 * ns][...] = post(*cs)

    in_specs = []
    ops = []
    for (x, xoff, w, woff) in srcs:
        in_specs += _conv_specs(x, xoff, w, woff, ts, tc)
        ops += [x, x, w]
    o_spec = pl.BlockSpec((ts, tc), lambda i, j: (i, j))
    o_shape = jax.ShapeDtypeStruct((S, ncol * tc), F32)
    if bwd:
        in_specs.append(o_spec)
        ops.append(cots)
        out_specs, out_shape = [o_spec] * ns, [o_shape] * ns
    else:
        out_specs, out_shape = o_spec, o_shape
    return pl.pallas_call(
        kern, name=name, grid=(S // ts, ncol), in_specs=in_specs, out_specs=out_specs, out_shape=out_shape,
        compiler_params=_cparams(("parallel", "parallel")))(*ops)


def _conv_bwd(dc, x, xoff, w, woff, ncol, tc, name, ts=256):
    S = x.shape[0]
    k_w = w.shape[0]
    r8 = ts // 8
    nblk8 = S // 8
    nrow = S // ts

    def kern(dc_ref, dn_ref, x_ref, xp_ref, w_ref, dx_ref, dw_ref):
        i = pl.program_id(1)
        row8 = _iota((8, tc), 0)
        dcur = dc_ref[...]
        dnext = jnp.where(i == nrow - 1, 0.0, dn_ref[...])
        xcur = x_ref[...]
        xprev = jnp.where(i == 0, 0.0, xp_ref[...])

        @pl.when(i == 0)
        def _():
            dw_ref[...] = jnp.zeros_like(dw_ref)

        dx = None
        for j in range(k_w):
            sh = k_w - 1 - j
            wj = w_ref[j:j + 1, :]
            t = _shift_up(dcur, dnext, sh, row8) * wj
            dx = t if dx is None else dx + t
            dw_ref[j:j + 1, :] += jnp.sum(dcur * _shift_down(xcur, xprev, sh, row8), axis=0, keepdims=True)
        dx_ref[...] = dx

    in_specs = [pl.BlockSpec((ts, tc), lambda j, i: (i, j)),
                pl.BlockSpec((8, tc), lambda j, i: (jnp.minimum((i + 1) * r8, nblk8 - 1), j)),
                pl.BlockSpec((ts, tc), functools.partial(lambda j, i, o: (i, j + o), o=xoff)),
                pl.BlockSpec((8, tc), functools.partial(lambda j, i, o: (jnp.maximum(i * r8 - 1, 0), j + o), o=xoff)),
                pl.BlockSpec((k_w, tc), functools.partial(lambda j, i, o: (0, j + o), o=woff))]
    out_specs = [pl.BlockSpec((ts, tc), lambda j, i: (i, j)), pl.BlockSpec((k_w, tc), lambda j, i: (0, j))]
    out_shape = [jax.ShapeDtypeStruct((S, ncol * tc), F32), jax.ShapeDtypeStruct((k_w, ncol * tc), F32)]
    return pl.pallas_call(
        kern, name=name, grid=(ncol, nrow), in_specs=in_specs, out_specs=out_specs, out_shape=out_shape,
        compiler_params=_cparams(("parallel", "arbitrary")))(dc, dc, x, x, w)


def _ret_tables(S):
    H, C, dh = 4, 128, 128
    lg = jnp.log1p(-jnp.exp2(-5.0 - jnp.arange(H, dtype=F32)))
    idx = jnp.arange(C, dtype=F32)
    diff = idx[:, None] - idx[None, :]
    causal = diff >= 0
    intra = jnp.where(causal, jnp.exp(lg[:, None, None] * jnp.where(causal, diff, 0.0)), 0.0)
    kdec = jnp.broadcast_to(jnp.exp(lg[:, None] * (C - 1 - idx))[:, :, None], (H, C, dh))
    qdec = jnp.broadcast_to(jnp.exp(lg[:, None] * (idx + 1))[:, :, None], (H, C, dh))
    cdec = jnp.broadcast_to(jnp.exp(lg * C)[:, None, None], (H, dh, dh))
    half = dh // 2
    inv = jnp.exp(-math.log(10000.0) * jnp.arange(half, dtype=F32) / half)
    ang = jnp.arange(S).astype(F32)[:, None] * inv[None, :]
    cos, sin = jnp.cos(ang), jnp.sin(ang)
    cosf = jnp.concatenate([cos, cos], axis=1)
    sinf = jnp.concatenate([-sin, sin], axis=1)
    return cosf, sinf, intra, kdec, qdec, cdec


def _ret_chunk(q, k, v, cosf, sinf, intra, kdec, qdec, cdec, state):
    qr = q * cosf + _swap_halves(q) * sinf
    kr = (k * cosf + _swap_halves(k) * sinf) * (128 ** -0.5)
    scores = mm_nt(qr, kr) * intra
    inner = mm(scores, v)
    kv = mm_tn(kr * kdec, v)
    cross = mm(qr * qdec, state)
    return inner + cross, state * cdec + kv


def _ret_specs(N, rev):
    def nn(n):
        return N - 1 - n if rev else n
    chunk = [pl.BlockSpec((128, 128), functools.partial(lambda h, n, o: (nn(n), h + o), o=o)) for o in (0, 4, 8)]
    pos = [pl.BlockSpec((128, 128), lambda h, n: (nn(n), 0))] * 2
    tabs = [pl.BlockSpec((None, 128, 128), lambda h, n: (h, 0, 0))] * 4
    st = pl.BlockSpec((None, None, 128, 128), lambda h, n: (h, nn(n), 0, 0))
    o = pl.BlockSpec((128, 128), lambda h, n: (nn(n), h))
    return chunk, pos, tabs, st, o


def _ret_fwd(proj, tabs):
    S = proj.shape[0]
    N = S // 128
    chunk, pos, tsp, st, o = _ret_specs(N, False)

    def kern(q_ref, k_ref, v_ref, c_ref, s_ref, i_ref, kd_ref, qd_ref, cd_ref, o_ref, sp_ref, st_ref):
        @pl.when(pl.program_id(1) == 0)
        def _():
            st_ref[...] = jnp.zeros_like(st_ref)

        state = st_ref[...]
        sp_ref[...] = state
        out, new = _ret_chunk(q_ref[...], k_ref[...], v_ref[...], c_ref[...], s_ref[...], i_ref[...], kd_ref[...],
                              qd_ref[...], cd_ref[...], state)
        o_ref[...] = out
        st_ref[...] = new

    return pl.pallas_call(
        kern, name="ret_fwd", grid=(4, N), in_specs=chunk + pos + tsp, out_specs=[o, st],
        out_shape=[jax.ShapeDtypeStruct((S, 512), F32), jax.ShapeDtypeStruct((4, N, 128, 128), F32)],
        scratch_shapes=[pltpu.VMEM((128, 128), F32)],
        compiler_params=_cparams(("parallel", "arbitrary")))(proj, proj, proj, *tabs)


def _ret_bwd(proj, tabs, states, do):
    S = proj.shape[0]
    N = S // 128
    chunk, pos, tsp, st, o = _ret_specs(N, True)

    def kern(q_ref, k_ref, v_ref, c_ref, s_ref, i_ref, kd_ref, qd_ref, cd_ref, sp_ref, do_ref,
             dq_ref, dk_ref, dv_ref, ds_ref):
        @pl.when(pl.program_id(1) == 0)
        def _():
            ds_ref[...] = jnp.zeros_like(ds_ref)

        consts = (c_ref[...], s_ref[...], i_ref[...], kd_ref[...], qd_ref[...], cd_ref[...])
        _, vjp = jax.vjp(lambda q, k, v, s: _ret_chunk(q, k, v, *consts, s), q_ref[...], k_ref[...], v_ref[...],
                         sp_ref[...])
        dq, dk, dv, ds = vjp((do_ref[...], ds_ref[...]))
        dq_ref[...] = dq
        dk_ref[...] = dk
        dv_ref[...] = dv
        ds_ref[...] = ds

    return pl.pallas_call(
        kern, name="ret_bwd", grid=(4, N), in_specs=chunk + pos + tsp + [st, o], out_specs=[o, o, o],
        out_shape=[jax.ShapeDtypeStruct((S, 512), F32)] * 3, scratch_shapes=[pltpu.VMEM((128, 128), F32)],
        compiler_params=_cparams(("parallel", "arbitrary")))(proj, proj, proj, *tabs, states, do)


GDN_C = 64
GDN_H = 8


def _gdn_intra(q, k, v, g_b, beta_b):
    c = GDN_C
    q = q * lax.rsqrt(jnp.sum(q * q, axis=-1, keepdims=True) + EPS) * (128 ** -0.5)
    k = k * lax.rsqrt(jnp.sum(k * k, axis=-1, keepdims=True) + EPS)
    ri, ci = _iota((c, c), 0), _iota((c, c), 1)
    incl = ri >= ci
    strict = ri > ci
    eye = (ri == ci).astype(F32)
    gc_b = mmh(incl.astype(F32), g_b)
    gl_b = mmh(jnp.ones((c, c), F32), g_b)
    kb = k * beta_b
    vb = v * beta_b
    gcc = gc_b[:, :c]
    gdiff = gcc - gcc.T
    decay = jnp.where(incl, jnp.exp(jnp.where(incl, gdiff, 0.0)), 0.0)
    a_mat = jnp.where(strict, mm_nt(kb, k) * decay, 0.0)
    p = -a_mat
    t_mat = eye + p
    for _ in range(5):
        p = mmh(p, p)
        t_mat = mmh(t_mat, eye + p)
    egc = jnp.exp(gc_b)
    w = mm(t_mat, kb * egc)
    u = mm(t_mat, vb)
    qk = jnp.where(incl, mm_nt(q, k) * decay, 0.0)
    return w, u, q * egc, k * jnp.exp(gl_b - gc_b), qk


def _gdn_step(w, u, q_dec, k_dec, qk, g_b, state):
    gl_s = mmh(jnp.ones((128, GDN_C), F32), g_b)
    v_new = u - mm(w, state)
    o = mm(q_dec, state) + mm(qk, v_new)
    return o, state * jnp.exp(gl_s) + mm_tn(k_dec, v_new)


def _hs(h):
    return slice(h * 128, (h + 1) * 128)


def _gdn_intra_call(qkv, g_e, beta_e, cots=None):
    S = qkv.shape[0]
    N = S // GDN_C
    bwd = cots is not None
    row = pl.BlockSpec((GDN_C, 1024), lambda n: (n, 0))
    qkv_spec = pl.BlockSpec((GDN_C, 3072), lambda n: (n, 0))
    qk_spec = pl.BlockSpec((GDN_H, GDN_C, GDN_C), lambda n: (0, n, 0))

    def kern(*refs):
        x_ref, g_ref, b_ref = refs[:3]
        for h in range(GDN_H):
            args = (x_ref[:, _hs(h)], x_ref[:, _hs(8 + h)], x_ref[:, _hs(16 + h)], g_ref[:, _hs(h)], b_ref[:, _hs(h)])
            if bwd:
                dw_ref, du_ref, dqd_ref, dkd_ref, dqk_ref, dgadd_ref = refs[3:9]
                outs = refs[9:]
                _, vjp = jax.vjp(_gdn_intra, *args)
                dq, dk, dv, dg, db = vjp((dw_ref[:, _hs(h)], du_ref[:, _hs(h)], dqd_ref[:, _hs(h)],
                                          dkd_ref[:, _hs(h)], dqk_ref[h]))
                for o_ref, d in zip(outs, (dq, dk, dv, dg + dgadd_ref[:, _hs(h)], db)):
                    o_ref[:, _hs(h)] = d
            else:
                w, u, qd, kd, qk = _gdn_intra(*args)
                for o_ref, o in zip(refs[3:7], (w, u, qd, kd)):
                    o_ref[:, _hs(h)] = o
                refs[7][h] = qk

    big = jax.ShapeDtypeStruct((S, 1024), F32)
    if bwd:
        in_specs = [qkv_spec, row, row, row, row, row, row, qk_spec, row]
        out_specs, out_shape = [row] * 5, [big] * 5
        ops = (qkv, g_e, beta_e) + tuple(cots)
    else:
        in_specs = [qkv_spec, row, row]
        out_specs = [row] * 4 + [qk_spec]
        out_shape = [big] * 4 + [jax.ShapeDtypeStruct((GDN_H, S, GDN_C), F32)]
        ops = (qkv, g_e, beta_e)
    return pl.pallas_call(
        kern, name="gdn_intra_bwd" if bwd else "gdn_intra", grid=(N,), in_specs=in_specs, out_specs=out_specs,
        out_shape=out_shape, compiler_params=_cparams(("parallel",)))(*ops)


def _gdn_pass(w, u, qd, kd, qk, g_e, states=None, do=None):
    S = w.shape[0]
    N = S // GDN_C
    bwd = do is not None

    def nn(n):
        return N - 1 - n if bwd else n

    row = pl.BlockSpec((GDN_C, 1024), lambda n: (nn(n), 0))
    qk_spec = pl.BlockSpec((GDN_H, GDN_C, GDN_C), lambda n: (0, nn(n), 0))
    st_spec = pl.BlockSpec((None, GDN_H, 128, 128), lambda n: (nn(n), 0, 0, 0))

    def kern(*refs):
        w_ref, u_ref, qd_ref, kd_ref, qk_ref, g_ref = refs[:6]
        carry = refs[-1]

        @pl.when(pl.program_id(0) == 0)
        def _():
            carry[...] = jnp.zeros_like(carry)

        for h in range(GDN_H):
            args = (w_ref[:, _hs(h)], u_ref[:, _hs(h)], qd_ref[:, _hs(h)], kd_ref[:, _hs(h)], qk_ref[h],
                    g_ref[:, _hs(h)])
            if bwd:
                sp_ref, do_ref = refs[6:8]
                outs = refs[8:14]
                _, vjp = jax.vjp(_gdn_step, *args, sp_ref[h])
                dw, du, dqd, dkd, dqk, dg, ds = vjp((do_ref[:, _hs(h)], carry[h]))
                for o_ref, d in zip(outs[:4], (dw, du, dqd, dkd)):
                    o_ref[:, _hs(h)] = d
                outs[4][h] = dqk
                outs[5][:, _hs(h)] = dg
                carry[h] = ds
            else:
                o_ref, sp_ref = refs[6:8]
                state = carry[h]
                sp_ref[h] = state
                o, new = _gdn_step(*args, state)
                o_ref[:, _hs(h)] = o
                carry[h] = new

    big = jax.ShapeDtypeStruct((S, 1024), F32)
    in_specs = [row, row, row, row, qk_spec, row]
    if bwd:
        in_specs += [st_spec, row]
        out_specs = [row] * 4 + [qk_spec, row]
        out_shape = [big] * 4 + [jax.ShapeDtypeStruct((GDN_H, S, GDN_C), F32), big]
        ops = (w, u, qd, kd, qk, g_e, states, do)
    else:
        out_specs = [row, st_spec]
        out_shape = [big, jax.ShapeDtypeStruct((N, GDN_H, 128, 128), F32)]
        ops = (w, u, qd, kd, qk, g_e)
    return pl.pallas_call(
        kern, name="gdn_pass_bwd" if bwd else "gdn_pass", grid=(N,), in_specs=in_specs, out_specs=out_specs,
        out_shape=out_shape, scratch_shapes=[pltpu.VMEM((GDN_H, 128, 128), F32)],
        compiler_params=_cparams(("arbitrary",)))(*ops)


def _s5_prep_fn(lr, li, ldt, br, bi, cr, ci):
    dt = jnp.exp(ldt)
    mag = jnp.exp(lr * dt)
    a_re = mag * jnp.cos(li * dt)
    a_im = mag * jnp.sin(li * dt)
    den = lr * lr + li * li
    z_re = ((a_re - 1.0) * lr + a_im * li) / den
    z_im = (a_im * lr - (a_re - 1.0) * li) / den
    e1 = ((_iota((512, 32), 0) >> 4) == _iota((512, 32), 1)).astype(F32)
    zr_e = mmh(e1, z_re)
    zi_e = mmh(e1, z_im)
    bb_re = zr_e * br - zi_e * bi
    bb_im = zr_e * bi + zi_e * br
    t1 = ((_iota((64, 2048), 1) & 63) == _iota((64, 2048), 0)).astype(F32)
    m1 = (_iota((512, 2048), 0) >> 4) == (_iota((512, 2048), 1) >> 6)
    bd_re = jnp.where(m1, mmh(bb_re, t1), 0.0)
    bd_im = jnp.where(m1, mmh(bb_im, t1), 0.0)
    t2 = ((_iota((16, 512), 1) & 15) == _iota((16, 512), 0)).astype(F32)
    m2 = (_iota((2048, 512), 0) >> 6) == (_iota((2048, 512), 1) >> 4)
    cd_re = jnp.where(m2, mmh(cr, t2), 0.0)
    cd_im = jnp.where(m2, mmh(ci, t2), 0.0)
    return a_re, a_im, bd_re, bd_im, cd_re, cd_im


_PREP_OUT = [(32, 64), (32, 64), (512, 2048), (512, 2048), (2048, 512), (2048, 512)]


def _s5_prep(params, cots=None):
    bwd = cots is not None

    def kern(*refs):
        vals = [r[...] for r in refs[:7]]
        if bwd:
            gs = tuple(r[...] for r in refs[7:13])
            _, vjp = jax.vjp(_s5_prep_fn, *vals)
            for o_ref, d in zip(refs[13:], vjp(gs)):
                o_ref[...] = d
        else:
            for o_ref, o in zip(refs[7:], _s5_prep_fn(*vals)):
                o_ref[...] = o

    if bwd:
        out_shape = [jax.ShapeDtypeStruct(p.shape, F32) for p in params]
        ops = list(params) + list(cots)
    else:
        out_shape = [jax.ShapeDtypeStruct(s, F32) for s in _PREP_OUT]
        ops = list(params)
    return pl.pallas_call(kern, name="s5_prep_bwd" if bwd else "s5_prep", out_shape=out_shape,
                          compiler_params=_cparams())(*ops)


def _cmul(ar, ai, br, bi):
    return ar * br - ai * bi, ar * bi + ai * br


def _power_table(ar, ai, row8, descending):
    pr, pi = ar, ai
    tr = jnp.zeros(row8.shape, F32)
    ti = jnp.zeros(row8.shape, F32)
    for n in range(8):
        r = 7 - n if descending else n
        tr = jnp.where(row8 == r, pr, tr)
        ti = jnp.where(row8 == r, pi, ti)
        if n < 7:
            pr, pi = _cmul(pr, pi, ar, ai)
    return tr, ti


def _tile_scan(xr, xi, pows, row8, up):
    for d, (pr, pi) in zip((1, 2, 4), pows):
        if up:
            sr = jnp.where(row8 < 8 - d, pltpu.roll(xr, 8 - d, 0), 0.0)
            si = jnp.where(row8 < 8 - d, pltpu.roll(xi, 8 - d, 0), 0.0)
        else:
            sr = jnp.where(row8 >= d, pltpu.roll(xr, d, 0), 0.0)
            si = jnp.where(row8 >= d, pltpu.roll(xi, d, 0), 0.0)
        mr, mi = _cmul(pr, pi, sr, si)
        xr, xi = xr + mr, xi + mi
    return xr, xi


def _pick_row(x, row8, r):
    return jnp.sum(jnp.where(row8 == r, x, 0.0), axis=0, keepdims=True)


SCAN_LB = 512
SCAN_TS = 512


def _scan_fwd(bu_re, bu_im, a_re, a_im):
    S, L = bu_re.shape
    ts, lb = min(SCAN_TS, S), SCAN_LB
    nt = ts // 8

    def kern(br_ref, bi_ref, ar_ref, ai_ref, or_ref, oi_ref, cr_ref, ci_ref):
        @pl.when(pl.program_id(1) == 0)
        def _():
            cr_ref[...] = jnp.zeros_like(cr_ref)
            ci_ref[...] = jnp.zeros_like(ci_ref)

        row8 = _iota((8, lb), 0)
        ar, ai = ar_ref[...], ai_ref[...]
        a2 = _cmul(ar, ai, ar, ai)
        a4 = _cmul(*a2, *a2)
        pows = ((ar, ai), a2, a4)
        tr, ti = _power_table(ar, ai, row8, False)

        def body(i, carry):
            cr, ci = carry
            off = pl.multiple_of(i * 8, 8)
            xr, xi = _tile_scan(br_ref[pl.ds(off, 8), :], bi_ref[pl.ds(off, 8), :], pows, row8, False)
            mr, mi = _cmul(tr, ti, cr, ci)
            xr, xi = xr + mr, xi + mi
            or_ref[pl.ds(off, 8), :] = xr
            oi_ref[pl.ds(off, 8), :] = xi
            return _pick_row(xr, row8, 7), _pick_row(xi, row8, 7)

        cr, ci = lax.fori_loop(0, nt, body, (cr_ref[...], ci_ref[...]))
        cr_ref[...] = cr
        ci_ref[...] = ci

    blk = pl.BlockSpec((ts, lb), lambda j, i: (i, j))
    par = pl.BlockSpec((1, lb), lambda j, i: (0, j))
    return pl.pallas_call(
        kern, name="s5_scan_fwd", grid=(L // lb, S // ts), in_specs=[blk, blk, par, par], out_specs=[blk, blk],
        out_shape=[jax.ShapeDtypeStruct((S, L), F32)] * 2,
        scratch_shapes=[pltpu.VMEM((1, lb), F32), pltpu.VMEM((1, lb), F32)],
        compiler_params=_cparams(("parallel", "arbitrary")))(bu_re, bu_im, a_re, a_im)


def _scan_bwd(dst_re, dst_im, st_re, st_im, a_re, a_im):
    S, L = dst_re.shape
    ts, lb = min(SCAN_TS, S), SCAN_LB
    nt = ts // 8
    nb = S // ts
    r8 = ts // 8

    def kern(dr_ref, di_ref, sr_ref, si_ref, pr_ref, pi_ref, ar_ref, ai_ref, gr_ref, gi_ref, dar_ref, dai_ref,
             cr_ref, ci_ref):
        step = pl.program_id(1)
        blk = nb - 1 - step

        @pl.when(step == 0)
        def _():
            cr_ref[...] = jnp.zeros_like(cr_ref)
            ci_ref[...] = jnp.zeros_like(ci_ref)
            dar_ref[...] = jnp.zeros_like(dar_ref)
            dai_ref[...] = jnp.zeros_like(dai_ref)

        row8 = _iota((8, lb), 0)
        ar, ai = ar_ref[...], ai_ref[...]
        nai = -ai
        a2 = _cmul(ar, nai, ar, nai)
        a4 = _cmul(*a2, *a2)
        pows = ((ar, nai), a2, a4)
        tr, ti = _power_table(ar, nai, row8, True)
        halo_r = jnp.where(blk == 0, 0.0, pr_ref[...])
        halo_i = jnp.where(blk == 0, 0.0, pi_ref[...])

        def body(n, carry):
            cr, ci, acc_r, acc_i = carry
            i = nt - 1 - n
            off = pl.multiple_of(i * 8, 8)
            gr, gi = _tile_scan(dr_ref[pl.ds(off, 8), :], di_ref[pl.ds(off, 8), :], pows, row8, True)
            mr, mi = _cmul(tr, ti, cr, ci)
            gr, gi = gr + mr, gi + mi
            gr_ref[pl.ds(off, 8), :] = gr
            gi_ref[pl.ds(off, 8), :] = gi
            poff = pl.multiple_of(jnp.maximum(i - 1, 0) * 8, 8)
            before_r = jnp.where(i == 0, halo_r, sr_ref[pl.ds(poff, 8), :])
            before_i = jnp.where(i == 0, halo_i, si_ref[pl.ds(poff, 8), :])
            last_r = _pick_row(before_r, row8, 7)
            last_i = _pick_row(before_i, row8, 7)
            spr = jnp.where(row8 >= 1, pltpu.roll(sr_ref[pl.ds(off, 8), :], 1, 0), last_r)
            spi = jnp.where(row8 >= 1, pltpu.roll(si_ref[pl.ds(off, 8), :], 1, 0), last_i)
            acc_r = acc_r + gr * spr + gi * spi
            acc_i = acc_i + gi * spr - gr * spi
            return _pick_row(gr, row8, 0), _pick_row(gi, row8, 0), acc_r, acc_i

        zero = jnp.zeros((8, lb), F32)
        cr, ci, acc_r, acc_i = lax.fori_loop(0, nt, body, (cr_ref[...], ci_ref[...], zero, zero))
        cr_ref[...] = cr
        ci_ref[...] = ci
        dar_ref[...] += jnp.sum(acc_r, axis=0, keepdims=True)
        dai_ref[...] += jnp.sum(acc_i, axis=0, keepdims=True)

    blk = pl.BlockSpec((ts, lb), lambda j, i: (nb - 1 - i, j))
    halo = pl.BlockSpec((8, lb), lambda j, i: (jnp.maximum((nb - 1 - i) * r8 - 1, 0), j))
    par = pl.BlockSpec((1, lb), lambda j, i: (0, j))
    return pl.pallas_call(
        kern, name="s5_scan_bwd", grid=(L // lb, nb), in_specs=[blk, blk, blk, blk, halo, halo, par, par],
        out_specs=[blk, blk, par, par],
        out_shape=[jax.ShapeDtypeStruct((S, L), F32)] * 2 + [jax.ShapeDtypeStruct((1, L), F32)] * 2,
        scratch_shapes=[pltpu.VMEM((1, lb), F32), pltpu.VMEM((1, lb), F32)],
        compiler_params=_cparams(("parallel", "arbitrary")))(dst_re, dst_im, st_re, st_im, st_re, st_im, a_re, a_im)


def _loss_grad(x, target, gain, ts=256):
    S, D = x.shape

    def kern(x_ref, t_ref, g_ref, loss_ref, dx_ref, dg_ref):
        i = pl.program_id(0)
        tgt = t_ref[...]

        def f(xv, gv):
            err = _rms(xv, gv) - tgt
            return 0.5 * jnp.mean(err * err, axis=-1, keepdims=True)

        rowloss, vjp = jax.vjp(f, x_ref[...], g_ref[...])
        dx, dg = vjp(jnp.ones_like(rowloss))
        dx_ref[...] = dx

        @pl.when(i == 0)
        def _():
            loss_ref[...] = jnp.zeros_like(loss_ref)
            dg_ref[...] = jnp.zeros_like(dg_ref)

        loss_ref[...] += jnp.broadcast_to(jnp.sum(rowloss, axis=0, keepdims=True), loss_ref.shape)
        dg_ref[...] += dg

    row = pl.BlockSpec((ts, D), lambda i: (i, 0))
    return pl.pallas_call(
        kern, name="loss_grad", grid=(S // ts,), in_specs=[row, row, pl.BlockSpec((1, D), lambda i: (0, 0))],
        out_specs=[pl.BlockSpec((8, 128), lambda i: (0, 0)), row, pl.BlockSpec((1, D), lambda i: (0, 0))],
        out_shape=[jax.ShapeDtypeStruct((8, 128), F32), jax.ShapeDtypeStruct((S, D), F32),
                   jax.ShapeDtypeStruct((1, D), F32)],
        compiler_params=_cparams(("arbitrary",)))(x, target, gain)


def _rms_fwd(x, g, name):
    return _rowwise(_rms_fn, [_blk(x)], [g], [x.shape[1]], name)[0]


def _rms_bwd(x, g, dy, name, add=None):
    return _rowwise_bwd(_rms_fn, [_blk(x)], [g], [dy], name, adds=None if add is None else {0: add})


FFN_TC = 1408


def _common_fwd(x, mem, P, L):
    hx = _rms_fwd(x, P['xa_norm'], L + "xa_norm")
    q = _matmul(hx, P['xa_wq'], name=L + "xa_q")
    memn = _rms_fwd(mem, P['mem_norm'], L + "mem_norm")
    kv = _matmul(memn, P['xa_wkv'], name=L + "xa_kv")
    att = _rowwise(_xattn_fn, [_blk(q)], [kv], [1024], L + "xattn")[0]
    x2 = _matmul(att, P['xa_wo'], res=x, name=L + "xa_o")
    hf = _rms_fwd(x2, P['ffn_norm'], L + "ffn_norm")
    hu = _matmul(hf, P['ffn_w_up'], name=L + "ffn_up")
    cw = P['ffn_conv']
    act = _conv_post([(hu, 0, cw, 0), (hu, 2, cw, 2)], _ffn_post, 2, FFN_TC, L + "ffn_conv")
    x3 = _matmul(act, P['ffn_w_down'], res=x2, name=L + "ffn_down")
    return x3, (x, mem, hx, q, memn, kv, att, x2, hf, hu, act)


def _common_bwd(saved, dx3, P, L):
    x, mem, hx, q, memn, kv, att, x2, hf, hu, act = saved
    G = {}
    dact = _matmul(dx3, P['ffn_w_down'], "nt", name=L + "ffn_down_dx")
    G['ffn_w_down'] = _matmul(act, dx3, "tn", name=L + "ffn_down_dw")
    cw = P['ffn_conv']
    dcu, dcg = _conv_post([(hu, 0, cw, 0), (hu, 2, cw, 2)], _ffn_post, 2, FFN_TC, L + "ffn_conv_dpost", cots=dact)
    dhu_u, dcw_u = _conv_bwd(dcu, hu, 0, cw, 0, 2, FFN_TC, L + "ffn_conv_bwd_up")
    dhu_g, dcw_g = _conv_bwd(dcg, hu, 2, cw, 2, 2, FFN_TC, L + "ffn_conv_bwd_gate")
    G['ffn_conv'] = jnp.concatenate([dcw_u, dcw_g], axis=1)
    dhf = _matmul_cat([dhu_u, dhu_g], P['ffn_w_up'], "nt", name=L + "ffn_up_dx")
    G['ffn_w_up'] = jnp.concatenate([_matmul(hf, dhu_u, "tn", name=L + "ffn_up_dw_up"),
                                     _matmul(hf, dhu_g, "tn", name=L + "ffn_up_dw_gate")], axis=1)
    dx2, G['ffn_norm'] = _rms_bwd(x2, P['ffn_norm'], dhf, L + "ffn_norm_bwd", add=dx3)
    datt = _matmul(dx2, P['xa_wo'], "nt", name=L + "xa_o_dx")
    G['xa_wo'] = _matmul(att, dx2, "tn", name=L + "xa_o_dw")
    dq, dkv = _rowwise_bwd(_xattn_fn, [_blk(q)], [kv], [datt], L + "xattn_bwd")
    dhx = _matmul(dq, P['xa_wq'], "nt", name=L + "xa_q_dx")
    G['xa_wq'] = _matmul(hx, dq, "tn", name=L + "xa_q_dw")
    dmemn = _matmul(dkv, P['xa_wkv'], "nt", name=L + "xa_kv_dx")
    G['xa_wkv'] = _matmul(memn, dkv, "tn", name=L + "xa_kv_dw")
    _, G['mem_norm'] = _rms_bwd(mem, P['mem_norm'], dmemn, L + "mem_norm_bwd")
    dx, G['xa_norm'] = _rms_bwd(x, P['xa_norm'], dhx, L + "xa_norm_bwd", add=dx2)
    return dx, G


U_COLS = (2048, 512)


def _even_fwd(x, P):
    S = x.shape[0]
    h0 = _rms_fwd(x, P['mix_norm'], "l0_mix_norm")
    proj = _matmul(h0, P['w_in'], name="l0_in")
    tabs = _ret_tables(S)
    o_raw, rstates = _ret_fwd(proj, tabs)
    o = _rowwise(_ret_post_fn, [_blk(o_raw), _blk(proj, 512, 3)], [P['ret_norm']], [512], "l0_ret_post")[0]
    prep_in = (P['s5_lambda_re'], P['s5_lambda_im'], P['s5_log_dt'], P['s5_b_re'], P['s5_b_im'], P['s5_c_re'],
               P['s5_c_im'])
    a_re, a_im, bd_re, bd_im, cd_re, cd_im = _s5_prep(prep_in)
    a_re_f, a_im_f = a_re.reshape(1, 2048), a_im.reshape(1, 2048)
    bu_re = _matmul(proj, bd_re, name="l0_s5_bu_re", a_cols=U_COLS)
    bu_im = _matmul(proj, bd_im, name="l0_s5_bu_im", a_cols=U_COLS)
    st_re, st_im = _scan_fwd(bu_re, bu_im, a_re_f, a_im_f)
    y1 = _matmul(st_re, cd_re, name="l0_s5_y_re")
    y2 = _matmul(st_im, cd_im, name="l0_s5_y_im")
    yg = _rowwise(_s5_post_fn, [_blk(y1), _blk(y2), _blk(proj, 512, 4)],
                  [P['s5_d'], P['s5_w_glu'], P['s5_b_glu']], [512], "l0_s5_post")[0]
    x1 = _matmul_cat([o, yg], P['w_out'], "nn", res=x, name="l0_out")
    saved = (x, h0, proj, tabs, o_raw, rstates, prep_in, a_re_f, a_im_f, bd_re, bd_im, cd_re, cd_im, st_re, st_im,
             y1, y2, o, yg)
    return x1, saved


def _even_bwd(saved, dx1, P):
    (x, h0, proj, tabs, o_raw, rstates, prep_in, a_re_f, a_im_f, bd_re, bd_im, cd_re, cd_im, st_re, st_im, y1, y2,
     o, yg) = saved
    G = {}
    dmerged = _matmul(dx1, P['w_out'], "nt", name="l0_out_dx")
    G['w_out'] = jnp.concatenate([_matmul(o, dx1, "tn", name="l0_out_dw_ret"),
                                  _matmul(yg, dx1, "tn", name="l0_out_dw_s5")], axis=0)
    do_raw, dgate, G['ret_norm'] = _rowwise_bwd(
        _ret_post_fn, [_blk(o_raw), _blk(proj, 512, 3)], [P['ret_norm']], [_blk(dmerged, 512, 0)], "l0_ret_post_bwd")
    dq, dk, dv = _ret_bwd(proj, tabs, rstates, do_raw)
    dy1, dy2, du_a, G['s5_d'], G['s5_w_glu'], G['s5_b_glu'] = _rowwise_bwd(
        _s5_post_fn, [_blk(y1), _blk(y2), _blk(proj, 512, 4)], [P['s5_d'], P['s5_w_glu'], P['s5_b_glu']],
        [_blk(dmerged, 512, 1)], "l0_s5_post_bwd")
    dst_re = _matmul(dy1, cd_re, "nt", name="l0_s5_y_re_dx")
    dcd_re = _matmul(st_re, dy1, "tn", name="l0_s5_y_re_dw")
    dst_im = _matmul(dy2, cd_im, "nt", name="l0_s5_y_im_dx")
    dcd_im = _matmul(st_im, dy2, "tn", name="l0_s5_y_im_dw")
    dbu_re, dbu_im, da_re, da_im = _scan_bwd(dst_re, dst_im, st_re, st_im, a_re_f, a_im_f)
    du = _matmul(dbu_re, bd_re, "nt", res=du_a, name="l0_s5_bu_re_dx")
    du = _matmul(dbu_im, bd_im, "nt", res=du, name="l0_s5_bu_im_dx")
    dbd_re = _matmul(proj, dbu_re, "tn", name="l0_s5_bu_re_dw", a_cols=U_COLS)
    dbd_im = _matmul(proj, dbu_im, "tn", name="l0_s5_bu_im_dw", a_cols=U_COLS)
    dprep = _s5_prep(prep_in, cots=(da_re.reshape(32, 64), da_im.reshape(32, 64), dbd_re, dbd_im, dcd_re, dcd_im))
    for n, d in zip(('s5_lambda_re', 's5_lambda_im', 's5_log_dt', 's5_b_re', 's5_b_im', 's5_c_re', 's5_c_im'), dprep):
        G[n] = d
    pieces = [dq, dk, dv, dgate, du]
    dh0 = _matmul_cat(pieces, P['w_in'], "nt", name="l0_in_dx")
    G['w_in'] = jnp.concatenate([_matmul(h0, p, "tn", name="l0_in_dw_%d" % n) for n, p in enumerate(pieces)], axis=1)
    dx, G['mix_norm'] = _rms_bwd(x, P['mix_norm'], dh0, "l0_mix_norm_bwd", add=dx1)
    return dx, G


def _odd_fwd(x, P):
    h1 = _rms_fwd(x, P['mix_norm'], "l1_mix_norm")
    pm = _matmul(h1, P['w_main'], name="l1_in_main")
    pt = _matmul(h1, P['w_tail'], name="l1_in_tail")
    qkv = _conv_post([(pm, 0, P['conv'], 0)], _silu, 3, 1024, "l1_conv")
    g_e, beta_e = _rowwise(_gdn_gates_fn, [_blk(pt)], [P['a_log_p'], P['dtb_p']], [1024, 1024], "l1_gdn_gates")
    w, u, qd, kd, qk = _gdn_intra_call(qkv, g_e, beta_e)
    o_raw, gstates = _gdn_pass(w, u, qd, kd, qk, g_e)
    og = _rowwise(_gdn_post_fn, [_blk(o_raw), _blk(pm, 1024, 3)], [P['o_norm']], [1024], "l1_gdn_post")[0]
    x1 = _matmul(og, P['w_out'], res=x, name="l1_out")
    return x1, (x, h1, pm, pt, qkv, g_e, beta_e, w, u, qd, kd, qk, o_raw, gstates, og)


def _odd_bwd(saved, dx1, P):
    x, h1, pm, pt, qkv, g_e, beta_e, w, u, qd, kd, qk, o_raw, gstates, og = saved
    G = {}
    dog = _matmul(dx1, P['w_out'], "nt", name="l1_out_dx")
    G['w_out'] = _matmul(og, dx1, "tn", name="l1_out_dw")
    do_raw, dz, G['o_norm'] = _rowwise_bwd(_gdn_post_fn, [_blk(o_raw), _blk(pm, 1024, 3)], [P['o_norm']], [dog],
                                           "l1_gdn_post_bwd")
    dw, du, dqd, dkd, dqk, dg_pass = _gdn_pass(w, u, qd, kd, qk, g_e, states=gstates, do=do_raw)
    dqkv = _gdn_intra_call(qkv, g_e, beta_e, cots=(dw, du, dqd, dkd, dqk, dg_pass))
    dg_e, dbeta_e = dqkv[3], dqkv[4]
    dpt, G['a_log_p'], G['dtb_p'] = _rowwise_bwd(_gdn_gates_fn, [_blk(pt)], [P['a_log_p'], P['dtb_p']],
                                                 [dg_e, dbeta_e], "l1_gdn_gates_bwd")
    pieces, dcw = [], []
    for part in range(3):
        (dc,) = _conv_post([(pm, part, P['conv'], part)], _silu, 1, 1024, "l1_conv_dpost_%d" % part, cots=dqkv[part])
        dxp, dwp = _conv_bwd(dc, pm, part, P['conv'], part, 1, 1024, "l1_conv_bwd_%d" % part)
        pieces.append(dxp)
        dcw.append(dwp)
    G['conv'] = jnp.concatenate(dcw, axis=1)
    pieces += [dz, dpt]
    dh1 = _matmul_cat(pieces, P['w_all'], "nt", name="l1_in_dx")
    G['w_all'] = jnp.concatenate([_matmul(h1, p, "tn", name="l1_in_dw_%d" % n) for n, p in enumerate(pieces)], axis=1)
    dx, G['mix_norm'] = _rms_bwd(x, P['mix_norm'], dh1, "l1_mix_norm_bwd", add=dx1)
    return dx, G


def _row(v):
    return v.reshape(1, -1)


def _local_step(x, mem, target, W):
    P0 = {
        'mix_norm': _row(W['l0_mix_norm']), 'w_in': W['l0_w_in'], 'ret_norm': _row(W['l0_ret_norm']),
        's5_lambda_re': W['l0_s5_lambda_re'], 's5_lambda_im': W['l0_s5_lambda_im'],
        's5_log_dt': W['l0_s5_log_dt'].reshape(32, 1),
        's5_b_re': W['l0_s5_b_re'].reshape(512, 64), 's5_b_im': W['l0_s5_b_im'].reshape(512, 64),
        's5_c_re': W['l0_s5_c_re'].reshape(2048, 16), 's5_c_im': W['l0_s5_c_im'].reshape(2048, 16),
        's5_d': _row(W['l0_s5_d']), 's5_w_glu': W['l0_s5_w_glu'].astype(F32), 's5_b_glu': _row(W['l0_s5_b_glu']),
        'w_out': W['l0_w_out'],
    }
    w_in1 = W['l1_w_in']
    pad8 = jnp.zeros((8,), F32)
    w_all = jnp.pad(w_in1, ((0, 0), (0, 112)))
    P1 = {
        'mix_norm': _row(W['l1_mix_norm']), 'w_main': w_in1[:, :4096], 'w_tail': w_all[:, 4096:], 'w_all': w_all,
        'conv': W['l1_conv'],
        'a_log_p': _row(jnp.concatenate([pad8, W['l1_a_log'], jnp.zeros((112,), F32)])),
        'dtb_p': _row(jnp.concatenate([pad8, W['l1_dt_bias'], jnp.zeros((112,), F32)])),
        'o_norm': _row(W['l1_o_norm']), 'w_out': W['l1_w_out'],
    }
    C = []
    for L in ('l0_', 'l1_'):
        C.append({'xa_norm': _row(W[L + 'xa_norm']), 'mem_norm': _row(W[L + 'mem_norm']), 'xa_wq': W[L + 'xa_wq'],
                  'xa_wkv': W[L + 'xa_wkv'], 'xa_wo': W[L + 'xa_wo'], 'ffn_norm': _row(W[L + 'ffn_norm']),
                  'ffn_w_up': W[L + 'ffn_w_up'], 'ffn_conv': W[L + 'ffn_conv'], 'ffn_w_down': W[L + 'ffn_w_down']})

    x1, s_even = _even_fwd(x, P0)
    x3, s_c0 = _common_fwd(x1, mem, C[0], "l0_")
    x4, s_odd = _odd_fwd(x3, P1)
    x6, s_c1 = _common_fwd(x4, mem, C[1], "l1_")
    loss_tile, dx6, d_final = _loss_grad(x6, target, _row(W['final_norm']))

    G = {'final_norm': d_final.reshape(-1)}
    dx4, g = _common_bwd(s_c1, dx6, C[1], "l1_")
    for k, v in g.items():
        G['l1_' + k] = v
    dx3, g = _odd_bwd(s_odd, dx4, P1)
    G['l1_mix_norm'] = g['mix_norm']
    G['l1_w_in'] = g['w_all'][:, :4112]
    G['l1_conv'] = g['conv']
    G['l1_a_log'] = g['a_log_p'][0, 8:16]
    G['l1_dt_bias'] = g['dtb_p'][0, 8:16]
    G['l1_o_norm'] = g['o_norm']
    G['l1_w_out'] = g['w_out']
    dx1, g = _common_bwd(s_c0, dx3, C[0], "l0_")
    for k, v in g.items():
        G['l0_' + k] = v
    dx0, g = _even_bwd(s_even, dx1, P0)
    for k, v in g.items():
        G['l0_' + k] = v
    return loss_tile, dx0, G


ANY = pl.BlockSpec(memory_space=pl.ANY)


def _place():
    return lax.axis_index("x"), lax.axis_index("y"), lax.axis_index("c")


def _chip_peers(x, y):
    return [(1 - x, y), (x, 1 - y), (1 - x, 1 - y)]


def _gather_shards(shards, modes):
    n = len(shards)
    shapes = [s.shape for s in shards]
    out_shape = []
    for s, m in zip(shards, modes):
        full = {'row': (4 * s.shape[0], s.shape[1]), 'col': (s.shape[0], 4 * s.shape[1]), 'slab': (4,) + s.shape}[m]
        out_shape.append(jax.ShapeDtypeStruct(full, s.dtype))

    def body(*refs):
        ins, outs = refs[:n], refs[n:2 * n]
        send_sems, recv_sems, local_sems = refs[2 * n:]
        x, y, c = _place()
        peers = _chip_peers(x, y)

        def dst(a, slab):
            r, w = shapes[a]
            if modes[a] == 'row':
                return outs[a].at[pl.ds(slab * r, r), :]
            if modes[a] == 'col':
                return outs[a].at[:, pl.ds(slab * w, w)]
            return outs[a].at[slab]

        def copy(a, k, slab, to):
            return pltpu.make_async_remote_copy(src_ref=ins[a], dst_ref=dst(a, slab), send_sem=send_sems.at[3 * a + k],
                                                recv_sem=recv_sems.at[3 * a + k], device_id=(to[0], to[1], c),
                                                device_id_type=MESH)

        me = 2 * x + y
        own = [pltpu.make_async_copy(ins[a], dst(a, me), local_sems.at[a]) for a in range(n)]
        sends = [copy(a, k, me, p) for a in range(n) for k, p in enumerate(peers)]
        for cp in own + sends:
            cp.start()
        for a in range(n):
            for k, p in enumerate(peers):
                copy(a, k, 2 * p[0] + p[1], p).wait_recv()
        for cp in sends:
            cp.wait_send()
        for cp in own:
            cp.wait()

    return pl.pallas_call(
        body, name="gather_weights", in_specs=[ANY] * n, out_specs=[ANY] * n, out_shape=out_shape,
        scratch_shapes=[pltpu.SemaphoreType.DMA((3 * n,)), pltpu.SemaphoreType.DMA((3 * n,)),
                        pltpu.SemaphoreType.DMA((n,))],
    )(*shards)


def _send_other_half(gs):
    n = len(gs)

    def body(*refs):
        ins, outs = refs[:n], refs[n:2 * n]
        send_sems, recv_sems = refs[2 * n:]
        x, y, c = _place()
        cps = []
        for a in range(n):
            rh = gs[a].shape[1] // 2
            cps.append(pltpu.make_async_remote_copy(
                src_ref=ins[a].at[:, pl.ds((1 - c) * rh, rh), :], dst_ref=outs[a], send_sem=send_sems.at[a],
                recv_sem=recv_sems.at[a], device_id=(x, y, 1 - c), device_id_type=MESH))
        for cp in cps:
            cp.start()
        for cp in cps:
            cp.wait()

    return pl.pallas_call(
        body, name="send_other_half", in_specs=[ANY] * n, out_specs=[ANY] * n,
        out_shape=[jax.ShapeDtypeStruct((4, g.shape[1] // 2, g.shape[2]), g.dtype) for g in gs],
        scratch_shapes=[pltpu.SemaphoreType.DMA((n,)), pltpu.SemaphoreType.DMA((n,))],
    )(*gs)


def _send_to_chips(ps):
    n = len(ps)

    def body(*refs):
        ins, outs = refs[:n], refs[n:2 * n]
        send_sems, recv_sems, local_sems = refs[2 * n:]
        x, y, c = _place()
        peers = _chip_peers(x, y)
        me = 2 * x + y

        def copy(a, k, src_slab, dst_slab, to):
            return pltpu.make_async_remote_copy(src_ref=ins[a].at[src_slab], dst_ref=outs[a].at[dst_slab],
                                                send_sem=send_sems.at[3 * a + k], recv_sem=recv_sems.at[3 * a + k],
                                                device_id=(to[0], to[1], c), device_id_type=MESH)

        own = [pltpu.make_async_copy(ins[a].at[me], outs[a].at[me], local_sems.at[a]) for a in range(n)]
        sends = [copy(a, k, 2 * p[0] + p[1], me, p) for a in range(n) for k, p in enumerate(peers)]
        for cp in own + sends:
            cp.start()
        for a in range(n):
            for k, p in enumerate(peers):
                copy(a, k, me, 2 * p[0] + p[1], p).wait_recv()
        for cp in sends:
            cp.wait_send()
        for cp in own:
            cp.wait()

    return pl.pallas_call(
        body, name="send_to_chips", in_specs=[ANY] * n, out_specs=[ANY] * n,
        out_shape=[jax.ShapeDtypeStruct(p.shape, p.dtype) for p in ps],
        scratch_shapes=[pltpu.SemaphoreType.DMA((3 * n,)), pltpu.SemaphoreType.DMA((3 * n,)),
                        pltpu.SemaphoreType.DMA((n,))],
    )(*ps)


def _share_halves(hs):
    n = len(hs)

    def body(*refs):
        ins, outs = refs[:n], refs[n:2 * n]
        send_sems, recv_sems, local_sems = refs[2 * n:]
        x, y, c = _place()
        own, sends, waits = [], [], []
        for a in range(n):
            rh = hs[a].shape[0]
            mine = outs[a].at[pl.ds(c * rh, rh), :]
            other = outs[a].at[pl.ds((1 - c) * rh, rh), :]
            own.append(pltpu.make_async_copy(ins[a], mine, local_sems.at[a]))
            sends.append(pltpu.make_async_remote_copy(src_ref=ins[a], dst_ref=mine, send_sem=send_sems.at[a],
                                                      recv_sem=recv_sems.at[a], device_id=(x, y, 1 - c),
                                                      device_id_type=MESH))
            waits.append(pltpu.make_async_remote_copy(src_ref=ins[a], dst_ref=other, send_sem=send_sems.at[a],
                                                      recv_sem=recv_sems.at[a], device_id=(x, y, 1 - c),
                                                      device_id_type=MESH))
        for cp in own + sends:
            cp.start()
        for cp in waits:
            cp.wait()
        for cp in own:
            cp.wait()

    return pl.pallas_call(
        body, name="share_halves", in_specs=[ANY] * n, out_specs=[ANY] * n,
        out_shape=[jax.ShapeDtypeStruct((2 * h.shape[0], h.shape[1]), h.dtype) for h in hs],
        scratch_shapes=[pltpu.SemaphoreType.DMA((n,)), pltpu.SemaphoreType.DMA((n,)), pltpu.SemaphoreType.DMA((n,))],
    )(*hs)


def _gather_all(mine):
    flips = [(dx, dy, dc) for dx in (0, 1) for dy in (0, 1) for dc in (0, 1) if (dx, dy, dc) != (0, 0, 0)]

    def body(x_ref, out_ref, send_sems, recv_sems, local_sem):
        x, y, c = _place()
        me = 4 * x + 2 * y + c

        def peer(f):
            return (x ^ f[0], y ^ f[1], c ^ f[2])

        def copy(k, slab, to):
            return pltpu.make_async_remote_copy(src_ref=x_ref, dst_ref=out_ref.at[slab], send_sem=send_sems.at[k],
                                                recv_sem=recv_sems.at[k], device_id=to, device_id_type=MESH)

        own = pltpu.make_async_copy(x_ref, out_ref.at[me], local_sem)
        own.start()
        sends = [copy(k, me, peer(f)) for k, f in enumerate(flips)]
        for s in sends:
            s.start()
        for k, f in enumerate(flips):
            p = peer(f)
            copy(k, 4 * p[0] + 2 * p[1] + p[2], p).wait_recv()
        for s in sends:
            s.wait_send()
        own.wait()

    return pl.pallas_call(
        body, name="gather_all", in_specs=[ANY], out_specs=ANY,
        out_shape=jax.ShapeDtypeStruct((8,) + mine.shape, mine.dtype),
        scratch_shapes=[pltpu.SemaphoreType.DMA((7,)), pltpu.SemaphoreType.DMA((7,)), pltpu.SemaphoreType.DMA],
    )(mine)


def _row_tile(rows):
    return _pick(rows, (512, 352, 256, 128, 64, 32, 16, 8))


def _pair_sum(g, got, c_arr, name):
    _, r, w = g.shape
    rh = r // 2
    tr = _row_tile(rh)
    nb = rh // tr

    def kern(c_ref, g_ref, o_ref, out_ref):
        out_ref[...] = (g_ref[...] + o_ref[...]).astype(BF16)

    grid_spec = pltpu.PrefetchScalarGridSpec(
        num_scalar_prefetch=1, grid=(4, nb),
        in_specs=[pl.BlockSpec((None, tr, w), lambda j, i, c_ref: (j, c_ref[0] * nb + i, 0)),
                  pl.BlockSpec((None, tr, w), lambda j, i, c_ref: (j, i, 0))],
        out_specs=pl.BlockSpec((None, tr, w), lambda j, i, c_ref: (j, i, 0)))
    return pl.pallas_call(kern, name=name, grid_spec=grid_spec, out_shape=jax.ShapeDtypeStruct((4, rh, w), BF16),
                          compiler_params=_cparams(("parallel", "parallel")))(c_arr, g, got)


def _slab_sum(slabs, name):
    n, R, w = slabs.shape
    tr = _row_tile(R)

    def kern(s_ref, o_ref):
        acc = s_ref[0].astype(F32)
        for k in range(1, n):
            acc = acc + s_ref[k].astype(F32)
        o_ref[...] = acc

    return pl.pallas_call(
        kern, name=name, grid=(R // tr,), in_specs=[pl.BlockSpec((n, tr, w), lambda i: (0, i, 0))],
        out_specs=pl.BlockSpec((tr, w), lambda i: (i, 0)), out_shape=jax.ShapeDtypeStruct((R, w), F32),
        compiler_params=_cparams(("parallel",)))(slabs)


def _adamw(w, g, m, v, name):
    R, C = w.shape
    tr = _pick(R, (256, 128, 64, 32, 16, 8))

    def kern(w_ref, g_ref, m_ref, v_ref, d_ref, nm_ref, nv_ref):
        gv = g_ref[...]
        m2 = ADAM_B1 * m_ref[...] + (1.0 - ADAM_B1) * gv
        v2 = ADAM_B2 * v_ref[...] + (1.0 - ADAM_B2) * jnp.square(gv)
        m_hat = m2 / (1.0 - ADAM_B1 ** ADAM_STEP)
        v_hat = v2 / (1.0 - ADAM_B2 ** ADAM_STEP)
        d_ref[...] = -ADAM_LR * (m_hat / (jnp.sqrt(v_hat) + ADAM_EPS) + ADAM_WD * w_ref[...])
        nm_ref[...] = m2
        nv_ref[...] = v2

    spec = pl.BlockSpec((tr, C), lambda i: (i, 0))
    return pl.pallas_call(
        kern, name=name, grid=(R // tr,), in_specs=[spec] * 4, out_specs=[spec] * 3,
        out_shape=[jax.ShapeDtypeStruct((R, C), F32)] * 3, compiler_params=_cparams(("parallel",)))(w, g, m, v)


def _pack_small(vals):
    flat = jnp.concatenate([vals[n].astype(F32).reshape(-1) for n in SMALL_NAMES])
    rows = -(-flat.shape[0] // (8 * LANES)) * 8
    return jnp.pad(flat, (0, rows * LANES - flat.shape[0])).reshape(rows, LANES)


def _unpack_small(packed, shapes):
    flat = packed.reshape(-1)
    out = {}
    off = 0
    for n in SMALL_NAMES:
        size = int(np.prod(shapes[n]))
        out[n] = flat[off:off + size].reshape(shapes[n])
        off += size
    return out


def _gather_weights(A):
    names = MATRIX_NAMES + list(CONVS)
    shards, modes = [], []
    for n in MATRIX_NAMES:
        shards.append(A[n].astype(BF16))
        modes.append('slab' if n == 'l1_w_in' else ('row' if MATRICES[n] == 0 else 'col'))
    for n in CONVS:
        shards.append(A[n])
        modes.append('slab')
    outs = _gather_shards(shards, modes)
    W = {}
    for n, m, o in zip(names, modes, outs):
        W[n] = jnp.concatenate([o[j] for j in range(4)], axis=1) if m == 'slab' else o
    return W


def _reduce_matrix_grads(G):
    c = lax.axis_index("c")
    c_arr = jnp.reshape(c, (1,)).astype(jnp.int32)
    gs = []
    for n in MATRIX_NAMES:
        g = G[n]
        if MATRICES[n] == 0:
            gs.append(g.reshape(4, g.shape[0] // 4, g.shape[1]))
        else:
            gs.append(g.reshape(g.shape[0], 4, g.shape[1] // 4).transpose(1, 0, 2))
    got = _send_other_half(gs)
    pairs = [_pair_sum(g, o, c_arr, "pair_sum_" + n) for n, g, o in zip(MATRIX_NAMES, gs, got)]
    recv = _send_to_chips(pairs)
    halves = [_slab_sum(r, "chip_sum_" + n) for n, r in zip(MATRIX_NAMES, recv)]
    return dict(zip(MATRIX_NAMES, _share_halves(halves)))


def kernel(*args):
    A = dict(zip(ARG_NAMES, args, strict=True))
    x, mem, target = A['x'][0], A['mem'][0], A['loss_target'][0]

    W = _gather_weights(A)
    for n in SMALL_NAMES:
        if n not in CONVS:
            W[n] = A[n]

    loss_tile, grad_x, G = _local_step(x, mem, target, W)
    loss = lax.psum(loss_tile[0, 0], ("x", "y", "c"))

    g_mat = _reduce_matrix_grads(G)

    g_small = _slab_sum(_gather_all(_pack_small({n: G[n] for n in SMALL_NAMES})), "sum_small")
    g_small = _unpack_small(g_small, {n: G[n].shape for n in SMALL_NAMES})
    me = 2 * lax.axis_index("x") + lax.axis_index("y")
    for n in CONVS:
        wd = A[n].shape[1]
        g_small[n] = lax.dynamic_slice_in_dim(g_small[n], me * wd, wd, axis=1)
    flat_names = [n for n in SMALL_NAMES if n not in CONVS]

    def pack_flat(prefix):
        return _pack_small_flat({n: A[prefix + n] for n in flat_names}, flat_names)

    shapes = {n: A[n].shape for n in flat_names}
    d_s, m_s, v_s = _adamw(pack_flat(''), _pack_small_flat(g_small, flat_names), pack_flat('m_'), pack_flat('v_'),
                           "adamw_small")
    d_s, m_s, v_s = (_unpack_flat(p, shapes, flat_names) for p in (d_s, m_s, v_s))

    grads, deltas, new_m, new_v = {}, {}, {}, {}
    for n in WEIGHTS:
        if n in MATRICES or n in CONVS:
            grads[n] = g_mat[n] if n in MATRICES else g_small[n]
            deltas[n], new_m[n], new_v[n] = _adamw(A[n], grads[n], A['m_' + n], A['v_' + n], "adamw_" + n)
        else:
            grads[n] = g_small[n].reshape(A[n].shape)
            deltas[n], new_m[n], new_v[n] = d_s[n], m_s[n], v_s[n]
    return (loss, grad_x[None], *[grads[n] for n in WEIGHTS], *[deltas[n] for n in WEIGHTS],
            *[new_m[n] for n in WEIGHTS], *[new_v[n] for n in WEIGHTS])


def _pack_small_flat(vals, names):
    flat = jnp.concatenate([vals[n].astype(F32).reshape(-1) for n in names])
    rows = -(-flat.shape[0] // (8 * LANES)) * 8
    return jnp.pad(flat, (0, rows * LANES - flat.shape[0])).reshape(rows, LANES)


def _unpack_flat(packed, shapes, names):
    flat = packed.reshape(-1)
    out = {}
    off = 0
    for n in names:
        size = int(np.prod(shapes[n]))
        out[n] = flat[off:off + size].reshape(shapes[n])
        off += size
    return out
```

```python
import functools
import math

import numpy as np
import jax
import jax.numpy as jnp
from jax import lax
from jax.experimental import pallas as pl
from jax.experimental.pallas import tpu as pltpu

F32 = jnp.float32
BF16 = jnp.bfloat16
EPS = 1e-6
MESH = pl.DeviceIdType.MESH

ADAM_LR = 0.001
ADAM_B1 = 0.9
ADAM_B2 = 0.999
ADAM_EPS = 1e-08
ADAM_WD = 0.01
ADAM_STEP = 10

VMEM_LIMIT_BYTES = 56 * 1024 * 1024
LANES = 1024

WEIGHTS = ['l0_mix_norm', 'l0_w_in', 'l0_ret_norm', 'l0_s5_lambda_re', 'l0_s5_lambda_im', 'l0_s5_b_re', 'l0_s5_b_im',
           'l0_s5_c_re', 'l0_s5_c_im', 'l0_s5_d', 'l0_s5_log_dt', 'l0_s5_w_glu', 'l0_s5_b_glu', 'l0_w_out',
           'l0_xa_norm', 'l0_mem_norm', 'l0_xa_wq', 'l0_xa_wkv', 'l0_xa_wo', 'l0_ffn_norm', 'l0_ffn_w_up',
           'l0_ffn_conv', 'l0_ffn_w_down', 'l1_mix_norm', 'l1_w_in', 'l1_conv', 'l1_a_log', 'l1_dt_bias',
           'l1_o_norm', 'l1_w_out', 'l1_xa_norm', 'l1_mem_norm', 'l1_xa_wq', 'l1_xa_wkv', 'l1_xa_wo',
           'l1_ffn_norm', 'l1_ffn_w_up', 'l1_ffn_conv', 'l1_ffn_w_down', 'final_norm']
ARG_NAMES = (['x', 'mem'] + WEIGHTS + ['loss_target'] + ['m_' + w for w in WEIGHTS] + ['v_' + w for w in WEIGHTS])

MATRICES = {
    'l0_w_in': 1, 'l0_s5_w_glu': 0, 'l0_w_out': 0, 'l0_xa_wq': 0, 'l0_xa_wkv': 1, 'l0_xa_wo': 0, 'l0_ffn_w_up': 1,
    'l0_ffn_w_down': 0, 'l1_w_in': 1, 'l1_w_out': 0, 'l1_xa_wq': 0, 'l1_xa_wkv': 1, 'l1_xa_wo': 0,
    'l1_ffn_w_up': 1, 'l1_ffn_w_down': 0,
}
CONVS = ('l0_ffn_conv', 'l1_conv', 'l1_ffn_conv')
MATRIX_NAMES = [w for w in WEIGHTS if w in MATRICES]
SMALL_NAMES = [w for w in WEIGHTS if w not in MATRICES]


def _cparams(sem=None):
    return pltpu.CompilerParams(dimension_semantics=sem, vmem_limit_bytes=VMEM_LIMIT_BYTES)


def _pick(n, cands):
    for c in cands:
        if n % c == 0:
            return c
    return n


_NN = ((1,), (0,))
_NT = ((1,), (1,))
_TN = ((0,), (0,))


def _dot(a, b, dims, hi):
    if hi is not None:
        return lax.dot_general(a.astype(F32), b.astype(F32), (dims, ((), ())), precision=hi,
                               preferred_element_type=F32)
    return lax.dot_general(a.astype(BF16), b.astype(BF16), (dims, ((), ())), preferred_element_type=F32)


def _make_mm(hi):
    @jax.custom_vjp
    def nn(a, b):
        return _dot(a, b, _NN, hi)

    def nn_f(a, b):
        return nn(a, b), (a, b)

    def nn_b(r, g):
        a, b = r
        return _dot(g, b, _NT, hi), _dot(a, g, _TN, hi)

    nn.defvjp(nn_f, nn_b)

    @jax.custom_vjp
    def nt(a, b):
        return _dot(a, b, _NT, hi)

    def nt_f(a, b):
        return nt(a, b), (a, b)

    def nt_b(r, g):
        a, b = r
        return _dot(g, b, _NN, hi), _dot(g, a, _TN, hi)

    nt.defvjp(nt_f, nt_b)

    @jax.custom_vjp
    def tn(a, b):
        return _dot(a, b, _TN, hi)

    def tn_f(a, b):
        return tn(a, b), (a, b)

    def tn_b(r, g):
        a, b = r
        return _dot(b, g, _NT, hi), _dot(a, g, _NN, hi)

    tn.defvjp(tn_f, tn_b)
    return nn, nt, tn


mm, mm_nt, mm_tn = _make_mm(None)
mmh, mmh_nt, mmh_tn = _make_mm(lax.Precision.HIGHEST)
mm3, _, _ = _make_mm(lax.Precision.HIGH)


@jax.custom_vjp
def _swap_halves(x):
    return pltpu.roll(x, 64, 1)


def _swap_f(x):
    return pltpu.roll(x, 64, 1), None


def _swap_b(_, g):
    return (pltpu.roll(g, 64, 1),)


_swap_halves.defvjp(_swap_f, _swap_b)


def _silu(x):
    return x * jax.nn.sigmoid(x)


def _rms(x, g):
    return x * lax.rsqrt(jnp.mean(x * x, axis=-1, keepdims=True) + EPS) * g


def _iota(shape, dim):
    return lax.broadcasted_iota(jnp.int32, shape, dim)


def _matmul(a, b, mode="nn", res=None, name="mm", a_cols=None, out_dtype=F32):
    a_off, a_w = (0, a.shape[1]) if a_cols is None else a_cols
    if mode == "nn":
        (M, K), (K2, N) = (a.shape[0], a_w), b.shape
    elif mode == "nt":
        (M, K), (N, K2) = (a.shape[0], a_w), b.shape
    else:
        (K, M), (K2, N) = (a.shape[0], a_w), b.shape
    assert K == K2, (a.shape, b.shape, mode)
    tm = _pick(M, (512, 256, 128))
    tn = _pick(N, (1024, 512, 256, 128))
    tk = _pick(K, (1024, 512, 256, 128))
    nk = K // tk
    dims = {"nn": _NN, "nt": _NT, "tn": _TN}[mode]
    ao = a_off // (tm if mode == "tn" else tk)
    assert ao * (tm if mode == "tn" else tk) == a_off
    if mode == "nn":
        a_spec = pl.BlockSpec((tm, tk), lambda i, j, k: (i, k + ao))
        b_spec = pl.BlockSpec((tk, tn), lambda i, j, k: (k, j))
    elif mode == "nt":
        a_spec = pl.BlockSpec((tm, tk), lambda i, j, k: (i, k + ao))
        b_spec = pl.BlockSpec((tn, tk), lambda i, j, k: (j, k))
    else:
        a_spec = pl.BlockSpec((tk, tm), lambda i, j, k: (k, i + ao))
        b_spec = pl.BlockSpec((tk, tn), lambda i, j, k: (k, j))
    o_spec = pl.BlockSpec((tm, tn), lambda i, j, k: (i, j))
    has_res = res is not None

    def kern(*refs):
        if has_res:
            a_ref, b_ref, r_ref, o_ref, acc_ref = refs
        else:
            a_ref, b_ref, o_ref, acc_ref = refs
        k = pl.program_id(2)

        @pl.when(k == 0)
        def _():
            acc_ref[...] = jnp.zeros_like(acc_ref)

        acc_ref[...] += lax.dot_general(a_ref[...].astype(BF16), b_ref[...].astype(BF16), (dims, ((), ())),
                                        preferred_element_type=F32)

        @pl.when(k == nk - 1)
        def _():
            if has_res:
                o_ref[...] = (acc_ref[...] + r_ref[...]).astype(o_ref.dtype)
            else:
                o_ref[...] = acc_ref[...].astype(o_ref.dtype)

    in_specs = [a_spec, b_spec] + ([o_spec] if has_res else [])
    ops = (a, b) + ((res,) if has_res else ())
    return pl.pallas_call(
        kern, name=name, grid=(M // tm, N // tn, nk), in_specs=in_specs, out_specs=o_spec,
        out_shape=jax.ShapeDtypeStruct((M, N), out_dtype), scratch_shapes=[pltpu.VMEM((tm, tn), F32)],
        compiler_params=_cparams(("parallel", "parallel", "arbitrary")))(*ops)


def _matmul_cat(pieces, b, mode="nn", res=None, name="mmcat"):
    M = pieces[0].shape[0]
    widths = [p.shape[1] for p in pieces]
    K = sum(widths)
    N = b.shape[1] if mode == "nn" else b.shape[0]
    assert (b.shape[0] if mode == "nn" else b.shape[1]) == K
    tm = _pick(M, (256, 128))
    tn = _pick(N, (1024, 512, 256, 128))
    npc = len(pieces)
    has_res = res is not None
    dims = _NN if mode == "nn" else _NT

    def kern(*refs):
        b_ref = refs[npc]
        o_ref = refs[-1]
        acc = refs[npc + 1][...] if has_res else None
        off = 0
        for p in range(npc):
            bp = b_ref[off:off + widths[p], :] if mode == "nn" else b_ref[:, off:off + widths[p]]
            t = lax.dot_general(refs[p][...].astype(BF16), bp.astype(BF16), (dims, ((), ())),
                                preferred_element_type=F32)
            acc = t if acc is None else acc + t
            off += widths[p]
        o_ref[...] = acc

    in_specs = [pl.BlockSpec((tm, w), lambda j, i: (i, 0)) for w in widths]
    in_specs.append(pl.BlockSpec((K, tn), lambda j, i: (0, j)) if mode == "nn"
                    else pl.BlockSpec((tn, K), lambda j, i: (j, 0)))
    o_spec = pl.BlockSpec((tm, tn), lambda j, i: (i, j))
    if has_res:
        in_specs.append(o_spec)
    ops = list(pieces) + [b] + ([res] if has_res else [])
    return pl.pallas_call(
        kern, name=name, grid=(N // tn, M // tm), in_specs=in_specs, out_specs=o_spec,
        out_shape=jax.ShapeDtypeStruct((M, N), F32), compiler_params=_cparams(("parallel", "parallel")))(*ops)


def _blk(a, width=None, colblk=0):
    return (a, a.shape[1] if width is None else width, colblk)


def _row_specs(blocked, params, ts):
    specs = []
    for (_, w, cb) in blocked:
        specs.append(pl.BlockSpec((ts, w), functools.partial(lambda i, cb: (i, cb), cb=cb)))
    for p in params:
        specs.append(pl.BlockSpec(p.shape, lambda i: (0, 0)))
    return specs


def _rowwise(fn, blocked, params, out_widths, name, ts=256, out_dtypes=None):
    S = blocked[0][0].shape[0]
    ts = min(ts, S)
    nb, npar = len(blocked), len(params)
    out_dtypes = [F32] * len(out_widths) if out_dtypes is None else out_dtypes

    def kern(*refs):
        vals = [r[...] for r in refs[:nb + npar]]
        outs = fn(*vals)
        for o_ref, o in zip(refs[nb + npar:], outs):
            o_ref[...] = o.astype(o_ref.dtype)

    return pl.pallas_call(
        kern, name=name, grid=(S // ts,), in_specs=_row_specs(blocked, params, ts),
        out_specs=[pl.BlockSpec((ts, w), lambda i: (i, 0)) for w in out_widths],
        out_shape=[jax.ShapeDtypeStruct((S, w), d) for w, d in zip(out_widths, out_dtypes)],
        compiler_params=_cparams(("parallel",)))(*[b[0] for b in blocked], *params)


def _rowwise_bwd(fn, blocked, params, cots, name, blocked_grad=None, param_grad=None, adds=None, ts=256,
                 out_dtypes=None):
    S = blocked[0][0].shape[0]
    ts = min(ts, S)
    cots = [c if isinstance(c, tuple) else _blk(c) for c in cots]
    nb, npar, nc = len(blocked), len(params), len(cots)
    blocked_grad = [True] * nb if blocked_grad is None else blocked_grad
    param_grad = [True] * npar if param_grad is None else param_grad
    adds = {} if adds is None else adds
    bidx = [i for i in range(nb) if blocked_grad[i]]
    pidx = [i for i in range(npar) if param_grad[i]]
    add_keys = sorted(adds)
    n_in = nb + npar + nc + len(add_keys)

    def kern(*refs):
        i = pl.program_id(0)
        xs = [r[...] for r in refs[:nb]]
        ps = [r[...] for r in refs[nb:nb + npar]]
        gs = [r[...] for r in refs[nb + npar:nb + npar + nc]]
        add_vals = {k: refs[nb + npar + nc + n][...] for n, k in enumerate(add_keys)}
        outs = refs[n_in:]

        def f(*diff):
            full_x = list(xs)
            full_p = list(ps)
            for n, ix in enumerate(bidx):
                full_x[ix] = diff[n]
            for n, ix in enumerate(pidx):
                full_p[ix] = diff[len(bidx) + n]
            return tuple(fn(*full_x, *full_p))

        _, vjp = jax.vjp(f, *[xs[ix] for ix in bidx], *[ps[ix] for ix in pidx])
        grads = vjp(tuple(gs))
        for n, ix in enumerate(bidx):
            g = grads[n]
            if ix in add_vals:
                g = g + add_vals[ix]
            outs[n][...] = g.astype(outs[n].dtype)
        for n in range(len(pidx)):
            o_ref = outs[len(bidx) + n]

            @pl.when(i == 0)
            def _(o_ref=o_ref):
                o_ref[...] = jnp.zeros_like(o_ref)

            o_ref[...] += grads[len(bidx) + n]

    in_specs = _row_specs(blocked, params, ts)
    in_specs += _row_specs(cots, [], ts)
    in_specs += [pl.BlockSpec((ts, adds[k].shape[1]), lambda i: (i, 0)) for k in add_keys]
    out_specs = [pl.BlockSpec((ts, blocked[ix][1]), lambda i: (i, 0)) for ix in bidx]
    out_specs += [pl.BlockSpec(params[ix].shape, lambda i: (0, 0)) for ix in pidx]
    out_dtypes = [F32] * len(bidx) if out_dtypes is None else out_dtypes
    out_shape = [jax.ShapeDtypeStruct((S, blocked[ix][1]), d) for ix, d in zip(bidx, out_dtypes)]
    out_shape += [jax.ShapeDtypeStruct(params[ix].shape, F32) for ix in pidx]
    return pl.pallas_call(
        kern, name=name, grid=(S // ts,), in_specs=in_specs, out_specs=out_specs, out_shape=out_shape,
        compiler_params=_cparams(("arbitrary",)))(*[b[0] for b in blocked], *params, *[c[0] for c in cots],
                                                    *[adds[k] for k in add_keys])


def _rms_fn(x, g):
    return (_rms(x, g),)


def _head_norm(o, n_heads, dh):
    outs = []
    for h in range(n_heads):
        oh = o[:, h * dh:(h + 1) * dh]
        outs.append(oh * lax.rsqrt(jnp.mean(oh * oh, axis=-1, keepdims=True) + EPS))
    return outs


def _ret_post_fn(o_raw, gate, ret_norm):
    o = jnp.concatenate(_head_norm(o_raw, 4, 128), axis=1)
    return (o * ret_norm * _silu(gate),)


def _s5_post_fn(y1, y2, u, d, w_glu, b_glu):
    y = y1 - y2 + d * u
    y = jax.nn.gelu(y)
    return (y * jax.nn.sigmoid(mm(y, w_glu) + b_glu),)


def _xattn_fn(q, kv):
    outs = []
    for h in range(4):
        qh = q[:, h * 256:(h + 1) * 256]
        kh = kv[:, h * 256:(h + 1) * 256]
        vh = kv[:, 1024 + h * 256:1024 + (h + 1) * 256]
        s = mm_nt(qh, kh) * (256 ** -0.5)
        s = s - lax.stop_gradient(jnp.max(s, axis=-1, keepdims=True))
        p = jnp.exp(s)
        p = p / jnp.sum(p, axis=-1, keepdims=True)
        outs.append(mm(p, vh))
    return (jnp.concatenate(outs, axis=1),)


def _softplus(x):
    return jnp.maximum(x, 0.0) + jnp.log1p(jnp.exp(-jnp.abs(x)))


def _gdn_gates_fn(pt, a_log_p, dtb_p):
    rows, cols = _iota((128, 1024), 0), _iota((128, 1024), 1)
    e_b = (rows == (cols >> 7)).astype(F32)
    e_a = (rows == (cols >> 7) + 8).astype(F32)
    beta = jax.nn.sigmoid(pt)
    g = -(jnp.exp(a_log_p) * _softplus(pt + dtb_p))
    return mmh(g, e_a), mmh(beta, e_b)


def _gdn_post_fn(o_raw, z, o_norm):
    outs = _head_norm(o_raw, 8, 128)
    o = jnp.concatenate([oh * o_norm for oh in outs], axis=1)
    return (o * _silu(z),)


def _ffn_post(up, gate):
    return _silu(gate) * up


def _shift_down(cur, prev8, sh, row8):
    if sh == 0:
        return cur
    r = pltpu.roll(cur, sh, 0)
    p = pltpu.roll(prev8, sh, 0)
    top = jnp.where(row8 < sh, p, r[0:8])
    return jnp.concatenate([top, r[8:]], axis=0)


def _shift_up(cur, next8, sh, row8):
    if sh == 0:
        return cur
    ts = cur.shape[0]
    r = pltpu.roll(cur, ts - sh, 0)
    p = pltpu.roll(next8, 8 - sh, 0)
    bot = jnp.where(row8 >= 8 - sh, p, r[ts - 8:])
    return jnp.concatenate([r[:ts - 8], bot], axis=0)


def _conv_rows(cur, prev8, wrows, row8):
    k_w = len(wrows)
    out = None
    for j in range(k_w):
        t = _shift_down(cur, prev8, k_w - 1 - j, row8) * wrows[j]
        out = t if out is None else out + t
    return out


def _conv_specs(x, xoff, w, woff, ts, tc):
    r8 = ts // 8
    return [pl.BlockSpec((ts, tc), functools.partial(lambda i, j, o: (i, j + o), o=xoff)),
            pl.BlockSpec((8, tc), functools.partial(lambda i, j, o: (jnp.maximum(i * r8 - 1, 0), j + o), o=xoff)),
            pl.BlockSpec((w.shape[0], tc), functools.partial(lambda i, j, o: (0, j + o), o=woff))]


def _conv_post(srcs, post, ncol, tc, name, cots=None, ts=256, out_dtype=F32):
    S = srcs[0][0].shape[0]
    ns = len(srcs)
    bwd = cots is not None

    def kern(*refs):
        first = pl.program_id(0) == 0
        row8 = _iota((8, tc), 0)
        cs = []
        for s in range(ns):
            cur_ref, prev_ref, w_ref = refs[3 * s:3 * s + 3]
            prev = jnp.where(first, 0.0, prev_ref[...])
            wrows = [w_ref[j:j + 1, :] for j in range(w_ref.shape[0])]
            cs.append(_conv_rows(cur_ref[...], prev, wrows, row8))
        if bwd:
            g = refs[3 * ns][...]
            _, vjp = jax.vjp(lambda *c: post(*c), *cs)
            for o_ref, d in zip(refs[3 * ns + 1:], vjp(g)):
                o_ref[...] = d
        else:
            refs[3 * ns][...] = post(*cs).astype(refs[3 * ns].dtype)

    in_specs = []
    ops = []
    for (x, xoff, w, woff) in srcs:
        in_specs += _conv_specs(x, xoff, w, woff, ts, tc)
        ops += [x, x, w]
    o_spec = pl.BlockSpec((ts, tc), lambda i, j: (i, j))
    o_shape = jax.ShapeDtypeStruct((S, ncol * tc), F32)
    if bwd:
        in_specs.append(o_spec)
        ops.append(cots)
        out_specs, out_shape = [o_spec] * ns, [o_shape] * ns
    else:
        out_specs, out_shape = o_spec, jax.ShapeDtypeStruct((S, ncol * tc), out_dtype)
    return pl.pallas_call(
        kern, name=name, grid=(S // ts, ncol), in_specs=in_specs, out_specs=out_specs, out_shape=out_shape,
        compiler_params=_cparams(("parallel", "parallel")))(*ops)


def _conv_bwd(dc, x, xoff, w, woff, ncol, tc, name, ts=256):
    S = x.shape[0]
    k_w = w.shape[0]
    r8 = ts // 8
    nblk8 = S // 8
    nrow = S // ts

    def kern(dc_ref, dn_ref, x_ref, xp_ref, w_ref, dx_ref, dw_ref):
        i = pl.program_id(1)
        row8 = _iota((8, tc), 0)
        dcur = dc_ref[...]
        dnext = jnp.where(i == nrow - 1, 0.0, dn_ref[...])
        xcur = x_ref[...]
        xprev = jnp.where(i == 0, 0.0, xp_ref[...])

        @pl.when(i == 0)
        def _():
            dw_ref[...] = jnp.zeros_like(dw_ref)

        dx = None
        for j in range(k_w):
            sh = k_w - 1 - j
            wj = w_ref[j:j + 1, :]
            t = _shift_up(dcur, dnext, sh, row8) * wj
            dx = t if dx is None else dx + t
            dw_ref[j:j + 1, :] += jnp.sum(dcur * _shift_down(xcur, xprev, sh, row8), axis=0, keepdims=True)
        dx_ref[...] = dx.astype(dx_ref.dtype)

    in_specs = [pl.BlockSpec((ts, tc), lambda j, i: (i, j)),
                pl.BlockSpec((8, tc), lambda j, i: (jnp.minimum((i + 1) * r8, nblk8 - 1), j)),
                pl.BlockSpec((ts, tc), functools.partial(lambda j, i, o: (i, j + o), o=xoff)),
                pl.BlockSpec((8, tc), functools.partial(lambda j, i, o: (jnp.maximum(i * r8 - 1, 0), j + o), o=xoff)),
                pl.BlockSpec((k_w, tc), functools.partial(lambda j, i, o: (0, j + o), o=woff))]
    out_specs = [pl.BlockSpec((ts, tc), lambda j, i: (i, j)), pl.BlockSpec((k_w, tc), lambda j, i: (0, j))]
    out_shape = [jax.ShapeDtypeStruct((S, ncol * tc), BF16), jax.ShapeDtypeStruct((k_w, ncol * tc), F32)]
    return pl.pallas_call(
        kern, name=name, grid=(ncol, nrow), in_specs=in_specs, out_specs=out_specs, out_shape=out_shape,
        compiler_params=_cparams(("parallel", "arbitrary")))(dc, dc, x, x, w)


def _ret_tables(S):
    H, C, dh = 4, 128, 128
    lg = jnp.log1p(-jnp.exp2(-5.0 - jnp.arange(H, dtype=F32)))
    idx = jnp.arange(C, dtype=F32)
    diff = idx[:, None] - idx[None, :]
    causal = diff >= 0
    intra = jnp.where(causal, jnp.exp(lg[:, None, None] * jnp.where(causal, diff, 0.0)), 0.0)
    kdec = jnp.broadcast_to(jnp.exp(lg[:, None] * (C - 1 - idx))[:, :, None], (H, C, dh))
    qdec = jnp.broadcast_to(jnp.exp(lg[:, None] * (idx + 1))[:, :, None], (H, C, dh))
    cdec = jnp.broadcast_to(jnp.exp(lg * C)[:, None, None], (H, dh, dh))
    half = dh // 2
    inv = jnp.exp(-math.log(10000.0) * jnp.arange(half, dtype=F32) / half)
    ang = jnp.arange(S).astype(F32)[:, None] * inv[None, :]
    cos, sin = jnp.cos(ang), jnp.sin(ang)
    cosf = jnp.concatenate([cos, cos], axis=1)
    sinf = jnp.concatenate([-sin, sin], axis=1)
    return cosf, sinf, intra, kdec, qdec, cdec


def _ret_chunk(q, k, v, cosf, sinf, intra, kdec, qdec, cdec, state):
    qr = q * cosf + _swap_halves(q) * sinf
    kr = (k * cosf + _swap_halves(k) * sinf) * (128 ** -0.5)
    scores = mm_nt(qr, kr) * intra
    inner = mm(scores, v)
    kv = mm_tn(kr * kdec, v)
    cross = mm(qr * qdec, state)
    return inner + cross, state * cdec + kv


def _ret_specs(N, rev):
    def nn(n):
        return N - 1 - n if rev else n
    chunk = [pl.BlockSpec((128, 128), functools.partial(lambda h, n, o: (nn(n), h + o), o=o)) for o in (0, 4, 8)]
    pos = [pl.BlockSpec((128, 128), lambda h, n: (nn(n), 0))] * 2
    tabs = [pl.BlockSpec((None, 128, 128), lambda h, n: (h, 0, 0))] * 4
    st = pl.BlockSpec((None, None, 128, 128), lambda h, n: (h, nn(n), 0, 0))
    o = pl.BlockSpec((128, 128), lambda h, n: (nn(n), h))
    return chunk, pos, tabs, st, o


def _ret_fwd(proj, tabs):
    S = proj.shape[0]
    N = S // 128
    chunk, pos, tsp, st, o = _ret_specs(N, False)

    def kern(q_ref, k_ref, v_ref, c_ref, s_ref, i_ref, kd_ref, qd_ref, cd_ref, o_ref, sp_ref, st_ref):
        @pl.when(pl.program_id(1) == 0)
        def _():
            st_ref[...] = jnp.zeros_like(st_ref)

        state = st_ref[...]
        sp_ref[...] = state
        out, new = _ret_chunk(q_ref[...], k_ref[...], v_ref[...], c_ref[...], s_ref[...], i_ref[...], kd_ref[...],
                              qd_ref[...], cd_ref[...], state)
        o_ref[...] = out
        st_ref[...] = new

    return pl.pallas_call(
        kern, name="ret_fwd", grid=(4, N), in_specs=chunk + pos + tsp, out_specs=[o, st],
        out_shape=[jax.ShapeDtypeStruct((S, 512), F32), jax.ShapeDtypeStruct((4, N, 128, 128), F32)],
        scratch_shapes=[pltpu.VMEM((128, 128), F32)],
        compiler_params=_cparams(("parallel", "arbitrary")))(proj, proj, proj, *tabs)


def _ret_bwd(proj, tabs, states, do):
    S = proj.shape[0]
    N = S // 128
    chunk, pos, tsp, st, o = _ret_specs(N, True)

    def kern(q_ref, k_ref, v_ref, c_ref, s_ref, i_ref, kd_ref, qd_ref, cd_ref, sp_ref, do_ref,
             dq_ref, dk_ref, dv_ref, ds_ref):
        @pl.when(pl.program_id(1) == 0)
        def _():
            ds_ref[...] = jnp.zeros_like(ds_ref)

        consts = (c_ref[...], s_ref[...], i_ref[...], kd_ref[...], qd_ref[...], cd_ref[...])
        _, vjp = jax.vjp(lambda q, k, v, s: _ret_chunk(q, k, v, *consts, s), q_ref[...], k_ref[...], v_ref[...],
                         sp_ref[...])
        dq, dk, dv, ds = vjp((do_ref[...], ds_ref[...]))
        dq_ref[...] = dq.astype(BF16)
        dk_ref[...] = dk.astype(BF16)
        dv_ref[...] = dv.astype(BF16)
        ds_ref[...] = ds

    return pl.pallas_call(
        kern, name="ret_bwd", grid=(4, N), in_specs=chunk + pos + tsp + [st, o], out_specs=[o, o, o],
        out_shape=[jax.ShapeDtypeStruct((S, 512), BF16)] * 3, scratch_shapes=[pltpu.VMEM((128, 128), F32)],
        compiler_params=_cparams(("parallel", "arbitrary")))(proj, proj, proj, *tabs, states, do)


GDN_C = 64
GDN_H = 8


def _unit_lower_inverse(a_mat, eye):
    p = -a_mat
    t_mat = eye + p
    for _ in range(5):
        p = mm3(p, p)
        t_mat = mm3(t_mat, eye + p)
    return t_mat


@jax.custom_vjp
def _known_inverse(a_mat, t_mat):
    return t_mat


def _known_inverse_f(a_mat, t_mat):
    return t_mat, t_mat


def _known_inverse_b(t_mat, g):
    return -mmh_tn(t_mat, mmh_nt(g, t_mat)), jnp.zeros_like(t_mat)


_known_inverse.defvjp(_known_inverse_f, _known_inverse_b)


def _gdn_intra(q, k, v, g_b, beta_b, t_known=None):
    c = GDN_C
    q = q * lax.rsqrt(jnp.sum(q * q, axis=-1, keepdims=True) + EPS) * (128 ** -0.5)
    k = k * lax.rsqrt(jnp.sum(k * k, axis=-1, keepdims=True) + EPS)
    ri, ci = _iota((c, c), 0), _iota((c, c), 1)
    incl = ri >= ci
    strict = ri > ci
    eye = (ri == ci).astype(F32)
    gc_b = mmh(incl.astype(F32), g_b)
    gl_b = mmh(jnp.ones((c, c), F32), g_b)
    kb = k * beta_b
    vb = v * beta_b
    gcc = gc_b[:, :c]
    gdiff = gcc - gcc.T
    decay = jnp.where(incl, jnp.exp(jnp.where(incl, gdiff, 0.0)), 0.0)
    a_mat = jnp.where(strict, mm_nt(kb, k) * decay, 0.0)
    t_mat = _unit_lower_inverse(a_mat, eye) if t_known is None else _known_inverse(a_mat, t_known)
    egc = jnp.exp(gc_b)
    w = mm(t_mat, kb * egc)
    u = mm(t_mat, vb)
    qk = jnp.where(incl, mm_nt(q, k) * decay, 0.0)
    return w, u, q * egc, k * jnp.exp(gl_b - gc_b), qk, t_mat


def _gdn_step(w, u, q_dec, k_dec, qk, g_b, state):
    gl_s = mmh(jnp.ones((128, GDN_C), F32), g_b)
    v_new = u - mm(w, state)
    o = mm(q_dec, state) + mm(qk, v_new)
    return o, state * jnp.exp(gl_s) + mm_tn(k_dec, v_new)


def _hs(h):
    return slice(h * 128, (h + 1) * 128)


def _gdn_intra_call(qkv, g_e, beta_e, cots=None):
    S = qkv.shape[0]
    N = S // GDN_C
    bwd = cots is not None
    row = pl.BlockSpec((GDN_C, 1024), lambda n: (n, 0))
    qkv_spec = pl.BlockSpec((GDN_C, 3072), lambda n: (n, 0))
    qk_spec = pl.BlockSpec((GDN_H, GDN_C, GDN_C), lambda n: (0, n, 0))

    def kern(*refs):
        x_ref, g_ref, b_ref = refs[:3]
        for h in range(GDN_H):
            args = (x_ref[:, _hs(h)], x_ref[:, _hs(8 + h)], x_ref[:, _hs(16 + h)], g_ref[:, _hs(h)], b_ref[:, _hs(h)])
            if bwd:
                dw_ref, du_ref, dqd_ref, dkd_ref, dqk_ref, dgadd_ref, t_ref = refs[3:10]
                outs = refs[10:]
                t_known = t_ref[h]
                _, vjp = jax.vjp(lambda *a: _gdn_intra(*a, t_known=t_known)[:5], *args)
                dq, dk, dv, dg, db = vjp((dw_ref[:, _hs(h)], du_ref[:, _hs(h)], dqd_ref[:, _hs(h)],
                                          dkd_ref[:, _hs(h)], dqk_ref[h]))
                for o_ref, d in zip(outs, (dq, dk, dv, dg + dgadd_ref[:, _hs(h)], db)):
                    o_ref[:, _hs(h)] = d
            else:
                w, u, qd, kd, qk, t_mat = _gdn_intra(*args)
                for o_ref, o in zip(refs[3:7], (w, u, qd, kd)):
                    o_ref[:, _hs(h)] = o
                refs[7][h] = qk
                refs[8][h] = t_mat

    big = jax.ShapeDtypeStruct((S, 1024), F32)
    sq = jax.ShapeDtypeStruct((GDN_H, S, GDN_C), F32)
    if bwd:
        in_specs = [qkv_spec, row, row, row, row, row, row, qk_spec, row, qk_spec]
        out_specs, out_shape = [row] * 5, [big] * 5
        ops = (qkv, g_e, beta_e) + tuple(cots)
    else:
        in_specs = [qkv_spec, row, row]
        out_specs = [row] * 4 + [qk_spec, qk_spec]
        out_shape = [big] * 4 + [sq, sq]
        ops = (qkv, g_e, beta_e)
    return pl.pallas_call(
        kern, name="gdn_intra_bwd" if bwd else "gdn_intra", grid=(N,), in_specs=in_specs, out_specs=out_specs,
        out_shape=out_shape, compiler_params=_cparams(("parallel",)))(*ops)


def _gdn_pass(w, u, qd, kd, qk, g_e, states=None, do=None):
    S = w.shape[0]
    N = S // GDN_C
    bwd = do is not None

    def nn(n):
        return N - 1 - n if bwd else n

    row = pl.BlockSpec((GDN_C, 1024), lambda n: (nn(n), 0))
    qk_spec = pl.BlockSpec((GDN_H, GDN_C, GDN_C), lambda n: (0, nn(n), 0))
    st_spec = pl.BlockSpec((None, GDN_H, 128, 128), lambda n: (nn(n), 0, 0, 0))

    def kern(*refs):
        w_ref, u_ref, qd_ref, kd_ref, qk_ref, g_ref = refs[:6]
        carry = refs[-1]

        @pl.when(pl.program_id(0) == 0)
        def _():
            carry[...] = jnp.zeros_like(carry)

        for h in range(GDN_H):
            args = (w_ref[:, _hs(h)], u_ref[:, _hs(h)], qd_ref[:, _hs(h)], kd_ref[:, _hs(h)], qk_ref[h],
                    g_ref[:, _hs(h)])
            if bwd:
                sp_ref, do_ref = refs[6:8]
                outs = refs[8:14]
                _, vjp = jax.vjp(_gdn_step, *args, sp_ref[h])
                dw, du, dqd, dkd, dqk, dg, ds = vjp((do_ref[:, _hs(h)], carry[h]))
                for o_ref, d in zip(outs[:4], (dw, du, dqd, dkd)):
                    o_ref[:, _hs(h)] = d
                outs[4][h] = dqk
                outs[5][:, _hs(h)] = dg
                carry[h] = ds
            else:
                o_ref, sp_ref = refs[6:8]
                state = carry[h]
                sp_ref[h] = state
                o, new = _gdn_step(*args, state)
                o_ref[:, _hs(h)] = o
                carry[h] = new

    big = jax.ShapeDtypeStruct((S, 1024), F32)
    in_specs = [row, row, row, row, qk_spec, row]
    if bwd:
        in_specs += [st_spec, row]
        out_specs = [row] * 4 + [qk_spec, row]
        out_shape = [big] * 4 + [jax.ShapeDtypeStruct((GDN_H, S, GDN_C), F32), big]
        ops = (w, u, qd, kd, qk, g_e, states, do)
    else:
        out_specs = [row, st_spec]
        out_shape = [big, jax.ShapeDtypeStruct((N, GDN_H, 128, 128), F32)]
        ops = (w, u, qd, kd, qk, g_e)
    return pl.pallas_call(
        kern, name="gdn_pass_bwd" if bwd else "gdn_pass", grid=(N,), in_specs=in_specs, out_specs=out_specs,
        out_shape=out_shape, scratch_shapes=[pltpu.VMEM((GDN_H, 128, 128), F32)],
        compiler_params=_cparams(("arbitrary",)))(*ops)


def _s5_prep_fn(lr, li, ldt, br, bi, cr, ci):
    dt = jnp.exp(ldt)
    mag = jnp.exp(lr * dt)
    a_re = mag * jnp.cos(li * dt)
    a_im = mag * jnp.sin(li * dt)
    den = lr * lr + li * li
    z_re = ((a_re - 1.0) * lr + a_im * li) / den
    z_im = (a_im * lr - (a_re - 1.0) * li) / den
    e1 = ((_iota((512, 32), 0) >> 4) == _iota((512, 32), 1)).astype(F32)
    zr_e = mmh(e1, z_re)
    zi_e = mmh(e1, z_im)
    bb_re = zr_e * br - zi_e * bi
    bb_im = zr_e * bi + zi_e * br
    t1 = ((_iota((64, 2048), 1) & 63) == _iota((64, 2048), 0)).astype(F32)
    m1 = (_iota((512, 2048), 0) >> 4) == (_iota((512, 2048), 1) >> 6)
    bd_re = jnp.where(m1, mmh(bb_re, t1), 0.0)
    bd_im = jnp.where(m1, mmh(bb_im, t1), 0.0)
    t2 = ((_iota((16, 512), 1) & 15) == _iota((16, 512), 0)).astype(F32)
    m2 = (_iota((2048, 512), 0) >> 6) == (_iota((2048, 512), 1) >> 4)
    cd_re = jnp.where(m2, mmh(cr, t2), 0.0)
    cd_im = jnp.where(m2, mmh(ci, t2), 0.0)
    return a_re, a_im, bd_re, bd_im, cd_re, cd_im


_PREP_OUT = [(32, 64), (32, 64), (512, 2048), (512, 2048), (2048, 512), (2048, 512)]


def _s5_prep(params, cots=None):
    bwd = cots is not None

    def kern(*refs):
        vals = [r[...] for r in refs[:7]]
        if bwd:
            gs = tuple(r[...] for r in refs[7:13])
            _, vjp = jax.vjp(_s5_prep_fn, *vals)
            for o_ref, d in zip(refs[13:], vjp(gs)):
                o_ref[...] = d
        else:
            for o_ref, o in zip(refs[7:], _s5_prep_fn(*vals)):
                o_ref[...] = o

    if bwd:
        out_shape = [jax.ShapeDtypeStruct(p.shape, F32) for p in params]
        ops = list(params) + list(cots)
    else:
        out_shape = [jax.ShapeDtypeStruct(s, F32) for s in _PREP_OUT]
        ops = list(params)
    return pl.pallas_call(kern, name="s5_prep_bwd" if bwd else "s5_prep", out_shape=out_shape,
                          compiler_params=_cparams())(*ops)


def _cmul(ar, ai, br, bi):
    return ar * br - ai * bi, ar * bi + ai * br


def _power_table(ar, ai, row8, descending):
    pr, pi = ar, ai
    tr = jnp.zeros(row8.shape, F32)
    ti = jnp.zeros(row8.shape, F32)
    for n in range(8):
        r = 7 - n if descending else n
        tr = jnp.where(row8 == r, pr, tr)
        ti = jnp.where(row8 == r, pi, ti)
        if n < 7:
            pr, pi = _cmul(pr, pi, ar, ai)
    return tr, ti


def _tile_scan(xr, xi, pows, row8, up):
    for d, (pr, pi) in zip((1, 2, 4), pows):
        if up:
            sr = jnp.where(row8 < 8 - d, pltpu.roll(xr, 8 - d, 0), 0.0)
            si = jnp.where(row8 < 8 - d, pltpu.roll(xi, 8 - d, 0), 0.0)
        else:
            sr = jnp.where(row8 >= d, pltpu.roll(xr, d, 0), 0.0)
            si = jnp.where(row8 >= d, pltpu.roll(xi, d, 0), 0.0)
        mr, mi = _cmul(pr, pi, sr, si)
        xr, xi = xr + mr, xi + mi
    return xr, xi


def _pick_row(x, row8, r):
    return jnp.sum(jnp.where(row8 == r, x, 0.0), axis=0, keepdims=True)


SCAN_LB = 512
SCAN_TS = 512


def _scan_fwd(bu_re, bu_im, a_re, a_im):
    S, L = bu_re.shape
    ts, lb = min(SCAN_TS, S), SCAN_LB
    nt = ts // 8

    def kern(br_ref, bi_ref, ar_ref, ai_ref, or_ref, oi_ref, cr_ref, ci_ref):
        @pl.when(pl.program_id(1) == 0)
        def _():
            cr_ref[...] = jnp.zeros_like(cr_ref)
            ci_ref[...] = jnp.zeros_like(ci_ref)

        row8 = _iota((8, lb), 0)
        ar, ai = ar_ref[...], ai_ref[...]
        a2 = _cmul(ar, ai, ar, ai)
        a4 = _cmul(*a2, *a2)
        pows = ((ar, ai), a2, a4)
        tr, ti = _power_table(ar, ai, row8, False)

        def body(i, carry):
            cr, ci = carry
            off = pl.multiple_of(i * 8, 8)
            xr, xi = _tile_scan(br_ref[pl.ds(off, 8), :], bi_ref[pl.ds(off, 8), :], pows, row8, False)
            mr, mi = _cmul(tr, ti, cr, ci)
            xr, xi = xr + mr, xi + mi
            or_ref[pl.ds(off, 8), :] = xr
            oi_ref[pl.ds(off, 8), :] = xi
            return _pick_row(xr, row8, 7), _pick_row(xi, row8, 7)

        cr, ci = lax.fori_loop(0, nt, body, (cr_ref[...], ci_ref[...]))
        cr_ref[...] = cr
        ci_ref[...] = ci

    blk = pl.BlockSpec((ts, lb), lambda j, i: (i, j))
    par = pl.BlockSpec((1, lb), lambda j, i: (0, j))
    return pl.pallas_call(
        kern, name="s5_scan_fwd", grid=(L // lb, S // ts), in_specs=[blk, blk, par, par], out_specs=[blk, blk],
        out_shape=[jax.ShapeDtypeStruct((S, L), F32)] * 2,
        scratch_shapes=[pltpu.VMEM((1, lb), F32), pltpu.VMEM((1, lb), F32)],
        compiler_params=_cparams(("parallel", "arbitrary")))(bu_re, bu_im, a_re, a_im)


def _scan_bwd(dst_re, dst_im, st_re, st_im, a_re, a_im):
    S, L = dst_re.shape
    ts, lb = min(SCAN_TS, S), SCAN_LB
    nt = ts // 8
    nb = S // ts
    r8 = ts // 8

    def kern(dr_ref, di_ref, sr_ref, si_ref, pr_ref, pi_ref, ar_ref, ai_ref, gr_ref, gi_ref, dar_ref, dai_ref,
             cr_ref, ci_ref):
        step = pl.program_id(1)
        blk = nb - 1 - step

        @pl.when(step == 0)
        def _():
            cr_ref[...] = jnp.zeros_like(cr_ref)
            ci_ref[...] = jnp.zeros_like(ci_ref)
            dar_ref[...] = jnp.zeros_like(dar_ref)
            dai_ref[...] = jnp.zeros_like(dai_ref)

        row8 = _iota((8, lb), 0)
        ar, ai = ar_ref[...], ai_ref[...]
        nai = -ai
        a2 = _cmul(ar, nai, ar, nai)
        a4 = _cmul(*a2, *a2)
        pows = ((ar, nai), a2, a4)
        tr, ti = _power_table(ar, nai, row8, True)
        halo_r = jnp.where(blk == 0, 0.0, pr_ref[...])
        halo_i = jnp.where(blk == 0, 0.0, pi_ref[...])

        def body(n, carry):
            cr, ci, acc_r, acc_i = carry
            i = nt - 1 - n
            off = pl.multiple_of(i * 8, 8)
            gr, gi = _tile_scan(dr_ref[pl.ds(off, 8), :], di_ref[pl.ds(off, 8), :], pows, row8, True)
            mr, mi = _cmul(tr, ti, cr, ci)
            gr, gi = gr + mr, gi + mi
            gr_ref[pl.ds(off, 8), :] = gr
            gi_ref[pl.ds(off, 8), :] = gi
            poff = pl.multiple_of(jnp.maximum(i - 1, 0) * 8, 8)
            before_r = jnp.where(i == 0, halo_r, sr_ref[pl.ds(poff, 8), :])
            before_i = jnp.where(i == 0, halo_i, si_ref[pl.ds(poff, 8), :])
            last_r = _pick_row(before_r, row8, 7)
            last_i = _pick_row(before_i, row8, 7)
            spr = jnp.where(row8 >= 1, pltpu.roll(sr_ref[pl.ds(off, 8), :], 1, 0), last_r)
            spi = jnp.where(row8 >= 1, pltpu.roll(si_ref[pl.ds(off, 8), :], 1, 0), last_i)
            acc_r = acc_r + gr * spr + gi * spi
            acc_i = acc_i + gi * spr - gr * spi
            return _pick_row(gr, row8, 0), _pick_row(gi, row8, 0), acc_r, acc_i

        zero = jnp.zeros((8, lb), F32)
        cr, ci, acc_r, acc_i = lax.fori_loop(0, nt, body, (cr_ref[...], ci_ref[...], zero, zero))
        cr_ref[...] = cr
        ci_ref[...] = ci
        dar_ref[...] += jnp.sum(acc_r, axis=0, keepdims=True)
        dai_ref[...] += jnp.sum(acc_i, axis=0, keepdims=True)

    blk = pl.BlockSpec((ts, lb), lambda j, i: (nb - 1 - i, j))
    halo = pl.BlockSpec((8, lb), lambda j, i: (jnp.maximum((nb - 1 - i) * r8 - 1, 0), j))
    par = pl.BlockSpec((1, lb), lambda j, i: (0, j))
    return pl.pallas_call(
        kern, name="s5_scan_bwd", grid=(L // lb, nb), in_specs=[blk, blk, blk, blk, halo, halo, par, par],
        out_specs=[blk, blk, par, par],
        out_shape=[jax.ShapeDtypeStruct((S, L), F32)] * 2 + [jax.ShapeDtypeStruct((1, L), F32)] * 2,
        scratch_shapes=[pltpu.VMEM((1, lb), F32), pltpu.VMEM((1, lb), F32)],
        compiler_params=_cparams(("parallel", "arbitrary")))(dst_re, dst_im, st_re, st_im, st_re, st_im, a_re, a_im)


def _loss_grad(x, target, gain, ts=256):
    S, D = x.shape

    def kern(x_ref, t_ref, g_ref, loss_ref, dx_ref, dg_ref):
        i = pl.program_id(0)
        tgt = t_ref[...]

        def f(xv, gv):
            err = _rms(xv, gv) - tgt
            return 0.5 * jnp.mean(err * err, axis=-1, keepdims=True)

        rowloss, vjp = jax.vjp(f, x_ref[...], g_ref[...])
        dx, dg = vjp(jnp.ones_like(rowloss))
        dx_ref[...] = dx

        @pl.when(i == 0)
        def _():
            loss_ref[...] = jnp.zeros_like(loss_ref)
            dg_ref[...] = jnp.zeros_like(dg_ref)

        loss_ref[...] += jnp.broadcast_to(jnp.sum(rowloss, axis=0, keepdims=True), loss_ref.shape)
        dg_ref[...] += dg

    row = pl.BlockSpec((ts, D), lambda i: (i, 0))
    return pl.pallas_call(
        kern, name="loss_grad", grid=(S // ts,), in_specs=[row, row, pl.BlockSpec((1, D), lambda i: (0, 0))],
        out_specs=[pl.BlockSpec((8, 128), lambda i: (0, 0)), row, pl.BlockSpec((1, D), lambda i: (0, 0))],
        out_shape=[jax.ShapeDtypeStruct((8, 128), F32), jax.ShapeDtypeStruct((S, D), F32),
                   jax.ShapeDtypeStruct((1, D), F32)],
        compiler_params=_cparams(("arbitrary",)))(x, target, gain)


def _rms_fwd(x, g, name):
    return _rowwise(_rms_fn, [_blk(x)], [g], [x.shape[1]], name, out_dtypes=[BF16])[0]


def _rms_bwd(x, g, dy, name, add=None):
    return _rowwise_bwd(_rms_fn, [_blk(x)], [g], [dy], name, adds=None if add is None else {0: add})


FFN_TC = 1408


def _common_fwd(x, mem, P, L):
    hx = _rms_fwd(x, P['xa_norm'], L + "xa_norm")
    q = _matmul(hx, P['xa_wq'], name=L + "xa_q")
    memn = _rms_fwd(mem, P['mem_norm'], L + "mem_norm")
    kv = _matmul(memn, P['xa_wkv'], name=L + "xa_kv")
    att = _rowwise(_xattn_fn, [_blk(q)], [kv], [1024], L + "xattn", out_dtypes=[BF16])[0]
    x2 = _matmul(att, P['xa_wo'], res=x, name=L + "xa_o")
    hf = _rms_fwd(x2, P['ffn_norm'], L + "ffn_norm")
    hu = _matmul(hf, P['ffn_w_up'], name=L + "ffn_up")
    cw = P['ffn_conv']
    act = _conv_post([(hu, 0, cw, 0), (hu, 2, cw, 2)], _ffn_post, 2, FFN_TC, L + "ffn_conv", out_dtype=BF16)
    x3 = _matmul(act, P['ffn_w_down'], res=x2, name=L + "ffn_down")
    return x3, (x, mem, hx, q, memn, kv, att, x2, hf, hu, act)


def _common_bwd(saved, dx3, P, L):
    x, mem, hx, q, memn, kv, att, x2, hf, hu, act = saved
    G = {}
    dact = _matmul(dx3, P['ffn_w_down'], "nt", name=L + "ffn_down_dx")
    G['ffn_w_down'] = _matmul(act, dx3, "tn", name=L + "ffn_down_dw")
    cw = P['ffn_conv']
    dcu, dcg = _conv_post([(hu, 0, cw, 0), (hu, 2, cw, 2)], _ffn_post, 2, FFN_TC, L + "ffn_conv_dpost", cots=dact)
    dhu_u, dcw_u = _conv_bwd(dcu, hu, 0, cw, 0, 2, FFN_TC, L + "ffn_conv_bwd_up")
    dhu_g, dcw_g = _conv_bwd(dcg, hu, 2, cw, 2, 2, FFN_TC, L + "ffn_conv_bwd_gate")
    G['ffn_conv'] = jnp.concatenate([dcw_u, dcw_g], axis=1)
    dhf = _matmul_cat([dhu_u, dhu_g], P['ffn_w_up'], "nt", name=L + "ffn_up_dx")
    G['ffn_w_up'] = jnp.concatenate([_matmul(hf, dhu_u, "tn", name=L + "ffn_up_dw_up"),
                                     _matmul(hf, dhu_g, "tn", name=L + "ffn_up_dw_gate")], axis=1)
    dx2, G['ffn_norm'] = _rms_bwd(x2, P['ffn_norm'], dhf, L + "ffn_norm_bwd", add=dx3)
    datt = _matmul(dx2, P['xa_wo'], "nt", name=L + "xa_o_dx")
    G['xa_wo'] = _matmul(att, dx2, "tn", name=L + "xa_o_dw")
    dq, dkv = _rowwise_bwd(_xattn_fn, [_blk(q)], [kv], [datt], L + "xattn_bwd", out_dtypes=[BF16])
    dhx = _matmul(dq, P['xa_wq'], "nt", name=L + "xa_q_dx")
    G['xa_wq'] = _matmul(hx, dq, "tn", name=L + "xa_q_dw")
    dmemn = _matmul(dkv, P['xa_wkv'], "nt", name=L + "xa_kv_dx")
    G['xa_wkv'] = _matmul(memn, dkv, "tn", name=L + "xa_kv_dw")
    _, G['mem_norm'] = _rms_bwd(mem, P['mem_norm'], dmemn, L + "mem_norm_bwd")
    dx, G['xa_norm'] = _rms_bwd(x, P['xa_norm'], dhx, L + "xa_norm_bwd", add=dx2)
    return dx, G


U_COLS = (2048, 512)


def _even_fwd(x, P):
    S = x.shape[0]
    h0 = _rms_fwd(x, P['mix_norm'], "l0_mix_norm")
    proj = _matmul(h0, P['w_in'], name="l0_in")
    tabs = _ret_tables(S)
    o_raw, rstates = _ret_fwd(proj, tabs)
    o = _rowwise(_ret_post_fn, [_blk(o_raw), _blk(proj, 512, 3)], [P['ret_norm']], [512], "l0_ret_post",
                 out_dtypes=[BF16])[0]
    prep_in = (P['s5_lambda_re'], P['s5_lambda_im'], P['s5_log_dt'], P['s5_b_re'], P['s5_b_im'], P['s5_c_re'],
               P['s5_c_im'])
    a_re, a_im, bd_re, bd_im, cd_re, cd_im = _s5_prep(prep_in)
    a_re_f, a_im_f = a_re.reshape(1, 2048), a_im.reshape(1, 2048)
    bu_re = _matmul(proj, bd_re, name="l0_s5_bu_re", a_cols=U_COLS)
    bu_im = _matmul(proj, bd_im, name="l0_s5_bu_im", a_cols=U_COLS)
    st_re, st_im = _scan_fwd(bu_re, bu_im, a_re_f, a_im_f)
    y1 = _matmul(st_re, cd_re, name="l0_s5_y_re")
    y2 = _matmul(st_im, cd_im, name="l0_s5_y_im")
    yg = _rowwise(_s5_post_fn, [_blk(y1), _blk(y2), _blk(proj, 512, 4)],
                  [P['s5_d'], P['s5_w_glu'], P['s5_b_glu']], [512], "l0_s5_post", out_dtypes=[BF16])[0]
    x1 = _matmul_cat([o, yg], P['w_out'], "nn", res=x, name="l0_out")
    saved = (x, h0, proj, tabs, o_raw, rstates, prep_in, a_re_f, a_im_f, bd_re, bd_im, cd_re, cd_im, st_re, st_im,
             y1, y2, o, yg)
    return x1, saved


def _even_bwd(saved, dx1, P):
    (x, h0, proj, tabs, o_raw, rstates, prep_in, a_re_f, a_im_f, bd_re, bd_im, cd_re, cd_im, st_re, st_im, y1, y2,
     o, yg) = saved
    G = {}
    dmerged = _matmul(dx1, P['w_out'], "nt", name="l0_out_dx")
    G['w_out'] = jnp.concatenate([_matmul(o, dx1, "tn", name="l0_out_dw_ret"),
                                  _matmul(yg, dx1, "tn", name="l0_out_dw_s5")], axis=0)
    do_raw, dgate, G['ret_norm'] = _rowwise_bwd(
        _ret_post_fn, [_blk(o_raw), _blk(proj, 512, 3)], [P['ret_norm']], [_blk(dmerged, 512, 0)], "l0_ret_post_bwd",
        out_dtypes=[F32, BF16])
    dq, dk, dv = _ret_bwd(proj, tabs, rstates, do_raw)
    dy1, dy2, du_a, G['s5_d'], G['s5_w_glu'], G['s5_b_glu'] = _rowwise_bwd(
        _s5_post_fn, [_blk(y1), _blk(y2), _blk(proj, 512, 4)], [P['s5_d'], P['s5_w_glu'], P['s5_b_glu']],
        [_blk(dmerged, 512, 1)], "l0_s5_post_bwd", out_dtypes=[BF16, BF16, F32])
    dst_re = _matmul(dy1, cd_re, "nt", name="l0_s5_y_re_dx")
    dcd_re = _matmul(st_re, dy1, "tn", name="l0_s5_y_re_dw")
    dst_im = _matmul(dy2, cd_im, "nt", name="l0_s5_y_im_dx")
    dcd_im = _matmul(st_im, dy2, "tn", name="l0_s5_y_im_dw")
    dbu_re, dbu_im, da_re, da_im = _scan_bwd(dst_re, dst_im, st_re, st_im, a_re_f, a_im_f)
    du = _matmul(dbu_re, bd_re, "nt", res=du_a, name="l0_s5_bu_re_dx")
    du = _matmul(dbu_im, bd_im, "nt", res=du, name="l0_s5_bu_im_dx", out_dtype=BF16)
    dbd_re = _matmul(proj, dbu_re, "tn", name="l0_s5_bu_re_dw", a_cols=U_COLS)
    dbd_im = _matmul(proj, dbu_im, "tn", name="l0_s5_bu_im_dw", a_cols=U_COLS)
    dprep = _s5_prep(prep_in, cots=(da_re.reshape(32, 64), da_im.reshape(32, 64), dbd_re, dbd_im, dcd_re, dcd_im))
    for n, d in zip(('s5_lambda_re', 's5_lambda_im', 's5_log_dt', 's5_b_re', 's5_b_im', 's5_c_re', 's5_c_im'), dprep):
        G[n] = d
    pieces = [dq, dk, dv, dgate, du]
    dh0 = _matmul_cat(pieces, P['w_in'], "nt", name="l0_in_dx")
    G['w_in'] = jnp.concatenate([_matmul(h0, p, "tn", name="l0_in_dw_%d" % n) for n, p in enumerate(pieces)], axis=1)
    dx, G['mix_norm'] = _rms_bwd(x, P['mix_norm'], dh0, "l0_mix_norm_bwd", add=dx1)
    return dx, G


def _odd_fwd(x, P):
    h1 = _rms_fwd(x, P['mix_norm'], "l1_mix_norm")
    pm = _matmul(h1, P['w_main'], name="l1_in_main")
    pt = _matmul(h1, P['w_tail'], name="l1_in_tail")
    qkv = _conv_post([(pm, 0, P['conv'], 0)], _silu, 3, 1024, "l1_conv")
    g_e, beta_e = _rowwise(_gdn_gates_fn, [_blk(pt)], [P['a_log_p'], P['dtb_p']], [1024, 1024], "l1_gdn_gates")
    w, u, qd, kd, qk, tinv = _gdn_intra_call(qkv, g_e, beta_e)
    o_raw, gstates = _gdn_pass(w, u, qd, kd, qk, g_e)
    og = _rowwise(_gdn_post_fn, [_blk(o_raw), _blk(pm, 1024, 3)], [P['o_norm']], [1024], "l1_gdn_post",
                  out_dtypes=[BF16])[0]
    x1 = _matmul(og, P['w_out'], res=x, name="l1_out")
    return x1, (x, h1, pm, pt, qkv, g_e, beta_e, w, u, qd, kd, qk, tinv, o_raw, gstates, og)


def _odd_bwd(saved, dx1, P):
    x, h1, pm, pt, qkv, g_e, beta_e, w, u, qd, kd, qk, tinv, o_raw, gstates, og = saved
    G = {}
    dog = _matmul(dx1, P['w_out'], "nt", name="l1_out_dx")
    G['w_out'] = _matmul(og, dx1, "tn", name="l1_out_dw")
    do_raw, dz, G['o_norm'] = _rowwise_bwd(_gdn_post_fn, [_blk(o_raw), _blk(pm, 1024, 3)], [P['o_norm']], [dog],
                                           "l1_gdn_post_bwd", out_dtypes=[F32, BF16])
    dw, du, dqd, dkd, dqk, dg_pass = _gdn_pass(w, u, qd, kd, qk, g_e, states=gstates, do=do_raw)
    dqkv = _gdn_intra_call(qkv, g_e, beta_e, cots=(dw, du, dqd, dkd, dqk, dg_pass, tinv))
    dg_e, dbeta_e = dqkv[3], dqkv[4]
    dpt, G['a_log_p'], G['dtb_p'] = _rowwise_bwd(_gdn_gates_fn, [_blk(pt)], [P['a_log_p'], P['dtb_p']],
                                                 [dg_e, dbeta_e], "l1_gdn_gates_bwd", out_dtypes=[BF16])
    pieces, dcw = [], []
    for part in range(3):
        (dc,) = _conv_post([(pm, part, P['conv'], part)], _silu, 1, 1024, "l1_conv_dpost_%d" % part, cots=dqkv[part])
        dxp, dwp = _conv_bwd(dc, pm, part, P['conv'], part, 1, 1024, "l1_conv_bwd_%d" % part)
        pieces.append(dxp)
        dcw.append(dwp)
    G['conv'] = jnp.concatenate(dcw, axis=1)
    pieces += [dz, dpt]
    dh1 = _matmul_cat(pieces, P['w_all'], "nt", name="l1_in_dx")
    G['w_all'] = jnp.concatenate([_matmul(h1, p, "tn", name="l1_in_dw_%d" % n) for n, p in enumerate(pieces)], axis=1)
    dx, G['mix_norm'] = _rms_bwd(x, P['mix_norm'], dh1, "l1_mix_norm_bwd", add=dx1)
    return dx, G


def _row(v):
    return v.reshape(1, -1)


def _local_step(x, mem, target, W):
    P0 = {
        'mix_norm': _row(W['l0_mix_norm']), 'w_in': W['l0_w_in'], 'ret_norm': _row(W['l0_ret_norm']),
        's5_lambda_re': W['l0_s5_lambda_re'], 's5_lambda_im': W['l0_s5_lambda_im'],
        's5_log_dt': W['l0_s5_log_dt'].reshape(32, 1),
        's5_b_re': W['l0_s5_b_re'].reshape(512, 64), 's5_b_im': W['l0_s5_b_im'].reshape(512, 64),
        's5_c_re': W['l0_s5_c_re'].reshape(2048, 16), 's5_c_im': W['l0_s5_c_im'].reshape(2048, 16),
        's5_d': _row(W['l0_s5_d']), 's5_w_glu': W['l0_s5_w_glu'].astype(F32), 's5_b_glu': _row(W['l0_s5_b_glu']),
        'w_out': W['l0_w_out'],
    }
    w_in1 = W['l1_w_in']
    pad8 = jnp.zeros((8,), F32)
    w_all = jnp.pad(w_in1, ((0, 0), (0, 112)))
    P1 = {
        'mix_norm': _row(W['l1_mix_norm']), 'w_main': w_in1[:, :4096], 'w_tail': w_all[:, 4096:], 'w_all': w_all,
        'conv': W['l1_conv'],
        'a_log_p': _row(jnp.concatenate([pad8, W['l1_a_log'], jnp.zeros((112,), F32)])),
        'dtb_p': _row(jnp.concatenate([pad8, W['l1_dt_bias'], jnp.zeros((112,), F32)])),
        'o_norm': _row(W['l1_o_norm']), 'w_out': W['l1_w_out'],
    }
    C = []
    for L in ('l0_', 'l1_'):
        C.append({'xa_norm': _row(W[L + 'xa_norm']), 'mem_norm': _row(W[L + 'mem_norm']), 'xa_wq': W[L + 'xa_wq'],
                  'xa_wkv': W[L + 'xa_wkv'], 'xa_wo': W[L + 'xa_wo'], 'ffn_norm': _row(W[L + 'ffn_norm']),
                  'ffn_w_up': W[L + 'ffn_w_up'], 'ffn_conv': W[L + 'ffn_conv'], 'ffn_w_down': W[L + 'ffn_w_down']})

    x1, s_even = _even_fwd(x, P0)
    x3, s_c0 = _common_fwd(x1, mem, C[0], "l0_")
    x4, s_odd = _odd_fwd(x3, P1)
    x6, s_c1 = _common_fwd(x4, mem, C[1], "l1_")
    loss_tile, dx6, d_final = _loss_grad(x6, target, _row(W['final_norm']))

    G = {'final_norm': d_final.reshape(-1)}
    dx4, g = _common_bwd(s_c1, dx6, C[1], "l1_")
    for k, v in g.items():
        G['l1_' + k] = v
    dx3, g = _odd_bwd(s_odd, dx4, P1)
    G['l1_mix_norm'] = g['mix_norm']
    G['l1_w_in'] = g['w_all'][:, :4112]
    G['l1_conv'] = g['conv']
    G['l1_a_log'] = g['a_log_p'][0, 8:16]
    G['l1_dt_bias'] = g['dtb_p'][0, 8:16]
    G['l1_o_norm'] = g['o_norm']
    G['l1_w_out'] = g['w_out']
    dx1, g = _common_bwd(s_c0, dx3, C[0], "l0_")
    for k, v in g.items():
        G['l0_' + k] = v
    dx0, g = _even_bwd(s_even, dx1, P0)
    for k, v in g.items():
        G['l0_' + k] = v
    return loss_tile, dx0, G


ANY = pl.BlockSpec(memory_space=pl.ANY)


def _place():
    return lax.axis_index("x"), lax.axis_index("y"), lax.axis_index("c")


def _chip_peers(x, y):
    return [(1 - x, y), (x, 1 - y), (1 - x, 1 - y)]


def _half(ref, mode, shard, j, h, split):
    r, w = shard
    rh = r // 2 if split else r
    h = h if split else 0
    if mode == 'row':
        return ref.at[pl.ds(j * r + h * rh, rh), :]
    if mode == 'col':
        return ref.at[pl.ds(h * rh, rh), pl.ds(j * w, w)]
    return ref.at[j, pl.ds(h * rh, rh), :]


def _place_shard(shard, mode, me_arr, name):
    r, w = shard.shape
    dtype = BF16 if mode != 'tap' else shard.dtype
    if mode == 'tap':
        mode = 'slab'
    tr = _row_tile(r, w)
    nb = r // tr

    def kern(me_ref, s_ref, o_ref):
        o_ref[...] = s_ref[...].astype(o_ref.dtype)

    if mode == 'row':
        full, o_spec = (4 * r, w), pl.BlockSpec((tr, w), lambda i, me: (me[0] * nb + i, 0))
    elif mode == 'col':
        full, o_spec = (r, 4 * w), pl.BlockSpec((tr, w), lambda i, me: (i, me[0]))
    else:
        full, o_spec = (4, r, w), pl.BlockSpec((None, tr, w), lambda i, me: (me[0], i, 0))
    grid_spec = pltpu.PrefetchScalarGridSpec(
        num_scalar_prefetch=1, grid=(nb,), in_specs=[pl.BlockSpec((tr, w), lambda i, me: (i, 0))], out_specs=o_spec)
    return pl.pallas_call(kern, name=name, grid_spec=grid_spec, out_shape=jax.ShapeDtypeStruct(full, dtype),
                          compiler_params=_cparams(("parallel",)))(me_arr, shard)


def _gather_placed(fulls, modes, shards, splits):
    n = len(fulls)

    def body(*refs):
        outs = refs[n:2 * n]
        send_sems, recv_sems = refs[2 * n:]
        x, y, c = _place()
        peers = _chip_peers(x, y)
        me = 2 * x + y

        def win(a, j, h):
            return _half(outs[a], modes[a], shards[a], j, h, splits[a])

        def copy(a, k, j, h, to):
            return pltpu.make_async_remote_copy(src_ref=win(a, j, h), dst_ref=win(a, j, h),
                                                send_sem=send_sems.at[6 * a + k], recv_sem=recv_sems.at[6 * a + k],
                                                device_id=to, device_id_type=MESH)

        over_ici = [copy(a, k, me, c, (p[0], p[1], c)) for a in range(n) for k, p in enumerate(peers)]
        for cp in over_ici:
            cp.start()
        passed = []
        for a in range(n):
            for k, p in enumerate(peers):
                j = 2 * p[0] + p[1]
                copy(a, k, j, c, (p[0], p[1], c)).wait_recv()
                if splits[a]:
                    fwd = copy(a, 3 + k, j, c, (x, y, 1 - c))
                    fwd.start()
                    passed.append(fwd)
        for a in range(n):
            if splits[a]:
                for k, p in enumerate(peers):
                    copy(a, 3 + k, 2 * p[0] + p[1], 1 - c, (x, y, 1 - c)).wait_recv()
        for cp in over_ici + passed:
            cp.wait_send()

    return pl.pallas_call(
        body, name="gather_weights", in_specs=[ANY] * n, out_specs=[ANY] * n,
        out_shape=[jax.ShapeDtypeStruct(f.shape, f.dtype) for f in fulls],
        input_output_aliases={a: a for a in range(n)},
        scratch_shapes=[pltpu.SemaphoreType.DMA((6 * n,)), pltpu.SemaphoreType.DMA((6 * n,))],
    )(*fulls)


def _send_other_half(gs):
    n = len(gs)

    def body(*refs):
        ins, outs = refs[:n], refs[n:2 * n]
        send_sems, recv_sems = refs[2 * n:]
        x, y, c = _place()
        cps = []
        for a in range(n):
            rh = gs[a].shape[1] // 2
            cps.append(pltpu.make_async_remote_copy(
                src_ref=ins[a].at[:, pl.ds((1 - c) * rh, rh), :], dst_ref=outs[a], send_sem=send_sems.at[a],
                recv_sem=recv_sems.at[a], device_id=(x, y, 1 - c), device_id_type=MESH))
        for cp in cps:
            cp.start()
        for cp in cps:
            cp.wait()

    return pl.pallas_call(
        body, name="send_other_half", in_specs=[ANY] * n, out_specs=[ANY] * n,
        out_shape=[jax.ShapeDtypeStruct((g.shape[0], g.shape[1] // 2, g.shape[2]), g.dtype) for g in gs],
        scratch_shapes=[pltpu.SemaphoreType.DMA((n,)), pltpu.SemaphoreType.DMA((n,))],
    )(*gs)


def _send_to_chips(ps, widths):
    n = len(ps)

    def body(*refs):
        ins, outs = refs[:n], refs[n:2 * n]
        send_sems, recv_sems = refs[2 * n:]
        x, y, c = _place()
        peers = _chip_peers(x, y)
        me = 2 * x + y

        def src(a, j):
            if ps[a].shape[0] == 4:
                return ins[a].at[j]
            return ins[a].at[0, :, pl.ds(j * widths[a], widths[a])]

        def copy(a, k, j, dst_slab, to):
            return pltpu.make_async_remote_copy(src_ref=src(a, j), dst_ref=outs[a].at[dst_slab],
                                                send_sem=send_sems.at[3 * a + k], recv_sem=recv_sems.at[3 * a + k],
                                                device_id=(to[0], to[1], c), device_id_type=MESH)

        sends = [copy(a, k, 2 * p[0] + p[1], me, p) for a in range(n) for k, p in enumerate(peers)]
        for cp in sends:
            cp.start()
        for a in range(n):
            for k, p in enumerate(peers):
                copy(a, k, me, 2 * p[0] + p[1], p).wait_recv()
        for cp in sends:
            cp.wait_send()

    return pl.pallas_call(
        body, name="send_to_chips", in_specs=[ANY] * n, out_specs=[ANY] * n,
        out_shape=[jax.ShapeDtypeStruct((4, p.shape[1], w), p.dtype) for p, w in zip(ps, widths)],
        scratch_shapes=[pltpu.SemaphoreType.DMA((3 * n,)), pltpu.SemaphoreType.DMA((3 * n,))],
    )(*ps)


def _share_halves(bufs):
    n = len(bufs)

    def body(*refs):
        outs = refs[n:2 * n]
        send_sems, recv_sems = refs[2 * n:]
        x, y, c = _place()
        sends, waits = [], []
        for a in range(n):
            rh = bufs[a].shape[0] // 2
            mine = outs[a].at[pl.ds(c * rh, rh), :]
            other = outs[a].at[pl.ds((1 - c) * rh, rh), :]
            sends.append(pltpu.make_async_remote_copy(src_ref=mine, dst_ref=mine, send_sem=send_sems.at[a],
                                                      recv_sem=recv_sems.at[a], device_id=(x, y, 1 - c),
                                                      device_id_type=MESH))
            waits.append(pltpu.make_async_remote_copy(src_ref=mine, dst_ref=other, send_sem=send_sems.at[a],
                                                      recv_sem=recv_sems.at[a], device_id=(x, y, 1 - c),
                                                      device_id_type=MESH))
        for cp in sends:
            cp.start()
        for cp in waits:
            cp.wait()

    return pl.pallas_call(
        body, name="share_halves", in_specs=[ANY] * n, out_specs=[ANY] * n,
        out_shape=[jax.ShapeDtypeStruct(b.shape, b.dtype) for b in bufs],
        input_output_aliases={a: a for a in range(n)},
        scratch_shapes=[pltpu.SemaphoreType.DMA((n,)), pltpu.SemaphoreType.DMA((n,))],
    )(*bufs)


def _gather_all(mine):
    flips = [(dx, dy, dc) for dx in (0, 1) for dy in (0, 1) for dc in (0, 1) if (dx, dy, dc) != (0, 0, 0)]

    def body(x_ref, out_ref, send_sems, recv_sems, local_sem):
        x, y, c = _place()
        me = 4 * x + 2 * y + c

        def peer(f):
            return (x ^ f[0], y ^ f[1], c ^ f[2])

        def copy(k, slab, to):
            return pltpu.make_async_remote_copy(src_ref=x_ref, dst_ref=out_ref.at[slab], send_sem=send_sems.at[k],
                                                recv_sem=recv_sems.at[k], device_id=to, device_id_type=MESH)

        own = pltpu.make_async_copy(x_ref, out_ref.at[me], local_sem)
        own.start()
        sends = [copy(k, me, peer(f)) for k, f in enumerate(flips)]
        for s in sends:
            s.start()
        for k, f in enumerate(flips):
            p = peer(f)
            copy(k, 4 * p[0] + 2 * p[1] + p[2], p).wait_recv()
        for s in sends:
            s.wait_send()
        own.wait()

    return pl.pallas_call(
        body, name="gather_all", in_specs=[ANY], out_specs=ANY,
        out_shape=jax.ShapeDtypeStruct((8,) + mine.shape, mine.dtype),
        scratch_shapes=[pltpu.SemaphoreType.DMA((7,)), pltpu.SemaphoreType.DMA((7,)), pltpu.SemaphoreType.DMA],
    )(mine)


TILE_BYTES = 2 * 1024 * 1024


def _row_tile(rows, width=1024):
    for t in (512, 352, 256, 176, 128, 64, 32, 16, 8):
        if rows % t == 0 and t * width * 4 <= TILE_BYTES:
            return t
    return rows


def _pair_sum(g, got, c_arr, name):
    ns, r, w = g.shape
    rh = r // 2
    tr = _row_tile(rh, w)
    nb = rh // tr

    def kern(c_ref, g_ref, o_ref, out_ref):
        out_ref[...] = (g_ref[...] + o_ref[...]).astype(BF16)

    grid_spec = pltpu.PrefetchScalarGridSpec(
        num_scalar_prefetch=1, grid=(ns, nb),
        in_specs=[pl.BlockSpec((None, tr, w), lambda j, i, c_ref: (j, c_ref[0] * nb + i, 0)),
                  pl.BlockSpec((None, tr, w), lambda j, i, c_ref: (j, i, 0))],
        out_specs=pl.BlockSpec((None, tr, w), lambda j, i, c_ref: (j, i, 0)))
    return pl.pallas_call(kern, name=name, grid_spec=grid_spec, out_shape=jax.ShapeDtypeStruct((ns, rh, w), BF16),
                          compiler_params=_cparams(("parallel", "parallel")))(c_arr, g, got)


def _chip_sum(pair, recv, w, mc_arr, name):
    rh = pair.shape[1]
    tr = _row_tile(rh, w)
    nb = rh // tr

    def kern(mc_ref, own_ref, r1_ref, r2_ref, r3_ref, out_ref):
        acc = own_ref[...].astype(F32)
        for r_ref in (r1_ref, r2_ref, r3_ref):
            acc = acc + r_ref[...].astype(F32)
        out_ref[...] = acc

    if pair.shape[0] == 4:
        own_spec = pl.BlockSpec((None, tr, w), lambda i, mc: (mc[0], i, 0))
    else:
        own_spec = pl.BlockSpec((None, tr, w), lambda i, mc: (0, i, mc[0]))
    recv_specs = [pl.BlockSpec((None, tr, w), functools.partial(lambda i, mc, d: ((mc[0] + d) % 4, i, 0), d=d))
                  for d in (1, 2, 3)]
    grid_spec = pltpu.PrefetchScalarGridSpec(
        num_scalar_prefetch=1, grid=(nb,), in_specs=[own_spec] + recv_specs,
        out_specs=pl.BlockSpec((tr, w), lambda i, mc: (mc[1] * nb + i, 0)))
    return pl.pallas_call(kern, name=name, grid_spec=grid_spec, out_shape=jax.ShapeDtypeStruct((2 * rh, w), F32),
                          compiler_params=_cparams(("parallel",)))(mc_arr, pair, recv, recv, recv)


def _slab_sum(slabs, name):
    n, R, w = slabs.shape
    tr = _row_tile(R)

    def kern(s_ref, o_ref):
        acc = s_ref[0].astype(F32)
        for k in range(1, n):
            acc = acc + s_ref[k].astype(F32)
        o_ref[...] = acc

    return pl.pallas_call(
        kern, name=name, grid=(R // tr,), in_specs=[pl.BlockSpec((n, tr, w), lambda i: (0, i, 0))],
        out_specs=pl.BlockSpec((tr, w), lambda i: (i, 0)), out_shape=jax.ShapeDtypeStruct((R, w), F32),
        compiler_params=_cparams(("parallel",)))(slabs)


def _adamw(w, g, m, v, name):
    R, C = w.shape
    tr = _pick(R, (256, 128, 64, 32, 16, 8))

    def kern(w_ref, g_ref, m_ref, v_ref, d_ref, nm_ref, nv_ref):
        gv = g_ref[...]
        m2 = ADAM_B1 * m_ref[...] + (1.0 - ADAM_B1) * gv
        v2 = ADAM_B2 * v_ref[...] + (1.0 - ADAM_B2) * jnp.square(gv)
        m_hat = m2 / (1.0 - ADAM_B1 ** ADAM_STEP)
        v_hat = v2 / (1.0 - ADAM_B2 ** ADAM_STEP)
        d_ref[...] = -ADAM_LR * (m_hat / (jnp.sqrt(v_hat) + ADAM_EPS) + ADAM_WD * w_ref[...])
        nm_ref[...] = m2
        nv_ref[...] = v2

    spec = pl.BlockSpec((tr, C), lambda i: (i, 0))
    return pl.pallas_call(
        kern, name=name, grid=(R // tr,), in_specs=[spec] * 4, out_specs=[spec] * 3,
        out_shape=[jax.ShapeDtypeStruct((R, C), F32)] * 3, compiler_params=_cparams(("parallel",)))(w, g, m, v)


def _pack_small(vals):
    flat = jnp.concatenate([vals[n].astype(F32).reshape(-1) for n in SMALL_NAMES])
    rows = -(-flat.shape[0] // (8 * LANES)) * 8
    return jnp.pad(flat, (0, rows * LANES - flat.shape[0])).reshape(rows, LANES)


def _unpack_small(packed, shapes):
    flat = packed.reshape(-1)
    out = {}
    off = 0
    for n in SMALL_NAMES:
        size = int(np.prod(shapes[n]))
        out[n] = flat[off:off + size].reshape(shapes[n])
        off += size
    return out


def _gather_weights(A):
    names = MATRIX_NAMES + list(CONVS)
    me_arr = jnp.reshape(2 * lax.axis_index("x") + lax.axis_index("y"), (1,)).astype(jnp.int32)
    modes = [_matrix_mode(n) for n in MATRIX_NAMES] + ['slab'] * len(CONVS)
    fulls = [_place_shard(A[n], 'tap' if n in CONVS else m, me_arr, "place_" + n) for n, m in zip(names, modes)]
    outs = _gather_placed(fulls, modes, [A[n].shape for n in names], [n not in CONVS for n in names])
    W = {}
    for n, m, o in zip(names, modes, outs):
        W[n] = jnp.concatenate([o[j] for j in range(4)], axis=1) if m == 'slab' else o
    return W


def _matrix_mode(n):
    return 'slab' if n == 'l1_w_in' else ('row' if MATRICES[n] == 0 else 'col')


def _reduce_matrix_grads(G):
    c = lax.axis_index("c")
    me = 2 * lax.axis_index("x") + lax.axis_index("y")
    c_arr = jnp.reshape(c, (1,)).astype(jnp.int32)
    mc_arr = jnp.stack([me, c]).astype(jnp.int32)
    gs, widths = [], []
    for n in MATRIX_NAMES:
        g = G[n]
        mode = _matrix_mode(n)
        if mode == 'row':
            gs.append(g.reshape(4, g.shape[0] // 4, g.shape[1]))
            widths.append(g.shape[1])
        elif mode == 'col':
            gs.append(g[None])
            widths.append(g.shape[1] // 4)
        else:
            gs.append(g.reshape(g.shape[0], 4, g.shape[1] // 4).transpose(1, 0, 2))
            widths.append(g.shape[1] // 4)
    got = _send_other_half(gs)
    pairs = [_pair_sum(g, o, c_arr, "pair_sum_" + n) for n, g, o in zip(MATRIX_NAMES, gs, got)]
    recv = _send_to_chips(pairs, widths)
    halves = [_chip_sum(p, r, w, mc_arr, "chip_sum_" + n) for n, p, r, w in zip(MATRIX_NAMES, pairs, recv, widths)]
    return dict(zip(MATRIX_NAMES, _share_halves(halves)))


def kernel(*args):
    A = dict(zip(ARG_NAMES, args, strict=True))
    x, mem, target = A['x'][0], A['mem'][0], A['loss_target'][0]

    W = _gather_weights(A)
    for n in SMALL_NAMES:
        if n not in CONVS:
            W[n] = A[n]

    loss_tile, grad_x, G = _local_step(x, mem, target, W)
    loss = lax.psum(loss_tile[0, 0], ("x", "y", "c"))

    g_mat = _reduce_matrix_grads(G)

    g_small = _slab_sum(_gather_all(_pack_small({n: G[n] for n in SMALL_NAMES})), "sum_small")
    g_small = _unpack_small(g_small, {n: G[n].shape for n in SMALL_NAMES})
    me = 2 * lax.axis_index("x") + lax.axis_index("y")
    for n in CONVS:
        wd = A[n].shape[1]
        g_small[n] = lax.dynamic_slice_in_dim(g_small[n], me * wd, wd, axis=1)
    flat_names = [n for n in SMALL_NAMES if n not in CONVS]

    def pack_flat(prefix):
        return _pack_small_flat({n: A[prefix + n] for n in flat_names}, flat_names)

    shapes = {n: A[n].shape for n in flat_names}
    d_s, m_s, v_s = _adamw(pack_flat(''), _pack_small_flat(g_small, flat_names), pack_flat('m_'), pack_flat('v_'),
                           "adamw_small")
    d_s, m_s, v_s = (_unpack_flat(p, shapes, flat_names) for p in (d_s, m_s, v_s))

    grads, deltas, new_m, new_v = {}, {}, {}, {}
    for n in WEIGHTS:
        if n in MATRICES or n in CONVS:
            grads[n] = g_mat[n] if n in MATRICES else g_small[n]
            deltas[n], new_m[n], new_v[n] = _adamw(A[n], grads[n], A['m_' + n], A['v_' + n], "adamw_" + n)
        else:
            grads[n] = g_small[n].reshape(A[n].shape)
            deltas[n], new_m[n], new_v[n] = d_s[n], m_s[n], v_s[n]
    return (loss, grad_x[None], *[grads[n] for n in WEIGHTS], *[deltas[n] for n in WEIGHTS],
            *[new_m[n] for n in WEIGHTS], *[new_v[n] for n in WEIGHTS])


def _pack_small_flat(vals, names):
    flat = jnp.concatenate([vals[n].astype(F32).reshape(-1) for n in names])
    rows = -(-flat.shape[0] // (8 * LANES)) * 8
    return jnp.pad(flat, (0, rows * LANES - flat.shape[0])).reshape(rows, LANES)


def _unpack_flat(packed, shapes, names):
    flat = packed.reshape(-1)
    out = {}
    off = 0
    for n in names:
        size = int(np.prod(shapes[n]))
        out[n] = flat[off:off + size].reshape(shapes[n])
        off += size
    return out
```

```python
import functools
import math

import numpy as np
import jax
import jax.numpy as jnp
from jax import lax
from jax.experimental import pallas as pl
from jax.experimental.pallas import tpu as pltpu

F32 = jnp.float32
BF16 = jnp.bfloat16
EPS = 1e-6
MESH = pl.DeviceIdType.MESH

ADAM_LR = 0.001
ADAM_B1 = 0.9
ADAM_B2 = 0.999
ADAM_EPS = 1e-08
ADAM_WD = 0.01
ADAM_STEP = 10

VMEM_LIMIT_BYTES = 56 * 1024 * 1024
LANES = 1024

WEIGHTS = ['l0_mix_norm', 'l0_w_in', 'l0_ret_norm', 'l0_s5_lambda_re', 'l0_s5_lambda_im', 'l0_s5_b_re', 'l0_s5_b_im',
           'l0_s5_c_re', 'l0_s5_c_im', 'l0_s5_d', 'l0_s5_log_dt', 'l0_s5_w_glu', 'l0_s5_b_glu', 'l0_w_out',
           'l0_xa_norm', 'l0_mem_norm', 'l0_xa_wq', 'l0_xa_wkv', 'l0_xa_wo', 'l0_ffn_norm', 'l0_ffn_w_up',
           'l0_ffn_conv', 'l0_ffn_w_down', 'l1_mix_norm', 'l1_w_in', 'l1_conv', 'l1_a_log', 'l1_dt_bias',
           'l1_o_norm', 'l1_w_out', 'l1_xa_norm', 'l1_mem_norm', 'l1_xa_wq', 'l1_xa_wkv', 'l1_xa_wo',
           'l1_ffn_norm', 'l1_ffn_w_up', 'l1_ffn_conv', 'l1_ffn_w_down', 'final_norm']
ARG_NAMES = (['x', 'mem'] + WEIGHTS + ['loss_target'] + ['m_' + w for w in WEIGHTS] + ['v_' + w for w in WEIGHTS])

MATRICES = {
    'l0_w_in': 1, 'l0_s5_w_glu': 0, 'l0_w_out': 0, 'l0_xa_wq': 0, 'l0_xa_wkv': 1, 'l0_xa_wo': 0, 'l0_ffn_w_up': 1,
    'l0_ffn_w_down': 0, 'l1_w_in': 1, 'l1_w_out': 0, 'l1_xa_wq': 0, 'l1_xa_wkv': 1, 'l1_xa_wo': 0,
    'l1_ffn_w_up': 1, 'l1_ffn_w_down': 0,
}
CONVS = ('l0_ffn_conv', 'l1_conv', 'l1_ffn_conv')
MATRIX_NAMES = [w for w in WEIGHTS if w in MATRICES]
SMALL_NAMES = [w for w in WEIGHTS if w not in MATRICES]


def _cparams(sem=None):
    return pltpu.CompilerParams(dimension_semantics=sem, vmem_limit_bytes=VMEM_LIMIT_BYTES)


def _pick(n, cands):
    for c in cands:
        if n % c == 0:
            return c
    return n


_NN = ((1,), (0,))
_NT = ((1,), (1,))
_TN = ((0,), (0,))


def _dot(a, b, dims, hi):
    if hi is not None:
        return lax.dot_general(a.astype(F32), b.astype(F32), (dims, ((), ())), precision=hi,
                               preferred_element_type=F32)
    return lax.dot_general(a.astype(BF16), b.astype(BF16), (dims, ((), ())), preferred_element_type=F32)


def _make_mm(hi):
    @jax.custom_vjp
    def nn(a, b):
        return _dot(a, b, _NN, hi)

    def nn_f(a, b):
        return nn(a, b), (a, b)

    def nn_b(r, g):
        a, b = r
        return _dot(g, b, _NT, hi), _dot(a, g, _TN, hi)

    nn.defvjp(nn_f, nn_b)

    @jax.custom_vjp
    def nt(a, b):
        return _dot(a, b, _NT, hi)

    def nt_f(a, b):
        return nt(a, b), (a, b)

    def nt_b(r, g):
        a, b = r
        return _dot(g, b, _NN, hi), _dot(g, a, _TN, hi)

    nt.defvjp(nt_f, nt_b)

    @jax.custom_vjp
    def tn(a, b):
        return _dot(a, b, _TN, hi)

    def tn_f(a, b):
        return tn(a, b), (a, b)

    def tn_b(r, g):
        a, b = r
        return _dot(b, g, _NT, hi), _dot(a, g, _NN, hi)

    tn.defvjp(tn_f, tn_b)
    return nn, nt, tn


mm, mm_nt, mm_tn = _make_mm(None)
mmh, mmh_nt, mmh_tn = _make_mm(lax.Precision.HIGHEST)
mm3, _, _ = _make_mm(lax.Precision.HIGH)


@jax.custom_vjp
def _swap_halves(x):
    return pltpu.roll(x, 64, 1)


def _swap_f(x):
    return pltpu.roll(x, 64, 1), None


def _swap_b(_, g):
    return (pltpu.roll(g, 64, 1),)


_swap_halves.defvjp(_swap_f, _swap_b)


def _silu(x):
    return x * jax.nn.sigmoid(x)


def _rms(x, g):
    return x * lax.rsqrt(jnp.mean(x * x, axis=-1, keepdims=True) + EPS) * g


def _iota(shape, dim):
    return lax.broadcasted_iota(jnp.int32, shape, dim)


def _matmul(a, b, mode="nn", res=None, name="mm", a_cols=None, out_dtype=F32):
    a_off, a_w = (0, a.shape[1]) if a_cols is None else a_cols
    if mode == "nn":
        (M, K), (K2, N) = (a.shape[0], a_w), b.shape
    elif mode == "nt":
        (M, K), (N, K2) = (a.shape[0], a_w), b.shape
    else:
        (K, M), (K2, N) = (a.shape[0], a_w), b.shape
    assert K == K2, (a.shape, b.shape, mode)
    tm = _pick(M, (512, 1408, 256, 128))
    tn = _pick(N, (1024, 1408, 512, 256, 128))
    tk = _pick(K, (1024, 1408, 512, 256, 128))
    nk = K // tk
    dims = {"nn": _NN, "nt": _NT, "tn": _TN}[mode]
    ao = a_off // (tm if mode == "tn" else tk)
    assert ao * (tm if mode == "tn" else tk) == a_off
    if mode == "nn":
        a_spec = pl.BlockSpec((tm, tk), lambda i, j, k: (i, k + ao))
        b_spec = pl.BlockSpec((tk, tn), lambda i, j, k: (k, j))
    elif mode == "nt":
        a_spec = pl.BlockSpec((tm, tk), lambda i, j, k: (i, k + ao))
        b_spec = pl.BlockSpec((tn, tk), lambda i, j, k: (j, k))
    else:
        a_spec = pl.BlockSpec((tk, tm), lambda i, j, k: (k, i + ao))
        b_spec = pl.BlockSpec((tk, tn), lambda i, j, k: (k, j))
    o_spec = pl.BlockSpec((tm, tn), lambda i, j, k: (i, j))
    has_res = res is not None

    def kern(*refs):
        if has_res:
            a_ref, b_ref, r_ref, o_ref, acc_ref = refs
        else:
            a_ref, b_ref, o_ref, acc_ref = refs
        k = pl.program_id(2)
        part = lax.dot_general(a_ref[...].astype(BF16), b_ref[...].astype(BF16), (dims, ((), ())),
                               preferred_element_type=F32)
        if nk == 1:
            o_ref[...] = (part + r_ref[...] if has_res else part).astype(o_ref.dtype)
            return

        @pl.when(k == 0)
        def _():
            acc_ref[...] = part

        @pl.when((k > 0) & (k < nk - 1))
        def _():
            acc_ref[...] += part

        @pl.when(k == nk - 1)
        def _():
            total = acc_ref[...] + part
            o_ref[...] = (total + r_ref[...] if has_res else total).astype(o_ref.dtype)

    in_specs = [a_spec, b_spec] + ([o_spec] if has_res else [])
    ops = (a, b) + ((res,) if has_res else ())
    return pl.pallas_call(
        kern, name=name, grid=(M // tm, N // tn, nk), in_specs=in_specs, out_specs=o_spec,
        out_shape=jax.ShapeDtypeStruct((M, N), out_dtype), scratch_shapes=[pltpu.VMEM((tm, tn), F32)],
        compiler_params=_cparams(("parallel", "parallel", "arbitrary")))(*ops)


def _matmul_cat(pieces, b, mode="nn", res=None, name="mmcat"):
    M = pieces[0].shape[0]
    widths = [p.shape[1] for p in pieces]
    K = sum(widths)
    N = b.shape[1] if mode == "nn" else b.shape[0]
    assert (b.shape[0] if mode == "nn" else b.shape[1]) == K
    tm = _pick(M, (256, 128))
    tn = _pick(N, (1024, 512, 256, 128))
    npc = len(pieces)
    has_res = res is not None
    dims = _NN if mode == "nn" else _NT

    def kern(*refs):
        b_ref = refs[npc]
        o_ref = refs[-1]
        acc = refs[npc + 1][...] if has_res else None
        off = 0
        for p in range(npc):
            bp = b_ref[off:off + widths[p], :] if mode == "nn" else b_ref[:, off:off + widths[p]]
            t = lax.dot_general(refs[p][...].astype(BF16), bp.astype(BF16), (dims, ((), ())),
                                preferred_element_type=F32)
            acc = t if acc is None else acc + t
            off += widths[p]
        o_ref[...] = acc

    in_specs = [pl.BlockSpec((tm, w), lambda j, i: (i, 0)) for w in widths]
    in_specs.append(pl.BlockSpec((K, tn), lambda j, i: (0, j)) if mode == "nn"
                    else pl.BlockSpec((tn, K), lambda j, i: (j, 0)))
    o_spec = pl.BlockSpec((tm, tn), lambda j, i: (i, j))
    if has_res:
        in_specs.append(o_spec)
    ops = list(pieces) + [b] + ([res] if has_res else [])
    return pl.pallas_call(
        kern, name=name, grid=(N // tn, M // tm), in_specs=in_specs, out_specs=o_spec,
        out_shape=jax.ShapeDtypeStruct((M, N), F32), compiler_params=_cparams(("parallel", "parallel")))(*ops)


def _blk(a, width=None, colblk=0):
    return (a, a.shape[1] if width is None else width, colblk)


def _row_specs(blocked, params, ts):
    specs = []
    for (_, w, cb) in blocked:
        specs.append(pl.BlockSpec((ts, w), functools.partial(lambda i, cb: (i, cb), cb=cb)))
    for p in params:
        specs.append(pl.BlockSpec(p.shape, lambda i: (0, 0)))
    return specs


def _rowwise(fn, blocked, params, out_widths, name, ts=256, out_dtypes=None):
    S = blocked[0][0].shape[0]
    ts = min(ts, S)
    nb, npar = len(blocked), len(params)
    out_dtypes = [F32] * len(out_widths) if out_dtypes is None else out_dtypes

    def kern(*refs):
        vals = [r[...] for r in refs[:nb + npar]]
        outs = fn(*vals)
        for o_ref, o in zip(refs[nb + npar:], outs):
            o_ref[...] = o.astype(o_ref.dtype)

    return pl.pallas_call(
        kern, name=name, grid=(S // ts,), in_specs=_row_specs(blocked, params, ts),
        out_specs=[pl.BlockSpec((ts, w), lambda i: (i, 0)) for w in out_widths],
        out_shape=[jax.ShapeDtypeStruct((S, w), d) for w, d in zip(out_widths, out_dtypes)],
        compiler_params=_cparams(("parallel",)))(*[b[0] for b in blocked], *params)


def _rowwise_bwd(fn, blocked, params, cots, name, blocked_grad=None, param_grad=None, adds=None, ts=256,
                 out_dtypes=None):
    S = blocked[0][0].shape[0]
    ts = min(ts, S)
    cots = [c if isinstance(c, tuple) else _blk(c) for c in cots]
    nb, npar, nc = len(blocked), len(params), len(cots)
    blocked_grad = [True] * nb if blocked_grad is None else blocked_grad
    param_grad = [True] * npar if param_grad is None else param_grad
    adds = {} if adds is None else adds
    bidx = [i for i in range(nb) if blocked_grad[i]]
    pidx = [i for i in range(npar) if param_grad[i]]
    add_keys = sorted(adds)
    n_in = nb + npar + nc + len(add_keys)

    def kern(*refs):
        i = pl.program_id(0)
        xs = [r[...] for r in refs[:nb]]
        ps = [r[...] for r in refs[nb:nb + npar]]
        gs = [r[...] for r in refs[nb + npar:nb + npar + nc]]
        add_vals = {k: refs[nb + npar + nc + n][...] for n, k in enumerate(add_keys)}
        outs = refs[n_in:]

        def f(*diff):
            full_x = list(xs)
            full_p = list(ps)
            for n, ix in enumerate(bidx):
                full_x[ix] = diff[n]
            for n, ix in enumerate(pidx):
                full_p[ix] = diff[len(bidx) + n]
            return tuple(fn(*full_x, *full_p))

        _, vjp = jax.vjp(f, *[xs[ix] for ix in bidx], *[ps[ix] for ix in pidx])
        grads = vjp(tuple(gs))
        for n, ix in enumerate(bidx):
            g = grads[n]
            if ix in add_vals:
                g = g + add_vals[ix]
            outs[n][...] = g.astype(outs[n].dtype)
        for n in range(len(pidx)):
            o_ref = outs[len(bidx) + n]

            @pl.when(i == 0)
            def _(o_ref=o_ref):
                o_ref[...] = jnp.zeros_like(o_ref)

            o_ref[...] += grads[len(bidx) + n]

    in_specs = _row_specs(blocked, params, ts)
    in_specs += _row_specs(cots, [], ts)
    in_specs += [pl.BlockSpec((ts, adds[k].shape[1]), lambda i: (i, 0)) for k in add_keys]
    out_specs = [pl.BlockSpec((ts, blocked[ix][1]), lambda i: (i, 0)) for ix in bidx]
    out_specs += [pl.BlockSpec(params[ix].shape, lambda i: (0, 0)) for ix in pidx]
    out_dtypes = [F32] * len(bidx) if out_dtypes is None else out_dtypes
    out_shape = [jax.ShapeDtypeStruct((S, blocked[ix][1]), d) for ix, d in zip(bidx, out_dtypes)]
    out_shape += [jax.ShapeDtypeStruct(params[ix].shape, F32) for ix in pidx]
    return pl.pallas_call(
        kern, name=name, grid=(S // ts,), in_specs=in_specs, out_specs=out_specs, out_shape=out_shape,
        compiler_params=_cparams(("arbitrary",)))(*[b[0] for b in blocked], *params, *[c[0] for c in cots],
                                                    *[adds[k] for k in add_keys])


def _rms_fn(x, g):
    return (_rms(x, g),)


def _head_norm(o, n_heads, dh):
    outs = []
    for h in range(n_heads):
        oh = o[:, h * dh:(h + 1) * dh]
        outs.append(oh * lax.rsqrt(jnp.mean(oh * oh, axis=-1, keepdims=True) + EPS))
    return outs


def _ret_post_fn(o_raw, gate, ret_norm):
    o = jnp.concatenate(_head_norm(o_raw, 4, 128), axis=1)
    return (o * ret_norm * _silu(gate),)


def _s5_post_fn(y1, y2, u, d, w_glu, b_glu):
    y = y1 - y2 + d * u
    y = jax.nn.gelu(y)
    return (y * jax.nn.sigmoid(mm(y, w_glu) + b_glu),)


def _xattn_fn(q, kv):
    outs = []
    for h in range(4):
        qh = q[:, h * 256:(h + 1) * 256]
        kh = kv[:, h * 256:(h + 1) * 256]
        vh = kv[:, 1024 + h * 256:1024 + (h + 1) * 256]
        s = mm_nt(qh, kh) * (256 ** -0.5)
        s = s - lax.stop_gradient(jnp.max(s, axis=-1, keepdims=True))
        p = jnp.exp(s)
        p = p / jnp.sum(p, axis=-1, keepdims=True)
        outs.append(mm(p, vh))
    return (jnp.concatenate(outs, axis=1),)


def _softplus(x):
    return jnp.maximum(x, 0.0) + jnp.log1p(jnp.exp(-jnp.abs(x)))


def _gdn_gates_fn(pt, a_log_p, dtb_p):
    rows, cols = _iota((128, 1024), 0), _iota((128, 1024), 1)
    e_b = (rows == (cols >> 7)).astype(F32)
    e_a = (rows == (cols >> 7) + 8).astype(F32)
    beta = jax.nn.sigmoid(pt)
    g = -(jnp.exp(a_log_p) * _softplus(pt + dtb_p))
    return mmh(g, e_a), mmh(beta, e_b)


def _gdn_post_fn(o_raw, z, o_norm):
    outs = _head_norm(o_raw, 8, 128)
    o = jnp.concatenate([oh * o_norm for oh in outs], axis=1)
    return (o * _silu(z),)


def _ffn_post(up, gate):
    return _silu(gate) * up


def _shift_down(cur, prev8, sh, row8):
    if sh == 0:
        return cur
    r = pltpu.roll(cur, sh, 0)
    p = pltpu.roll(prev8, sh, 0)
    top = jnp.where(row8 < sh, p, r[0:8])
    return jnp.concatenate([top, r[8:]], axis=0)


def _shift_up(cur, next8, sh, row8):
    if sh == 0:
        return cur
    ts = cur.shape[0]
    r = pltpu.roll(cur, ts - sh, 0)
    p = pltpu.roll(next8, 8 - sh, 0)
    bot = jnp.where(row8 >= 8 - sh, p, r[ts - 8:])
    return jnp.concatenate([r[:ts - 8], bot], axis=0)


def _conv_rows(cur, prev8, wrows, row8):
    k_w = len(wrows)
    out = None
    for j in range(k_w):
        t = _shift_down(cur, prev8, k_w - 1 - j, row8) * wrows[j]
        out = t if out is None else out + t
    return out


def _conv_specs(x, xoff, w, woff, ts, tc):
    r8 = ts // 8
    return [pl.BlockSpec((ts, tc), functools.partial(lambda i, j, o: (i, j + o), o=xoff)),
            pl.BlockSpec((8, tc), functools.partial(lambda i, j, o: (jnp.maximum(i * r8 - 1, 0), j + o), o=xoff)),
            pl.BlockSpec((w.shape[0], tc), functools.partial(lambda i, j, o: (0, j + o), o=woff))]


def _conv_post(srcs, post, ncol, tc, name, cots=None, ts=256, out_dtype=F32):
    S = srcs[0][0].shape[0]
    ns = len(srcs)
    bwd = cots is not None

    def kern(*refs):
        first = pl.program_id(0) == 0
        row8 = _iota((8, tc), 0)
        cs = []
        for s in range(ns):
            cur_ref, prev_ref, w_ref = refs[3 * s:3 * s + 3]
            prev = jnp.where(first, 0.0, prev_ref[...])
            wrows = [w_ref[j:j + 1, :] for j in range(w_ref.shape[0])]
            cs.append(_conv_rows(cur_ref[...], prev, wrows, row8))
        if bwd:
            g = refs[3 * ns][...]
            _, vjp = jax.vjp(lambda *c: post(*c), *cs)
            for o_ref, d in zip(refs[3 * ns + 1:], vjp(g)):
                o_ref[...] = d
        else:
            refs[3 * ns][...] = post(*cs).astype(refs[3 * ns].dtype)

    in_specs = []
    ops = []
    for (x, xoff, w, woff) in srcs:
        in_specs += _conv_specs(x, xoff, w, woff, ts, tc)
        ops += [x, x, w]
    o_spec = pl.BlockSpec((ts, tc), lambda i, j: (i, j))
    o_shape = jax.ShapeDtypeStruct((S, ncol * tc), F32)
    if bwd:
        in_specs.append(o_spec)
        ops.append(cots)
        out_specs, out_shape = [o_spec] * ns, [o_shape] * ns
    else:
        out_specs, out_shape = o_spec, jax.ShapeDtypeStruct((S, ncol * tc), out_dtype)
    return pl.pallas_call(
        kern, name=name, grid=(S // ts, ncol), in_specs=in_specs, out_specs=out_specs, out_shape=out_shape,
        compiler_params=_cparams(("parallel", "parallel")))(*ops)


def _conv_bwd(dc, x, xoff, w, woff, ncol, tc, name, ts=256):
    S = x.shape[0]
    k_w = w.shape[0]
    r8 = ts // 8
    nblk8 = S // 8
    nrow = S // ts

    def kern(dc_ref, dn_ref, x_ref, xp_ref, w_ref, dx_ref, dw_ref):
        i = pl.program_id(1)
        row8 = _iota((8, tc), 0)
        dcur = dc_ref[...]
        dnext = jnp.where(i == nrow - 1, 0.0, dn_ref[...])
        xcur = x_ref[...]
        xprev = jnp.where(i == 0, 0.0, xp_ref[...])

        @pl.when(i == 0)
        def _():
            dw_ref[...] = jnp.zeros_like(dw_ref)

        dx = None
        for j in range(k_w):
            sh = k_w - 1 - j
            wj = w_ref[j:j + 1, :]
            t = _shift_up(dcur, dnext, sh, row8) * wj
            dx = t if dx is None else dx + t
            dw_ref[j:j + 1, :] += jnp.sum(dcur * _shift_down(xcur, xprev, sh, row8), axis=0, keepdims=True)
        dx_ref[...] = dx.astype(dx_ref.dtype)

    in_specs = [pl.BlockSpec((ts, tc), lambda j, i: (i, j)),
                pl.BlockSpec((8, tc), lambda j, i: (jnp.minimum((i + 1) * r8, nblk8 - 1), j)),
                pl.BlockSpec((ts, tc), functools.partial(lambda j, i, o: (i, j + o), o=xoff)),
                pl.BlockSpec((8, tc), functools.partial(lambda j, i, o: (jnp.maximum(i * r8 - 1, 0), j + o), o=xoff)),
                pl.BlockSpec((k_w, tc), functools.partial(lambda j, i, o: (0, j + o), o=woff))]
    out_specs = [pl.BlockSpec((ts, tc), lambda j, i: (i, j)), pl.BlockSpec((k_w, tc), lambda j, i: (0, j))]
    out_shape = [jax.ShapeDtypeStruct((S, ncol * tc), BF16), jax.ShapeDtypeStruct((k_w, ncol * tc), F32)]
    return pl.pallas_call(
        kern, name=name, grid=(ncol, nrow), in_specs=in_specs, out_specs=out_specs, out_shape=out_shape,
        compiler_params=_cparams(("parallel", "arbitrary")))(dc, dc, x, x, w)


def _ret_tables(S):
    H, C, dh = 4, 128, 128
    lg = jnp.log1p(-jnp.exp2(-5.0 - jnp.arange(H, dtype=F32)))
    idx = jnp.arange(C, dtype=F32)
    diff = idx[:, None] - idx[None, :]
    causal = diff >= 0
    intra = jnp.where(causal, jnp.exp(lg[:, None, None] * jnp.where(causal, diff, 0.0)), 0.0)
    kdec = jnp.broadcast_to(jnp.exp(lg[:, None] * (C - 1 - idx))[:, :, None], (H, C, dh))
    qdec = jnp.broadcast_to(jnp.exp(lg[:, None] * (idx + 1))[:, :, None], (H, C, dh))
    cdec = jnp.broadcast_to(jnp.exp(lg * C)[:, None, None], (H, dh, dh))
    half = dh // 2
    inv = jnp.exp(-math.log(10000.0) * jnp.arange(half, dtype=F32) / half)
    ang = jnp.arange(S).astype(F32)[:, None] * inv[None, :]
    cos, sin = jnp.cos(ang), jnp.sin(ang)
    cosf = jnp.concatenate([cos, cos], axis=1)
    sinf = jnp.concatenate([-sin, sin], axis=1)
    return cosf, sinf, intra, kdec, qdec, cdec


def _ret_chunk(q, k, v, cosf, sinf, intra, kdec, qdec, cdec, state):
    qr = q * cosf + _swap_halves(q) * sinf
    kr = (k * cosf + _swap_halves(k) * sinf) * (128 ** -0.5)
    scores = mm_nt(qr, kr) * intra
    inner = mm(scores, v)
    kv = mm_tn(kr * kdec, v)
    cross = mm(qr * qdec, state)
    return inner + cross, state * cdec + kv


def _ret_specs(N, rev):
    def nn(n):
        return N - 1 - n if rev else n
    chunk = [pl.BlockSpec((128, 128), functools.partial(lambda h, n, o: (nn(n), h + o), o=o)) for o in (0, 4, 8)]
    pos = [pl.BlockSpec((128, 128), lambda h, n: (nn(n), 0))] * 2
    tabs = [pl.BlockSpec((None, 128, 128), lambda h, n: (h, 0, 0))] * 4
    st = pl.BlockSpec((None, None, 128, 128), lambda h, n: (h, nn(n), 0, 0))
    o = pl.BlockSpec((128, 128), lambda h, n: (nn(n), h))
    return chunk, pos, tabs, st, o


def _ret_fwd(proj, tabs):
    S = proj.shape[0]
    N = S // 128
    chunk, pos, tsp, st, o = _ret_specs(N, False)

    def kern(q_ref, k_ref, v_ref, c_ref, s_ref, i_ref, kd_ref, qd_ref, cd_ref, o_ref, sp_ref, st_ref):
        @pl.when(pl.program_id(1) == 0)
        def _():
            st_ref[...] = jnp.zeros_like(st_ref)

        state = st_ref[...]
        sp_ref[...] = state
        out, new = _ret_chunk(q_ref[...], k_ref[...], v_ref[...], c_ref[...], s_ref[...], i_ref[...], kd_ref[...],
                              qd_ref[...], cd_ref[...], state)
        o_ref[...] = out
        st_ref[...] = new

    return pl.pallas_call(
        kern, name="ret_fwd", grid=(4, N), in_specs=chunk + pos + tsp, out_specs=[o, st],
        out_shape=[jax.ShapeDtypeStruct((S, 512), F32), jax.ShapeDtypeStruct((4, N, 128, 128), F32)],
        scratch_shapes=[pltpu.VMEM((128, 128), F32)],
        compiler_params=_cparams(("parallel", "arbitrary")))(proj, proj, proj, *tabs)


def _ret_bwd(proj, tabs, states, do):
    S = proj.shape[0]
    N = S // 128
    chunk, pos, tsp, st, o = _ret_specs(N, True)

    def kern(q_ref, k_ref, v_ref, c_ref, s_ref, i_ref, kd_ref, qd_ref, cd_ref, sp_ref, do_ref,
             dq_ref, dk_ref, dv_ref, ds_ref):
        @pl.when(pl.program_id(1) == 0)
        def _():
            ds_ref[...] = jnp.zeros_like(ds_ref)

        consts = (c_ref[...], s_ref[...], i_ref[...], kd_ref[...], qd_ref[...], cd_ref[...])
        _, vjp = jax.vjp(lambda q, k, v, s: _ret_chunk(q, k, v, *consts, s), q_ref[...], k_ref[...], v_ref[...],
                         sp_ref[...])
        dq, dk, dv, ds = vjp((do_ref[...], ds_ref[...]))
        dq_ref[...] = dq.astype(BF16)
        dk_ref[...] = dk.astype(BF16)
        dv_ref[...] = dv.astype(BF16)
        ds_ref[...] = ds

    return pl.pallas_call(
        kern, name="ret_bwd", grid=(4, N), in_specs=chunk + pos + tsp + [st, o], out_specs=[o, o, o],
        out_shape=[jax.ShapeDtypeStruct((S, 512), BF16)] * 3, scratch_shapes=[pltpu.VMEM((128, 128), F32)],
        compiler_params=_cparams(("parallel", "arbitrary")))(proj, proj, proj, *tabs, states, do)


GDN_C = 64
GDN_H = 8


def _unit_lower_inverse(a_mats, eye):
    p = [-a for a in a_mats]
    t = [eye + x for x in p]
    for _ in range(5):
        p = [mm3(x, x) for x in p]
        t = [mm3(y, eye + x) for y, x in zip(t, p)]
    return t


@jax.custom_vjp
def _known_inverse(a_mat, t_mat):
    return t_mat


def _known_inverse_f(a_mat, t_mat):
    return t_mat, t_mat


def _known_inverse_b(t_mat, g):
    return -mmh_tn(t_mat, mmh_nt(g, t_mat)), jnp.zeros_like(t_mat)


_known_inverse.defvjp(_known_inverse_f, _known_inverse_b)


def _gdn_intra(q, k, v, g_b, beta_b, t_known=None):
    c = GDN_C
    hs = range(len(q))
    q = [x * lax.rsqrt(jnp.sum(x * x, axis=-1, keepdims=True) + EPS) * (128 ** -0.5) for x in q]
    k = [x * lax.rsqrt(jnp.sum(x * x, axis=-1, keepdims=True) + EPS) for x in k]
    ri, ci = _iota((c, c), 0), _iota((c, c), 1)
    incl = ri >= ci
    strict = ri > ci
    eye = (ri == ci).astype(F32)
    lower = incl.astype(F32)
    ones = jnp.ones((c, c), F32)
    gc_b = [mmh(lower, g) for g in g_b]
    gl_b = [mmh(ones, g) for g in g_b]
    kb = [k[h] * beta_b[h] for h in hs]
    vb = [v[h] * beta_b[h] for h in hs]
    gcc = [g[:, :c] for g in gc_b]
    decay = [jnp.where(incl, jnp.exp(jnp.where(incl, g - g.T, 0.0)), 0.0) for g in gcc]
    a_mat = [jnp.where(strict, mm_nt(kb[h], k[h]) * decay[h], 0.0) for h in hs]
    if t_known is None:
        t_mat = _unit_lower_inverse(a_mat, eye)
    else:
        t_mat = [_known_inverse(a_mat[h], t_known[h]) for h in hs]
    egc = [jnp.exp(g) for g in gc_b]
    w = [mm(t_mat[h], kb[h] * egc[h]) for h in hs]
    u = [mm(t_mat[h], vb[h]) for h in hs]
    qk = [jnp.where(incl, mm_nt(q[h], k[h]) * decay[h], 0.0) for h in hs]
    q_dec = [q[h] * egc[h] for h in hs]
    k_dec = [k[h] * jnp.exp(gl_b[h] - gc_b[h]) for h in hs]
    return w, u, q_dec, k_dec, qk, t_mat


def _gdn_step(w, u, q_dec, k_dec, qk, g_b, state):
    hs = range(len(w))
    ones = jnp.ones((128, GDN_C), F32)
    gl_s = [mmh(ones, g) for g in g_b]
    ws = [mm(w[h], state[h]) for h in hs]
    qs = [mm(q_dec[h], state[h]) for h in hs]
    v_new = [u[h] - ws[h] for h in hs]
    o = [qs[h] + mm(qk[h], v_new[h]) for h in hs]
    new = [state[h] * jnp.exp(gl_s[h]) + mm_tn(k_dec[h], v_new[h]) for h in hs]
    return o, new


def _hs(h):
    return slice(h * 128, (h + 1) * 128)


def _gdn_intra_call(qkv, g_e, beta_e, cots=None):
    S = qkv.shape[0]
    N = S // GDN_C
    bwd = cots is not None
    row = pl.BlockSpec((GDN_C, 1024), lambda n: (n, 0))
    qkv_spec = pl.BlockSpec((GDN_C, 3072), lambda n: (n, 0))
    qk_spec = pl.BlockSpec((GDN_H, GDN_C, GDN_C), lambda n: (0, n, 0))

    def kern(*refs):
        x_ref, g_ref, b_ref = refs[:3]
        heads = range(GDN_H)

        def cols(ref, off=0):
            return [ref[:, _hs(off + h)] for h in heads]

        args = (cols(x_ref), cols(x_ref, 8), cols(x_ref, 16), cols(g_ref), cols(b_ref))
        if bwd:
            dw_ref, du_ref, dqd_ref, dkd_ref, dqk_ref, dgadd_ref, t_ref = refs[3:10]
            outs = refs[10:]
            t_known = [t_ref[h] for h in heads]
            _, vjp = jax.vjp(lambda *a: _gdn_intra(*a, t_known=t_known)[:5], *args)
            dq, dk, dv, dg, db = vjp((cols(dw_ref), cols(du_ref), cols(dqd_ref), cols(dkd_ref),
                                      [dqk_ref[h] for h in heads]))
            dgadd = cols(dgadd_ref)
            for h in heads:
                for o_ref, d in zip(outs, (dq[h], dk[h], dv[h], dg[h] + dgadd[h], db[h])):
                    o_ref[:, _hs(h)] = d
        else:
            w, u, qd, kd, qk, t_mat = _gdn_intra(*args)
            for h in heads:
                for o_ref, o in zip(refs[3:7], (w[h], u[h], qd[h], kd[h])):
                    o_ref[:, _hs(h)] = o
                refs[7][h] = qk[h]
                refs[8][h] = t_mat[h]

    big = jax.ShapeDtypeStruct((S, 1024), F32)
    sq = jax.ShapeDtypeStruct((GDN_H, S, GDN_C), F32)
    if bwd:
        in_specs = [qkv_spec, row, row, row, row, row, row, qk_spec, row, qk_spec]
        out_specs, out_shape = [row] * 5, [big] * 5
        ops = (qkv, g_e, beta_e) + tuple(cots)
    else:
        in_specs = [qkv_spec, row, row]
        out_specs = [row] * 4 + [qk_spec, qk_spec]
        out_shape = [big] * 4 + [sq, sq]
        ops = (qkv, g_e, beta_e)
    return pl.pallas_call(
        kern, name="gdn_intra_bwd" if bwd else "gdn_intra", grid=(N,), in_specs=in_specs, out_specs=out_specs,
        out_shape=out_shape, compiler_params=_cparams(("parallel",)))(*ops)


def _gdn_pass(w, u, qd, kd, qk, g_e, states=None, do=None):
    S = w.shape[0]
    N = S // GDN_C
    bwd = do is not None

    def nn(n):
        return N - 1 - n if bwd else n

    row = pl.BlockSpec((GDN_C, 1024), lambda n: (nn(n), 0))
    qk_spec = pl.BlockSpec((GDN_H, GDN_C, GDN_C), lambda n: (0, nn(n), 0))
    st_spec = pl.BlockSpec((None, GDN_H, 128, 128), lambda n: (nn(n), 0, 0, 0))

    def kern(*refs):
        w_ref, u_ref, qd_ref, kd_ref, qk_ref, g_ref = refs[:6]
        carry = refs[-1]

        @pl.when(pl.program_id(0) == 0)
        def _():
            carry[...] = jnp.zeros_like(carry)

        heads = range(GDN_H)

        def cols(ref):
            return [ref[:, _hs(h)] for h in heads]

        args = (cols(w_ref), cols(u_ref), cols(qd_ref), cols(kd_ref), [qk_ref[h] for h in heads], cols(g_ref))
        if bwd:
            sp_ref, do_ref = refs[6:8]
            outs = refs[8:14]
            _, vjp = jax.vjp(_gdn_step, *args, [sp_ref[h] for h in heads])
            dw, du, dqd, dkd, dqk, dg, ds = vjp((cols(do_ref), [carry[h] for h in heads]))
            for h in heads:
                for o_ref, d in zip(outs[:4], (dw[h], du[h], dqd[h], dkd[h])):
                    o_ref[:, _hs(h)] = d
                outs[4][h] = dqk[h]
                outs[5][:, _hs(h)] = dg[h]
                carry[h] = ds[h]
        else:
            o_ref, sp_ref = refs[6:8]
            state = [carry[h] for h in heads]
            o, new = _gdn_step(*args, state)
            for h in heads:
                sp_ref[h] = state[h]
                o_ref[:, _hs(h)] = o[h]
                carry[h] = new[h]

    big = jax.ShapeDtypeStruct((S, 1024), F32)
    in_specs = [row, row, row, row, qk_spec, row]
    if bwd:
        in_specs += [st_spec, row]
        out_specs = [row] * 4 + [qk_spec, row]
        out_shape = [big] * 4 + [jax.ShapeDtypeStruct((GDN_H, S, GDN_C), F32), big]
        ops = (w, u, qd, kd, qk, g_e, states, do)
    else:
        out_specs = [row, st_spec]
        out_shape = [big, jax.ShapeDtypeStruct((N, GDN_H, 128, 128), F32)]
        ops = (w, u, qd, kd, qk, g_e)
    return pl.pallas_call(
        kern, name="gdn_pass_bwd" if bwd else "gdn_pass", grid=(N,), in_specs=in_specs, out_specs=out_specs,
        out_shape=out_shape, scratch_shapes=[pltpu.VMEM((GDN_H, 128, 128), F32)],
        compiler_params=_cparams(("arbitrary",)))(*ops)


def _s5_prep_fn(lr, li, ldt, br, bi, cr, ci):
    dt = jnp.exp(ldt)
    mag = jnp.exp(lr * dt)
    a_re = mag * jnp.cos(li * dt)
    a_im = mag * jnp.sin(li * dt)
    den = lr * lr + li * li
    z_re = ((a_re - 1.0) * lr + a_im * li) / den
    z_im = (a_im * lr - (a_re - 1.0) * li) / den
    e1 = ((_iota((512, 32), 0) >> 4) == _iota((512, 32), 1)).astype(F32)
    zr_e = mmh(e1, z_re)
    zi_e = mmh(e1, z_im)
    bb_re = zr_e * br - zi_e * bi
    bb_im = zr_e * bi + zi_e * br
    t1 = ((_iota((64, 2048), 1) & 63) == _iota((64, 2048), 0)).astype(F32)
    m1 = (_iota((512, 2048), 0) >> 4) == (_iota((512, 2048), 1) >> 6)
    bd_re = jnp.where(m1, mmh(bb_re, t1), 0.0)
    bd_im = jnp.where(m1, mmh(bb_im, t1), 0.0)
    t2 = ((_iota((16, 512), 1) & 15) == _iota((16, 512), 0)).astype(F32)
    m2 = (_iota((2048, 512), 0) >> 6) == (_iota((2048, 512), 1) >> 4)
    cd_re = jnp.where(m2, mmh(cr, t2), 0.0)
    cd_im = jnp.where(m2, mmh(ci, t2), 0.0)
    return a_re, a_im, bd_re, bd_im, cd_re, cd_im


_PREP_OUT = [(32, 64), (32, 64), (512, 2048), (512, 2048), (2048, 512), (2048, 512)]


def _s5_prep(params, cots=None):
    bwd = cots is not None

    def kern(*refs):
        vals = [r[...] for r in refs[:7]]
        if bwd:
            gs = tuple(r[...] for r in refs[7:13])
            _, vjp = jax.vjp(_s5_prep_fn, *vals)
            for o_ref, d in zip(refs[13:], vjp(gs)):
                o_ref[...] = d
        else:
            for o_ref, o in zip(refs[7:], _s5_prep_fn(*vals)):
                o_ref[...] = o

    if bwd:
        out_shape = [jax.ShapeDtypeStruct(p.shape, F32) for p in params]
        ops = list(params) + list(cots)
    else:
        out_shape = [jax.ShapeDtypeStruct(s, F32) for s in _PREP_OUT]
        ops = list(params)
    return pl.pallas_call(kern, name="s5_prep_bwd" if bwd else "s5_prep", out_shape=out_shape,
                          compiler_params=_cparams())(*ops)


def _cmul(ar, ai, br, bi):
    return ar * br - ai * bi, ar * bi + ai * br


def _power_table(ar, ai, row8, descending):
    pr, pi = ar, ai
    tr = jnp.zeros(row8.shape, F32)
    ti = jnp.zeros(row8.shape, F32)
    for n in range(8):
        r = 7 - n if descending else n
        tr = jnp.where(row8 == r, pr, tr)
        ti = jnp.where(row8 == r, pi, ti)
        if n < 7:
            pr, pi = _cmul(pr, pi, ar, ai)
    return tr, ti


def _tile_scan(xr, xi, pows, row8, up):
    for d, (pr, pi) in zip((1, 2, 4), pows):
        if up:
            sr = jnp.where(row8 < 8 - d, pltpu.roll(xr, 8 - d, 0), 0.0)
            si = jnp.where(row8 < 8 - d, pltpu.roll(xi, 8 - d, 0), 0.0)
        else:
            sr = jnp.where(row8 >= d, pltpu.roll(xr, d, 0), 0.0)
            si = jnp.where(row8 >= d, pltpu.roll(xi, d, 0), 0.0)
        mr, mi = _cmul(pr, pi, sr, si)
        xr, xi = xr + mr, xi + mi
    return xr, xi


def _pick_row(x, row8, r):
    return jnp.sum(jnp.where(row8 == r, x, 0.0), axis=0, keepdims=True)


SCAN_LB = 512
SCAN_TS = 512


def _scan_fwd(bu_re, bu_im, a_re, a_im):
    S, L = bu_re.shape
    ts, lb = min(SCAN_TS, S), SCAN_LB
    nt = ts // 8

    def kern(br_ref, bi_ref, ar_ref, ai_ref, or_ref, oi_ref, cr_ref, ci_ref):
        @pl.when(pl.program_id(1) == 0)
        def _():
            cr_ref[...] = jnp.zeros_like(cr_ref)
            ci_ref[...] = jnp.zeros_like(ci_ref)

        row8 = _iota((8, lb), 0)
        ar, ai = ar_ref[...], ai_ref[...]
        a2 = _cmul(ar, ai, ar, ai)
        a4 = _cmul(*a2, *a2)
        pows = ((ar, ai), a2, a4)
        tr, ti = _power_table(ar, ai, row8, False)

        def body(i, carry):
            cr, ci = carry
            off = pl.multiple_of(i * 8, 8)
            xr, xi = _tile_scan(br_ref[pl.ds(off, 8), :], bi_ref[pl.ds(off, 8), :], pows, row8, False)
            mr, mi = _cmul(tr, ti, cr, ci)
            xr, xi = xr + mr, xi + mi
            or_ref[pl.ds(off, 8), :] = xr
            oi_ref[pl.ds(off, 8), :] = xi
            return _pick_row(xr, row8, 7), _pick_row(xi, row8, 7)

        cr, ci = lax.fori_loop(0, nt, body, (cr_ref[...], ci_ref[...]))
        cr_ref[...] = cr
        ci_ref[...] = ci

    blk = pl.BlockSpec((ts, lb), lambda j, i: (i, j))
    par = pl.BlockSpec((1, lb), lambda j, i: (0, j))
    return pl.pallas_call(
        kern, name="s5_scan_fwd", grid=(L // lb, S // ts), in_specs=[blk, blk, par, par], out_specs=[blk, blk],
        out_shape=[jax.ShapeDtypeStruct((S, L), F32)] * 2,
        scratch_shapes=[pltpu.VMEM((1, lb), F32), pltpu.VMEM((1, lb), F32)],
        compiler_params=_cparams(("parallel", "arbitrary")))(bu_re, bu_im, a_re, a_im)


def _scan_bwd(dst_re, dst_im, st_re, st_im, a_re, a_im):
    S, L = dst_re.shape
    ts, lb = min(SCAN_TS, S), SCAN_LB
    nt = ts // 8
    nb = S // ts
    r8 = ts // 8

    def kern(dr_ref, di_ref, sr_ref, si_ref, pr_ref, pi_ref, ar_ref, ai_ref, gr_ref, gi_ref, dar_ref, dai_ref,
             cr_ref, ci_ref):
        step = pl.program_id(1)
        blk = nb - 1 - step

        @pl.when(step == 0)
        def _():
            cr_ref[...] = jnp.zeros_like(cr_ref)
            ci_ref[...] = jnp.zeros_like(ci_ref)
            dar_ref[...] = jnp.zeros_like(dar_ref)
            dai_ref[...] = jnp.zeros_like(dai_ref)

        row8 = _iota((8, lb), 0)
        ar, ai = ar_ref[...], ai_ref[...]
        nai = -ai
        a2 = _cmul(ar, nai, ar, nai)
        a4 = _cmul(*a2, *a2)
        pows = ((ar, nai), a2, a4)
        tr, ti = _power_table(ar, nai, row8, True)
        halo_r = jnp.where(blk == 0, 0.0, pr_ref[...])
        halo_i = jnp.where(blk == 0, 0.0, pi_ref[...])

        def body(n, carry):
            cr, ci, acc_r, acc_i = carry
            i = nt - 1 - n
            off = pl.multiple_of(i * 8, 8)
            gr, gi = _tile_scan(dr_ref[pl.ds(off, 8), :], di_ref[pl.ds(off, 8), :], pows, row8, True)
            mr, mi = _cmul(tr, ti, cr, ci)
            gr, gi = gr + mr, gi + mi
            gr_ref[pl.ds(off, 8), :] = gr
            gi_ref[pl.ds(off, 8), :] = gi
            poff = pl.multiple_of(jnp.maximum(i - 1, 0) * 8, 8)
            before_r = jnp.where(i == 0, halo_r, sr_ref[pl.ds(poff, 8), :])
            before_i = jnp.where(i == 0, halo_i, si_ref[pl.ds(poff, 8), :])
            last_r = _pick_row(before_r, row8, 7)
            last_i = _pick_row(before_i, row8, 7)
            spr = jnp.where(row8 >= 1, pltpu.roll(sr_ref[pl.ds(off, 8), :], 1, 0), last_r)
            spi = jnp.where(row8 >= 1, pltpu.roll(si_ref[pl.ds(off, 8), :], 1, 0), last_i)
            acc_r = acc_r + gr * spr + gi * spi
            acc_i = acc_i + gi * spr - gr * spi
            return _pick_row(gr, row8, 0), _pick_row(gi, row8, 0), acc_r, acc_i

        zero = jnp.zeros((8, lb), F32)
        cr, ci, acc_r, acc_i = lax.fori_loop(0, nt, body, (cr_ref[...], ci_ref[...], zero, zero))
        cr_ref[...] = cr
        ci_ref[...] = ci
        dar_ref[...] += jnp.sum(acc_r, axis=0, keepdims=True)
        dai_ref[...] += jnp.sum(acc_i, axis=0, keepdims=True)

    blk = pl.BlockSpec((ts, lb), lambda j, i: (nb - 1 - i, j))
    halo = pl.BlockSpec((8, lb), lambda j, i: (jnp.maximum((nb - 1 - i) * r8 - 1, 0), j))
    par = pl.BlockSpec((1, lb), lambda j, i: (0, j))
    return pl.pallas_call(
        kern, name="s5_scan_bwd", grid=(L // lb, nb), in_specs=[blk, blk, blk, blk, halo, halo, par, par],
        out_specs=[blk, blk, par, par],
        out_shape=[jax.ShapeDtypeStruct((S, L), F32)] * 2 + [jax.ShapeDtypeStruct((1, L), F32)] * 2,
        scratch_shapes=[pltpu.VMEM((1, lb), F32), pltpu.VMEM((1, lb), F32)],
        compiler_params=_cparams(("parallel", "arbitrary")))(dst_re, dst_im, st_re, st_im, st_re, st_im, a_re, a_im)


def _loss_grad(x, target, gain, ts=256):
    S, D = x.shape

    def kern(x_ref, t_ref, g_ref, loss_ref, dx_ref, dg_ref):
        i = pl.program_id(0)
        tgt = t_ref[...]

        def f(xv, gv):
            err = _rms(xv, gv) - tgt
            return 0.5 * jnp.mean(err * err, axis=-1, keepdims=True)

        rowloss, vjp = jax.vjp(f, x_ref[...], g_ref[...])
        dx, dg = vjp(jnp.ones_like(rowloss))
        dx_ref[...] = dx

        @pl.when(i == 0)
        def _():
            loss_ref[...] = jnp.zeros_like(loss_ref)
            dg_ref[...] = jnp.zeros_like(dg_ref)

        loss_ref[...] += jnp.broadcast_to(jnp.sum(rowloss, axis=0, keepdims=True), loss_ref.shape)
        dg_ref[...] += dg

    row = pl.BlockSpec((ts, D), lambda i: (i, 0))
    return pl.pallas_call(
        kern, name="loss_grad", grid=(S // ts,), in_specs=[row, row, pl.BlockSpec((1, D), lambda i: (0, 0))],
        out_specs=[pl.BlockSpec((8, 128), lambda i: (0, 0)), row, pl.BlockSpec((1, D), lambda i: (0, 0))],
        out_shape=[jax.ShapeDtypeStruct((8, 128), F32), jax.ShapeDtypeStruct((S, D), F32),
                   jax.ShapeDtypeStruct((1, D), F32)],
        compiler_params=_cparams(("arbitrary",)))(x, target, gain)


def _rms_fwd(x, g, name):
    return _rowwise(_rms_fn, [_blk(x)], [g], [x.shape[1]], name, out_dtypes=[BF16])[0]


def _rms_bwd(x, g, dy, name, add=None):
    return _rowwise_bwd(_rms_fn, [_blk(x)], [g], [dy], name, adds=None if add is None else {0: add})


FFN_TC = 1408


def _common_fwd(x, mem, P, L):
    hx = _rms_fwd(x, P['xa_norm'], L + "xa_norm")
    q = _matmul(hx, P['xa_wq'], name=L + "xa_q")
    memn = _rms_fwd(mem, P['mem_norm'], L + "mem_norm")
    kv = _matmul(memn, P['xa_wkv'], name=L + "xa_kv")
    att = _rowwise(_xattn_fn, [_blk(q)], [kv], [1024], L + "xattn", out_dtypes=[BF16])[0]
    x2 = _matmul(att, P['xa_wo'], res=x, name=L + "xa_o")
    hf = _rms_fwd(x2, P['ffn_norm'], L + "ffn_norm")
    hu = _matmul(hf, P['ffn_w_up'], name=L + "ffn_up")
    cw = P['ffn_conv']
    act = _conv_post([(hu, 0, cw, 0), (hu, 2, cw, 2)], _ffn_post, 2, FFN_TC, L + "ffn_conv", out_dtype=BF16)
    x3 = _matmul(act, P['ffn_w_down'], res=x2, name=L + "ffn_down")
    return x3, (x, mem, hx, q, memn, kv, att, x2, hf, hu, act)


def _common_bwd(saved, dx3, P, L):
    x, mem, hx, q, memn, kv, att, x2, hf, hu, act = saved
    G = {}
    dact = _matmul(dx3, P['ffn_w_down'], "nt", name=L + "ffn_down_dx")
    G['ffn_w_down'] = _matmul(act, dx3, "tn", name=L + "ffn_down_dw")
    cw = P['ffn_conv']
    dcu, dcg = _conv_post([(hu, 0, cw, 0), (hu, 2, cw, 2)], _ffn_post, 2, FFN_TC, L + "ffn_conv_dpost", cots=dact)
    dhu_u, dcw_u = _conv_bwd(dcu, hu, 0, cw, 0, 2, FFN_TC, L + "ffn_conv_bwd_up")
    dhu_g, dcw_g = _conv_bwd(dcg, hu, 2, cw, 2, 2, FFN_TC, L + "ffn_conv_bwd_gate")
    G['ffn_conv'] = jnp.concatenate([dcw_u, dcw_g], axis=1)
    dhf = _matmul_cat([dhu_u, dhu_g], P['ffn_w_up'], "nt", name=L + "ffn_up_dx")
    G['ffn_w_up'] = jnp.concatenate([_matmul(hf, dhu_u, "tn", name=L + "ffn_up_dw_up"),
                                     _matmul(hf, dhu_g, "tn", name=L + "ffn_up_dw_gate")], axis=1)
    dx2, G['ffn_norm'] = _rms_bwd(x2, P['ffn_norm'], dhf, L + "ffn_norm_bwd", add=dx3)
    datt = _matmul(dx2, P['xa_wo'], "nt", name=L + "xa_o_dx")
    G['xa_wo'] = _matmul(att, dx2, "tn", name=L + "xa_o_dw")
    dq, dkv = _rowwise_bwd(_xattn_fn, [_blk(q)], [kv], [datt], L + "xattn_bwd", out_dtypes=[BF16])
    dhx = _matmul(dq, P['xa_wq'], "nt", name=L + "xa_q_dx")
    G['xa_wq'] = _matmul(hx, dq, "tn", name=L + "xa_q_dw")
    dmemn = _matmul(dkv, P['xa_wkv'], "nt", name=L + "xa_kv_dx")
    G['xa_wkv'] = _matmul(memn, dkv, "tn", name=L + "xa_kv_dw")
    _, G['mem_norm'] = _rms_bwd(mem, P['mem_norm'], dmemn, L + "mem_norm_bwd")
    dx, G['xa_norm'] = _rms_bwd(x, P['xa_norm'], dhx, L + "xa_norm_bwd", add=dx2)
    return dx, G


U_COLS = (2048, 512)


def _even_fwd(x, P):
    S = x.shape[0]
    h0 = _rms_fwd(x, P['mix_norm'], "l0_mix_norm")
    proj = _matmul(h0, P['w_in'], name="l0_in")
    tabs = _ret_tables(S)
    o_raw, rstates = _ret_fwd(proj, tabs)
    o = _rowwise(_ret_post_fn, [_blk(o_raw), _blk(proj, 512, 3)], [P['ret_norm']], [512], "l0_ret_post",
                 out_dtypes=[BF16])[0]
    prep_in = (P['s5_lambda_re'], P['s5_lambda_im'], P['s5_log_dt'], P['s5_b_re'], P['s5_b_im'], P['s5_c_re'],
               P['s5_c_im'])
    a_re, a_im, bd_re, bd_im, cd_re, cd_im = _s5_prep(prep_in)
    a_re_f, a_im_f = a_re.reshape(1, 2048), a_im.reshape(1, 2048)
    bu_re = _matmul(proj, bd_re, name="l0_s5_bu_re", a_cols=U_COLS)
    bu_im = _matmul(proj, bd_im, name="l0_s5_bu_im", a_cols=U_COLS)
    st_re, st_im = _scan_fwd(bu_re, bu_im, a_re_f, a_im_f)
    y1 = _matmul(st_re, cd_re, name="l0_s5_y_re")
    y2 = _matmul(st_im, cd_im, name="l0_s5_y_im")
    yg = _rowwise(_s5_post_fn, [_blk(y1), _blk(y2), _blk(proj, 512, 4)],
                  [P['s5_d'], P['s5_w_glu'], P['s5_b_glu']], [512], "l0_s5_post", out_dtypes=[BF16])[0]
    x1 = _matmul_cat([o, yg], P['w_out'], "nn", res=x, name="l0_out")
    saved = (x, h0, proj, tabs, o_raw, rstates, prep_in, a_re_f, a_im_f, bd_re, bd_im, cd_re, cd_im, st_re, st_im,
             y1, y2, o, yg)
    return x1, saved


def _even_bwd(saved, dx1, P):
    (x, h0, proj, tabs, o_raw, rstates, prep_in, a_re_f, a_im_f, bd_re, bd_im, cd_re, cd_im, st_re, st_im, y1, y2,
     o, yg) = saved
    G = {}
    dmerged = _matmul(dx1, P['w_out'], "nt", name="l0_out_dx")
    G['w_out'] = jnp.concatenate([_matmul(o, dx1, "tn", name="l0_out_dw_ret"),
                                  _matmul(yg, dx1, "tn", name="l0_out_dw_s5")], axis=0)
    do_raw, dgate, G['ret_norm'] = _rowwise_bwd(
        _ret_post_fn, [_blk(o_raw), _blk(proj, 512, 3)], [P['ret_norm']], [_blk(dmerged, 512, 0)], "l0_ret_post_bwd",
        out_dtypes=[F32, BF16])
    dq, dk, dv = _ret_bwd(proj, tabs, rstates, do_raw)
    dy1, dy2, du_a, G['s5_d'], G['s5_w_glu'], G['s5_b_glu'] = _rowwise_bwd(
        _s5_post_fn, [_blk(y1), _blk(y2), _blk(proj, 512, 4)], [P['s5_d'], P['s5_w_glu'], P['s5_b_glu']],
        [_blk(dmerged, 512, 1)], "l0_s5_post_bwd", out_dtypes=[BF16, BF16, F32])
    dst_re = _matmul(dy1, cd_re, "nt", name="l0_s5_y_re_dx")
    dcd_re = _matmul(st_re, dy1, "tn", name="l0_s5_y_re_dw")
    dst_im = _matmul(dy2, cd_im, "nt", name="l0_s5_y_im_dx")
    dcd_im = _matmul(st_im, dy2, "tn", name="l0_s5_y_im_dw")
    dbu_re, dbu_im, da_re, da_im = _scan_bwd(dst_re, dst_im, st_re, st_im, a_re_f, a_im_f)
    du = _matmul(dbu_re, bd_re, "nt", res=du_a, name="l0_s5_bu_re_dx")
    du = _matmul(dbu_im, bd_im, "nt", res=du, name="l0_s5_bu_im_dx", out_dtype=BF16)
    dbd_re = _matmul(proj, dbu_re, "tn", name="l0_s5_bu_re_dw", a_cols=U_COLS)
    dbd_im = _matmul(proj, dbu_im, "tn", name="l0_s5_bu_im_dw", a_cols=U_COLS)
    dprep = _s5_prep(prep_in, cots=(da_re.reshape(32, 64), da_im.reshape(32, 64), dbd_re, dbd_im, dcd_re, dcd_im))
    for n, d in zip(('s5_lambda_re', 's5_lambda_im', 's5_log_dt', 's5_b_re', 's5_b_im', 's5_c_re', 's5_c_im'), dprep):
        G[n] = d
    pieces = [dq, dk, dv, dgate, du]
    dh0 = _matmul_cat(pieces, P['w_in'], "nt", name="l0_in_dx")
    G['w_in'] = jnp.concatenate([_matmul(h0, p, "tn", name="l0_in_dw_%d" % n) for n, p in enumerate(pieces)], axis=1)
    dx, G['mix_norm'] = _rms_bwd(x, P['mix_norm'], dh0, "l0_mix_norm_bwd", add=dx1)
    return dx, G


def _odd_fwd(x, P):
    h1 = _rms_fwd(x, P['mix_norm'], "l1_mix_norm")
    pm = _matmul(h1, P['w_main'], name="l1_in_main")
    pt = _matmul(h1, P['w_tail'], name="l1_in_tail")
    qkv = _conv_post([(pm, 0, P['conv'], 0)], _silu, 3, 1024, "l1_conv")
    g_e, beta_e = _rowwise(_gdn_gates_fn, [_blk(pt)], [P['a_log_p'], P['dtb_p']], [1024, 1024], "l1_gdn_gates")
    w, u, qd, kd, qk, tinv = _gdn_intra_call(qkv, g_e, beta_e)
    o_raw, gstates = _gdn_pass(w, u, qd, kd, qk, g_e)
    og = _rowwise(_gdn_post_fn, [_blk(o_raw), _blk(pm, 1024, 3)], [P['o_norm']], [1024], "l1_gdn_post",
                  out_dtypes=[BF16])[0]
    x1 = _matmul(og, P['w_out'], res=x, name="l1_out")
    return x1, (x, h1, pm, pt, qkv, g_e, beta_e, w, u, qd, kd, qk, tinv, o_raw, gstates, og)


def _odd_bwd(saved, dx1, P):
    x, h1, pm, pt, qkv, g_e, beta_e, w, u, qd, kd, qk, tinv, o_raw, gstates, og = saved
    G = {}
    dog = _matmul(dx1, P['w_out'], "nt", name="l1_out_dx")
    G['w_out'] = _matmul(og, dx1, "tn", name="l1_out_dw")
    do_raw, dz, G['o_norm'] = _rowwise_bwd(_gdn_post_fn, [_blk(o_raw), _blk(pm, 1024, 3)], [P['o_norm']], [dog],
                                           "l1_gdn_post_bwd", out_dtypes=[F32, BF16])
    dw, du, dqd, dkd, dqk, dg_pass = _gdn_pass(w, u, qd, kd, qk, g_e, states=gstates, do=do_raw)
    dqkv = _gdn_intra_call(qkv, g_e, beta_e, cots=(dw, du, dqd, dkd, dqk, dg_pass, tinv))
    dg_e, dbeta_e = dqkv[3], dqkv[4]
    dpt, G['a_log_p'], G['dtb_p'] = _rowwise_bwd(_gdn_gates_fn, [_blk(pt)], [P['a_log_p'], P['dtb_p']],
                                                 [dg_e, dbeta_e], "l1_gdn_gates_bwd", out_dtypes=[BF16])
    pieces, dcw = [], []
    for part in range(3):
        (dc,) = _conv_post([(pm, part, P['conv'], part)], _silu, 1, 1024, "l1_conv_dpost_%d" % part, cots=dqkv[part])
        dxp, dwp = _conv_bwd(dc, pm, part, P['conv'], part, 1, 1024, "l1_conv_bwd_%d" % part)
        pieces.append(dxp)
        dcw.append(dwp)
    G['conv'] = jnp.concatenate(dcw, axis=1)
    pieces += [dz, dpt]
    dh1 = _matmul_cat(pieces, P['w_all'], "nt", name="l1_in_dx")
    G['w_all'] = jnp.concatenate([_matmul(h1, p, "tn", name="l1_in_dw_%d" % n) for n, p in enumerate(pieces)], axis=1)
    dx, G['mix_norm'] = _rms_bwd(x, P['mix_norm'], dh1, "l1_mix_norm_bwd", add=dx1)
    return dx, G


def _row(v):
    return v.reshape(1, -1)


def _local_step(x, mem, target, W):
    P0 = {
        'mix_norm': _row(W['l0_mix_norm']), 'w_in': W['l0_w_in'], 'ret_norm': _row(W['l0_ret_norm']),
        's5_lambda_re': W['l0_s5_lambda_re'], 's5_lambda_im': W['l0_s5_lambda_im'],
        's5_log_dt': W['l0_s5_log_dt'].reshape(32, 1),
        's5_b_re': W['l0_s5_b_re'].reshape(512, 64), 's5_b_im': W['l0_s5_b_im'].reshape(512, 64),
        's5_c_re': W['l0_s5_c_re'].reshape(2048, 16), 's5_c_im': W['l0_s5_c_im'].reshape(2048, 16),
        's5_d': _row(W['l0_s5_d']), 's5_w_glu': W['l0_s5_w_glu'].astype(F32), 's5_b_glu': _row(W['l0_s5_b_glu']),
        'w_out': W['l0_w_out'],
    }
    w_in1 = W['l1_w_in']
    pad8 = jnp.zeros((8,), F32)
    w_all = jnp.pad(w_in1, ((0, 0), (0, 112)))
    P1 = {
        'mix_norm': _row(W['l1_mix_norm']), 'w_main': w_in1[:, :4096], 'w_tail': w_all[:, 4096:], 'w_all': w_all,
        'conv': W['l1_conv'],
        'a_log_p': _row(jnp.concatenate([pad8, W['l1_a_log'], jnp.zeros((112,), F32)])),
        'dtb_p': _row(jnp.concatenate([pad8, W['l1_dt_bias'], jnp.zeros((112,), F32)])),
        'o_norm': _row(W['l1_o_norm']), 'w_out': W['l1_w_out'],
    }
    C = []
    for L in ('l0_', 'l1_'):
        C.append({'xa_norm': _row(W[L + 'xa_norm']), 'mem_norm': _row(W[L + 'mem_norm']), 'xa_wq': W[L + 'xa_wq'],
                  'xa_wkv': W[L + 'xa_wkv'], 'xa_wo': W[L + 'xa_wo'], 'ffn_norm': _row(W[L + 'ffn_norm']),
                  'ffn_w_up': W[L + 'ffn_w_up'], 'ffn_conv': W[L + 'ffn_conv'], 'ffn_w_down': W[L + 'ffn_w_down']})

    x1, s_even = _even_fwd(x, P0)
    x3, s_c0 = _common_fwd(x1, mem, C[0], "l0_")
    x4, s_odd = _odd_fwd(x3, P1)
    x6, s_c1 = _common_fwd(x4, mem, C[1], "l1_")
    loss_tile, dx6, d_final = _loss_grad(x6, target, _row(W['final_norm']))

    G = {'final_norm': d_final.reshape(-1)}
    dx4, g = _common_bwd(s_c1, dx6, C[1], "l1_")
    for k, v in g.items():
        G['l1_' + k] = v
    dx3, g = _odd_bwd(s_odd, dx4, P1)
    G['l1_mix_norm'] = g['mix_norm']
    G['l1_w_in'] = g['w_all'][:, :4112]
    G['l1_conv'] = g['conv']
    G['l1_a_log'] = g['a_log_p'][0, 8:16]
    G['l1_dt_bias'] = g['dtb_p'][0, 8:16]
    G['l1_o_norm'] = g['o_norm']
    G['l1_w_out'] = g['w_out']
    dx1, g = _common_bwd(s_c0, dx3, C[0], "l0_")
    for k, v in g.items():
        G['l0_' + k] = v
    dx0, g = _even_bwd(s_even, dx1, P0)
    for k, v in g.items():
        G['l0_' + k] = v
    return loss_tile, dx0, G


ANY = pl.BlockSpec(memory_space=pl.ANY)


def _place():
    return lax.axis_index("x"), lax.axis_index("y"), lax.axis_index("c")


def _chip_peers(x, y):
    return [(1 - x, y), (x, 1 - y), (1 - x, 1 - y)]


def _half(ref, mode, shard, j, h, split):
    r, w = shard
    rh = r // 2 if split else r
    h = h if split else 0
    if mode == 'row':
        return ref.at[pl.ds(j * r + h * rh, rh), :]
    if mode == 'col':
        return ref.at[pl.ds(h * rh, rh), pl.ds(j * w, w)]
    return ref.at[j, pl.ds(h * rh, rh), :]


def _place_shard(shard, mode, me_arr, name):
    r, w = shard.shape
    dtype = BF16 if mode != 'tap' else shard.dtype
    if mode == 'tap':
        mode = 'slab'
    tr = _row_tile(r, w)
    nb = r // tr

    def kern(me_ref, s_ref, o_ref):
        o_ref[...] = s_ref[...].astype(o_ref.dtype)

    if mode == 'row':
        full, o_spec = (4 * r, w), pl.BlockSpec((tr, w), lambda i, me: (me[0] * nb + i, 0))
    elif mode == 'col':
        full, o_spec = (r, 4 * w), pl.BlockSpec((tr, w), lambda i, me: (i, me[0]))
    else:
        full, o_spec = (4, r, w), pl.BlockSpec((None, tr, w), lambda i, me: (me[0], i, 0))
    grid_spec = pltpu.PrefetchScalarGridSpec(
        num_scalar_prefetch=1, grid=(nb,), in_specs=[pl.BlockSpec((tr, w), lambda i, me: (i, 0))], out_specs=o_spec)
    return pl.pallas_call(kern, name=name, grid_spec=grid_spec, out_shape=jax.ShapeDtypeStruct(full, dtype),
                          compiler_params=_cparams(("parallel",)))(me_arr, shard)


def _gather_placed(fulls, modes, shards, splits):
    n = len(fulls)

    def body(*refs):
        outs = refs[n:2 * n]
        send_sems, recv_sems = refs[2 * n:]
        x, y, c = _place()
        peers = _chip_peers(x, y)
        me = 2 * x + y

        def win(a, j, h):
            return _half(outs[a], modes[a], shards[a], j, h, splits[a])

        def copy(a, k, j, h, to):
            return pltpu.make_async_remote_copy(src_ref=win(a, j, h), dst_ref=win(a, j, h),
                                                send_sem=send_sems.at[6 * a + k], recv_sem=recv_sems.at[6 * a + k],
                                                device_id=to, device_id_type=MESH)

        over_ici = [copy(a, k, me, c, (p[0], p[1], c)) for a in range(n) for k, p in enumerate(peers)]
        for cp in over_ici:
            cp.start()
        passed = []
        for a in range(n):
            for k, p in enumerate(peers):
                j = 2 * p[0] + p[1]
                copy(a, k, j, c, (p[0], p[1], c)).wait_recv()
                if splits[a]:
                    fwd = copy(a, 3 + k, j, c, (x, y, 1 - c))
                    fwd.start()
                    passed.append(fwd)
        for a in range(n):
            if splits[a]:
                for k, p in enumerate(peers):
                    copy(a, 3 + k, 2 * p[0] + p[1], 1 - c, (x, y, 1 - c)).wait_recv()
        for cp in over_ici + passed:
            cp.wait_send()

    return pl.pallas_call(
        body, name="gather_weights", in_specs=[ANY] * n, out_specs=[ANY] * n,
        out_shape=[jax.ShapeDtypeStruct(f.shape, f.dtype) for f in fulls],
        input_output_aliases={a: a for a in range(n)},
        scratch_shapes=[pltpu.SemaphoreType.DMA((6 * n,)), pltpu.SemaphoreType.DMA((6 * n,))],
    )(*fulls)


def _send_other_half(gs):
    n = len(gs)

    def body(*refs):
        ins, outs = refs[:n], refs[n:2 * n]
        send_sems, recv_sems = refs[2 * n:]
        x, y, c = _place()
        cps = []
        for a in range(n):
            rh = gs[a].shape[1] // 2
            cps.append(pltpu.make_async_remote_copy(
                src_ref=ins[a].at[:, pl.ds((1 - c) * rh, rh), :], dst_ref=outs[a], send_sem=send_sems.at[a],
                recv_sem=recv_sems.at[a], device_id=(x, y, 1 - c), device_id_type=MESH))
        for cp in cps:
            cp.start()
        for cp in cps:
            cp.wait()

    return pl.pallas_call(
        body, name="send_other_half", in_specs=[ANY] * n, out_specs=[ANY] * n,
        out_shape=[jax.ShapeDtypeStruct((g.shape[0], g.shape[1] // 2, g.shape[2]), g.dtype) for g in gs],
        scratch_shapes=[pltpu.SemaphoreType.DMA((n,)), pltpu.SemaphoreType.DMA((n,))],
    )(*gs)


def _send_to_chips(ps, widths):
    n = len(ps)

    def body(*refs):
        ins, outs = refs[:n], refs[n:2 * n]
        send_sems, recv_sems = refs[2 * n:]
        x, y, c = _place()
        peers = _chip_peers(x, y)
        me = 2 * x + y

        def src(a, j):
            if ps[a].shape[0] == 4:
                return ins[a].at[j]
            return ins[a].at[0, :, pl.ds(j * widths[a], widths[a])]

        def copy(a, k, j, dst_slab, to):
            return pltpu.make_async_remote_copy(src_ref=src(a, j), dst_ref=outs[a].at[dst_slab],
                                                send_sem=send_sems.at[3 * a + k], recv_sem=recv_sems.at[3 * a + k],
                                                device_id=(to[0], to[1], c), device_id_type=MESH)

        sends = [copy(a, k, 2 * p[0] + p[1], me, p) for a in range(n) for k, p in enumerate(peers)]
        for cp in sends:
            cp.start()
        for a in range(n):
            for k, p in enumerate(peers):
                copy(a, k, me, 2 * p[0] + p[1], p).wait_recv()
        for cp in sends:
            cp.wait_send()

    return pl.pallas_call(
        body, name="send_to_chips", in_specs=[ANY] * n, out_specs=[ANY] * n,
        out_shape=[jax.ShapeDtypeStruct((4, p.shape[1], w), p.dtype) for p, w in zip(ps, widths)],
        scratch_shapes=[pltpu.SemaphoreType.DMA((3 * n,)), pltpu.SemaphoreType.DMA((3 * n,))],
    )(*ps)


def _share_halves(bufs):
    n = len(bufs)

    def body(*refs):
        outs = refs[n:2 * n]
        send_sems, recv_sems = refs[2 * n:]
        x, y, c = _place()
        sends, waits = [], []
        for a in range(n):
            rh = bufs[a].shape[0] // 2
            mine = outs[a].at[pl.ds(c * rh, rh), :]
            other = outs[a].at[pl.ds((1 - c) * rh, rh), :]
            sends.append(pltpu.make_async_remote_copy(src_ref=mine, dst_ref=mine, send_sem=send_sems.at[a],
                                                      recv_sem=recv_sems.at[a], device_id=(x, y, 1 - c),
                                                      device_id_type=MESH))
            waits.append(pltpu.make_async_remote_copy(src_ref=mine, dst_ref=other, send_sem=send_sems.at[a],
                                                      recv_sem=recv_sems.at[a], device_id=(x, y, 1 - c),
                                                      device_id_type=MESH))
        for cp in sends:
            cp.start()
        for cp in waits:
            cp.wait()

    return pl.pallas_call(
        body, name="share_halves", in_specs=[ANY] * n, out_specs=[ANY] * n,
        out_shape=[jax.ShapeDtypeStruct(b.shape, b.dtype) for b in bufs],
        input_output_aliases={a: a for a in range(n)},
        scratch_shapes=[pltpu.SemaphoreType.DMA((n,)), pltpu.SemaphoreType.DMA((n,))],
    )(*bufs)


def _gather_all(mine):
    flips = [(dx, dy, dc) for dx in (0, 1) for dy in (0, 1) for dc in (0, 1) if (dx, dy, dc) != (0, 0, 0)]

    def body(x_ref, out_ref, send_sems, recv_sems, local_sem):
        x, y, c = _place()
        me = 4 * x + 2 * y + c

        def peer(f):
            return (x ^ f[0], y ^ f[1], c ^ f[2])

        def copy(k, slab, to):
            return pltpu.make_async_remote_copy(src_ref=x_ref, dst_ref=out_ref.at[slab], send_sem=send_sems.at[k],
                                                recv_sem=recv_sems.at[k], device_id=to, device_id_type=MESH)

        own = pltpu.make_async_copy(x_ref, out_ref.at[me], local_sem)
        own.start()
        sends = [copy(k, me, peer(f)) for k, f in enumerate(flips)]
        for s in sends:
            s.start()
        for k, f in enumerate(flips):
            p = peer(f)
            copy(k, 4 * p[0] + 2 * p[1] + p[2], p).wait_recv()
        for s in sends:
            s.wait_send()
        own.wait()

    return pl.pallas_call(
        body, name="gather_all", in_specs=[ANY], out_specs=ANY,
        out_shape=jax.ShapeDtypeStruct((8,) + mine.shape, mine.dtype),
        scratch_shapes=[pltpu.SemaphoreType.DMA((7,)), pltpu.SemaphoreType.DMA((7,)), pltpu.SemaphoreType.DMA],
    )(mine)


TILE_BYTES = 2 * 1024 * 1024


def _row_tile(rows, width=1024):
    for t in (512, 352, 256, 176, 128, 64, 32, 16, 8):
        if rows % t == 0 and t * width * 4 <= TILE_BYTES:
            return t
    return rows


def _pair_sum(g, got, c_arr, name):
    ns, r, w = g.shape
    rh = r // 2
    tr = _row_tile(rh, w)
    nb = rh // tr

    def kern(c_ref, g_ref, o_ref, out_ref):
        out_ref[...] = (g_ref[...] + o_ref[...]).astype(BF16)

    grid_spec = pltpu.PrefetchScalarGridSpec(
        num_scalar_prefetch=1, grid=(ns, nb),
        in_specs=[pl.BlockSpec((None, tr, w), lambda j, i, c_ref: (j, c_ref[0] * nb + i, 0)),
                  pl.BlockSpec((None, tr, w), lambda j, i, c_ref: (j, i, 0))],
        out_specs=pl.BlockSpec((None, tr, w), lambda j, i, c_ref: (j, i, 0)))
    return pl.pallas_call(kern, name=name, grid_spec=grid_spec, out_shape=jax.ShapeDtypeStruct((ns, rh, w), BF16),
                          compiler_params=_cparams(("parallel", "parallel")))(c_arr, g, got)


def _chip_sum(pair, recv, w, mc_arr, name):
    rh = pair.shape[1]
    tr = _row_tile(rh, w)
    nb = rh // tr

    def kern(mc_ref, own_ref, r1_ref, r2_ref, r3_ref, out_ref):
        acc = own_ref[...].astype(F32)
        for r_ref in (r1_ref, r2_ref, r3_ref):
            acc = acc + r_ref[...].astype(F32)
        out_ref[...] = acc

    if pair.shape[0] == 4:
        own_spec = pl.BlockSpec((None, tr, w), lambda i, mc: (mc[0], i, 0))
    else:
        own_spec = pl.BlockSpec((None, tr, w), lambda i, mc: (0, i, mc[0]))
    recv_specs = [pl.BlockSpec((None, tr, w), functools.partial(lambda i, mc, d: ((mc[0] + d) % 4, i, 0), d=d))
                  for d in (1, 2, 3)]
    grid_spec = pltpu.PrefetchScalarGridSpec(
        num_scalar_prefetch=1, grid=(nb,), in_specs=[own_spec] + recv_specs,
        out_specs=pl.BlockSpec((tr, w), lambda i, mc: (mc[1] * nb + i, 0)))
    return pl.pallas_call(kern, name=name, grid_spec=grid_spec, out_shape=jax.ShapeDtypeStruct((2 * rh, w), F32),
                          compiler_params=_cparams(("parallel",)))(mc_arr, pair, recv, recv, recv)


def _slab_sum(slabs, name):
    n, R, w = slabs.shape
    tr = _row_tile(R)

    def kern(s_ref, o_ref):
        acc = s_ref[0].astype(F32)
        for k in range(1, n):
            acc = acc + s_ref[k].astype(F32)
        o_ref[...] = acc

    return pl.pallas_call(
        kern, name=name, grid=(R // tr,), in_specs=[pl.BlockSpec((n, tr, w), lambda i: (0, i, 0))],
        out_specs=pl.BlockSpec((tr, w), lambda i: (i, 0)), out_shape=jax.ShapeDtypeStruct((R, w), F32),
        compiler_params=_cparams(("parallel",)))(slabs)


def _adamw(w, g, m, v, name):
    R, C = w.shape
    tr = _pick(R, (256, 128, 64, 32, 16, 8))

    def kern(w_ref, g_ref, m_ref, v_ref, d_ref, nm_ref, nv_ref):
        gv = g_ref[...]
        m2 = ADAM_B1 * m_ref[...] + (1.0 - ADAM_B1) * gv
        v2 = ADAM_B2 * v_ref[...] + (1.0 - ADAM_B2) * jnp.square(gv)
        m_hat = m2 / (1.0 - ADAM_B1 ** ADAM_STEP)
        v_hat = v2 / (1.0 - ADAM_B2 ** ADAM_STEP)
        d_ref[...] = -ADAM_LR * (m_hat / (jnp.sqrt(v_hat) + ADAM_EPS) + ADAM_WD * w_ref[...])
        nm_ref[...] = m2
        nv_ref[...] = v2

    spec = pl.BlockSpec((tr, C), lambda i: (i, 0))
    return pl.pallas_call(
        kern, name=name, grid=(R // tr,), in_specs=[spec] * 4, out_specs=[spec] * 3,
        out_shape=[jax.ShapeDtypeStruct((R, C), F32)] * 3, compiler_params=_cparams(("parallel",)))(w, g, m, v)


def _pack_small(vals):
    flat = jnp.concatenate([vals[n].astype(F32).reshape(-1) for n in SMALL_NAMES])
    rows = -(-flat.shape[0] // (8 * LANES)) * 8
    return jnp.pad(flat, (0, rows * LANES - flat.shape[0])).reshape(rows, LANES)


def _unpack_small(packed, shapes):
    flat = packed.reshape(-1)
    out = {}
    off = 0
    for n in SMALL_NAMES:
        size = int(np.prod(shapes[n]))
        out[n] = flat[off:off + size].reshape(shapes[n])
        off += size
    return out


def _gather_weights(A):
    names = MATRIX_NAMES + list(CONVS)
    me_arr = jnp.reshape(2 * lax.axis_index("x") + lax.axis_index("y"), (1,)).astype(jnp.int32)
    modes = [_matrix_mode(n) for n in MATRIX_NAMES] + ['slab'] * len(CONVS)
    fulls = [_place_shard(A[n], 'tap' if n in CONVS else m, me_arr, "place_" + n) for n, m in zip(names, modes)]
    outs = _gather_placed(fulls, modes, [A[n].shape for n in names], [n not in CONVS for n in names])
    W = {}
    for n, m, o in zip(names, modes, outs):
        W[n] = jnp.concatenate([o[j] for j in range(4)], axis=1) if m == 'slab' else o
    return W


def _matrix_mode(n):
    return 'slab' if n == 'l1_w_in' else ('row' if MATRICES[n] == 0 else 'col')


def _reduce_matrix_grads(G):
    c = lax.axis_index("c")
    me = 2 * lax.axis_index("x") + lax.axis_index("y")
    c_arr = jnp.reshape(c, (1,)).astype(jnp.int32)
    mc_arr = jnp.stack([me, c]).astype(jnp.int32)
    gs, widths = [], []
    for n in MATRIX_NAMES:
        g = G[n]
        mode = _matrix_mode(n)
        if mode == 'row':
            gs.append(g.reshape(4, g.shape[0] // 4, g.shape[1]))
            widths.append(g.shape[1])
        elif mode == 'col':
            gs.append(g[None])
            widths.append(g.shape[1] // 4)
        else:
            wd = g.shape[1] // 4
            gs.append(jnp.stack([g[:, j * wd:(j + 1) * wd] for j in range(4)]))
            widths.append(wd)
    got = _send_other_half(gs)
    pairs = [_pair_sum(g, o, c_arr, "pair_sum_" + n) for n, g, o in zip(MATRIX_NAMES, gs, got)]
    recv = _send_to_chips(pairs, widths)
    halves = [_chip_sum(p, r, w, mc_arr, "chip_sum_" + n) for n, p, r, w in zip(MATRIX_NAMES, pairs, recv, widths)]
    return dict(zip(MATRIX_NAMES, _share_halves(halves)))


def kernel(*args):
    A = dict(zip(ARG_NAMES, args, strict=True))
    x, mem, target = A['x'][0], A['mem'][0], A['loss_target'][0]

    W = _gather_weights(A)
    for n in SMALL_NAMES:
        if n not in CONVS:
            W[n] = A[n]

    loss_tile, grad_x, G = _local_step(x, mem, target, W)
    loss = lax.psum(loss_tile[0, 0], ("x", "y", "c"))

    g_mat = _reduce_matrix_grads(G)

    g_small = _slab_sum(_gather_all(_pack_small({n: G[n] for n in SMALL_NAMES})), "sum_small")
    g_small = _unpack_small(g_small, {n: G[n].shape for n in SMALL_NAMES})
    me = 2 * lax.axis_index("x") + lax.axis_index("y")
    for n in CONVS:
        wd = A[n].shape[1]
        g_small[n] = lax.dynamic_slice_in_dim(g_small[n], me * wd, wd, axis=1)
    flat_names = [n for n in SMALL_NAMES if n not in CONVS]

    def pack_flat(prefix):
        return _pack_small_flat({n: A[prefix + n] for n in flat_names}, flat_names)

    shapes = {n: A[n].shape for n in flat_names}
    d_s, m_s, v_s = _adamw(pack_flat(''), _pack_small_flat(g_small, flat_names), pack_flat('m_'), pack_flat('v_'),
                           "adamw_small")
    d_s, m_s, v_s = (_unpack_flat(p, shapes, flat_names) for p in (d_s, m_s, v_s))

    grads, deltas, new_m, new_v = {}, {}, {}, {}
    for n in WEIGHTS:
        if n in MATRICES or n in CONVS:
            grads[n] = g_mat[n] if n in MATRICES else g_small[n]
            deltas[n], new_m[n], new_v[n] = _adamw(A[n], grads[n], A['m_' + n], A['v_' + n], "adamw_" + n)
        else:
            grads[n] = g_small[n].reshape(A[n].shape)
            deltas[n], new_m[n], new_v[n] = d_s[n], m_s[n], v_s[n]
    return (loss, grad_x[None], *[grads[n] for n in WEIGHTS], *[deltas[n] for n in WEIGHTS],
            *[new_m[n] for n in WEIGHTS], *[new_v[n] for n in WEIGHTS])


def _pack_small_flat(vals, names):
    flat = jnp.concatenate([vals[n].astype(F32).reshape(-1) for n in names])
    rows = -(-flat.shape[0] // (8 * LANES)) * 8
    return jnp.pad(flat, (0, rows * LANES - flat.shape[0])).reshape(rows, LANES)


def _unpack_flat(packed, shapes, names):
    flat = packed.reshape(-1)
    out = {}
    off = 0
    for n in names:
        size = int(np.prod(shapes[n]))
        out[n] = flat[off:off + size].reshape(shapes[n])
        off += size
    return out
```

```python
import functools
import math

import numpy as np
import jax
import jax.numpy as jnp
from jax import lax
from jax.experimental import pallas as pl
from jax.experimental.pallas import tpu as pltpu

F32 = jnp.float32
BF16 = jnp.bfloat16
EPS = 1e-6
MESH = pl.DeviceIdType.MESH

ADAM_LR = 0.001
ADAM_B1 = 0.9
ADAM_B2 = 0.999
ADAM_EPS = 1e-08
ADAM_WD = 0.01
ADAM_STEP = 10

VMEM_LIMIT_BYTES = 56 * 1024 * 1024
LANES = 1024

WEIGHTS = ['l0_mix_norm', 'l0_w_in', 'l0_ret_norm', 'l0_s5_lambda_re', 'l0_s5_lambda_im', 'l0_s5_b_re', 'l0_s5_b_im',
           'l0_s5_c_re', 'l0_s5_c_im', 'l0_s5_d', 'l0_s5_log_dt', 'l0_s5_w_glu', 'l0_s5_b_glu', 'l0_w_out',
           'l0_xa_norm', 'l0_mem_norm', 'l0_xa_wq', 'l0_xa_wkv', 'l0_xa_wo', 'l0_ffn_norm', 'l0_ffn_w_up',
           'l0_ffn_conv', 'l0_ffn_w_down', 'l1_mix_norm', 'l1_w_in', 'l1_conv', 'l1_a_log', 'l1_dt_bias',
           'l1_o_norm', 'l1_w_out', 'l1_xa_norm', 'l1_mem_norm', 'l1_xa_wq', 'l1_xa_wkv', 'l1_xa_wo',
           'l1_ffn_norm', 'l1_ffn_w_up', 'l1_ffn_conv', 'l1_ffn_w_down', 'final_norm']
ARG_NAMES = (['x', 'mem'] + WEIGHTS + ['loss_target'] + ['m_' + w for w in WEIGHTS] + ['v_' + w for w in WEIGHTS])

MATRICES = {
    'l0_w_in': 1, 'l0_s5_w_glu': 0, 'l0_w_out': 0, 'l0_xa_wq': 0, 'l0_xa_wkv': 1, 'l0_xa_wo': 0, 'l0_ffn_w_up': 1,
    'l0_ffn_w_down': 0, 'l1_w_in': 1, 'l1_w_out': 0, 'l1_xa_wq': 0, 'l1_xa_wkv': 1, 'l1_xa_wo': 0,
    'l1_ffn_w_up': 1, 'l1_ffn_w_down': 0,
}
CONVS = ('l0_ffn_conv', 'l1_conv', 'l1_ffn_conv')
MATRIX_NAMES = [w for w in WEIGHTS if w in MATRICES]
SMALL_NAMES = [w for w in WEIGHTS if w not in MATRICES]


def _cparams(sem=None):
    return pltpu.CompilerParams(dimension_semantics=sem, vmem_limit_bytes=VMEM_LIMIT_BYTES)


def _pick(n, cands):
    for c in cands:
        if n % c == 0:
            return c
    return n


_NN = ((1,), (0,))
_NT = ((1,), (1,))
_TN = ((0,), (0,))


def _dot(a, b, dims, hi):
    if hi is not None:
        return lax.dot_general(a.astype(F32), b.astype(F32), (dims, ((), ())), precision=hi,
                               preferred_element_type=F32)
    return lax.dot_general(a.astype(BF16), b.astype(BF16), (dims, ((), ())), preferred_element_type=F32)


def _make_mm(hi):
    @jax.custom_vjp
    def nn(a, b):
        return _dot(a, b, _NN, hi)

    def nn_f(a, b):
        return nn(a, b), (a, b)

    def nn_b(r, g):
        a, b = r
        return _dot(g, b, _NT, hi), _dot(a, g, _TN, hi)

    nn.defvjp(nn_f, nn_b)

    @jax.custom_vjp
    def nt(a, b):
        return _dot(a, b, _NT, hi)

    def nt_f(a, b):
        return nt(a, b), (a, b)

    def nt_b(r, g):
        a, b = r
        return _dot(g, b, _NN, hi), _dot(g, a, _TN, hi)

    nt.defvjp(nt_f, nt_b)

    @jax.custom_vjp
    def tn(a, b):
        return _dot(a, b, _TN, hi)

    def tn_f(a, b):
        return tn(a, b), (a, b)

    def tn_b(r, g):
        a, b = r
        return _dot(b, g, _NT, hi), _dot(a, g, _NN, hi)

    tn.defvjp(tn_f, tn_b)
    return nn, nt, tn


mm, mm_nt, mm_tn = _make_mm(None)
mmh, mmh_nt, mmh_tn = _make_mm(lax.Precision.HIGHEST)
mm3, _, _ = _make_mm(lax.Precision.HIGH)


@jax.custom_vjp
def _swap_halves(x):
    return pltpu.roll(x, 64, 1)


def _swap_f(x):
    return pltpu.roll(x, 64, 1), None


def _swap_b(_, g):
    return (pltpu.roll(g, 64, 1),)


_swap_halves.defvjp(_swap_f, _swap_b)


def _silu(x):
    return x * jax.nn.sigmoid(x)


def _rms(x, g):
    return x * lax.rsqrt(jnp.mean(x * x, axis=-1, keepdims=True) + EPS) * g


def _iota(shape, dim):
    return lax.broadcasted_iota(jnp.int32, shape, dim)


def _matmul(a, b, mode="nn", res=None, name="mm", a_cols=None, out_dtype=F32):
    a_off, a_w = (0, a.shape[1]) if a_cols is None else a_cols
    if mode == "nn":
        (M, K), (K2, N) = (a.shape[0], a_w), b.shape
    elif mode == "nt":
        (M, K), (N, K2) = (a.shape[0], a_w), b.shape
    else:
        (K, M), (K2, N) = (a.shape[0], a_w), b.shape
    assert K == K2, (a.shape, b.shape, mode)
    tm = _pick(M, (512, 1408, 256, 128))
    tn = _pick(N, (1024, 1408, 512, 256, 128))
    tk = _pick(K, (1024, 1408, 512, 256, 128))
    nk = K // tk
    dims = {"nn": _NN, "nt": _NT, "tn": _TN}[mode]
    ao = a_off // (tm if mode == "tn" else tk)
    assert ao * (tm if mode == "tn" else tk) == a_off
    if mode == "nn":
        a_spec = pl.BlockSpec((tm, tk), lambda i, j, k: (i, k + ao))
        b_spec = pl.BlockSpec((tk, tn), lambda i, j, k: (k, j))
    elif mode == "nt":
        a_spec = pl.BlockSpec((tm, tk), lambda i, j, k: (i, k + ao))
        b_spec = pl.BlockSpec((tn, tk), lambda i, j, k: (j, k))
    else:
        a_spec = pl.BlockSpec((tk, tm), lambda i, j, k: (k, i + ao))
        b_spec = pl.BlockSpec((tk, tn), lambda i, j, k: (k, j))
    o_spec = pl.BlockSpec((tm, tn), lambda i, j, k: (i, j))
    has_res = res is not None

    def kern(*refs):
        if has_res:
            a_ref, b_ref, r_ref, o_ref, acc_ref = refs
        else:
            a_ref, b_ref, o_ref, acc_ref = refs
        k = pl.program_id(2)
        part = lax.dot_general(a_ref[...].astype(BF16), b_ref[...].astype(BF16), (dims, ((), ())),
                               preferred_element_type=F32)
        if nk == 1:
            o_ref[...] = (part + r_ref[...] if has_res else part).astype(o_ref.dtype)
            return

        @pl.when(k == 0)
        def _():
            acc_ref[...] = part

        @pl.when((k > 0) & (k < nk - 1))
        def _():
            acc_ref[...] += part

        @pl.when(k == nk - 1)
        def _():
            total = acc_ref[...] + part
            o_ref[...] = (total + r_ref[...] if has_res else total).astype(o_ref.dtype)

    in_specs = [a_spec, b_spec] + ([o_spec] if has_res else [])
    ops = (a, b) + ((res,) if has_res else ())
    return pl.pallas_call(
        kern, name=name, grid=(M // tm, N // tn, nk), in_specs=in_specs, out_specs=o_spec,
        out_shape=jax.ShapeDtypeStruct((M, N), out_dtype), scratch_shapes=[pltpu.VMEM((tm, tn), F32)],
        compiler_params=_cparams(("parallel", "parallel", "arbitrary")))(*ops)


def _matmul_cat(pieces, b, mode="nn", res=None, name="mmcat"):
    M = pieces[0].shape[0]
    widths = [p.shape[1] for p in pieces]
    K = sum(widths)
    N = b.shape[1] if mode == "nn" else b.shape[0]
    assert (b.shape[0] if mode == "nn" else b.shape[1]) == K
    tm = _pick(M, (256, 128))
    tn = _pick(N, (1024, 512, 256, 128))
    npc = len(pieces)
    has_res = res is not None
    dims = _NN if mode == "nn" else _NT

    def kern(*refs):
        b_ref = refs[npc]
        o_ref = refs[-1]
        acc = refs[npc + 1][...] if has_res else None
        off = 0
        for p in range(npc):
            bp = b_ref[off:off + widths[p], :] if mode == "nn" else b_ref[:, off:off + widths[p]]
            t = lax.dot_general(refs[p][...].astype(BF16), bp.astype(BF16), (dims, ((), ())),
                                preferred_element_type=F32)
            acc = t if acc is None else acc + t
            off += widths[p]
        o_ref[...] = acc

    in_specs = [pl.BlockSpec((tm, w), lambda j, i: (i, 0)) for w in widths]
    in_specs.append(pl.BlockSpec((K, tn), lambda j, i: (0, j)) if mode == "nn"
                    else pl.BlockSpec((tn, K), lambda j, i: (j, 0)))
    o_spec = pl.BlockSpec((tm, tn), lambda j, i: (i, j))
    if has_res:
        in_specs.append(o_spec)
    ops = list(pieces) + [b] + ([res] if has_res else [])
    return pl.pallas_call(
        kern, name=name, grid=(N // tn, M // tm), in_specs=in_specs, out_specs=o_spec,
        out_shape=jax.ShapeDtypeStruct((M, N), F32), compiler_params=_cparams(("parallel", "parallel")))(*ops)


def _blk(a, width=None, colblk=0):
    return (a, a.shape[1] if width is None else width, colblk)


def _row_specs(blocked, params, ts):
    specs = []
    for (_, w, cb) in blocked:
        specs.append(pl.BlockSpec((ts, w), functools.partial(lambda i, cb: (i, cb), cb=cb)))
    for p in params:
        specs.append(pl.BlockSpec(p.shape, lambda i: (0, 0)))
    return specs


def _rowwise(fn, blocked, params, out_widths, name, ts=256, out_dtypes=None):
    S = blocked[0][0].shape[0]
    ts = min(ts, S)
    nb, npar = len(blocked), len(params)
    out_dtypes = [F32] * len(out_widths) if out_dtypes is None else out_dtypes

    def kern(*refs):
        vals = [r[...] for r in refs[:nb + npar]]
        outs = fn(*vals)
        for o_ref, o in zip(refs[nb + npar:], outs):
            o_ref[...] = o.astype(o_ref.dtype)

    return pl.pallas_call(
        kern, name=name, grid=(S // ts,), in_specs=_row_specs(blocked, params, ts),
        out_specs=[pl.BlockSpec((ts, w), lambda i: (i, 0)) for w in out_widths],
        out_shape=[jax.ShapeDtypeStruct((S, w), d) for w, d in zip(out_widths, out_dtypes)],
        compiler_params=_cparams(("parallel",)))(*[b[0] for b in blocked], *params)


def _rowwise_bwd(fn, blocked, params, cots, name, blocked_grad=None, param_grad=None, adds=None, ts=256,
                 out_dtypes=None):
    S = blocked[0][0].shape[0]
    ts = min(ts, S)
    cots = [c if isinstance(c, tuple) else _blk(c) for c in cots]
    nb, npar, nc = len(blocked), len(params), len(cots)
    blocked_grad = [True] * nb if blocked_grad is None else blocked_grad
    param_grad = [True] * npar if param_grad is None else param_grad
    adds = {} if adds is None else adds
    bidx = [i for i in range(nb) if blocked_grad[i]]
    pidx = [i for i in range(npar) if param_grad[i]]
    add_keys = sorted(adds)
    n_in = nb + npar + nc + len(add_keys)

    def kern(*refs):
        i = pl.program_id(0)
        xs = [r[...] for r in refs[:nb]]
        ps = [r[...] for r in refs[nb:nb + npar]]
        gs = [r[...] for r in refs[nb + npar:nb + npar + nc]]
        add_vals = {k: refs[nb + npar + nc + n][...] for n, k in enumerate(add_keys)}
        outs = refs[n_in:]

        def f(*diff):
            full_x = list(xs)
            full_p = list(ps)
            for n, ix in enumerate(bidx):
                full_x[ix] = diff[n]
            for n, ix in enumerate(pidx):
                full_p[ix] = diff[len(bidx) + n]
            return tuple(fn(*full_x, *full_p))

        _, vjp = jax.vjp(f, *[xs[ix] for ix in bidx], *[ps[ix] for ix in pidx])
        grads = vjp(tuple(gs))
        for n, ix in enumerate(bidx):
            g = grads[n]
            if ix in add_vals:
                g = g + add_vals[ix]
            outs[n][...] = g.astype(outs[n].dtype)
        for n in range(len(pidx)):
            o_ref = outs[len(bidx) + n]

            @pl.when(i == 0)
            def _(o_ref=o_ref):
                o_ref[...] = jnp.zeros_like(o_ref)

            o_ref[...] += grads[len(bidx) + n]

    in_specs = _row_specs(blocked, params, ts)
    in_specs += _row_specs(cots, [], ts)
    in_specs += [pl.BlockSpec((ts, adds[k].shape[1]), lambda i: (i, 0)) for k in add_keys]
    out_specs = [pl.BlockSpec((ts, blocked[ix][1]), lambda i: (i, 0)) for ix in bidx]
    out_specs += [pl.BlockSpec(params[ix].shape, lambda i: (0, 0)) for ix in pidx]
    out_dtypes = [F32] * len(bidx) if out_dtypes is None else out_dtypes
    out_shape = [jax.ShapeDtypeStruct((S, blocked[ix][1]), d) for ix, d in zip(bidx, out_dtypes)]
    out_shape += [jax.ShapeDtypeStruct(params[ix].shape, F32) for ix in pidx]
    return pl.pallas_call(
        kern, name=name, grid=(S // ts,), in_specs=in_specs, out_specs=out_specs, out_shape=out_shape,
        compiler_params=_cparams(("arbitrary",)))(*[b[0] for b in blocked], *params, *[c[0] for c in cots],
                                                    *[adds[k] for k in add_keys])


def _rms_fn(x, g):
    return (_rms(x, g),)


def _head_norm(o, n_heads, dh):
    outs = []
    for h in range(n_heads):
        oh = o[:, h * dh:(h + 1) * dh]
        outs.append(oh * lax.rsqrt(jnp.mean(oh * oh, axis=-1, keepdims=True) + EPS))
    return outs


def _ret_post_fn(o_raw, gate, ret_norm):
    o = jnp.concatenate(_head_norm(o_raw, 4, 128), axis=1)
    return (o * ret_norm * _silu(gate),)


def _s5_post_fn(y1, y2, u, d, w_glu, b_glu):
    y = y1 - y2 + d * u
    y = jax.nn.gelu(y)
    return (y * jax.nn.sigmoid(mm(y, w_glu) + b_glu),)


def _xattn_fn(q, kv):
    outs = []
    for h in range(4):
        qh = q[:, h * 256:(h + 1) * 256]
        kh = kv[:, h * 256:(h + 1) * 256]
        vh = kv[:, 1024 + h * 256:1024 + (h + 1) * 256]
        s = mm_nt(qh, kh) * (256 ** -0.5)
        s = s - lax.stop_gradient(jnp.max(s, axis=-1, keepdims=True))
        p = jnp.exp(s)
        p = p / jnp.sum(p, axis=-1, keepdims=True)
        outs.append(mm(p, vh))
    return (jnp.concatenate(outs, axis=1),)


def _softplus(x):
    return jnp.maximum(x, 0.0) + jnp.log1p(jnp.exp(-jnp.abs(x)))


def _gdn_gates_fn(pt, a_log_p, dtb_p):
    rows, cols = _iota((128, 1024), 0), _iota((128, 1024), 1)
    e_b = (rows == (cols >> 7)).astype(F32)
    e_a = (rows == (cols >> 7) + 8).astype(F32)
    beta = jax.nn.sigmoid(pt)
    g = -(jnp.exp(a_log_p) * _softplus(pt + dtb_p))
    return mmh(g, e_a), mmh(beta, e_b)


def _gdn_post_fn(o_raw, z, o_norm):
    outs = _head_norm(o_raw, 8, 128)
    o = jnp.concatenate([oh * o_norm for oh in outs], axis=1)
    return (o * _silu(z),)


def _ffn_post(up, gate):
    return _silu(gate) * up


def _shift_down(cur, prev8, sh, row8):
    if sh == 0:
        return cur
    r = pltpu.roll(cur, sh, 0)
    p = pltpu.roll(prev8, sh, 0)
    top = jnp.where(row8 < sh, p, r[0:8])
    if cur.shape[0] == 8:
        return top
    return jnp.concatenate([top, r[8:]], axis=0)


def _shift_up(cur, next8, sh, row8):
    if sh == 0:
        return cur
    ts = cur.shape[0]
    r = pltpu.roll(cur, ts - sh, 0)
    p = pltpu.roll(next8, 8 - sh, 0)
    bot = jnp.where(row8 >= 8 - sh, p, r[ts - 8:])
    return jnp.concatenate([r[:ts - 8], bot], axis=0)


def _conv_rows(cur, prev8, wrows, row8):
    k_w = len(wrows)
    out = None
    for j in range(k_w):
        t = _shift_down(cur, prev8, k_w - 1 - j, row8) * wrows[j]
        out = t if out is None else out + t
    return out


def _conv_specs(x, xoff, w, woff, ts, tc):
    r8 = ts // 8
    return [pl.BlockSpec((ts, tc), functools.partial(lambda i, j, o: (i, j + o), o=xoff)),
            pl.BlockSpec((8, tc), functools.partial(lambda i, j, o: (jnp.maximum(i * r8 - 1, 0), j + o), o=xoff)),
            pl.BlockSpec((w.shape[0], tc), functools.partial(lambda i, j, o: (0, j + o), o=woff))]


def _conv_post(srcs, post, ncol, tc, name, cots=None, ts=256, out_dtype=F32):
    S = srcs[0][0].shape[0]
    ns = len(srcs)
    bwd = cots is not None

    def kern(*refs):
        first = pl.program_id(0) == 0
        row8 = _iota((8, tc), 0)
        cs = []
        for s in range(ns):
            cur_ref, prev_ref, w_ref = refs[3 * s:3 * s + 3]
            prev = jnp.where(first, 0.0, prev_ref[...])
            wrows = [w_ref[j:j + 1, :] for j in range(w_ref.shape[0])]
            cs.append(_conv_rows(cur_ref[...], prev, wrows, row8))
        if bwd:
            g = refs[3 * ns][...]
            _, vjp = jax.vjp(lambda *c: post(*c), *cs)
            for o_ref, d in zip(refs[3 * ns + 1:], vjp(g)):
                o_ref[...] = d
        else:
            refs[3 * ns][...] = post(*cs).astype(refs[3 * ns].dtype)

    in_specs = []
    ops = []
    for (x, xoff, w, woff) in srcs:
        in_specs += _conv_specs(x, xoff, w, woff, ts, tc)
        ops += [x, x, w]
    o_spec = pl.BlockSpec((ts, tc), lambda i, j: (i, j))
    o_shape = jax.ShapeDtypeStruct((S, ncol * tc), F32)
    if bwd:
        in_specs.append(o_spec)
        ops.append(cots)
        out_specs, out_shape = [o_spec] * ns, [o_shape] * ns
    else:
        out_specs, out_shape = o_spec, jax.ShapeDtypeStruct((S, ncol * tc), out_dtype)
    return pl.pallas_call(
        kern, name=name, grid=(S // ts, ncol), in_specs=in_specs, out_specs=out_specs, out_shape=out_shape,
        compiler_params=_cparams(("parallel", "parallel")))(*ops)


def _conv_bwd(dc, x, xoff, w, woff, ncol, tc, name, ts=256):
    S = x.shape[0]
    k_w = w.shape[0]
    r8 = ts // 8
    nblk8 = S // 8
    nrow = S // ts

    def kern(dc_ref, dn_ref, x_ref, xp_ref, w_ref, dx_ref, dw_ref):
        i = pl.program_id(1)
        row8 = _iota((8, tc), 0)
        dcur = dc_ref[...]
        dnext = jnp.where(i == nrow - 1, 0.0, dn_ref[...])
        xcur = x_ref[...]
        xprev = jnp.where(i == 0, 0.0, xp_ref[...])

        @pl.when(i == 0)
        def _():
            dw_ref[...] = jnp.zeros_like(dw_ref)

        dx = None
        for j in range(k_w):
            sh = k_w - 1 - j
            wj = w_ref[j:j + 1, :]
            t = _shift_up(dcur, dnext, sh, row8) * wj
            dx = t if dx is None else dx + t
            dw_ref[j:j + 1, :] += jnp.sum(dcur * _shift_down(xcur, xprev, sh, row8), axis=0, keepdims=True)
        dx_ref[...] = dx.astype(dx_ref.dtype)

    in_specs = [pl.BlockSpec((ts, tc), lambda j, i: (i, j)),
                pl.BlockSpec((8, tc), lambda j, i: (jnp.minimum((i + 1) * r8, nblk8 - 1), j)),
                pl.BlockSpec((ts, tc), functools.partial(lambda j, i, o: (i, j + o), o=xoff)),
                pl.BlockSpec((8, tc), functools.partial(lambda j, i, o: (jnp.maximum(i * r8 - 1, 0), j + o), o=xoff)),
                pl.BlockSpec((k_w, tc), functools.partial(lambda j, i, o: (0, j + o), o=woff))]
    out_specs = [pl.BlockSpec((ts, tc), lambda j, i: (i, j)), pl.BlockSpec((k_w, tc), lambda j, i: (0, j))]
    out_shape = [jax.ShapeDtypeStruct((S, ncol * tc), BF16), jax.ShapeDtypeStruct((k_w, ncol * tc), F32)]
    return pl.pallas_call(
        kern, name=name, grid=(ncol, nrow), in_specs=in_specs, out_specs=out_specs, out_shape=out_shape,
        compiler_params=_cparams(("parallel", "arbitrary")))(dc, dc, x, x, w)


def _conv_post_bwd(srcs, post, ncol, tc, cot, name, ts=256):
    S = srcs[0][0].shape[0]
    ns = len(srcs)
    r8 = ts // 8
    nblk8 = S // 8
    nrow = S // ts

    def kern(*refs):
        i = pl.program_id(1)
        row8 = _iota((8, tc), 0)
        g_ref, gn_ref = refs[4 * ns:4 * ns + 2]
        outs = refs[4 * ns + 2:]
        xs, xps, ws, cs, cns = [], [], [], [], []
        for s in range(ns):
            cur_ref, prev_ref, next_ref, w_ref = refs[4 * s:4 * s + 4]
            xcur = cur_ref[...]
            xprev = jnp.where(i == 0, 0.0, prev_ref[...])
            wrows = [w_ref[j:j + 1, :] for j in range(w_ref.shape[0])]
            xs.append(xcur)
            xps.append(xprev)
            ws.append(wrows)
            cs.append(_conv_rows(xcur, xprev, wrows, row8))
            cns.append(_conv_rows(next_ref[...], xcur[ts - 8:], wrows, row8))
        _, vjp = jax.vjp(lambda *c: post(*c), *cs)
        dcs = vjp(g_ref[...])
        _, vjp_next = jax.vjp(lambda *c: post(*c), *cns)
        dcns = vjp_next(jnp.where(i == nrow - 1, 0.0, gn_ref[...]))
        for s in range(ns):
            dx_ref, dw_ref = outs[2 * s], outs[2 * s + 1]

            @pl.when(i == 0)
            def _(dw_ref=dw_ref):
                dw_ref[...] = jnp.zeros_like(dw_ref)

            k_w = len(ws[s])
            dx = None
            for j in range(k_w):
                sh = k_w - 1 - j
                t = _shift_up(dcs[s], dcns[s], sh, row8) * ws[s][j]
                dx = t if dx is None else dx + t
                dw_ref[j:j + 1, :] += jnp.sum(dcs[s] * _shift_down(xs[s], xps[s], sh, row8), axis=0, keepdims=True)
            dx_ref[...] = dx.astype(dx_ref.dtype)

    def nxt(i):
        return jnp.minimum((i + 1) * r8, nblk8 - 1)

    in_specs, ops = [], []
    for (x, xoff, w, woff) in srcs:
        in_specs += [pl.BlockSpec((ts, tc), functools.partial(lambda j, i, o: (i, j + o), o=xoff)),
                     pl.BlockSpec((8, tc), functools.partial(lambda j, i, o: (jnp.maximum(i * r8 - 1, 0), j + o),
                                                             o=xoff)),
                     pl.BlockSpec((8, tc), functools.partial(lambda j, i, o: (nxt(i), j + o), o=xoff)),
                     pl.BlockSpec((w.shape[0], tc), functools.partial(lambda j, i, o: (0, j + o), o=woff))]
        ops += [x, x, x, w]
    in_specs += [pl.BlockSpec((ts, tc), lambda j, i: (i, j)), pl.BlockSpec((8, tc), lambda j, i: (nxt(i), j))]
    ops += [cot, cot]
    out_specs, out_shape = [], []
    for (x, xoff, w, woff) in srcs:
        out_specs += [pl.BlockSpec((ts, tc), lambda j, i: (i, j)), pl.BlockSpec((w.shape[0], tc), lambda j, i: (0, j))]
        out_shape += [jax.ShapeDtypeStruct((S, ncol * tc), BF16), jax.ShapeDtypeStruct((w.shape[0], ncol * tc), F32)]
    return pl.pallas_call(
        kern, name=name, grid=(ncol, nrow), in_specs=in_specs, out_specs=out_specs, out_shape=out_shape,
        compiler_params=_cparams(("parallel", "arbitrary")))(*ops)


def _ret_tables(S):
    H, C, dh = 4, 128, 128
    lg = jnp.log1p(-jnp.exp2(-5.0 - jnp.arange(H, dtype=F32)))
    idx = jnp.arange(C, dtype=F32)
    diff = idx[:, None] - idx[None, :]
    causal = diff >= 0
    intra = jnp.where(causal, jnp.exp(lg[:, None, None] * jnp.where(causal, diff, 0.0)), 0.0)
    kdec = jnp.broadcast_to(jnp.exp(lg[:, None] * (C - 1 - idx))[:, :, None], (H, C, dh))
    qdec = jnp.broadcast_to(jnp.exp(lg[:, None] * (idx + 1))[:, :, None], (H, C, dh))
    cdec = jnp.broadcast_to(jnp.exp(lg * C)[:, None, None], (H, dh, dh))
    half = dh // 2
    inv = jnp.exp(-math.log(10000.0) * jnp.arange(half, dtype=F32) / half)
    ang = jnp.arange(S).astype(F32)[:, None] * inv[None, :]
    cos, sin = jnp.cos(ang), jnp.sin(ang)
    cosf = jnp.concatenate([cos, cos], axis=1)
    sinf = jnp.concatenate([-sin, sin], axis=1)
    return cosf, sinf, intra, kdec, qdec, cdec


def _ret_chunk(q, k, v, cosf, sinf, intra, kdec, qdec, cdec, state):
    hs = range(len(q))
    qr = [q[h] * cosf + _swap_halves(q[h]) * sinf for h in hs]
    kr = [(k[h] * cosf + _swap_halves(k[h]) * sinf) * (128 ** -0.5) for h in hs]
    scores = [mm_nt(qr[h], kr[h]) * intra[h] for h in hs]
    inner = [mm(scores[h], v[h]) for h in hs]
    kv = [mm_tn(kr[h] * kdec[h], v[h]) for h in hs]
    cross = [mm(qr[h] * qdec[h], state[h]) for h in hs]
    return [inner[h] + cross[h] for h in hs], [state[h] * cdec[h] + kv[h] for h in hs]


RET_H = 4


def _ret_call(proj, tabs, states=None, do=None):
    S = proj.shape[0]
    N = S // 128
    bwd = do is not None

    def nn(n):
        return N - 1 - n if bwd else n

    qkv_spec = pl.BlockSpec((128, 3 * 512), lambda n: (nn(n), 0))
    pos = pl.BlockSpec((128, 128), lambda n: (nn(n), 0))
    tab = pl.BlockSpec((RET_H, 128, 128), lambda n: (0, 0, 0))
    st_spec = pl.BlockSpec((None, RET_H, 128, 128), lambda n: (nn(n), 0, 0, 0))
    o_spec = pl.BlockSpec((128, 512), lambda n: (nn(n), 0))

    def kern(*refs):
        x_ref, c_ref, s_ref, i_ref, kd_ref, qd_ref, cd_ref = refs[:7]
        carry = refs[-1]
        heads = range(RET_H)

        @pl.when(pl.program_id(0) == 0)
        def _():
            carry[...] = jnp.zeros_like(carry)

        def cols(ref, off=0):
            return [ref[:, _hs(off + h)] for h in heads]

        def tabs_of(ref):
            return [ref[h] for h in heads]

        consts = (c_ref[...], s_ref[...], tabs_of(i_ref), tabs_of(kd_ref), tabs_of(qd_ref), tabs_of(cd_ref))
        qkv = (cols(x_ref), cols(x_ref, RET_H), cols(x_ref, 2 * RET_H))
        if bwd:
            sp_ref, do_ref = refs[7:9]
            outs = refs[9:12]
            _, vjp = jax.vjp(lambda q, k, v, s: _ret_chunk(q, k, v, *consts, s), *qkv, tabs_of(sp_ref))
            dq, dk, dv, ds = vjp((cols(do_ref), tabs_of(carry)))
            for h in heads:
                for o_ref, d in zip(outs, (dq[h], dk[h], dv[h])):
                    o_ref[:, _hs(h)] = d.astype(o_ref.dtype)
                carry[h] = ds[h]
        else:
            o_ref, sp_ref = refs[7:9]
            state = tabs_of(carry)
            out, new = _ret_chunk(*qkv, *consts, state)
            for h in heads:
                sp_ref[h] = state[h]
                o_ref[:, _hs(h)] = out[h]
                carry[h] = new[h]

    in_specs = [qkv_spec, pos, pos, tab, tab, tab, tab]
    if bwd:
        in_specs += [st_spec, o_spec]
        out_specs = [o_spec] * 3
        out_shape = [jax.ShapeDtypeStruct((S, 512), BF16)] * 3
        ops = (proj, *tabs, states, do)
    else:
        out_specs = [o_spec, st_spec]
        out_shape = [jax.ShapeDtypeStruct((S, 512), F32), jax.ShapeDtypeStruct((N, RET_H, 128, 128), F32)]
        ops = (proj, *tabs)
    return pl.pallas_call(
        kern, name="ret_bwd" if bwd else "ret_fwd", grid=(N,), in_specs=in_specs, out_specs=out_specs,
        out_shape=out_shape, scratch_shapes=[pltpu.VMEM((RET_H, 128, 128), F32)],
        compiler_params=_cparams(("arbitrary",)))(*ops)


GDN_C = 64
GDN_H = 8


def _unit_lower_inverse(a_mats, eye):
    p = [-a for a in a_mats]
    t = [eye + x for x in p]
    for _ in range(5):
        p = [mm3(x, x) for x in p]
        t = [mm3(y, eye + x) for y, x in zip(t, p)]
    return t


@jax.custom_vjp
def _known_inverse(a_mat, t_mat):
    return t_mat


def _known_inverse_f(a_mat, t_mat):
    return t_mat, t_mat


def _known_inverse_b(t_mat, g):
    return -mmh_tn(t_mat, mmh_nt(g, t_mat)), jnp.zeros_like(t_mat)


_known_inverse.defvjp(_known_inverse_f, _known_inverse_b)


def _gdn_intra(q, k, v, g_b, beta_b, t_known=None):
    c = GDN_C
    hs = range(len(q))
    q = [x * lax.rsqrt(jnp.sum(x * x, axis=-1, keepdims=True) + EPS) * (128 ** -0.5) for x in q]
    k = [x * lax.rsqrt(jnp.sum(x * x, axis=-1, keepdims=True) + EPS) for x in k]
    ri, ci = _iota((c, c), 0), _iota((c, c), 1)
    incl = ri >= ci
    strict = ri > ci
    eye = (ri == ci).astype(F32)
    lower = incl.astype(F32)
    ones = jnp.ones((c, c), F32)
    gc_b = [mmh(lower, g) for g in g_b]
    gl_b = [mmh(ones, g) for g in g_b]
    kb = [k[h] * beta_b[h] for h in hs]
    vb = [v[h] * beta_b[h] for h in hs]
    gcc = [g[:, :c] for g in gc_b]
    decay = [jnp.where(incl, jnp.exp(jnp.where(incl, g - g.T, 0.0)), 0.0) for g in gcc]
    a_mat = [jnp.where(strict, mm_nt(kb[h], k[h]) * decay[h], 0.0) for h in hs]
    if t_known is None:
        t_mat = _unit_lower_inverse(a_mat, eye)
    else:
        t_mat = [_known_inverse(a_mat[h], t_known[h]) for h in hs]
    egc = [jnp.exp(g) for g in gc_b]
    w = [mm(t_mat[h], kb[h] * egc[h]) for h in hs]
    u = [mm(t_mat[h], vb[h]) for h in hs]
    qk = [jnp.where(incl, mm_nt(q[h], k[h]) * decay[h], 0.0) for h in hs]
    q_dec = [q[h] * egc[h] for h in hs]
    k_dec = [k[h] * jnp.exp(gl_b[h] - gc_b[h]) for h in hs]
    return w, u, q_dec, k_dec, qk, t_mat


def _gdn_step(w, u, q_dec, k_dec, qk, g_b, state):
    hs = range(len(w))
    ones = jnp.ones((128, GDN_C), F32)
    gl_s = [mmh(ones, g) for g in g_b]
    ws = [mm(w[h], state[h]) for h in hs]
    qs = [mm(q_dec[h], state[h]) for h in hs]
    v_new = [u[h] - ws[h] for h in hs]
    o = [qs[h] + mm(qk[h], v_new[h]) for h in hs]
    new = [state[h] * jnp.exp(gl_s[h]) + mm_tn(k_dec[h], v_new[h]) for h in hs]
    return o, new


def _hs(h):
    return slice(h * 128, (h + 1) * 128)


def _gdn_intra_call(qkv, g_e, beta_e, cots=None):
    S = qkv.shape[0]
    N = S // GDN_C
    bwd = cots is not None
    row = pl.BlockSpec((GDN_C, 1024), lambda n: (n, 0))
    qkv_spec = pl.BlockSpec((GDN_C, 3072), lambda n: (n, 0))
    qk_spec = pl.BlockSpec((GDN_H, GDN_C, GDN_C), lambda n: (0, n, 0))

    def kern(*refs):
        x_ref, g_ref, b_ref = refs[:3]
        heads = range(GDN_H)

        def cols(ref, off=0):
            return [ref[:, _hs(off + h)] for h in heads]

        args = (cols(x_ref), cols(x_ref, 8), cols(x_ref, 16), cols(g_ref), cols(b_ref))
        if bwd:
            dw_ref, du_ref, dqd_ref, dkd_ref, dqk_ref, dgadd_ref, t_ref = refs[3:10]
            outs = refs[10:]
            t_known = [t_ref[h] for h in heads]
            _, vjp = jax.vjp(lambda *a: _gdn_intra(*a, t_known=t_known)[:5], *args)
            dq, dk, dv, dg, db = vjp((cols(dw_ref), cols(du_ref), cols(dqd_ref), cols(dkd_ref),
                                      [dqk_ref[h] for h in heads]))
            dgadd = cols(dgadd_ref)
            for h in heads:
                for o_ref, d in zip(outs, (dq[h], dk[h], dv[h], dg[h] + dgadd[h], db[h])):
                    o_ref[:, _hs(h)] = d
        else:
            w, u, qd, kd, qk, t_mat = _gdn_intra(*args)
            for h in heads:
                for o_ref, o in zip(refs[3:7], (w[h], u[h], qd[h], kd[h])):
                    o_ref[:, _hs(h)] = o
                refs[7][h] = qk[h]
                refs[8][h] = t_mat[h]

    big = jax.ShapeDtypeStruct((S, 1024), F32)
    sq = jax.ShapeDtypeStruct((GDN_H, S, GDN_C), F32)
    if bwd:
        in_specs = [qkv_spec, row, row, row, row, row, row, qk_spec, row, qk_spec]
        out_specs, out_shape = [row] * 5, [big] * 5
        ops = (qkv, g_e, beta_e) + tuple(cots)
    else:
        in_specs = [qkv_spec, row, row]
        out_specs = [row] * 4 + [qk_spec, qk_spec]
        out_shape = [big] * 4 + [sq, sq]
        ops = (qkv, g_e, beta_e)
    return pl.pallas_call(
        kern, name="gdn_intra_bwd" if bwd else "gdn_intra", grid=(N,), in_specs=in_specs, out_specs=out_specs,
        out_shape=out_shape, compiler_params=_cparams(("parallel",)))(*ops)


def _gdn_pass(w, u, qd, kd, qk, g_e, states=None, do=None):
    S = w.shape[0]
    N = S // GDN_C
    bwd = do is not None

    def nn(n):
        return N - 1 - n if bwd else n

    row = pl.BlockSpec((GDN_C, 1024), lambda n: (nn(n), 0))
    qk_spec = pl.BlockSpec((GDN_H, GDN_C, GDN_C), lambda n: (0, nn(n), 0))
    st_spec = pl.BlockSpec((None, GDN_H, 128, 128), lambda n: (nn(n), 0, 0, 0))

    def kern(*refs):
        w_ref, u_ref, qd_ref, kd_ref, qk_ref, g_ref = refs[:6]
        carry = refs[-1]

        @pl.when(pl.program_id(0) == 0)
        def _():
            carry[...] = jnp.zeros_like(carry)

        heads = range(GDN_H)

        def cols(ref):
            return [ref[:, _hs(h)] for h in heads]

        args = (cols(w_ref), cols(u_ref), cols(qd_ref), cols(kd_ref), [qk_ref[h] for h in heads], cols(g_ref))
        if bwd:
            sp_ref, do_ref = refs[6:8]
            outs = refs[8:14]
            _, vjp = jax.vjp(_gdn_step, *args, [sp_ref[h] for h in heads])
            dw, du, dqd, dkd, dqk, dg, ds = vjp((cols(do_ref), [carry[h] for h in heads]))
            for h in heads:
                for o_ref, d in zip(outs[:4], (dw[h], du[h], dqd[h], dkd[h])):
                    o_ref[:, _hs(h)] = d
                outs[4][h] = dqk[h]
                outs[5][:, _hs(h)] = dg[h]
                carry[h] = ds[h]
        else:
            o_ref, sp_ref = refs[6:8]
            state = [carry[h] for h in heads]
            o, new = _gdn_step(*args, state)
            for h in heads:
                sp_ref[h] = state[h]
                o_ref[:, _hs(h)] = o[h]
                carry[h] = new[h]

    big = jax.ShapeDtypeStruct((S, 1024), F32)
    in_specs = [row, row, row, row, qk_spec, row]
    if bwd:
        in_specs += [st_spec, row]
        out_specs = [row] * 4 + [qk_spec, row]
        out_shape = [big] * 4 + [jax.ShapeDtypeStruct((GDN_H, S, GDN_C), F32), big]
        ops = (w, u, qd, kd, qk, g_e, states, do)
    else:
        out_specs = [row, st_spec]
        out_shape = [big, jax.ShapeDtypeStruct((N, GDN_H, 128, 128), F32)]
        ops = (w, u, qd, kd, qk, g_e)
    return pl.pallas_call(
        kern, name="gdn_pass_bwd" if bwd else "gdn_pass", grid=(N,), in_specs=in_specs, out_specs=out_specs,
        out_shape=out_shape, scratch_shapes=[pltpu.VMEM((GDN_H, 128, 128), F32)],
        compiler_params=_cparams(("arbitrary",)))(*ops)


def _s5_prep_fn(lr, li, ldt, br, bi, cr, ci):
    dt = jnp.exp(ldt)
    mag = jnp.exp(lr * dt)
    a_re = mag * jnp.cos(li * dt)
    a_im = mag * jnp.sin(li * dt)
    den = lr * lr + li * li
    z_re = ((a_re - 1.0) * lr + a_im * li) / den
    z_im = (a_im * lr - (a_re - 1.0) * li) / den
    e1 = ((_iota((512, 32), 0) >> 4) == _iota((512, 32), 1)).astype(F32)
    zr_e = mmh(e1, z_re)
    zi_e = mmh(e1, z_im)
    bb_re = zr_e * br - zi_e * bi
    bb_im = zr_e * bi + zi_e * br
    t1 = ((_iota((64, 2048), 1) & 63) == _iota((64, 2048), 0)).astype(F32)
    m1 = (_iota((512, 2048), 0) >> 4) == (_iota((512, 2048), 1) >> 6)
    bd_re = jnp.where(m1, mmh(bb_re, t1), 0.0)
    bd_im = jnp.where(m1, mmh(bb_im, t1), 0.0)
    t2 = ((_iota((16, 512), 1) & 15) == _iota((16, 512), 0)).astype(F32)
    m2 = (_iota((2048, 512), 0) >> 6) == (_iota((2048, 512), 1) >> 4)
    cd_re = jnp.where(m2, mmh(cr, t2), 0.0)
    cd_im = jnp.where(m2, mmh(ci, t2), 0.0)
    return a_re, a_im, bd_re, bd_im, cd_re, cd_im


_PREP_OUT = [(32, 64), (32, 64), (512, 2048), (512, 2048), (2048, 512), (2048, 512)]


def _s5_prep(params, cots=None):
    bwd = cots is not None

    def kern(*refs):
        vals = [r[...] for r in refs[:7]]
        if bwd:
            gs = tuple(r[...] for r in refs[7:13])
            _, vjp = jax.vjp(_s5_prep_fn, *vals)
            for o_ref, d in zip(refs[13:], vjp(gs)):
                o_ref[...] = d
        else:
            for o_ref, o in zip(refs[7:], _s5_prep_fn(*vals)):
                o_ref[...] = o

    if bwd:
        out_shape = [jax.ShapeDtypeStruct(p.shape, F32) for p in params]
        ops = list(params) + list(cots)
    else:
        out_shape = [jax.ShapeDtypeStruct(s, F32) for s in _PREP_OUT]
        ops = list(params)
    return pl.pallas_call(kern, name="s5_prep_bwd" if bwd else "s5_prep", out_shape=out_shape,
                          compiler_params=_cparams())(*ops)


def _cmul(ar, ai, br, bi):
    return ar * br - ai * bi, ar * bi + ai * br


def _power_table(ar, ai, row8, descending):
    pr, pi = ar, ai
    tr = jnp.zeros(row8.shape, F32)
    ti = jnp.zeros(row8.shape, F32)
    for n in range(8):
        r = 7 - n if descending else n
        tr = jnp.where(row8 == r, pr, tr)
        ti = jnp.where(row8 == r, pi, ti)
        if n < 7:
            pr, pi = _cmul(pr, pi, ar, ai)
    return tr, ti


def _tile_scan(xr, xi, pows, row8, up):
    for d, (pr, pi) in zip((1, 2, 4), pows):
        if up:
            sr = jnp.where(row8 < 8 - d, pltpu.roll(xr, 8 - d, 0), 0.0)
            si = jnp.where(row8 < 8 - d, pltpu.roll(xi, 8 - d, 0), 0.0)
        else:
            sr = jnp.where(row8 >= d, pltpu.roll(xr, d, 0), 0.0)
            si = jnp.where(row8 >= d, pltpu.roll(xi, d, 0), 0.0)
        mr, mi = _cmul(pr, pi, sr, si)
        xr, xi = xr + mr, xi + mi
    return xr, xi


def _pick_row(x, row8, r):
    return jnp.sum(jnp.where(row8 == r, x, 0.0), axis=0, keepdims=True)


SCAN_LB = 512
SCAN_TS = 512


def _scan_fwd(bu_re, bu_im, a_re, a_im):
    S, L = bu_re.shape
    ts, lb = min(SCAN_TS, S), SCAN_LB
    nt = ts // 8

    def kern(br_ref, bi_ref, ar_ref, ai_ref, or_ref, oi_ref, cr_ref, ci_ref):
        @pl.when(pl.program_id(1) == 0)
        def _():
            cr_ref[...] = jnp.zeros_like(cr_ref)
            ci_ref[...] = jnp.zeros_like(ci_ref)

        row8 = _iota((8, lb), 0)
        ar, ai = ar_ref[...], ai_ref[...]
        a2 = _cmul(ar, ai, ar, ai)
        a4 = _cmul(*a2, *a2)
        pows = ((ar, ai), a2, a4)
        tr, ti = _power_table(ar, ai, row8, False)

        def body(i, carry):
            cr, ci = carry
            off = pl.multiple_of(i * 8, 8)
            xr, xi = _tile_scan(br_ref[pl.ds(off, 8), :], bi_ref[pl.ds(off, 8), :], pows, row8, False)
            mr, mi = _cmul(tr, ti, cr, ci)
            xr, xi = xr + mr, xi + mi
            or_ref[pl.ds(off, 8), :] = xr
            oi_ref[pl.ds(off, 8), :] = xi
            return _pick_row(xr, row8, 7), _pick_row(xi, row8, 7)

        cr, ci = lax.fori_loop(0, nt, body, (cr_ref[...], ci_ref[...]))
        cr_ref[...] = cr
        ci_ref[...] = ci

    blk = pl.BlockSpec((ts, lb), lambda j, i: (i, j))
    par = pl.BlockSpec((1, lb), lambda j, i: (0, j))
    return pl.pallas_call(
        kern, name="s5_scan_fwd", grid=(L // lb, S // ts), in_specs=[blk, blk, par, par], out_specs=[blk, blk],
        out_shape=[jax.ShapeDtypeStruct((S, L), F32)] * 2,
        scratch_shapes=[pltpu.VMEM((1, lb), F32), pltpu.VMEM((1, lb), F32)],
        compiler_params=_cparams(("parallel", "arbitrary")))(bu_re, bu_im, a_re, a_im)


def _scan_bwd(dst_re, dst_im, st_re, st_im, a_re, a_im):
    S, L = dst_re.shape
    ts, lb = min(SCAN_TS, S), SCAN_LB
    nt = ts // 8
    nb = S // ts
    r8 = ts // 8

    def kern(dr_ref, di_ref, sr_ref, si_ref, pr_ref, pi_ref, ar_ref, ai_ref, gr_ref, gi_ref, dar_ref, dai_ref,
             cr_ref, ci_ref):
        step = pl.program_id(1)
        blk = nb - 1 - step

        @pl.when(step == 0)
        def _():
            cr_ref[...] = jnp.zeros_like(cr_ref)
            ci_ref[...] = jnp.zeros_like(ci_ref)
            dar_ref[...] = jnp.zeros_like(dar_ref)
            dai_ref[...] = jnp.zeros_like(dai_ref)

        row8 = _iota((8, lb), 0)
        ar, ai = ar_ref[...], ai_ref[...]
        nai = -ai
        a2 = _cmul(ar, nai, ar, nai)
        a4 = _cmul(*a2, *a2)
        pows = ((ar, nai), a2, a4)
        tr, ti = _power_table(ar, nai, row8, True)
        halo_r = jnp.where(blk == 0, 0.0, pr_ref[...])
        halo_i = jnp.where(blk == 0, 0.0, pi_ref[...])

        def body(n, carry):
            cr, ci, acc_r, acc_i = carry
            i = nt - 1 - n
            off = pl.multiple_of(i * 8, 8)
            gr, gi = _tile_scan(dr_ref[pl.ds(off, 8), :], di_ref[pl.ds(off, 8), :], pows, row8, True)
            mr, mi = _cmul(tr, ti, cr, ci)
            gr, gi = gr + mr, gi + mi
            gr_ref[pl.ds(off, 8), :] = gr
            gi_ref[pl.ds(off, 8), :] = gi
            poff = pl.multiple_of(jnp.maximum(i - 1, 0) * 8, 8)
            before_r = jnp.where(i == 0, halo_r, sr_ref[pl.ds(poff, 8), :])
            before_i = jnp.where(i == 0, halo_i, si_ref[pl.ds(poff, 8), :])
            last_r = _pick_row(before_r, row8, 7)
            last_i = _pick_row(before_i, row8, 7)
            spr = jnp.where(row8 >= 1, pltpu.roll(sr_ref[pl.ds(off, 8), :], 1, 0), last_r)
            spi = jnp.where(row8 >= 1, pltpu.roll(si_ref[pl.ds(off, 8), :], 1, 0), last_i)
            acc_r = acc_r + gr * spr + gi * spi
            acc_i = acc_i + gi * spr - gr * spi
            return _pick_row(gr, row8, 0), _pick_row(gi, row8, 0), acc_r, acc_i

        zero = jnp.zeros((8, lb), F32)
        cr, ci, acc_r, acc_i = lax.fori_loop(0, nt, body, (cr_ref[...], ci_ref[...], zero, zero))
        cr_ref[...] = cr
        ci_ref[...] = ci
        dar_ref[...] += jnp.sum(acc_r, axis=0, keepdims=True)
        dai_ref[...] += jnp.sum(acc_i, axis=0, keepdims=True)

    blk = pl.BlockSpec((ts, lb), lambda j, i: (nb - 1 - i, j))
    halo = pl.BlockSpec((8, lb), lambda j, i: (jnp.maximum((nb - 1 - i) * r8 - 1, 0), j))
    par = pl.BlockSpec((1, lb), lambda j, i: (0, j))
    return pl.pallas_call(
        kern, name="s5_scan_bwd", grid=(L // lb, nb), in_specs=[blk, blk, blk, blk, halo, halo, par, par],
        out_specs=[blk, blk, par, par],
        out_shape=[jax.ShapeDtypeStruct((S, L), F32)] * 2 + [jax.ShapeDtypeStruct((1, L), F32)] * 2,
        scratch_shapes=[pltpu.VMEM((1, lb), F32), pltpu.VMEM((1, lb), F32)],
        compiler_params=_cparams(("parallel", "arbitrary")))(dst_re, dst_im, st_re, st_im, st_re, st_im, a_re, a_im)


def _loss_grad(x, target, gain, ts=256):
    S, D = x.shape

    def kern(x_ref, t_ref, g_ref, loss_ref, dx_ref, dg_ref):
        i = pl.program_id(0)
        tgt = t_ref[...]

        def f(xv, gv):
            err = _rms(xv, gv) - tgt
            return 0.5 * jnp.mean(err * err, axis=-1, keepdims=True)

        rowloss, vjp = jax.vjp(f, x_ref[...], g_ref[...])
        dx, dg = vjp(jnp.ones_like(rowloss))
        dx_ref[...] = dx

        @pl.when(i == 0)
        def _():
            loss_ref[...] = jnp.zeros_like(loss_ref)
            dg_ref[...] = jnp.zeros_like(dg_ref)

        loss_ref[...] += jnp.broadcast_to(jnp.sum(rowloss, axis=0, keepdims=True), loss_ref.shape)
        dg_ref[...] += dg

    row = pl.BlockSpec((ts, D), lambda i: (i, 0))
    return pl.pallas_call(
        kern, name="loss_grad", grid=(S // ts,), in_specs=[row, row, pl.BlockSpec((1, D), lambda i: (0, 0))],
        out_specs=[pl.BlockSpec((8, 128), lambda i: (0, 0)), row, pl.BlockSpec((1, D), lambda i: (0, 0))],
        out_shape=[jax.ShapeDtypeStruct((8, 128), F32), jax.ShapeDtypeStruct((S, D), F32),
                   jax.ShapeDtypeStruct((1, D), F32)],
        compiler_params=_cparams(("arbitrary",)))(x, target, gain)


def _rms_fwd(x, g, name):
    return _rowwise(_rms_fn, [_blk(x)], [g], [x.shape[1]], name, out_dtypes=[BF16])[0]


def _rms_bwd(x, g, dy, name, add=None):
    return _rowwise_bwd(_rms_fn, [_blk(x)], [g], [dy], name, adds=None if add is None else {0: add})


FFN_TC = 1408


def _common_fwd(x, mem, P, L):
    hx = _rms_fwd(x, P['xa_norm'], L + "xa_norm")
    q = _matmul(hx, P['xa_wq'], name=L + "xa_q")
    memn = _rms_fwd(mem, P['mem_norm'], L + "mem_norm")
    kv = _matmul(memn, P['xa_wkv'], name=L + "xa_kv")
    att = _rowwise(_xattn_fn, [_blk(q)], [kv], [1024], L + "xattn", out_dtypes=[BF16])[0]
    x2 = _matmul(att, P['xa_wo'], res=x, name=L + "xa_o")
    hf = _rms_fwd(x2, P['ffn_norm'], L + "ffn_norm")
    hu = _matmul(hf, P['ffn_w_up'], name=L + "ffn_up")
    cw = P['ffn_conv']
    act = _conv_post([(hu, 0, cw, 0), (hu, 2, cw, 2)], _ffn_post, 2, FFN_TC, L + "ffn_conv", out_dtype=BF16)
    x3 = _matmul(act, P['ffn_w_down'], res=x2, name=L + "ffn_down")
    return x3, (x, mem, hx, q, memn, kv, att, x2, hf, hu, act)


def _common_bwd(saved, dx3, P, L):
    x, mem, hx, q, memn, kv, att, x2, hf, hu, act = saved
    G = {}
    dact = _matmul(dx3, P['ffn_w_down'], "nt", name=L + "ffn_down_dx")
    G['ffn_w_down'] = _matmul(act, dx3, "tn", name=L + "ffn_down_dw")
    cw = P['ffn_conv']
    dhu_u, dcw_u, dhu_g, dcw_g = _conv_post_bwd([(hu, 0, cw, 0), (hu, 2, cw, 2)], _ffn_post, 2, FFN_TC, dact,
                                                L + "ffn_conv_bwd")
    G['ffn_conv'] = jnp.concatenate([dcw_u, dcw_g], axis=1)
    dhf = _matmul_cat([dhu_u, dhu_g], P['ffn_w_up'], "nt", name=L + "ffn_up_dx")
    G['ffn_w_up'] = jnp.concatenate([_matmul(hf, dhu_u, "tn", name=L + "ffn_up_dw_up"),
                                     _matmul(hf, dhu_g, "tn", name=L + "ffn_up_dw_gate")], axis=1)
    dx2, G['ffn_norm'] = _rms_bwd(x2, P['ffn_norm'], dhf, L + "ffn_norm_bwd", add=dx3)
    datt = _matmul(dx2, P['xa_wo'], "nt", name=L + "xa_o_dx")
    G['xa_wo'] = _matmul(att, dx2, "tn", name=L + "xa_o_dw")
    dq, dkv = _rowwise_bwd(_xattn_fn, [_blk(q)], [kv], [datt], L + "xattn_bwd", out_dtypes=[BF16])
    dhx = _matmul(dq, P['xa_wq'], "nt", name=L + "xa_q_dx")
    G['xa_wq'] = _matmul(hx, dq, "tn", name=L + "xa_q_dw")
    dmemn = _matmul(dkv, P['xa_wkv'], "nt", name=L + "xa_kv_dx")
    G['xa_wkv'] = _matmul(memn, dkv, "tn", name=L + "xa_kv_dw")
    _, G['mem_norm'] = _rms_bwd(mem, P['mem_norm'], dmemn, L + "mem_norm_bwd")
    dx, G['xa_norm'] = _rms_bwd(x, P['xa_norm'], dhx, L + "xa_norm_bwd", add=dx2)
    return dx, G


U_COLS = (2048, 512)


def _even_fwd(x, P):
    S = x.shape[0]
    h0 = _rms_fwd(x, P['mix_norm'], "l0_mix_norm")
    proj = _matmul(h0, P['w_in'], name="l0_in")
    tabs = _ret_tables(S)
    o_raw, rstates = _ret_call(proj, tabs)
    o = _rowwise(_ret_post_fn, [_blk(o_raw), _blk(proj, 512, 3)], [P['ret_norm']], [512], "l0_ret_post",
                 out_dtypes=[BF16])[0]
    prep_in = (P['s5_lambda_re'], P['s5_lambda_im'], P['s5_log_dt'], P['s5_b_re'], P['s5_b_im'], P['s5_c_re'],
               P['s5_c_im'])
    a_re, a_im, bd_re, bd_im, cd_re, cd_im = _s5_prep(prep_in)
    a_re_f, a_im_f = a_re.reshape(1, 2048), a_im.reshape(1, 2048)
    bu_re = _matmul(proj, bd_re, name="l0_s5_bu_re", a_cols=U_COLS)
    bu_im = _matmul(proj, bd_im, name="l0_s5_bu_im", a_cols=U_COLS)
    st_re, st_im = _scan_fwd(bu_re, bu_im, a_re_f, a_im_f)
    y1 = _matmul(st_re, cd_re, name="l0_s5_y_re")
    y2 = _matmul(st_im, cd_im, name="l0_s5_y_im")
    yg = _rowwise(_s5_post_fn, [_blk(y1), _blk(y2), _blk(proj, 512, 4)],
                  [P['s5_d'], P['s5_w_glu'], P['s5_b_glu']], [512], "l0_s5_post", out_dtypes=[BF16])[0]
    x1 = _matmul_cat([o, yg], P['w_out'], "nn", res=x, name="l0_out")
    saved = (x, h0, proj, tabs, o_raw, rstates, prep_in, a_re_f, a_im_f, bd_re, bd_im, cd_re, cd_im, st_re, st_im,
             y1, y2, o, yg)
    return x1, saved


def _even_bwd(saved, dx1, P):
    (x, h0, proj, tabs, o_raw, rstates, prep_in, a_re_f, a_im_f, bd_re, bd_im, cd_re, cd_im, st_re, st_im, y1, y2,
     o, yg) = saved
    G = {}
    dmerged = _matmul(dx1, P['w_out'], "nt", name="l0_out_dx")
    G['w_out'] = jnp.concatenate([_matmul(o, dx1, "tn", name="l0_out_dw_ret"),
                                  _matmul(yg, dx1, "tn", name="l0_out_dw_s5")], axis=0)
    do_raw, dgate, G['ret_norm'] = _rowwise_bwd(
        _ret_post_fn, [_blk(o_raw), _blk(proj, 512, 3)], [P['ret_norm']], [_blk(dmerged, 512, 0)], "l0_ret_post_bwd",
        out_dtypes=[F32, BF16])
    dq, dk, dv = _ret_call(proj, tabs, states=rstates, do=do_raw)
    dy1, dy2, du_a, G['s5_d'], G['s5_w_glu'], G['s5_b_glu'] = _rowwise_bwd(
        _s5_post_fn, [_blk(y1), _blk(y2), _blk(proj, 512, 4)], [P['s5_d'], P['s5_w_glu'], P['s5_b_glu']],
        [_blk(dmerged, 512, 1)], "l0_s5_post_bwd", out_dtypes=[BF16, BF16, F32])
    dst_re = _matmul(dy1, cd_re, "nt", name="l0_s5_y_re_dx")
    dcd_re = _matmul(st_re, dy1, "tn", name="l0_s5_y_re_dw")
    dst_im = _matmul(dy2, cd_im, "nt", name="l0_s5_y_im_dx")
    dcd_im = _matmul(st_im, dy2, "tn", name="l0_s5_y_im_dw")
    dbu_re, dbu_im, da_re, da_im = _scan_bwd(dst_re, dst_im, st_re, st_im, a_re_f, a_im_f)
    du = _matmul(dbu_re, bd_re, "nt", res=du_a, name="l0_s5_bu_re_dx")
    du = _matmul(dbu_im, bd_im, "nt", res=du, name="l0_s5_bu_im_dx", out_dtype=BF16)
    dbd_re = _matmul(proj, dbu_re, "tn", name="l0_s5_bu_re_dw", a_cols=U_COLS)
    dbd_im = _matmul(proj, dbu_im, "tn", name="l0_s5_bu_im_dw", a_cols=U_COLS)
    dprep = _s5_prep(prep_in, cots=(da_re.reshape(32, 64), da_im.reshape(32, 64), dbd_re, dbd_im, dcd_re, dcd_im))
    for n, d in zip(('s5_lambda_re', 's5_lambda_im', 's5_log_dt', 's5_b_re', 's5_b_im', 's5_c_re', 's5_c_im'), dprep):
        G[n] = d
    pieces = [dq, dk, dv, dgate, du]
    dh0 = _matmul_cat(pieces, P['w_in'], "nt", name="l0_in_dx")
    G['w_in'] = jnp.concatenate([_matmul(h0, p, "tn", name="l0_in_dw_%d" % n) for n, p in enumerate(pieces)], axis=1)
    dx, G['mix_norm'] = _rms_bwd(x, P['mix_norm'], dh0, "l0_mix_norm_bwd", add=dx1)
    return dx, G


def _odd_fwd(x, P):
    h1 = _rms_fwd(x, P['mix_norm'], "l1_mix_norm")
    pm = _matmul(h1, P['w_main'], name="l1_in_main")
    pt = _matmul(h1, P['w_tail'], name="l1_in_tail")
    qkv = _conv_post([(pm, 0, P['conv'], 0)], _silu, 3, 1024, "l1_conv")
    g_e, beta_e = _rowwise(_gdn_gates_fn, [_blk(pt)], [P['a_log_p'], P['dtb_p']], [1024, 1024], "l1_gdn_gates")
    w, u, qd, kd, qk, tinv = _gdn_intra_call(qkv, g_e, beta_e)
    o_raw, gstates = _gdn_pass(w, u, qd, kd, qk, g_e)
    og = _rowwise(_gdn_post_fn, [_blk(o_raw), _blk(pm, 1024, 3)], [P['o_norm']], [1024], "l1_gdn_post",
                  out_dtypes=[BF16])[0]
    x1 = _matmul(og, P['w_out'], res=x, name="l1_out")
    return x1, (x, h1, pm, pt, qkv, g_e, beta_e, w, u, qd, kd, qk, tinv, o_raw, gstates, og)


def _odd_bwd(saved, dx1, P):
    x, h1, pm, pt, qkv, g_e, beta_e, w, u, qd, kd, qk, tinv, o_raw, gstates, og = saved
    G = {}
    dog = _matmul(dx1, P['w_out'], "nt", name="l1_out_dx")
    G['w_out'] = _matmul(og, dx1, "tn", name="l1_out_dw")
    do_raw, dz, G['o_norm'] = _rowwise_bwd(_gdn_post_fn, [_blk(o_raw), _blk(pm, 1024, 3)], [P['o_norm']], [dog],
                                           "l1_gdn_post_bwd", out_dtypes=[F32, BF16])
    dw, du, dqd, dkd, dqk, dg_pass = _gdn_pass(w, u, qd, kd, qk, g_e, states=gstates, do=do_raw)
    dqkv = _gdn_intra_call(qkv, g_e, beta_e, cots=(dw, du, dqd, dkd, dqk, dg_pass, tinv))
    dg_e, dbeta_e = dqkv[3], dqkv[4]
    dpt, G['a_log_p'], G['dtb_p'] = _rowwise_bwd(_gdn_gates_fn, [_blk(pt)], [P['a_log_p'], P['dtb_p']],
                                                 [dg_e, dbeta_e], "l1_gdn_gates_bwd", out_dtypes=[BF16])
    pieces, dcw = [], []
    for part in range(3):
        dxp, dwp = _conv_post_bwd([(pm, part, P['conv'], part)], _silu, 1, 1024, dqkv[part],
                                  "l1_conv_bwd_%d" % part)
        pieces.append(dxp)
        dcw.append(dwp)
    G['conv'] = jnp.concatenate(dcw, axis=1)
    pieces += [dz, dpt]
    dh1 = _matmul_cat(pieces, P['w_all'], "nt", name="l1_in_dx")
    G['w_all'] = jnp.concatenate([_matmul(h1, p, "tn", name="l1_in_dw_%d" % n) for n, p in enumerate(pieces)], axis=1)
    dx, G['mix_norm'] = _rms_bwd(x, P['mix_norm'], dh1, "l1_mix_norm_bwd", add=dx1)
    return dx, G


def _row(v):
    return v.reshape(1, -1)


def _local_step(x, mem, target, W):
    P0 = {
        'mix_norm': _row(W['l0_mix_norm']), 'w_in': W['l0_w_in'], 'ret_norm': _row(W['l0_ret_norm']),
        's5_lambda_re': W['l0_s5_lambda_re'], 's5_lambda_im': W['l0_s5_lambda_im'],
        's5_log_dt': W['l0_s5_log_dt'].reshape(32, 1),
        's5_b_re': W['l0_s5_b_re'].reshape(512, 64), 's5_b_im': W['l0_s5_b_im'].reshape(512, 64),
        's5_c_re': W['l0_s5_c_re'].reshape(2048, 16), 's5_c_im': W['l0_s5_c_im'].reshape(2048, 16),
        's5_d': _row(W['l0_s5_d']), 's5_w_glu': W['l0_s5_w_glu'].astype(F32), 's5_b_glu': _row(W['l0_s5_b_glu']),
        'w_out': W['l0_w_out'],
    }
    w_in1 = W['l1_w_in']
    pad8 = jnp.zeros((8,), F32)
    w_all = jnp.pad(w_in1, ((0, 0), (0, 112)))
    P1 = {
        'mix_norm': _row(W['l1_mix_norm']), 'w_main': w_in1[:, :4096], 'w_tail': w_all[:, 4096:], 'w_all': w_all,
        'conv': W['l1_conv'],
        'a_log_p': _row(jnp.concatenate([pad8, W['l1_a_log'], jnp.zeros((112,), F32)])),
        'dtb_p': _row(jnp.concatenate([pad8, W['l1_dt_bias'], jnp.zeros((112,), F32)])),
        'o_norm': _row(W['l1_o_norm']), 'w_out': W['l1_w_out'],
    }
    C = []
    for L in ('l0_', 'l1_'):
        C.append({'xa_norm': _row(W[L + 'xa_norm']), 'mem_norm': _row(W[L + 'mem_norm']), 'xa_wq': W[L + 'xa_wq'],
                  'xa_wkv': W[L + 'xa_wkv'], 'xa_wo': W[L + 'xa_wo'], 'ffn_norm': _row(W[L + 'ffn_norm']),
                  'ffn_w_up': W[L + 'ffn_w_up'], 'ffn_conv': W[L + 'ffn_conv'], 'ffn_w_down': W[L + 'ffn_w_down']})

    x1, s_even = _even_fwd(x, P0)
    x3, s_c0 = _common_fwd(x1, mem, C[0], "l0_")
    x4, s_odd = _odd_fwd(x3, P1)
    x6, s_c1 = _common_fwd(x4, mem, C[1], "l1_")
    loss_tile, dx6, d_final = _loss_grad(x6, target, _row(W['final_norm']))

    G = {'final_norm': d_final.reshape(-1)}
    dx4, g = _common_bwd(s_c1, dx6, C[1], "l1_")
    for k, v in g.items():
        G['l1_' + k] = v
    dx3, g = _odd_bwd(s_odd, dx4, P1)
    G['l1_mix_norm'] = g['mix_norm']
    G['l1_w_in'] = g['w_all'][:, :4112]
    G['l1_conv'] = g['conv']
    G['l1_a_log'] = g['a_log_p'][0, 8:16]
    G['l1_dt_bias'] = g['dtb_p'][0, 8:16]
    G['l1_o_norm'] = g['o_norm']
    G['l1_w_out'] = g['w_out']
    dx1, g = _common_bwd(s_c0, dx3, C[0], "l0_")
    for k, v in g.items():
        G['l0_' + k] = v
    dx0, g = _even_bwd(s_even, dx1, P0)
    for k, v in g.items():
        G['l0_' + k] = v
    return loss_tile, dx0, G


ANY = pl.BlockSpec(memory_space=pl.ANY)


def _place():
    return lax.axis_index("x"), lax.axis_index("y"), lax.axis_index("c")


def _chip_peers(x, y):
    return [(1 - x, y), (x, 1 - y), (1 - x, 1 - y)]


def _half(ref, mode, shard, j, h, split):
    r, w = shard
    rh = r // 2 if split else r
    h = h if split else 0
    if mode == 'row':
        return ref.at[pl.ds(j * r + h * rh, rh), :]
    if mode == 'col':
        return ref.at[pl.ds(h * rh, rh), pl.ds(j * w, w)]
    return ref.at[j, pl.ds(h * rh, rh), :]


def _place_shard(shard, mode, me_arr, name):
    r, w = shard.shape
    dtype = BF16 if mode != 'tap' else shard.dtype
    if mode == 'tap':
        mode = 'slab'
    tr = _row_tile(r, w)
    nb = r // tr

    def kern(me_ref, s_ref, o_ref):
        o_ref[...] = s_ref[...].astype(o_ref.dtype)

    if mode == 'row':
        full, o_spec = (4 * r, w), pl.BlockSpec((tr, w), lambda i, me: (me[0] * nb + i, 0))
    elif mode == 'col':
        full, o_spec = (r, 4 * w), pl.BlockSpec((tr, w), lambda i, me: (i, me[0]))
    else:
        full, o_spec = (4, r, w), pl.BlockSpec((None, tr, w), lambda i, me: (me[0], i, 0))
    grid_spec = pltpu.PrefetchScalarGridSpec(
        num_scalar_prefetch=1, grid=(nb,), in_specs=[pl.BlockSpec((tr, w), lambda i, me: (i, 0))], out_specs=o_spec)
    return pl.pallas_call(kern, name=name, grid_spec=grid_spec, out_shape=jax.ShapeDtypeStruct(full, dtype),
                          compiler_params=_cparams(("parallel",)))(me_arr, shard)


def _gather_placed(fulls, modes, shards, splits):
    n = len(fulls)

    def body(*refs):
        outs = refs[n:2 * n]
        send_sems, recv_sems = refs[2 * n:]
        x, y, c = _place()
        peers = _chip_peers(x, y)
        me = 2 * x + y

        def win(a, j, h):
            return _half(outs[a], modes[a], shards[a], j, h, splits[a])

        def copy(a, k, j, h, to):
            return pltpu.make_async_remote_copy(src_ref=win(a, j, h), dst_ref=win(a, j, h),
                                                send_sem=send_sems.at[6 * a + k], recv_sem=recv_sems.at[6 * a + k],
                                                device_id=to, device_id_type=MESH)

        over_ici = [copy(a, k, me, c, (p[0], p[1], c)) for a in range(n) for k, p in enumerate(peers)]
        for cp in over_ici:
            cp.start()
        passed = []
        for a in range(n):
            for k, p in enumerate(peers):
                j = 2 * p[0] + p[1]
                copy(a, k, j, c, (p[0], p[1], c)).wait_recv()
                if splits[a]:
                    fwd = copy(a, 3 + k, j, c, (x, y, 1 - c))
                    fwd.start()
                    passed.append(fwd)
        for a in range(n):
            if splits[a]:
                for k, p in enumerate(peers):
                    copy(a, 3 + k, 2 * p[0] + p[1], 1 - c, (x, y, 1 - c)).wait_recv()
        for cp in over_ici + passed:
            cp.wait_send()

    return pl.pallas_call(
        body, name="gather_weights", in_specs=[ANY] * n, out_specs=[ANY] * n,
        out_shape=[jax.ShapeDtypeStruct(f.shape, f.dtype) for f in fulls],
        input_output_aliases={a: a for a in range(n)},
        scratch_shapes=[pltpu.SemaphoreType.DMA((6 * n,)), pltpu.SemaphoreType.DMA((6 * n,))],
    )(*fulls)


_FLIPS = [(dx, dy, dc) for dx in (0, 1) for dy in (0, 1) for dc in (0, 1) if (dx, dy, dc) != (0, 0, 0)]


def _send_other_half(gs, small):
    n = len(gs)

    def body(*refs):
        ins, outs = refs[:n], refs[n + 1:2 * n + 1]
        small_ref = refs[2 * n + 1]
        send_sems, recv_sems, small_send, small_recv = refs[2 * n + 2:]
        x, y, c = _place()
        me = 4 * x + 2 * y + c

        def peer(f):
            return (x ^ f[0], y ^ f[1], c ^ f[2])

        def small_copy(k, slab, to):
            return pltpu.make_async_remote_copy(src_ref=small_ref.at[slab], dst_ref=small_ref.at[slab],
                                                send_sem=small_send.at[k], recv_sem=small_recv.at[k], device_id=to,
                                                device_id_type=MESH)

        cps = []
        for a in range(n):
            rh = gs[a].shape[1] // 2
            cps.append(pltpu.make_async_remote_copy(
                src_ref=ins[a].at[:, pl.ds((1 - c) * rh, rh), :], dst_ref=outs[a], send_sem=send_sems.at[a],
                recv_sem=recv_sems.at[a], device_id=(x, y, 1 - c), device_id_type=MESH))
        smalls = [small_copy(k, me, peer(f)) for k, f in enumerate(_FLIPS)]
        for cp in cps + smalls:
            cp.start()
        for cp in cps:
            cp.wait()
        for k, f in enumerate(_FLIPS):
            p = peer(f)
            small_copy(k, 4 * p[0] + 2 * p[1] + p[2], p).wait_recv()
        for cp in smalls:
            cp.wait_send()

    outs = pl.pallas_call(
        body, name="send_other_half", in_specs=[ANY] * (n + 1), out_specs=[ANY] * (n + 1),
        out_shape=[jax.ShapeDtypeStruct((g.shape[0], g.shape[1] // 2, g.shape[2]), g.dtype) for g in gs]
        + [jax.ShapeDtypeStruct(small.shape, small.dtype)],
        input_output_aliases={n: n},
        scratch_shapes=[pltpu.SemaphoreType.DMA((n,)), pltpu.SemaphoreType.DMA((n,)),
                        pltpu.SemaphoreType.DMA((7,)), pltpu.SemaphoreType.DMA((7,))],
    )(*gs, small)
    return outs[:n], outs[n]


def _send_to_chips(ps, widths):
    n = len(ps)

    def body(*refs):
        ins, outs = refs[:n], refs[n:2 * n]
        send_sems, recv_sems = refs[2 * n:]
        x, y, c = _place()
        peers = _chip_peers(x, y)
        me = 2 * x + y

        def src(a, j):
            if ps[a].shape[0] == 4:
                return ins[a].at[j]
            return ins[a].at[0, :, pl.ds(j * widths[a], widths[a])]

        def copy(a, k, j, dst_slab, to):
            return pltpu.make_async_remote_copy(src_ref=src(a, j), dst_ref=outs[a].at[dst_slab],
                                                send_sem=send_sems.at[3 * a + k], recv_sem=recv_sems.at[3 * a + k],
                                                device_id=(to[0], to[1], c), device_id_type=MESH)

        sends = [copy(a, k, 2 * p[0] + p[1], me, p) for a in range(n) for k, p in enumerate(peers)]
        for cp in sends:
            cp.start()
        for a in range(n):
            for k, p in enumerate(peers):
                copy(a, k, me, 2 * p[0] + p[1], p).wait_recv()
        for cp in sends:
            cp.wait_send()

    return pl.pallas_call(
        body, name="send_to_chips", in_specs=[ANY] * n, out_specs=[ANY] * n,
        out_shape=[jax.ShapeDtypeStruct((4, p.shape[1], w), p.dtype) for p, w in zip(ps, widths)],
        scratch_shapes=[pltpu.SemaphoreType.DMA((3 * n,)), pltpu.SemaphoreType.DMA((3 * n,))],
    )(*ps)


def _share_halves(bufs):
    n = len(bufs)

    def body(*refs):
        outs = refs[n:2 * n]
        send_sems, recv_sems = refs[2 * n:]
        x, y, c = _place()
        sends, waits = [], []
        for a in range(n):
            rh = bufs[a].shape[0] // 2
            mine = outs[a].at[pl.ds(c * rh, rh), :]
            other = outs[a].at[pl.ds((1 - c) * rh, rh), :]
            sends.append(pltpu.make_async_remote_copy(src_ref=mine, dst_ref=mine, send_sem=send_sems.at[a],
                                                      recv_sem=recv_sems.at[a], device_id=(x, y, 1 - c),
                                                      device_id_type=MESH))
            waits.append(pltpu.make_async_remote_copy(src_ref=mine, dst_ref=other, send_sem=send_sems.at[a],
                                                      recv_sem=recv_sems.at[a], device_id=(x, y, 1 - c),
                                                      device_id_type=MESH))
        for cp in sends:
            cp.start()
        for cp in waits:
            cp.wait()

    return pl.pallas_call(
        body, name="share_halves", in_specs=[ANY] * n, out_specs=[ANY] * n,
        out_shape=[jax.ShapeDtypeStruct(b.shape, b.dtype) for b in bufs],
        input_output_aliases={a: a for a in range(n)},
        scratch_shapes=[pltpu.SemaphoreType.DMA((n,)), pltpu.SemaphoreType.DMA((n,))],
    )(*bufs)


def _gather_all(mine):
    flips = [(dx, dy, dc) for dx in (0, 1) for dy in (0, 1) for dc in (0, 1) if (dx, dy, dc) != (0, 0, 0)]

    def body(x_ref, out_ref, send_sems, recv_sems, local_sem):
        x, y, c = _place()
        me = 4 * x + 2 * y + c

        def peer(f):
            return (x ^ f[0], y ^ f[1], c ^ f[2])

        def copy(k, slab, to):
            return pltpu.make_async_remote_copy(src_ref=x_ref, dst_ref=out_ref.at[slab], send_sem=send_sems.at[k],
                                                recv_sem=recv_sems.at[k], device_id=to, device_id_type=MESH)

        own = pltpu.make_async_copy(x_ref, out_ref.at[me], local_sem)
        own.start()
        sends = [copy(k, me, peer(f)) for k, f in enumerate(flips)]
        for s in sends:
            s.start()
        for k, f in enumerate(flips):
            p = peer(f)
            copy(k, 4 * p[0] + 2 * p[1] + p[2], p).wait_recv()
        for s in sends:
            s.wait_send()
        own.wait()

    return pl.pallas_call(
        body, name="gather_all", in_specs=[ANY], out_specs=ANY,
        out_shape=jax.ShapeDtypeStruct((8,) + mine.shape, mine.dtype),
        scratch_shapes=[pltpu.SemaphoreType.DMA((7,)), pltpu.SemaphoreType.DMA((7,)), pltpu.SemaphoreType.DMA],
    )(mine)


TILE_BYTES = 2 * 1024 * 1024


def _row_tile(rows, width=1024):
    for t in (512, 352, 256, 176, 128, 64, 32, 16, 8):
        if rows % t == 0 and t * width * 4 <= TILE_BYTES:
            return t
    return rows


def _pair_sum(g, got, c_arr, name):
    ns, r, w = g.shape
    rh = r // 2
    tr = _row_tile(rh, w)
    nb = rh // tr

    def kern(c_ref, g_ref, o_ref, out_ref):
        out_ref[...] = (g_ref[...] + o_ref[...]).astype(BF16)

    grid_spec = pltpu.PrefetchScalarGridSpec(
        num_scalar_prefetch=1, grid=(ns, nb),
        in_specs=[pl.BlockSpec((None, tr, w), lambda j, i, c_ref: (j, c_ref[0] * nb + i, 0)),
                  pl.BlockSpec((None, tr, w), lambda j, i, c_ref: (j, i, 0))],
        out_specs=pl.BlockSpec((None, tr, w), lambda j, i, c_ref: (j, i, 0)))
    return pl.pallas_call(kern, name=name, grid_spec=grid_spec, out_shape=jax.ShapeDtypeStruct((ns, rh, w), BF16),
                          compiler_params=_cparams(("parallel", "parallel")))(c_arr, g, got)


def _chip_sum(pair, recv, w, mc_arr, name):
    rh = pair.shape[1]
    tr = _row_tile(rh, w)
    nb = rh // tr

    def kern(mc_ref, own_ref, r1_ref, r2_ref, r3_ref, out_ref):
        acc = own_ref[...].astype(F32)
        for r_ref in (r1_ref, r2_ref, r3_ref):
            acc = acc + r_ref[...].astype(F32)
        out_ref[...] = acc

    if pair.shape[0] == 4:
        own_spec = pl.BlockSpec((None, tr, w), lambda i, mc: (mc[0], i, 0))
    else:
        own_spec = pl.BlockSpec((None, tr, w), lambda i, mc: (0, i, mc[0]))
    recv_specs = [pl.BlockSpec((None, tr, w), functools.partial(lambda i, mc, d: ((mc[0] + d) % 4, i, 0), d=d))
                  for d in (1, 2, 3)]
    grid_spec = pltpu.PrefetchScalarGridSpec(
        num_scalar_prefetch=1, grid=(nb,), in_specs=[own_spec] + recv_specs,
        out_specs=pl.BlockSpec((tr, w), lambda i, mc: (mc[1] * nb + i, 0)))
    return pl.pallas_call(kern, name=name, grid_spec=grid_spec, out_shape=jax.ShapeDtypeStruct((2 * rh, w), F32),
                          compiler_params=_cparams(("parallel",)))(mc_arr, pair, recv, recv, recv)


def _slab_sum(slabs, name):
    n, R, w = slabs.shape
    tr = _row_tile(R)

    def kern(s_ref, o_ref):
        acc = s_ref[0].astype(F32)
        for k in range(1, n):
            acc = acc + s_ref[k].astype(F32)
        o_ref[...] = acc

    return pl.pallas_call(
        kern, name=name, grid=(R // tr,), in_specs=[pl.BlockSpec((n, tr, w), lambda i: (0, i, 0))],
        out_specs=pl.BlockSpec((tr, w), lambda i: (i, 0)), out_shape=jax.ShapeDtypeStruct((R, w), F32),
        compiler_params=_cparams(("parallel",)))(slabs)


def _adamw(w, g, m, v, name):
    R, C = w.shape
    tr = _pick(R, (256, 128, 64, 32, 16, 8))

    def kern(w_ref, g_ref, m_ref, v_ref, d_ref, nm_ref, nv_ref):
        gv = g_ref[...]
        m2 = ADAM_B1 * m_ref[...] + (1.0 - ADAM_B1) * gv
        v2 = ADAM_B2 * v_ref[...] + (1.0 - ADAM_B2) * jnp.square(gv)
        m_hat = m2 / (1.0 - ADAM_B1 ** ADAM_STEP)
        v_hat = v2 / (1.0 - ADAM_B2 ** ADAM_STEP)
        d_ref[...] = -ADAM_LR * (m_hat / (jnp.sqrt(v_hat) + ADAM_EPS) + ADAM_WD * w_ref[...])
        nm_ref[...] = m2
        nv_ref[...] = v2

    spec = pl.BlockSpec((tr, C), lambda i: (i, 0))
    return pl.pallas_call(
        kern, name=name, grid=(R // tr,), in_specs=[spec] * 4, out_specs=[spec] * 3,
        out_shape=[jax.ShapeDtypeStruct((R, C), F32)] * 3, compiler_params=_cparams(("parallel",)))(w, g, m, v)


def _pack_small(vals):
    flat = jnp.concatenate([vals[n].astype(F32).reshape(-1) for n in SMALL_NAMES])
    rows = -(-flat.shape[0] // (8 * LANES)) * 8
    return jnp.pad(flat, (0, rows * LANES - flat.shape[0])).reshape(rows, LANES)


def _unpack_small(packed, shapes):
    flat = packed.reshape(-1)
    out = {}
    off = 0
    for n in SMALL_NAMES:
        size = int(np.prod(shapes[n]))
        out[n] = flat[off:off + size].reshape(shapes[n])
        off += size
    return out


def _gather_weights(A):
    names = MATRIX_NAMES + list(CONVS)
    me_arr = jnp.reshape(2 * lax.axis_index("x") + lax.axis_index("y"), (1,)).astype(jnp.int32)
    modes = [_matrix_mode(n) for n in MATRIX_NAMES] + ['slab'] * len(CONVS)
    fulls = [_place_shard(A[n], 'tap' if n in CONVS else m, me_arr, "place_" + n) for n, m in zip(names, modes)]
    outs = _gather_placed(fulls, modes, [A[n].shape for n in names], [n not in CONVS for n in names])
    W = {}
    for n, m, o in zip(names, modes, outs):
        W[n] = jnp.concatenate([o[j] for j in range(4)], axis=1) if m == 'slab' else o
    return W


def _matrix_mode(n):
    return 'slab' if n == 'l1_w_in' else ('row' if MATRICES[n] == 0 else 'col')


def _reduce_matrix_grads(G):
    c = lax.axis_index("c")
    me = 2 * lax.axis_index("x") + lax.axis_index("y")
    c_arr = jnp.reshape(c, (1,)).astype(jnp.int32)
    mc_arr = jnp.stack([me, c]).astype(jnp.int32)
    gs, widths = [], []
    for n in MATRIX_NAMES:
        g = G[n]
        mode = _matrix_mode(n)
        if mode == 'row':
            gs.append(g.reshape(4, g.shape[0] // 4, g.shape[1]))
            widths.append(g.shape[1])
        elif mode == 'col':
            gs.append(g[None])
            widths.append(g.shape[1] // 4)
        else:
            wd = g.shape[1] // 4
            gs.append(jnp.stack([g[:, j * wd:(j + 1) * wd] for j in range(4)]))
            widths.append(wd)
    packed = _pack_small({n: G[n] for n in SMALL_NAMES})
    small = lax.dynamic_update_slice(jnp.zeros((8,) + packed.shape, F32), packed[None], (2 * me + c, 0, 0))
    got, small = _send_other_half(gs, small)
    pairs = [_pair_sum(g, o, c_arr, "pair_sum_" + n) for n, g, o in zip(MATRIX_NAMES, gs, got)]
    recv = _send_to_chips(pairs, widths)
    halves = [_chip_sum(p, r, w, mc_arr, "chip_sum_" + n) for n, p, r, w in zip(MATRIX_NAMES, pairs, recv, widths)]
    return dict(zip(MATRIX_NAMES, _share_halves(halves))), _slab_sum(small, "sum_small")


def kernel(*args):
    A = dict(zip(ARG_NAMES, args, strict=True))
    x, mem, target = A['x'][0], A['mem'][0], A['loss_target'][0]

    W = _gather_weights(A)
    for n in SMALL_NAMES:
        if n not in CONVS:
            W[n] = A[n]

    loss_tile, grad_x, G = _local_step(x, mem, target, W)
    loss = lax.psum(loss_tile[0, 0], ("x", "y", "c"))

    g_mat, g_small = _reduce_matrix_grads(G)
    g_small = _unpack_small(g_small, {n: G[n].shape for n in SMALL_NAMES})
    me = 2 * lax.axis_index("x") + lax.axis_index("y")
    for n in CONVS:
        wd = A[n].shape[1]
        g_small[n] = lax.dynamic_slice_in_dim(g_small[n], me * wd, wd, axis=1)
    flat_names = [n for n in SMALL_NAMES if n not in CONVS]

    def pack_flat(prefix):
        return _pack_small_flat({n: A[prefix + n] for n in flat_names}, flat_names)

    shapes = {n: A[n].shape for n in flat_names}
    d_s, m_s, v_s = _adamw(pack_flat(''), _pack_small_flat(g_small, flat_names), pack_flat('m_'), pack_flat('v_'),
                           "adamw_small")
    d_s, m_s, v_s = (_unpack_flat(p, shapes, flat_names) for p in (d_s, m_s, v_s))

    grads, deltas, new_m, new_v = {}, {}, {}, {}
    for n in WEIGHTS:
        if n in MATRICES or n in CONVS:
            grads[n] = g_mat[n] if n in MATRICES else g_small[n]
            deltas[n], new_m[n], new_v[n] = _adamw(A[n], grads[n], A['m_' + n], A['v_' + n], "adamw_" + n)
        else:
            grads[n] = g_small[n].reshape(A[n].shape)
            deltas[n], new_m[n], new_v[n] = d_s[n], m_s[n], v_s[n]
    return (loss, grad_x[None], *[grads[n] for n in WEIGHTS], *[deltas[n] for n in WEIGHTS],
            *[new_m[n] for n in WEIGHTS], *[new_v[n] for n in WEIGHTS])


def _pack_small_flat(vals, names):
    flat = jnp.concatenate([vals[n].astype(F32).reshape(-1) for n in names])
    rows = -(-flat.shape[0] // (8 * LANES)) * 8
    return jnp.pad(flat, (0, rows * LANES - flat.shape[0])).reshape(rows, LANES)


def _unpack_flat(packed, shapes, names):
    flat = packed.reshape(-1)
    out = {}
    off = 0
    for n in names:
        size = int(np.prod(shapes[n]))
        out[n] = flat[off:off + size].reshape(shapes[n])
        off += size
    return out
```

```python
import functools
import math

import numpy as np
import jax
import jax.numpy as jnp
from jax import lax
from jax.experimental import pallas as pl
from jax.experimental.pallas import tpu as pltpu

F32 = jnp.float32
BF16 = jnp.bfloat16
EPS = 1e-6
MESH = pl.DeviceIdType.MESH

ADAM_LR = 0.001
ADAM_B1 = 0.9
ADAM_B2 = 0.999
ADAM_EPS = 1e-08
ADAM_WD = 0.01
ADAM_STEP = 10

VMEM_LIMIT_BYTES = 56 * 1024 * 1024
LANES = 1024

WEIGHTS = ['l0_mix_norm', 'l0_w_in', 'l0_ret_norm', 'l0_s5_lambda_re', 'l0_s5_lambda_im', 'l0_s5_b_re', 'l0_s5_b_im',
           'l0_s5_c_re', 'l0_s5_c_im', 'l0_s5_d', 'l0_s5_log_dt', 'l0_s5_w_glu', 'l0_s5_b_glu', 'l0_w_out',
           'l0_xa_norm', 'l0_mem_norm', 'l0_xa_wq', 'l0_xa_wkv', 'l0_xa_wo', 'l0_ffn_norm', 'l0_ffn_w_up',
           'l0_ffn_conv', 'l0_ffn_w_down', 'l1_mix_norm', 'l1_w_in', 'l1_conv', 'l1_a_log', 'l1_dt_bias',
           'l1_o_norm', 'l1_w_out', 'l1_xa_norm', 'l1_mem_norm', 'l1_xa_wq', 'l1_xa_wkv', 'l1_xa_wo',
           'l1_ffn_norm', 'l1_ffn_w_up', 'l1_ffn_conv', 'l1_ffn_w_down', 'final_norm']
ARG_NAMES = (['x', 'mem'] + WEIGHTS + ['loss_target'] + ['m_' + w for w in WEIGHTS] + ['v_' + w for w in WEIGHTS])

MATRICES = {
    'l0_w_in': 1, 'l0_s5_w_glu': 0, 'l0_w_out': 0, 'l0_xa_wq': 0, 'l0_xa_wkv': 1, 'l0_xa_wo': 0, 'l0_ffn_w_up': 1,
    'l0_ffn_w_down': 0, 'l1_w_in': 1, 'l1_w_out': 0, 'l1_xa_wq': 0, 'l1_xa_wkv': 1, 'l1_xa_wo': 0,
    'l1_ffn_w_up': 1, 'l1_ffn_w_down': 0,
}
CONVS = ('l0_ffn_conv', 'l1_conv', 'l1_ffn_conv')
MATRIX_NAMES = [w for w in WEIGHTS if w in MATRICES]
SMALL_NAMES = [w for w in WEIGHTS if w not in MATRICES]


def _cparams(sem=None):
    return pltpu.CompilerParams(dimension_semantics=sem, vmem_limit_bytes=VMEM_LIMIT_BYTES)


def _pick(n, cands):
    for c in cands:
        if n % c == 0:
            return c
    return n


_NN = ((1,), (0,))
_NT = ((1,), (1,))
_TN = ((0,), (0,))


def _dot(a, b, dims, hi):
    if hi is not None:
        return lax.dot_general(a.astype(F32), b.astype(F32), (dims, ((), ())), precision=hi,
                               preferred_element_type=F32)
    return lax.dot_general(a.astype(BF16), b.astype(BF16), (dims, ((), ())), preferred_element_type=F32)


def _make_mm(hi):
    @jax.custom_vjp
    def nn(a, b):
        return _dot(a, b, _NN, hi)

    def nn_f(a, b):
        return nn(a, b), (a, b)

    def nn_b(r, g):
        a, b = r
        return _dot(g, b, _NT, hi), _dot(a, g, _TN, hi)

    nn.defvjp(nn_f, nn_b)

    @jax.custom_vjp
    def nt(a, b):
        return _dot(a, b, _NT, hi)

    def nt_f(a, b):
        return nt(a, b), (a, b)

    def nt_b(r, g):
        a, b = r
        return _dot(g, b, _NN, hi), _dot(g, a, _TN, hi)

    nt.defvjp(nt_f, nt_b)

    @jax.custom_vjp
    def tn(a, b):
        return _dot(a, b, _TN, hi)

    def tn_f(a, b):
        return tn(a, b), (a, b)

    def tn_b(r, g):
        a, b = r
        return _dot(b, g, _NT, hi), _dot(a, g, _NN, hi)

    tn.defvjp(tn_f, tn_b)
    return nn, nt, tn


mm, mm_nt, mm_tn = _make_mm(None)
mmh, mmh_nt, mmh_tn = _make_mm(lax.Precision.HIGHEST)
mm3, _, _ = _make_mm(lax.Precision.HIGH)


@jax.custom_vjp
def _swap_halves(x):
    return pltpu.roll(x, 64, 1)


def _swap_f(x):
    return pltpu.roll(x, 64, 1), None


def _swap_b(_, g):
    return (pltpu.roll(g, 64, 1),)


_swap_halves.defvjp(_swap_f, _swap_b)


def _silu(x):
    return x * jax.nn.sigmoid(x)


def _rms(x, g):
    return x * lax.rsqrt(jnp.mean(x * x, axis=-1, keepdims=True) + EPS) * g


def _iota(shape, dim):
    return lax.broadcasted_iota(jnp.int32, shape, dim)


def _matmul(a, b, mode="nn", res=None, name="mm", a_cols=None, out_dtype=F32):
    a_off, a_w = (0, a.shape[1]) if a_cols is None else a_cols
    if mode == "nn":
        (M, K), (K2, N) = (a.shape[0], a_w), b.shape
    elif mode == "nt":
        (M, K), (N, K2) = (a.shape[0], a_w), b.shape
    else:
        (K, M), (K2, N) = (a.shape[0], a_w), b.shape
    assert K == K2, (a.shape, b.shape, mode)
    tm = _pick(M, (512, 1408, 256, 128))
    tn = _pick(N, (1024, 1408, 512, 256, 128))
    tk = _pick(K, (1024, 1408, 512, 256, 128))
    nk = K // tk
    dims = {"nn": _NN, "nt": _NT, "tn": _TN}[mode]
    ao = a_off // (tm if mode == "tn" else tk)
    assert ao * (tm if mode == "tn" else tk) == a_off
    if mode == "nn":
        a_spec = pl.BlockSpec((tm, tk), lambda i, j, k: (i, k + ao))
        b_spec = pl.BlockSpec((tk, tn), lambda i, j, k: (k, j))
    elif mode == "nt":
        a_spec = pl.BlockSpec((tm, tk), lambda i, j, k: (i, k + ao))
        b_spec = pl.BlockSpec((tn, tk), lambda i, j, k: (j, k))
    else:
        a_spec = pl.BlockSpec((tk, tm), lambda i, j, k: (k, i + ao))
        b_spec = pl.BlockSpec((tk, tn), lambda i, j, k: (k, j))
    o_spec = pl.BlockSpec((tm, tn), lambda i, j, k: (i, j))
    has_res = res is not None

    def kern(*refs):
        if has_res:
            a_ref, b_ref, r_ref, o_ref, acc_ref = refs
        else:
            a_ref, b_ref, o_ref, acc_ref = refs
        k = pl.program_id(2)
        part = lax.dot_general(a_ref[...].astype(BF16), b_ref[...].astype(BF16), (dims, ((), ())),
                               preferred_element_type=F32)
        if nk == 1:
            o_ref[...] = (part + r_ref[...] if has_res else part).astype(o_ref.dtype)
            return

        @pl.when(k == 0)
        def _():
            acc_ref[...] = part

        @pl.when((k > 0) & (k < nk - 1))
        def _():
            acc_ref[...] += part

        @pl.when(k == nk - 1)
        def _():
            total = acc_ref[...] + part
            o_ref[...] = (total + r_ref[...] if has_res else total).astype(o_ref.dtype)

    in_specs = [a_spec, b_spec] + ([o_spec] if has_res else [])
    ops = (a, b) + ((res,) if has_res else ())
    return pl.pallas_call(
        kern, name=name, grid=(M // tm, N // tn, nk), in_specs=in_specs, out_specs=o_spec,
        out_shape=jax.ShapeDtypeStruct((M, N), out_dtype), scratch_shapes=[pltpu.VMEM((tm, tn), F32)],
        compiler_params=_cparams(("parallel", "parallel", "arbitrary")))(*ops)


def _matmul_cat(pieces, b, mode="nn", res=None, name="mmcat"):
    M = pieces[0].shape[0]
    widths = [p.shape[1] for p in pieces]
    K = sum(widths)
    N = b.shape[1] if mode == "nn" else b.shape[0]
    assert (b.shape[0] if mode == "nn" else b.shape[1]) == K
    tm = _pick(M, (256, 128))
    tn = _pick(N, (1024, 512, 256, 128))
    npc = len(pieces)
    has_res = res is not None
    dims = _NN if mode == "nn" else _NT

    def kern(*refs):
        b_ref = refs[npc]
        o_ref = refs[-1]
        acc = refs[npc + 1][...] if has_res else None
        off = 0
        for p in range(npc):
            bp = b_ref[off:off + widths[p], :] if mode == "nn" else b_ref[:, off:off + widths[p]]
            t = lax.dot_general(refs[p][...].astype(BF16), bp.astype(BF16), (dims, ((), ())),
                                preferred_element_type=F32)
            acc = t if acc is None else acc + t
            off += widths[p]
        o_ref[...] = acc

    in_specs = [pl.BlockSpec((tm, w), lambda j, i: (i, 0)) for w in widths]
    in_specs.append(pl.BlockSpec((K, tn), lambda j, i: (0, j)) if mode == "nn"
                    else pl.BlockSpec((tn, K), lambda j, i: (j, 0)))
    o_spec = pl.BlockSpec((tm, tn), lambda j, i: (i, j))
    if has_res:
        in_specs.append(o_spec)
    ops = list(pieces) + [b] + ([res] if has_res else [])
    return pl.pallas_call(
        kern, name=name, grid=(N // tn, M // tm), in_specs=in_specs, out_specs=o_spec,
        out_shape=jax.ShapeDtypeStruct((M, N), F32), compiler_params=_cparams(("parallel", "parallel")))(*ops)


def _blk(a, width=None, colblk=0):
    return (a, a.shape[1] if width is None else width, colblk)


def _row_specs(blocked, params, ts):
    specs = []
    for (_, w, cb) in blocked:
        specs.append(pl.BlockSpec((ts, w), functools.partial(lambda i, cb: (i, cb), cb=cb)))
    for p in params:
        specs.append(pl.BlockSpec(p.shape, lambda i: (0, 0)))
    return specs


def _rowwise(fn, blocked, params, out_widths, name, ts=256, out_dtypes=None):
    S = blocked[0][0].shape[0]
    ts = min(ts, S)
    nb, npar = len(blocked), len(params)
    out_dtypes = [F32] * len(out_widths) if out_dtypes is None else out_dtypes

    def kern(*refs):
        vals = [r[...] for r in refs[:nb + npar]]
        outs = fn(*vals)
        for o_ref, o in zip(refs[nb + npar:], outs):
            o_ref[...] = o.astype(o_ref.dtype)

    return pl.pallas_call(
        kern, name=name, grid=(S // ts,), in_specs=_row_specs(blocked, params, ts),
        out_specs=[pl.BlockSpec((ts, w), lambda i: (i, 0)) for w in out_widths],
        out_shape=[jax.ShapeDtypeStruct((S, w), d) for w, d in zip(out_widths, out_dtypes)],
        compiler_params=_cparams(("parallel",)))(*[b[0] for b in blocked], *params)


def _rowwise_bwd(fn, blocked, params, cots, name, blocked_grad=None, param_grad=None, adds=None, ts=256,
                 out_dtypes=None):
    S = blocked[0][0].shape[0]
    ts = min(ts, S)
    cots = [c if isinstance(c, tuple) else _blk(c) for c in cots]
    nb, npar, nc = len(blocked), len(params), len(cots)
    blocked_grad = [True] * nb if blocked_grad is None else blocked_grad
    param_grad = [True] * npar if param_grad is None else param_grad
    adds = {} if adds is None else adds
    bidx = [i for i in range(nb) if blocked_grad[i]]
    pidx = [i for i in range(npar) if param_grad[i]]
    add_keys = sorted(adds)
    n_in = nb + npar + nc + len(add_keys)

    def kern(*refs):
        i = pl.program_id(0)
        xs = [r[...] for r in refs[:nb]]
        ps = [r[...] for r in refs[nb:nb + npar]]
        gs = [r[...] for r in refs[nb + npar:nb + npar + nc]]
        add_vals = {k: refs[nb + npar + nc + n][...] for n, k in enumerate(add_keys)}
        outs = refs[n_in:]

        def f(*diff):
            full_x = list(xs)
            full_p = list(ps)
            for n, ix in enumerate(bidx):
                full_x[ix] = diff[n]
            for n, ix in enumerate(pidx):
                full_p[ix] = diff[len(bidx) + n]
            return tuple(fn(*full_x, *full_p))

        _, vjp = jax.vjp(f, *[xs[ix] for ix in bidx], *[ps[ix] for ix in pidx])
        grads = vjp(tuple(gs))
        for n, ix in enumerate(bidx):
            g = grads[n]
            if ix in add_vals:
                g = g + add_vals[ix]
            outs[n][...] = g.astype(outs[n].dtype)
        for n in range(len(pidx)):
            o_ref = outs[len(bidx) + n]

            @pl.when(i == 0)
            def _(o_ref=o_ref):
                o_ref[...] = jnp.zeros_like(o_ref)

            o_ref[...] += grads[len(bidx) + n]

    in_specs = _row_specs(blocked, params, ts)
    in_specs += _row_specs(cots, [], ts)
    in_specs += [pl.BlockSpec((ts, adds[k].shape[1]), lambda i: (i, 0)) for k in add_keys]
    out_specs = [pl.BlockSpec((ts, blocked[ix][1]), lambda i: (i, 0)) for ix in bidx]
    out_specs += [pl.BlockSpec(params[ix].shape, lambda i: (0, 0)) for ix in pidx]
    out_dtypes = [F32] * len(bidx) if out_dtypes is None else out_dtypes
    out_shape = [jax.ShapeDtypeStruct((S, blocked[ix][1]), d) for ix, d in zip(bidx, out_dtypes)]
    out_shape += [jax.ShapeDtypeStruct(params[ix].shape, F32) for ix in pidx]
    return pl.pallas_call(
        kern, name=name, grid=(S // ts,), in_specs=in_specs, out_specs=out_specs, out_shape=out_shape,
        compiler_params=_cparams(("arbitrary",)))(*[b[0] for b in blocked], *params, *[c[0] for c in cots],
                                                    *[adds[k] for k in add_keys])


def _rms_fn(x, g):
    return (_rms(x, g),)


def _head_norm(o, n_heads, dh):
    outs = []
    for h in range(n_heads):
        oh = o[:, h * dh:(h + 1) * dh]
        outs.append(oh * lax.rsqrt(jnp.mean(oh * oh, axis=-1, keepdims=True) + EPS))
    return outs


def _ret_post_fn(o_raw, gate, ret_norm):
    o = jnp.concatenate(_head_norm(o_raw, 4, 128), axis=1)
    return (o * ret_norm * _silu(gate),)


def _s5_post_fn(y1, y2, u, d, w_glu, b_glu):
    y = y1 - y2 + d * u
    y = jax.nn.gelu(y)
    return (y * jax.nn.sigmoid(mm(y, w_glu) + b_glu),)


def _xattn_fn(q, kv):
    outs = []
    for h in range(4):
        qh = q[:, h * 256:(h + 1) * 256]
        kh = kv[:, h * 256:(h + 1) * 256]
        vh = kv[:, 1024 + h * 256:1024 + (h + 1) * 256]
        s = mm_nt(qh, kh) * (256 ** -0.5)
        s = s - lax.stop_gradient(jnp.max(s, axis=-1, keepdims=True))
        p = jnp.exp(s)
        p = p / jnp.sum(p, axis=-1, keepdims=True)
        outs.append(mm(p, vh))
    return (jnp.concatenate(outs, axis=1),)


def _softplus(x):
    return jnp.maximum(x, 0.0) + jnp.log1p(jnp.exp(-jnp.abs(x)))


def _gdn_gates_fn(pt, a_log_p, dtb_p):
    rows, cols = _iota((128, 1024), 0), _iota((128, 1024), 1)
    e_b = (rows == (cols >> 7)).astype(F32)
    e_a = (rows == (cols >> 7) + 8).astype(F32)
    beta = jax.nn.sigmoid(pt)
    g = -(jnp.exp(a_log_p) * _softplus(pt + dtb_p))
    return mmh(g, e_a), mmh(beta, e_b)


def _gdn_post_fn(o_raw, z, o_norm):
    outs = _head_norm(o_raw, 8, 128)
    o = jnp.concatenate([oh * o_norm for oh in outs], axis=1)
    return (o * _silu(z),)


def _ffn_post(up, gate):
    return _silu(gate) * up


def _shift_down(cur, prev8, sh, row8):
    if sh == 0:
        return cur
    r = pltpu.roll(cur, sh, 0)
    p = pltpu.roll(prev8, sh, 0)
    top = jnp.where(row8 < sh, p, r[0:8])
    if cur.shape[0] == 8:
        return top
    return jnp.concatenate([top, r[8:]], axis=0)


def _shift_up(cur, next8, sh, row8):
    if sh == 0:
        return cur
    ts = cur.shape[0]
    r = pltpu.roll(cur, ts - sh, 0)
    p = pltpu.roll(next8, 8 - sh, 0)
    bot = jnp.where(row8 >= 8 - sh, p, r[ts - 8:])
    return jnp.concatenate([r[:ts - 8], bot], axis=0)


def _conv_rows(cur, prev8, wrows, row8):
    k_w = len(wrows)
    out = None
    for j in range(k_w):
        t = _shift_down(cur, prev8, k_w - 1 - j, row8) * wrows[j]
        out = t if out is None else out + t
    return out


def _conv_specs(x, xoff, w, woff, ts, tc):
    r8 = ts // 8
    return [pl.BlockSpec((ts, tc), functools.partial(lambda i, j, o: (i, j + o), o=xoff)),
            pl.BlockSpec((8, tc), functools.partial(lambda i, j, o: (jnp.maximum(i * r8 - 1, 0), j + o), o=xoff)),
            pl.BlockSpec((w.shape[0], tc), functools.partial(lambda i, j, o: (0, j + o), o=woff))]


def _conv_post(srcs, post, ncol, tc, name, cots=None, ts=256, out_dtype=F32):
    S = srcs[0][0].shape[0]
    ns = len(srcs)
    bwd = cots is not None

    def kern(*refs):
        first = pl.program_id(0) == 0
        row8 = _iota((8, tc), 0)
        cs = []
        for s in range(ns):
            cur_ref, prev_ref, w_ref = refs[3 * s:3 * s + 3]
            prev = jnp.where(first, 0.0, prev_ref[...])
            wrows = [w_ref[j:j + 1, :] for j in range(w_ref.shape[0])]
            cs.append(_conv_rows(cur_ref[...], prev, wrows, row8))
        if bwd:
            g = refs[3 * ns][...]
            _, vjp = jax.vjp(lambda *c: post(*c), *cs)
            for o_ref, d in zip(refs[3 * ns + 1:], vjp(g)):
                o_ref[...] = d
        else:
            refs[3 * ns][...] = post(*cs).astype(refs[3 * ns].dtype)

    in_specs = []
    ops = []
    for (x, xoff, w, woff) in srcs:
        in_specs += _conv_specs(x, xoff, w, woff, ts, tc)
        ops += [x, x, w]
    o_spec = pl.BlockSpec((ts, tc), lambda i, j: (i, j))
    o_shape = jax.ShapeDtypeStruct((S, ncol * tc), F32)
    if bwd:
        in_specs.append(o_spec)
        ops.append(cots)
        out_specs, out_shape = [o_spec] * ns, [o_shape] * ns
    else:
        out_specs, out_shape = o_spec, jax.ShapeDtypeStruct((S, ncol * tc), out_dtype)
    return pl.pallas_call(
        kern, name=name, grid=(S // ts, ncol), in_specs=in_specs, out_specs=out_specs, out_shape=out_shape,
        compiler_params=_cparams(("parallel", "parallel")))(*ops)


def _conv_bwd(dc, x, xoff, w, woff, ncol, tc, name, ts=256):
    S = x.shape[0]
    k_w = w.shape[0]
    r8 = ts // 8
    nblk8 = S // 8
    nrow = S // ts

    def kern(dc_ref, dn_ref, x_ref, xp_ref, w_ref, dx_ref, dw_ref):
        i = pl.program_id(1)
        row8 = _iota((8, tc), 0)
        dcur = dc_ref[...]
        dnext = jnp.where(i == nrow - 1, 0.0, dn_ref[...])
        xcur = x_ref[...]
        xprev = jnp.where(i == 0, 0.0, xp_ref[...])

        @pl.when(i == 0)
        def _():
            dw_ref[...] = jnp.zeros_like(dw_ref)

        dx = None
        for j in range(k_w):
            sh = k_w - 1 - j
            wj = w_ref[j:j + 1, :]
            t = _shift_up(dcur, dnext, sh, row8) * wj
            dx = t if dx is None else dx + t
            dw_ref[j:j + 1, :] += jnp.sum(dcur * _shift_down(xcur, xprev, sh, row8), axis=0, keepdims=True)
        dx_ref[...] = dx.astype(dx_ref.dtype)

    in_specs = [pl.BlockSpec((ts, tc), lambda j, i: (i, j)),
                pl.BlockSpec((8, tc), lambda j, i: (jnp.minimum((i + 1) * r8, nblk8 - 1), j)),
                pl.BlockSpec((ts, tc), functools.partial(lambda j, i, o: (i, j + o), o=xoff)),
                pl.BlockSpec((8, tc), functools.partial(lambda j, i, o: (jnp.maximum(i * r8 - 1, 0), j + o), o=xoff)),
                pl.BlockSpec((k_w, tc), functools.partial(lambda j, i, o: (0, j + o), o=woff))]
    out_specs = [pl.BlockSpec((ts, tc), lambda j, i: (i, j)), pl.BlockSpec((k_w, tc), lambda j, i: (0, j))]
    out_shape = [jax.ShapeDtypeStruct((S, ncol * tc), BF16), jax.ShapeDtypeStruct((k_w, ncol * tc), F32)]
    return pl.pallas_call(
        kern, name=name, grid=(ncol, nrow), in_specs=in_specs, out_specs=out_specs, out_shape=out_shape,
        compiler_params=_cparams(("parallel", "arbitrary")))(dc, dc, x, x, w)


def _conv_post_bwd(srcs, post, ncol, tc, cot, name, ts=256):
    S = srcs[0][0].shape[0]
    ns = len(srcs)
    r8 = ts // 8
    nblk8 = S // 8
    nrow = S // ts

    def kern(*refs):
        i = pl.program_id(1)
        row8 = _iota((8, tc), 0)
        g_ref, gn_ref = refs[4 * ns:4 * ns + 2]
        outs = refs[4 * ns + 2:]
        xs, xps, ws, cs, cns = [], [], [], [], []
        for s in range(ns):
            cur_ref, prev_ref, next_ref, w_ref = refs[4 * s:4 * s + 4]
            xcur = cur_ref[...]
            xprev = jnp.where(i == 0, 0.0, prev_ref[...])
            wrows = [w_ref[j:j + 1, :] for j in range(w_ref.shape[0])]
            xs.append(xcur)
            xps.append(xprev)
            ws.append(wrows)
            cs.append(_conv_rows(xcur, xprev, wrows, row8))
            cns.append(_conv_rows(next_ref[...], xcur[ts - 8:], wrows, row8))
        _, vjp = jax.vjp(lambda *c: post(*c), *cs)
        dcs = vjp(g_ref[...])
        _, vjp_next = jax.vjp(lambda *c: post(*c), *cns)
        dcns = vjp_next(jnp.where(i == nrow - 1, 0.0, gn_ref[...]))
        for s in range(ns):
            dx_ref, dw_ref = outs[2 * s], outs[2 * s + 1]

            @pl.when(i == 0)
            def _(dw_ref=dw_ref):
                dw_ref[...] = jnp.zeros_like(dw_ref)

            k_w = len(ws[s])
            dx = None
            for j in range(k_w):
                sh = k_w - 1 - j
                t = _shift_up(dcs[s], dcns[s], sh, row8) * ws[s][j]
                dx = t if dx is None else dx + t
                dw_ref[j:j + 1, :] += jnp.sum(dcs[s] * _shift_down(xs[s], xps[s], sh, row8), axis=0, keepdims=True)
            dx_ref[...] = dx.astype(dx_ref.dtype)

    def nxt(i):
        return jnp.minimum((i + 1) * r8, nblk8 - 1)

    in_specs, ops = [], []
    for (x, xoff, w, woff) in srcs:
        in_specs += [pl.BlockSpec((ts, tc), functools.partial(lambda j, i, o: (i, j + o), o=xoff)),
                     pl.BlockSpec((8, tc), functools.partial(lambda j, i, o: (jnp.maximum(i * r8 - 1, 0), j + o),
                                                             o=xoff)),
                     pl.BlockSpec((8, tc), functools.partial(lambda j, i, o: (nxt(i), j + o), o=xoff)),
                     pl.BlockSpec((w.shape[0], tc), functools.partial(lambda j, i, o: (0, j + o), o=woff))]
        ops += [x, x, x, w]
    in_specs += [pl.BlockSpec((ts, tc), lambda j, i: (i, j)), pl.BlockSpec((8, tc), lambda j, i: (nxt(i), j))]
    ops += [cot, cot]
    out_specs, out_shape = [], []
    for (x, xoff, w, woff) in srcs:
        out_specs += [pl.BlockSpec((ts, tc), lambda j, i: (i, j)), pl.BlockSpec((w.shape[0], tc), lambda j, i: (0, j))]
        out_shape += [jax.ShapeDtypeStruct((S, ncol * tc), BF16), jax.ShapeDtypeStruct((w.shape[0], ncol * tc), F32)]
    return pl.pallas_call(
        kern, name=name, grid=(ncol, nrow), in_specs=in_specs, out_specs=out_specs, out_shape=out_shape,
        compiler_params=_cparams(("parallel", "arbitrary")))(*ops)


def _ret_tables(S):
    H, C, dh = 4, 128, 128
    lg = jnp.log1p(-jnp.exp2(-5.0 - jnp.arange(H, dtype=F32)))
    idx = jnp.arange(C, dtype=F32)
    diff = idx[:, None] - idx[None, :]
    causal = diff >= 0
    intra = jnp.where(causal, jnp.exp(lg[:, None, None] * jnp.where(causal, diff, 0.0)), 0.0)
    kdec = jnp.broadcast_to(jnp.exp(lg[:, None] * (C - 1 - idx))[:, :, None], (H, C, dh))
    qdec = jnp.broadcast_to(jnp.exp(lg[:, None] * (idx + 1))[:, :, None], (H, C, dh))
    cdec = jnp.broadcast_to(jnp.exp(lg * C)[:, None, None], (H, dh, dh))
    half = dh // 2
    inv = jnp.exp(-math.log(10000.0) * jnp.arange(half, dtype=F32) / half)
    ang = jnp.arange(S).astype(F32)[:, None] * inv[None, :]
    cos, sin = jnp.cos(ang), jnp.sin(ang)
    cosf = jnp.concatenate([cos, cos], axis=1)
    sinf = jnp.concatenate([-sin, sin], axis=1)
    return cosf, sinf, intra, kdec, qdec, cdec


def _ret_chunk(q, k, v, cosf, sinf, intra, kdec, qdec, cdec, state):
    hs = range(len(q))
    qr = [q[h] * cosf + _swap_halves(q[h]) * sinf for h in hs]
    kr = [(k[h] * cosf + _swap_halves(k[h]) * sinf) * (128 ** -0.5) for h in hs]
    scores = [mm_nt(qr[h], kr[h]) * intra[h] for h in hs]
    inner = [mm(scores[h], v[h]) for h in hs]
    kv = [mm_tn(kr[h] * kdec[h], v[h]) for h in hs]
    cross = [mm(qr[h] * qdec[h], state[h]) for h in hs]
    return [inner[h] + cross[h] for h in hs], [state[h] * cdec[h] + kv[h] for h in hs]


RET_H = 4


def _ret_call(proj, tabs, states=None, do=None):
    S = proj.shape[0]
    N = S // 128
    bwd = do is not None

    def nn(n):
        return N - 1 - n if bwd else n

    qkv_spec = pl.BlockSpec((128, 3 * 512), lambda n: (nn(n), 0))
    pos = pl.BlockSpec((128, 128), lambda n: (nn(n), 0))
    tab = pl.BlockSpec((RET_H, 128, 128), lambda n: (0, 0, 0))
    st_spec = pl.BlockSpec((None, RET_H, 128, 128), lambda n: (nn(n), 0, 0, 0))
    o_spec = pl.BlockSpec((128, 512), lambda n: (nn(n), 0))

    def kern(*refs):
        x_ref, c_ref, s_ref, i_ref, kd_ref, qd_ref, cd_ref = refs[:7]
        carry = refs[-1]
        heads = range(RET_H)

        @pl.when(pl.program_id(0) == 0)
        def _():
            carry[...] = jnp.zeros_like(carry)

        def cols(ref, off=0):
            return [ref[:, _hs(off + h)] for h in heads]

        def tabs_of(ref):
            return [ref[h] for h in heads]

        consts = (c_ref[...], s_ref[...], tabs_of(i_ref), tabs_of(kd_ref), tabs_of(qd_ref), tabs_of(cd_ref))
        qkv = (cols(x_ref), cols(x_ref, RET_H), cols(x_ref, 2 * RET_H))
        if bwd:
            sp_ref, do_ref = refs[7:9]
            outs = refs[9:12]
            _, vjp = jax.vjp(lambda q, k, v, s: _ret_chunk(q, k, v, *consts, s), *qkv, tabs_of(sp_ref))
            dq, dk, dv, ds = vjp((cols(do_ref), tabs_of(carry)))
            for h in heads:
                for o_ref, d in zip(outs, (dq[h], dk[h], dv[h])):
                    o_ref[:, _hs(h)] = d.astype(o_ref.dtype)
                carry[h] = ds[h]
        else:
            o_ref, sp_ref = refs[7:9]
            state = tabs_of(carry)
            out, new = _ret_chunk(*qkv, *consts, state)
            for h in heads:
                sp_ref[h] = state[h]
                o_ref[:, _hs(h)] = out[h]
                carry[h] = new[h]

    in_specs = [qkv_spec, pos, pos, tab, tab, tab, tab]
    if bwd:
        in_specs += [st_spec, o_spec]
        out_specs = [o_spec] * 3
        out_shape = [jax.ShapeDtypeStruct((S, 512), BF16)] * 3
        ops = (proj, *tabs, states, do)
    else:
        out_specs = [o_spec, st_spec]
        out_shape = [jax.ShapeDtypeStruct((S, 512), F32), jax.ShapeDtypeStruct((N, RET_H, 128, 128), F32)]
        ops = (proj, *tabs)
    return pl.pallas_call(
        kern, name="ret_bwd" if bwd else "ret_fwd", grid=(N,), in_specs=in_specs, out_specs=out_specs,
        out_shape=out_shape, scratch_shapes=[pltpu.VMEM((RET_H, 128, 128), F32)],
        compiler_params=_cparams(("arbitrary",)))(*ops)


GDN_C = 64
GDN_H = 8


def _unit_lower_inverse(a_mats, eye):
    p = [-a for a in a_mats]
    t = [eye + x for x in p]
    for _ in range(5):
        p = [mm3(x, x) for x in p]
        t = [mm3(y, eye + x) for y, x in zip(t, p)]
    return t


@jax.custom_vjp
def _known_inverse(a_mat, t_mat):
    return t_mat


def _known_inverse_f(a_mat, t_mat):
    return t_mat, t_mat


def _known_inverse_b(t_mat, g):
    return -mmh_tn(t_mat, mmh_nt(g, t_mat)), jnp.zeros_like(t_mat)


_known_inverse.defvjp(_known_inverse_f, _known_inverse_b)


def _gdn_intra(q, k, v, g_b, beta_b, t_known=None):
    c = GDN_C
    hs = range(len(q))
    q = [x * lax.rsqrt(jnp.sum(x * x, axis=-1, keepdims=True) + EPS) * (128 ** -0.5) for x in q]
    k = [x * lax.rsqrt(jnp.sum(x * x, axis=-1, keepdims=True) + EPS) for x in k]
    ri, ci = _iota((c, c), 0), _iota((c, c), 1)
    incl = ri >= ci
    strict = ri > ci
    eye = (ri == ci).astype(F32)
    lower = incl.astype(F32)
    ones = jnp.ones((c, c), F32)
    gc_b = [mmh(lower, g) for g in g_b]
    gl_b = [mmh(ones, g) for g in g_b]
    kb = [k[h] * beta_b[h] for h in hs]
    vb = [v[h] * beta_b[h] for h in hs]
    gcc = [g[:, :c] for g in gc_b]
    decay = [jnp.where(incl, jnp.exp(jnp.where(incl, g - g.T, 0.0)), 0.0) for g in gcc]
    a_mat = [jnp.where(strict, mm_nt(kb[h], k[h]) * decay[h], 0.0) for h in hs]
    if t_known is None:
        t_mat = _unit_lower_inverse(a_mat, eye)
    else:
        t_mat = [_known_inverse(a_mat[h], t_known[h]) for h in hs]
    egc = [jnp.exp(g) for g in gc_b]
    w = [mm(t_mat[h], kb[h] * egc[h]) for h in hs]
    u = [mm(t_mat[h], vb[h]) for h in hs]
    qk = [jnp.where(incl, mm_nt(q[h], k[h]) * decay[h], 0.0) for h in hs]
    q_dec = [q[h] * egc[h] for h in hs]
    k_dec = [k[h] * jnp.exp(gl_b[h] - gc_b[h]) for h in hs]
    return w, u, q_dec, k_dec, qk, t_mat


def _gdn_step(w, u, q_dec, k_dec, qk, g_b, state):
    hs = range(len(w))
    ones = jnp.ones((128, GDN_C), F32)
    gl_s = [mmh(ones, g) for g in g_b]
    ws = [mm(w[h], state[h]) for h in hs]
    qs = [mm(q_dec[h], state[h]) for h in hs]
    v_new = [u[h] - ws[h] for h in hs]
    o = [qs[h] + mm(qk[h], v_new[h]) for h in hs]
    new = [state[h] * jnp.exp(gl_s[h]) + mm_tn(k_dec[h], v_new[h]) for h in hs]
    return o, new


def _hs(h):
    return slice(h * 128, (h + 1) * 128)


def _gdn_intra_call(qkv, g_e, beta_e, cots=None):
    S = qkv.shape[0]
    N = S // GDN_C
    bwd = cots is not None
    row = pl.BlockSpec((GDN_C, 1024), lambda n: (n, 0))
    qkv_spec = pl.BlockSpec((GDN_C, 3072), lambda n: (n, 0))
    qk_spec = pl.BlockSpec((GDN_H, GDN_C, GDN_C), lambda n: (0, n, 0))

    def kern(*refs):
        x_ref, g_ref, b_ref = refs[:3]
        heads = range(GDN_H)

        def cols(ref, off=0):
            return [ref[:, _hs(off + h)] for h in heads]

        args = (cols(x_ref), cols(x_ref, 8), cols(x_ref, 16), cols(g_ref), cols(b_ref))
        if bwd:
            dw_ref, du_ref, dqd_ref, dkd_ref, dqk_ref, dgadd_ref, t_ref = refs[3:10]
            outs = refs[10:]
            t_known = [t_ref[h] for h in heads]
            _, vjp = jax.vjp(lambda *a: _gdn_intra(*a, t_known=t_known)[:5], *args)
            dq, dk, dv, dg, db = vjp((cols(dw_ref), cols(du_ref), cols(dqd_ref), cols(dkd_ref),
                                      [dqk_ref[h] for h in heads]))
            dgadd = cols(dgadd_ref)
            for h in heads:
                for o_ref, d in zip(outs, (dq[h], dk[h], dv[h], dg[h] + dgadd[h], db[h])):
                    o_ref[:, _hs(h)] = d
        else:
            w, u, qd, kd, qk, t_mat = _gdn_intra(*args)
            for h in heads:
                for o_ref, o in zip(refs[3:7], (w[h], u[h], qd[h], kd[h])):
                    o_ref[:, _hs(h)] = o
                refs[7][h] = qk[h]
                refs[8][h] = t_mat[h]

    big = jax.ShapeDtypeStruct((S, 1024), F32)
    sq = jax.ShapeDtypeStruct((GDN_H, S, GDN_C), F32)
    if bwd:
        in_specs = [qkv_spec, row, row, row, row, row, row, qk_spec, row, qk_spec]
        out_specs, out_shape = [row] * 5, [big] * 5
        ops = (qkv, g_e, beta_e) + tuple(cots)
    else:
        in_specs = [qkv_spec, row, row]
        out_specs = [row] * 4 + [qk_spec, qk_spec]
        out_shape = [big] * 4 + [sq, sq]
        ops = (qkv, g_e, beta_e)
    return pl.pallas_call(
        kern, name="gdn_intra_bwd" if bwd else "gdn_intra", grid=(N,), in_specs=in_specs, out_specs=out_specs,
        out_shape=out_shape, compiler_params=_cparams(("parallel",)))(*ops)


def _gdn_pass(w, u, qd, kd, qk, g_e, states=None, do=None):
    S = w.shape[0]
    N = S // GDN_C
    bwd = do is not None

    def nn(n):
        return N - 1 - n if bwd else n

    row = pl.BlockSpec((GDN_C, 1024), lambda n: (nn(n), 0))
    qk_spec = pl.BlockSpec((GDN_H, GDN_C, GDN_C), lambda n: (0, nn(n), 0))
    st_spec = pl.BlockSpec((None, GDN_H, 128, 128), lambda n: (nn(n), 0, 0, 0))

    def kern(*refs):
        w_ref, u_ref, qd_ref, kd_ref, qk_ref, g_ref = refs[:6]
        carry = refs[-1]

        @pl.when(pl.program_id(0) == 0)
        def _():
            carry[...] = jnp.zeros_like(carry)

        heads = range(GDN_H)

        def cols(ref):
            return [ref[:, _hs(h)] for h in heads]

        args = (cols(w_ref), cols(u_ref), cols(qd_ref), cols(kd_ref), [qk_ref[h] for h in heads], cols(g_ref))
        if bwd:
            sp_ref, do_ref = refs[6:8]
            outs = refs[8:14]
            _, vjp = jax.vjp(_gdn_step, *args, [sp_ref[h] for h in heads])
            dw, du, dqd, dkd, dqk, dg, ds = vjp((cols(do_ref), [carry[h] for h in heads]))
            for h in heads:
                for o_ref, d in zip(outs[:4], (dw[h], du[h], dqd[h], dkd[h])):
                    o_ref[:, _hs(h)] = d
                outs[4][h] = dqk[h]
                outs[5][:, _hs(h)] = dg[h]
                carry[h] = ds[h]
        else:
            o_ref, sp_ref = refs[6:8]
            state = [carry[h] for h in heads]
            o, new = _gdn_step(*args, state)
            for h in heads:
                sp_ref[h] = state[h]
                o_ref[:, _hs(h)] = o[h]
                carry[h] = new[h]

    big = jax.ShapeDtypeStruct((S, 1024), F32)
    in_specs = [row, row, row, row, qk_spec, row]
    if bwd:
        in_specs += [st_spec, row]
        out_specs = [row] * 4 + [qk_spec, row]
        out_shape = [big] * 4 + [jax.ShapeDtypeStruct((GDN_H, S, GDN_C), F32), big]
        ops = (w, u, qd, kd, qk, g_e, states, do)
    else:
        out_specs = [row, st_spec]
        out_shape = [big, jax.ShapeDtypeStruct((N, GDN_H, 128, 128), F32)]
        ops = (w, u, qd, kd, qk, g_e)
    return pl.pallas_call(
        kern, name="gdn_pass_bwd" if bwd else "gdn_pass", grid=(N,), in_specs=in_specs, out_specs=out_specs,
        out_shape=out_shape, scratch_shapes=[pltpu.VMEM((GDN_H, 128, 128), F32)],
        compiler_params=_cparams(("arbitrary",)))(*ops)


def _s5_prep_fn(lr, li, ldt, br, bi, cr, ci):
    dt = jnp.exp(ldt)
    mag = jnp.exp(lr * dt)
    a_re = mag * jnp.cos(li * dt)
    a_im = mag * jnp.sin(li * dt)
    den = lr * lr + li * li
    z_re = ((a_re - 1.0) * lr + a_im * li) / den
    z_im = (a_im * lr - (a_re - 1.0) * li) / den
    e1 = ((_iota((512, 32), 0) >> 4) == _iota((512, 32), 1)).astype(F32)
    zr_e = mmh(e1, z_re)
    zi_e = mmh(e1, z_im)
    bb_re = zr_e * br - zi_e * bi
    bb_im = zr_e * bi + zi_e * br
    t1 = ((_iota((64, 2048), 1) & 63) == _iota((64, 2048), 0)).astype(F32)
    m1 = (_iota((512, 2048), 0) >> 4) == (_iota((512, 2048), 1) >> 6)
    bd_re = jnp.where(m1, mmh(bb_re, t1), 0.0)
    bd_im = jnp.where(m1, mmh(bb_im, t1), 0.0)
    t2 = ((_iota((16, 512), 1) & 15) == _iota((16, 512), 0)).astype(F32)
    m2 = (_iota((2048, 512), 0) >> 6) == (_iota((2048, 512), 1) >> 4)
    cd_re = jnp.where(m2, mmh(cr, t2), 0.0)
    cd_im = jnp.where(m2, mmh(ci, t2), 0.0)
    return a_re, a_im, bd_re, bd_im, cd_re, cd_im


_PREP_OUT = [(32, 64), (32, 64), (512, 2048), (512, 2048), (2048, 512), (2048, 512)]


def _s5_prep(params, cots=None):
    bwd = cots is not None

    def kern(*refs):
        vals = [r[...] for r in refs[:7]]
        if bwd:
            gs = tuple(r[...] for r in refs[7:13])
            _, vjp = jax.vjp(_s5_prep_fn, *vals)
            for o_ref, d in zip(refs[13:], vjp(gs)):
                o_ref[...] = d
        else:
            for o_ref, o in zip(refs[7:], _s5_prep_fn(*vals)):
                o_ref[...] = o

    if bwd:
        out_shape = [jax.ShapeDtypeStruct(p.shape, F32) for p in params]
        ops = list(params) + list(cots)
    else:
        out_shape = [jax.ShapeDtypeStruct(s, F32) for s in _PREP_OUT]
        ops = list(params)
    return pl.pallas_call(kern, name="s5_prep_bwd" if bwd else "s5_prep", out_shape=out_shape,
                          compiler_params=_cparams())(*ops)


def _cmul(ar, ai, br, bi):
    return ar * br - ai * bi, ar * bi + ai * br


def _power_table(ar, ai, row8, descending):
    pr, pi = ar, ai
    tr = jnp.zeros(row8.shape, F32)
    ti = jnp.zeros(row8.shape, F32)
    for n in range(8):
        r = 7 - n if descending else n
        tr = jnp.where(row8 == r, pr, tr)
        ti = jnp.where(row8 == r, pi, ti)
        if n < 7:
            pr, pi = _cmul(pr, pi, ar, ai)
    return tr, ti


def _tile_scan(xr, xi, pows, row8, up):
    for d, (pr, pi) in zip((1, 2, 4), pows):
        if up:
            sr = jnp.where(row8 < 8 - d, pltpu.roll(xr, 8 - d, 0), 0.0)
            si = jnp.where(row8 < 8 - d, pltpu.roll(xi, 8 - d, 0), 0.0)
        else:
            sr = jnp.where(row8 >= d, pltpu.roll(xr, d, 0), 0.0)
            si = jnp.where(row8 >= d, pltpu.roll(xi, d, 0), 0.0)
        mr, mi = _cmul(pr, pi, sr, si)
        xr, xi = xr + mr, xi + mi
    return xr, xi


def _pick_row(x, row8, r):
    return jnp.sum(jnp.where(row8 == r, x, 0.0), axis=0, keepdims=True)


SCAN_LB = 512
SCAN_TS = 512


def _scan_fwd(bu_re, bu_im, a_re, a_im):
    S, L = bu_re.shape
    ts, lb = min(SCAN_TS, S), SCAN_LB
    nt = ts // 8

    def kern(br_ref, bi_ref, ar_ref, ai_ref, or_ref, oi_ref, cr_ref, ci_ref):
        @pl.when(pl.program_id(1) == 0)
        def _():
            cr_ref[...] = jnp.zeros_like(cr_ref)
            ci_ref[...] = jnp.zeros_like(ci_ref)

        row8 = _iota((8, lb), 0)
        ar, ai = ar_ref[...], ai_ref[...]
        a2 = _cmul(ar, ai, ar, ai)
        a4 = _cmul(*a2, *a2)
        pows = ((ar, ai), a2, a4)
        tr, ti = _power_table(ar, ai, row8, False)

        def body(i, carry):
            cr, ci = carry
            off = pl.multiple_of(i * 8, 8)
            xr, xi = _tile_scan(br_ref[pl.ds(off, 8), :], bi_ref[pl.ds(off, 8), :], pows, row8, False)
            mr, mi = _cmul(tr, ti, cr, ci)
            xr, xi = xr + mr, xi + mi
            or_ref[pl.ds(off, 8), :] = xr
            oi_ref[pl.ds(off, 8), :] = xi
            return _pick_row(xr, row8, 7), _pick_row(xi, row8, 7)

        cr, ci = lax.fori_loop(0, nt, body, (cr_ref[...], ci_ref[...]))
        cr_ref[...] = cr
        ci_ref[...] = ci

    blk = pl.BlockSpec((ts, lb), lambda j, i: (i, j))
    par = pl.BlockSpec((1, lb), lambda j, i: (0, j))
    return pl.pallas_call(
        kern, name="s5_scan_fwd", grid=(L // lb, S // ts), in_specs=[blk, blk, par, par], out_specs=[blk, blk],
        out_shape=[jax.ShapeDtypeStruct((S, L), F32)] * 2,
        scratch_shapes=[pltpu.VMEM((1, lb), F32), pltpu.VMEM((1, lb), F32)],
        compiler_params=_cparams(("parallel", "arbitrary")))(bu_re, bu_im, a_re, a_im)


def _scan_bwd(dst_re, dst_im, st_re, st_im, a_re, a_im):
    S, L = dst_re.shape
    ts, lb = min(SCAN_TS, S), SCAN_LB
    nt = ts // 8
    nb = S // ts
    r8 = ts // 8

    def kern(dr_ref, di_ref, sr_ref, si_ref, pr_ref, pi_ref, ar_ref, ai_ref, gr_ref, gi_ref, dar_ref, dai_ref,
             cr_ref, ci_ref):
        step = pl.program_id(1)
        blk = nb - 1 - step

        @pl.when(step == 0)
        def _():
            cr_ref[...] = jnp.zeros_like(cr_ref)
            ci_ref[...] = jnp.zeros_like(ci_ref)
            dar_ref[...] = jnp.zeros_like(dar_ref)
            dai_ref[...] = jnp.zeros_like(dai_ref)

        row8 = _iota((8, lb), 0)
        ar, ai = ar_ref[...], ai_ref[...]
        nai = -ai
        a2 = _cmul(ar, nai, ar, nai)
        a4 = _cmul(*a2, *a2)
        pows = ((ar, nai), a2, a4)
        tr, ti = _power_table(ar, nai, row8, True)
        halo_r = jnp.where(blk == 0, 0.0, pr_ref[...])
        halo_i = jnp.where(blk == 0, 0.0, pi_ref[...])

        def body(n, carry):
            cr, ci, acc_r, acc_i = carry
            i = nt - 1 - n
            off = pl.multiple_of(i * 8, 8)
            gr, gi = _tile_scan(dr_ref[pl.ds(off, 8), :], di_ref[pl.ds(off, 8), :], pows, row8, True)
            mr, mi = _cmul(tr, ti, cr, ci)
            gr, gi = gr + mr, gi + mi
            gr_ref[pl.ds(off, 8), :] = gr
            gi_ref[pl.ds(off, 8), :] = gi
            poff = pl.multiple_of(jnp.maximum(i - 1, 0) * 8, 8)
            before_r = jnp.where(i == 0, halo_r, sr_ref[pl.ds(poff, 8), :])
            before_i = jnp.where(i == 0, halo_i, si_ref[pl.ds(poff, 8), :])
            last_r = _pick_row(before_r, row8, 7)
            last_i = _pick_row(before_i, row8, 7)
            spr = jnp.where(row8 >= 1, pltpu.roll(sr_ref[pl.ds(off, 8), :], 1, 0), last_r)
            spi = jnp.where(row8 >= 1, pltpu.roll(si_ref[pl.ds(off, 8), :], 1, 0), last_i)
            acc_r = acc_r + gr * spr + gi * spi
            acc_i = acc_i + gi * spr - gr * spi
            return _pick_row(gr, row8, 0), _pick_row(gi, row8, 0), acc_r, acc_i

        zero = jnp.zeros((8, lb), F32)
        cr, ci, acc_r, acc_i = lax.fori_loop(0, nt, body, (cr_ref[...], ci_ref[...], zero, zero))
        cr_ref[...] = cr
        ci_ref[...] = ci
        dar_ref[...] += jnp.sum(acc_r, axis=0, keepdims=True)
        dai_ref[...] += jnp.sum(acc_i, axis=0, keepdims=True)

    blk = pl.BlockSpec((ts, lb), lambda j, i: (nb - 1 - i, j))
    halo = pl.BlockSpec((8, lb), lambda j, i: (jnp.maximum((nb - 1 - i) * r8 - 1, 0), j))
    par = pl.BlockSpec((1, lb), lambda j, i: (0, j))
    return pl.pallas_call(
        kern, name="s5_scan_bwd", grid=(L // lb, nb), in_specs=[blk, blk, blk, blk, halo, halo, par, par],
        out_specs=[blk, blk, par, par],
        out_shape=[jax.ShapeDtypeStruct((S, L), F32)] * 2 + [jax.ShapeDtypeStruct((1, L), F32)] * 2,
        scratch_shapes=[pltpu.VMEM((1, lb), F32), pltpu.VMEM((1, lb), F32)],
        compiler_params=_cparams(("parallel", "arbitrary")))(dst_re, dst_im, st_re, st_im, st_re, st_im, a_re, a_im)


def _loss_grad(x, target, gain, ts=256):
    S, D = x.shape

    def kern(x_ref, t_ref, g_ref, loss_ref, dx_ref, dg_ref):
        i = pl.program_id(0)
        tgt = t_ref[...]

        def f(xv, gv):
            err = _rms(xv, gv) - tgt
            return 0.5 * jnp.mean(err * err, axis=-1, keepdims=True)

        rowloss, vjp = jax.vjp(f, x_ref[...], g_ref[...])
        dx, dg = vjp(jnp.ones_like(rowloss))
        dx_ref[...] = dx

        @pl.when(i == 0)
        def _():
            loss_ref[...] = jnp.zeros_like(loss_ref)
            dg_ref[...] = jnp.zeros_like(dg_ref)

        loss_ref[...] += jnp.broadcast_to(jnp.sum(rowloss, axis=0, keepdims=True), loss_ref.shape)
        dg_ref[...] += dg

    row = pl.BlockSpec((ts, D), lambda i: (i, 0))
    return pl.pallas_call(
        kern, name="loss_grad", grid=(S // ts,), in_specs=[row, row, pl.BlockSpec((1, D), lambda i: (0, 0))],
        out_specs=[pl.BlockSpec((8, 128), lambda i: (0, 0)), row, pl.BlockSpec((1, D), lambda i: (0, 0))],
        out_shape=[jax.ShapeDtypeStruct((8, 128), F32), jax.ShapeDtypeStruct((S, D), F32),
                   jax.ShapeDtypeStruct((1, D), F32)],
        compiler_params=_cparams(("arbitrary",)))(x, target, gain)


def _rms_fwd(x, g, name):
    return _rowwise(_rms_fn, [_blk(x)], [g], [x.shape[1]], name, out_dtypes=[BF16])[0]


def _rms_bwd(x, g, dy, name, add=None):
    return _rowwise_bwd(_rms_fn, [_blk(x)], [g], [dy], name, adds=None if add is None else {0: add})


FFN_TC = 1408


def _common_fwd(x, mem, P, L):
    hx = _rms_fwd(x, P['xa_norm'], L + "xa_norm")
    q = _matmul(hx, P['xa_wq'], name=L + "xa_q")
    memn = _rms_fwd(mem, P['mem_norm'], L + "mem_norm")
    kv = _matmul(memn, P['xa_wkv'], name=L + "xa_kv")
    att = _rowwise(_xattn_fn, [_blk(q)], [kv], [1024], L + "xattn", out_dtypes=[BF16])[0]
    x2 = _matmul(att, P['xa_wo'], res=x, name=L + "xa_o")
    hf = _rms_fwd(x2, P['ffn_norm'], L + "ffn_norm")
    hu = _matmul(hf, P['ffn_w_up'], name=L + "ffn_up")
    cw = P['ffn_conv']
    act = _conv_post([(hu, 0, cw, 0), (hu, 2, cw, 2)], _ffn_post, 2, FFN_TC, L + "ffn_conv", out_dtype=BF16)
    x3 = _matmul(act, P['ffn_w_down'], res=x2, name=L + "ffn_down")
    return x3, (x, mem, hx, q, memn, kv, att, x2, hf, hu, act)


def _common_bwd(saved, dx3, P, L):
    x, mem, hx, q, memn, kv, att, x2, hf, hu, act = saved
    G = {}
    dact = _matmul(dx3, P['ffn_w_down'], "nt", name=L + "ffn_down_dx")
    G['ffn_w_down'] = _matmul(act, dx3, "tn", name=L + "ffn_down_dw")
    cw = P['ffn_conv']
    dhu_u, dcw_u, dhu_g, dcw_g = _conv_post_bwd([(hu, 0, cw, 0), (hu, 2, cw, 2)], _ffn_post, 2, FFN_TC, dact,
                                                L + "ffn_conv_bwd")
    G['ffn_conv'] = jnp.concatenate([dcw_u, dcw_g], axis=1)
    dhf = _matmul_cat([dhu_u, dhu_g], P['ffn_w_up'], "nt", name=L + "ffn_up_dx")
    G['ffn_w_up'] = jnp.concatenate([_matmul(hf, dhu_u, "tn", name=L + "ffn_up_dw_up"),
                                     _matmul(hf, dhu_g, "tn", name=L + "ffn_up_dw_gate")], axis=1)
    dx2, G['ffn_norm'] = _rms_bwd(x2, P['ffn_norm'], dhf, L + "ffn_norm_bwd", add=dx3)
    datt = _matmul(dx2, P['xa_wo'], "nt", name=L + "xa_o_dx")
    G['xa_wo'] = _matmul(att, dx2, "tn", name=L + "xa_o_dw")
    dq, dkv = _rowwise_bwd(_xattn_fn, [_blk(q)], [kv], [datt], L + "xattn_bwd", out_dtypes=[BF16])
    dhx = _matmul(dq, P['xa_wq'], "nt", name=L + "xa_q_dx")
    G['xa_wq'] = _matmul(hx, dq, "tn", name=L + "xa_q_dw")
    dmemn = _matmul(dkv, P['xa_wkv'], "nt", name=L + "xa_kv_dx")
    G['xa_wkv'] = _matmul(memn, dkv, "tn", name=L + "xa_kv_dw")
    _, G['mem_norm'] = _rms_bwd(mem, P['mem_norm'], dmemn, L + "mem_norm_bwd")
    dx, G['xa_norm'] = _rms_bwd(x, P['xa_norm'], dhx, L + "xa_norm_bwd", add=dx2)
    return dx, G


U_COLS = (2048, 512)


def _even_fwd(x, P):
    S = x.shape[0]
    h0 = _rms_fwd(x, P['mix_norm'], "l0_mix_norm")
    proj = _matmul(h0, P['w_in'], name="l0_in")
    tabs = _ret_tables(S)
    o_raw, rstates = _ret_call(proj, tabs)
    o = _rowwise(_ret_post_fn, [_blk(o_raw), _blk(proj, 512, 3)], [P['ret_norm']], [512], "l0_ret_post",
                 out_dtypes=[BF16])[0]
    prep_in = (P['s5_lambda_re'], P['s5_lambda_im'], P['s5_log_dt'], P['s5_b_re'], P['s5_b_im'], P['s5_c_re'],
               P['s5_c_im'])
    a_re, a_im, bd_re, bd_im, cd_re, cd_im = _s5_prep(prep_in)
    a_re_f, a_im_f = a_re.reshape(1, 2048), a_im.reshape(1, 2048)
    bu_re = _matmul(proj, bd_re, name="l0_s5_bu_re", a_cols=U_COLS)
    bu_im = _matmul(proj, bd_im, name="l0_s5_bu_im", a_cols=U_COLS)
    st_re, st_im = _scan_fwd(bu_re, bu_im, a_re_f, a_im_f)
    y1 = _matmul(st_re, cd_re, name="l0_s5_y_re")
    y2 = _matmul(st_im, cd_im, name="l0_s5_y_im")
    yg = _rowwise(_s5_post_fn, [_blk(y1), _blk(y2), _blk(proj, 512, 4)],
                  [P['s5_d'], P['s5_w_glu'], P['s5_b_glu']], [512], "l0_s5_post", out_dtypes=[BF16])[0]
    x1 = _matmul_cat([o, yg], P['w_out'], "nn", res=x, name="l0_out")
    saved = (x, h0, proj, tabs, o_raw, rstates, prep_in, a_re_f, a_im_f, bd_re, bd_im, cd_re, cd_im, st_re, st_im,
             y1, y2, o, yg)
    return x1, saved


def _even_bwd(saved, dx1, P):
    (x, h0, proj, tabs, o_raw, rstates, prep_in, a_re_f, a_im_f, bd_re, bd_im, cd_re, cd_im, st_re, st_im, y1, y2,
     o, yg) = saved
    G = {}
    dmerged = _matmul(dx1, P['w_out'], "nt", name="l0_out_dx")
    G['w_out'] = jnp.concatenate([_matmul(o, dx1, "tn", name="l0_out_dw_ret"),
                                  _matmul(yg, dx1, "tn", name="l0_out_dw_s5")], axis=0)
    do_raw, dgate, G['ret_norm'] = _rowwise_bwd(
        _ret_post_fn, [_blk(o_raw), _blk(proj, 512, 3)], [P['ret_norm']], [_blk(dmerged, 512, 0)], "l0_ret_post_bwd",
        out_dtypes=[F32, BF16])
    dq, dk, dv = _ret_call(proj, tabs, states=rstates, do=do_raw)
    dy1, dy2, du_a, G['s5_d'], G['s5_w_glu'], G['s5_b_glu'] = _rowwise_bwd(
        _s5_post_fn, [_blk(y1), _blk(y2), _blk(proj, 512, 4)], [P['s5_d'], P['s5_w_glu'], P['s5_b_glu']],
        [_blk(dmerged, 512, 1)], "l0_s5_post_bwd", out_dtypes=[BF16, BF16, F32])
    dst_re = _matmul(dy1, cd_re, "nt", name="l0_s5_y_re_dx")
    dcd_re = _matmul(st_re, dy1, "tn", name="l0_s5_y_re_dw")
    dst_im = _matmul(dy2, cd_im, "nt", name="l0_s5_y_im_dx")
    dcd_im = _matmul(st_im, dy2, "tn", name="l0_s5_y_im_dw")
    dbu_re, dbu_im, da_re, da_im = _scan_bwd(dst_re, dst_im, st_re, st_im, a_re_f, a_im_f)
    du = _matmul(dbu_re, bd_re, "nt", res=du_a, name="l0_s5_bu_re_dx")
    du = _matmul(dbu_im, bd_im, "nt", res=du, name="l0_s5_bu_im_dx", out_dtype=BF16)
    dbd_re = _matmul(proj, dbu_re, "tn", name="l0_s5_bu_re_dw", a_cols=U_COLS)
    dbd_im = _matmul(proj, dbu_im, "tn", name="l0_s5_bu_im_dw", a_cols=U_COLS)
    dprep = _s5_prep(prep_in, cots=(da_re.reshape(32, 64), da_im.reshape(32, 64), dbd_re, dbd_im, dcd_re, dcd_im))
    for n, d in zip(('s5_lambda_re', 's5_lambda_im', 's5_log_dt', 's5_b_re', 's5_b_im', 's5_c_re', 's5_c_im'), dprep):
        G[n] = d
    pieces = [dq, dk, dv, dgate, du]
    dh0 = _matmul_cat(pieces, P['w_in'], "nt", name="l0_in_dx")
    G['w_in'] = jnp.concatenate([_matmul(h0, p, "tn", name="l0_in_dw_%d" % n) for n, p in enumerate(pieces)], axis=1)
    dx, G['mix_norm'] = _rms_bwd(x, P['mix_norm'], dh0, "l0_mix_norm_bwd", add=dx1)
    return dx, G


def _odd_fwd(x, P):
    h1 = _rms_fwd(x, P['mix_norm'], "l1_mix_norm")
    pm = _matmul(h1, P['w_main'], name="l1_in_main")
    pt = _matmul(h1, P['w_tail'], name="l1_in_tail")
    qkv = _conv_post([(pm, 0, P['conv'], 0)], _silu, 3, 1024, "l1_conv")
    g_e, beta_e = _rowwise(_gdn_gates_fn, [_blk(pt)], [P['a_log_p'], P['dtb_p']], [1024, 1024], "l1_gdn_gates")
    w, u, qd, kd, qk, tinv = _gdn_intra_call(qkv, g_e, beta_e)
    o_raw, gstates = _gdn_pass(w, u, qd, kd, qk, g_e)
    og = _rowwise(_gdn_post_fn, [_blk(o_raw), _blk(pm, 1024, 3)], [P['o_norm']], [1024], "l1_gdn_post",
                  out_dtypes=[BF16])[0]
    x1 = _matmul(og, P['w_out'], res=x, name="l1_out")
    return x1, (x, h1, pm, pt, qkv, g_e, beta_e, w, u, qd, kd, qk, tinv, o_raw, gstates, og)


def _odd_bwd(saved, dx1, P):
    x, h1, pm, pt, qkv, g_e, beta_e, w, u, qd, kd, qk, tinv, o_raw, gstates, og = saved
    G = {}
    dog = _matmul(dx1, P['w_out'], "nt", name="l1_out_dx")
    G['w_out'] = _matmul(og, dx1, "tn", name="l1_out_dw")
    do_raw, dz, G['o_norm'] = _rowwise_bwd(_gdn_post_fn, [_blk(o_raw), _blk(pm, 1024, 3)], [P['o_norm']], [dog],
                                           "l1_gdn_post_bwd", out_dtypes=[F32, BF16])
    dw, du, dqd, dkd, dqk, dg_pass = _gdn_pass(w, u, qd, kd, qk, g_e, states=gstates, do=do_raw)
    dqkv = _gdn_intra_call(qkv, g_e, beta_e, cots=(dw, du, dqd, dkd, dqk, dg_pass, tinv))
    dg_e, dbeta_e = dqkv[3], dqkv[4]
    dpt, G['a_log_p'], G['dtb_p'] = _rowwise_bwd(_gdn_gates_fn, [_blk(pt)], [P['a_log_p'], P['dtb_p']],
                                                 [dg_e, dbeta_e], "l1_gdn_gates_bwd", out_dtypes=[BF16])
    pieces, dcw = [], []
    for part in range(3):
        dxp, dwp = _conv_post_bwd([(pm, part, P['conv'], part)], _silu, 1, 1024, dqkv[part],
                                  "l1_conv_bwd_%d" % part)
        pieces.append(dxp)
        dcw.append(dwp)
    G['conv'] = jnp.concatenate(dcw, axis=1)
    pieces += [dz, dpt]
    dh1 = _matmul_cat(pieces, P['w_all'], "nt", name="l1_in_dx")
    G['w_all'] = jnp.concatenate([_matmul(h1, p, "tn", name="l1_in_dw_%d" % n) for n, p in enumerate(pieces)], axis=1)
    dx, G['mix_norm'] = _rms_bwd(x, P['mix_norm'], dh1, "l1_mix_norm_bwd", add=dx1)
    return dx, G


def _row(v):
    return v.reshape(1, -1)


def _local_step(x, mem, target, W, later_weights=None, early_grads=None):
    P0 = {
        'mix_norm': _row(W['l0_mix_norm']), 'w_in': W['l0_w_in'], 'ret_norm': _row(W['l0_ret_norm']),
        's5_lambda_re': W['l0_s5_lambda_re'], 's5_lambda_im': W['l0_s5_lambda_im'],
        's5_log_dt': W['l0_s5_log_dt'].reshape(32, 1),
        's5_b_re': W['l0_s5_b_re'].reshape(512, 64), 's5_b_im': W['l0_s5_b_im'].reshape(512, 64),
        's5_c_re': W['l0_s5_c_re'].reshape(2048, 16), 's5_c_im': W['l0_s5_c_im'].reshape(2048, 16),
        's5_d': _row(W['l0_s5_d']), 's5_w_glu': W['l0_s5_w_glu'].astype(F32), 's5_b_glu': _row(W['l0_s5_b_glu']),
        'w_out': W['l0_w_out'],
    }
    def common(L):
        return {'xa_norm': _row(W[L + 'xa_norm']), 'mem_norm': _row(W[L + 'mem_norm']), 'xa_wq': W[L + 'xa_wq'],
                'xa_wkv': W[L + 'xa_wkv'], 'xa_wo': W[L + 'xa_wo'], 'ffn_norm': _row(W[L + 'ffn_norm']),
                'ffn_w_up': W[L + 'ffn_w_up'], 'ffn_conv': W[L + 'ffn_conv'], 'ffn_w_down': W[L + 'ffn_w_down']}

    C0 = common('l0_')
    x1, s_even = _even_fwd(x, P0)
    x3, s_c0 = _common_fwd(x1, mem, C0, "l0_")

    if later_weights is not None:
        W = dict(W, **later_weights(x3))
    w_in1 = W['l1_w_in']
    pad8 = jnp.zeros((8,), F32)
    w_all = jnp.pad(w_in1, ((0, 0), (0, 112)))
    P1 = {
        'mix_norm': _row(W['l1_mix_norm']), 'w_main': w_in1[:, :4096], 'w_tail': w_all[:, 4096:], 'w_all': w_all,
        'conv': W['l1_conv'],
        'a_log_p': _row(jnp.concatenate([pad8, W['l1_a_log'], jnp.zeros((112,), F32)])),
        'dtb_p': _row(jnp.concatenate([pad8, W['l1_dt_bias'], jnp.zeros((112,), F32)])),
        'o_norm': _row(W['l1_o_norm']), 'w_out': W['l1_w_out'],
    }
    C1 = common('l1_')
    x4, s_odd = _odd_fwd(x3, P1)
    x6, s_c1 = _common_fwd(x4, mem, C1, "l1_")
    loss_tile, dx6, d_final = _loss_grad(x6, target, _row(W['final_norm']))

    G = {'final_norm': d_final.reshape(-1)}
    dx4, g = _common_bwd(s_c1, dx6, C1, "l1_")
    for k, v in g.items():
        G['l1_' + k] = v
    dx3, g = _odd_bwd(s_odd, dx4, P1)
    G['l1_mix_norm'] = g['mix_norm']
    G['l1_w_in'] = g['w_all'][:, :4112]
    G['l1_conv'] = g['conv']
    G['l1_a_log'] = g['a_log_p'][0, 8:16]
    G['l1_dt_bias'] = g['dtb_p'][0, 8:16]
    G['l1_o_norm'] = g['o_norm']
    G['l1_w_out'] = g['w_out']
    if early_grads is not None:
        zero = early_grads(G)
        C0 = dict(C0, ffn_w_down=C0['ffn_w_down'] + zero.astype(C0['ffn_w_down'].dtype))
    dx1, g = _common_bwd(s_c0, dx3, C0, "l0_")
    for k, v in g.items():
        G['l0_' + k] = v
    dx0, g = _even_bwd(s_even, dx1, P0)
    for k, v in g.items():
        G['l0_' + k] = v
    return loss_tile, dx0, G


ANY = pl.BlockSpec(memory_space=pl.ANY)


def _place():
    return lax.axis_index("x"), lax.axis_index("y"), lax.axis_index("c")


def _my_chip():
    return 2 * lax.axis_index("x") + lax.axis_index("y")


def _chip_peers(x, y):
    return [(1 - x, y), (x, 1 - y), (1 - x, 1 - y)]


def _half(ref, mode, shard, j, h, split):
    r, w = shard
    rh = r // 2 if split else r
    h = h if split else 0
    if mode == 'row':
        return ref.at[pl.ds(j * r + h * rh, rh), :]
    if mode == 'col':
        return ref.at[pl.ds(h * rh, rh), pl.ds(j * w, w)]
    return ref.at[j, pl.ds(h * rh, rh), :]


def _place_shard(shard, mode, name):
    r, w = shard.shape
    dtype = BF16 if mode != 'tap' else shard.dtype
    if mode == 'tap':
        mode = 'slab'
    tr = _row_tile(r, w)
    nb = r // tr

    def kern(s_ref, o_ref):
        o_ref[...] = s_ref[...].astype(o_ref.dtype)

    if mode == 'row':
        full, o_spec = (4 * r, w), pl.BlockSpec((tr, w), lambda i: (_my_chip() * nb + i, 0))
    elif mode == 'col':
        full, o_spec = (r, 4 * w), pl.BlockSpec((tr, w), lambda i: (i, _my_chip()))
    else:
        full, o_spec = (4, r, w), pl.BlockSpec((None, tr, w), lambda i: (_my_chip(), i, 0))
    return pl.pallas_call(kern, name=name, grid=(nb,), in_specs=[pl.BlockSpec((tr, w), lambda i: (i, 0))],
                          out_specs=o_spec, out_shape=jax.ShapeDtypeStruct(full, dtype),
                          compiler_params=_cparams(("parallel",)))(shard)


def _gather_placed(fulls, modes, shards, splits):
    n = len(fulls)

    def body(*refs):
        outs = refs[n:2 * n]
        send_sems, recv_sems = refs[2 * n:]
        x, y, c = _place()
        peers = _chip_peers(x, y)
        me = 2 * x + y

        def win(a, j, h):
            return _half(outs[a], modes[a], shards[a], j, h, splits[a])

        def copy(a, k, j, h, to):
            return pltpu.make_async_remote_copy(src_ref=win(a, j, h), dst_ref=win(a, j, h),
                                                send_sem=send_sems.at[6 * a + k], recv_sem=recv_sems.at[6 * a + k],
                                                device_id=to, device_id_type=MESH)

        over_ici = [copy(a, k, me, c, (p[0], p[1], c)) for a in range(n) for k, p in enumerate(peers)]
        for cp in over_ici:
            cp.start()
        passed = []
        for a in range(n):
            for k, p in enumerate(peers):
                j = 2 * p[0] + p[1]
                copy(a, k, j, c, (p[0], p[1], c)).wait_recv()
                if splits[a]:
                    fwd = copy(a, 3 + k, j, c, (x, y, 1 - c))
                    fwd.start()
                    passed.append(fwd)
        for a in range(n):
            if splits[a]:
                for k, p in enumerate(peers):
                    copy(a, 3 + k, 2 * p[0] + p[1], 1 - c, (x, y, 1 - c)).wait_recv()
        for cp in over_ici + passed:
            cp.wait_send()

    return pl.pallas_call(
        body, name="gather_weights", in_specs=[ANY] * n, out_specs=[ANY] * n,
        out_shape=[jax.ShapeDtypeStruct(f.shape, f.dtype) for f in fulls],
        input_output_aliases={a: a for a in range(n)},
        scratch_shapes=[pltpu.SemaphoreType.DMA((6 * n,)), pltpu.SemaphoreType.DMA((6 * n,))],
    )(*fulls)


_FLIPS = [(dx, dy, dc) for dx in (0, 1) for dy in (0, 1) for dc in (0, 1) if (dx, dy, dc) != (0, 0, 0)]


def _send_other_half(gs, small, name):
    n = len(gs)

    def body(*refs):
        ins, outs = refs[:n], refs[n + 1:2 * n + 1]
        small_ref = refs[2 * n + 1]
        send_sems, recv_sems, small_send, small_recv = refs[2 * n + 2:]
        x, y, c = _place()
        me = 4 * x + 2 * y + c

        def peer(f):
            return (x ^ f[0], y ^ f[1], c ^ f[2])

        def small_copy(k, slab, to):
            return pltpu.make_async_remote_copy(src_ref=small_ref.at[slab], dst_ref=small_ref.at[slab],
                                                send_sem=small_send.at[k], recv_sem=small_recv.at[k], device_id=to,
                                                device_id_type=MESH)

        cps = []
        for a in range(n):
            rh = gs[a].shape[1] // 2
            cps.append(pltpu.make_async_remote_copy(
                src_ref=ins[a].at[:, pl.ds((1 - c) * rh, rh), :], dst_ref=outs[a], send_sem=send_sems.at[a],
                recv_sem=recv_sems.at[a], device_id=(x, y, 1 - c), device_id_type=MESH))
        smalls = [small_copy(k, me, peer(f)) for k, f in enumerate(_FLIPS)]
        for cp in cps + smalls:
            cp.start()
        for cp in cps:
            cp.wait()
        for k, f in enumerate(_FLIPS):
            p = peer(f)
            small_copy(k, 4 * p[0] + 2 * p[1] + p[2], p).wait_recv()
        for cp in smalls:
            cp.wait_send()

    outs = pl.pallas_call(
        body, name=name, in_specs=[ANY] * (n + 1), out_specs=[ANY] * (n + 1),
        out_shape=[jax.ShapeDtypeStruct((g.shape[0], g.shape[1] // 2, g.shape[2]), g.dtype) for g in gs]
        + [jax.ShapeDtypeStruct(small.shape, small.dtype)],
        input_output_aliases={n: n},
        scratch_shapes=[pltpu.SemaphoreType.DMA((n,)), pltpu.SemaphoreType.DMA((n,)),
                        pltpu.SemaphoreType.DMA((7,)), pltpu.SemaphoreType.DMA((7,))],
    )(*gs, small)
    return outs[:n], outs[n]


def _send_to_chips(ps, widths):
    n = len(ps)

    def body(*refs):
        ins, outs = refs[:n], refs[n:2 * n]
        send_sems, recv_sems = refs[2 * n:]
        x, y, c = _place()
        peers = _chip_peers(x, y)
        me = 2 * x + y

        def src(a, j):
            if ps[a].shape[0] == 4:
                return ins[a].at[j]
            return ins[a].at[0, :, pl.ds(j * widths[a], widths[a])]

        def copy(a, k, j, dst_slab, to):
            return pltpu.make_async_remote_copy(src_ref=src(a, j), dst_ref=outs[a].at[dst_slab],
                                                send_sem=send_sems.at[3 * a + k], recv_sem=recv_sems.at[3 * a + k],
                                                device_id=(to[0], to[1], c), device_id_type=MESH)

        sends = [copy(a, k, 2 * p[0] + p[1], me, p) for a in range(n) for k, p in enumerate(peers)]
        for cp in sends:
            cp.start()
        for a in range(n):
            for k, p in enumerate(peers):
                copy(a, k, me, 2 * p[0] + p[1], p).wait_recv()
        for cp in sends:
            cp.wait_send()

    return pl.pallas_call(
        body, name="send_to_chips", in_specs=[ANY] * n, out_specs=[ANY] * n,
        out_shape=[jax.ShapeDtypeStruct((4, p.shape[1], w), p.dtype) for p, w in zip(ps, widths)],
        scratch_shapes=[pltpu.SemaphoreType.DMA((3 * n,)), pltpu.SemaphoreType.DMA((3 * n,))],
    )(*ps)


def _share_halves(bufs, name):
    n = len(bufs)

    def body(*refs):
        outs = refs[n:2 * n]
        send_sems, recv_sems = refs[2 * n:]
        x, y, c = _place()
        sends, waits = [], []
        for a in range(n):
            rh = bufs[a].shape[0] // 2
            mine = outs[a].at[pl.ds(c * rh, rh), :]
            other = outs[a].at[pl.ds((1 - c) * rh, rh), :]
            sends.append(pltpu.make_async_remote_copy(src_ref=mine, dst_ref=mine, send_sem=send_sems.at[a],
                                                      recv_sem=recv_sems.at[a], device_id=(x, y, 1 - c),
                                                      device_id_type=MESH))
            waits.append(pltpu.make_async_remote_copy(src_ref=mine, dst_ref=other, send_sem=send_sems.at[a],
                                                      recv_sem=recv_sems.at[a], device_id=(x, y, 1 - c),
                                                      device_id_type=MESH))
        for cp in sends:
            cp.start()
        for cp in waits:
            cp.wait()

    return pl.pallas_call(
        body, name=name, in_specs=[ANY] * n, out_specs=[ANY] * n,
        out_shape=[jax.ShapeDtypeStruct(b.shape, b.dtype) for b in bufs],
        input_output_aliases={a: a for a in range(n)},
        scratch_shapes=[pltpu.SemaphoreType.DMA((n,)), pltpu.SemaphoreType.DMA((n,))],
    )(*bufs)


def _gather_all(mine):
    flips = [(dx, dy, dc) for dx in (0, 1) for dy in (0, 1) for dc in (0, 1) if (dx, dy, dc) != (0, 0, 0)]

    def body(x_ref, out_ref, send_sems, recv_sems, local_sem):
        x, y, c = _place()
        me = 4 * x + 2 * y + c

        def peer(f):
            return (x ^ f[0], y ^ f[1], c ^ f[2])

        def copy(k, slab, to):
            return pltpu.make_async_remote_copy(src_ref=x_ref, dst_ref=out_ref.at[slab], send_sem=send_sems.at[k],
                                                recv_sem=recv_sems.at[k], device_id=to, device_id_type=MESH)

        own = pltpu.make_async_copy(x_ref, out_ref.at[me], local_sem)
        own.start()
        sends = [copy(k, me, peer(f)) for k, f in enumerate(flips)]
        for s in sends:
            s.start()
        for k, f in enumerate(flips):
            p = peer(f)
            copy(k, 4 * p[0] + 2 * p[1] + p[2], p).wait_recv()
        for s in sends:
            s.wait_send()
        own.wait()

    return pl.pallas_call(
        body, name="gather_all", in_specs=[ANY], out_specs=ANY,
        out_shape=jax.ShapeDtypeStruct((8,) + mine.shape, mine.dtype),
        scratch_shapes=[pltpu.SemaphoreType.DMA((7,)), pltpu.SemaphoreType.DMA((7,)), pltpu.SemaphoreType.DMA],
    )(mine)


TILE_BYTES = 2 * 1024 * 1024


def _row_tile(rows, width=1024):
    for t in (512, 352, 256, 176, 128, 64, 32, 16, 8):
        if rows % t == 0 and t * width * 4 <= TILE_BYTES:
            return t
    return rows


def _pair_sum(g, got, name):
    ns, r, w = g.shape
    rh = r // 2
    tr = _row_tile(rh, w)
    nb = rh // tr

    def kern(g_ref, o_ref, out_ref):
        out_ref[...] = (g_ref[...] + o_ref[...]).astype(BF16)

    return pl.pallas_call(
        kern, name=name, grid=(ns, nb),
        in_specs=[pl.BlockSpec((None, tr, w), lambda j, i: (j, lax.axis_index("c") * nb + i, 0)),
                  pl.BlockSpec((None, tr, w), lambda j, i: (j, i, 0))],
        out_specs=pl.BlockSpec((None, tr, w), lambda j, i: (j, i, 0)),
        out_shape=jax.ShapeDtypeStruct((ns, rh, w), BF16),
        compiler_params=_cparams(("parallel", "parallel")))(g, got)


def _chip_sum(pair, recv, w, name):
    rh = pair.shape[1]
    tr = _row_tile(rh, w)
    nb = rh // tr

    def kern(own_ref, r1_ref, r2_ref, r3_ref, out_ref):
        acc = own_ref[...].astype(F32)
        for r_ref in (r1_ref, r2_ref, r3_ref):
            acc = acc + r_ref[...].astype(F32)
        out_ref[...] = acc

    if pair.shape[0] == 4:
        own_spec = pl.BlockSpec((None, tr, w), lambda i: (_my_chip(), i, 0))
    else:
        own_spec = pl.BlockSpec((None, tr, w), lambda i: (0, i, _my_chip()))
    recv_specs = [pl.BlockSpec((None, tr, w), functools.partial(lambda i, d: ((_my_chip() + d) % 4, i, 0), d=d))
                  for d in (1, 2, 3)]
    return pl.pallas_call(
        kern, name=name, grid=(nb,), in_specs=[own_spec] + recv_specs,
        out_specs=pl.BlockSpec((tr, w), lambda i: (lax.axis_index("c") * nb + i, 0)),
        out_shape=jax.ShapeDtypeStruct((2 * rh, w), F32), compiler_params=_cparams(("parallel",)))(pair, recv, recv, recv)


def _slab_sum(slabs, name):
    n, R, w = slabs.shape
    tr = _row_tile(R)

    def kern(s_ref, o_ref):
        acc = s_ref[0].astype(F32)
        for k in range(1, n):
            acc = acc + s_ref[k].astype(F32)
        o_ref[...] = acc

    return pl.pallas_call(
        kern, name=name, grid=(R // tr,), in_specs=[pl.BlockSpec((n, tr, w), lambda i: (0, i, 0))],
        out_specs=pl.BlockSpec((tr, w), lambda i: (i, 0)), out_shape=jax.ShapeDtypeStruct((R, w), F32),
        compiler_params=_cparams(("parallel",)))(slabs)


def _adamw(w, g, m, v, name):
    R, C = w.shape
    tr = _pick(R, (256, 128, 64, 32, 16, 8))

    def kern(w_ref, g_ref, m_ref, v_ref, d_ref, nm_ref, nv_ref):
        gv = g_ref[...]
        m2 = ADAM_B1 * m_ref[...] + (1.0 - ADAM_B1) * gv
        v2 = ADAM_B2 * v_ref[...] + (1.0 - ADAM_B2) * jnp.square(gv)
        m_hat = m2 / (1.0 - ADAM_B1 ** ADAM_STEP)
        v_hat = v2 / (1.0 - ADAM_B2 ** ADAM_STEP)
        d_ref[...] = -ADAM_LR * (m_hat / (jnp.sqrt(v_hat) + ADAM_EPS) + ADAM_WD * w_ref[...])
        nm_ref[...] = m2
        nv_ref[...] = v2

    spec = pl.BlockSpec((tr, C), lambda i: (i, 0))
    return pl.pallas_call(
        kern, name=name, grid=(R // tr,), in_specs=[spec] * 4, out_specs=[spec] * 3,
        out_shape=[jax.ShapeDtypeStruct((R, C), F32)] * 3, compiler_params=_cparams(("parallel",)))(w, g, m, v)


def _pack_small(vals):
    flat = jnp.concatenate([vals[n].astype(F32).reshape(-1) for n in SMALL_NAMES])
    rows = -(-flat.shape[0] // (8 * LANES)) * 8
    return jnp.pad(flat, (0, rows * LANES - flat.shape[0])).reshape(rows, LANES)


def _unpack_small(packed, shapes):
    flat = packed.reshape(-1)
    out = {}
    off = 0
    for n in SMALL_NAMES:
        size = int(np.prod(shapes[n]))
        out[n] = flat[off:off + size].reshape(shapes[n])
        off += size
    return out


HBM = pl.BlockSpec(memory_space=pltpu.HBM)
SEM = pl.BlockSpec(memory_space=pltpu.SEMAPHORE)
DATAFLOW = pltpu.SideEffectType.DATAFLOW_SIDE_EFFECTING


def _in_hbm(a):
    return pltpu.with_memory_space_constraint(a, pltpu.HBM)


def _split_copy_start(srcs, lands, copies, after, name):
    ns, nl = len(srcs), len(lands)
    ncopy = len(copies(list(srcs), list(lands), None, None, probe=True))

    def body(*refs):
        src_refs, land_refs = refs[:ns], refs[ns:ns + nl]
        send_sems, recv_sems = refs[ns + nl + 1:ns + nl + 3]
        token = refs[-1]
        for cp in copies(src_refs, land_refs, send_sems, recv_sems):
            cp.start()
        token[...] = jnp.zeros_like(token)

    outs = pl.pallas_call(
        body, name=name,
        out_shape=(pltpu.SemaphoreType.DMA((ncopy,)), pltpu.SemaphoreType.DMA((ncopy,)),
                   *[pltpu.HBM(a.shape, a.dtype) for a in srcs], *[pltpu.HBM(a.shape, a.dtype) for a in lands],
                   jax.ShapeDtypeStruct((8, 128), F32)),
        in_specs=[HBM] * (ns + nl) + [ANY],
        out_specs=(SEM, SEM, *[HBM] * (ns + nl), pl.BlockSpec(memory_space=pltpu.VMEM)),
        input_output_aliases={i: 2 + i for i in range(ns + nl)},
        compiler_params=pltpu.CompilerParams(has_side_effects=DATAFLOW),
    )(*[_in_hbm(a) for a in srcs], *[_in_hbm(a) for a in lands], after)
    return outs[0], outs[1], outs[2:2 + ns], outs[2 + ns:2 + ns + nl], outs[-1][0, 0]


def _split_copy_wait(send_sems, recv_sems, srcs, lands, copies, after, name):
    ns, nl = len(srcs), len(lands)

    def body(*refs):
        src_refs, land_refs = refs[:ns], refs[ns:ns + nl]
        send_ref, recv_ref = refs[ns + nl:ns + nl + 2]
        for cp in copies(src_refs, land_refs, send_ref, recv_ref):
            cp.wait_send()
            cp.wait_recv()

    outs = pl.pallas_call(
        body, name=name,
        out_shape=tuple(pltpu.HBM(a.shape, a.dtype) for a in list(srcs) + list(lands)),
        in_specs=[HBM] * (ns + nl) + [SEM, SEM, ANY], out_specs=tuple([HBM] * (ns + nl)),
        input_output_aliases={i: i for i in range(ns + nl)},
        compiler_params=pltpu.CompilerParams(has_side_effects=DATAFLOW),
    )(*srcs, *lands, send_sems, recv_sems, after)
    return outs[:ns], outs[ns:]


def _matrix_mode(n):
    return 'slab' if n == 'l1_w_in' else ('row' if MATRICES[n] == 0 else 'col')


def _placed(A, names):
    modes = ['slab' if n in CONVS else _matrix_mode(n) for n in names]
    fulls = [_place_shard(A[n], 'tap' if n in CONVS else m, "place_" + n) for n, m in zip(names, modes)]
    return fulls, modes


def _assembled(names, modes, outs):
    return {n: jnp.concatenate([o[j] for j in range(4)], axis=1) if m == 'slab' else o
            for n, m, o in zip(names, modes, outs)}


def _gather_weights(A, names):
    fulls, modes = _placed(A, names)
    outs = _gather_placed(fulls, modes, [A[n].shape for n in names], [n not in CONVS for n in names])
    return _assembled(names, modes, outs)


def _whole_shard_copies(modes, shards):
    def copies(src_refs, land_refs, send_sems, recv_sems, probe=False):
        if probe:
            return [None] * (3 * len(land_refs))
        x, y, c = _place()
        me = 2 * x + y
        out = []
        for a, ref in enumerate(land_refs):
            for k, p in enumerate(_chip_peers(x, y)):
                out.append(pltpu.make_async_remote_copy(
                    src_ref=_half(ref, modes[a], shards[a], me, 0, False),
                    dst_ref=_half(ref, modes[a], shards[a], me, 0, False),
                    send_sem=send_sems.at[3 * a + k], recv_sem=recv_sems.at[3 * a + k],
                    device_id=(p[0], p[1], c), device_id_type=MESH))
        return out
    return copies


def _gather_weights_start(A, names, after):
    fulls, modes = _placed(A, names)
    copies = _whole_shard_copies(modes, [A[n].shape for n in names])
    send_sems, recv_sems, _, lands, zero = _split_copy_start([], fulls, copies, after, "gather_start")
    return (send_sems, recv_sems, lands, copies, names, modes), zero


def _gather_weights_wait(state, after):
    send_sems, recv_sems, lands, copies, names, modes = state
    _, outs = _split_copy_wait(send_sems, recv_sems, [], lands, copies, after, "gather_wait")
    return _assembled(names, modes, outs)


def _to_chips_copies(pair_shapes, widths):
    def copies(src_refs, land_refs, send_sems, recv_sems, probe=False):
        if probe:
            return [None] * (3 * len(land_refs))
        x, y, c = _place()
        me = 2 * x + y
        out = []
        for a, (src, land) in enumerate(zip(src_refs, land_refs)):
            for k, p in enumerate(_chip_peers(x, y)):
                j = 2 * p[0] + p[1]
                part = src.at[j] if pair_shapes[a][0] == 4 else src.at[0, :, pl.ds(j * widths[a], widths[a])]
                out.append(pltpu.make_async_remote_copy(
                    src_ref=part, dst_ref=land.at[me], send_sem=send_sems.at[3 * a + k],
                    recv_sem=recv_sems.at[3 * a + k], device_id=(p[0], p[1], c), device_id_type=MESH))
        return out
    return copies


def _reduce_begin(G, names, small, tag):
    gs, widths = [], []
    for n in names:
        g = G[n]
        mode = _matrix_mode(n)
        if mode == 'row':
            gs.append(g.reshape(4, g.shape[0] // 4, g.shape[1]))
            widths.append(g.shape[1])
        elif mode == 'col':
            gs.append(g[None])
            widths.append(g.shape[1] // 4)
        else:
            wd = g.shape[1] // 4
            gs.append(jnp.stack([g[:, j * wd:(j + 1) * wd] for j in range(4)]))
            widths.append(wd)
    got, small = _send_other_half(gs, small, "send_other_half_" + tag)
    pairs = [_pair_sum(g, o, "pair_sum_" + n) for n, g, o in zip(names, gs, got)]
    return pairs, widths, small


def _reduce_end(names, pairs, recv, widths, tag):
    halves = [_chip_sum(p, r, w, "chip_sum_" + n) for n, p, r, w in zip(names, pairs, recv, widths)]
    return dict(zip(names, _share_halves(halves, "share_halves_" + tag)))


def _small_slab(packed):
    me8 = 4 * lax.axis_index("x") + 2 * lax.axis_index("y") + lax.axis_index("c")
    return lax.dynamic_update_slice(jnp.zeros((8,) + packed.shape, F32), packed[None], (me8, 0, 0))


def kernel(*args):
    A = dict(zip(ARG_NAMES, args, strict=True))
    x, mem, target = A['x'][0], A['mem'][0], A['loss_target'][0]

    first = [n for n in MATRIX_NAMES if n.startswith('l0_')] + list(CONVS)
    second = [n for n in MATRIX_NAMES if n.startswith('l1_')]
    W = _gather_weights(A, first)
    for n in SMALL_NAMES:
        if n not in CONVS:
            W[n] = A[n]
    flight, zero = _gather_weights_start(A, second, W['l0_w_in'])
    W['l0_mix_norm'] = W['l0_mix_norm'] + zero

    reduce_state = {}

    def early_grads(G):
        pairs, widths, _ = _reduce_begin(G, second, jnp.zeros((8, 8, LANES), F32), "l1")
        copies = _to_chips_copies([p.shape for p in pairs], widths)
        lands = [lax.empty((4, p.shape[1], w), p.dtype) for p, w in zip(pairs, widths)]
        send_sems, recv_sems, pairs, lands, zero = _split_copy_start(pairs, lands, copies, G['final_norm'],
                                                                     "reduce_start")
        reduce_state.update(send_sems=send_sems, recv_sems=recv_sems, pairs=pairs, lands=lands, widths=widths,
                            copies=copies)
        return zero

    loss_tile, grad_x, G = _local_step(x, mem, target, W, later_weights=lambda after: _gather_weights_wait(flight, after),
                                       early_grads=early_grads)
    loss = lax.psum(loss_tile[0, 0], ("x", "y", "c"))

    rs = reduce_state
    sent, recv = _split_copy_wait(rs['send_sems'], rs['recv_sems'], rs['pairs'], rs['lands'], rs['copies'], grad_x,
                                  "reduce_wait")
    g_mat = _reduce_end(second, sent, recv, rs['widths'], "l1")
    pairs, widths, g_small = _reduce_begin(G, first[:-len(CONVS)], _small_slab(_pack_small({n: G[n] for n in SMALL_NAMES})),
                                           "l0")
    g_mat.update(_reduce_end(first[:-len(CONVS)], pairs, _send_to_chips(pairs, widths), widths, "l0"))
    g_small = _unpack_small(_slab_sum(g_small, "sum_small"), {n: G[n].shape for n in SMALL_NAMES})
    me = 2 * lax.axis_index("x") + lax.axis_index("y")
    for n in CONVS:
        wd = A[n].shape[1]
        g_small[n] = lax.dynamic_slice_in_dim(g_small[n], me * wd, wd, axis=1)
    flat_names = [n for n in SMALL_NAMES if n not in CONVS]

    def pack_flat(prefix):
        return _pack_small_flat({n: A[prefix + n] for n in flat_names}, flat_names)

    shapes = {n: A[n].shape for n in flat_names}
    d_s, m_s, v_s = _adamw(pack_flat(''), _pack_small_flat(g_small, flat_names), pack_flat('m_'), pack_flat('v_'),
                           "adamw_small")
    d_s, m_s, v_s = (_unpack_flat(p, shapes, flat_names) for p in (d_s, m_s, v_s))

    grads, deltas, new_m, new_v = {}, {}, {}, {}
    for n in WEIGHTS:
        if n in MATRICES or n in CONVS:
            grads[n] = g_mat[n] if n in MATRICES else g_small[n]
            deltas[n], new_m[n], new_v[n] = _adamw(A[n], grads[n], A['m_' + n], A['v_' + n], "adamw_" + n)
        else:
            grads[n] = g_small[n].reshape(A[n].shape)
            deltas[n], new_m[n], new_v[n] = d_s[n], m_s[n], v_s[n]
    return (loss, grad_x[None], *[grads[n] for n in WEIGHTS], *[deltas[n] for n in WEIGHTS],
            *[new_m[n] for n in WEIGHTS], *[new_v[n] for n in WEIGHTS])


def _pack_small_flat(vals, names):
    flat = jnp.concatenate([vals[n].astype(F32).reshape(-1) for n in names])
    rows = -(-flat.shape[0] // (8 * LANES)) * 8
    return jnp.pad(flat, (0, rows * LANES - flat.shape[0])).reshape(rows, LANES)


def _unpack_flat(packed, shapes, names):
    flat = packed.reshape(-1)
    out = {}
    off = 0
    for n in names:
        size = int(np.prod(shapes[n]))
        out[n] = flat[off:off + size].reshape(shapes[n])
        off += size
    return out
```

```python
import functools
import math

import numpy as np
import jax
import jax.numpy as jnp
from jax import lax
from jax.experimental import pallas as pl
from jax.experimental.pallas import tpu as pltpu

F32 = jnp.float32
BF16 = jnp.bfloat16
EPS = 1e-6
MESH = pl.DeviceIdType.MESH

ADAM_LR = 0.001
ADAM_B1 = 0.9
ADAM_B2 = 0.999
ADAM_EPS = 1e-08
ADAM_WD = 0.01
ADAM_STEP = 10

VMEM_LIMIT_BYTES = 56 * 1024 * 1024
LANES = 1024

WEIGHTS = ['l0_mix_norm', 'l0_w_in', 'l0_ret_norm', 'l0_s5_lambda_re', 'l0_s5_lambda_im', 'l0_s5_b_re', 'l0_s5_b_im',
           'l0_s5_c_re', 'l0_s5_c_im', 'l0_s5_d', 'l0_s5_log_dt', 'l0_s5_w_glu', 'l0_s5_b_glu', 'l0_w_out',
           'l0_xa_norm', 'l0_mem_norm', 'l0_xa_wq', 'l0_xa_wkv', 'l0_xa_wo', 'l0_ffn_norm', 'l0_ffn_w_up',
           'l0_ffn_conv', 'l0_ffn_w_down', 'l1_mix_norm', 'l1_w_in', 'l1_conv', 'l1_a_log', 'l1_dt_bias',
           'l1_o_norm', 'l1_w_out', 'l1_xa_norm', 'l1_mem_norm', 'l1_xa_wq', 'l1_xa_wkv', 'l1_xa_wo',
           'l1_ffn_norm', 'l1_ffn_w_up', 'l1_ffn_conv', 'l1_ffn_w_down', 'final_norm']
ARG_NAMES = (['x', 'mem'] + WEIGHTS + ['loss_target'] + ['m_' + w for w in WEIGHTS] + ['v_' + w for w in WEIGHTS])

MATRICES = {
    'l0_w_in': 1, 'l0_s5_w_glu': 0, 'l0_w_out': 0, 'l0_xa_wq': 0, 'l0_xa_wkv': 1, 'l0_xa_wo': 0, 'l0_ffn_w_up': 1,
    'l0_ffn_w_down': 0, 'l1_w_in': 1, 'l1_w_out': 0, 'l1_xa_wq': 0, 'l1_xa_wkv': 1, 'l1_xa_wo': 0,
    'l1_ffn_w_up': 1, 'l1_ffn_w_down': 0,
}
CONVS = ('l0_ffn_conv', 'l1_conv', 'l1_ffn_conv')
MATRIX_NAMES = [w for w in WEIGHTS if w in MATRICES]
SMALL_NAMES = [w for w in WEIGHTS if w not in MATRICES]


def _cparams(sem=None):
    return pltpu.CompilerParams(dimension_semantics=sem, vmem_limit_bytes=VMEM_LIMIT_BYTES)


def _pick(n, cands):
    for c in cands:
        if n % c == 0:
            return c
    return n


_NN = ((1,), (0,))
_NT = ((1,), (1,))
_TN = ((0,), (0,))


def _dot(a, b, dims, hi):
    if hi is not None:
        return lax.dot_general(a.astype(F32), b.astype(F32), (dims, ((), ())), precision=hi,
                               preferred_element_type=F32)
    return lax.dot_general(a.astype(BF16), b.astype(BF16), (dims, ((), ())), preferred_element_type=F32)


def _make_mm(hi):
    @jax.custom_vjp
    def nn(a, b):
        return _dot(a, b, _NN, hi)

    def nn_f(a, b):
        return nn(a, b), (a, b)

    def nn_b(r, g):
        a, b = r
        return _dot(g, b, _NT, hi), _dot(a, g, _TN, hi)

    nn.defvjp(nn_f, nn_b)

    @jax.custom_vjp
    def nt(a, b):
        return _dot(a, b, _NT, hi)

    def nt_f(a, b):
        return nt(a, b), (a, b)

    def nt_b(r, g):
        a, b = r
        return _dot(g, b, _NN, hi), _dot(g, a, _TN, hi)

    nt.defvjp(nt_f, nt_b)

    @jax.custom_vjp
    def tn(a, b):
        return _dot(a, b, _TN, hi)

    def tn_f(a, b):
        return tn(a, b), (a, b)

    def tn_b(r, g):
        a, b = r
        return _dot(b, g, _NT, hi), _dot(a, g, _NN, hi)

    tn.defvjp(tn_f, tn_b)
    return nn, nt, tn


mm, mm_nt, mm_tn = _make_mm(None)
mmh, mmh_nt, mmh_tn = _make_mm(lax.Precision.HIGHEST)
mm3, _, _ = _make_mm(lax.Precision.HIGH)


@jax.custom_vjp
def _swap_halves(x):
    return pltpu.roll(x, 64, 1)


def _swap_f(x):
    return pltpu.roll(x, 64, 1), None


def _swap_b(_, g):
    return (pltpu.roll(g, 64, 1),)


_swap_halves.defvjp(_swap_f, _swap_b)


def _silu(x):
    return x * jax.nn.sigmoid(x)


def _rms(x, g):
    return x * lax.rsqrt(jnp.mean(x * x, axis=-1, keepdims=True) + EPS) * g


def _iota(shape, dim):
    return lax.broadcasted_iota(jnp.int32, shape, dim)


def _matmul(a, b, mode="nn", res=None, name="mm", a_cols=None, out_dtype=F32):
    a_off, a_w = (0, a.shape[1]) if a_cols is None else a_cols
    if mode == "nn":
        (M, K), (K2, N) = (a.shape[0], a_w), b.shape
    elif mode == "nt":
        (M, K), (N, K2) = (a.shape[0], a_w), b.shape
    else:
        (K, M), (K2, N) = (a.shape[0], a_w), b.shape
    assert K == K2, (a.shape, b.shape, mode)
    tm = _pick(M, (512, 1408, 256, 128))
    tn = _pick(N, (1024, 1408, 512, 256, 128))
    tk = _pick(K, (1024, 1408, 512, 256, 128))
    nk = K // tk
    dims = {"nn": _NN, "nt": _NT, "tn": _TN}[mode]
    ao = a_off // (tm if mode == "tn" else tk)
    assert ao * (tm if mode == "tn" else tk) == a_off
    if mode == "nn":
        a_spec = pl.BlockSpec((tm, tk), lambda i, j, k: (i, k + ao))
        b_spec = pl.BlockSpec((tk, tn), lambda i, j, k: (k, j))
    elif mode == "nt":
        a_spec = pl.BlockSpec((tm, tk), lambda i, j, k: (i, k + ao))
        b_spec = pl.BlockSpec((tn, tk), lambda i, j, k: (j, k))
    else:
        a_spec = pl.BlockSpec((tk, tm), lambda i, j, k: (k, i + ao))
        b_spec = pl.BlockSpec((tk, tn), lambda i, j, k: (k, j))
    o_spec = pl.BlockSpec((tm, tn), lambda i, j, k: (i, j))
    has_res = res is not None

    def kern(*refs):
        if has_res:
            a_ref, b_ref, r_ref, o_ref, acc_ref = refs
        else:
            a_ref, b_ref, o_ref, acc_ref = refs
        k = pl.program_id(2)
        part = lax.dot_general(a_ref[...].astype(BF16), b_ref[...].astype(BF16), (dims, ((), ())),
                               preferred_element_type=F32)
        if nk == 1:
            o_ref[...] = (part + r_ref[...] if has_res else part).astype(o_ref.dtype)
            return

        @pl.when(k == 0)
        def _():
            acc_ref[...] = part

        @pl.when((k > 0) & (k < nk - 1))
        def _():
            acc_ref[...] += part

        @pl.when(k == nk - 1)
        def _():
            total = acc_ref[...] + part
            o_ref[...] = (total + r_ref[...] if has_res else total).astype(o_ref.dtype)

    in_specs = [a_spec, b_spec] + ([o_spec] if has_res else [])
    ops = (a, b) + ((res,) if has_res else ())
    return pl.pallas_call(
        kern, name=name, grid=(M // tm, N // tn, nk), in_specs=in_specs, out_specs=o_spec,
        out_shape=jax.ShapeDtypeStruct((M, N), out_dtype), scratch_shapes=[pltpu.VMEM((tm, tn), F32)],
        compiler_params=_cparams(("parallel", "parallel", "arbitrary")))(*ops)


def _matmul_cat(pieces, b, mode="nn", res=None, name="mmcat"):
    M = pieces[0].shape[0]
    widths = [p.shape[1] for p in pieces]
    K = sum(widths)
    N = b.shape[1] if mode == "nn" else b.shape[0]
    assert (b.shape[0] if mode == "nn" else b.shape[1]) == K
    tm = _pick(M, (256, 128))
    tn = _pick(N, (1024, 512, 256, 128))
    npc = len(pieces)
    has_res = res is not None
    dims = _NN if mode == "nn" else _NT

    def kern(*refs):
        b_ref = refs[npc]
        o_ref = refs[-1]
        acc = refs[npc + 1][...] if has_res else None
        off = 0
        for p in range(npc):
            bp = b_ref[off:off + widths[p], :] if mode == "nn" else b_ref[:, off:off + widths[p]]
            t = lax.dot_general(refs[p][...].astype(BF16), bp.astype(BF16), (dims, ((), ())),
                                preferred_element_type=F32)
            acc = t if acc is None else acc + t
            off += widths[p]
        o_ref[...] = acc

    in_specs = [pl.BlockSpec((tm, w), lambda j, i: (i, 0)) for w in widths]
    in_specs.append(pl.BlockSpec((K, tn), lambda j, i: (0, j)) if mode == "nn"
                    else pl.BlockSpec((tn, K), lambda j, i: (j, 0)))
    o_spec = pl.BlockSpec((tm, tn), lambda j, i: (i, j))
    if has_res:
        in_specs.append(o_spec)
    ops = list(pieces) + [b] + ([res] if has_res else [])
    return pl.pallas_call(
        kern, name=name, grid=(N // tn, M // tm), in_specs=in_specs, out_specs=o_spec,
        out_shape=jax.ShapeDtypeStruct((M, N), F32), compiler_params=_cparams(("parallel", "parallel")))(*ops)


def _blk(a, width=None, colblk=0):
    return (a, a.shape[1] if width is None else width, colblk)


def _row_specs(blocked, params, ts):
    specs = []
    for (_, w, cb) in blocked:
        specs.append(pl.BlockSpec((ts, w), functools.partial(lambda i, cb: (i, cb), cb=cb)))
    for p in params:
        specs.append(pl.BlockSpec(p.shape, lambda i: (0, 0)))
    return specs


def _rowwise(fn, blocked, params, out_widths, name, ts=256, out_dtypes=None):
    S = blocked[0][0].shape[0]
    ts = min(ts, S)
    nb, npar = len(blocked), len(params)
    out_dtypes = [F32] * len(out_widths) if out_dtypes is None else out_dtypes

    def kern(*refs):
        vals = [r[...] for r in refs[:nb + npar]]
        outs = fn(*vals)
        for o_ref, o in zip(refs[nb + npar:], outs):
            o_ref[...] = o.astype(o_ref.dtype)

    return pl.pallas_call(
        kern, name=name, grid=(S // ts,), in_specs=_row_specs(blocked, params, ts),
        out_specs=[pl.BlockSpec((ts, w), lambda i: (i, 0)) for w in out_widths],
        out_shape=[jax.ShapeDtypeStruct((S, w), d) for w, d in zip(out_widths, out_dtypes)],
        compiler_params=_cparams(("parallel",)))(*[b[0] for b in blocked], *params)


def _rowwise_bwd(fn, blocked, params, cots, name, blocked_grad=None, param_grad=None, adds=None, ts=256,
                 out_dtypes=None):
    S = blocked[0][0].shape[0]
    ts = min(ts, S)
    cots = [c if isinstance(c, tuple) else _blk(c) for c in cots]
    nb, npar, nc = len(blocked), len(params), len(cots)
    blocked_grad = [True] * nb if blocked_grad is None else blocked_grad
    param_grad = [True] * npar if param_grad is None else param_grad
    adds = {} if adds is None else adds
    bidx = [i for i in range(nb) if blocked_grad[i]]
    pidx = [i for i in range(npar) if param_grad[i]]
    add_keys = sorted(adds)
    n_in = nb + npar + nc + len(add_keys)

    def kern(*refs):
        i = pl.program_id(0)
        xs = [r[...] for r in refs[:nb]]
        ps = [r[...] for r in refs[nb:nb + npar]]
        gs = [r[...] for r in refs[nb + npar:nb + npar + nc]]
        add_vals = {k: refs[nb + npar + nc + n][...] for n, k in enumerate(add_keys)}
        outs = refs[n_in:]

        def f(*diff):
            full_x = list(xs)
            full_p = list(ps)
            for n, ix in enumerate(bidx):
                full_x[ix] = diff[n]
            for n, ix in enumerate(pidx):
                full_p[ix] = diff[len(bidx) + n]
            return tuple(fn(*full_x, *full_p))

        _, vjp = jax.vjp(f, *[xs[ix] for ix in bidx], *[ps[ix] for ix in pidx])
        grads = vjp(tuple(gs))
        for n, ix in enumerate(bidx):
            g = grads[n]
            if ix in add_vals:
                g = g + add_vals[ix]
            outs[n][...] = g.astype(outs[n].dtype)
        for n in range(len(pidx)):
            o_ref = outs[len(bidx) + n]

            @pl.when(i == 0)
            def _(o_ref=o_ref):
                o_ref[...] = jnp.zeros_like(o_ref)

            o_ref[...] += grads[len(bidx) + n]

    in_specs = _row_specs(blocked, params, ts)
    in_specs += _row_specs(cots, [], ts)
    in_specs += [pl.BlockSpec((ts, adds[k].shape[1]), lambda i: (i, 0)) for k in add_keys]
    out_specs = [pl.BlockSpec((ts, blocked[ix][1]), lambda i: (i, 0)) for ix in bidx]
    out_specs += [pl.BlockSpec(params[ix].shape, lambda i: (0, 0)) for ix in pidx]
    out_dtypes = [F32] * len(bidx) if out_dtypes is None else out_dtypes
    out_shape = [jax.ShapeDtypeStruct((S, blocked[ix][1]), d) for ix, d in zip(bidx, out_dtypes)]
    out_shape += [jax.ShapeDtypeStruct(params[ix].shape, F32) for ix in pidx]
    return pl.pallas_call(
        kern, name=name, grid=(S // ts,), in_specs=in_specs, out_specs=out_specs, out_shape=out_shape,
        compiler_params=_cparams(("arbitrary",)))(*[b[0] for b in blocked], *params, *[c[0] for c in cots],
                                                    *[adds[k] for k in add_keys])


def _rms_fn(x, g):
    return (_rms(x, g),)


def _head_norm(o, n_heads, dh):
    outs = []
    for h in range(n_heads):
        oh = o[:, h * dh:(h + 1) * dh]
        outs.append(oh * lax.rsqrt(jnp.mean(oh * oh, axis=-1, keepdims=True) + EPS))
    return outs


def _ret_post_fn(o_raw, gate, ret_norm):
    o = jnp.concatenate(_head_norm(o_raw, 4, 128), axis=1)
    return (o * ret_norm * _silu(gate),)


def _s5_post_fn(y1, y2, u, d, w_glu, b_glu):
    y = y1 - y2 + d * u
    y = jax.nn.gelu(y)
    return (y * jax.nn.sigmoid(mm(y, w_glu) + b_glu),)


def _xattn_fn(q, kv):
    outs = []
    for h in range(4):
        qh = q[:, h * 256:(h + 1) * 256]
        kh = kv[:, h * 256:(h + 1) * 256]
        vh = kv[:, 1024 + h * 256:1024 + (h + 1) * 256]
        s = mm_nt(qh, kh) * (256 ** -0.5)
        s = s - lax.stop_gradient(jnp.max(s, axis=-1, keepdims=True))
        p = jnp.exp(s)
        p = p / jnp.sum(p, axis=-1, keepdims=True)
        outs.append(mm(p, vh))
    return (jnp.concatenate(outs, axis=1),)


def _softplus(x):
    return jnp.maximum(x, 0.0) + jnp.log1p(jnp.exp(-jnp.abs(x)))


def _gdn_gates_fn(pt, a_log_p, dtb_p):
    rows, cols = _iota((128, 1024), 0), _iota((128, 1024), 1)
    e_b = (rows == (cols >> 7)).astype(F32)
    e_a = (rows == (cols >> 7) + 8).astype(F32)
    beta = jax.nn.sigmoid(pt)
    g = -(jnp.exp(a_log_p) * _softplus(pt + dtb_p))
    return mmh(g, e_a), mmh(beta, e_b)


def _gdn_post_fn(o_raw, z, o_norm):
    outs = _head_norm(o_raw, 8, 128)
    o = jnp.concatenate([oh * o_norm for oh in outs], axis=1)
    return (o * _silu(z),)


def _ffn_post(up, gate):
    return _silu(gate) * up


def _shift_down(cur, prev8, sh, row8):
    if sh == 0:
        return cur
    r = pltpu.roll(cur, sh, 0)
    p = pltpu.roll(prev8, sh, 0)
    top = jnp.where(row8 < sh, p, r[0:8])
    if cur.shape[0] == 8:
        return top
    return jnp.concatenate([top, r[8:]], axis=0)


def _shift_up(cur, next8, sh, row8):
    if sh == 0:
        return cur
    ts = cur.shape[0]
    r = pltpu.roll(cur, ts - sh, 0)
    p = pltpu.roll(next8, 8 - sh, 0)
    bot = jnp.where(row8 >= 8 - sh, p, r[ts - 8:])
    return jnp.concatenate([r[:ts - 8], bot], axis=0)


def _conv_rows(cur, prev8, wrows, row8):
    k_w = len(wrows)
    out = None
    for j in range(k_w):
        t = _shift_down(cur, prev8, k_w - 1 - j, row8) * wrows[j]
        out = t if out is None else out + t
    return out


def _conv_specs(x, xoff, w, woff, ts, tc):
    r8 = ts // 8
    return [pl.BlockSpec((ts, tc), functools.partial(lambda i, j, o: (i, j + o), o=xoff)),
            pl.BlockSpec((8, tc), functools.partial(lambda i, j, o: (jnp.maximum(i * r8 - 1, 0), j + o), o=xoff)),
            pl.BlockSpec((w.shape[0], tc), functools.partial(lambda i, j, o: (0, j + o), o=woff))]


def _conv_post(srcs, post, ncol, tc, name, cots=None, ts=256, out_dtype=F32):
    S = srcs[0][0].shape[0]
    ns = len(srcs)
    bwd = cots is not None

    def kern(*refs):
        first = pl.program_id(0) == 0
        row8 = _iota((8, tc), 0)
        cs = []
        for s in range(ns):
            cur_ref, prev_ref, w_ref = refs[3 * s:3 * s + 3]
            prev = jnp.where(first, 0.0, prev_ref[...])
            wrows = [w_ref[j:j + 1, :] for j in range(w_ref.shape[0])]
            cs.append(_conv_rows(cur_ref[...], prev, wrows, row8))
        if bwd:
            g = refs[3 * ns][...]
            _, vjp = jax.vjp(lambda *c: post(*c), *cs)
            for o_ref, d in zip(refs[3 * ns + 1:], vjp(g)):
                o_ref[...] = d
        else:
            refs[3 * ns][...] = post(*cs).astype(refs[3 * ns].dtype)

    in_specs = []
    ops = []
    for (x, xoff, w, woff) in srcs:
        in_specs += _conv_specs(x, xoff, w, woff, ts, tc)
        ops += [x, x, w]
    o_spec = pl.BlockSpec((ts, tc), lambda i, j: (i, j))
    o_shape = jax.ShapeDtypeStruct((S, ncol * tc), F32)
    if bwd:
        in_specs.append(o_spec)
        ops.append(cots)
        out_specs, out_shape = [o_spec] * ns, [o_shape] * ns
    else:
        out_specs, out_shape = o_spec, jax.ShapeDtypeStruct((S, ncol * tc), out_dtype)
    return pl.pallas_call(
        kern, name=name, grid=(S // ts, ncol), in_specs=in_specs, out_specs=out_specs, out_shape=out_shape,
        compiler_params=_cparams(("parallel", "parallel")))(*ops)


def _conv_bwd(dc, x, xoff, w, woff, ncol, tc, name, ts=256):
    S = x.shape[0]
    k_w = w.shape[0]
    r8 = ts // 8
    nblk8 = S // 8
    nrow = S // ts

    def kern(dc_ref, dn_ref, x_ref, xp_ref, w_ref, dx_ref, dw_ref):
        i = pl.program_id(1)
        row8 = _iota((8, tc), 0)
        dcur = dc_ref[...]
        dnext = jnp.where(i == nrow - 1, 0.0, dn_ref[...])
        xcur = x_ref[...]
        xprev = jnp.where(i == 0, 0.0, xp_ref[...])

        @pl.when(i == 0)
        def _():
            dw_ref[...] = jnp.zeros_like(dw_ref)

        dx = None
        for j in range(k_w):
            sh = k_w - 1 - j
            wj = w_ref[j:j + 1, :]
            t = _shift_up(dcur, dnext, sh, row8) * wj
            dx = t if dx is None else dx + t
            dw_ref[j:j + 1, :] += jnp.sum(dcur * _shift_down(xcur, xprev, sh, row8), axis=0, keepdims=True)
        dx_ref[...] = dx.astype(dx_ref.dtype)

    in_specs = [pl.BlockSpec((ts, tc), lambda j, i: (i, j)),
                pl.BlockSpec((8, tc), lambda j, i: (jnp.minimum((i + 1) * r8, nblk8 - 1), j)),
                pl.BlockSpec((ts, tc), functools.partial(lambda j, i, o: (i, j + o), o=xoff)),
                pl.BlockSpec((8, tc), functools.partial(lambda j, i, o: (jnp.maximum(i * r8 - 1, 0), j + o), o=xoff)),
                pl.BlockSpec((k_w, tc), functools.partial(lambda j, i, o: (0, j + o), o=woff))]
    out_specs = [pl.BlockSpec((ts, tc), lambda j, i: (i, j)), pl.BlockSpec((k_w, tc), lambda j, i: (0, j))]
    out_shape = [jax.ShapeDtypeStruct((S, ncol * tc), BF16), jax.ShapeDtypeStruct((k_w, ncol * tc), F32)]
    return pl.pallas_call(
        kern, name=name, grid=(ncol, nrow), in_specs=in_specs, out_specs=out_specs, out_shape=out_shape,
        compiler_params=_cparams(("parallel", "arbitrary")))(dc, dc, x, x, w)


def _conv_post_bwd(srcs, post, ncol, tc, cot, name, ts=256):
    S = srcs[0][0].shape[0]
    ns = len(srcs)
    r8 = ts // 8
    nblk8 = S // 8
    nrow = S // ts

    def kern(*refs):
        i = pl.program_id(1)
        row8 = _iota((8, tc), 0)
        g_ref, gn_ref = refs[4 * ns:4 * ns + 2]
        outs = refs[4 * ns + 2:]
        xs, xps, ws, cs, cns = [], [], [], [], []
        for s in range(ns):
            cur_ref, prev_ref, next_ref, w_ref = refs[4 * s:4 * s + 4]
            xcur = cur_ref[...]
            xprev = jnp.where(i == 0, 0.0, prev_ref[...])
            wrows = [w_ref[j:j + 1, :] for j in range(w_ref.shape[0])]
            xs.append(xcur)
            xps.append(xprev)
            ws.append(wrows)
            cs.append(_conv_rows(xcur, xprev, wrows, row8))
            cns.append(_conv_rows(next_ref[...], xcur[ts - 8:], wrows, row8))
        _, vjp = jax.vjp(lambda *c: post(*c), *cs)
        dcs = vjp(g_ref[...])
        _, vjp_next = jax.vjp(lambda *c: post(*c), *cns)
        dcns = vjp_next(jnp.where(i == nrow - 1, 0.0, gn_ref[...]))
        for s in range(ns):
            dx_ref, dw_ref = outs[2 * s], outs[2 * s + 1]

            @pl.when(i == 0)
            def _(dw_ref=dw_ref):
                dw_ref[...] = jnp.zeros_like(dw_ref)

            k_w = len(ws[s])
            dx = None
            for j in range(k_w):
                sh = k_w - 1 - j
                t = _shift_up(dcs[s], dcns[s], sh, row8) * ws[s][j]
                dx = t if dx is None else dx + t
                dw_ref[j:j + 1, :] += jnp.sum(dcs[s] * _shift_down(xs[s], xps[s], sh, row8), axis=0, keepdims=True)
            dx_ref[...] = dx.astype(dx_ref.dtype)

    def nxt(i):
        return jnp.minimum((i + 1) * r8, nblk8 - 1)

    in_specs, ops = [], []
    for (x, xoff, w, woff) in srcs:
        in_specs += [pl.BlockSpec((ts, tc), functools.partial(lambda j, i, o: (i, j + o), o=xoff)),
                     pl.BlockSpec((8, tc), functools.partial(lambda j, i, o: (jnp.maximum(i * r8 - 1, 0), j + o),
                                                             o=xoff)),
                     pl.BlockSpec((8, tc), functools.partial(lambda j, i, o: (nxt(i), j + o), o=xoff)),
                     pl.BlockSpec((w.shape[0], tc), functools.partial(lambda j, i, o: (0, j + o), o=woff))]
        ops += [x, x, x, w]
    in_specs += [pl.BlockSpec((ts, tc), lambda j, i: (i, j)), pl.BlockSpec((8, tc), lambda j, i: (nxt(i), j))]
    ops += [cot, cot]
    out_specs, out_shape = [], []
    for (x, xoff, w, woff) in srcs:
        out_specs += [pl.BlockSpec((ts, tc), lambda j, i: (i, j)), pl.BlockSpec((w.shape[0], tc), lambda j, i: (0, j))]
        out_shape += [jax.ShapeDtypeStruct((S, ncol * tc), BF16), jax.ShapeDtypeStruct((w.shape[0], ncol * tc), F32)]
    return pl.pallas_call(
        kern, name=name, grid=(ncol, nrow), in_specs=in_specs, out_specs=out_specs, out_shape=out_shape,
        compiler_params=_cparams(("parallel", "arbitrary")))(*ops)


def _ret_tables(S):
    H, C, dh = 4, 128, 128
    lg = jnp.log1p(-jnp.exp2(-5.0 - jnp.arange(H, dtype=F32)))
    idx = jnp.arange(C, dtype=F32)
    diff = idx[:, None] - idx[None, :]
    causal = diff >= 0
    intra = jnp.where(causal, jnp.exp(lg[:, None, None] * jnp.where(causal, diff, 0.0)), 0.0)
    kdec = jnp.broadcast_to(jnp.exp(lg[:, None] * (C - 1 - idx))[:, :, None], (H, C, dh))
    qdec = jnp.broadcast_to(jnp.exp(lg[:, None] * (idx + 1))[:, :, None], (H, C, dh))
    cdec = jnp.broadcast_to(jnp.exp(lg * C)[:, None, None], (H, dh, dh))
    half = dh // 2
    inv = jnp.exp(-math.log(10000.0) * jnp.arange(half, dtype=F32) / half)
    ang = jnp.arange(S).astype(F32)[:, None] * inv[None, :]
    cos, sin = jnp.cos(ang), jnp.sin(ang)
    cosf = jnp.concatenate([cos, cos], axis=1)
    sinf = jnp.concatenate([-sin, sin], axis=1)
    return cosf, sinf, intra, kdec, qdec, cdec


def _ret_chunk(q, k, v, cosf, sinf, intra, kdec, qdec, cdec, state):
    hs = range(len(q))
    qr = [q[h] * cosf + _swap_halves(q[h]) * sinf for h in hs]
    kr = [(k[h] * cosf + _swap_halves(k[h]) * sinf) * (128 ** -0.5) for h in hs]
    scores = [mm_nt(qr[h], kr[h]) * intra[h] for h in hs]
    inner = [mm(scores[h], v[h]) for h in hs]
    kv = [mm_tn(kr[h] * kdec[h], v[h]) for h in hs]
    cross = [mm(qr[h] * qdec[h], state[h]) for h in hs]
    return [inner[h] + cross[h] for h in hs], [state[h] * cdec[h] + kv[h] for h in hs]


RET_H = 4


def _ret_call(proj, tabs, states=None, do=None):
    S = proj.shape[0]
    N = S // 128
    bwd = do is not None

    def nn(n):
        return N - 1 - n if bwd else n

    qkv_spec = pl.BlockSpec((128, 3 * 512), lambda n: (nn(n), 0))
    pos = pl.BlockSpec((128, 128), lambda n: (nn(n), 0))
    tab = pl.BlockSpec((RET_H, 128, 128), lambda n: (0, 0, 0))
    st_spec = pl.BlockSpec((None, RET_H, 128, 128), lambda n: (nn(n), 0, 0, 0))
    o_spec = pl.BlockSpec((128, 512), lambda n: (nn(n), 0))

    def kern(*refs):
        x_ref, c_ref, s_ref, i_ref, kd_ref, qd_ref, cd_ref = refs[:7]
        carry = refs[-1]
        heads = range(RET_H)

        @pl.when(pl.program_id(0) == 0)
        def _():
            carry[...] = jnp.zeros_like(carry)

        def cols(ref, off=0):
            return [ref[:, _hs(off + h)] for h in heads]

        def tabs_of(ref):
            return [ref[h] for h in heads]

        consts = (c_ref[...], s_ref[...], tabs_of(i_ref), tabs_of(kd_ref), tabs_of(qd_ref), tabs_of(cd_ref))
        qkv = (cols(x_ref), cols(x_ref, RET_H), cols(x_ref, 2 * RET_H))
        if bwd:
            sp_ref, do_ref = refs[7:9]
            outs = refs[9:12]
            _, vjp = jax.vjp(lambda q, k, v, s: _ret_chunk(q, k, v, *consts, s), *qkv, tabs_of(sp_ref))
            dq, dk, dv, ds = vjp((cols(do_ref), tabs_of(carry)))
            for h in heads:
                for o_ref, d in zip(outs, (dq[h], dk[h], dv[h])):
                    o_ref[:, _hs(h)] = d.astype(o_ref.dtype)
                carry[h] = ds[h]
        else:
            o_ref, sp_ref = refs[7:9]
            state = tabs_of(carry)
            out, new = _ret_chunk(*qkv, *consts, state)
            for h in heads:
                sp_ref[h] = state[h]
                o_ref[:, _hs(h)] = out[h]
                carry[h] = new[h]

    in_specs = [qkv_spec, pos, pos, tab, tab, tab, tab]
    if bwd:
        in_specs += [st_spec, o_spec]
        out_specs = [o_spec] * 3
        out_shape = [jax.ShapeDtypeStruct((S, 512), BF16)] * 3
        ops = (proj, *tabs, states, do)
    else:
        out_specs = [o_spec, st_spec]
        out_shape = [jax.ShapeDtypeStruct((S, 512), F32), jax.ShapeDtypeStruct((N, RET_H, 128, 128), F32)]
        ops = (proj, *tabs)
    return pl.pallas_call(
        kern, name="ret_bwd" if bwd else "ret_fwd", grid=(N,), in_specs=in_specs, out_specs=out_specs,
        out_shape=out_shape, scratch_shapes=[pltpu.VMEM((RET_H, 128, 128), F32)],
        compiler_params=_cparams(("arbitrary",)))(*ops)


GDN_C = 64
GDN_H = 8


def _unit_lower_inverse(a_mats, eye):
    p = [-a for a in a_mats]
    t = [eye + x for x in p]
    for _ in range(5):
        p = [mm3(x, x) for x in p]
        t = [mm3(y, eye + x) for y, x in zip(t, p)]
    return t


@jax.custom_vjp
def _known_inverse(a_mat, t_mat):
    return t_mat


def _known_inverse_f(a_mat, t_mat):
    return t_mat, t_mat


def _known_inverse_b(t_mat, g):
    return -mmh_tn(t_mat, mmh_nt(g, t_mat)), jnp.zeros_like(t_mat)


_known_inverse.defvjp(_known_inverse_f, _known_inverse_b)


def _gdn_intra(q, k, v, g_b, beta_b, t_known=None):
    c = GDN_C
    hs = range(len(q))
    q = [x * lax.rsqrt(jnp.sum(x * x, axis=-1, keepdims=True) + EPS) * (128 ** -0.5) for x in q]
    k = [x * lax.rsqrt(jnp.sum(x * x, axis=-1, keepdims=True) + EPS) for x in k]
    ri, ci = _iota((c, c), 0), _iota((c, c), 1)
    incl = ri >= ci
    strict = ri > ci
    eye = (ri == ci).astype(F32)
    lower = incl.astype(F32)
    ones = jnp.ones((c, c), F32)
    gc_b = [mmh(lower, g) for g in g_b]
    gl_b = [mmh(ones, g) for g in g_b]
    kb = [k[h] * beta_b[h] for h in hs]
    vb = [v[h] * beta_b[h] for h in hs]
    gcc = [g[:, :c] for g in gc_b]
    decay = [jnp.where(incl, jnp.exp(jnp.where(incl, g - g.T, 0.0)), 0.0) for g in gcc]
    a_mat = [jnp.where(strict, mm_nt(kb[h], k[h]) * decay[h], 0.0) for h in hs]
    if t_known is None:
        t_mat = _unit_lower_inverse(a_mat, eye)
    else:
        t_mat = [_known_inverse(a_mat[h], t_known[h]) for h in hs]
    egc = [jnp.exp(g) for g in gc_b]
    w = [mm(t_mat[h], kb[h] * egc[h]) for h in hs]
    u = [mm(t_mat[h], vb[h]) for h in hs]
    qk = [jnp.where(incl, mm_nt(q[h], k[h]) * decay[h], 0.0) for h in hs]
    q_dec = [q[h] * egc[h] for h in hs]
    k_dec = [k[h] * jnp.exp(gl_b[h] - gc_b[h]) for h in hs]
    return w, u, q_dec, k_dec, qk, t_mat


def _gdn_step(w, u, q_dec, k_dec, qk, g_b, state):
    hs = range(len(w))
    ones = jnp.ones((128, GDN_C), F32)
    gl_s = [mmh(ones, g) for g in g_b]
    ws = [mm(w[h], state[h]) for h in hs]
    qs = [mm(q_dec[h], state[h]) for h in hs]
    v_new = [u[h] - ws[h] for h in hs]
    o = [qs[h] + mm(qk[h], v_new[h]) for h in hs]
    new = [state[h] * jnp.exp(gl_s[h]) + mm_tn(k_dec[h], v_new[h]) for h in hs]
    return o, new


def _hs(h):
    return slice(h * 128, (h + 1) * 128)


def _gdn_intra_call(qkv, g_e, beta_e, cots=None):
    S = qkv.shape[0]
    N = S // GDN_C
    bwd = cots is not None
    row = pl.BlockSpec((GDN_C, 1024), lambda n: (n, 0))
    qkv_spec = pl.BlockSpec((GDN_C, 3072), lambda n: (n, 0))
    qk_spec = pl.BlockSpec((GDN_H, GDN_C, GDN_C), lambda n: (0, n, 0))

    def kern(*refs):
        x_ref, g_ref, b_ref = refs[:3]
        heads = range(GDN_H)

        def cols(ref, off=0):
            return [ref[:, _hs(off + h)] for h in heads]

        args = (cols(x_ref), cols(x_ref, 8), cols(x_ref, 16), cols(g_ref), cols(b_ref))
        if bwd:
            dw_ref, du_ref, dqd_ref, dkd_ref, dqk_ref, dgadd_ref, t_ref = refs[3:10]
            outs = refs[10:]
            t_known = [t_ref[h] for h in heads]
            _, vjp = jax.vjp(lambda *a: _gdn_intra(*a, t_known=t_known)[:5], *args)
            dq, dk, dv, dg, db = vjp((cols(dw_ref), cols(du_ref), cols(dqd_ref), cols(dkd_ref),
                                      [dqk_ref[h] for h in heads]))
            dgadd = cols(dgadd_ref)
            for h in heads:
                for o_ref, d in zip(outs, (dq[h], dk[h], dv[h], dg[h] + dgadd[h], db[h])):
                    o_ref[:, _hs(h)] = d
        else:
            w, u, qd, kd, qk, t_mat = _gdn_intra(*args)
            for h in heads:
                for o_ref, o in zip(refs[3:7], (w[h], u[h], qd[h], kd[h])):
                    o_ref[:, _hs(h)] = o
                refs[7][h] = qk[h]
                refs[8][h] = t_mat[h]

    big = jax.ShapeDtypeStruct((S, 1024), F32)
    sq = jax.ShapeDtypeStruct((GDN_H, S, GDN_C), F32)
    if bwd:
        in_specs = [qkv_spec, row, row, row, row, row, row, qk_spec, row, qk_spec]
        out_specs, out_shape = [row] * 5, [big] * 5
        ops = (qkv, g_e, beta_e) + tuple(cots)
    else:
        in_specs = [qkv_spec, row, row]
        out_specs = [row] * 4 + [qk_spec, qk_spec]
        out_shape = [big] * 4 + [sq, sq]
        ops = (qkv, g_e, beta_e)
    return pl.pallas_call(
        kern, name="gdn_intra_bwd" if bwd else "gdn_intra", grid=(N,), in_specs=in_specs, out_specs=out_specs,
        out_shape=out_shape, compiler_params=_cparams(("parallel",)))(*ops)


def _gdn_pass(w, u, qd, kd, qk, g_e, states=None, do=None):
    S = w.shape[0]
    N = S // GDN_C
    bwd = do is not None

    def nn(n):
        return N - 1 - n if bwd else n

    row = pl.BlockSpec((GDN_C, 1024), lambda n: (nn(n), 0))
    qk_spec = pl.BlockSpec((GDN_H, GDN_C, GDN_C), lambda n: (0, nn(n), 0))
    st_spec = pl.BlockSpec((None, GDN_H, 128, 128), lambda n: (nn(n), 0, 0, 0))

    def kern(*refs):
        w_ref, u_ref, qd_ref, kd_ref, qk_ref, g_ref = refs[:6]
        carry = refs[-1]

        @pl.when(pl.program_id(0) == 0)
        def _():
            carry[...] = jnp.zeros_like(carry)

        heads = range(GDN_H)

        def cols(ref):
            return [ref[:, _hs(h)] for h in heads]

        args = (cols(w_ref), cols(u_ref), cols(qd_ref), cols(kd_ref), [qk_ref[h] for h in heads], cols(g_ref))
        if bwd:
            sp_ref, do_ref = refs[6:8]
            outs = refs[8:14]
            _, vjp = jax.vjp(_gdn_step, *args, [sp_ref[h] for h in heads])
            dw, du, dqd, dkd, dqk, dg, ds = vjp((cols(do_ref), [carry[h] for h in heads]))
            for h in heads:
                for o_ref, d in zip(outs[:4], (dw[h], du[h], dqd[h], dkd[h])):
                    o_ref[:, _hs(h)] = d
                outs[4][h] = dqk[h]
                outs[5][:, _hs(h)] = dg[h]
                carry[h] = ds[h]
        else:
            o_ref, sp_ref = refs[6:8]
            state = [carry[h] for h in heads]
            o, new = _gdn_step(*args, state)
            for h in heads:
                sp_ref[h] = state[h]
                o_ref[:, _hs(h)] = o[h]
                carry[h] = new[h]

    big = jax.ShapeDtypeStruct((S, 1024), F32)
    in_specs = [row, row, row, row, qk_spec, row]
    if bwd:
        in_specs += [st_spec, row]
        out_specs = [row] * 4 + [qk_spec, row]
        out_shape = [big] * 4 + [jax.ShapeDtypeStruct((GDN_H, S, GDN_C), F32), big]
        ops = (w, u, qd, kd, qk, g_e, states, do)
    else:
        out_specs = [row, st_spec]
        out_shape = [big, jax.ShapeDtypeStruct((N, GDN_H, 128, 128), F32)]
        ops = (w, u, qd, kd, qk, g_e)
    return pl.pallas_call(
        kern, name="gdn_pass_bwd" if bwd else "gdn_pass", grid=(N,), in_specs=in_specs, out_specs=out_specs,
        out_shape=out_shape, scratch_shapes=[pltpu.VMEM((GDN_H, 128, 128), F32)],
        compiler_params=_cparams(("arbitrary",)))(*ops)


def _s5_prep_fn(lr, li, ldt, br, bi, cr, ci):
    dt = jnp.exp(ldt)
    mag = jnp.exp(lr * dt)
    a_re = mag * jnp.cos(li * dt)
    a_im = mag * jnp.sin(li * dt)
    den = lr * lr + li * li
    z_re = ((a_re - 1.0) * lr + a_im * li) / den
    z_im = (a_im * lr - (a_re - 1.0) * li) / den
    e1 = ((_iota((512, 32), 0) >> 4) == _iota((512, 32), 1)).astype(F32)
    zr_e = mmh(e1, z_re)
    zi_e = mmh(e1, z_im)
    bb_re = zr_e * br - zi_e * bi
    bb_im = zr_e * bi + zi_e * br
    t1 = ((_iota((64, 2048), 1) & 63) == _iota((64, 2048), 0)).astype(F32)
    m1 = (_iota((512, 2048), 0) >> 4) == (_iota((512, 2048), 1) >> 6)
    bd_re = jnp.where(m1, mmh(bb_re, t1), 0.0)
    bd_im = jnp.where(m1, mmh(bb_im, t1), 0.0)
    t2 = ((_iota((16, 512), 1) & 15) == _iota((16, 512), 0)).astype(F32)
    m2 = (_iota((2048, 512), 0) >> 6) == (_iota((2048, 512), 1) >> 4)
    cd_re = jnp.where(m2, mmh(cr, t2), 0.0)
    cd_im = jnp.where(m2, mmh(ci, t2), 0.0)
    return a_re, a_im, bd_re, bd_im, cd_re, cd_im


_PREP_OUT = [(32, 64), (32, 64), (512, 2048), (512, 2048), (2048, 512), (2048, 512)]


def _s5_prep(params, cots=None):
    bwd = cots is not None

    def kern(*refs):
        vals = [r[...] for r in refs[:7]]
        if bwd:
            gs = tuple(r[...] for r in refs[7:13])
            _, vjp = jax.vjp(_s5_prep_fn, *vals)
            for o_ref, d in zip(refs[13:], vjp(gs)):
                o_ref[...] = d
        else:
            for o_ref, o in zip(refs[7:], _s5_prep_fn(*vals)):
                o_ref[...] = o

    if bwd:
        out_shape = [jax.ShapeDtypeStruct(p.shape, F32) for p in params]
        ops = list(params) + list(cots)
    else:
        out_shape = [jax.ShapeDtypeStruct(s, F32) for s in _PREP_OUT]
        ops = list(params)
    return pl.pallas_call(kern, name="s5_prep_bwd" if bwd else "s5_prep", out_shape=out_shape,
                          compiler_params=_cparams())(*ops)


def _cmul(ar, ai, br, bi):
    return ar * br - ai * bi, ar * bi + ai * br


def _power_table(ar, ai, row8, descending):
    pr, pi = ar, ai
    tr = jnp.zeros(row8.shape, F32)
    ti = jnp.zeros(row8.shape, F32)
    for n in range(8):
        r = 7 - n if descending else n
        tr = jnp.where(row8 == r, pr, tr)
        ti = jnp.where(row8 == r, pi, ti)
        if n < 7:
            pr, pi = _cmul(pr, pi, ar, ai)
    return tr, ti


def _tile_scan(xr, xi, pows, row8, up):
    for d, (pr, pi) in zip((1, 2, 4), pows):
        if up:
            sr = jnp.where(row8 < 8 - d, pltpu.roll(xr, 8 - d, 0), 0.0)
            si = jnp.where(row8 < 8 - d, pltpu.roll(xi, 8 - d, 0), 0.0)
        else:
            sr = jnp.where(row8 >= d, pltpu.roll(xr, d, 0), 0.0)
            si = jnp.where(row8 >= d, pltpu.roll(xi, d, 0), 0.0)
        mr, mi = _cmul(pr, pi, sr, si)
        xr, xi = xr + mr, xi + mi
    return xr, xi


def _pick_row(x, row8, r):
    return jnp.sum(jnp.where(row8 == r, x, 0.0), axis=0, keepdims=True)


SCAN_LB = 512
SCAN_TS = 512


def _scan_fwd(bu_re, bu_im, a_re, a_im):
    S, L = bu_re.shape
    ts, lb = min(SCAN_TS, S), SCAN_LB
    nt = ts // 8

    def kern(br_ref, bi_ref, ar_ref, ai_ref, or_ref, oi_ref, cr_ref, ci_ref):
        @pl.when(pl.program_id(1) == 0)
        def _():
            cr_ref[...] = jnp.zeros_like(cr_ref)
            ci_ref[...] = jnp.zeros_like(ci_ref)

        row8 = _iota((8, lb), 0)
        ar, ai = ar_ref[...], ai_ref[...]
        a2 = _cmul(ar, ai, ar, ai)
        a4 = _cmul(*a2, *a2)
        pows = ((ar, ai), a2, a4)
        tr, ti = _power_table(ar, ai, row8, False)

        def body(i, carry):
            cr, ci = carry
            off = pl.multiple_of(i * 8, 8)
            xr, xi = _tile_scan(br_ref[pl.ds(off, 8), :], bi_ref[pl.ds(off, 8), :], pows, row8, False)
            mr, mi = _cmul(tr, ti, cr, ci)
            xr, xi = xr + mr, xi + mi
            or_ref[pl.ds(off, 8), :] = xr
            oi_ref[pl.ds(off, 8), :] = xi
            return _pick_row(xr, row8, 7), _pick_row(xi, row8, 7)

        cr, ci = lax.fori_loop(0, nt, body, (cr_ref[...], ci_ref[...]))
        cr_ref[...] = cr
        ci_ref[...] = ci

    blk = pl.BlockSpec((ts, lb), lambda j, i: (i, j))
    par = pl.BlockSpec((1, lb), lambda j, i: (0, j))
    return pl.pallas_call(
        kern, name="s5_scan_fwd", grid=(L // lb, S // ts), in_specs=[blk, blk, par, par], out_specs=[blk, blk],
        out_shape=[jax.ShapeDtypeStruct((S, L), F32)] * 2,
        scratch_shapes=[pltpu.VMEM((1, lb), F32), pltpu.VMEM((1, lb), F32)],
        compiler_params=_cparams(("parallel", "arbitrary")))(bu_re, bu_im, a_re, a_im)


def _scan_bwd(dst_re, dst_im, st_re, st_im, a_re, a_im):
    S, L = dst_re.shape
    ts, lb = min(SCAN_TS, S), SCAN_LB
    nt = ts // 8
    nb = S // ts
    r8 = ts // 8

    def kern(dr_ref, di_ref, sr_ref, si_ref, pr_ref, pi_ref, ar_ref, ai_ref, gr_ref, gi_ref, dar_ref, dai_ref,
             cr_ref, ci_ref):
        step = pl.program_id(1)
        blk = nb - 1 - step

        @pl.when(step == 0)
        def _():
            cr_ref[...] = jnp.zeros_like(cr_ref)
            ci_ref[...] = jnp.zeros_like(ci_ref)
            dar_ref[...] = jnp.zeros_like(dar_ref)
            dai_ref[...] = jnp.zeros_like(dai_ref)

        row8 = _iota((8, lb), 0)
        ar, ai = ar_ref[...], ai_ref[...]
        nai = -ai
        a2 = _cmul(ar, nai, ar, nai)
        a4 = _cmul(*a2, *a2)
        pows = ((ar, nai), a2, a4)
        tr, ti = _power_table(ar, nai, row8, True)
        halo_r = jnp.where(blk == 0, 0.0, pr_ref[...])
        halo_i = jnp.where(blk == 0, 0.0, pi_ref[...])

        def body(n, carry):
            cr, ci, acc_r, acc_i = carry
            i = nt - 1 - n
            off = pl.multiple_of(i * 8, 8)
            gr, gi = _tile_scan(dr_ref[pl.ds(off, 8), :], di_ref[pl.ds(off, 8), :], pows, row8, True)
            mr, mi = _cmul(tr, ti, cr, ci)
            gr, gi = gr + mr, gi + mi
            gr_ref[pl.ds(off, 8), :] = gr
            gi_ref[pl.ds(off, 8), :] = gi
            poff = pl.multiple_of(jnp.maximum(i - 1, 0) * 8, 8)
            before_r = jnp.where(i == 0, halo_r, sr_ref[pl.ds(poff, 8), :])
            before_i = jnp.where(i == 0, halo_i, si_ref[pl.ds(poff, 8), :])
            last_r = _pick_row(before_r, row8, 7)
            last_i = _pick_row(before_i, row8, 7)
            spr = jnp.where(row8 >= 1, pltpu.roll(sr_ref[pl.ds(off, 8), :], 1, 0), last_r)
            spi = jnp.where(row8 >= 1, pltpu.roll(si_ref[pl.ds(off, 8), :], 1, 0), last_i)
            acc_r = acc_r + gr * spr + gi * spi
            acc_i = acc_i + gi * spr - gr * spi
            return _pick_row(gr, row8, 0), _pick_row(gi, row8, 0), acc_r, acc_i

        zero = jnp.zeros((8, lb), F32)
        cr, ci, acc_r, acc_i = lax.fori_loop(0, nt, body, (cr_ref[...], ci_ref[...], zero, zero))
        cr_ref[...] = cr
        ci_ref[...] = ci
        dar_ref[...] += jnp.sum(acc_r, axis=0, keepdims=True)
        dai_ref[...] += jnp.sum(acc_i, axis=0, keepdims=True)

    blk = pl.BlockSpec((ts, lb), lambda j, i: (nb - 1 - i, j))
    halo = pl.BlockSpec((8, lb), lambda j, i: (jnp.maximum((nb - 1 - i) * r8 - 1, 0), j))
    par = pl.BlockSpec((1, lb), lambda j, i: (0, j))
    return pl.pallas_call(
        kern, name="s5_scan_bwd", grid=(L // lb, nb), in_specs=[blk, blk, blk, blk, halo, halo, par, par],
        out_specs=[blk, blk, par, par],
        out_shape=[jax.ShapeDtypeStruct((S, L), F32)] * 2 + [jax.ShapeDtypeStruct((1, L), F32)] * 2,
        scratch_shapes=[pltpu.VMEM((1, lb), F32), pltpu.VMEM((1, lb), F32)],
        compiler_params=_cparams(("parallel", "arbitrary")))(dst_re, dst_im, st_re, st_im, st_re, st_im, a_re, a_im)


def _loss_grad(x, target, gain, ts=256):
    S, D = x.shape

    def kern(x_ref, t_ref, g_ref, loss_ref, dx_ref, dg_ref):
        i = pl.program_id(0)
        tgt = t_ref[...]

        def f(xv, gv):
            err = _rms(xv, gv) - tgt
            return 0.5 * jnp.mean(err * err, axis=-1, keepdims=True)

        rowloss, vjp = jax.vjp(f, x_ref[...], g_ref[...])
        dx, dg = vjp(jnp.ones_like(rowloss))
        dx_ref[...] = dx

        @pl.when(i == 0)
        def _():
            loss_ref[...] = jnp.zeros_like(loss_ref)
            dg_ref[...] = jnp.zeros_like(dg_ref)

        loss_ref[...] += jnp.broadcast_to(jnp.sum(rowloss, axis=0, keepdims=True), loss_ref.shape)
        dg_ref[...] += dg

    row = pl.BlockSpec((ts, D), lambda i: (i, 0))
    return pl.pallas_call(
        kern, name="loss_grad", grid=(S // ts,), in_specs=[row, row, pl.BlockSpec((1, D), lambda i: (0, 0))],
        out_specs=[pl.BlockSpec((8, 128), lambda i: (0, 0)), row, pl.BlockSpec((1, D), lambda i: (0, 0))],
        out_shape=[jax.ShapeDtypeStruct((8, 128), F32), jax.ShapeDtypeStruct((S, D), F32),
                   jax.ShapeDtypeStruct((1, D), F32)],
        compiler_params=_cparams(("arbitrary",)))(x, target, gain)


def _rms_fwd(x, g, name):
    return _rowwise(_rms_fn, [_blk(x)], [g], [x.shape[1]], name, out_dtypes=[BF16])[0]


def _rms_bwd(x, g, dy, name, add=None):
    return _rowwise_bwd(_rms_fn, [_blk(x)], [g], [dy], name, adds=None if add is None else {0: add})


FFN_TC = 1408


def _common_fwd(x, mem, P, L):
    hx = _rms_fwd(x, P['xa_norm'], L + "xa_norm")
    q = _matmul(hx, P['xa_wq'], name=L + "xa_q")
    memn = _rms_fwd(mem, P['mem_norm'], L + "mem_norm")
    kv = _matmul(memn, P['xa_wkv'], name=L + "xa_kv")
    att = _rowwise(_xattn_fn, [_blk(q)], [kv], [1024], L + "xattn", out_dtypes=[BF16])[0]
    x2 = _matmul(att, P['xa_wo'], res=x, name=L + "xa_o")
    hf = _rms_fwd(x2, P['ffn_norm'], L + "ffn_norm")
    hu = _matmul(hf, P['ffn_w_up'], name=L + "ffn_up")
    cw = P['ffn_conv']
    act = _conv_post([(hu, 0, cw, 0), (hu, 2, cw, 2)], _ffn_post, 2, FFN_TC, L + "ffn_conv", out_dtype=BF16)
    x3 = _matmul(act, P['ffn_w_down'], res=x2, name=L + "ffn_down")
    return x3, (x, mem, hx, q, memn, kv, att, x2, hf, hu, act)


def _common_bwd(saved, dx3, P, L):
    x, mem, hx, q, memn, kv, att, x2, hf, hu, act = saved
    G = {}
    dact = _matmul(dx3, P['ffn_w_down'], "nt", name=L + "ffn_down_dx")
    G['ffn_w_down'] = _matmul(act, dx3, "tn", name=L + "ffn_down_dw")
    cw = P['ffn_conv']
    dhu_u, dcw_u, dhu_g, dcw_g = _conv_post_bwd([(hu, 0, cw, 0), (hu, 2, cw, 2)], _ffn_post, 2, FFN_TC, dact,
                                                L + "ffn_conv_bwd")
    G['ffn_conv'] = jnp.concatenate([dcw_u, dcw_g], axis=1)
    dhf = _matmul_cat([dhu_u, dhu_g], P['ffn_w_up'], "nt", name=L + "ffn_up_dx")
    G['ffn_w_up'] = jnp.concatenate([_matmul(hf, dhu_u, "tn", name=L + "ffn_up_dw_up"),
                                     _matmul(hf, dhu_g, "tn", name=L + "ffn_up_dw_gate")], axis=1)
    dx2, G['ffn_norm'] = _rms_bwd(x2, P['ffn_norm'], dhf, L + "ffn_norm_bwd", add=dx3)
    datt = _matmul(dx2, P['xa_wo'], "nt", name=L + "xa_o_dx")
    G['xa_wo'] = _matmul(att, dx2, "tn", name=L + "xa_o_dw")
    dq, dkv = _rowwise_bwd(_xattn_fn, [_blk(q)], [kv], [datt], L + "xattn_bwd", out_dtypes=[BF16])
    dhx = _matmul(dq, P['xa_wq'], "nt", name=L + "xa_q_dx")
    G['xa_wq'] = _matmul(hx, dq, "tn", name=L + "xa_q_dw")
    dmemn = _matmul(dkv, P['xa_wkv'], "nt", name=L + "xa_kv_dx")
    G['xa_wkv'] = _matmul(memn, dkv, "tn", name=L + "xa_kv_dw")
    _, G['mem_norm'] = _rms_bwd(mem, P['mem_norm'], dmemn, L + "mem_norm_bwd")
    dx, G['xa_norm'] = _rms_bwd(x, P['xa_norm'], dhx, L + "xa_norm_bwd", add=dx2)
    return dx, G


U_COLS = (2048, 512)


def _even_fwd(x, P):
    S = x.shape[0]
    h0 = _rms_fwd(x, P['mix_norm'], "l0_mix_norm")
    proj = _matmul(h0, P['w_in'], name="l0_in")
    tabs = _ret_tables(S)
    o_raw, rstates = _ret_call(proj, tabs)
    o = _rowwise(_ret_post_fn, [_blk(o_raw), _blk(proj, 512, 3)], [P['ret_norm']], [512], "l0_ret_post",
                 out_dtypes=[BF16])[0]
    prep_in = (P['s5_lambda_re'], P['s5_lambda_im'], P['s5_log_dt'], P['s5_b_re'], P['s5_b_im'], P['s5_c_re'],
               P['s5_c_im'])
    a_re, a_im, bd_re, bd_im, cd_re, cd_im = _s5_prep(prep_in)
    a_re_f, a_im_f = a_re.reshape(1, 2048), a_im.reshape(1, 2048)
    bu_re = _matmul(proj, bd_re, name="l0_s5_bu_re", a_cols=U_COLS)
    bu_im = _matmul(proj, bd_im, name="l0_s5_bu_im", a_cols=U_COLS)
    st_re, st_im = _scan_fwd(bu_re, bu_im, a_re_f, a_im_f)
    y1 = _matmul(st_re, cd_re, name="l0_s5_y_re")
    y2 = _matmul(st_im, cd_im, name="l0_s5_y_im")
    yg = _rowwise(_s5_post_fn, [_blk(y1), _blk(y2), _blk(proj, 512, 4)],
                  [P['s5_d'], P['s5_w_glu'], P['s5_b_glu']], [512], "l0_s5_post", out_dtypes=[BF16])[0]
    x1 = _matmul_cat([o, yg], P['w_out'], "nn", res=x, name="l0_out")
    saved = (x, h0, proj, tabs, o_raw, rstates, prep_in, a_re_f, a_im_f, bd_re, bd_im, cd_re, cd_im, st_re, st_im,
             y1, y2, o, yg)
    return x1, saved


def _even_bwd(saved, dx1, P):
    (x, h0, proj, tabs, o_raw, rstates, prep_in, a_re_f, a_im_f, bd_re, bd_im, cd_re, cd_im, st_re, st_im, y1, y2,
     o, yg) = saved
    G = {}
    dmerged = _matmul(dx1, P['w_out'], "nt", name="l0_out_dx")
    G['w_out'] = jnp.concatenate([_matmul(o, dx1, "tn", name="l0_out_dw_ret"),
                                  _matmul(yg, dx1, "tn", name="l0_out_dw_s5")], axis=0)
    do_raw, dgate, G['ret_norm'] = _rowwise_bwd(
        _ret_post_fn, [_blk(o_raw), _blk(proj, 512, 3)], [P['ret_norm']], [_blk(dmerged, 512, 0)], "l0_ret_post_bwd",
        out_dtypes=[F32, BF16])
    dq, dk, dv = _ret_call(proj, tabs, states=rstates, do=do_raw)
    dy1, dy2, du_a, G['s5_d'], G['s5_w_glu'], G['s5_b_glu'] = _rowwise_bwd(
        _s5_post_fn, [_blk(y1), _blk(y2), _blk(proj, 512, 4)], [P['s5_d'], P['s5_w_glu'], P['s5_b_glu']],
        [_blk(dmerged, 512, 1)], "l0_s5_post_bwd", out_dtypes=[BF16, BF16, F32])
    dst_re = _matmul(dy1, cd_re, "nt", name="l0_s5_y_re_dx")
    dcd_re = _matmul(st_re, dy1, "tn", name="l0_s5_y_re_dw")
    dst_im = _matmul(dy2, cd_im, "nt", name="l0_s5_y_im_dx")
    dcd_im = _matmul(st_im, dy2, "tn", name="l0_s5_y_im_dw")
    dbu_re, dbu_im, da_re, da_im = _scan_bwd(dst_re, dst_im, st_re, st_im, a_re_f, a_im_f)
    du = _matmul(dbu_re, bd_re, "nt", res=du_a, name="l0_s5_bu_re_dx")
    du = _matmul(dbu_im, bd_im, "nt", res=du, name="l0_s5_bu_im_dx", out_dtype=BF16)
    dbd_re = _matmul(proj, dbu_re, "tn", name="l0_s5_bu_re_dw", a_cols=U_COLS)
    dbd_im = _matmul(proj, dbu_im, "tn", name="l0_s5_bu_im_dw", a_cols=U_COLS)
    dprep = _s5_prep(prep_in, cots=(da_re.reshape(32, 64), da_im.reshape(32, 64), dbd_re, dbd_im, dcd_re, dcd_im))
    for n, d in zip(('s5_lambda_re', 's5_lambda_im', 's5_log_dt', 's5_b_re', 's5_b_im', 's5_c_re', 's5_c_im'), dprep):
        G[n] = d
    pieces = [dq, dk, dv, dgate, du]
    dh0 = _matmul_cat(pieces, P['w_in'], "nt", name="l0_in_dx")
    G['w_in'] = jnp.concatenate([_matmul(h0, p, "tn", name="l0_in_dw_%d" % n) for n, p in enumerate(pieces)], axis=1)
    dx, G['mix_norm'] = _rms_bwd(x, P['mix_norm'], dh0, "l0_mix_norm_bwd", add=dx1)
    return dx, G


def _odd_fwd(x, P):
    h1 = _rms_fwd(x, P['mix_norm'], "l1_mix_norm")
    pm = _matmul(h1, P['w_main'], name="l1_in_main")
    pt = _matmul(h1, P['w_tail'], name="l1_in_tail")
    qkv = _conv_post([(pm, 0, P['conv'], 0)], _silu, 3, 1024, "l1_conv")
    g_e, beta_e = _rowwise(_gdn_gates_fn, [_blk(pt)], [P['a_log_p'], P['dtb_p']], [1024, 1024], "l1_gdn_gates")
    w, u, qd, kd, qk, tinv = _gdn_intra_call(qkv, g_e, beta_e)
    o_raw, gstates = _gdn_pass(w, u, qd, kd, qk, g_e)
    og = _rowwise(_gdn_post_fn, [_blk(o_raw), _blk(pm, 1024, 3)], [P['o_norm']], [1024], "l1_gdn_post",
                  out_dtypes=[BF16])[0]
    x1 = _matmul(og, P['w_out'], res=x, name="l1_out")
    return x1, (x, h1, pm, pt, qkv, g_e, beta_e, w, u, qd, kd, qk, tinv, o_raw, gstates, og)


def _odd_bwd(saved, dx1, P):
    x, h1, pm, pt, qkv, g_e, beta_e, w, u, qd, kd, qk, tinv, o_raw, gstates, og = saved
    G = {}
    dog = _matmul(dx1, P['w_out'], "nt", name="l1_out_dx")
    G['w_out'] = _matmul(og, dx1, "tn", name="l1_out_dw")
    do_raw, dz, G['o_norm'] = _rowwise_bwd(_gdn_post_fn, [_blk(o_raw), _blk(pm, 1024, 3)], [P['o_norm']], [dog],
                                           "l1_gdn_post_bwd", out_dtypes=[F32, BF16])
    dw, du, dqd, dkd, dqk, dg_pass = _gdn_pass(w, u, qd, kd, qk, g_e, states=gstates, do=do_raw)
    dqkv = _gdn_intra_call(qkv, g_e, beta_e, cots=(dw, du, dqd, dkd, dqk, dg_pass, tinv))
    dg_e, dbeta_e = dqkv[3], dqkv[4]
    dpt, G['a_log_p'], G['dtb_p'] = _rowwise_bwd(_gdn_gates_fn, [_blk(pt)], [P['a_log_p'], P['dtb_p']],
                                                 [dg_e, dbeta_e], "l1_gdn_gates_bwd", out_dtypes=[BF16])
    pieces, dcw = [], []
    for part in range(3):
        dxp, dwp = _conv_post_bwd([(pm, part, P['conv'], part)], _silu, 1, 1024, dqkv[part],
                                  "l1_conv_bwd_%d" % part)
        pieces.append(dxp)
        dcw.append(dwp)
    G['conv'] = jnp.concatenate(dcw, axis=1)
    pieces += [dz, dpt]
    dh1 = _matmul_cat(pieces, P['w_all'], "nt", name="l1_in_dx")
    G['w_all'] = jnp.concatenate([_matmul(h1, p, "tn", name="l1_in_dw_%d" % n) for n, p in enumerate(pieces)], axis=1)
    dx, G['mix_norm'] = _rms_bwd(x, P['mix_norm'], dh1, "l1_mix_norm_bwd", add=dx1)
    return dx, G


def _row(v):
    return v.reshape(1, -1)


def _local_step(x, mem, target, W, later_weights=None, early_grads=None):
    P0 = {
        'mix_norm': _row(W['l0_mix_norm']), 'w_in': W['l0_w_in'], 'ret_norm': _row(W['l0_ret_norm']),
        's5_lambda_re': W['l0_s5_lambda_re'], 's5_lambda_im': W['l0_s5_lambda_im'],
        's5_log_dt': W['l0_s5_log_dt'].reshape(32, 1),
        's5_b_re': W['l0_s5_b_re'].reshape(512, 64), 's5_b_im': W['l0_s5_b_im'].reshape(512, 64),
        's5_c_re': W['l0_s5_c_re'].reshape(2048, 16), 's5_c_im': W['l0_s5_c_im'].reshape(2048, 16),
        's5_d': _row(W['l0_s5_d']), 's5_w_glu': W['l0_s5_w_glu'].astype(F32), 's5_b_glu': _row(W['l0_s5_b_glu']),
        'w_out': W['l0_w_out'],
    }
    def common(L):
        return {'xa_norm': _row(W[L + 'xa_norm']), 'mem_norm': _row(W[L + 'mem_norm']), 'xa_wq': W[L + 'xa_wq'],
                'xa_wkv': W[L + 'xa_wkv'], 'xa_wo': W[L + 'xa_wo'], 'ffn_norm': _row(W[L + 'ffn_norm']),
                'ffn_w_up': W[L + 'ffn_w_up'], 'ffn_conv': W[L + 'ffn_conv'], 'ffn_w_down': W[L + 'ffn_w_down']}

    x1, s_even = _even_fwd(x, P0)
    if later_weights is not None:
        W = dict(W, **later_weights('l0_common', x1))
    C0 = common('l0_')
    x3, s_c0 = _common_fwd(x1, mem, C0, "l0_")

    if later_weights is not None:
        W = dict(W, **later_weights('l1', x3))
    w_in1 = W['l1_w_in']
    pad8 = jnp.zeros((8,), F32)
    w_all = jnp.pad(w_in1, ((0, 0), (0, 112)))
    P1 = {
        'mix_norm': _row(W['l1_mix_norm']), 'w_main': w_in1[:, :4096], 'w_tail': w_all[:, 4096:], 'w_all': w_all,
        'conv': W['l1_conv'],
        'a_log_p': _row(jnp.concatenate([pad8, W['l1_a_log'], jnp.zeros((112,), F32)])),
        'dtb_p': _row(jnp.concatenate([pad8, W['l1_dt_bias'], jnp.zeros((112,), F32)])),
        'o_norm': _row(W['l1_o_norm']), 'w_out': W['l1_w_out'],
    }
    C1 = common('l1_')
    x4, s_odd = _odd_fwd(x3, P1)
    x6, s_c1 = _common_fwd(x4, mem, C1, "l1_")
    loss_tile, dx6, d_final = _loss_grad(x6, target, _row(W['final_norm']))

    G = {'final_norm': d_final.reshape(-1)}
    dx4, g = _common_bwd(s_c1, dx6, C1, "l1_")
    for k, v in g.items():
        G['l1_' + k] = v
    dx3, g = _odd_bwd(s_odd, dx4, P1)
    G['l1_mix_norm'] = g['mix_norm']
    G['l1_w_in'] = g['w_all'][:, :4112]
    G['l1_conv'] = g['conv']
    G['l1_a_log'] = g['a_log_p'][0, 8:16]
    G['l1_dt_bias'] = g['dtb_p'][0, 8:16]
    G['l1_o_norm'] = g['o_norm']
    G['l1_w_out'] = g['w_out']
    if early_grads is not None:
        zero = early_grads('l1', G)
        C0 = dict(C0, ffn_w_down=C0['ffn_w_down'] + zero.astype(C0['ffn_w_down'].dtype))
    dx1, g = _common_bwd(s_c0, dx3, C0, "l0_")
    for k, v in g.items():
        G['l0_' + k] = v
    if early_grads is not None:
        zero = early_grads('l0_common', G)
        P0 = dict(P0, w_out=P0['w_out'] + zero.astype(P0['w_out'].dtype))
    dx0, g = _even_bwd(s_even, dx1, P0)
    for k, v in g.items():
        G['l0_' + k] = v
    return loss_tile, dx0, G


ANY = pl.BlockSpec(memory_space=pl.ANY)


def _place():
    return lax.axis_index("x"), lax.axis_index("y"), lax.axis_index("c")


def _my_chip():
    return 2 * lax.axis_index("x") + lax.axis_index("y")


def _chip_peers(x, y):
    return [(1 - x, y), (x, 1 - y), (1 - x, 1 - y)]


def _half(ref, mode, shard, j, h, split):
    r, w = shard
    rh = r // 2 if split else r
    h = h if split else 0
    if mode == 'row':
        return ref.at[pl.ds(j * r + h * rh, rh), :]
    if mode == 'col':
        return ref.at[pl.ds(h * rh, rh), pl.ds(j * w, w)]
    return ref.at[j, pl.ds(h * rh, rh), :]


def _place_shard(shard, mode, name):
    r, w = shard.shape
    dtype = BF16 if mode != 'tap' else shard.dtype
    if mode == 'tap':
        mode = 'slab'
    tr = _row_tile(r, w)
    nb = r // tr

    def kern(s_ref, o_ref):
        o_ref[...] = s_ref[...].astype(o_ref.dtype)

    if mode == 'row':
        full, o_spec = (4 * r, w), pl.BlockSpec((tr, w), lambda i: (_my_chip() * nb + i, 0))
    elif mode == 'col':
        full, o_spec = (r, 4 * w), pl.BlockSpec((tr, w), lambda i: (i, _my_chip()))
    else:
        full, o_spec = (4, r, w), pl.BlockSpec((None, tr, w), lambda i: (_my_chip(), i, 0))
    return pl.pallas_call(kern, name=name, grid=(nb,), in_specs=[pl.BlockSpec((tr, w), lambda i: (i, 0))],
                          out_specs=o_spec, out_shape=jax.ShapeDtypeStruct(full, dtype),
                          compiler_params=_cparams(("parallel",)))(shard)


def _gather_placed(fulls, modes, shards, splits):
    n = len(fulls)

    def body(*refs):
        outs = refs[n:2 * n]
        send_sems, recv_sems = refs[2 * n:]
        x, y, c = _place()
        peers = _chip_peers(x, y)
        me = 2 * x + y

        def win(a, j, h):
            return _half(outs[a], modes[a], shards[a], j, h, splits[a])

        def copy(a, k, j, h, to):
            return pltpu.make_async_remote_copy(src_ref=win(a, j, h), dst_ref=win(a, j, h),
                                                send_sem=send_sems.at[6 * a + k], recv_sem=recv_sems.at[6 * a + k],
                                                device_id=to, device_id_type=MESH)

        over_ici = [copy(a, k, me, c, (p[0], p[1], c)) for a in range(n) for k, p in enumerate(peers)]
        for cp in over_ici:
            cp.start()
        passed = []
        for a in range(n):
            for k, p in enumerate(peers):
                j = 2 * p[0] + p[1]
                copy(a, k, j, c, (p[0], p[1], c)).wait_recv()
                if splits[a]:
                    fwd = copy(a, 3 + k, j, c, (x, y, 1 - c))
                    fwd.start()
                    passed.append(fwd)
        for a in range(n):
            if splits[a]:
                for k, p in enumerate(peers):
                    copy(a, 3 + k, 2 * p[0] + p[1], 1 - c, (x, y, 1 - c)).wait_recv()
        for cp in over_ici + passed:
            cp.wait_send()

    return pl.pallas_call(
        body, name="gather_weights", in_specs=[ANY] * n, out_specs=[ANY] * n,
        out_shape=[jax.ShapeDtypeStruct(f.shape, f.dtype) for f in fulls],
        input_output_aliases={a: a for a in range(n)},
        scratch_shapes=[pltpu.SemaphoreType.DMA((6 * n,)), pltpu.SemaphoreType.DMA((6 * n,))],
    )(*fulls)


_FLIPS = [(dx, dy, dc) for dx in (0, 1) for dy in (0, 1) for dc in (0, 1) if (dx, dy, dc) != (0, 0, 0)]


def _send_other_half(gs, small, name):
    n = len(gs)

    def body(*refs):
        ins, outs = refs[:n], refs[n + 1:2 * n + 1]
        small_ref = refs[2 * n + 1]
        send_sems, recv_sems, small_send, small_recv = refs[2 * n + 2:]
        x, y, c = _place()
        me = 4 * x + 2 * y + c

        def peer(f):
            return (x ^ f[0], y ^ f[1], c ^ f[2])

        def small_copy(k, slab, to):
            return pltpu.make_async_remote_copy(src_ref=small_ref.at[slab], dst_ref=small_ref.at[slab],
                                                send_sem=small_send.at[k], recv_sem=small_recv.at[k], device_id=to,
                                                device_id_type=MESH)

        cps = []
        for a in range(n):
            rh = gs[a].shape[1] // 2
            cps.append(pltpu.make_async_remote_copy(
                src_ref=ins[a].at[:, pl.ds((1 - c) * rh, rh), :], dst_ref=outs[a], send_sem=send_sems.at[a],
                recv_sem=recv_sems.at[a], device_id=(x, y, 1 - c), device_id_type=MESH))
        smalls = [small_copy(k, me, peer(f)) for k, f in enumerate(_FLIPS)]
        for cp in cps + smalls:
            cp.start()
        for cp in cps:
            cp.wait()
        for k, f in enumerate(_FLIPS):
            p = peer(f)
            small_copy(k, 4 * p[0] + 2 * p[1] + p[2], p).wait_recv()
        for cp in smalls:
            cp.wait_send()

    outs = pl.pallas_call(
        body, name=name, in_specs=[ANY] * (n + 1), out_specs=[ANY] * (n + 1),
        out_shape=[jax.ShapeDtypeStruct((g.shape[0], g.shape[1] // 2, g.shape[2]), g.dtype) for g in gs]
        + [jax.ShapeDtypeStruct(small.shape, small.dtype)],
        input_output_aliases={n: n},
        scratch_shapes=[pltpu.SemaphoreType.DMA((n,)), pltpu.SemaphoreType.DMA((n,)),
                        pltpu.SemaphoreType.DMA((7,)), pltpu.SemaphoreType.DMA((7,))],
    )(*gs, small)
    return outs[:n], outs[n]


def _send_to_chips(ps, widths):
    n = len(ps)

    def body(*refs):
        ins, outs = refs[:n], refs[n:2 * n]
        send_sems, recv_sems = refs[2 * n:]
        x, y, c = _place()
        peers = _chip_peers(x, y)
        me = 2 * x + y

        def src(a, j):
            if ps[a].shape[0] == 4:
                return ins[a].at[j]
            return ins[a].at[0, :, pl.ds(j * widths[a], widths[a])]

        def copy(a, k, j, dst_slab, to):
            return pltpu.make_async_remote_copy(src_ref=src(a, j), dst_ref=outs[a].at[dst_slab],
                                                send_sem=send_sems.at[3 * a + k], recv_sem=recv_sems.at[3 * a + k],
                                                device_id=(to[0], to[1], c), device_id_type=MESH)

        sends = [copy(a, k, 2 * p[0] + p[1], me, p) for a in range(n) for k, p in enumerate(peers)]
        for cp in sends:
            cp.start()
        for a in range(n):
            for k, p in enumerate(peers):
                copy(a, k, me, 2 * p[0] + p[1], p).wait_recv()
        for cp in sends:
            cp.wait_send()

    return pl.pallas_call(
        body, name="send_to_chips", in_specs=[ANY] * n, out_specs=[ANY] * n,
        out_shape=[jax.ShapeDtypeStruct((4, p.shape[1], w), p.dtype) for p, w in zip(ps, widths)],
        scratch_shapes=[pltpu.SemaphoreType.DMA((3 * n,)), pltpu.SemaphoreType.DMA((3 * n,))],
    )(*ps)


def _share_halves(bufs, name):
    n = len(bufs)

    def body(*refs):
        outs = refs[n:2 * n]
        send_sems, recv_sems = refs[2 * n:]
        x, y, c = _place()
        sends, waits = [], []
        for a in range(n):
            rh = bufs[a].shape[0] // 2
            mine = outs[a].at[pl.ds(c * rh, rh), :]
            other = outs[a].at[pl.ds((1 - c) * rh, rh), :]
            sends.append(pltpu.make_async_remote_copy(src_ref=mine, dst_ref=mine, send_sem=send_sems.at[a],
                                                      recv_sem=recv_sems.at[a], device_id=(x, y, 1 - c),
                                                      device_id_type=MESH))
            waits.append(pltpu.make_async_remote_copy(src_ref=mine, dst_ref=other, send_sem=send_sems.at[a],
                                                      recv_sem=recv_sems.at[a], device_id=(x, y, 1 - c),
                                                      device_id_type=MESH))
        for cp in sends:
            cp.start()
        for cp in waits:
            cp.wait()

    return pl.pallas_call(
        body, name=name, in_specs=[ANY] * n, out_specs=[ANY] * n,
        out_shape=[jax.ShapeDtypeStruct(b.shape, b.dtype) for b in bufs],
        input_output_aliases={a: a for a in range(n)},
        scratch_shapes=[pltpu.SemaphoreType.DMA((n,)), pltpu.SemaphoreType.DMA((n,))],
    )(*bufs)


def _gather_all(mine):
    flips = [(dx, dy, dc) for dx in (0, 1) for dy in (0, 1) for dc in (0, 1) if (dx, dy, dc) != (0, 0, 0)]

    def body(x_ref, out_ref, send_sems, recv_sems, local_sem):
        x, y, c = _place()
        me = 4 * x + 2 * y + c

        def peer(f):
            return (x ^ f[0], y ^ f[1], c ^ f[2])

        def copy(k, slab, to):
            return pltpu.make_async_remote_copy(src_ref=x_ref, dst_ref=out_ref.at[slab], send_sem=send_sems.at[k],
                                                recv_sem=recv_sems.at[k], device_id=to, device_id_type=MESH)

        own = pltpu.make_async_copy(x_ref, out_ref.at[me], local_sem)
        own.start()
        sends = [copy(k, me, peer(f)) for k, f in enumerate(flips)]
        for s in sends:
            s.start()
        for k, f in enumerate(flips):
            p = peer(f)
            copy(k, 4 * p[0] + 2 * p[1] + p[2], p).wait_recv()
        for s in sends:
            s.wait_send()
        own.wait()

    return pl.pallas_call(
        body, name="gather_all", in_specs=[ANY], out_specs=ANY,
        out_shape=jax.ShapeDtypeStruct((8,) + mine.shape, mine.dtype),
        scratch_shapes=[pltpu.SemaphoreType.DMA((7,)), pltpu.SemaphoreType.DMA((7,)), pltpu.SemaphoreType.DMA],
    )(mine)


TILE_BYTES = 2 * 1024 * 1024


def _row_tile(rows, width=1024):
    for t in (512, 352, 256, 176, 128, 64, 32, 16, 8):
        if rows % t == 0 and t * width * 4 <= TILE_BYTES:
            return t
    return rows


def _pair_sum(g, got, name):
    ns, r, w = g.shape
    rh = r // 2
    tr = _row_tile(rh, w)
    nb = rh // tr

    def kern(g_ref, o_ref, out_ref):
        out_ref[...] = (g_ref[...] + o_ref[...]).astype(BF16)

    return pl.pallas_call(
        kern, name=name, grid=(ns, nb),
        in_specs=[pl.BlockSpec((None, tr, w), lambda j, i: (j, lax.axis_index("c") * nb + i, 0)),
                  pl.BlockSpec((None, tr, w), lambda j, i: (j, i, 0))],
        out_specs=pl.BlockSpec((None, tr, w), lambda j, i: (j, i, 0)),
        out_shape=jax.ShapeDtypeStruct((ns, rh, w), BF16),
        compiler_params=_cparams(("parallel", "parallel")))(g, got)


def _chip_sum(pair, recv, w, name):
    rh = pair.shape[1]
    tr = _row_tile(rh, w)
    nb = rh // tr

    def kern(own_ref, r1_ref, r2_ref, r3_ref, out_ref):
        acc = own_ref[...].astype(F32)
        for r_ref in (r1_ref, r2_ref, r3_ref):
            acc = acc + r_ref[...].astype(F32)
        out_ref[...] = acc

    if pair.shape[0] == 4:
        own_spec = pl.BlockSpec((None, tr, w), lambda i: (_my_chip(), i, 0))
    else:
        own_spec = pl.BlockSpec((None, tr, w), lambda i: (0, i, _my_chip()))
    recv_specs = [pl.BlockSpec((None, tr, w), functools.partial(lambda i, d: ((_my_chip() + d) % 4, i, 0), d=d))
                  for d in (1, 2, 3)]
    return pl.pallas_call(
        kern, name=name, grid=(nb,), in_specs=[own_spec] + recv_specs,
        out_specs=pl.BlockSpec((tr, w), lambda i: (lax.axis_index("c") * nb + i, 0)),
        out_shape=jax.ShapeDtypeStruct((2 * rh, w), F32), compiler_params=_cparams(("parallel",)))(pair, recv, recv, recv)


def _slab_sum(slabs, name):
    n, R, w = slabs.shape
    tr = _row_tile(R)

    def kern(s_ref, o_ref):
        acc = s_ref[0].astype(F32)
        for k in range(1, n):
            acc = acc + s_ref[k].astype(F32)
        o_ref[...] = acc

    return pl.pallas_call(
        kern, name=name, grid=(R // tr,), in_specs=[pl.BlockSpec((n, tr, w), lambda i: (0, i, 0))],
        out_specs=pl.BlockSpec((tr, w), lambda i: (i, 0)), out_shape=jax.ShapeDtypeStruct((R, w), F32),
        compiler_params=_cparams(("parallel",)))(slabs)


def _adamw(w, g, m, v, name):
    R, C = w.shape
    tr = _pick(R, (256, 128, 64, 32, 16, 8))

    def kern(w_ref, g_ref, m_ref, v_ref, d_ref, nm_ref, nv_ref):
        gv = g_ref[...]
        m2 = ADAM_B1 * m_ref[...] + (1.0 - ADAM_B1) * gv
        v2 = ADAM_B2 * v_ref[...] + (1.0 - ADAM_B2) * jnp.square(gv)
        m_hat = m2 / (1.0 - ADAM_B1 ** ADAM_STEP)
        v_hat = v2 / (1.0 - ADAM_B2 ** ADAM_STEP)
        d_ref[...] = -ADAM_LR * (m_hat / (jnp.sqrt(v_hat) + ADAM_EPS) + ADAM_WD * w_ref[...])
        nm_ref[...] = m2
        nv_ref[...] = v2

    spec = pl.BlockSpec((tr, C), lambda i: (i, 0))
    return pl.pallas_call(
        kern, name=name, grid=(R // tr,), in_specs=[spec] * 4, out_specs=[spec] * 3,
        out_shape=[jax.ShapeDtypeStruct((R, C), F32)] * 3, compiler_params=_cparams(("parallel",)))(w, g, m, v)


def _pack_small(vals):
    flat = jnp.concatenate([vals[n].astype(F32).reshape(-1) for n in SMALL_NAMES])
    rows = -(-flat.shape[0] // (8 * LANES)) * 8
    return jnp.pad(flat, (0, rows * LANES - flat.shape[0])).reshape(rows, LANES)


def _unpack_small(packed, shapes):
    flat = packed.reshape(-1)
    out = {}
    off = 0
    for n in SMALL_NAMES:
        size = int(np.prod(shapes[n]))
        out[n] = flat[off:off + size].reshape(shapes[n])
        off += size
    return out


HBM = pl.BlockSpec(memory_space=pltpu.HBM)
SEM = pl.BlockSpec(memory_space=pltpu.SEMAPHORE)
DATAFLOW = pltpu.SideEffectType.DATAFLOW_SIDE_EFFECTING


def _in_hbm(a):
    return pltpu.with_memory_space_constraint(a, pltpu.HBM)


def _split_copy_start(srcs, lands, copies, after, name):
    ns, nl = len(srcs), len(lands)
    ncopy = len(copies(list(srcs), list(lands), None, None, probe=True))

    def body(*refs):
        src_refs, land_refs = refs[:ns], refs[ns:ns + nl]
        send_sems, recv_sems = refs[ns + nl + 1:ns + nl + 3]
        token = refs[-1]
        for cp in copies(src_refs, land_refs, send_sems, recv_sems):
            cp.start()
        token[...] = jnp.zeros_like(token)

    outs = pl.pallas_call(
        body, name=name,
        out_shape=(pltpu.SemaphoreType.DMA((ncopy,)), pltpu.SemaphoreType.DMA((ncopy,)),
                   *[pltpu.HBM(a.shape, a.dtype) for a in srcs], *[pltpu.HBM(a.shape, a.dtype) for a in lands],
                   jax.ShapeDtypeStruct((8, 128), F32)),
        in_specs=[HBM] * (ns + nl) + [ANY],
        out_specs=(SEM, SEM, *[HBM] * (ns + nl), pl.BlockSpec(memory_space=pltpu.VMEM)),
        input_output_aliases={i: 2 + i for i in range(ns + nl)},
        compiler_params=pltpu.CompilerParams(has_side_effects=DATAFLOW),
    )(*[_in_hbm(a) for a in srcs], *[_in_hbm(a) for a in lands], after)
    return outs[0], outs[1], outs[2:2 + ns], outs[2 + ns:2 + ns + nl], outs[-1]


def _split_copy_wait(send_sems, recv_sems, srcs, lands, copies, after, name):
    ns, nl = len(srcs), len(lands)

    def body(*refs):
        src_refs, land_refs = refs[:ns], refs[ns:ns + nl]
        send_ref, recv_ref = refs[ns + nl:ns + nl + 2]
        for cp in copies(src_refs, land_refs, send_ref, recv_ref):
            cp.wait_send()
            cp.wait_recv()

    outs = pl.pallas_call(
        body, name=name,
        out_shape=tuple(pltpu.HBM(a.shape, a.dtype) for a in list(srcs) + list(lands)),
        in_specs=[HBM] * (ns + nl) + [SEM, SEM, ANY], out_specs=tuple([HBM] * (ns + nl)),
        input_output_aliases={i: i for i in range(ns + nl)},
        compiler_params=pltpu.CompilerParams(has_side_effects=DATAFLOW),
    )(*srcs, *lands, send_sems, recv_sems, after)
    return outs[:ns], outs[ns:]


def _matrix_mode(n):
    return 'slab' if n == 'l1_w_in' else ('row' if MATRICES[n] == 0 else 'col')


def _placed(A, names):
    modes = ['slab' if n in CONVS else _matrix_mode(n) for n in names]
    fulls = [_place_shard(A[n], 'tap' if n in CONVS else m, "place_" + n) for n, m in zip(names, modes)]
    return fulls, modes


def _assembled(names, modes, outs):
    return {n: jnp.concatenate([o[j] for j in range(4)], axis=1) if m == 'slab' else o
            for n, m, o in zip(names, modes, outs)}


def _gather_weights(A, names):
    fulls, modes = _placed(A, names)
    outs = _gather_placed(fulls, modes, [A[n].shape for n in names], [n not in CONVS for n in names])
    return _assembled(names, modes, outs)


def _whole_shard_copies(modes, shards):
    def copies(src_refs, land_refs, send_sems, recv_sems, probe=False):
        if probe:
            return [None] * (3 * len(land_refs))
        x, y, c = _place()
        me = 2 * x + y
        out = []
        for a, ref in enumerate(land_refs):
            for k, p in enumerate(_chip_peers(x, y)):
                out.append(pltpu.make_async_remote_copy(
                    src_ref=_half(ref, modes[a], shards[a], me, 0, False),
                    dst_ref=_half(ref, modes[a], shards[a], me, 0, False),
                    send_sem=send_sems.at[3 * a + k], recv_sem=recv_sems.at[3 * a + k],
                    device_id=(p[0], p[1], c), device_id_type=MESH))
        return out
    return copies


def _gather_weights_start(A, names, after, tag):
    fulls, modes = _placed(A, names)
    copies = _whole_shard_copies(modes, [A[n].shape for n in names])
    send_sems, recv_sems, _, lands, zeros = _split_copy_start([], fulls, copies, after, "gather_start_" + tag)
    return (send_sems, recv_sems, lands, copies, names, modes), zeros


def _gather_weights_wait(state, after, tag):
    send_sems, recv_sems, lands, copies, names, modes = state
    _, outs = _split_copy_wait(send_sems, recv_sems, [], lands, copies, after, "gather_wait_" + tag)
    return _assembled(names, modes, outs)


def _to_chips_copies(pair_shapes, widths):
    def copies(src_refs, land_refs, send_sems, recv_sems, probe=False):
        if probe:
            return [None] * (3 * len(land_refs))
        x, y, c = _place()
        me = 2 * x + y
        out = []
        for a, (src, land) in enumerate(zip(src_refs, land_refs)):
            for k, p in enumerate(_chip_peers(x, y)):
                j = 2 * p[0] + p[1]
                part = src.at[j] if pair_shapes[a][0] == 4 else src.at[0, :, pl.ds(j * widths[a], widths[a])]
                out.append(pltpu.make_async_remote_copy(
                    src_ref=part, dst_ref=land.at[me], send_sem=send_sems.at[3 * a + k],
                    recv_sem=recv_sems.at[3 * a + k], device_id=(p[0], p[1], c), device_id_type=MESH))
        return out
    return copies


def _reduce_begin(G, names, small, tag):
    gs, widths = [], []
    for n in names:
        g = G[n]
        mode = _matrix_mode(n)
        if mode == 'row':
            gs.append(g.reshape(4, g.shape[0] // 4, g.shape[1]))
            widths.append(g.shape[1])
        elif mode == 'col':
            gs.append(g[None])
            widths.append(g.shape[1] // 4)
        else:
            wd = g.shape[1] // 4
            gs.append(jnp.stack([g[:, j * wd:(j + 1) * wd] for j in range(4)]))
            widths.append(wd)
    got, small = _send_other_half(gs, small, "send_other_half_" + tag)
    pairs = [_pair_sum(g, o, "pair_sum_" + n) for n, g, o in zip(names, gs, got)]
    return pairs, widths, small


def _reduce_end(names, pairs, recv, widths, tag):
    halves = [_chip_sum(p, r, w, "chip_sum_" + n) for n, p, r, w in zip(names, pairs, recv, widths)]
    return dict(zip(names, _share_halves(halves, "share_halves_" + tag)))


def _small_slab(packed):
    me8 = 4 * lax.axis_index("x") + 2 * lax.axis_index("y") + lax.axis_index("c")
    return lax.dynamic_update_slice(jnp.zeros((8,) + packed.shape, F32), packed[None], (me8, 0, 0))


def kernel(*args):
    A = dict(zip(ARG_NAMES, args, strict=True))
    x, mem, target = A['x'][0], A['mem'][0], A['loss_target'][0]

    stages = {'l0_mixer': ['l0_w_in', 'l0_s5_w_glu', 'l0_w_out'],
              'l0_common': [n for n in MATRIX_NAMES if n.startswith(('l0_xa_', 'l0_ffn_'))],
              'l1': [n for n in MATRIX_NAMES if n.startswith('l1_')]}
    W = _gather_weights(A, stages['l0_mixer'] + list(CONVS))
    for n in SMALL_NAMES:
        if n not in CONVS:
            W[n] = A[n]
    flights = {}
    after = W['l0_w_in']
    for stage in ('l0_common', 'l1'):
        flights[stage], after = _gather_weights_start(A, stages[stage], after, stage)
    W['l0_mix_norm'] = W['l0_mix_norm'] + after[0, 0]

    reduce_state = {}

    def early_grads(stage, G):
        pairs, widths, _ = _reduce_begin(G, stages[stage], jnp.zeros((8, 8, LANES), F32), stage)
        copies = _to_chips_copies([p.shape for p in pairs], widths)
        lands = [lax.empty((4, p.shape[1], w), p.dtype) for p, w in zip(pairs, widths)]
        send_sems, recv_sems, pairs, lands, zeros = _split_copy_start(pairs, lands, copies, G['final_norm'],
                                                                      "reduce_start_" + stage)
        reduce_state[stage] = (send_sems, recv_sems, pairs, lands, copies, widths)
        return zeros[0, 0]

    loss_tile, grad_x, G = _local_step(
        x, mem, target, W, later_weights=lambda stage, after: _gather_weights_wait(flights[stage], after, stage),
        early_grads=early_grads)
    loss = lax.psum(loss_tile[0, 0], ("x", "y", "c"))

    g_mat = {}
    for stage in ('l1', 'l0_common'):
        send_sems, recv_sems, pairs, lands, copies, widths = reduce_state[stage]
        sent, recv = _split_copy_wait(send_sems, recv_sems, pairs, lands, copies, grad_x, "reduce_wait_" + stage)
        g_mat.update(_reduce_end(stages[stage], sent, recv, widths, stage))
    pairs, widths, g_small = _reduce_begin(G, stages['l0_mixer'],
                                           _small_slab(_pack_small({n: G[n] for n in SMALL_NAMES})), "l0_mixer")
    g_mat.update(_reduce_end(stages['l0_mixer'], pairs, _send_to_chips(pairs, widths), widths, "l0_mixer"))
    g_small = _unpack_small(_slab_sum(g_small, "sum_small"), {n: G[n].shape for n in SMALL_NAMES})
    me = 2 * lax.axis_index("x") + lax.axis_index("y")
    for n in CONVS:
        wd = A[n].shape[1]
        g_small[n] = lax.dynamic_slice_in_dim(g_small[n], me * wd, wd, axis=1)
    flat_names = [n for n in SMALL_NAMES if n not in CONVS]

    def pack_flat(prefix):
        return _pack_small_flat({n: A[prefix + n] for n in flat_names}, flat_names)

    shapes = {n: A[n].shape for n in flat_names}
    d_s, m_s, v_s = _adamw(pack_flat(''), _pack_small_flat(g_small, flat_names), pack_flat('m_'), pack_flat('v_'),
                           "adamw_small")
    d_s, m_s, v_s = (_unpack_flat(p, shapes, flat_names) for p in (d_s, m_s, v_s))

    grads, deltas, new_m, new_v = {}, {}, {}, {}
    for n in WEIGHTS:
        if n in MATRICES or n in CONVS:
            grads[n] = g_mat[n] if n in MATRICES else g_small[n]
            deltas[n], new_m[n], new_v[n] = _adamw(A[n], grads[n], A['m_' + n], A['v_' + n], "adamw_" + n)
        else:
            grads[n] = g_small[n].reshape(A[n].shape)
            deltas[n], new_m[n], new_v[n] = d_s[n], m_s[n], v_s[n]
    return (loss, grad_x[None], *[grads[n] for n in WEIGHTS], *[deltas[n] for n in WEIGHTS],
            *[new_m[n] for n in WEIGHTS], *[new_v[n] for n in WEIGHTS])


def _pack_small_flat(vals, names):
    flat = jnp.concatenate([vals[n].astype(F32).reshape(-1) for n in names])
    rows = -(-flat.shape[0] // (8 * LANES)) * 8
    return jnp.pad(flat, (0, rows * LANES - flat.shape[0])).reshape(rows, LANES)


def _unpack_flat(packed, shapes, names):
    flat = packed.reshape(-1)
    out = {}
    off = 0
    for n in names:
        size = int(np.prod(shapes[n]))
        out[n] = flat[off:off + size].reshape(shapes[n])
        off += size
    return out
```

```python
import functools
import math

import numpy as np
import jax
import jax.numpy as jnp
from jax import lax
from jax.experimental import pallas as pl
from jax.experimental.pallas import tpu as pltpu

F32 = jnp.float32
BF16 = jnp.bfloat16
EPS = 1e-6
MESH = pl.DeviceIdType.MESH

ADAM_LR = 0.001
ADAM_B1 = 0.9
ADAM_B2 = 0.999
ADAM_EPS = 1e-08
ADAM_WD = 0.01
ADAM_STEP = 10

VMEM_LIMIT_BYTES = 56 * 1024 * 1024
MATMUL_VMEM_BYTES = 44 * 1024 * 1024
LANES = 1024

WEIGHTS = ['l0_mix_norm', 'l0_w_in', 'l0_ret_norm', 'l0_s5_lambda_re', 'l0_s5_lambda_im', 'l0_s5_b_re', 'l0_s5_b_im',
           'l0_s5_c_re', 'l0_s5_c_im', 'l0_s5_d', 'l0_s5_log_dt', 'l0_s5_w_glu', 'l0_s5_b_glu', 'l0_w_out',
           'l0_xa_norm', 'l0_mem_norm', 'l0_xa_wq', 'l0_xa_wkv', 'l0_xa_wo', 'l0_ffn_norm', 'l0_ffn_w_up',
           'l0_ffn_conv', 'l0_ffn_w_down', 'l1_mix_norm', 'l1_w_in', 'l1_conv', 'l1_a_log', 'l1_dt_bias',
           'l1_o_norm', 'l1_w_out', 'l1_xa_norm', 'l1_mem_norm', 'l1_xa_wq', 'l1_xa_wkv', 'l1_xa_wo',
           'l1_ffn_norm', 'l1_ffn_w_up', 'l1_ffn_conv', 'l1_ffn_w_down', 'final_norm']
ARG_NAMES = (['x', 'mem'] + WEIGHTS + ['loss_target'] + ['m_' + w for w in WEIGHTS] + ['v_' + w for w in WEIGHTS])

MATRICES = {
    'l0_w_in': 1, 'l0_s5_w_glu': 0, 'l0_w_out': 0, 'l0_xa_wq': 0, 'l0_xa_wkv': 1, 'l0_xa_wo': 0, 'l0_ffn_w_up': 1,
    'l0_ffn_w_down': 0, 'l1_w_in': 1, 'l1_w_out': 0, 'l1_xa_wq': 0, 'l1_xa_wkv': 1, 'l1_xa_wo': 0,
    'l1_ffn_w_up': 1, 'l1_ffn_w_down': 0,
}
CONVS = ('l0_ffn_conv', 'l1_conv', 'l1_ffn_conv')
MATRIX_NAMES = [w for w in WEIGHTS if w in MATRICES]
SMALL_NAMES = [w for w in WEIGHTS if w not in MATRICES]


def _cparams(sem=None):
    return pltpu.CompilerParams(dimension_semantics=sem, vmem_limit_bytes=VMEM_LIMIT_BYTES)


def _pick(n, cands):
    for c in cands:
        if n % c == 0:
            return c
    return n


_NN = ((1,), (0,))
_NT = ((1,), (1,))
_TN = ((0,), (0,))


def _dot(a, b, dims, hi):
    if hi is not None:
        return lax.dot_general(a.astype(F32), b.astype(F32), (dims, ((), ())), precision=hi,
                               preferred_element_type=F32)
    return lax.dot_general(a.astype(BF16), b.astype(BF16), (dims, ((), ())), preferred_element_type=F32)


def _make_mm(hi):
    @jax.custom_vjp
    def nn(a, b):
        return _dot(a, b, _NN, hi)

    def nn_f(a, b):
        return nn(a, b), (a, b)

    def nn_b(r, g):
        a, b = r
        return _dot(g, b, _NT, hi), _dot(a, g, _TN, hi)

    nn.defvjp(nn_f, nn_b)

    @jax.custom_vjp
    def nt(a, b):
        return _dot(a, b, _NT, hi)

    def nt_f(a, b):
        return nt(a, b), (a, b)

    def nt_b(r, g):
        a, b = r
        return _dot(g, b, _NN, hi), _dot(g, a, _TN, hi)

    nt.defvjp(nt_f, nt_b)

    @jax.custom_vjp
    def tn(a, b):
        return _dot(a, b, _TN, hi)

    def tn_f(a, b):
        return tn(a, b), (a, b)

    def tn_b(r, g):
        a, b = r
        return _dot(b, g, _NT, hi), _dot(a, g, _NN, hi)

    tn.defvjp(tn_f, tn_b)
    return nn, nt, tn


mm, mm_nt, mm_tn = _make_mm(None)
mmh, mmh_nt, mmh_tn = _make_mm(lax.Precision.HIGHEST)
mm3, _, _ = _make_mm(lax.Precision.HIGH)


@jax.custom_vjp
def _swap_halves(x):
    return pltpu.roll(x, 64, 1)


def _swap_f(x):
    return pltpu.roll(x, 64, 1), None


def _swap_b(_, g):
    return (pltpu.roll(g, 64, 1),)


_swap_halves.defvjp(_swap_f, _swap_b)


def _silu(x):
    return x * jax.nn.sigmoid(x)


def _rms(x, g):
    return x * lax.rsqrt(jnp.mean(x * x, axis=-1, keepdims=True) + EPS) * g


def _iota(shape, dim):
    return lax.broadcasted_iota(jnp.int32, shape, dim)


def _matmul(a, b, mode="nn", res=None, name="mm", a_cols=None, out_dtype=F32):
    a_off, a_w = (0, a.shape[1]) if a_cols is None else a_cols
    if mode == "nn":
        (M, K), (K2, N) = (a.shape[0], a_w), b.shape
    elif mode == "nt":
        (M, K), (N, K2) = (a.shape[0], a_w), b.shape
    else:
        (K, M), (K2, N) = (a.shape[0], a_w), b.shape
    assert K == K2, (a.shape, b.shape, mode)
    tn = _pick(N, (1024, 1408, 512, 256, 128))
    tk = _pick(K, (1024, 1408, 512, 256, 128))
    nk = K // tk
    a_bytes = 2 if a.dtype == BF16 else 4
    b_bytes = 2 if b.dtype == BF16 else 4
    for tm in (1024, 512, 1408, 256, 128):
        need = 2 * (tm * tk * a_bytes + tk * tn * b_bytes + (tm * tn * 4 if res is not None else 0)) + 3 * tm * tn * 4
        if M % tm == 0 and need <= MATMUL_VMEM_BYTES:
            break
    dims = {"nn": _NN, "nt": _NT, "tn": _TN}[mode]
    ao = a_off // (tm if mode == "tn" else tk)
    assert ao * (tm if mode == "tn" else tk) == a_off
    if mode == "nn":
        a_spec = pl.BlockSpec((tm, tk), lambda i, j, k: (i, k + ao))
        b_spec = pl.BlockSpec((tk, tn), lambda i, j, k: (k, j))
    elif mode == "nt":
        a_spec = pl.BlockSpec((tm, tk), lambda i, j, k: (i, k + ao))
        b_spec = pl.BlockSpec((tn, tk), lambda i, j, k: (j, k))
    else:
        a_spec = pl.BlockSpec((tk, tm), lambda i, j, k: (k, i + ao))
        b_spec = pl.BlockSpec((tk, tn), lambda i, j, k: (k, j))
    o_spec = pl.BlockSpec((tm, tn), lambda i, j, k: (i, j))
    has_res = res is not None

    def kern(*refs):
        if has_res:
            a_ref, b_ref, r_ref, o_ref, acc_ref = refs
        else:
            a_ref, b_ref, o_ref, acc_ref = refs
        k = pl.program_id(2)
        part = lax.dot_general(a_ref[...].astype(BF16), b_ref[...].astype(BF16), (dims, ((), ())),
                               preferred_element_type=F32)
        if nk == 1:
            o_ref[...] = (part + r_ref[...] if has_res else part).astype(o_ref.dtype)
            return

        @pl.when(k == 0)
        def _():
            acc_ref[...] = part

        @pl.when((k > 0) & (k < nk - 1))
        def _():
            acc_ref[...] += part

        @pl.when(k == nk - 1)
        def _():
            total = acc_ref[...] + part
            o_ref[...] = (total + r_ref[...] if has_res else total).astype(o_ref.dtype)

    in_specs = [a_spec, b_spec] + ([o_spec] if has_res else [])
    ops = (a, b) + ((res,) if has_res else ())
    return pl.pallas_call(
        kern, name=name, grid=(M // tm, N // tn, nk), in_specs=in_specs, out_specs=o_spec,
        out_shape=jax.ShapeDtypeStruct((M, N), out_dtype), scratch_shapes=[pltpu.VMEM((tm, tn), F32)],
        compiler_params=_cparams(("parallel", "parallel", "arbitrary")))(*ops)


def _matmul_cat(pieces, b, mode="nn", res=None, name="mmcat"):
    M = pieces[0].shape[0]
    widths = [p.shape[1] for p in pieces]
    K = sum(widths)
    N = b.shape[1] if mode == "nn" else b.shape[0]
    assert (b.shape[0] if mode == "nn" else b.shape[1]) == K
    tn = _pick(N, (1024, 512, 256, 128))
    a_bytes = 2 if pieces[0].dtype == BF16 else 4
    for tm in (1024, 512, 256, 128):
        need = 2 * (tm * K * a_bytes + K * tn * 2 + (tm * tn * 4 if res is not None else 0)) + 3 * tm * tn * 4
        if M % tm == 0 and need <= MATMUL_VMEM_BYTES:
            break
    npc = len(pieces)
    has_res = res is not None
    dims = _NN if mode == "nn" else _NT

    def kern(*refs):
        b_ref = refs[npc]
        o_ref = refs[-1]
        acc = refs[npc + 1][...] if has_res else None
        off = 0
        for p in range(npc):
            bp = b_ref[off:off + widths[p], :] if mode == "nn" else b_ref[:, off:off + widths[p]]
            t = lax.dot_general(refs[p][...].astype(BF16), bp.astype(BF16), (dims, ((), ())),
                                preferred_element_type=F32)
            acc = t if acc is None else acc + t
            off += widths[p]
        o_ref[...] = acc

    in_specs = [pl.BlockSpec((tm, w), lambda j, i: (i, 0)) for w in widths]
    in_specs.append(pl.BlockSpec((K, tn), lambda j, i: (0, j)) if mode == "nn"
                    else pl.BlockSpec((tn, K), lambda j, i: (j, 0)))
    o_spec = pl.BlockSpec((tm, tn), lambda j, i: (i, j))
    if has_res:
        in_specs.append(o_spec)
    ops = list(pieces) + [b] + ([res] if has_res else [])
    return pl.pallas_call(
        kern, name=name, grid=(N // tn, M // tm), in_specs=in_specs, out_specs=o_spec,
        out_shape=jax.ShapeDtypeStruct((M, N), F32), compiler_params=_cparams(("parallel", "parallel")))(*ops)


def _blk(a, width=None, colblk=0):
    return (a, a.shape[1] if width is None else width, colblk)


def _row_specs(blocked, params, ts):
    specs = []
    for (_, w, cb) in blocked:
        specs.append(pl.BlockSpec((ts, w), functools.partial(lambda i, cb: (i, cb), cb=cb)))
    for p in params:
        specs.append(pl.BlockSpec(p.shape, lambda i: (0, 0)))
    return specs


def _rowwise(fn, blocked, params, out_widths, name, ts=256, out_dtypes=None):
    S = blocked[0][0].shape[0]
    ts = min(ts, S)
    nb, npar = len(blocked), len(params)
    out_dtypes = [F32] * len(out_widths) if out_dtypes is None else out_dtypes

    def kern(*refs):
        vals = [r[...] for r in refs[:nb + npar]]
        outs = fn(*vals)
        for o_ref, o in zip(refs[nb + npar:], outs):
            o_ref[...] = o.astype(o_ref.dtype)

    return pl.pallas_call(
        kern, name=name, grid=(S // ts,), in_specs=_row_specs(blocked, params, ts),
        out_specs=[pl.BlockSpec((ts, w), lambda i: (i, 0)) for w in out_widths],
        out_shape=[jax.ShapeDtypeStruct((S, w), d) for w, d in zip(out_widths, out_dtypes)],
        compiler_params=_cparams(("parallel",)))(*[b[0] for b in blocked], *params)


def _rowwise_bwd(fn, blocked, params, cots, name, blocked_grad=None, param_grad=None, adds=None, ts=256,
                 out_dtypes=None):
    S = blocked[0][0].shape[0]
    ts = min(ts, S)
    cots = [c if isinstance(c, tuple) else _blk(c) for c in cots]
    nb, npar, nc = len(blocked), len(params), len(cots)
    blocked_grad = [True] * nb if blocked_grad is None else blocked_grad
    param_grad = [True] * npar if param_grad is None else param_grad
    adds = {} if adds is None else adds
    bidx = [i for i in range(nb) if blocked_grad[i]]
    pidx = [i for i in range(npar) if param_grad[i]]
    add_keys = sorted(adds)
    n_in = nb + npar + nc + len(add_keys)

    def kern(*refs):
        i = pl.program_id(0)
        xs = [r[...] for r in refs[:nb]]
        ps = [r[...] for r in refs[nb:nb + npar]]
        gs = [r[...] for r in refs[nb + npar:nb + npar + nc]]
        add_vals = {k: refs[nb + npar + nc + n][...] for n, k in enumerate(add_keys)}
        outs = refs[n_in:]

        def f(*diff):
            full_x = list(xs)
            full_p = list(ps)
            for n, ix in enumerate(bidx):
                full_x[ix] = diff[n]
            for n, ix in enumerate(pidx):
                full_p[ix] = diff[len(bidx) + n]
            return tuple(fn(*full_x, *full_p))

        _, vjp = jax.vjp(f, *[xs[ix] for ix in bidx], *[ps[ix] for ix in pidx])
        grads = vjp(tuple(gs))
        for n, ix in enumerate(bidx):
            g = grads[n]
            if ix in add_vals:
                g = g + add_vals[ix]
            outs[n][...] = g.astype(outs[n].dtype)
        for n in range(len(pidx)):
            o_ref = outs[len(bidx) + n]

            @pl.when(i == 0)
            def _(o_ref=o_ref):
                o_ref[...] = jnp.zeros_like(o_ref)

            o_ref[...] += grads[len(bidx) + n]

    in_specs = _row_specs(blocked, params, ts)
    in_specs += _row_specs(cots, [], ts)
    in_specs += [pl.BlockSpec((ts, adds[k].shape[1]), lambda i: (i, 0)) for k in add_keys]
    out_specs = [pl.BlockSpec((ts, blocked[ix][1]), lambda i: (i, 0)) for ix in bidx]
    out_specs += [pl.BlockSpec(params[ix].shape, lambda i: (0, 0)) for ix in pidx]
    out_dtypes = [F32] * len(bidx) if out_dtypes is None else out_dtypes
    out_shape = [jax.ShapeDtypeStruct((S, blocked[ix][1]), d) for ix, d in zip(bidx, out_dtypes)]
    out_shape += [jax.ShapeDtypeStruct(params[ix].shape, F32) for ix in pidx]
    return pl.pallas_call(
        kern, name=name, grid=(S // ts,), in_specs=in_specs, out_specs=out_specs, out_shape=out_shape,
        compiler_params=_cparams(("arbitrary",)))(*[b[0] for b in blocked], *params, *[c[0] for c in cots],
                                                    *[adds[k] for k in add_keys])


def _rms_fn(x, g):
    return (_rms(x, g),)


def _head_norm(o, n_heads, dh):
    outs = []
    for h in range(n_heads):
        oh = o[:, h * dh:(h + 1) * dh]
        outs.append(oh * lax.rsqrt(jnp.mean(oh * oh, axis=-1, keepdims=True) + EPS))
    return outs


def _ret_post_fn(o_raw, gate, ret_norm):
    o = jnp.concatenate(_head_norm(o_raw, 4, 128), axis=1)
    return (o * ret_norm * _silu(gate),)


def _s5_post_fn(y1, y2, u, d, w_glu, b_glu):
    y = y1 - y2 + d * u
    y = jax.nn.gelu(y)
    return (y * jax.nn.sigmoid(mm(y, w_glu) + b_glu),)


def _xattn_fn(q, kv):
    outs = []
    for h in range(4):
        qh = q[:, h * 256:(h + 1) * 256]
        kh = kv[:, h * 256:(h + 1) * 256]
        vh = kv[:, 1024 + h * 256:1024 + (h + 1) * 256]
        s = mm_nt(qh, kh) * (256 ** -0.5)
        s = s - lax.stop_gradient(jnp.max(s, axis=-1, keepdims=True))
        p = jnp.exp(s)
        p = p / jnp.sum(p, axis=-1, keepdims=True)
        outs.append(mm(p, vh))
    return (jnp.concatenate(outs, axis=1),)


def _softplus(x):
    return jnp.maximum(x, 0.0) + jnp.log1p(jnp.exp(-jnp.abs(x)))


def _gdn_gates_fn(pt, a_log_p, dtb_p):
    rows, cols = _iota((128, 1024), 0), _iota((128, 1024), 1)
    e_b = (rows == (cols >> 7)).astype(F32)
    e_a = (rows == (cols >> 7) + 8).astype(F32)
    beta = jax.nn.sigmoid(pt)
    g = -(jnp.exp(a_log_p) * _softplus(pt + dtb_p))
    return mmh(g, e_a), mmh(beta, e_b)


def _gdn_post_fn(o_raw, z, o_norm):
    outs = _head_norm(o_raw, 8, 128)
    o = jnp.concatenate([oh * o_norm for oh in outs], axis=1)
    return (o * _silu(z),)


def _ffn_post(up, gate):
    return _silu(gate) * up


def _shift_down(cur, prev8, sh, row8):
    if sh == 0:
        return cur
    r = pltpu.roll(cur, sh, 0)
    p = pltpu.roll(prev8, sh, 0)
    top = jnp.where(row8 < sh, p, r[0:8])
    if cur.shape[0] == 8:
        return top
    return jnp.concatenate([top, r[8:]], axis=0)


def _shift_up(cur, next8, sh, row8):
    if sh == 0:
        return cur
    ts = cur.shape[0]
    r = pltpu.roll(cur, ts - sh, 0)
    p = pltpu.roll(next8, 8 - sh, 0)
    bot = jnp.where(row8 >= 8 - sh, p, r[ts - 8:])
    return jnp.concatenate([r[:ts - 8], bot], axis=0)


def _conv_rows(cur, prev8, wrows, row8):
    k_w = len(wrows)
    out = None
    for j in range(k_w):
        t = _shift_down(cur, prev8, k_w - 1 - j, row8) * wrows[j]
        out = t if out is None else out + t
    return out


def _conv_specs(x, xoff, w, woff, ts, tc):
    r8 = ts // 8
    return [pl.BlockSpec((ts, tc), functools.partial(lambda i, j, o: (i, j + o), o=xoff)),
            pl.BlockSpec((8, tc), functools.partial(lambda i, j, o: (jnp.maximum(i * r8 - 1, 0), j + o), o=xoff)),
            pl.BlockSpec((w.shape[0], tc), functools.partial(lambda i, j, o: (0, j + o), o=woff))]


def _conv_post(srcs, post, ncol, tc, name, cots=None, ts=256, out_dtype=F32):
    S = srcs[0][0].shape[0]
    ns = len(srcs)
    bwd = cots is not None

    def kern(*refs):
        first = pl.program_id(0) == 0
        row8 = _iota((8, tc), 0)
        cs = []
        for s in range(ns):
            cur_ref, prev_ref, w_ref = refs[3 * s:3 * s + 3]
            prev = jnp.where(first, 0.0, prev_ref[...])
            wrows = [w_ref[j:j + 1, :] for j in range(w_ref.shape[0])]
            cs.append(_conv_rows(cur_ref[...], prev, wrows, row8))
        if bwd:
            g = refs[3 * ns][...]
            _, vjp = jax.vjp(lambda *c: post(*c), *cs)
            for o_ref, d in zip(refs[3 * ns + 1:], vjp(g)):
                o_ref[...] = d
        else:
            refs[3 * ns][...] = post(*cs).astype(refs[3 * ns].dtype)

    in_specs = []
    ops = []
    for (x, xoff, w, woff) in srcs:
        in_specs += _conv_specs(x, xoff, w, woff, ts, tc)
        ops += [x, x, w]
    o_spec = pl.BlockSpec((ts, tc), lambda i, j: (i, j))
    o_shape = jax.ShapeDtypeStruct((S, ncol * tc), F32)
    if bwd:
        in_specs.append(o_spec)
        ops.append(cots)
        out_specs, out_shape = [o_spec] * ns, [o_shape] * ns
    else:
        out_specs, out_shape = o_spec, jax.ShapeDtypeStruct((S, ncol * tc), out_dtype)
    return pl.pallas_call(
        kern, name=name, grid=(S // ts, ncol), in_specs=in_specs, out_specs=out_specs, out_shape=out_shape,
        compiler_params=_cparams(("parallel", "parallel")))(*ops)


def _conv_bwd(dc, x, xoff, w, woff, ncol, tc, name, ts=256):
    S = x.shape[0]
    k_w = w.shape[0]
    r8 = ts // 8
    nblk8 = S // 8
    nrow = S // ts

    def kern(dc_ref, dn_ref, x_ref, xp_ref, w_ref, dx_ref, dw_ref):
        i = pl.program_id(1)
        row8 = _iota((8, tc), 0)
        dcur = dc_ref[...]
        dnext = jnp.where(i == nrow - 1, 0.0, dn_ref[...])
        xcur = x_ref[...]
        xprev = jnp.where(i == 0, 0.0, xp_ref[...])

        @pl.when(i == 0)
        def _():
            dw_ref[...] = jnp.zeros_like(dw_ref)

        dx = None
        for j in range(k_w):
            sh = k_w - 1 - j
            wj = w_ref[j:j + 1, :]
            t = _shift_up(dcur, dnext, sh, row8) * wj
            dx = t if dx is None else dx + t
            dw_ref[j:j + 1, :] += jnp.sum(dcur * _shift_down(xcur, xprev, sh, row8), axis=0, keepdims=True)
        dx_ref[...] = dx.astype(dx_ref.dtype)

    in_specs = [pl.BlockSpec((ts, tc), lambda j, i: (i, j)),
                pl.BlockSpec((8, tc), lambda j, i: (jnp.minimum((i + 1) * r8, nblk8 - 1), j)),
                pl.BlockSpec((ts, tc), functools.partial(lambda j, i, o: (i, j + o), o=xoff)),
                pl.BlockSpec((8, tc), functools.partial(lambda j, i, o: (jnp.maximum(i * r8 - 1, 0), j + o), o=xoff)),
                pl.BlockSpec((k_w, tc), functools.partial(lambda j, i, o: (0, j + o), o=woff))]
    out_specs = [pl.BlockSpec((ts, tc), lambda j, i: (i, j)), pl.BlockSpec((k_w, tc), lambda j, i: (0, j))]
    out_shape = [jax.ShapeDtypeStruct((S, ncol * tc), BF16), jax.ShapeDtypeStruct((k_w, ncol * tc), F32)]
    return pl.pallas_call(
        kern, name=name, grid=(ncol, nrow), in_specs=in_specs, out_specs=out_specs, out_shape=out_shape,
        compiler_params=_cparams(("parallel", "arbitrary")))(dc, dc, x, x, w)


def _conv_post_bwd(srcs, post, ncol, tc, cot, name, ts=256):
    S = srcs[0][0].shape[0]
    ns = len(srcs)
    r8 = ts // 8
    nblk8 = S // 8
    nrow = S // ts

    def kern(*refs):
        i = pl.program_id(1)
        row8 = _iota((8, tc), 0)
        g_ref, gn_ref = refs[4 * ns:4 * ns + 2]
        outs = refs[4 * ns + 2:]
        xs, xps, ws, cs, cns = [], [], [], [], []
        for s in range(ns):
            cur_ref, prev_ref, next_ref, w_ref = refs[4 * s:4 * s + 4]
            xcur = cur_ref[...]
            xprev = jnp.where(i == 0, 0.0, prev_ref[...])
            wrows = [w_ref[j:j + 1, :] for j in range(w_ref.shape[0])]
            xs.append(xcur)
            xps.append(xprev)
            ws.append(wrows)
            cs.append(_conv_rows(xcur, xprev, wrows, row8))
            cns.append(_conv_rows(next_ref[...], xcur[ts - 8:], wrows, row8))
        _, vjp = jax.vjp(lambda *c: post(*c), *cs)
        dcs = vjp(g_ref[...])
        _, vjp_next = jax.vjp(lambda *c: post(*c), *cns)
        dcns = vjp_next(jnp.where(i == nrow - 1, 0.0, gn_ref[...]))
        for s in range(ns):
            dx_ref, dw_ref = outs[2 * s], outs[2 * s + 1]

            @pl.when(i == 0)
            def _(dw_ref=dw_ref):
                dw_ref[...] = jnp.zeros_like(dw_ref)

            k_w = len(ws[s])
            dx = None
            for j in range(k_w):
                sh = k_w - 1 - j
                t = _shift_up(dcs[s], dcns[s], sh, row8) * ws[s][j]
                dx = t if dx is None else dx + t
                dw_ref[j:j + 1, :] += jnp.sum(dcs[s] * _shift_down(xs[s], xps[s], sh, row8), axis=0, keepdims=True)
            dx_ref[...] = dx.astype(dx_ref.dtype)

    def nxt(i):
        return jnp.minimum((i + 1) * r8, nblk8 - 1)

    in_specs, ops = [], []
    for (x, xoff, w, woff) in srcs:
        in_specs += [pl.BlockSpec((ts, tc), functools.partial(lambda j, i, o: (i, j + o), o=xoff)),
                     pl.BlockSpec((8, tc), functools.partial(lambda j, i, o: (jnp.maximum(i * r8 - 1, 0), j + o),
                                                             o=xoff)),
                     pl.BlockSpec((8, tc), functools.partial(lambda j, i, o: (nxt(i), j + o), o=xoff)),
                     pl.BlockSpec((w.shape[0], tc), functools.partial(lambda j, i, o: (0, j + o), o=woff))]
        ops += [x, x, x, w]
    in_specs += [pl.BlockSpec((ts, tc), lambda j, i: (i, j)), pl.BlockSpec((8, tc), lambda j, i: (nxt(i), j))]
    ops += [cot, cot]
    out_specs, out_shape = [], []
    for (x, xoff, w, woff) in srcs:
        out_specs += [pl.BlockSpec((ts, tc), lambda j, i: (i, j)), pl.BlockSpec((w.shape[0], tc), lambda j, i: (0, j))]
        out_shape += [jax.ShapeDtypeStruct((S, ncol * tc), BF16), jax.ShapeDtypeStruct((w.shape[0], ncol * tc), F32)]
    return pl.pallas_call(
        kern, name=name, grid=(ncol, nrow), in_specs=in_specs, out_specs=out_specs, out_shape=out_shape,
        compiler_params=_cparams(("parallel", "arbitrary")))(*ops)


def _ret_tables(S):
    H, C, dh = 4, 128, 128
    lg = jnp.log1p(-jnp.exp2(-5.0 - jnp.arange(H, dtype=F32)))
    idx = jnp.arange(C, dtype=F32)
    diff = idx[:, None] - idx[None, :]
    causal = diff >= 0
    intra = jnp.where(causal, jnp.exp(lg[:, None, None] * jnp.where(causal, diff, 0.0)), 0.0)
    kdec = jnp.broadcast_to(jnp.exp(lg[:, None] * (C - 1 - idx))[:, :, None], (H, C, dh))
    qdec = jnp.broadcast_to(jnp.exp(lg[:, None] * (idx + 1))[:, :, None], (H, C, dh))
    cdec = jnp.broadcast_to(jnp.exp(lg * C)[:, None, None], (H, dh, dh))
    half = dh // 2
    inv = jnp.exp(-math.log(10000.0) * jnp.arange(half, dtype=F32) / half)
    ang = jnp.arange(S).astype(F32)[:, None] * inv[None, :]
    cos, sin = jnp.cos(ang), jnp.sin(ang)
    cosf = jnp.concatenate([cos, cos], axis=1)
    sinf = jnp.concatenate([-sin, sin], axis=1)
    return cosf, sinf, intra, kdec, qdec, cdec


def _ret_chunk(q, k, v, cosf, sinf, intra, kdec, qdec, cdec, state):
    hs = range(len(q))
    qr = [q[h] * cosf + _swap_halves(q[h]) * sinf for h in hs]
    kr = [(k[h] * cosf + _swap_halves(k[h]) * sinf) * (128 ** -0.5) for h in hs]
    scores = [mm_nt(qr[h], kr[h]) * intra[h] for h in hs]
    inner = [mm(scores[h], v[h]) for h in hs]
    kv = [mm_tn(kr[h] * kdec[h], v[h]) for h in hs]
    cross = [mm(qr[h] * qdec[h], state[h]) for h in hs]
    return [inner[h] + cross[h] for h in hs], [state[h] * cdec[h] + kv[h] for h in hs]


RET_H = 4


def _ret_call(proj, tabs, states=None, do=None):
    S = proj.shape[0]
    N = S // 128
    bwd = do is not None

    def nn(n):
        return N - 1 - n if bwd else n

    qkv_spec = pl.BlockSpec((128, 3 * 512), lambda n: (nn(n), 0))
    pos = pl.BlockSpec((128, 128), lambda n: (nn(n), 0))
    tab = pl.BlockSpec((RET_H, 128, 128), lambda n: (0, 0, 0))
    st_spec = pl.BlockSpec((None, RET_H, 128, 128), lambda n: (nn(n), 0, 0, 0))
    o_spec = pl.BlockSpec((128, 512), lambda n: (nn(n), 0))

    def kern(*refs):
        x_ref, c_ref, s_ref, i_ref, kd_ref, qd_ref, cd_ref = refs[:7]
        carry = refs[-1]
        heads = range(RET_H)

        @pl.when(pl.program_id(0) == 0)
        def _():
            carry[...] = jnp.zeros_like(carry)

        def cols(ref, off=0):
            return [ref[:, _hs(off + h)] for h in heads]

        def tabs_of(ref):
            return [ref[h] for h in heads]

        consts = (c_ref[...], s_ref[...], tabs_of(i_ref), tabs_of(kd_ref), tabs_of(qd_ref), tabs_of(cd_ref))
        qkv = (cols(x_ref), cols(x_ref, RET_H), cols(x_ref, 2 * RET_H))
        if bwd:
            sp_ref, do_ref = refs[7:9]
            outs = refs[9:12]
            _, vjp = jax.vjp(lambda q, k, v, s: _ret_chunk(q, k, v, *consts, s), *qkv, tabs_of(sp_ref))
            dq, dk, dv, ds = vjp((cols(do_ref), tabs_of(carry)))
            for h in heads:
                for o_ref, d in zip(outs, (dq[h], dk[h], dv[h])):
                    o_ref[:, _hs(h)] = d.astype(o_ref.dtype)
                carry[h] = ds[h]
        else:
            o_ref, sp_ref = refs[7:9]
            state = tabs_of(carry)
            out, new = _ret_chunk(*qkv, *consts, state)
            for h in heads:
                sp_ref[h] = state[h]
                o_ref[:, _hs(h)] = out[h]
                carry[h] = new[h]

    in_specs = [qkv_spec, pos, pos, tab, tab, tab, tab]
    if bwd:
        in_specs += [st_spec, o_spec]
        out_specs = [o_spec] * 3
        out_shape = [jax.ShapeDtypeStruct((S, 512), BF16)] * 3
        ops = (proj, *tabs, states, do)
    else:
        out_specs = [o_spec, st_spec]
        out_shape = [jax.ShapeDtypeStruct((S, 512), F32), jax.ShapeDtypeStruct((N, RET_H, 128, 128), F32)]
        ops = (proj, *tabs)
    return pl.pallas_call(
        kern, name="ret_bwd" if bwd else "ret_fwd", grid=(N,), in_specs=in_specs, out_specs=out_specs,
        out_shape=out_shape, scratch_shapes=[pltpu.VMEM((RET_H, 128, 128), F32)],
        compiler_params=_cparams(("arbitrary",)))(*ops)


GDN_C = 64
GDN_H = 8


def _unit_lower_inverse(a_mats, eye):
    p = [-a for a in a_mats]
    t = [eye + x for x in p]
    for _ in range(5):
        p = [mm3(x, x) for x in p]
        t = [mm3(y, eye + x) for y, x in zip(t, p)]
    return t


@jax.custom_vjp
def _known_inverse(a_mat, t_mat):
    return t_mat


def _known_inverse_f(a_mat, t_mat):
    return t_mat, t_mat


def _known_inverse_b(t_mat, g):
    return -mmh_tn(t_mat, mmh_nt(g, t_mat)), jnp.zeros_like(t_mat)


_known_inverse.defvjp(_known_inverse_f, _known_inverse_b)


def _gdn_intra(q, k, v, g_b, beta_b, t_known=None):
    c = GDN_C
    hs = range(len(q))
    q = [x * lax.rsqrt(jnp.sum(x * x, axis=-1, keepdims=True) + EPS) * (128 ** -0.5) for x in q]
    k = [x * lax.rsqrt(jnp.sum(x * x, axis=-1, keepdims=True) + EPS) for x in k]
    ri, ci = _iota((c, c), 0), _iota((c, c), 1)
    incl = ri >= ci
    strict = ri > ci
    eye = (ri == ci).astype(F32)
    lower = incl.astype(F32)
    ones = jnp.ones((c, c), F32)
    gc_b = [mmh(lower, g) for g in g_b]
    gl_b = [mmh(ones, g) for g in g_b]
    kb = [k[h] * beta_b[h] for h in hs]
    vb = [v[h] * beta_b[h] for h in hs]
    gcc = [g[:, :c] for g in gc_b]
    decay = [jnp.where(incl, jnp.exp(jnp.where(incl, g - g.T, 0.0)), 0.0) for g in gcc]
    a_mat = [jnp.where(strict, mm_nt(kb[h], k[h]) * decay[h], 0.0) for h in hs]
    if t_known is None:
        t_mat = _unit_lower_inverse(a_mat, eye)
    else:
        t_mat = [_known_inverse(a_mat[h], t_known[h]) for h in hs]
    egc = [jnp.exp(g) for g in gc_b]
    w = [mm(t_mat[h], kb[h] * egc[h]) for h in hs]
    u = [mm(t_mat[h], vb[h]) for h in hs]
    qk = [jnp.where(incl, mm_nt(q[h], k[h]) * decay[h], 0.0) for h in hs]
    q_dec = [q[h] * egc[h] for h in hs]
    k_dec = [k[h] * jnp.exp(gl_b[h] - gc_b[h]) for h in hs]
    return w, u, q_dec, k_dec, qk, t_mat


def _gdn_step(w, u, q_dec, k_dec, qk, g_b, state):
    hs = range(len(w))
    ones = jnp.ones((128, GDN_C), F32)
    gl_s = [mmh(ones, g) for g in g_b]
    ws = [mm(w[h], state[h]) for h in hs]
    qs = [mm(q_dec[h], state[h]) for h in hs]
    v_new = [u[h] - ws[h] for h in hs]
    o = [qs[h] + mm(qk[h], v_new[h]) for h in hs]
    new = [state[h] * jnp.exp(gl_s[h]) + mm_tn(k_dec[h], v_new[h]) for h in hs]
    return o, new


def _hs(h):
    return slice(h * 128, (h + 1) * 128)


def _gdn_intra_call(qkv, g_e, beta_e, cots=None):
    S = qkv.shape[0]
    N = S // GDN_C
    bwd = cots is not None
    row = pl.BlockSpec((GDN_C, 1024), lambda n: (n, 0))
    qkv_spec = pl.BlockSpec((GDN_C, 3072), lambda n: (n, 0))
    qk_spec = pl.BlockSpec((GDN_H, GDN_C, GDN_C), lambda n: (0, n, 0))

    def kern(*refs):
        x_ref, g_ref, b_ref = refs[:3]
        heads = range(GDN_H)

        def cols(ref, off=0):
            return [ref[:, _hs(off + h)] for h in heads]

        args = (cols(x_ref), cols(x_ref, 8), cols(x_ref, 16), cols(g_ref), cols(b_ref))
        if bwd:
            dw_ref, du_ref, dqd_ref, dkd_ref, dqk_ref, dgadd_ref, t_ref = refs[3:10]
            outs = refs[10:]
            t_known = [t_ref[h] for h in heads]
            _, vjp = jax.vjp(lambda *a: _gdn_intra(*a, t_known=t_known)[:5], *args)
            dq, dk, dv, dg, db = vjp((cols(dw_ref), cols(du_ref), cols(dqd_ref), cols(dkd_ref),
                                      [dqk_ref[h] for h in heads]))
            dgadd = cols(dgadd_ref)
            for h in heads:
                for o_ref, d in zip(outs, (dq[h], dk[h], dv[h], dg[h] + dgadd[h], db[h])):
                    o_ref[:, _hs(h)] = d
        else:
            w, u, qd, kd, qk, t_mat = _gdn_intra(*args)
            for h in heads:
                for o_ref, o in zip(refs[3:7], (w[h], u[h], qd[h], kd[h])):
                    o_ref[:, _hs(h)] = o
                refs[7][h] = qk[h]
                refs[8][h] = t_mat[h]

    big = jax.ShapeDtypeStruct((S, 1024), F32)
    sq = jax.ShapeDtypeStruct((GDN_H, S, GDN_C), F32)
    if bwd:
        in_specs = [qkv_spec, row, row, row, row, row, row, qk_spec, row, qk_spec]
        out_specs, out_shape = [row] * 5, [big] * 5
        ops = (qkv, g_e, beta_e) + tuple(cots)
    else:
        in_specs = [qkv_spec, row, row]
        out_specs = [row] * 4 + [qk_spec, qk_spec]
        out_shape = [big] * 4 + [sq, sq]
        ops = (qkv, g_e, beta_e)
    return pl.pallas_call(
        kern, name="gdn_intra_bwd" if bwd else "gdn_intra", grid=(N,), in_specs=in_specs, out_specs=out_specs,
        out_shape=out_shape, compiler_params=_cparams(("parallel",)))(*ops)


def _gdn_pass(w, u, qd, kd, qk, g_e, states=None, do=None):
    S = w.shape[0]
    N = S // GDN_C
    bwd = do is not None

    def nn(n):
        return N - 1 - n if bwd else n

    row = pl.BlockSpec((GDN_C, 1024), lambda n: (nn(n), 0))
    qk_spec = pl.BlockSpec((GDN_H, GDN_C, GDN_C), lambda n: (0, nn(n), 0))
    st_spec = pl.BlockSpec((None, GDN_H, 128, 128), lambda n: (nn(n), 0, 0, 0))

    def kern(*refs):
        w_ref, u_ref, qd_ref, kd_ref, qk_ref, g_ref = refs[:6]
        carry = refs[-1]

        @pl.when(pl.program_id(0) == 0)
        def _():
            carry[...] = jnp.zeros_like(carry)

        heads = range(GDN_H)

        def cols(ref):
            return [ref[:, _hs(h)] for h in heads]

        args = (cols(w_ref), cols(u_ref), cols(qd_ref), cols(kd_ref), [qk_ref[h] for h in heads], cols(g_ref))
        if bwd:
            sp_ref, do_ref = refs[6:8]
            outs = refs[8:14]
            _, vjp = jax.vjp(_gdn_step, *args, [sp_ref[h] for h in heads])
            dw, du, dqd, dkd, dqk, dg, ds = vjp((cols(do_ref), [carry[h] for h in heads]))
            for h in heads:
                for o_ref, d in zip(outs[:4], (dw[h], du[h], dqd[h], dkd[h])):
                    o_ref[:, _hs(h)] = d
                outs[4][h] = dqk[h]
                outs[5][:, _hs(h)] = dg[h]
                carry[h] = ds[h]
        else:
            o_ref, sp_ref = refs[6:8]
            state = [carry[h] for h in heads]
            o, new = _gdn_step(*args, state)
            for h in heads:
                sp_ref[h] = state[h]
                o_ref[:, _hs(h)] = o[h]
                carry[h] = new[h]

    big = jax.ShapeDtypeStruct((S, 1024), F32)
    in_specs = [row, row, row, row, qk_spec, row]
    if bwd:
        in_specs += [st_spec, row]
        out_specs = [row] * 4 + [qk_spec, row]
        out_shape = [big] * 4 + [jax.ShapeDtypeStruct((GDN_H, S, GDN_C), F32), big]
        ops = (w, u, qd, kd, qk, g_e, states, do)
    else:
        out_specs = [row, st_spec]
        out_shape = [big, jax.ShapeDtypeStruct((N, GDN_H, 128, 128), F32)]
        ops = (w, u, qd, kd, qk, g_e)
    return pl.pallas_call(
        kern, name="gdn_pass_bwd" if bwd else "gdn_pass", grid=(N,), in_specs=in_specs, out_specs=out_specs,
        out_shape=out_shape, scratch_shapes=[pltpu.VMEM((GDN_H, 128, 128), F32)],
        compiler_params=_cparams(("arbitrary",)))(*ops)


def _s5_prep_fn(lr, li, ldt, br, bi, cr, ci):
    dt = jnp.exp(ldt)
    mag = jnp.exp(lr * dt)
    a_re = mag * jnp.cos(li * dt)
    a_im = mag * jnp.sin(li * dt)
    den = lr * lr + li * li
    z_re = ((a_re - 1.0) * lr + a_im * li) / den
    z_im = (a_im * lr - (a_re - 1.0) * li) / den
    e1 = ((_iota((512, 32), 0) >> 4) == _iota((512, 32), 1)).astype(F32)
    zr_e = mmh(e1, z_re)
    zi_e = mmh(e1, z_im)
    bb_re = zr_e * br - zi_e * bi
    bb_im = zr_e * bi + zi_e * br
    t1 = ((_iota((64, 2048), 1) & 63) == _iota((64, 2048), 0)).astype(F32)
    m1 = (_iota((512, 2048), 0) >> 4) == (_iota((512, 2048), 1) >> 6)
    bd_re = jnp.where(m1, mmh(bb_re, t1), 0.0)
    bd_im = jnp.where(m1, mmh(bb_im, t1), 0.0)
    t2 = ((_iota((16, 512), 1) & 15) == _iota((16, 512), 0)).astype(F32)
    m2 = (_iota((2048, 512), 0) >> 6) == (_iota((2048, 512), 1) >> 4)
    cd_re = jnp.where(m2, mmh(cr, t2), 0.0)
    cd_im = jnp.where(m2, mmh(ci, t2), 0.0)
    return a_re, a_im, bd_re, bd_im, cd_re, cd_im


_PREP_OUT = [(32, 64), (32, 64), (512, 2048), (512, 2048), (2048, 512), (2048, 512)]


def _s5_prep(params, cots=None):
    bwd = cots is not None

    def kern(*refs):
        vals = [r[...] for r in refs[:7]]
        if bwd:
            gs = tuple(r[...] for r in refs[7:13])
            _, vjp = jax.vjp(_s5_prep_fn, *vals)
            for o_ref, d in zip(refs[13:], vjp(gs)):
                o_ref[...] = d
        else:
            for o_ref, o in zip(refs[7:], _s5_prep_fn(*vals)):
                o_ref[...] = o

    if bwd:
        out_shape = [jax.ShapeDtypeStruct(p.shape, F32) for p in params]
        ops = list(params) + list(cots)
    else:
        out_shape = [jax.ShapeDtypeStruct(s, F32) for s in _PREP_OUT]
        ops = list(params)
    return pl.pallas_call(kern, name="s5_prep_bwd" if bwd else "s5_prep", out_shape=out_shape,
                          compiler_params=_cparams())(*ops)


def _cmul(ar, ai, br, bi):
    return ar * br - ai * bi, ar * bi + ai * br


def _power_table(ar, ai, row8, descending):
    pr, pi = ar, ai
    tr = jnp.zeros(row8.shape, F32)
    ti = jnp.zeros(row8.shape, F32)
    for n in range(8):
        r = 7 - n if descending else n
        tr = jnp.where(row8 == r, pr, tr)
        ti = jnp.where(row8 == r, pi, ti)
        if n < 7:
            pr, pi = _cmul(pr, pi, ar, ai)
    return tr, ti


def _tile_scan(xr, xi, pows, row8, up):
    for d, (pr, pi) in zip((1, 2, 4), pows):
        if up:
            sr = jnp.where(row8 < 8 - d, pltpu.roll(xr, 8 - d, 0), 0.0)
            si = jnp.where(row8 < 8 - d, pltpu.roll(xi, 8 - d, 0), 0.0)
        else:
            sr = jnp.where(row8 >= d, pltpu.roll(xr, d, 0), 0.0)
            si = jnp.where(row8 >= d, pltpu.roll(xi, d, 0), 0.0)
        mr, mi = _cmul(pr, pi, sr, si)
        xr, xi = xr + mr, xi + mi
    return xr, xi


def _pick_row(x, row8, r):
    return jnp.sum(jnp.where(row8 == r, x, 0.0), axis=0, keepdims=True)


SCAN_LB = 512
SCAN_TS = 512


def _scan_fwd(bu_re, bu_im, a_re, a_im):
    S, L = bu_re.shape
    ts, lb = min(SCAN_TS, S), SCAN_LB
    nt = ts // 8

    def kern(br_ref, bi_ref, ar_ref, ai_ref, or_ref, oi_ref, cr_ref, ci_ref):
        @pl.when(pl.program_id(1) == 0)
        def _():
            cr_ref[...] = jnp.zeros_like(cr_ref)
            ci_ref[...] = jnp.zeros_like(ci_ref)

        row8 = _iota((8, lb), 0)
        ar, ai = ar_ref[...], ai_ref[...]
        a2 = _cmul(ar, ai, ar, ai)
        a4 = _cmul(*a2, *a2)
        pows = ((ar, ai), a2, a4)
        tr, ti = _power_table(ar, ai, row8, False)

        def body(i, carry):
            cr, ci = carry
            off = pl.multiple_of(i * 8, 8)
            xr, xi = _tile_scan(br_ref[pl.ds(off, 8), :], bi_ref[pl.ds(off, 8), :], pows, row8, False)
            mr, mi = _cmul(tr, ti, cr, ci)
            xr, xi = xr + mr, xi + mi
            or_ref[pl.ds(off, 8), :] = xr
            oi_ref[pl.ds(off, 8), :] = xi
            return _pick_row(xr, row8, 7), _pick_row(xi, row8, 7)

        cr, ci = lax.fori_loop(0, nt, body, (cr_ref[...], ci_ref[...]))
        cr_ref[...] = cr
        ci_ref[...] = ci

    blk = pl.BlockSpec((ts, lb), lambda j, i: (i, j))
    par = pl.BlockSpec((1, lb), lambda j, i: (0, j))
    return pl.pallas_call(
        kern, name="s5_scan_fwd", grid=(L // lb, S // ts), in_specs=[blk, blk, par, par], out_specs=[blk, blk],
        out_shape=[jax.ShapeDtypeStruct((S, L), F32)] * 2,
        scratch_shapes=[pltpu.VMEM((1, lb), F32), pltpu.VMEM((1, lb), F32)],
        compiler_params=_cparams(("parallel", "arbitrary")))(bu_re, bu_im, a_re, a_im)


def _scan_bwd(dst_re, dst_im, st_re, st_im, a_re, a_im):
    S, L = dst_re.shape
    ts, lb = min(SCAN_TS, S), SCAN_LB
    nt = ts // 8
    nb = S // ts
    r8 = ts // 8

    def kern(dr_ref, di_ref, sr_ref, si_ref, pr_ref, pi_ref, ar_ref, ai_ref, gr_ref, gi_ref, dar_ref, dai_ref,
             cr_ref, ci_ref):
        step = pl.program_id(1)
        blk = nb - 1 - step

        @pl.when(step == 0)
        def _():
            cr_ref[...] = jnp.zeros_like(cr_ref)
            ci_ref[...] = jnp.zeros_like(ci_ref)
            dar_ref[...] = jnp.zeros_like(dar_ref)
            dai_ref[...] = jnp.zeros_like(dai_ref)

        row8 = _iota((8, lb), 0)
        ar, ai = ar_ref[...], ai_ref[...]
        nai = -ai
        a2 = _cmul(ar, nai, ar, nai)
        a4 = _cmul(*a2, *a2)
        pows = ((ar, nai), a2, a4)
        tr, ti = _power_table(ar, nai, row8, True)
        halo_r = jnp.where(blk == 0, 0.0, pr_ref[...])
        halo_i = jnp.where(blk == 0, 0.0, pi_ref[...])

        def body(n, carry):
            cr, ci, acc_r, acc_i = carry
            i = nt - 1 - n
            off = pl.multiple_of(i * 8, 8)
            gr, gi = _tile_scan(dr_ref[pl.ds(off, 8), :], di_ref[pl.ds(off, 8), :], pows, row8, True)
            mr, mi = _cmul(tr, ti, cr, ci)
            gr, gi = gr + mr, gi + mi
            gr_ref[pl.ds(off, 8), :] = gr
            gi_ref[pl.ds(off, 8), :] = gi
            poff = pl.multiple_of(jnp.maximum(i - 1, 0) * 8, 8)
            before_r = jnp.where(i == 0, halo_r, sr_ref[pl.ds(poff, 8), :])
            before_i = jnp.where(i == 0, halo_i, si_ref[pl.ds(poff, 8), :])
            last_r = _pick_row(before_r, row8, 7)
            last_i = _pick_row(before_i, row8, 7)
            spr = jnp.where(row8 >= 1, pltpu.roll(sr_ref[pl.ds(off, 8), :], 1, 0), last_r)
            spi = jnp.where(row8 >= 1, pltpu.roll(si_ref[pl.ds(off, 8), :], 1, 0), last_i)
            acc_r = acc_r + gr * spr + gi * spi
            acc_i = acc_i + gi * spr - gr * spi
            return _pick_row(gr, row8, 0), _pick_row(gi, row8, 0), acc_r, acc_i

        zero = jnp.zeros((8, lb), F32)
        cr, ci, acc_r, acc_i = lax.fori_loop(0, nt, body, (cr_ref[...], ci_ref[...], zero, zero))
        cr_ref[...] = cr
        ci_ref[...] = ci
        dar_ref[...] += jnp.sum(acc_r, axis=0, keepdims=True)
        dai_ref[...] += jnp.sum(acc_i, axis=0, keepdims=True)

    blk = pl.BlockSpec((ts, lb), lambda j, i: (nb - 1 - i, j))
    halo = pl.BlockSpec((8, lb), lambda j, i: (jnp.maximum((nb - 1 - i) * r8 - 1, 0), j))
    par = pl.BlockSpec((1, lb), lambda j, i: (0, j))
    return pl.pallas_call(
        kern, name="s5_scan_bwd", grid=(L // lb, nb), in_specs=[blk, blk, blk, blk, halo, halo, par, par],
        out_specs=[blk, blk, par, par],
        out_shape=[jax.ShapeDtypeStruct((S, L), F32)] * 2 + [jax.ShapeDtypeStruct((1, L), F32)] * 2,
        scratch_shapes=[pltpu.VMEM((1, lb), F32), pltpu.VMEM((1, lb), F32)],
        compiler_params=_cparams(("parallel", "arbitrary")))(dst_re, dst_im, st_re, st_im, st_re, st_im, a_re, a_im)


def _loss_grad(x, target, gain, ts=256):
    S, D = x.shape

    def kern(x_ref, t_ref, g_ref, loss_ref, dx_ref, dg_ref):
        i = pl.program_id(0)
        tgt = t_ref[...]

        def f(xv, gv):
            err = _rms(xv, gv) - tgt
            return 0.5 * jnp.mean(err * err, axis=-1, keepdims=True)

        rowloss, vjp = jax.vjp(f, x_ref[...], g_ref[...])
        dx, dg = vjp(jnp.ones_like(rowloss))
        dx_ref[...] = dx

        @pl.when(i == 0)
        def _():
            loss_ref[...] = jnp.zeros_like(loss_ref)
            dg_ref[...] = jnp.zeros_like(dg_ref)

        loss_ref[...] += jnp.broadcast_to(jnp.sum(rowloss, axis=0, keepdims=True), loss_ref.shape)
        dg_ref[...] += dg

    row = pl.BlockSpec((ts, D), lambda i: (i, 0))
    return pl.pallas_call(
        kern, name="loss_grad", grid=(S // ts,), in_specs=[row, row, pl.BlockSpec((1, D), lambda i: (0, 0))],
        out_specs=[pl.BlockSpec((8, 128), lambda i: (0, 0)), row, pl.BlockSpec((1, D), lambda i: (0, 0))],
        out_shape=[jax.ShapeDtypeStruct((8, 128), F32), jax.ShapeDtypeStruct((S, D), F32),
                   jax.ShapeDtypeStruct((1, D), F32)],
        compiler_params=_cparams(("arbitrary",)))(x, target, gain)


def _rms_fwd(x, g, name):
    return _rowwise(_rms_fn, [_blk(x)], [g], [x.shape[1]], name, out_dtypes=[BF16])[0]


def _rms_bwd(x, g, dy, name, add=None):
    return _rowwise_bwd(_rms_fn, [_blk(x)], [g], [dy], name, adds=None if add is None else {0: add})


FFN_TC = 1408


def _common_fwd(x, mem, P, L):
    hx = _rms_fwd(x, P['xa_norm'], L + "xa_norm")
    q = _matmul(hx, P['xa_wq'], name=L + "xa_q")
    memn = _rms_fwd(mem, P['mem_norm'], L + "mem_norm")
    kv = _matmul(memn, P['xa_wkv'], name=L + "xa_kv")
    att = _rowwise(_xattn_fn, [_blk(q)], [kv], [1024], L + "xattn", out_dtypes=[BF16])[0]
    x2 = _matmul(att, P['xa_wo'], res=x, name=L + "xa_o")
    hf = _rms_fwd(x2, P['ffn_norm'], L + "ffn_norm")
    hu = _matmul(hf, P['ffn_w_up'], name=L + "ffn_up")
    cw = P['ffn_conv']
    act = _conv_post([(hu, 0, cw, 0), (hu, 2, cw, 2)], _ffn_post, 2, FFN_TC, L + "ffn_conv", out_dtype=BF16)
    x3 = _matmul(act, P['ffn_w_down'], res=x2, name=L + "ffn_down")
    return x3, (x, mem, hx, q, memn, kv, att, x2, hf, hu, act)


def _common_bwd(saved, dx3, P, L):
    x, mem, hx, q, memn, kv, att, x2, hf, hu, act = saved
    G = {}
    dact = _matmul(dx3, P['ffn_w_down'], "nt", name=L + "ffn_down_dx")
    G['ffn_w_down'] = _matmul(act, dx3, "tn", name=L + "ffn_down_dw")
    cw = P['ffn_conv']
    dhu_u, dcw_u, dhu_g, dcw_g = _conv_post_bwd([(hu, 0, cw, 0), (hu, 2, cw, 2)], _ffn_post, 2, FFN_TC, dact,
                                                L + "ffn_conv_bwd")
    G['ffn_conv'] = jnp.concatenate([dcw_u, dcw_g], axis=1)
    dhf = _matmul_cat([dhu_u, dhu_g], P['ffn_w_up'], "nt", name=L + "ffn_up_dx")
    G['ffn_w_up'] = jnp.concatenate([_matmul(hf, dhu_u, "tn", name=L + "ffn_up_dw_up"),
                                     _matmul(hf, dhu_g, "tn", name=L + "ffn_up_dw_gate")], axis=1)
    dx2, G['ffn_norm'] = _rms_bwd(x2, P['ffn_norm'], dhf, L + "ffn_norm_bwd", add=dx3)
    datt = _matmul(dx2, P['xa_wo'], "nt", name=L + "xa_o_dx")
    G['xa_wo'] = _matmul(att, dx2, "tn", name=L + "xa_o_dw")
    dq, dkv = _rowwise_bwd(_xattn_fn, [_blk(q)], [kv], [datt], L + "xattn_bwd", out_dtypes=[BF16])
    dhx = _matmul(dq, P['xa_wq'], "nt", name=L + "xa_q_dx")
    G['xa_wq'] = _matmul(hx, dq, "tn", name=L + "xa_q_dw")
    dmemn = _matmul(dkv, P['xa_wkv'], "nt", name=L + "xa_kv_dx")
    G['xa_wkv'] = _matmul(memn, dkv, "tn", name=L + "xa_kv_dw")
    _, G['mem_norm'] = _rms_bwd(mem, P['mem_norm'], dmemn, L + "mem_norm_bwd")
    dx, G['xa_norm'] = _rms_bwd(x, P['xa_norm'], dhx, L + "xa_norm_bwd", add=dx2)
    return dx, G


U_COLS = (2048, 512)


def _even_fwd(x, P):
    S = x.shape[0]
    h0 = _rms_fwd(x, P['mix_norm'], "l0_mix_norm")
    proj = _matmul(h0, P['w_in'], name="l0_in")
    tabs = _ret_tables(S)
    o_raw, rstates = _ret_call(proj, tabs)
    o = _rowwise(_ret_post_fn, [_blk(o_raw), _blk(proj, 512, 3)], [P['ret_norm']], [512], "l0_ret_post",
                 out_dtypes=[BF16])[0]
    prep_in = (P['s5_lambda_re'], P['s5_lambda_im'], P['s5_log_dt'], P['s5_b_re'], P['s5_b_im'], P['s5_c_re'],
               P['s5_c_im'])
    a_re, a_im, bd_re, bd_im, cd_re, cd_im = _s5_prep(prep_in)
    a_re_f, a_im_f = a_re.reshape(1, 2048), a_im.reshape(1, 2048)
    bu_re = _matmul(proj, bd_re, name="l0_s5_bu_re", a_cols=U_COLS)
    bu_im = _matmul(proj, bd_im, name="l0_s5_bu_im", a_cols=U_COLS)
    st_re, st_im = _scan_fwd(bu_re, bu_im, a_re_f, a_im_f)
    y1 = _matmul(st_re, cd_re, name="l0_s5_y_re")
    y2 = _matmul(st_im, cd_im, name="l0_s5_y_im")
    yg = _rowwise(_s5_post_fn, [_blk(y1), _blk(y2), _blk(proj, 512, 4)],
                  [P['s5_d'], P['s5_w_glu'], P['s5_b_glu']], [512], "l0_s5_post", out_dtypes=[BF16])[0]
    x1 = _matmul_cat([o, yg], P['w_out'], "nn", res=x, name="l0_out")
    saved = (x, h0, proj, tabs, o_raw, rstates, prep_in, a_re_f, a_im_f, bd_re, bd_im, cd_re, cd_im, st_re, st_im,
             y1, y2, o, yg)
    return x1, saved


def _even_bwd(saved, dx1, P):
    (x, h0, proj, tabs, o_raw, rstates, prep_in, a_re_f, a_im_f, bd_re, bd_im, cd_re, cd_im, st_re, st_im, y1, y2,
     o, yg) = saved
    G = {}
    dmerged = _matmul(dx1, P['w_out'], "nt", name="l0_out_dx")
    G['w_out'] = jnp.concatenate([_matmul(o, dx1, "tn", name="l0_out_dw_ret"),
                                  _matmul(yg, dx1, "tn", name="l0_out_dw_s5")], axis=0)
    do_raw, dgate, G['ret_norm'] = _rowwise_bwd(
        _ret_post_fn, [_blk(o_raw), _blk(proj, 512, 3)], [P['ret_norm']], [_blk(dmerged, 512, 0)], "l0_ret_post_bwd",
        out_dtypes=[F32, BF16])
    dq, dk, dv = _ret_call(proj, tabs, states=rstates, do=do_raw)
    dy1, dy2, du_a, G['s5_d'], G['s5_w_glu'], G['s5_b_glu'] = _rowwise_bwd(
        _s5_post_fn, [_blk(y1), _blk(y2), _blk(proj, 512, 4)], [P['s5_d'], P['s5_w_glu'], P['s5_b_glu']],
        [_blk(dmerged, 512, 1)], "l0_s5_post_bwd", out_dtypes=[BF16, BF16, F32])
    dst_re = _matmul(dy1, cd_re, "nt", name="l0_s5_y_re_dx")
    dcd_re = _matmul(st_re, dy1, "tn", name="l0_s5_y_re_dw")
    dst_im = _matmul(dy2, cd_im, "nt", name="l0_s5_y_im_dx")
    dcd_im = _matmul(st_im, dy2, "tn", name="l0_s5_y_im_dw")
    dbu_re, dbu_im, da_re, da_im = _scan_bwd(dst_re, dst_im, st_re, st_im, a_re_f, a_im_f)
    du = _matmul(dbu_re, bd_re, "nt", res=du_a, name="l0_s5_bu_re_dx")
    du = _matmul(dbu_im, bd_im, "nt", res=du, name="l0_s5_bu_im_dx", out_dtype=BF16)
    dbd_re = _matmul(proj, dbu_re, "tn", name="l0_s5_bu_re_dw", a_cols=U_COLS)
    dbd_im = _matmul(proj, dbu_im, "tn", name="l0_s5_bu_im_dw", a_cols=U_COLS)
    dprep = _s5_prep(prep_in, cots=(da_re.reshape(32, 64), da_im.reshape(32, 64), dbd_re, dbd_im, dcd_re, dcd_im))
    for n, d in zip(('s5_lambda_re', 's5_lambda_im', 's5_log_dt', 's5_b_re', 's5_b_im', 's5_c_re', 's5_c_im'), dprep):
        G[n] = d
    pieces = [dq, dk, dv, dgate, du]
    dh0 = _matmul_cat(pieces, P['w_in'], "nt", name="l0_in_dx")
    G['w_in'] = jnp.concatenate([_matmul(h0, p, "tn", name="l0_in_dw_%d" % n) for n, p in enumerate(pieces)], axis=1)
    dx, G['mix_norm'] = _rms_bwd(x, P['mix_norm'], dh0, "l0_mix_norm_bwd", add=dx1)
    return dx, G


def _odd_fwd(x, P):
    h1 = _rms_fwd(x, P['mix_norm'], "l1_mix_norm")
    pm = _matmul(h1, P['w_main'], name="l1_in_main")
    pt = _matmul(h1, P['w_tail'], name="l1_in_tail")
    qkv = _conv_post([(pm, 0, P['conv'], 0)], _silu, 3, 1024, "l1_conv")
    g_e, beta_e = _rowwise(_gdn_gates_fn, [_blk(pt)], [P['a_log_p'], P['dtb_p']], [1024, 1024], "l1_gdn_gates")
    w, u, qd, kd, qk, tinv = _gdn_intra_call(qkv, g_e, beta_e)
    o_raw, gstates = _gdn_pass(w, u, qd, kd, qk, g_e)
    og = _rowwise(_gdn_post_fn, [_blk(o_raw), _blk(pm, 1024, 3)], [P['o_norm']], [1024], "l1_gdn_post",
                  out_dtypes=[BF16])[0]
    x1 = _matmul(og, P['w_out'], res=x, name="l1_out")
    return x1, (x, h1, pm, pt, qkv, g_e, beta_e, w, u, qd, kd, qk, tinv, o_raw, gstates, og)


def _odd_bwd(saved, dx1, P):
    x, h1, pm, pt, qkv, g_e, beta_e, w, u, qd, kd, qk, tinv, o_raw, gstates, og = saved
    G = {}
    dog = _matmul(dx1, P['w_out'], "nt", name="l1_out_dx")
    G['w_out'] = _matmul(og, dx1, "tn", name="l1_out_dw")
    do_raw, dz, G['o_norm'] = _rowwise_bwd(_gdn_post_fn, [_blk(o_raw), _blk(pm, 1024, 3)], [P['o_norm']], [dog],
                                           "l1_gdn_post_bwd", out_dtypes=[F32, BF16])
    dw, du, dqd, dkd, dqk, dg_pass = _gdn_pass(w, u, qd, kd, qk, g_e, states=gstates, do=do_raw)
    dqkv = _gdn_intra_call(qkv, g_e, beta_e, cots=(dw, du, dqd, dkd, dqk, dg_pass, tinv))
    dg_e, dbeta_e = dqkv[3], dqkv[4]
    dpt, G['a_log_p'], G['dtb_p'] = _rowwise_bwd(_gdn_gates_fn, [_blk(pt)], [P['a_log_p'], P['dtb_p']],
                                                 [dg_e, dbeta_e], "l1_gdn_gates_bwd", out_dtypes=[BF16])
    pieces, dcw = [], []
    for part in range(3):
        dxp, dwp = _conv_post_bwd([(pm, part, P['conv'], part)], _silu, 1, 1024, dqkv[part],
                                  "l1_conv_bwd_%d" % part)
        pieces.append(dxp)
        dcw.append(dwp)
    G['conv'] = jnp.concatenate(dcw, axis=1)
    pieces += [dz, dpt]
    dh1 = _matmul_cat(pieces, P['w_all'], "nt", name="l1_in_dx")
    G['w_all'] = jnp.concatenate([_matmul(h1, p, "tn", name="l1_in_dw_%d" % n) for n, p in enumerate(pieces)], axis=1)
    dx, G['mix_norm'] = _rms_bwd(x, P['mix_norm'], dh1, "l1_mix_norm_bwd", add=dx1)
    return dx, G


def _row(v):
    return v.reshape(1, -1)


def _local_step(x, mem, target, W, later_weights=None, early_grads=None):
    P0 = {
        'mix_norm': _row(W['l0_mix_norm']), 'w_in': W['l0_w_in'], 'ret_norm': _row(W['l0_ret_norm']),
        's5_lambda_re': W['l0_s5_lambda_re'], 's5_lambda_im': W['l0_s5_lambda_im'],
        's5_log_dt': W['l0_s5_log_dt'].reshape(32, 1),
        's5_b_re': W['l0_s5_b_re'].reshape(512, 64), 's5_b_im': W['l0_s5_b_im'].reshape(512, 64),
        's5_c_re': W['l0_s5_c_re'].reshape(2048, 16), 's5_c_im': W['l0_s5_c_im'].reshape(2048, 16),
        's5_d': _row(W['l0_s5_d']), 's5_w_glu': W['l0_s5_w_glu'].astype(F32), 's5_b_glu': _row(W['l0_s5_b_glu']),
        'w_out': W['l0_w_out'],
    }
    def common(L):
        return {'xa_norm': _row(W[L + 'xa_norm']), 'mem_norm': _row(W[L + 'mem_norm']), 'xa_wq': W[L + 'xa_wq'],
                'xa_wkv': W[L + 'xa_wkv'], 'xa_wo': W[L + 'xa_wo'], 'ffn_norm': _row(W[L + 'ffn_norm']),
                'ffn_w_up': W[L + 'ffn_w_up'], 'ffn_conv': W[L + 'ffn_conv'], 'ffn_w_down': W[L + 'ffn_w_down']}

    x1, s_even = _even_fwd(x, P0)
    if later_weights is not None:
        W = dict(W, **later_weights('l0_common', x1))
    C0 = common('l0_')
    x3, s_c0 = _common_fwd(x1, mem, C0, "l0_")

    if later_weights is not None:
        W = dict(W, **later_weights('l1', x3))
    w_in1 = W['l1_w_in']
    pad8 = jnp.zeros((8,), F32)
    w_all = jnp.pad(w_in1, ((0, 0), (0, 112)))
    P1 = {
        'mix_norm': _row(W['l1_mix_norm']), 'w_main': w_in1[:, :4096], 'w_tail': w_all[:, 4096:], 'w_all': w_all,
        'conv': W['l1_conv'],
        'a_log_p': _row(jnp.concatenate([pad8, W['l1_a_log'], jnp.zeros((112,), F32)])),
        'dtb_p': _row(jnp.concatenate([pad8, W['l1_dt_bias'], jnp.zeros((112,), F32)])),
        'o_norm': _row(W['l1_o_norm']), 'w_out': W['l1_w_out'],
    }
    C1 = common('l1_')
    x4, s_odd = _odd_fwd(x3, P1)
    x6, s_c1 = _common_fwd(x4, mem, C1, "l1_")
    loss_tile, dx6, d_final = _loss_grad(x6, target, _row(W['final_norm']))

    G = {'final_norm': d_final.reshape(-1)}
    dx4, g = _common_bwd(s_c1, dx6, C1, "l1_")
    for k, v in g.items():
        G['l1_' + k] = v
    dx3, g = _odd_bwd(s_odd, dx4, P1)
    G['l1_mix_norm'] = g['mix_norm']
    G['l1_w_in'] = g['w_all'][:, :4112]
    G['l1_conv'] = g['conv']
    G['l1_a_log'] = g['a_log_p'][0, 8:16]
    G['l1_dt_bias'] = g['dtb_p'][0, 8:16]
    G['l1_o_norm'] = g['o_norm']
    G['l1_w_out'] = g['w_out']
    if early_grads is not None:
        zero = early_grads('l1', G)
        C0 = dict(C0, ffn_w_down=C0['ffn_w_down'] + zero.astype(C0['ffn_w_down'].dtype))
    dx1, g = _common_bwd(s_c0, dx3, C0, "l0_")
    for k, v in g.items():
        G['l0_' + k] = v
    if early_grads is not None:
        zero = early_grads('l0_common', G)
        P0 = dict(P0, w_out=P0['w_out'] + zero.astype(P0['w_out'].dtype))
    dx0, g = _even_bwd(s_even, dx1, P0)
    for k, v in g.items():
        G['l0_' + k] = v
    return loss_tile, dx0, G


ANY = pl.BlockSpec(memory_space=pl.ANY)


def _place():
    return lax.axis_index("x"), lax.axis_index("y"), lax.axis_index("c")


def _my_chip():
    return 2 * lax.axis_index("x") + lax.axis_index("y")


def _chip_peers(x, y):
    return [(1 - x, y), (x, 1 - y), (1 - x, 1 - y)]


def _half(ref, mode, shard, j, h, split):
    r, w = shard
    rh = r // 2 if split else r
    h = h if split else 0
    if mode == 'row':
        return ref.at[pl.ds(j * r + h * rh, rh), :]
    if mode == 'col':
        return ref.at[pl.ds(h * rh, rh), pl.ds(j * w, w)]
    return ref.at[j, pl.ds(h * rh, rh), :]


def _place_shard(shard, mode, name):
    r, w = shard.shape
    dtype = BF16 if mode != 'tap' else shard.dtype
    if mode == 'tap':
        mode = 'slab'
    tr = _row_tile(r, w)
    nb = r // tr

    def kern(s_ref, o_ref):
        o_ref[...] = s_ref[...].astype(o_ref.dtype)

    if mode == 'row':
        full, o_spec = (4 * r, w), pl.BlockSpec((tr, w), lambda i: (_my_chip() * nb + i, 0))
    elif mode == 'col':
        full, o_spec = (r, 4 * w), pl.BlockSpec((tr, w), lambda i: (i, _my_chip()))
    else:
        full, o_spec = (4, r, w), pl.BlockSpec((None, tr, w), lambda i: (_my_chip(), i, 0))
    return pl.pallas_call(kern, name=name, grid=(nb,), in_specs=[pl.BlockSpec((tr, w), lambda i: (i, 0))],
                          out_specs=o_spec, out_shape=jax.ShapeDtypeStruct(full, dtype),
                          compiler_params=_cparams(("parallel",)))(shard)


def _gather_placed(fulls, modes, shards, splits):
    n = len(fulls)

    def body(*refs):
        outs = refs[n:2 * n]
        send_sems, recv_sems = refs[2 * n:]
        x, y, c = _place()
        peers = _chip_peers(x, y)
        me = 2 * x + y

        def win(a, j, h):
            return _half(outs[a], modes[a], shards[a], j, h, splits[a])

        def copy(a, k, j, h, to):
            return pltpu.make_async_remote_copy(src_ref=win(a, j, h), dst_ref=win(a, j, h),
                                                send_sem=send_sems.at[6 * a + k], recv_sem=recv_sems.at[6 * a + k],
                                                device_id=to, device_id_type=MESH)

        over_ici = [copy(a, k, me, c, (p[0], p[1], c)) for a in range(n) for k, p in enumerate(peers)]
        for cp in over_ici:
            cp.start()
        passed = []
        for a in range(n):
            for k, p in enumerate(peers):
                j = 2 * p[0] + p[1]
                copy(a, k, j, c, (p[0], p[1], c)).wait_recv()
                if splits[a]:
                    fwd = copy(a, 3 + k, j, c, (x, y, 1 - c))
                    fwd.start()
                    passed.append(fwd)
        for a in range(n):
            if splits[a]:
                for k, p in enumerate(peers):
                    copy(a, 3 + k, 2 * p[0] + p[1], 1 - c, (x, y, 1 - c)).wait_recv()
        for cp in over_ici + passed:
            cp.wait_send()

    return pl.pallas_call(
        body, name="gather_weights", in_specs=[ANY] * n, out_specs=[ANY] * n,
        out_shape=[jax.ShapeDtypeStruct(f.shape, f.dtype) for f in fulls],
        input_output_aliases={a: a for a in range(n)},
        scratch_shapes=[pltpu.SemaphoreType.DMA((6 * n,)), pltpu.SemaphoreType.DMA((6 * n,))],
    )(*fulls)


_FLIPS = [(dx, dy, dc) for dx in (0, 1) for dy in (0, 1) for dc in (0, 1) if (dx, dy, dc) != (0, 0, 0)]


def _send_other_half(gs, small, name):
    n = len(gs)

    def body(*refs):
        ins, outs = refs[:n], refs[n + 1:2 * n + 1]
        small_ref = refs[2 * n + 1]
        send_sems, recv_sems, small_send, small_recv = refs[2 * n + 2:]
        x, y, c = _place()
        me = 4 * x + 2 * y + c

        def peer(f):
            return (x ^ f[0], y ^ f[1], c ^ f[2])

        def small_copy(k, slab, to):
            return pltpu.make_async_remote_copy(src_ref=small_ref.at[slab], dst_ref=small_ref.at[slab],
                                                send_sem=small_send.at[k], recv_sem=small_recv.at[k], device_id=to,
                                                device_id_type=MESH)

        cps = []
        for a in range(n):
            rh = gs[a].shape[1] // 2
            cps.append(pltpu.make_async_remote_copy(
                src_ref=ins[a].at[:, pl.ds((1 - c) * rh, rh), :], dst_ref=outs[a], send_sem=send_sems.at[a],
                recv_sem=recv_sems.at[a], device_id=(x, y, 1 - c), device_id_type=MESH))
        smalls = [small_copy(k, me, peer(f)) for k, f in enumerate(_FLIPS)]
        for cp in cps + smalls:
            cp.start()
        for cp in cps:
            cp.wait()
        for k, f in enumerate(_FLIPS):
            p = peer(f)
            small_copy(k, 4 * p[0] + 2 * p[1] + p[2], p).wait_recv()
        for cp in smalls:
            cp.wait_send()

    outs = pl.pallas_call(
        body, name=name, in_specs=[ANY] * (n + 1), out_specs=[ANY] * (n + 1),
        out_shape=[jax.ShapeDtypeStruct((g.shape[0], g.shape[1] // 2, g.shape[2]), g.dtype) for g in gs]
        + [jax.ShapeDtypeStruct(small.shape, small.dtype)],
        input_output_aliases={n: n},
        scratch_shapes=[pltpu.SemaphoreType.DMA((n,)), pltpu.SemaphoreType.DMA((n,)),
                        pltpu.SemaphoreType.DMA((7,)), pltpu.SemaphoreType.DMA((7,))],
    )(*gs, small)
    return outs[:n], outs[n]


def _send_to_chips(ps, widths):
    n = len(ps)

    def body(*refs):
        ins, outs = refs[:n], refs[n:2 * n]
        send_sems, recv_sems = refs[2 * n:]
        x, y, c = _place()
        peers = _chip_peers(x, y)
        me = 2 * x + y

        def src(a, j):
            if ps[a].shape[0] == 4:
                return ins[a].at[j]
            return ins[a].at[0, :, pl.ds(j * widths[a], widths[a])]

        def copy(a, k, j, dst_slab, to):
            return pltpu.make_async_remote_copy(src_ref=src(a, j), dst_ref=outs[a].at[dst_slab],
                                                send_sem=send_sems.at[3 * a + k], recv_sem=recv_sems.at[3 * a + k],
                                                device_id=(to[0], to[1], c), device_id_type=MESH)

        sends = [copy(a, k, 2 * p[0] + p[1], me, p) for a in range(n) for k, p in enumerate(peers)]
        for cp in sends:
            cp.start()
        for a in range(n):
            for k, p in enumerate(peers):
                copy(a, k, me, 2 * p[0] + p[1], p).wait_recv()
        for cp in sends:
            cp.wait_send()

    return pl.pallas_call(
        body, name="send_to_chips", in_specs=[ANY] * n, out_specs=[ANY] * n,
        out_shape=[jax.ShapeDtypeStruct((4, p.shape[1], w), p.dtype) for p, w in zip(ps, widths)],
        scratch_shapes=[pltpu.SemaphoreType.DMA((3 * n,)), pltpu.SemaphoreType.DMA((3 * n,))],
    )(*ps)


def _share_halves(bufs, name):
    n = len(bufs)

    def body(*refs):
        outs = refs[n:2 * n]
        send_sems, recv_sems = refs[2 * n:]
        x, y, c = _place()
        sends, waits = [], []
        for a in range(n):
            rh = bufs[a].shape[0] // 2
            mine = outs[a].at[pl.ds(c * rh, rh), :]
            other = outs[a].at[pl.ds((1 - c) * rh, rh), :]
            sends.append(pltpu.make_async_remote_copy(src_ref=mine, dst_ref=mine, send_sem=send_sems.at[a],
                                                      recv_sem=recv_sems.at[a], device_id=(x, y, 1 - c),
                                                      device_id_type=MESH))
            waits.append(pltpu.make_async_remote_copy(src_ref=mine, dst_ref=other, send_sem=send_sems.at[a],
                                                      recv_sem=recv_sems.at[a], device_id=(x, y, 1 - c),
                                                      device_id_type=MESH))
        for cp in sends:
            cp.start()
        for cp in waits:
            cp.wait()

    return pl.pallas_call(
        body, name=name, in_specs=[ANY] * n, out_specs=[ANY] * n,
        out_shape=[jax.ShapeDtypeStruct(b.shape, b.dtype) for b in bufs],
        input_output_aliases={a: a for a in range(n)},
        scratch_shapes=[pltpu.SemaphoreType.DMA((n,)), pltpu.SemaphoreType.DMA((n,))],
    )(*bufs)


def _gather_all(mine):
    flips = [(dx, dy, dc) for dx in (0, 1) for dy in (0, 1) for dc in (0, 1) if (dx, dy, dc) != (0, 0, 0)]

    def body(x_ref, out_ref, send_sems, recv_sems, local_sem):
        x, y, c = _place()
        me = 4 * x + 2 * y + c

        def peer(f):
            return (x ^ f[0], y ^ f[1], c ^ f[2])

        def copy(k, slab, to):
            return pltpu.make_async_remote_copy(src_ref=x_ref, dst_ref=out_ref.at[slab], send_sem=send_sems.at[k],
                                                recv_sem=recv_sems.at[k], device_id=to, device_id_type=MESH)

        own = pltpu.make_async_copy(x_ref, out_ref.at[me], local_sem)
        own.start()
        sends = [copy(k, me, peer(f)) for k, f in enumerate(flips)]
        for s in sends:
            s.start()
        for k, f in enumerate(flips):
            p = peer(f)
            copy(k, 4 * p[0] + 2 * p[1] + p[2], p).wait_recv()
        for s in sends:
            s.wait_send()
        own.wait()

    return pl.pallas_call(
        body, name="gather_all", in_specs=[ANY], out_specs=ANY,
        out_shape=jax.ShapeDtypeStruct((8,) + mine.shape, mine.dtype),
        scratch_shapes=[pltpu.SemaphoreType.DMA((7,)), pltpu.SemaphoreType.DMA((7,)), pltpu.SemaphoreType.DMA],
    )(mine)


TILE_BYTES = 2 * 1024 * 1024


def _row_tile(rows, width=1024):
    for t in (512, 352, 256, 176, 128, 64, 32, 16, 8):
        if rows % t == 0 and t * width * 4 <= TILE_BYTES:
            return t
    return rows


def _pair_sum(g, got, name):
    ns, r, w = g.shape
    rh = r // 2
    tr = _row_tile(rh, w)
    nb = rh // tr

    def kern(g_ref, o_ref, out_ref):
        out_ref[...] = (g_ref[...] + o_ref[...]).astype(BF16)

    return pl.pallas_call(
        kern, name=name, grid=(ns, nb),
        in_specs=[pl.BlockSpec((None, tr, w), lambda j, i: (j, lax.axis_index("c") * nb + i, 0)),
                  pl.BlockSpec((None, tr, w), lambda j, i: (j, i, 0))],
        out_specs=pl.BlockSpec((None, tr, w), lambda j, i: (j, i, 0)),
        out_shape=jax.ShapeDtypeStruct((ns, rh, w), BF16),
        compiler_params=_cparams(("parallel", "parallel")))(g, got)


def _chip_sum(pair, recv, w, name):
    rh = pair.shape[1]
    tr = _row_tile(rh, w)
    nb = rh // tr

    def kern(own_ref, r1_ref, r2_ref, r3_ref, out_ref):
        acc = own_ref[...].astype(F32)
        for r_ref in (r1_ref, r2_ref, r3_ref):
            acc = acc + r_ref[...].astype(F32)
        out_ref[...] = acc

    if pair.shape[0] == 4:
        own_spec = pl.BlockSpec((None, tr, w), lambda i: (_my_chip(), i, 0))
    else:
        own_spec = pl.BlockSpec((None, tr, w), lambda i: (0, i, _my_chip()))
    recv_specs = [pl.BlockSpec((None, tr, w), functools.partial(lambda i, d: ((_my_chip() + d) % 4, i, 0), d=d))
                  for d in (1, 2, 3)]
    return pl.pallas_call(
        kern, name=name, grid=(nb,), in_specs=[own_spec] + recv_specs,
        out_specs=pl.BlockSpec((tr, w), lambda i: (lax.axis_index("c") * nb + i, 0)),
        out_shape=jax.ShapeDtypeStruct((2 * rh, w), F32), compiler_params=_cparams(("parallel",)))(pair, recv, recv, recv)


def _slab_sum(slabs, name):
    n, R, w = slabs.shape
    tr = _row_tile(R)

    def kern(s_ref, o_ref):
        acc = s_ref[0].astype(F32)
        for k in range(1, n):
            acc = acc + s_ref[k].astype(F32)
        o_ref[...] = acc

    return pl.pallas_call(
        kern, name=name, grid=(R // tr,), in_specs=[pl.BlockSpec((n, tr, w), lambda i: (0, i, 0))],
        out_specs=pl.BlockSpec((tr, w), lambda i: (i, 0)), out_shape=jax.ShapeDtypeStruct((R, w), F32),
        compiler_params=_cparams(("parallel",)))(slabs)


def _adamw(w, g, m, v, name):
    R, C = w.shape
    tr = _pick(R, (256, 128, 64, 32, 16, 8))

    def kern(w_ref, g_ref, m_ref, v_ref, d_ref, nm_ref, nv_ref):
        gv = g_ref[...]
        m2 = ADAM_B1 * m_ref[...] + (1.0 - ADAM_B1) * gv
        v2 = ADAM_B2 * v_ref[...] + (1.0 - ADAM_B2) * jnp.square(gv)
        m_hat = m2 / (1.0 - ADAM_B1 ** ADAM_STEP)
        v_hat = v2 / (1.0 - ADAM_B2 ** ADAM_STEP)
        d_ref[...] = -ADAM_LR * (m_hat / (jnp.sqrt(v_hat) + ADAM_EPS) + ADAM_WD * w_ref[...])
        nm_ref[...] = m2
        nv_ref[...] = v2

    spec = pl.BlockSpec((tr, C), lambda i: (i, 0))
    return pl.pallas_call(
        kern, name=name, grid=(R // tr,), in_specs=[spec] * 4, out_specs=[spec] * 3,
        out_shape=[jax.ShapeDtypeStruct((R, C), F32)] * 3, compiler_params=_cparams(("parallel",)))(w, g, m, v)


def _pack_small(vals):
    flat = jnp.concatenate([vals[n].astype(F32).reshape(-1) for n in SMALL_NAMES])
    rows = -(-flat.shape[0] // (8 * LANES)) * 8
    return jnp.pad(flat, (0, rows * LANES - flat.shape[0])).reshape(rows, LANES)


def _unpack_small(packed, shapes):
    flat = packed.reshape(-1)
    out = {}
    off = 0
    for n in SMALL_NAMES:
        size = int(np.prod(shapes[n]))
        out[n] = flat[off:off + size].reshape(shapes[n])
        off += size
    return out


HBM = pl.BlockSpec(memory_space=pltpu.HBM)
SEM = pl.BlockSpec(memory_space=pltpu.SEMAPHORE)
DATAFLOW = pltpu.SideEffectType.DATAFLOW_SIDE_EFFECTING


def _in_hbm(a):
    return pltpu.with_memory_space_constraint(a, pltpu.HBM)


def _split_copy_start(srcs, lands, copies, after, name):
    ns, nl = len(srcs), len(lands)
    ncopy = len(copies(list(srcs), list(lands), None, None, probe=True))

    def body(*refs):
        src_refs, land_refs = refs[:ns], refs[ns:ns + nl]
        send_sems, recv_sems = refs[ns + nl + 1:ns + nl + 3]
        token = refs[-1]
        for cp in copies(src_refs, land_refs, send_sems, recv_sems):
            cp.start()
        token[...] = jnp.zeros_like(token)

    outs = pl.pallas_call(
        body, name=name,
        out_shape=(pltpu.SemaphoreType.DMA((ncopy,)), pltpu.SemaphoreType.DMA((ncopy,)),
                   *[pltpu.HBM(a.shape, a.dtype) for a in srcs], *[pltpu.HBM(a.shape, a.dtype) for a in lands],
                   jax.ShapeDtypeStruct((8, 128), F32)),
        in_specs=[HBM] * (ns + nl) + [ANY],
        out_specs=(SEM, SEM, *[HBM] * (ns + nl), pl.BlockSpec(memory_space=pltpu.VMEM)),
        input_output_aliases={i: 2 + i for i in range(ns + nl)},
        compiler_params=pltpu.CompilerParams(has_side_effects=DATAFLOW),
    )(*[_in_hbm(a) for a in srcs], *[_in_hbm(a) for a in lands], after)
    return outs[0], outs[1], outs[2:2 + ns], outs[2 + ns:2 + ns + nl], outs[-1]


def _split_copy_wait(send_sems, recv_sems, srcs, lands, copies, after, name):
    ns, nl = len(srcs), len(lands)

    def body(*refs):
        src_refs, land_refs = refs[:ns], refs[ns:ns + nl]
        send_ref, recv_ref = refs[ns + nl:ns + nl + 2]
        for cp in copies(src_refs, land_refs, send_ref, recv_ref):
            cp.wait_send()
            cp.wait_recv()

    outs = pl.pallas_call(
        body, name=name,
        out_shape=tuple(pltpu.HBM(a.shape, a.dtype) for a in list(srcs) + list(lands)),
        in_specs=[HBM] * (ns + nl) + [SEM, SEM, ANY], out_specs=tuple([HBM] * (ns + nl)),
        input_output_aliases={i: i for i in range(ns + nl)},
        compiler_params=pltpu.CompilerParams(has_side_effects=DATAFLOW),
    )(*srcs, *lands, send_sems, recv_sems, after)
    return outs[:ns], outs[ns:]


def _matrix_mode(n):
    return 'slab' if n == 'l1_w_in' else ('row' if MATRICES[n] == 0 else 'col')


def _placed(A, names):
    modes = ['slab' if n in CONVS else _matrix_mode(n) for n in names]
    fulls = [_place_shard(A[n], 'tap' if n in CONVS else m, "place_" + n) for n, m in zip(names, modes)]
    return fulls, modes


def _assembled(names, modes, outs):
    return {n: jnp.concatenate([o[j] for j in range(4)], axis=1) if m == 'slab' else o
            for n, m, o in zip(names, modes, outs)}


def _gather_weights(A, names):
    fulls, modes = _placed(A, names)
    outs = _gather_placed(fulls, modes, [A[n].shape for n in names], [n not in CONVS for n in names])
    return _assembled(names, modes, outs)


def _whole_shard_copies(modes, shards):
    def copies(src_refs, land_refs, send_sems, recv_sems, probe=False):
        if probe:
            return [None] * (3 * len(land_refs))
        x, y, c = _place()
        me = 2 * x + y
        out = []
        for a, ref in enumerate(land_refs):
            for k, p in enumerate(_chip_peers(x, y)):
                out.append(pltpu.make_async_remote_copy(
                    src_ref=_half(ref, modes[a], shards[a], me, 0, False),
                    dst_ref=_half(ref, modes[a], shards[a], me, 0, False),
                    send_sem=send_sems.at[3 * a + k], recv_sem=recv_sems.at[3 * a + k],
                    device_id=(p[0], p[1], c), device_id_type=MESH))
        return out
    return copies


def _gather_weights_start(A, names, after, tag):
    fulls, modes = _placed(A, names)
    copies = _whole_shard_copies(modes, [A[n].shape for n in names])
    send_sems, recv_sems, _, lands, zeros = _split_copy_start([], fulls, copies, after, "gather_start_" + tag)
    return (send_sems, recv_sems, lands, copies, names, modes), zeros


def _gather_weights_wait(state, after, tag):
    send_sems, recv_sems, lands, copies, names, modes = state
    _, outs = _split_copy_wait(send_sems, recv_sems, [], lands, copies, after, "gather_wait_" + tag)
    return _assembled(names, modes, outs)


def _to_chips_copies(pair_shapes, widths):
    def copies(src_refs, land_refs, send_sems, recv_sems, probe=False):
        if probe:
            return [None] * (3 * len(land_refs))
        x, y, c = _place()
        me = 2 * x + y
        out = []
        for a, (src, land) in enumerate(zip(src_refs, land_refs)):
            for k, p in enumerate(_chip_peers(x, y)):
                j = 2 * p[0] + p[1]
                part = src.at[j] if pair_shapes[a][0] == 4 else src.at[0, :, pl.ds(j * widths[a], widths[a])]
                out.append(pltpu.make_async_remote_copy(
                    src_ref=part, dst_ref=land.at[me], send_sem=send_sems.at[3 * a + k],
                    recv_sem=recv_sems.at[3 * a + k], device_id=(p[0], p[1], c), device_id_type=MESH))
        return out
    return copies


def _reduce_begin(G, names, small, tag):
    gs, widths = [], []
    for n in names:
        g = G[n]
        mode = _matrix_mode(n)
        if mode == 'row':
            gs.append(g.reshape(4, g.shape[0] // 4, g.shape[1]))
            widths.append(g.shape[1])
        elif mode == 'col':
            gs.append(g[None])
            widths.append(g.shape[1] // 4)
        else:
            wd = g.shape[1] // 4
            gs.append(jnp.stack([g[:, j * wd:(j + 1) * wd] for j in range(4)]))
            widths.append(wd)
    got, small = _send_other_half(gs, small, "send_other_half_" + tag)
    pairs = [_pair_sum(g, o, "pair_sum_" + n) for n, g, o in zip(names, gs, got)]
    return pairs, widths, small


def _reduce_end(names, pairs, recv, widths, tag):
    halves = [_chip_sum(p, r, w, "chip_sum_" + n) for n, p, r, w in zip(names, pairs, recv, widths)]
    return dict(zip(names, _share_halves(halves, "share_halves_" + tag)))


def _small_slab(packed):
    me8 = 4 * lax.axis_index("x") + 2 * lax.axis_index("y") + lax.axis_index("c")
    return lax.dynamic_update_slice(jnp.zeros((8,) + packed.shape, F32), packed[None], (me8, 0, 0))


def kernel(*args):
    A = dict(zip(ARG_NAMES, args, strict=True))
    x, mem, target = A['x'][0], A['mem'][0], A['loss_target'][0]

    stages = {'l0_mixer': ['l0_w_in', 'l0_s5_w_glu', 'l0_w_out'],
              'l0_common': [n for n in MATRIX_NAMES if n.startswith(('l0_xa_', 'l0_ffn_'))],
              'l1': [n for n in MATRIX_NAMES if n.startswith('l1_')]}
    W = _gather_weights(A, stages['l0_mixer'] + list(CONVS))
    for n in SMALL_NAMES:
        if n not in CONVS:
            W[n] = A[n]
    flights = {}
    after = W['l0_w_in']
    for stage in ('l0_common', 'l1'):
        flights[stage], after = _gather_weights_start(A, stages[stage], after, stage)
    W['l0_mix_norm'] = W['l0_mix_norm'] + after[0, 0]

    reduce_state = {}

    def early_grads(stage, G):
        pairs, widths, _ = _reduce_begin(G, stages[stage], jnp.zeros((8, 8, LANES), F32), stage)
        copies = _to_chips_copies([p.shape for p in pairs], widths)
        lands = [lax.empty((4, p.shape[1], w), p.dtype) for p, w in zip(pairs, widths)]
        send_sems, recv_sems, pairs, lands, zeros = _split_copy_start(pairs, lands, copies, G['final_norm'],
                                                                      "reduce_start_" + stage)
        reduce_state[stage] = (send_sems, recv_sems, pairs, lands, copies, widths)
        return zeros[0, 0]

    loss_tile, grad_x, G = _local_step(
        x, mem, target, W, later_weights=lambda stage, after: _gather_weights_wait(flights[stage], after, stage),
        early_grads=early_grads)
    loss = lax.psum(loss_tile[0, 0], ("x", "y", "c"))

    g_mat = {}
    for stage in ('l1', 'l0_common'):
        send_sems, recv_sems, pairs, lands, copies, widths = reduce_state[stage]
        sent, recv = _split_copy_wait(send_sems, recv_sems, pairs, lands, copies, grad_x, "reduce_wait_" + stage)
        g_mat.update(_reduce_end(stages[stage], sent, recv, widths, stage))
    pairs, widths, g_small = _reduce_begin(G, stages['l0_mixer'],
                                           _small_slab(_pack_small({n: G[n] for n in SMALL_NAMES})), "l0_mixer")
    g_mat.update(_reduce_end(stages['l0_mixer'], pairs, _send_to_chips(pairs, widths), widths, "l0_mixer"))
    g_small = _unpack_small(_slab_sum(g_small, "sum_small"), {n: G[n].shape for n in SMALL_NAMES})
    me = 2 * lax.axis_index("x") + lax.axis_index("y")
    for n in CONVS:
        wd = A[n].shape[1]
        g_small[n] = lax.dynamic_slice_in_dim(g_small[n], me * wd, wd, axis=1)
    flat_names = [n for n in SMALL_NAMES if n not in CONVS]

    def pack_flat(prefix):
        return _pack_small_flat({n: A[prefix + n] for n in flat_names}, flat_names)

    shapes = {n: A[n].shape for n in flat_names}
    d_s, m_s, v_s = _adamw(pack_flat(''), _pack_small_flat(g_small, flat_names), pack_flat('m_'), pack_flat('v_'),
                           "adamw_small")
    d_s, m_s, v_s = (_unpack_flat(p, shapes, flat_names) for p in (d_s, m_s, v_s))

    grads, deltas, new_m, new_v = {}, {}, {}, {}
    for n in WEIGHTS:
        if n in MATRICES or n in CONVS:
            grads[n] = g_mat[n] if n in MATRICES else g_small[n]
            deltas[n], new_m[n], new_v[n] = _adamw(A[n], grads[n], A['m_' + n], A['v_' + n], "adamw_" + n)
        else:
            grads[n] = g_small[n].reshape(A[n].shape)
            deltas[n], new_m[n], new_v[n] = d_s[n], m_s[n], v_s[n]
    return (loss, grad_x[None], *[grads[n] for n in WEIGHTS], *[deltas[n] for n in WEIGHTS],
            *[new_m[n] for n in WEIGHTS], *[new_v[n] for n in WEIGHTS])


def _pack_small_flat(vals, names):
    flat = jnp.concatenate([vals[n].astype(F32).reshape(-1) for n in names])
    rows = -(-flat.shape[0] // (8 * LANES)) * 8
    return jnp.pad(flat, (0, rows * LANES - flat.shape[0])).reshape(rows, LANES)


def _unpack_flat(packed, shapes, names):
    flat = packed.reshape(-1)
    out = {}
    off = 0
    for n in names:
        size = int(np.prod(shapes[n]))
        out[n] = flat[off:off + size].reshape(shapes[n])
        off += size
    return out
```

```python
import functools
import math

import numpy as np
import jax
import jax.numpy as jnp
from jax import lax
from jax.experimental import pallas as pl
from jax.experimental.pallas import tpu as pltpu

F32 = jnp.float32
BF16 = jnp.bfloat16
EPS = 1e-6
MESH = pl.DeviceIdType.MESH

ADAM_LR = 0.001
ADAM_B1 = 0.9
ADAM_B2 = 0.999
ADAM_EPS = 1e-08
ADAM_WD = 0.01
ADAM_STEP = 10

VMEM_LIMIT_BYTES = 56 * 1024 * 1024
MATMUL_VMEM_BYTES = 44 * 1024 * 1024
LANES = 1024

WEIGHTS = ['l0_mix_norm', 'l0_w_in', 'l0_ret_norm', 'l0_s5_lambda_re', 'l0_s5_lambda_im', 'l0_s5_b_re', 'l0_s5_b_im',
           'l0_s5_c_re', 'l0_s5_c_im', 'l0_s5_d', 'l0_s5_log_dt', 'l0_s5_w_glu', 'l0_s5_b_glu', 'l0_w_out',
           'l0_xa_norm', 'l0_mem_norm', 'l0_xa_wq', 'l0_xa_wkv', 'l0_xa_wo', 'l0_ffn_norm', 'l0_ffn_w_up',
           'l0_ffn_conv', 'l0_ffn_w_down', 'l1_mix_norm', 'l1_w_in', 'l1_conv', 'l1_a_log', 'l1_dt_bias',
           'l1_o_norm', 'l1_w_out', 'l1_xa_norm', 'l1_mem_norm', 'l1_xa_wq', 'l1_xa_wkv', 'l1_xa_wo',
           'l1_ffn_norm', 'l1_ffn_w_up', 'l1_ffn_conv', 'l1_ffn_w_down', 'final_norm']
ARG_NAMES = (['x', 'mem'] + WEIGHTS + ['loss_target'] + ['m_' + w for w in WEIGHTS] + ['v_' + w for w in WEIGHTS])

MATRICES = {
    'l0_w_in': 1, 'l0_s5_w_glu': 0, 'l0_w_out': 0, 'l0_xa_wq': 0, 'l0_xa_wkv': 1, 'l0_xa_wo': 0, 'l0_ffn_w_up': 1,
    'l0_ffn_w_down': 0, 'l1_w_in': 1, 'l1_w_out': 0, 'l1_xa_wq': 0, 'l1_xa_wkv': 1, 'l1_xa_wo': 0,
    'l1_ffn_w_up': 1, 'l1_ffn_w_down': 0,
}
CONVS = ('l0_ffn_conv', 'l1_conv', 'l1_ffn_conv')
MATRIX_NAMES = [w for w in WEIGHTS if w in MATRICES]
SMALL_NAMES = [w for w in WEIGHTS if w not in MATRICES]


def _cparams(sem=None):
    return pltpu.CompilerParams(dimension_semantics=sem, vmem_limit_bytes=VMEM_LIMIT_BYTES)


def _pick(n, cands):
    for c in cands:
        if n % c == 0:
            return c
    return n


_NN = ((1,), (0,))
_NT = ((1,), (1,))
_TN = ((0,), (0,))


def _dot(a, b, dims, hi):
    if hi is not None:
        return lax.dot_general(a.astype(F32), b.astype(F32), (dims, ((), ())), precision=hi,
                               preferred_element_type=F32)
    return lax.dot_general(a.astype(BF16), b.astype(BF16), (dims, ((), ())), preferred_element_type=F32)


def _make_mm(hi):
    @jax.custom_vjp
    def nn(a, b):
        return _dot(a, b, _NN, hi)

    def nn_f(a, b):
        return nn(a, b), (a, b)

    def nn_b(r, g):
        a, b = r
        return _dot(g, b, _NT, hi), _dot(a, g, _TN, hi)

    nn.defvjp(nn_f, nn_b)

    @jax.custom_vjp
    def nt(a, b):
        return _dot(a, b, _NT, hi)

    def nt_f(a, b):
        return nt(a, b), (a, b)

    def nt_b(r, g):
        a, b = r
        return _dot(g, b, _NN, hi), _dot(g, a, _TN, hi)

    nt.defvjp(nt_f, nt_b)

    @jax.custom_vjp
    def tn(a, b):
        return _dot(a, b, _TN, hi)

    def tn_f(a, b):
        return tn(a, b), (a, b)

    def tn_b(r, g):
        a, b = r
        return _dot(b, g, _NT, hi), _dot(a, g, _NN, hi)

    tn.defvjp(tn_f, tn_b)
    return nn, nt, tn


mm, mm_nt, mm_tn = _make_mm(None)
mmh, mmh_nt, mmh_tn = _make_mm(lax.Precision.HIGHEST)
mm3, _, _ = _make_mm(lax.Precision.HIGH)


@jax.custom_vjp
def _swap_halves(x):
    return pltpu.roll(x, 64, 1)


def _swap_f(x):
    return pltpu.roll(x, 64, 1), None


def _swap_b(_, g):
    return (pltpu.roll(g, 64, 1),)


_swap_halves.defvjp(_swap_f, _swap_b)


def _silu(x):
    return x * jax.nn.sigmoid(x)


def _rms(x, g):
    return x * lax.rsqrt(jnp.mean(x * x, axis=-1, keepdims=True) + EPS) * g


def _iota(shape, dim):
    return lax.broadcasted_iota(jnp.int32, shape, dim)


def _matmul(a, b, mode="nn", res=None, name="mm", a_cols=None, out_dtype=F32):
    a_off, a_w = (0, a.shape[1]) if a_cols is None else a_cols
    if mode == "nn":
        (M, K), (K2, N) = (a.shape[0], a_w), b.shape
    elif mode == "nt":
        (M, K), (N, K2) = (a.shape[0], a_w), b.shape
    else:
        (K, M), (K2, N) = (a.shape[0], a_w), b.shape
    assert K == K2, (a.shape, b.shape, mode)
    tn = _pick(N, (1024, 1408, 512, 256, 128))
    tk = _pick(K, (1024, 1408, 512, 256, 128))
    nk = K // tk
    a_bytes = 2 if a.dtype == BF16 else 4
    b_bytes = 2 if b.dtype == BF16 else 4
    for tm in (1024, 512, 1408, 256, 128):
        need = 2 * (tm * tk * a_bytes + tk * tn * b_bytes + (tm * tn * 4 if res is not None else 0)) + 3 * tm * tn * 4
        if M % tm == 0 and need <= MATMUL_VMEM_BYTES:
            break
    dims = {"nn": _NN, "nt": _NT, "tn": _TN}[mode]
    ao = a_off // (tm if mode == "tn" else tk)
    assert ao * (tm if mode == "tn" else tk) == a_off
    if mode == "nn":
        a_spec = pl.BlockSpec((tm, tk), lambda i, j, k: (i, k + ao))
        b_spec = pl.BlockSpec((tk, tn), lambda i, j, k: (k, j))
    elif mode == "nt":
        a_spec = pl.BlockSpec((tm, tk), lambda i, j, k: (i, k + ao))
        b_spec = pl.BlockSpec((tn, tk), lambda i, j, k: (j, k))
    else:
        a_spec = pl.BlockSpec((tk, tm), lambda i, j, k: (k, i + ao))
        b_spec = pl.BlockSpec((tk, tn), lambda i, j, k: (k, j))
    o_spec = pl.BlockSpec((tm, tn), lambda i, j, k: (i, j))
    has_res = res is not None

    def kern(*refs):
        if has_res:
            a_ref, b_ref, r_ref, o_ref, acc_ref = refs
        else:
            a_ref, b_ref, o_ref, acc_ref = refs
        k = pl.program_id(2)
        part = lax.dot_general(a_ref[...].astype(BF16), b_ref[...].astype(BF16), (dims, ((), ())),
                               preferred_element_type=F32)
        if nk == 1:
            o_ref[...] = (part + r_ref[...] if has_res else part).astype(o_ref.dtype)
            return

        @pl.when(k == 0)
        def _():
            acc_ref[...] = part

        @pl.when((k > 0) & (k < nk - 1))
        def _():
            acc_ref[...] += part

        @pl.when(k == nk - 1)
        def _():
            total = acc_ref[...] + part
            o_ref[...] = (total + r_ref[...] if has_res else total).astype(o_ref.dtype)

    in_specs = [a_spec, b_spec] + ([o_spec] if has_res else [])
    ops = (a, b) + ((res,) if has_res else ())
    return pl.pallas_call(
        kern, name=name, grid=(M // tm, N // tn, nk), in_specs=in_specs, out_specs=o_spec,
        out_shape=jax.ShapeDtypeStruct((M, N), out_dtype), scratch_shapes=[pltpu.VMEM((tm, tn), F32)],
        compiler_params=_cparams(("parallel", "parallel", "arbitrary")))(*ops)


def _matmul_cat(pieces, b, mode="nn", res=None, name="mmcat"):
    M = pieces[0].shape[0]
    widths = [p.shape[1] for p in pieces]
    K = sum(widths)
    N = b.shape[1] if mode == "nn" else b.shape[0]
    assert (b.shape[0] if mode == "nn" else b.shape[1]) == K
    tn = _pick(N, (1024, 512, 256, 128))
    a_bytes = 2 if pieces[0].dtype == BF16 else 4
    for tm in (1024, 512, 256, 128):
        need = 2 * (tm * K * a_bytes + K * tn * 2 + (tm * tn * 4 if res is not None else 0)) + 3 * tm * tn * 4
        if M % tm == 0 and need <= MATMUL_VMEM_BYTES:
            break
    npc = len(pieces)
    has_res = res is not None
    dims = _NN if mode == "nn" else _NT

    def kern(*refs):
        b_ref = refs[npc]
        o_ref = refs[-1]
        acc = refs[npc + 1][...] if has_res else None
        off = 0
        for p in range(npc):
            bp = b_ref[off:off + widths[p], :] if mode == "nn" else b_ref[:, off:off + widths[p]]
            t = lax.dot_general(refs[p][...].astype(BF16), bp.astype(BF16), (dims, ((), ())),
                                preferred_element_type=F32)
            acc = t if acc is None else acc + t
            off += widths[p]
        o_ref[...] = acc

    in_specs = [pl.BlockSpec((tm, w), lambda j, i: (i, 0)) for w in widths]
    in_specs.append(pl.BlockSpec((K, tn), lambda j, i: (0, j)) if mode == "nn"
                    else pl.BlockSpec((tn, K), lambda j, i: (j, 0)))
    o_spec = pl.BlockSpec((tm, tn), lambda j, i: (i, j))
    if has_res:
        in_specs.append(o_spec)
    ops = list(pieces) + [b] + ([res] if has_res else [])
    return pl.pallas_call(
        kern, name=name, grid=(N // tn, M // tm), in_specs=in_specs, out_specs=o_spec,
        out_shape=jax.ShapeDtypeStruct((M, N), F32), compiler_params=_cparams(("parallel", "parallel")))(*ops)


def _blk(a, width=None, colblk=0):
    return (a, a.shape[1] if width is None else width, colblk)


def _row_specs(blocked, params, ts):
    specs = []
    for (_, w, cb) in blocked:
        specs.append(pl.BlockSpec((ts, w), functools.partial(lambda i, cb: (i, cb), cb=cb)))
    for p in params:
        specs.append(pl.BlockSpec(p.shape, lambda i: (0, 0)))
    return specs


def _rowwise(fn, blocked, params, out_widths, name, ts=256, out_dtypes=None):
    S = blocked[0][0].shape[0]
    ts = min(ts, S)
    nb, npar = len(blocked), len(params)
    out_dtypes = [F32] * len(out_widths) if out_dtypes is None else out_dtypes

    def kern(*refs):
        vals = [r[...] for r in refs[:nb + npar]]
        outs = fn(*vals)
        for o_ref, o in zip(refs[nb + npar:], outs):
            o_ref[...] = o.astype(o_ref.dtype)

    return pl.pallas_call(
        kern, name=name, grid=(S // ts,), in_specs=_row_specs(blocked, params, ts),
        out_specs=[pl.BlockSpec((ts, w), lambda i: (i, 0)) for w in out_widths],
        out_shape=[jax.ShapeDtypeStruct((S, w), d) for w, d in zip(out_widths, out_dtypes)],
        compiler_params=_cparams(("parallel",)))(*[b[0] for b in blocked], *params)


def _rowwise_bwd(fn, blocked, params, cots, name, blocked_grad=None, param_grad=None, adds=None, ts=256,
                 out_dtypes=None):
    S = blocked[0][0].shape[0]
    ts = min(ts, S)
    cots = [c if isinstance(c, tuple) else _blk(c) for c in cots]
    nb, npar, nc = len(blocked), len(params), len(cots)
    blocked_grad = [True] * nb if blocked_grad is None else blocked_grad
    param_grad = [True] * npar if param_grad is None else param_grad
    adds = {} if adds is None else adds
    bidx = [i for i in range(nb) if blocked_grad[i]]
    pidx = [i for i in range(npar) if param_grad[i]]
    add_keys = sorted(adds)
    n_in = nb + npar + nc + len(add_keys)

    def kern(*refs):
        i = pl.program_id(0)
        xs = [r[...] for r in refs[:nb]]
        ps = [r[...] for r in refs[nb:nb + npar]]
        gs = [r[...] for r in refs[nb + npar:nb + npar + nc]]
        add_vals = {k: refs[nb + npar + nc + n][...] for n, k in enumerate(add_keys)}
        outs = refs[n_in:]

        def f(*diff):
            full_x = list(xs)
            full_p = list(ps)
            for n, ix in enumerate(bidx):
                full_x[ix] = diff[n]
            for n, ix in enumerate(pidx):
                full_p[ix] = diff[len(bidx) + n]
            return tuple(fn(*full_x, *full_p))

        _, vjp = jax.vjp(f, *[xs[ix] for ix in bidx], *[ps[ix] for ix in pidx])
        grads = vjp(tuple(gs))
        for n, ix in enumerate(bidx):
            g = grads[n]
            if ix in add_vals:
                g = g + add_vals[ix]
            outs[n][...] = g.astype(outs[n].dtype)
        for n in range(len(pidx)):
            o_ref = outs[len(bidx) + n]

            @pl.when(i == 0)
            def _(o_ref=o_ref):
                o_ref[...] = jnp.zeros_like(o_ref)

            o_ref[...] += grads[len(bidx) + n]

    in_specs = _row_specs(blocked, params, ts)
    in_specs += _row_specs(cots, [], ts)
    in_specs += [pl.BlockSpec((ts, adds[k].shape[1]), lambda i: (i, 0)) for k in add_keys]
    out_specs = [pl.BlockSpec((ts, blocked[ix][1]), lambda i: (i, 0)) for ix in bidx]
    out_specs += [pl.BlockSpec(params[ix].shape, lambda i: (0, 0)) for ix in pidx]
    out_dtypes = [F32] * len(bidx) if out_dtypes is None else out_dtypes
    out_shape = [jax.ShapeDtypeStruct((S, blocked[ix][1]), d) for ix, d in zip(bidx, out_dtypes)]
    out_shape += [jax.ShapeDtypeStruct(params[ix].shape, F32) for ix in pidx]
    return pl.pallas_call(
        kern, name=name, grid=(S // ts,), in_specs=in_specs, out_specs=out_specs, out_shape=out_shape,
        compiler_params=_cparams(("arbitrary",)))(*[b[0] for b in blocked], *params, *[c[0] for c in cots],
                                                    *[adds[k] for k in add_keys])


def _rms_fn(x, g):
    return (_rms(x, g),)


def _head_norm(o, n_heads, dh):
    outs = []
    for h in range(n_heads):
        oh = o[:, h * dh:(h + 1) * dh]
        outs.append(oh * lax.rsqrt(jnp.mean(oh * oh, axis=-1, keepdims=True) + EPS))
    return outs


def _ret_post_fn(o_raw, gate, ret_norm):
    o = jnp.concatenate(_head_norm(o_raw, 4, 128), axis=1)
    return (o * ret_norm * _silu(gate),)


def _s5_post_fn(y1, y2, u, d, w_glu, b_glu):
    y = y1 - y2 + d * u
    y = jax.nn.gelu(y)
    return (y * jax.nn.sigmoid(mm(y, w_glu) + b_glu),)


def _xattn_fn(q, kv):
    outs = []
    for h in range(4):
        qh = q[:, h * 256:(h + 1) * 256]
        kh = kv[:, h * 256:(h + 1) * 256]
        vh = kv[:, 1024 + h * 256:1024 + (h + 1) * 256]
        s = mm_nt(qh, kh) * (256 ** -0.5)
        s = s - lax.stop_gradient(jnp.max(s, axis=-1, keepdims=True))
        p = jnp.exp(s)
        p = p / jnp.sum(p, axis=-1, keepdims=True)
        outs.append(mm(p, vh))
    return (jnp.concatenate(outs, axis=1),)


def _softplus(x):
    return jnp.maximum(x, 0.0) + jnp.log1p(jnp.exp(-jnp.abs(x)))


def _gdn_gates_fn(pt, a_log_p, dtb_p):
    rows, cols = _iota((128, 1024), 0), _iota((128, 1024), 1)
    e_b = (rows == (cols >> 7)).astype(F32)
    e_a = (rows == (cols >> 7) + 8).astype(F32)
    beta = jax.nn.sigmoid(pt)
    g = -(jnp.exp(a_log_p) * _softplus(pt + dtb_p))
    return mmh(g, e_a), mmh(beta, e_b)


def _gdn_post_fn(o_raw, z, o_norm):
    outs = _head_norm(o_raw, 8, 128)
    o = jnp.concatenate([oh * o_norm for oh in outs], axis=1)
    return (o * _silu(z),)


def _ffn_post(up, gate):
    return _silu(gate) * up


def _shift_down(cur, prev8, sh, row8):
    if sh == 0:
        return cur
    r = pltpu.roll(cur, sh, 0)
    p = pltpu.roll(prev8, sh, 0)
    top = jnp.where(row8 < sh, p, r[0:8])
    if cur.shape[0] == 8:
        return top
    return jnp.concatenate([top, r[8:]], axis=0)


def _shift_up(cur, next8, sh, row8):
    if sh == 0:
        return cur
    ts = cur.shape[0]
    r = pltpu.roll(cur, ts - sh, 0)
    p = pltpu.roll(next8, 8 - sh, 0)
    bot = jnp.where(row8 >= 8 - sh, p, r[ts - 8:])
    return jnp.concatenate([r[:ts - 8], bot], axis=0)


def _conv_rows(cur, prev8, wrows, row8):
    k_w = len(wrows)
    out = None
    for j in range(k_w):
        t = _shift_down(cur, prev8, k_w - 1 - j, row8) * wrows[j]
        out = t if out is None else out + t
    return out


def _conv_specs(x, xoff, w, woff, ts, tc):
    r8 = ts // 8
    return [pl.BlockSpec((ts, tc), functools.partial(lambda i, j, o: (i, j + o), o=xoff)),
            pl.BlockSpec((8, tc), functools.partial(lambda i, j, o: (jnp.maximum(i * r8 - 1, 0), j + o), o=xoff)),
            pl.BlockSpec((w.shape[0], tc), functools.partial(lambda i, j, o: (0, j + o), o=woff))]


def _conv_post(srcs, post, ncol, tc, name, cots=None, ts=256, out_dtype=F32):
    S = srcs[0][0].shape[0]
    ns = len(srcs)
    bwd = cots is not None

    def kern(*refs):
        first = pl.program_id(0) == 0
        row8 = _iota((8, tc), 0)
        cs = []
        for s in range(ns):
            cur_ref, prev_ref, w_ref = refs[3 * s:3 * s + 3]
            prev = jnp.where(first, 0.0, prev_ref[...])
            wrows = [w_ref[j:j + 1, :] for j in range(w_ref.shape[0])]
            cs.append(_conv_rows(cur_ref[...], prev, wrows, row8))
        if bwd:
            g = refs[3 * ns][...]
            _, vjp = jax.vjp(lambda *c: post(*c), *cs)
            for o_ref, d in zip(refs[3 * ns + 1:], vjp(g)):
                o_ref[...] = d
        else:
            refs[3 * ns][...] = post(*cs).astype(refs[3 * ns].dtype)

    in_specs = []
    ops = []
    for (x, xoff, w, woff) in srcs:
        in_specs += _conv_specs(x, xoff, w, woff, ts, tc)
        ops += [x, x, w]
    o_spec = pl.BlockSpec((ts, tc), lambda i, j: (i, j))
    o_shape = jax.ShapeDtypeStruct((S, ncol * tc), F32)
    if bwd:
        in_specs.append(o_spec)
        ops.append(cots)
        out_specs, out_shape = [o_spec] * ns, [o_shape] * ns
    else:
        out_specs, out_shape = o_spec, jax.ShapeDtypeStruct((S, ncol * tc), out_dtype)
    return pl.pallas_call(
        kern, name=name, grid=(S // ts, ncol), in_specs=in_specs, out_specs=out_specs, out_shape=out_shape,
        compiler_params=_cparams(("parallel", "parallel")))(*ops)


def _conv_bwd(dc, x, xoff, w, woff, ncol, tc, name, ts=256):
    S = x.shape[0]
    k_w = w.shape[0]
    r8 = ts // 8
    nblk8 = S // 8
    nrow = S // ts

    def kern(dc_ref, dn_ref, x_ref, xp_ref, w_ref, dx_ref, dw_ref):
        i = pl.program_id(1)
        row8 = _iota((8, tc), 0)
        dcur = dc_ref[...]
        dnext = jnp.where(i == nrow - 1, 0.0, dn_ref[...])
        xcur = x_ref[...]
        xprev = jnp.where(i == 0, 0.0, xp_ref[...])

        @pl.when(i == 0)
        def _():
            dw_ref[...] = jnp.zeros_like(dw_ref)

        dx = None
        for j in range(k_w):
            sh = k_w - 1 - j
            wj = w_ref[j:j + 1, :]
            t = _shift_up(dcur, dnext, sh, row8) * wj
            dx = t if dx is None else dx + t
            dw_ref[j:j + 1, :] += jnp.sum(dcur * _shift_down(xcur, xprev, sh, row8), axis=0, keepdims=True)
        dx_ref[...] = dx.astype(dx_ref.dtype)

    in_specs = [pl.BlockSpec((ts, tc), lambda j, i: (i, j)),
                pl.BlockSpec((8, tc), lambda j, i: (jnp.minimum((i + 1) * r8, nblk8 - 1), j)),
                pl.BlockSpec((ts, tc), functools.partial(lambda j, i, o: (i, j + o), o=xoff)),
                pl.BlockSpec((8, tc), functools.partial(lambda j, i, o: (jnp.maximum(i * r8 - 1, 0), j + o), o=xoff)),
                pl.BlockSpec((k_w, tc), functools.partial(lambda j, i, o: (0, j + o), o=woff))]
    out_specs = [pl.BlockSpec((ts, tc), lambda j, i: (i, j)), pl.BlockSpec((k_w, tc), lambda j, i: (0, j))]
    out_shape = [jax.ShapeDtypeStruct((S, ncol * tc), BF16), jax.ShapeDtypeStruct((k_w, ncol * tc), F32)]
    return pl.pallas_call(
        kern, name=name, grid=(ncol, nrow), in_specs=in_specs, out_specs=out_specs, out_shape=out_shape,
        compiler_params=_cparams(("parallel", "arbitrary")))(dc, dc, x, x, w)


def _conv_post_bwd(srcs, post, ncol, tc, cot, name, ts=256):
    S = srcs[0][0].shape[0]
    ns = len(srcs)
    r8 = ts // 8
    nblk8 = S // 8
    nrow = S // ts

    def kern(*refs):
        i = pl.program_id(1)
        row8 = _iota((8, tc), 0)
        g_ref, gn_ref = refs[4 * ns:4 * ns + 2]
        outs = refs[4 * ns + 2:]
        xs, xps, ws, cs, cns = [], [], [], [], []
        for s in range(ns):
            cur_ref, prev_ref, next_ref, w_ref = refs[4 * s:4 * s + 4]
            xcur = cur_ref[...]
            xprev = jnp.where(i == 0, 0.0, prev_ref[...])
            wrows = [w_ref[j:j + 1, :] for j in range(w_ref.shape[0])]
            xs.append(xcur)
            xps.append(xprev)
            ws.append(wrows)
            cs.append(_conv_rows(xcur, xprev, wrows, row8))
            cns.append(_conv_rows(next_ref[...], xcur[ts - 8:], wrows, row8))
        _, vjp = jax.vjp(lambda *c: post(*c), *cs)
        dcs = vjp(g_ref[...])
        _, vjp_next = jax.vjp(lambda *c: post(*c), *cns)
        dcns = vjp_next(jnp.where(i == nrow - 1, 0.0, gn_ref[...]))
        for s in range(ns):
            dx_ref, dw_ref = outs[2 * s], outs[2 * s + 1]

            @pl.when(i == 0)
            def _(dw_ref=dw_ref):
                dw_ref[...] = jnp.zeros_like(dw_ref)

            k_w = len(ws[s])
            dx = None
            for j in range(k_w):
                sh = k_w - 1 - j
                t = _shift_up(dcs[s], dcns[s], sh, row8) * ws[s][j]
                dx = t if dx is None else dx + t
                dw_ref[j:j + 1, :] += jnp.sum(dcs[s] * _shift_down(xs[s], xps[s], sh, row8), axis=0, keepdims=True)
            dx_ref[...] = dx.astype(dx_ref.dtype)

    def nxt(i):
        return jnp.minimum((i + 1) * r8, nblk8 - 1)

    in_specs, ops = [], []
    for (x, xoff, w, woff) in srcs:
        in_specs += [pl.BlockSpec((ts, tc), functools.partial(lambda j, i, o: (i, j + o), o=xoff)),
                     pl.BlockSpec((8, tc), functools.partial(lambda j, i, o: (jnp.maximum(i * r8 - 1, 0), j + o),
                                                             o=xoff)),
                     pl.BlockSpec((8, tc), functools.partial(lambda j, i, o: (nxt(i), j + o), o=xoff)),
                     pl.BlockSpec((w.shape[0], tc), functools.partial(lambda j, i, o: (0, j + o), o=woff))]
        ops += [x, x, x, w]
    in_specs += [pl.BlockSpec((ts, tc), lambda j, i: (i, j)), pl.BlockSpec((8, tc), lambda j, i: (nxt(i), j))]
    ops += [cot, cot]
    out_specs, out_shape = [], []
    for (x, xoff, w, woff) in srcs:
        out_specs += [pl.BlockSpec((ts, tc), lambda j, i: (i, j)), pl.BlockSpec((w.shape[0], tc), lambda j, i: (0, j))]
        out_shape += [jax.ShapeDtypeStruct((S, ncol * tc), BF16), jax.ShapeDtypeStruct((w.shape[0], ncol * tc), F32)]
    return pl.pallas_call(
        kern, name=name, grid=(ncol, nrow), in_specs=in_specs, out_specs=out_specs, out_shape=out_shape,
        compiler_params=_cparams(("parallel", "arbitrary")))(*ops)


def _ret_tables(S):
    H, C, dh = 4, 128, 128
    lg = jnp.log1p(-jnp.exp2(-5.0 - jnp.arange(H, dtype=F32)))
    idx = jnp.arange(C, dtype=F32)
    diff = idx[:, None] - idx[None, :]
    causal = diff >= 0
    intra = jnp.where(causal, jnp.exp(lg[:, None, None] * jnp.where(causal, diff, 0.0)), 0.0)
    kdec = jnp.broadcast_to(jnp.exp(lg[:, None] * (C - 1 - idx))[:, :, None], (H, C, dh))
    qdec = jnp.broadcast_to(jnp.exp(lg[:, None] * (idx + 1))[:, :, None], (H, C, dh))
    cdec = jnp.broadcast_to(jnp.exp(lg * C)[:, None, None], (H, dh, dh))
    half = dh // 2
    inv = jnp.exp(-math.log(10000.0) * jnp.arange(half, dtype=F32) / half)
    ang = jnp.arange(S).astype(F32)[:, None] * inv[None, :]
    cos, sin = jnp.cos(ang), jnp.sin(ang)
    cosf = jnp.concatenate([cos, cos], axis=1)
    sinf = jnp.concatenate([-sin, sin], axis=1)
    return cosf, sinf, intra, kdec, qdec, cdec


def _ret_chunk(q, k, v, cosf, sinf, intra, kdec, qdec, cdec, state):
    hs = range(len(q))
    qr = [q[h] * cosf + _swap_halves(q[h]) * sinf for h in hs]
    kr = [(k[h] * cosf + _swap_halves(k[h]) * sinf) * (128 ** -0.5) for h in hs]
    scores = [mm_nt(qr[h], kr[h]) * intra[h] for h in hs]
    inner = [mm(scores[h], v[h]) for h in hs]
    kv = [mm_tn(kr[h] * kdec[h], v[h]) for h in hs]
    cross = [mm(qr[h] * qdec[h], state[h]) for h in hs]
    return [inner[h] + cross[h] for h in hs], [state[h] * cdec[h] + kv[h] for h in hs]


RET_H = 4


def _ret_call(proj, tabs, states=None, do=None):
    S = proj.shape[0]
    N = S // 128
    bwd = do is not None

    def nn(n):
        return N - 1 - n if bwd else n

    qkv_spec = pl.BlockSpec((128, 3 * 512), lambda n: (nn(n), 0))
    pos = pl.BlockSpec((128, 128), lambda n: (nn(n), 0))
    tab = pl.BlockSpec((RET_H, 128, 128), lambda n: (0, 0, 0))
    st_spec = pl.BlockSpec((None, RET_H, 128, 128), lambda n: (nn(n), 0, 0, 0))
    o_spec = pl.BlockSpec((128, 512), lambda n: (nn(n), 0))

    def kern(*refs):
        x_ref, c_ref, s_ref, i_ref, kd_ref, qd_ref, cd_ref = refs[:7]
        carry = refs[-1]
        heads = range(RET_H)

        @pl.when(pl.program_id(0) == 0)
        def _():
            carry[...] = jnp.zeros_like(carry)

        def cols(ref, off=0):
            return [ref[:, _hs(off + h)] for h in heads]

        def tabs_of(ref):
            return [ref[h] for h in heads]

        consts = (c_ref[...], s_ref[...], tabs_of(i_ref), tabs_of(kd_ref), tabs_of(qd_ref), tabs_of(cd_ref))
        qkv = (cols(x_ref), cols(x_ref, RET_H), cols(x_ref, 2 * RET_H))
        if bwd:
            sp_ref, do_ref = refs[7:9]
            outs = refs[9:12]
            _, vjp = jax.vjp(lambda q, k, v, s: _ret_chunk(q, k, v, *consts, s), *qkv, tabs_of(sp_ref))
            dq, dk, dv, ds = vjp((cols(do_ref), tabs_of(carry)))
            for h in heads:
                for o_ref, d in zip(outs, (dq[h], dk[h], dv[h])):
                    o_ref[:, _hs(h)] = d.astype(o_ref.dtype)
                carry[h] = ds[h]
        else:
            o_ref, sp_ref = refs[7:9]
            state = tabs_of(carry)
            out, new = _ret_chunk(*qkv, *consts, state)
            for h in heads:
                sp_ref[h] = state[h]
                o_ref[:, _hs(h)] = out[h]
                carry[h] = new[h]

    in_specs = [qkv_spec, pos, pos, tab, tab, tab, tab]
    if bwd:
        in_specs += [st_spec, o_spec]
        out_specs = [o_spec] * 3
        out_shape = [jax.ShapeDtypeStruct((S, 512), BF16)] * 3
        ops = (proj, *tabs, states, do)
    else:
        out_specs = [o_spec, st_spec]
        out_shape = [jax.ShapeDtypeStruct((S, 512), F32), jax.ShapeDtypeStruct((N, RET_H, 128, 128), F32)]
        ops = (proj, *tabs)
    return pl.pallas_call(
        kern, name="ret_bwd" if bwd else "ret_fwd", grid=(N,), in_specs=in_specs, out_specs=out_specs,
        out_shape=out_shape, scratch_shapes=[pltpu.VMEM((RET_H, 128, 128), F32)],
        compiler_params=_cparams(("arbitrary",)))(*ops)


GDN_C = 64
GDN_H = 8


def _unit_lower_inverse(a_mats, eye):
    p = [-a for a in a_mats]
    t = [eye + x for x in p]
    for _ in range(5):
        p = [mm3(x, x) for x in p]
        t = [mm3(y, eye + x) for y, x in zip(t, p)]
    return t


@jax.custom_vjp
def _known_inverse(a_mat, t_mat):
    return t_mat


def _known_inverse_f(a_mat, t_mat):
    return t_mat, t_mat


def _known_inverse_b(t_mat, g):
    return -mmh_tn(t_mat, mmh_nt(g, t_mat)), jnp.zeros_like(t_mat)


_known_inverse.defvjp(_known_inverse_f, _known_inverse_b)


def _gdn_intra(q, k, v, g_b, beta_b, t_known=None):
    c = GDN_C
    hs = range(len(q))
    q = [x * lax.rsqrt(jnp.sum(x * x, axis=-1, keepdims=True) + EPS) * (128 ** -0.5) for x in q]
    k = [x * lax.rsqrt(jnp.sum(x * x, axis=-1, keepdims=True) + EPS) for x in k]
    ri, ci = _iota((c, c), 0), _iota((c, c), 1)
    incl = ri >= ci
    strict = ri > ci
    eye = (ri == ci).astype(F32)
    lower = incl.astype(F32)
    gc_b = [mm3(lower, g) for g in g_b]
    gl_b = [jnp.sum(g, axis=0, keepdims=True) for g in g_b]
    kb = [k[h] * beta_b[h] for h in hs]
    vb = [v[h] * beta_b[h] for h in hs]
    gcc = [g[:, :c] for g in gc_b]
    decay = [jnp.where(incl, jnp.exp(jnp.where(incl, g - g.T, 0.0)), 0.0) for g in gcc]
    a_mat = [jnp.where(strict, mm_nt(kb[h], k[h]) * decay[h], 0.0) for h in hs]
    if t_known is None:
        t_mat = _unit_lower_inverse(a_mat, eye)
    else:
        t_mat = [_known_inverse(a_mat[h], t_known[h]) for h in hs]
    egc = [jnp.exp(g) for g in gc_b]
    w = [mm(t_mat[h], kb[h] * egc[h]) for h in hs]
    u = [mm(t_mat[h], vb[h]) for h in hs]
    qk = [jnp.where(incl, mm_nt(q[h], k[h]) * decay[h], 0.0) for h in hs]
    q_dec = [q[h] * egc[h] for h in hs]
    k_dec = [k[h] * jnp.exp(gl_b[h] - gc_b[h]) for h in hs]
    return w, u, q_dec, k_dec, qk, t_mat


def _gdn_step(w, u, q_dec, k_dec, qk, g_b, state):
    hs = range(len(w))
    gl_s = [jnp.sum(g, axis=0, keepdims=True) for g in g_b]
    ws = [mm(w[h], state[h]) for h in hs]
    qs = [mm(q_dec[h], state[h]) for h in hs]
    v_new = [u[h] - ws[h] for h in hs]
    o = [qs[h] + mm(qk[h], v_new[h]) for h in hs]
    new = [state[h] * jnp.exp(gl_s[h]) + mm_tn(k_dec[h], v_new[h]) for h in hs]
    return o, new


def _hs(h):
    return slice(h * 128, (h + 1) * 128)


def _gdn_intra_call(qkv, g_e, beta_e, cots=None):
    S = qkv.shape[0]
    N = S // GDN_C
    bwd = cots is not None
    row = pl.BlockSpec((GDN_C, 1024), lambda n: (n, 0))
    qkv_spec = pl.BlockSpec((GDN_C, 3072), lambda n: (n, 0))
    qk_spec = pl.BlockSpec((GDN_H, GDN_C, GDN_C), lambda n: (0, n, 0))

    def kern(*refs):
        x_ref, g_ref, b_ref = refs[:3]
        heads = range(GDN_H)

        def cols(ref, off=0):
            return [ref[:, _hs(off + h)] for h in heads]

        args = (cols(x_ref), cols(x_ref, 8), cols(x_ref, 16), cols(g_ref), cols(b_ref))
        if bwd:
            dw_ref, du_ref, dqd_ref, dkd_ref, dqk_ref, dgadd_ref, t_ref = refs[3:10]
            outs = refs[10:]
            t_known = [t_ref[h] for h in heads]
            _, vjp = jax.vjp(lambda *a: _gdn_intra(*a, t_known=t_known)[:5], *args)
            dq, dk, dv, dg, db = vjp((cols(dw_ref), cols(du_ref), cols(dqd_ref), cols(dkd_ref),
                                      [dqk_ref[h] for h in heads]))
            dgadd = cols(dgadd_ref)
            for h in heads:
                for o_ref, d in zip(outs, (dq[h], dk[h], dv[h], dg[h] + dgadd[h], db[h])):
                    o_ref[:, _hs(h)] = d
        else:
            w, u, qd, kd, qk, t_mat = _gdn_intra(*args)
            for h in heads:
                for o_ref, o in zip(refs[3:7], (w[h], u[h], qd[h], kd[h])):
                    o_ref[:, _hs(h)] = o
                refs[7][h] = qk[h]
                refs[8][h] = t_mat[h]

    big = jax.ShapeDtypeStruct((S, 1024), F32)
    sq = jax.ShapeDtypeStruct((GDN_H, S, GDN_C), F32)
    if bwd:
        in_specs = [qkv_spec, row, row, row, row, row, row, qk_spec, row, qk_spec]
        out_specs, out_shape = [row] * 5, [big] * 5
        ops = (qkv, g_e, beta_e) + tuple(cots)
    else:
        in_specs = [qkv_spec, row, row]
        out_specs = [row] * 4 + [qk_spec, qk_spec]
        out_shape = [big] * 4 + [sq, sq]
        ops = (qkv, g_e, beta_e)
    return pl.pallas_call(
        kern, name="gdn_intra_bwd" if bwd else "gdn_intra", grid=(N,), in_specs=in_specs, out_specs=out_specs,
        out_shape=out_shape, compiler_params=_cparams(("parallel",)))(*ops)


def _gdn_pass(w, u, qd, kd, qk, g_e, states=None, do=None):
    S = w.shape[0]
    N = S // GDN_C
    bwd = do is not None

    def nn(n):
        return N - 1 - n if bwd else n

    row = pl.BlockSpec((GDN_C, 1024), lambda n: (nn(n), 0))
    qk_spec = pl.BlockSpec((GDN_H, GDN_C, GDN_C), lambda n: (0, nn(n), 0))
    st_spec = pl.BlockSpec((None, GDN_H, 128, 128), lambda n: (nn(n), 0, 0, 0))

    def kern(*refs):
        w_ref, u_ref, qd_ref, kd_ref, qk_ref, g_ref = refs[:6]
        carry = refs[-1]

        @pl.when(pl.program_id(0) == 0)
        def _():
            carry[...] = jnp.zeros_like(carry)

        heads = range(GDN_H)

        def cols(ref):
            return [ref[:, _hs(h)] for h in heads]

        args = (cols(w_ref), cols(u_ref), cols(qd_ref), cols(kd_ref), [qk_ref[h] for h in heads], cols(g_ref))
        if bwd:
            sp_ref, do_ref = refs[6:8]
            outs = refs[8:14]
            _, vjp = jax.vjp(_gdn_step, *args, [sp_ref[h] for h in heads])
            dw, du, dqd, dkd, dqk, dg, ds = vjp((cols(do_ref), [carry[h] for h in heads]))
            for h in heads:
                for o_ref, d in zip(outs[:4], (dw[h], du[h], dqd[h], dkd[h])):
                    o_ref[:, _hs(h)] = d
                outs[4][h] = dqk[h]
                outs[5][:, _hs(h)] = dg[h]
                carry[h] = ds[h]
        else:
            o_ref, sp_ref = refs[6:8]
            state = [carry[h] for h in heads]
            o, new = _gdn_step(*args, state)
            for h in heads:
                sp_ref[h] = state[h]
                o_ref[:, _hs(h)] = o[h]
                carry[h] = new[h]

    big = jax.ShapeDtypeStruct((S, 1024), F32)
    in_specs = [row, row, row, row, qk_spec, row]
    if bwd:
        in_specs += [st_spec, row]
        out_specs = [row] * 4 + [qk_spec, row]
        out_shape = [big] * 4 + [jax.ShapeDtypeStruct((GDN_H, S, GDN_C), F32), big]
        ops = (w, u, qd, kd, qk, g_e, states, do)
    else:
        out_specs = [row, st_spec]
        out_shape = [big, jax.ShapeDtypeStruct((N, GDN_H, 128, 128), F32)]
        ops = (w, u, qd, kd, qk, g_e)
    return pl.pallas_call(
        kern, name="gdn_pass_bwd" if bwd else "gdn_pass", grid=(N,), in_specs=in_specs, out_specs=out_specs,
        out_shape=out_shape, scratch_shapes=[pltpu.VMEM((GDN_H, 128, 128), F32)],
        compiler_params=_cparams(("arbitrary",)))(*ops)


def _s5_prep_fn(lr, li, ldt, br, bi, cr, ci):
    dt = jnp.exp(ldt)
    mag = jnp.exp(lr * dt)
    a_re = mag * jnp.cos(li * dt)
    a_im = mag * jnp.sin(li * dt)
    den = lr * lr + li * li
    z_re = ((a_re - 1.0) * lr + a_im * li) / den
    z_im = (a_im * lr - (a_re - 1.0) * li) / den
    e1 = ((_iota((512, 32), 0) >> 4) == _iota((512, 32), 1)).astype(F32)
    zr_e = mmh(e1, z_re)
    zi_e = mmh(e1, z_im)
    bb_re = zr_e * br - zi_e * bi
    bb_im = zr_e * bi + zi_e * br
    t1 = ((_iota((64, 2048), 1) & 63) == _iota((64, 2048), 0)).astype(F32)
    m1 = (_iota((512, 2048), 0) >> 4) == (_iota((512, 2048), 1) >> 6)
    bd_re = jnp.where(m1, mmh(bb_re, t1), 0.0)
    bd_im = jnp.where(m1, mmh(bb_im, t1), 0.0)
    t2 = ((_iota((16, 512), 1) & 15) == _iota((16, 512), 0)).astype(F32)
    m2 = (_iota((2048, 512), 0) >> 6) == (_iota((2048, 512), 1) >> 4)
    cd_re = jnp.where(m2, mmh(cr, t2), 0.0)
    cd_im = jnp.where(m2, mmh(ci, t2), 0.0)
    return a_re, a_im, bd_re, bd_im, cd_re, cd_im


_PREP_OUT = [(32, 64), (32, 64), (512, 2048), (512, 2048), (2048, 512), (2048, 512)]


def _s5_prep(params, cots=None):
    bwd = cots is not None

    def kern(*refs):
        vals = [r[...] for r in refs[:7]]
        if bwd:
            gs = tuple(r[...] for r in refs[7:13])
            _, vjp = jax.vjp(_s5_prep_fn, *vals)
            for o_ref, d in zip(refs[13:], vjp(gs)):
                o_ref[...] = d
        else:
            for o_ref, o in zip(refs[7:], _s5_prep_fn(*vals)):
                o_ref[...] = o

    if bwd:
        out_shape = [jax.ShapeDtypeStruct(p.shape, F32) for p in params]
        ops = list(params) + list(cots)
    else:
        out_shape = [jax.ShapeDtypeStruct(s, F32) for s in _PREP_OUT]
        ops = list(params)
    return pl.pallas_call(kern, name="s5_prep_bwd" if bwd else "s5_prep", out_shape=out_shape,
                          compiler_params=_cparams())(*ops)


def _cmul(ar, ai, br, bi):
    return ar * br - ai * bi, ar * bi + ai * br


def _power_table(ar, ai, row8, descending):
    pr, pi = ar, ai
    tr = jnp.zeros(row8.shape, F32)
    ti = jnp.zeros(row8.shape, F32)
    for n in range(8):
        r = 7 - n if descending else n
        tr = jnp.where(row8 == r, pr, tr)
        ti = jnp.where(row8 == r, pi, ti)
        if n < 7:
            pr, pi = _cmul(pr, pi, ar, ai)
    return tr, ti


def _tile_scan(xr, xi, pows, row8, up):
    for d, (pr, pi) in zip((1, 2, 4), pows):
        if up:
            sr = jnp.where(row8 < 8 - d, pltpu.roll(xr, 8 - d, 0), 0.0)
            si = jnp.where(row8 < 8 - d, pltpu.roll(xi, 8 - d, 0), 0.0)
        else:
            sr = jnp.where(row8 >= d, pltpu.roll(xr, d, 0), 0.0)
            si = jnp.where(row8 >= d, pltpu.roll(xi, d, 0), 0.0)
        mr, mi = _cmul(pr, pi, sr, si)
        xr, xi = xr + mr, xi + mi
    return xr, xi


def _pick_row(x, row8, r):
    return jnp.sum(jnp.where(row8 == r, x, 0.0), axis=0, keepdims=True)


SCAN_LB = 512
SCAN_TS = 512


def _scan_fwd(bu_re, bu_im, a_re, a_im):
    S, L = bu_re.shape
    ts, lb = min(SCAN_TS, S), SCAN_LB
    nt = ts // 8

    def kern(br_ref, bi_ref, ar_ref, ai_ref, or_ref, oi_ref, cr_ref, ci_ref):
        @pl.when(pl.program_id(1) == 0)
        def _():
            cr_ref[...] = jnp.zeros_like(cr_ref)
            ci_ref[...] = jnp.zeros_like(ci_ref)

        row8 = _iota((8, lb), 0)
        ar, ai = ar_ref[...], ai_ref[...]
        a2 = _cmul(ar, ai, ar, ai)
        a4 = _cmul(*a2, *a2)
        pows = ((ar, ai), a2, a4)
        tr, ti = _power_table(ar, ai, row8, False)

        def body(i, carry):
            cr, ci = carry
            off = pl.multiple_of(i * 8, 8)
            xr, xi = _tile_scan(br_ref[pl.ds(off, 8), :], bi_ref[pl.ds(off, 8), :], pows, row8, False)
            mr, mi = _cmul(tr, ti, cr, ci)
            xr, xi = xr + mr, xi + mi
            or_ref[pl.ds(off, 8), :] = xr
            oi_ref[pl.ds(off, 8), :] = xi
            return _pick_row(xr, row8, 7), _pick_row(xi, row8, 7)

        cr, ci = lax.fori_loop(0, nt, body, (cr_ref[...], ci_ref[...]))
        cr_ref[...] = cr
        ci_ref[...] = ci

    blk = pl.BlockSpec((ts, lb), lambda j, i: (i, j))
    par = pl.BlockSpec((1, lb), lambda j, i: (0, j))
    return pl.pallas_call(
        kern, name="s5_scan_fwd", grid=(L // lb, S // ts), in_specs=[blk, blk, par, par], out_specs=[blk, blk],
        out_shape=[jax.ShapeDtypeStruct((S, L), F32)] * 2,
        scratch_shapes=[pltpu.VMEM((1, lb), F32), pltpu.VMEM((1, lb), F32)],
        compiler_params=_cparams(("parallel", "arbitrary")))(bu_re, bu_im, a_re, a_im)


def _scan_bwd(dst_re, dst_im, st_re, st_im, a_re, a_im):
    S, L = dst_re.shape
    ts, lb = min(SCAN_TS, S), SCAN_LB
    nt = ts // 8
    nb = S // ts
    r8 = ts // 8

    def kern(dr_ref, di_ref, sr_ref, si_ref, pr_ref, pi_ref, ar_ref, ai_ref, gr_ref, gi_ref, dar_ref, dai_ref,
             cr_ref, ci_ref):
        step = pl.program_id(1)
        blk = nb - 1 - step

        @pl.when(step == 0)
        def _():
            cr_ref[...] = jnp.zeros_like(cr_ref)
            ci_ref[...] = jnp.zeros_like(ci_ref)
            dar_ref[...] = jnp.zeros_like(dar_ref)
            dai_ref[...] = jnp.zeros_like(dai_ref)

        row8 = _iota((8, lb), 0)
        ar, ai = ar_ref[...], ai_ref[...]
        nai = -ai
        a2 = _cmul(ar, nai, ar, nai)
        a4 = _cmul(*a2, *a2)
        pows = ((ar, nai), a2, a4)
        tr, ti = _power_table(ar, nai, row8, True)
        halo_r = jnp.where(blk == 0, 0.0, pr_ref[...])
        halo_i = jnp.where(blk == 0, 0.0, pi_ref[...])

        def body(n, carry):
            cr, ci, acc_r, acc_i = carry
            i = nt - 1 - n
            off = pl.multiple_of(i * 8, 8)
            gr, gi = _tile_scan(dr_ref[pl.ds(off, 8), :], di_ref[pl.ds(off, 8), :], pows, row8, True)
            mr, mi = _cmul(tr, ti, cr, ci)
            gr, gi = gr + mr, gi + mi
            gr_ref[pl.ds(off, 8), :] = gr
            gi_ref[pl.ds(off, 8), :] = gi
            poff = pl.multiple_of(jnp.maximum(i - 1, 0) * 8, 8)
            before_r = jnp.where(i == 0, halo_r, sr_ref[pl.ds(poff, 8), :])
            before_i = jnp.where(i == 0, halo_i, si_ref[pl.ds(poff, 8), :])
            last_r = _pick_row(before_r, row8, 7)
            last_i = _pick_row(before_i, row8, 7)
            spr = jnp.where(row8 >= 1, pltpu.roll(sr_ref[pl.ds(off, 8), :], 1, 0), last_r)
            spi = jnp.where(row8 >= 1, pltpu.roll(si_ref[pl.ds(off, 8), :], 1, 0), last_i)
            acc_r = acc_r + gr * spr + gi * spi
            acc_i = acc_i + gi * spr - gr * spi
            return _pick_row(gr, row8, 0), _pick_row(gi, row8, 0), acc_r, acc_i

        zero = jnp.zeros((8, lb), F32)
        cr, ci, acc_r, acc_i = lax.fori_loop(0, nt, body, (cr_ref[...], ci_ref[...], zero, zero))
        cr_ref[...] = cr
        ci_ref[...] = ci
        dar_ref[...] += jnp.sum(acc_r, axis=0, keepdims=True)
        dai_ref[...] += jnp.sum(acc_i, axis=0, keepdims=True)

    blk = pl.BlockSpec((ts, lb), lambda j, i: (nb - 1 - i, j))
    halo = pl.BlockSpec((8, lb), lambda j, i: (jnp.maximum((nb - 1 - i) * r8 - 1, 0), j))
    par = pl.BlockSpec((1, lb), lambda j, i: (0, j))
    return pl.pallas_call(
        kern, name="s5_scan_bwd", grid=(L // lb, nb), in_specs=[blk, blk, blk, blk, halo, halo, par, par],
        out_specs=[blk, blk, par, par],
        out_shape=[jax.ShapeDtypeStruct((S, L), F32)] * 2 + [jax.ShapeDtypeStruct((1, L), F32)] * 2,
        scratch_shapes=[pltpu.VMEM((1, lb), F32), pltpu.VMEM((1, lb), F32)],
        compiler_params=_cparams(("parallel", "arbitrary")))(dst_re, dst_im, st_re, st_im, st_re, st_im, a_re, a_im)


def _loss_grad(x, target, gain, ts=256):
    S, D = x.shape

    def kern(x_ref, t_ref, g_ref, loss_ref, dx_ref, dg_ref):
        i = pl.program_id(0)
        tgt = t_ref[...]

        def f(xv, gv):
            err = _rms(xv, gv) - tgt
            return 0.5 * jnp.mean(err * err, axis=-1, keepdims=True)

        rowloss, vjp = jax.vjp(f, x_ref[...], g_ref[...])
        dx, dg = vjp(jnp.ones_like(rowloss))
        dx_ref[...] = dx

        @pl.when(i == 0)
        def _():
            loss_ref[...] = jnp.zeros_like(loss_ref)
            dg_ref[...] = jnp.zeros_like(dg_ref)

        loss_ref[...] += jnp.broadcast_to(jnp.sum(rowloss, axis=0, keepdims=True), loss_ref.shape)
        dg_ref[...] += dg

    row = pl.BlockSpec((ts, D), lambda i: (i, 0))
    return pl.pallas_call(
        kern, name="loss_grad", grid=(S // ts,), in_specs=[row, row, pl.BlockSpec((1, D), lambda i: (0, 0))],
        out_specs=[pl.BlockSpec((8, 128), lambda i: (0, 0)), row, pl.BlockSpec((1, D), lambda i: (0, 0))],
        out_shape=[jax.ShapeDtypeStruct((8, 128), F32), jax.ShapeDtypeStruct((S, D), F32),
                   jax.ShapeDtypeStruct((1, D), F32)],
        compiler_params=_cparams(("arbitrary",)))(x, target, gain)


def _rms_fwd(x, g, name):
    return _rowwise(_rms_fn, [_blk(x)], [g], [x.shape[1]], name, out_dtypes=[BF16])[0]


def _rms_bwd(x, g, dy, name, add=None):
    return _rowwise_bwd(_rms_fn, [_blk(x)], [g], [dy], name, adds=None if add is None else {0: add})


FFN_TC = 1408


def _common_fwd(x, mem, P, L):
    hx = _rms_fwd(x, P['xa_norm'], L + "xa_norm")
    q = _matmul(hx, P['xa_wq'], name=L + "xa_q")
    memn = _rms_fwd(mem, P['mem_norm'], L + "mem_norm")
    kv = _matmul(memn, P['xa_wkv'], name=L + "xa_kv")
    att = _rowwise(_xattn_fn, [_blk(q)], [kv], [1024], L + "xattn", out_dtypes=[BF16])[0]
    x2 = _matmul(att, P['xa_wo'], res=x, name=L + "xa_o")
    hf = _rms_fwd(x2, P['ffn_norm'], L + "ffn_norm")
    hu = _matmul(hf, P['ffn_w_up'], name=L + "ffn_up")
    cw = P['ffn_conv']
    act = _conv_post([(hu, 0, cw, 0), (hu, 2, cw, 2)], _ffn_post, 2, FFN_TC, L + "ffn_conv", out_dtype=BF16)
    x3 = _matmul(act, P['ffn_w_down'], res=x2, name=L + "ffn_down")
    return x3, (x, mem, hx, q, memn, kv, att, x2, hf, hu, act)


def _common_bwd(saved, dx3, P, L):
    x, mem, hx, q, memn, kv, att, x2, hf, hu, act = saved
    G = {}
    dact = _matmul(dx3, P['ffn_w_down'], "nt", name=L + "ffn_down_dx")
    G['ffn_w_down'] = _matmul(act, dx3, "tn", name=L + "ffn_down_dw")
    cw = P['ffn_conv']
    dhu_u, dcw_u, dhu_g, dcw_g = _conv_post_bwd([(hu, 0, cw, 0), (hu, 2, cw, 2)], _ffn_post, 2, FFN_TC, dact,
                                                L + "ffn_conv_bwd")
    G['ffn_conv'] = jnp.concatenate([dcw_u, dcw_g], axis=1)
    dhf = _matmul_cat([dhu_u, dhu_g], P['ffn_w_up'], "nt", name=L + "ffn_up_dx")
    G['ffn_w_up'] = jnp.concatenate([_matmul(hf, dhu_u, "tn", name=L + "ffn_up_dw_up"),
                                     _matmul(hf, dhu_g, "tn", name=L + "ffn_up_dw_gate")], axis=1)
    dx2, G['ffn_norm'] = _rms_bwd(x2, P['ffn_norm'], dhf, L + "ffn_norm_bwd", add=dx3)
    datt = _matmul(dx2, P['xa_wo'], "nt", name=L + "xa_o_dx")
    G['xa_wo'] = _matmul(att, dx2, "tn", name=L + "xa_o_dw")
    dq, dkv = _rowwise_bwd(_xattn_fn, [_blk(q)], [kv], [datt], L + "xattn_bwd", out_dtypes=[BF16])
    dhx = _matmul(dq, P['xa_wq'], "nt", name=L + "xa_q_dx")
    G['xa_wq'] = _matmul(hx, dq, "tn", name=L + "xa_q_dw")
    dmemn = _matmul(dkv, P['xa_wkv'], "nt", name=L + "xa_kv_dx")
    G['xa_wkv'] = _matmul(memn, dkv, "tn", name=L + "xa_kv_dw")
    _, G['mem_norm'] = _rms_bwd(mem, P['mem_norm'], dmemn, L + "mem_norm_bwd")
    dx, G['xa_norm'] = _rms_bwd(x, P['xa_norm'], dhx, L + "xa_norm_bwd", add=dx2)
    return dx, G


U_COLS = (2048, 512)


def _even_fwd(x, P):
    S = x.shape[0]
    h0 = _rms_fwd(x, P['mix_norm'], "l0_mix_norm")
    proj = _matmul(h0, P['w_in'], name="l0_in")
    tabs = _ret_tables(S)
    o_raw, rstates = _ret_call(proj, tabs)
    o = _rowwise(_ret_post_fn, [_blk(o_raw), _blk(proj, 512, 3)], [P['ret_norm']], [512], "l0_ret_post",
                 out_dtypes=[BF16])[0]
    prep_in = (P['s5_lambda_re'], P['s5_lambda_im'], P['s5_log_dt'], P['s5_b_re'], P['s5_b_im'], P['s5_c_re'],
               P['s5_c_im'])
    a_re, a_im, bd_re, bd_im, cd_re, cd_im = _s5_prep(prep_in)
    a_re_f, a_im_f = a_re.reshape(1, 2048), a_im.reshape(1, 2048)
    bu_re = _matmul(proj, bd_re, name="l0_s5_bu_re", a_cols=U_COLS)
    bu_im = _matmul(proj, bd_im, name="l0_s5_bu_im", a_cols=U_COLS)
    st_re, st_im = _scan_fwd(bu_re, bu_im, a_re_f, a_im_f)
    y1 = _matmul(st_re, cd_re, name="l0_s5_y_re")
    y2 = _matmul(st_im, cd_im, name="l0_s5_y_im")
    yg = _rowwise(_s5_post_fn, [_blk(y1), _blk(y2), _blk(proj, 512, 4)],
                  [P['s5_d'], P['s5_w_glu'], P['s5_b_glu']], [512], "l0_s5_post", out_dtypes=[BF16])[0]
    x1 = _matmul_cat([o, yg], P['w_out'], "nn", res=x, name="l0_out")
    saved = (x, h0, proj, tabs, o_raw, rstates, prep_in, a_re_f, a_im_f, bd_re, bd_im, cd_re, cd_im, st_re, st_im,
             y1, y2, o, yg)
    return x1, saved


def _even_bwd(saved, dx1, P):
    (x, h0, proj, tabs, o_raw, rstates, prep_in, a_re_f, a_im_f, bd_re, bd_im, cd_re, cd_im, st_re, st_im, y1, y2,
     o, yg) = saved
    G = {}
    dmerged = _matmul(dx1, P['w_out'], "nt", name="l0_out_dx")
    G['w_out'] = jnp.concatenate([_matmul(o, dx1, "tn", name="l0_out_dw_ret"),
                                  _matmul(yg, dx1, "tn", name="l0_out_dw_s5")], axis=0)
    do_raw, dgate, G['ret_norm'] = _rowwise_bwd(
        _ret_post_fn, [_blk(o_raw), _blk(proj, 512, 3)], [P['ret_norm']], [_blk(dmerged, 512, 0)], "l0_ret_post_bwd",
        out_dtypes=[F32, BF16])
    dq, dk, dv = _ret_call(proj, tabs, states=rstates, do=do_raw)
    dy1, dy2, du_a, G['s5_d'], G['s5_w_glu'], G['s5_b_glu'] = _rowwise_bwd(
        _s5_post_fn, [_blk(y1), _blk(y2), _blk(proj, 512, 4)], [P['s5_d'], P['s5_w_glu'], P['s5_b_glu']],
        [_blk(dmerged, 512, 1)], "l0_s5_post_bwd", out_dtypes=[BF16, BF16, F32])
    dst_re = _matmul(dy1, cd_re, "nt", name="l0_s5_y_re_dx")
    dcd_re = _matmul(st_re, dy1, "tn", name="l0_s5_y_re_dw")
    dst_im = _matmul(dy2, cd_im, "nt", name="l0_s5_y_im_dx")
    dcd_im = _matmul(st_im, dy2, "tn", name="l0_s5_y_im_dw")
    dbu_re, dbu_im, da_re, da_im = _scan_bwd(dst_re, dst_im, st_re, st_im, a_re_f, a_im_f)
    du = _matmul(dbu_re, bd_re, "nt", res=du_a, name="l0_s5_bu_re_dx")
    du = _matmul(dbu_im, bd_im, "nt", res=du, name="l0_s5_bu_im_dx", out_dtype=BF16)
    dbd_re = _matmul(proj, dbu_re, "tn", name="l0_s5_bu_re_dw", a_cols=U_COLS)
    dbd_im = _matmul(proj, dbu_im, "tn", name="l0_s5_bu_im_dw", a_cols=U_COLS)
    dprep = _s5_prep(prep_in, cots=(da_re.reshape(32, 64), da_im.reshape(32, 64), dbd_re, dbd_im, dcd_re, dcd_im))
    for n, d in zip(('s5_lambda_re', 's5_lambda_im', 's5_log_dt', 's5_b_re', 's5_b_im', 's5_c_re', 's5_c_im'), dprep):
        G[n] = d
    pieces = [dq, dk, dv, dgate, du]
    dh0 = _matmul_cat(pieces, P['w_in'], "nt", name="l0_in_dx")
    G['w_in'] = jnp.concatenate([_matmul(h0, p, "tn", name="l0_in_dw_%d" % n) for n, p in enumerate(pieces)], axis=1)
    dx, G['mix_norm'] = _rms_bwd(x, P['mix_norm'], dh0, "l0_mix_norm_bwd", add=dx1)
    return dx, G


def _odd_fwd(x, P):
    h1 = _rms_fwd(x, P['mix_norm'], "l1_mix_norm")
    pm = _matmul(h1, P['w_main'], name="l1_in_main")
    pt = _matmul(h1, P['w_tail'], name="l1_in_tail")
    qkv = _conv_post([(pm, 0, P['conv'], 0)], _silu, 3, 1024, "l1_conv")
    g_e, beta_e = _rowwise(_gdn_gates_fn, [_blk(pt)], [P['a_log_p'], P['dtb_p']], [1024, 1024], "l1_gdn_gates")
    w, u, qd, kd, qk, tinv = _gdn_intra_call(qkv, g_e, beta_e)
    o_raw, gstates = _gdn_pass(w, u, qd, kd, qk, g_e)
    og = _rowwise(_gdn_post_fn, [_blk(o_raw), _blk(pm, 1024, 3)], [P['o_norm']], [1024], "l1_gdn_post",
                  out_dtypes=[BF16])[0]
    x1 = _matmul(og, P['w_out'], res=x, name="l1_out")
    return x1, (x, h1, pm, pt, qkv, g_e, beta_e, w, u, qd, kd, qk, tinv, o_raw, gstates, og)


def _odd_bwd(saved, dx1, P):
    x, h1, pm, pt, qkv, g_e, beta_e, w, u, qd, kd, qk, tinv, o_raw, gstates, og = saved
    G = {}
    dog = _matmul(dx1, P['w_out'], "nt", name="l1_out_dx")
    G['w_out'] = _matmul(og, dx1, "tn", name="l1_out_dw")
    do_raw, dz, G['o_norm'] = _rowwise_bwd(_gdn_post_fn, [_blk(o_raw), _blk(pm, 1024, 3)], [P['o_norm']], [dog],
                                           "l1_gdn_post_bwd", out_dtypes=[F32, BF16])
    dw, du, dqd, dkd, dqk, dg_pass = _gdn_pass(w, u, qd, kd, qk, g_e, states=gstates, do=do_raw)
    dqkv = _gdn_intra_call(qkv, g_e, beta_e, cots=(dw, du, dqd, dkd, dqk, dg_pass, tinv))
    dg_e, dbeta_e = dqkv[3], dqkv[4]
    dpt, G['a_log_p'], G['dtb_p'] = _rowwise_bwd(_gdn_gates_fn, [_blk(pt)], [P['a_log_p'], P['dtb_p']],
                                                 [dg_e, dbeta_e], "l1_gdn_gates_bwd", out_dtypes=[BF16])
    pieces, dcw = [], []
    for part in range(3):
        dxp, dwp = _conv_post_bwd([(pm, part, P['conv'], part)], _silu, 1, 1024, dqkv[part],
                                  "l1_conv_bwd_%d" % part)
        pieces.append(dxp)
        dcw.append(dwp)
    G['conv'] = jnp.concatenate(dcw, axis=1)
    pieces += [dz, dpt]
    dh1 = _matmul_cat(pieces, P['w_all'], "nt", name="l1_in_dx")
    G['w_all'] = jnp.concatenate([_matmul(h1, p, "tn", name="l1_in_dw_%d" % n) for n, p in enumerate(pieces)], axis=1)
    dx, G['mix_norm'] = _rms_bwd(x, P['mix_norm'], dh1, "l1_mix_norm_bwd", add=dx1)
    return dx, G


def _row(v):
    return v.reshape(1, -1)


def _local_step(x, mem, target, W, later_weights=None, early_grads=None):
    P0 = {
        'mix_norm': _row(W['l0_mix_norm']), 'w_in': W['l0_w_in'], 'ret_norm': _row(W['l0_ret_norm']),
        's5_lambda_re': W['l0_s5_lambda_re'], 's5_lambda_im': W['l0_s5_lambda_im'],
        's5_log_dt': W['l0_s5_log_dt'].reshape(32, 1),
        's5_b_re': W['l0_s5_b_re'].reshape(512, 64), 's5_b_im': W['l0_s5_b_im'].reshape(512, 64),
        's5_c_re': W['l0_s5_c_re'].reshape(2048, 16), 's5_c_im': W['l0_s5_c_im'].reshape(2048, 16),
        's5_d': _row(W['l0_s5_d']), 's5_w_glu': W['l0_s5_w_glu'].astype(F32), 's5_b_glu': _row(W['l0_s5_b_glu']),
        'w_out': W['l0_w_out'],
    }
    def common(L):
        return {'xa_norm': _row(W[L + 'xa_norm']), 'mem_norm': _row(W[L + 'mem_norm']), 'xa_wq': W[L + 'xa_wq'],
                'xa_wkv': W[L + 'xa_wkv'], 'xa_wo': W[L + 'xa_wo'], 'ffn_norm': _row(W[L + 'ffn_norm']),
                'ffn_w_up': W[L + 'ffn_w_up'], 'ffn_conv': W[L + 'ffn_conv'], 'ffn_w_down': W[L + 'ffn_w_down']}

    x1, s_even = _even_fwd(x, P0)
    if later_weights is not None:
        W = dict(W, **later_weights('l0_common', x1))
    C0 = common('l0_')
    x3, s_c0 = _common_fwd(x1, mem, C0, "l0_")

    if later_weights is not None:
        W = dict(W, **later_weights('l1', x3))
    w_in1 = W['l1_w_in']
    pad8 = jnp.zeros((8,), F32)
    w_all = jnp.pad(w_in1, ((0, 0), (0, 112)))
    P1 = {
        'mix_norm': _row(W['l1_mix_norm']), 'w_main': w_in1[:, :4096], 'w_tail': w_all[:, 4096:], 'w_all': w_all,
        'conv': W['l1_conv'],
        'a_log_p': _row(jnp.concatenate([pad8, W['l1_a_log'], jnp.zeros((112,), F32)])),
        'dtb_p': _row(jnp.concatenate([pad8, W['l1_dt_bias'], jnp.zeros((112,), F32)])),
        'o_norm': _row(W['l1_o_norm']), 'w_out': W['l1_w_out'],
    }
    C1 = common('l1_')
    x4, s_odd = _odd_fwd(x3, P1)
    x6, s_c1 = _common_fwd(x4, mem, C1, "l1_")
    loss_tile, dx6, d_final = _loss_grad(x6, target, _row(W['final_norm']))

    G = {'final_norm': d_final.reshape(-1)}
    dx4, g = _common_bwd(s_c1, dx6, C1, "l1_")
    for k, v in g.items():
        G['l1_' + k] = v
    dx3, g = _odd_bwd(s_odd, dx4, P1)
    G['l1_mix_norm'] = g['mix_norm']
    G['l1_w_in'] = g['w_all'][:, :4112]
    G['l1_conv'] = g['conv']
    G['l1_a_log'] = g['a_log_p'][0, 8:16]
    G['l1_dt_bias'] = g['dtb_p'][0, 8:16]
    G['l1_o_norm'] = g['o_norm']
    G['l1_w_out'] = g['w_out']
    if early_grads is not None:
        zero = early_grads('l1', G)
        C0 = dict(C0, ffn_w_down=C0['ffn_w_down'] + zero.astype(C0['ffn_w_down'].dtype))
    dx1, g = _common_bwd(s_c0, dx3, C0, "l0_")
    for k, v in g.items():
        G['l0_' + k] = v
    if early_grads is not None:
        zero = early_grads('l0_common', G)
        P0 = dict(P0, w_out=P0['w_out'] + zero.astype(P0['w_out'].dtype))
    dx0, g = _even_bwd(s_even, dx1, P0)
    for k, v in g.items():
        G['l0_' + k] = v
    return loss_tile, dx0, G


ANY = pl.BlockSpec(memory_space=pl.ANY)


def _place():
    return lax.axis_index("x"), lax.axis_index("y"), lax.axis_index("c")


def _my_chip():
    return 2 * lax.axis_index("x") + lax.axis_index("y")


def _chip_peers(x, y):
    return [(1 - x, y), (x, 1 - y), (1 - x, 1 - y)]


def _half(ref, mode, shard, j, h, split):
    r, w = shard
    rh = r // 2 if split else r
    h = h if split else 0
    if mode == 'row':
        return ref.at[pl.ds(j * r + h * rh, rh), :]
    if mode == 'col':
        return ref.at[pl.ds(h * rh, rh), pl.ds(j * w, w)]
    return ref.at[j, pl.ds(h * rh, rh), :]


def _place_shard(shard, mode, name):
    r, w = shard.shape
    dtype = BF16 if mode != 'tap' else shard.dtype
    if mode == 'tap':
        mode = 'slab'
    tr = _row_tile(r, w)
    nb = r // tr

    def kern(s_ref, o_ref):
        o_ref[...] = s_ref[...].astype(o_ref.dtype)

    if mode == 'row':
        full, o_spec = (4 * r, w), pl.BlockSpec((tr, w), lambda i: (_my_chip() * nb + i, 0))
    elif mode == 'col':
        full, o_spec = (r, 4 * w), pl.BlockSpec((tr, w), lambda i: (i, _my_chip()))
    else:
        full, o_spec = (4, r, w), pl.BlockSpec((None, tr, w), lambda i: (_my_chip(), i, 0))
    return pl.pallas_call(kern, name=name, grid=(nb,), in_specs=[pl.BlockSpec((tr, w), lambda i: (i, 0))],
                          out_specs=o_spec, out_shape=jax.ShapeDtypeStruct(full, dtype),
                          compiler_params=_cparams(("parallel",)))(shard)


def _gather_placed(fulls, modes, shards, splits):
    n = len(fulls)

    def body(*refs):
        outs = refs[n:2 * n]
        send_sems, recv_sems = refs[2 * n:]
        x, y, c = _place()
        peers = _chip_peers(x, y)
        me = 2 * x + y

        def win(a, j, h):
            return _half(outs[a], modes[a], shards[a], j, h, splits[a])

        def copy(a, k, j, h, to):
            return pltpu.make_async_remote_copy(src_ref=win(a, j, h), dst_ref=win(a, j, h),
                                                send_sem=send_sems.at[6 * a + k], recv_sem=recv_sems.at[6 * a + k],
                                                device_id=to, device_id_type=MESH)

        over_ici = [copy(a, k, me, c, (p[0], p[1], c)) for a in range(n) for k, p in enumerate(peers)]
        for cp in over_ici:
            cp.start()
        passed = []
        for a in range(n):
            for k, p in enumerate(peers):
                j = 2 * p[0] + p[1]
                copy(a, k, j, c, (p[0], p[1], c)).wait_recv()
                if splits[a]:
                    fwd = copy(a, 3 + k, j, c, (x, y, 1 - c))
                    fwd.start()
                    passed.append(fwd)
        for a in range(n):
            if splits[a]:
                for k, p in enumerate(peers):
                    copy(a, 3 + k, 2 * p[0] + p[1], 1 - c, (x, y, 1 - c)).wait_recv()
        for cp in over_ici + passed:
            cp.wait_send()

    return pl.pallas_call(
        body, name="gather_weights", in_specs=[ANY] * n, out_specs=[ANY] * n,
        out_shape=[jax.ShapeDtypeStruct(f.shape, f.dtype) for f in fulls],
        input_output_aliases={a: a for a in range(n)},
        scratch_shapes=[pltpu.SemaphoreType.DMA((6 * n,)), pltpu.SemaphoreType.DMA((6 * n,))],
    )(*fulls)


_FLIPS = [(dx, dy, dc) for dx in (0, 1) for dy in (0, 1) for dc in (0, 1) if (dx, dy, dc) != (0, 0, 0)]


def _send_other_half(gs, small, name):
    n = len(gs)

    def body(*refs):
        ins, outs = refs[:n], refs[n + 1:2 * n + 1]
        small_ref = refs[2 * n + 1]
        send_sems, recv_sems, small_send, small_recv = refs[2 * n + 2:]
        x, y, c = _place()
        me = 4 * x + 2 * y + c

        def peer(f):
            return (x ^ f[0], y ^ f[1], c ^ f[2])

        def small_copy(k, slab, to):
            return pltpu.make_async_remote_copy(src_ref=small_ref.at[slab], dst_ref=small_ref.at[slab],
                                                send_sem=small_send.at[k], recv_sem=small_recv.at[k], device_id=to,
                                                device_id_type=MESH)

        cps = []
        for a in range(n):
            rh = gs[a].shape[1] // 2
            cps.append(pltpu.make_async_remote_copy(
                src_ref=ins[a].at[:, pl.ds((1 - c) * rh, rh), :], dst_ref=outs[a], send_sem=send_sems.at[a],
                recv_sem=recv_sems.at[a], device_id=(x, y, 1 - c), device_id_type=MESH))
        smalls = [small_copy(k, me, peer(f)) for k, f in enumerate(_FLIPS)]
        for cp in cps + smalls:
            cp.start()
        for cp in cps:
            cp.wait()
        for k, f in enumerate(_FLIPS):
            p = peer(f)
            small_copy(k, 4 * p[0] + 2 * p[1] + p[2], p).wait_recv()
        for cp in smalls:
            cp.wait_send()

    outs = pl.pallas_call(
        body, name=name, in_specs=[ANY] * (n + 1), out_specs=[ANY] * (n + 1),
        out_shape=[jax.ShapeDtypeStruct((g.shape[0], g.shape[1] // 2, g.shape[2]), g.dtype) for g in gs]
        + [jax.ShapeDtypeStruct(small.shape, small.dtype)],
        input_output_aliases={n: n},
        scratch_shapes=[pltpu.SemaphoreType.DMA((n,)), pltpu.SemaphoreType.DMA((n,)),
                        pltpu.SemaphoreType.DMA((7,)), pltpu.SemaphoreType.DMA((7,))],
    )(*gs, small)
    return outs[:n], outs[n]


def _send_to_chips(ps, widths):
    n = len(ps)

    def body(*refs):
        ins, outs = refs[:n], refs[n:2 * n]
        send_sems, recv_sems = refs[2 * n:]
        x, y, c = _place()
        peers = _chip_peers(x, y)
        me = 2 * x + y

        def src(a, j):
            if ps[a].shape[0] == 4:
                return ins[a].at[j]
            return ins[a].at[0, :, pl.ds(j * widths[a], widths[a])]

        def copy(a, k, j, dst_slab, to):
            return pltpu.make_async_remote_copy(src_ref=src(a, j), dst_ref=outs[a].at[dst_slab],
                                                send_sem=send_sems.at[3 * a + k], recv_sem=recv_sems.at[3 * a + k],
                                                device_id=(to[0], to[1], c), device_id_type=MESH)

        sends = [copy(a, k, 2 * p[0] + p[1], me, p) for a in range(n) for k, p in enumerate(peers)]
        for cp in sends:
            cp.start()
        for a in range(n):
            for k, p in enumerate(peers):
                copy(a, k, me, 2 * p[0] + p[1], p).wait_recv()
        for cp in sends:
            cp.wait_send()

    return pl.pallas_call(
        body, name="send_to_chips", in_specs=[ANY] * n, out_specs=[ANY] * n,
        out_shape=[jax.ShapeDtypeStruct((4, p.shape[1], w), p.dtype) for p, w in zip(ps, widths)],
        scratch_shapes=[pltpu.SemaphoreType.DMA((3 * n,)), pltpu.SemaphoreType.DMA((3 * n,))],
    )(*ps)


def _share_halves(bufs, name):
    n = len(bufs)

    def body(*refs):
        outs = refs[n:2 * n]
        send_sems, recv_sems = refs[2 * n:]
        x, y, c = _place()
        sends, waits = [], []
        for a in range(n):
            rh = bufs[a].shape[0] // 2
            mine = outs[a].at[pl.ds(c * rh, rh), :]
            other = outs[a].at[pl.ds((1 - c) * rh, rh), :]
            sends.append(pltpu.make_async_remote_copy(src_ref=mine, dst_ref=mine, send_sem=send_sems.at[a],
                                                      recv_sem=recv_sems.at[a], device_id=(x, y, 1 - c),
                                                      device_id_type=MESH))
            waits.append(pltpu.make_async_remote_copy(src_ref=mine, dst_ref=other, send_sem=send_sems.at[a],
                                                      recv_sem=recv_sems.at[a], device_id=(x, y, 1 - c),
                                                      device_id_type=MESH))
        for cp in sends:
            cp.start()
        for cp in waits:
            cp.wait()

    return pl.pallas_call(
        body, name=name, in_specs=[ANY] * n, out_specs=[ANY] * n,
        out_shape=[jax.ShapeDtypeStruct(b.shape, b.dtype) for b in bufs],
        input_output_aliases={a: a for a in range(n)},
        scratch_shapes=[pltpu.SemaphoreType.DMA((n,)), pltpu.SemaphoreType.DMA((n,))],
    )(*bufs)


def _gather_all(mine):
    flips = [(dx, dy, dc) for dx in (0, 1) for dy in (0, 1) for dc in (0, 1) if (dx, dy, dc) != (0, 0, 0)]

    def body(x_ref, out_ref, send_sems, recv_sems, local_sem):
        x, y, c = _place()
        me = 4 * x + 2 * y + c

        def peer(f):
            return (x ^ f[0], y ^ f[1], c ^ f[2])

        def copy(k, slab, to):
            return pltpu.make_async_remote_copy(src_ref=x_ref, dst_ref=out_ref.at[slab], send_sem=send_sems.at[k],
                                                recv_sem=recv_sems.at[k], device_id=to, device_id_type=MESH)

        own = pltpu.make_async_copy(x_ref, out_ref.at[me], local_sem)
        own.start()
        sends = [copy(k, me, peer(f)) for k, f in enumerate(flips)]
        for s in sends:
            s.start()
        for k, f in enumerate(flips):
            p = peer(f)
            copy(k, 4 * p[0] + 2 * p[1] + p[2], p).wait_recv()
        for s in sends:
            s.wait_send()
        own.wait()

    return pl.pallas_call(
        body, name="gather_all", in_specs=[ANY], out_specs=ANY,
        out_shape=jax.ShapeDtypeStruct((8,) + mine.shape, mine.dtype),
        scratch_shapes=[pltpu.SemaphoreType.DMA((7,)), pltpu.SemaphoreType.DMA((7,)), pltpu.SemaphoreType.DMA],
    )(mine)


TILE_BYTES = 2 * 1024 * 1024


def _row_tile(rows, width=1024):
    for t in (512, 352, 256, 176, 128, 64, 32, 16, 8):
        if rows % t == 0 and t * width * 4 <= TILE_BYTES:
            return t
    return rows


def _pair_sum(g, got, name):
    ns, r, w = g.shape
    rh = r // 2
    tr = _row_tile(rh, w)
    nb = rh // tr

    def kern(g_ref, o_ref, out_ref):
        out_ref[...] = (g_ref[...] + o_ref[...]).astype(BF16)

    return pl.pallas_call(
        kern, name=name, grid=(ns, nb),
        in_specs=[pl.BlockSpec((None, tr, w), lambda j, i: (j, lax.axis_index("c") * nb + i, 0)),
                  pl.BlockSpec((None, tr, w), lambda j, i: (j, i, 0))],
        out_specs=pl.BlockSpec((None, tr, w), lambda j, i: (j, i, 0)),
        out_shape=jax.ShapeDtypeStruct((ns, rh, w), BF16),
        compiler_params=_cparams(("parallel", "parallel")))(g, got)


def _chip_sum(pair, recv, w, name):
    rh = pair.shape[1]
    tr = _row_tile(rh, w)
    nb = rh // tr

    def kern(own_ref, r1_ref, r2_ref, r3_ref, out_ref):
        acc = own_ref[...].astype(F32)
        for r_ref in (r1_ref, r2_ref, r3_ref):
            acc = acc + r_ref[...].astype(F32)
        out_ref[...] = acc

    if pair.shape[0] == 4:
        own_spec = pl.BlockSpec((None, tr, w), lambda i: (_my_chip(), i, 0))
    else:
        own_spec = pl.BlockSpec((None, tr, w), lambda i: (0, i, _my_chip()))
    recv_specs = [pl.BlockSpec((None, tr, w), functools.partial(lambda i, d: ((_my_chip() + d) % 4, i, 0), d=d))
                  for d in (1, 2, 3)]
    return pl.pallas_call(
        kern, name=name, grid=(nb,), in_specs=[own_spec] + recv_specs,
        out_specs=pl.BlockSpec((tr, w), lambda i: (lax.axis_index("c") * nb + i, 0)),
        out_shape=jax.ShapeDtypeStruct((2 * rh, w), F32), compiler_params=_cparams(("parallel",)))(pair, recv, recv, recv)


def _slab_sum(slabs, name):
    n, R, w = slabs.shape
    tr = _row_tile(R)

    def kern(s_ref, o_ref):
        acc = s_ref[0].astype(F32)
        for k in range(1, n):
            acc = acc + s_ref[k].astype(F32)
        o_ref[...] = acc

    return pl.pallas_call(
        kern, name=name, grid=(R // tr,), in_specs=[pl.BlockSpec((n, tr, w), lambda i: (0, i, 0))],
        out_specs=pl.BlockSpec((tr, w), lambda i: (i, 0)), out_shape=jax.ShapeDtypeStruct((R, w), F32),
        compiler_params=_cparams(("parallel",)))(slabs)


def _adamw(w, g, m, v, name):
    R, C = w.shape
    tr = _pick(R, (256, 128, 64, 32, 16, 8))

    def kern(w_ref, g_ref, m_ref, v_ref, d_ref, nm_ref, nv_ref, g_out_ref):
        gv = g_ref[...]
        g_out_ref[...] = gv
        m2 = ADAM_B1 * m_ref[...] + (1.0 - ADAM_B1) * gv
        v2 = ADAM_B2 * v_ref[...] + (1.0 - ADAM_B2) * jnp.square(gv)
        m_hat = m2 / (1.0 - ADAM_B1 ** ADAM_STEP)
        v_hat = v2 / (1.0 - ADAM_B2 ** ADAM_STEP)
        d_ref[...] = -ADAM_LR * (m_hat / (jnp.sqrt(v_hat) + ADAM_EPS) + ADAM_WD * w_ref[...])
        nm_ref[...] = m2
        nv_ref[...] = v2

    spec = pl.BlockSpec((tr, C), lambda i: (i, 0))
    return pl.pallas_call(
        kern, name=name, grid=(R // tr,), in_specs=[spec] * 4, out_specs=[spec] * 4,
        out_shape=[jax.ShapeDtypeStruct((R, C), F32)] * 4, compiler_params=_cparams(("parallel",)))(w, g, m, v)


def _pack_small(vals):
    flat = jnp.concatenate([vals[n].astype(F32).reshape(-1) for n in SMALL_NAMES])
    rows = -(-flat.shape[0] // (8 * LANES)) * 8
    return jnp.pad(flat, (0, rows * LANES - flat.shape[0])).reshape(rows, LANES)


def _unpack_small(packed, shapes):
    flat = packed.reshape(-1)
    out = {}
    off = 0
    for n in SMALL_NAMES:
        size = int(np.prod(shapes[n]))
        out[n] = flat[off:off + size].reshape(shapes[n])
        off += size
    return out


HBM = pl.BlockSpec(memory_space=pltpu.HBM)
SEM = pl.BlockSpec(memory_space=pltpu.SEMAPHORE)
DATAFLOW = pltpu.SideEffectType.DATAFLOW_SIDE_EFFECTING


def _in_hbm(a):
    return pltpu.with_memory_space_constraint(a, pltpu.HBM)


def _split_copy_start(srcs, lands, copies, after, name):
    ns, nl = len(srcs), len(lands)
    ncopy = len(copies(list(srcs), list(lands), None, None, probe=True))

    def body(*refs):
        src_refs, land_refs = refs[:ns], refs[ns:ns + nl]
        send_sems, recv_sems = refs[ns + nl + 1:ns + nl + 3]
        token = refs[-1]
        for cp in copies(src_refs, land_refs, send_sems, recv_sems):
            cp.start()
        token[...] = jnp.zeros_like(token)

    outs = pl.pallas_call(
        body, name=name,
        out_shape=(pltpu.SemaphoreType.DMA((ncopy,)), pltpu.SemaphoreType.DMA((ncopy,)),
                   *[pltpu.HBM(a.shape, a.dtype) for a in srcs], *[pltpu.HBM(a.shape, a.dtype) for a in lands],
                   jax.ShapeDtypeStruct((8, 128), F32)),
        in_specs=[HBM] * (ns + nl) + [ANY],
        out_specs=(SEM, SEM, *[HBM] * (ns + nl), pl.BlockSpec(memory_space=pltpu.VMEM)),
        input_output_aliases={i: 2 + i for i in range(ns + nl)},
        compiler_params=pltpu.CompilerParams(has_side_effects=DATAFLOW),
    )(*[_in_hbm(a) for a in srcs], *[_in_hbm(a) for a in lands], after)
    return outs[0], outs[1], outs[2:2 + ns], outs[2 + ns:2 + ns + nl], outs[-1]


def _split_copy_wait(send_sems, recv_sems, srcs, lands, copies, after, name):
    ns, nl = len(srcs), len(lands)

    def body(*refs):
        src_refs, land_refs = refs[:ns], refs[ns:ns + nl]
        send_ref, recv_ref = refs[ns + nl:ns + nl + 2]
        for cp in copies(src_refs, land_refs, send_ref, recv_ref):
            cp.wait_send()
            cp.wait_recv()

    outs = pl.pallas_call(
        body, name=name,
        out_shape=tuple(pltpu.HBM(a.shape, a.dtype) for a in list(srcs) + list(lands)),
        in_specs=[HBM] * (ns + nl) + [SEM, SEM, ANY], out_specs=tuple([HBM] * (ns + nl)),
        input_output_aliases={i: i for i in range(ns + nl)},
        compiler_params=pltpu.CompilerParams(has_side_effects=DATAFLOW),
    )(*srcs, *lands, send_sems, recv_sems, after)
    return outs[:ns], outs[ns:]


def _matrix_mode(n):
    return 'slab' if n == 'l1_w_in' else ('row' if MATRICES[n] == 0 else 'col')


def _placed(A, names):
    modes = ['slab' if n in CONVS else _matrix_mode(n) for n in names]
    fulls = [_place_shard(A[n], 'tap' if n in CONVS else m, "place_" + n) for n, m in zip(names, modes)]
    return fulls, modes


def _assembled(names, modes, outs):
    return {n: jnp.concatenate([o[j] for j in range(4)], axis=1) if m == 'slab' else o
            for n, m, o in zip(names, modes, outs)}


def _gather_weights(A, names):
    fulls, modes = _placed(A, names)
    outs = _gather_placed(fulls, modes, [A[n].shape for n in names], [n not in CONVS for n in names])
    return _assembled(names, modes, outs)


def _whole_shard_copies(modes, shards):
    def copies(src_refs, land_refs, send_sems, recv_sems, probe=False):
        if probe:
            return [None] * (3 * len(land_refs))
        x, y, c = _place()
        me = 2 * x + y
        out = []
        for a, ref in enumerate(land_refs):
            for k, p in enumerate(_chip_peers(x, y)):
                out.append(pltpu.make_async_remote_copy(
                    src_ref=_half(ref, modes[a], shards[a], me, 0, False),
                    dst_ref=_half(ref, modes[a], shards[a], me, 0, False),
                    send_sem=send_sems.at[3 * a + k], recv_sem=recv_sems.at[3 * a + k],
                    device_id=(p[0], p[1], c), device_id_type=MESH))
        return out
    return copies


def _gather_weights_start(A, names, after, tag):
    fulls, modes = _placed(A, names)
    copies = _whole_shard_copies(modes, [A[n].shape for n in names])
    send_sems, recv_sems, _, lands, zeros = _split_copy_start([], fulls, copies, after, "gather_start_" + tag)
    return (send_sems, recv_sems, lands, copies, names, modes), zeros


def _gather_weights_wait(state, after, tag):
    send_sems, recv_sems, lands, copies, names, modes = state
    _, outs = _split_copy_wait(send_sems, recv_sems, [], lands, copies, after, "gather_wait_" + tag)
    return _assembled(names, modes, outs)


def _to_chips_copies(pair_shapes, widths):
    def copies(src_refs, land_refs, send_sems, recv_sems, probe=False):
        if probe:
            return [None] * (3 * len(land_refs))
        x, y, c = _place()
        me = 2 * x + y
        out = []
        for a, (src, land) in enumerate(zip(src_refs, land_refs)):
            for k, p in enumerate(_chip_peers(x, y)):
                j = 2 * p[0] + p[1]
                part = src.at[j] if pair_shapes[a][0] == 4 else src.at[0, :, pl.ds(j * widths[a], widths[a])]
                out.append(pltpu.make_async_remote_copy(
                    src_ref=part, dst_ref=land.at[me], send_sem=send_sems.at[3 * a + k],
                    recv_sem=recv_sems.at[3 * a + k], device_id=(p[0], p[1], c), device_id_type=MESH))
        return out
    return copies


def _reduce_begin(G, names, small, tag):
    gs, widths = [], []
    for n in names:
        g = G[n]
        mode = _matrix_mode(n)
        if mode == 'row':
            gs.append(g.reshape(4, g.shape[0] // 4, g.shape[1]))
            widths.append(g.shape[1])
        elif mode == 'col':
            gs.append(g[None])
            widths.append(g.shape[1] // 4)
        else:
            wd = g.shape[1] // 4
            gs.append(jnp.stack([g[:, j * wd:(j + 1) * wd] for j in range(4)]))
            widths.append(wd)
    got, small = _send_other_half(gs, small, "send_other_half_" + tag)
    pairs = [_pair_sum(g, o, "pair_sum_" + n) for n, g, o in zip(names, gs, got)]
    return pairs, widths, small


def _reduce_end(names, pairs, recv, widths, tag):
    halves = [_chip_sum(p, r, w, "chip_sum_" + n) for n, p, r, w in zip(names, pairs, recv, widths)]
    return dict(zip(names, _share_halves(halves, "share_halves_" + tag)))


def _small_slab(packed):
    me8 = 4 * lax.axis_index("x") + 2 * lax.axis_index("y") + lax.axis_index("c")
    return lax.dynamic_update_slice(jnp.zeros((8,) + packed.shape, F32), packed[None], (me8, 0, 0))


def kernel(*args):
    A = dict(zip(ARG_NAMES, args, strict=True))
    x, mem, target = A['x'][0], A['mem'][0], A['loss_target'][0]

    stages = {'l0_mixer': ['l0_w_in', 'l0_s5_w_glu', 'l0_w_out'],
              'l0_common': [n for n in MATRIX_NAMES if n.startswith(('l0_xa_', 'l0_ffn_'))],
              'l1': [n for n in MATRIX_NAMES if n.startswith('l1_')]}
    W = _gather_weights(A, stages['l0_mixer'] + list(CONVS))
    for n in SMALL_NAMES:
        if n not in CONVS:
            W[n] = A[n]
    flights = {}
    after = W['l0_w_in']
    for stage in ('l0_common', 'l1'):
        flights[stage], after = _gather_weights_start(A, stages[stage], after, stage)
    W['l0_mix_norm'] = W['l0_mix_norm'] + after[0, 0]

    reduce_state = {}

    def early_grads(stage, G):
        pairs, widths, _ = _reduce_begin(G, stages[stage], jnp.zeros((8, 8, LANES), F32), stage)
        copies = _to_chips_copies([p.shape for p in pairs], widths)
        lands = [lax.empty((4, p.shape[1], w), p.dtype) for p, w in zip(pairs, widths)]
        send_sems, recv_sems, pairs, lands, zeros = _split_copy_start(pairs, lands, copies, G['final_norm'],
                                                                      "reduce_start_" + stage)
        reduce_state[stage] = (send_sems, recv_sems, pairs, lands, copies, widths)
        return zeros[0, 0]

    loss_tile, grad_x, G = _local_step(
        x, mem, target, W, later_weights=lambda stage, after: _gather_weights_wait(flights[stage], after, stage),
        early_grads=early_grads)
    loss = lax.psum(loss_tile[0, 0], ("x", "y", "c"))

    g_mat = {}
    for stage in ('l1', 'l0_common'):
        send_sems, recv_sems, pairs, lands, copies, widths = reduce_state[stage]
        sent, recv = _split_copy_wait(send_sems, recv_sems, pairs, lands, copies, grad_x, "reduce_wait_" + stage)
        g_mat.update(_reduce_end(stages[stage], sent, recv, widths, stage))
    pairs, widths, g_small = _reduce_begin(G, stages['l0_mixer'],
                                           _small_slab(_pack_small({n: G[n] for n in SMALL_NAMES})), "l0_mixer")
    g_mat.update(_reduce_end(stages['l0_mixer'], pairs, _send_to_chips(pairs, widths), widths, "l0_mixer"))
    g_small = _unpack_small(_slab_sum(g_small, "sum_small"), {n: G[n].shape for n in SMALL_NAMES})
    me = 2 * lax.axis_index("x") + lax.axis_index("y")
    for n in CONVS:
        wd = A[n].shape[1]
        g_small[n] = lax.dynamic_slice_in_dim(g_small[n], me * wd, wd, axis=1)
    flat_names = [n for n in SMALL_NAMES if n not in CONVS]

    def pack_flat(prefix):
        return _pack_small_flat({n: A[prefix + n] for n in flat_names}, flat_names)

    shapes = {n: A[n].shape for n in flat_names}
    d_s, m_s, v_s, _ = _adamw(pack_flat(''), _pack_small_flat(g_small, flat_names), pack_flat('m_'), pack_flat('v_'),
                              "adamw_small")
    d_s, m_s, v_s = (_unpack_flat(p, shapes, flat_names) for p in (d_s, m_s, v_s))

    grads, deltas, new_m, new_v = {}, {}, {}, {}
    for n in WEIGHTS:
        if n in MATRICES or n in CONVS:
            g = g_mat[n] if n in MATRICES else g_small[n]
            deltas[n], new_m[n], new_v[n], grads[n] = _adamw(A[n], g, A['m_' + n], A['v_' + n], "adamw_" + n)
        else:
            grads[n] = g_small[n].reshape(A[n].shape)
            deltas[n], new_m[n], new_v[n] = d_s[n], m_s[n], v_s[n]
    return (loss, grad_x[None], *[grads[n] for n in WEIGHTS], *[deltas[n] for n in WEIGHTS],
            *[new_m[n] for n in WEIGHTS], *[new_v[n] for n in WEIGHTS])


def _pack_small_flat(vals, names):
    flat = jnp.concatenate([vals[n].astype(F32).reshape(-1) for n in names])
    rows = -(-flat.shape[0] // (8 * LANES)) * 8
    return jnp.pad(flat, (0, rows * LANES - flat.shape[0])).reshape(rows, LANES)


def _unpack_flat(packed, shapes, names):
    flat = packed.reshape(-1)
    out = {}
    off = 0
    for n in names:
        size = int(np.prod(shapes[n]))
        out[n] = flat[off:off + size].reshape(shapes[n])
        off += size
    return out
```

```python
import functools
import math

import numpy as np
import jax
import jax.numpy as jnp
from jax import lax
from jax.experimental import pallas as pl
from jax.experimental.pallas import tpu as pltpu

F32 = jnp.float32
BF16 = jnp.bfloat16
EPS = 1e-6
MESH = pl.DeviceIdType.MESH

ADAM_LR = 0.001
ADAM_B1 = 0.9
ADAM_B2 = 0.999
ADAM_EPS = 1e-08
ADAM_WD = 0.01
ADAM_STEP = 10

VMEM_LIMIT_BYTES = 56 * 1024 * 1024
MATMUL_VMEM_BYTES = 44 * 1024 * 1024
LANES = 1024

WEIGHTS = ['l0_mix_norm', 'l0_w_in', 'l0_ret_norm', 'l0_s5_lambda_re', 'l0_s5_lambda_im', 'l0_s5_b_re', 'l0_s5_b_im',
           'l0_s5_c_re', 'l0_s5_c_im', 'l0_s5_d', 'l0_s5_log_dt', 'l0_s5_w_glu', 'l0_s5_b_glu', 'l0_w_out',
           'l0_xa_norm', 'l0_mem_norm', 'l0_xa_wq', 'l0_xa_wkv', 'l0_xa_wo', 'l0_ffn_norm', 'l0_ffn_w_up',
           'l0_ffn_conv', 'l0_ffn_w_down', 'l1_mix_norm', 'l1_w_in', 'l1_conv', 'l1_a_log', 'l1_dt_bias',
           'l1_o_norm', 'l1_w_out', 'l1_xa_norm', 'l1_mem_norm', 'l1_xa_wq', 'l1_xa_wkv', 'l1_xa_wo',
           'l1_ffn_norm', 'l1_ffn_w_up', 'l1_ffn_conv', 'l1_ffn_w_down', 'final_norm']
ARG_NAMES = (['x', 'mem'] + WEIGHTS + ['loss_target'] + ['m_' + w for w in WEIGHTS] + ['v_' + w for w in WEIGHTS])

MATRICES = {
    'l0_w_in': 1, 'l0_s5_w_glu': 0, 'l0_w_out': 0, 'l0_xa_wq': 0, 'l0_xa_wkv': 1, 'l0_xa_wo': 0, 'l0_ffn_w_up': 1,
    'l0_ffn_w_down': 0, 'l1_w_in': 1, 'l1_w_out': 0, 'l1_xa_wq': 0, 'l1_xa_wkv': 1, 'l1_xa_wo': 0,
    'l1_ffn_w_up': 1, 'l1_ffn_w_down': 0,
}
CONVS = ('l0_ffn_conv', 'l1_conv', 'l1_ffn_conv')
MATRIX_NAMES = [w for w in WEIGHTS if w in MATRICES]
SMALL_NAMES = [w for w in WEIGHTS if w not in MATRICES]


def _cparams(sem=None):
    return pltpu.CompilerParams(dimension_semantics=sem, vmem_limit_bytes=VMEM_LIMIT_BYTES)


def _pick(n, cands):
    for c in cands:
        if n % c == 0:
            return c
    return n


_NN = ((1,), (0,))
_NT = ((1,), (1,))
_TN = ((0,), (0,))


def _dot(a, b, dims, hi):
    if hi is not None:
        return lax.dot_general(a.astype(F32), b.astype(F32), (dims, ((), ())), precision=hi,
                               preferred_element_type=F32)
    return lax.dot_general(a.astype(BF16), b.astype(BF16), (dims, ((), ())), preferred_element_type=F32)


def _make_mm(hi):
    @jax.custom_vjp
    def nn(a, b):
        return _dot(a, b, _NN, hi)

    def nn_f(a, b):
        return nn(a, b), (a, b)

    def nn_b(r, g):
        a, b = r
        return _dot(g, b, _NT, hi), _dot(a, g, _TN, hi)

    nn.defvjp(nn_f, nn_b)

    @jax.custom_vjp
    def nt(a, b):
        return _dot(a, b, _NT, hi)

    def nt_f(a, b):
        return nt(a, b), (a, b)

    def nt_b(r, g):
        a, b = r
        return _dot(g, b, _NN, hi), _dot(g, a, _TN, hi)

    nt.defvjp(nt_f, nt_b)

    @jax.custom_vjp
    def tn(a, b):
        return _dot(a, b, _TN, hi)

    def tn_f(a, b):
        return tn(a, b), (a, b)

    def tn_b(r, g):
        a, b = r
        return _dot(b, g, _NT, hi), _dot(a, g, _NN, hi)

    tn.defvjp(tn_f, tn_b)
    return nn, nt, tn


mm, mm_nt, mm_tn = _make_mm(None)
mmh, mmh_nt, mmh_tn = _make_mm(lax.Precision.HIGHEST)
mm3, _, _ = _make_mm(lax.Precision.HIGH)


@jax.custom_vjp
def _swap_halves(x):
    return pltpu.roll(x, 64, 1)


def _swap_f(x):
    return pltpu.roll(x, 64, 1), None


def _swap_b(_, g):
    return (pltpu.roll(g, 64, 1),)


_swap_halves.defvjp(_swap_f, _swap_b)


def _silu(x):
    return x * jax.nn.sigmoid(x)


def _rms(x, g):
    return x * lax.rsqrt(jnp.mean(x * x, axis=-1, keepdims=True) + EPS) * g


def _iota(shape, dim):
    return lax.broadcasted_iota(jnp.int32, shape, dim)


def _matmul_tiles(M, N, K, a_bytes, b_bytes, has_res, a_off):
    def divisors(n, cands):
        return [c for c in cands if n % c == 0] or [n]

    fallback = None
    for tk in divisors(K, (K, 2048, 1408, 1024, 512, 256, 128)):
        for tm in divisors(M, (1024, 512, 1408, 256, 128)):
            for tn in divisors(N, (1408, 1024, 512, 256, 128)):
                need = 2 * (tm * tk * a_bytes + tk * tn * b_bytes + (tm * tn * 4 if has_res else 0)) + 3 * tm * tn * 4
                if need > MATMUL_VMEM_BYTES or a_off % tk or a_off % tm:
                    continue
                if tm >= 256 and tn >= 256:
                    return tm, tn, tk
                fallback = fallback or (tm, tn, tk)
    return fallback


def _matmul(a, b, mode="nn", res=None, name="mm", a_cols=None, out_dtype=F32):
    a_off, a_w = (0, a.shape[1]) if a_cols is None else a_cols
    if mode == "nn":
        (M, K), (K2, N) = (a.shape[0], a_w), b.shape
    elif mode == "nt":
        (M, K), (N, K2) = (a.shape[0], a_w), b.shape
    else:
        (K, M), (K2, N) = (a.shape[0], a_w), b.shape
    assert K == K2, (a.shape, b.shape, mode)
    tm, tn, tk = _matmul_tiles(M, N, K, 2 if a.dtype == BF16 else 4, 2 if b.dtype == BF16 else 4, res is not None,
                               a_off)
    nk = K // tk
    dims = {"nn": _NN, "nt": _NT, "tn": _TN}[mode]
    ao = a_off // (tm if mode == "tn" else tk)
    assert ao * (tm if mode == "tn" else tk) == a_off
    if mode == "nn":
        a_spec = pl.BlockSpec((tm, tk), lambda i, j, k: (i, k + ao))
        b_spec = pl.BlockSpec((tk, tn), lambda i, j, k: (k, j))
    elif mode == "nt":
        a_spec = pl.BlockSpec((tm, tk), lambda i, j, k: (i, k + ao))
        b_spec = pl.BlockSpec((tn, tk), lambda i, j, k: (j, k))
    else:
        a_spec = pl.BlockSpec((tk, tm), lambda i, j, k: (k, i + ao))
        b_spec = pl.BlockSpec((tk, tn), lambda i, j, k: (k, j))
    o_spec = pl.BlockSpec((tm, tn), lambda i, j, k: (i, j))
    has_res = res is not None

    def kern(*refs):
        a_ref, b_ref = refs[:2]
        r_ref = refs[2] if has_res else None
        o_ref = refs[3] if has_res else refs[2]
        acc_ref = refs[-1] if nk > 1 else None
        k = pl.program_id(2)
        part = lax.dot_general(a_ref[...].astype(BF16), b_ref[...].astype(BF16), (dims, ((), ())),
                               preferred_element_type=F32)
        if nk == 1:
            o_ref[...] = (part + r_ref[...] if has_res else part).astype(o_ref.dtype)
            return

        @pl.when(k == 0)
        def _():
            acc_ref[...] = part

        @pl.when((k > 0) & (k < nk - 1))
        def _():
            acc_ref[...] += part

        @pl.when(k == nk - 1)
        def _():
            total = acc_ref[...] + part
            o_ref[...] = (total + r_ref[...] if has_res else total).astype(o_ref.dtype)

    in_specs = [a_spec, b_spec] + ([o_spec] if has_res else [])
    ops = (a, b) + ((res,) if has_res else ())
    return pl.pallas_call(
        kern, name=name, grid=(M // tm, N // tn, nk), in_specs=in_specs, out_specs=o_spec,
        out_shape=jax.ShapeDtypeStruct((M, N), out_dtype),
        scratch_shapes=[pltpu.VMEM((tm, tn), F32)] if nk > 1 else [],
        compiler_params=_cparams(("parallel", "parallel", "arbitrary")))(*ops)


def _matmul_cat(pieces, b, mode="nn", res=None, name="mmcat"):
    M = pieces[0].shape[0]
    widths = [p.shape[1] for p in pieces]
    K = sum(widths)
    N = b.shape[1] if mode == "nn" else b.shape[0]
    assert (b.shape[0] if mode == "nn" else b.shape[1]) == K
    tn = _pick(N, (1024, 512, 256, 128))
    a_bytes = 2 if pieces[0].dtype == BF16 else 4
    for tm in (1024, 512, 256, 128):
        need = 2 * (tm * K * a_bytes + K * tn * 2 + (tm * tn * 4 if res is not None else 0)) + 3 * tm * tn * 4
        if M % tm == 0 and need <= MATMUL_VMEM_BYTES:
            break
    npc = len(pieces)
    has_res = res is not None
    dims = _NN if mode == "nn" else _NT

    def kern(*refs):
        b_ref = refs[npc]
        o_ref = refs[-1]
        acc = refs[npc + 1][...] if has_res else None
        off = 0
        for p in range(npc):
            bp = b_ref[off:off + widths[p], :] if mode == "nn" else b_ref[:, off:off + widths[p]]
            t = lax.dot_general(refs[p][...].astype(BF16), bp.astype(BF16), (dims, ((), ())),
                                preferred_element_type=F32)
            acc = t if acc is None else acc + t
            off += widths[p]
        o_ref[...] = acc

    in_specs = [pl.BlockSpec((tm, w), lambda j, i: (i, 0)) for w in widths]
    in_specs.append(pl.BlockSpec((K, tn), lambda j, i: (0, j)) if mode == "nn"
                    else pl.BlockSpec((tn, K), lambda j, i: (j, 0)))
    o_spec = pl.BlockSpec((tm, tn), lambda j, i: (i, j))
    if has_res:
        in_specs.append(o_spec)
    ops = list(pieces) + [b] + ([res] if has_res else [])
    return pl.pallas_call(
        kern, name=name, grid=(N // tn, M // tm), in_specs=in_specs, out_specs=o_spec,
        out_shape=jax.ShapeDtypeStruct((M, N), F32), compiler_params=_cparams(("parallel", "parallel")))(*ops)


def _blk(a, width=None, colblk=0):
    return (a, a.shape[1] if width is None else width, colblk)


def _row_specs(blocked, params, ts):
    specs = []
    for (_, w, cb) in blocked:
        specs.append(pl.BlockSpec((ts, w), functools.partial(lambda i, cb: (i, cb), cb=cb)))
    for p in params:
        specs.append(pl.BlockSpec(p.shape, lambda i: (0, 0)))
    return specs


def _rowwise(fn, blocked, params, out_widths, name, ts=256, out_dtypes=None):
    S = blocked[0][0].shape[0]
    ts = min(ts, S)
    nb, npar = len(blocked), len(params)
    out_dtypes = [F32] * len(out_widths) if out_dtypes is None else out_dtypes

    def kern(*refs):
        vals = [r[...] for r in refs[:nb + npar]]
        outs = fn(*vals)
        for o_ref, o in zip(refs[nb + npar:], outs):
            o_ref[...] = o.astype(o_ref.dtype)

    return pl.pallas_call(
        kern, name=name, grid=(S // ts,), in_specs=_row_specs(blocked, params, ts),
        out_specs=[pl.BlockSpec((ts, w), lambda i: (i, 0)) for w in out_widths],
        out_shape=[jax.ShapeDtypeStruct((S, w), d) for w, d in zip(out_widths, out_dtypes)],
        compiler_params=_cparams(("parallel",)))(*[b[0] for b in blocked], *params)


def _rowwise_bwd(fn, blocked, params, cots, name, blocked_grad=None, param_grad=None, adds=None, ts=256,
                 out_dtypes=None):
    S = blocked[0][0].shape[0]
    ts = min(ts, S)
    cots = [c if isinstance(c, tuple) else _blk(c) for c in cots]
    nb, npar, nc = len(blocked), len(params), len(cots)
    blocked_grad = [True] * nb if blocked_grad is None else blocked_grad
    param_grad = [True] * npar if param_grad is None else param_grad
    adds = {} if adds is None else adds
    bidx = [i for i in range(nb) if blocked_grad[i]]
    pidx = [i for i in range(npar) if param_grad[i]]
    add_keys = sorted(adds)
    n_in = nb + npar + nc + len(add_keys)

    def kern(*refs):
        i = pl.program_id(0)
        xs = [r[...] for r in refs[:nb]]
        ps = [r[...] for r in refs[nb:nb + npar]]
        gs = [r[...] for r in refs[nb + npar:nb + npar + nc]]
        add_vals = {k: refs[nb + npar + nc + n][...] for n, k in enumerate(add_keys)}
        outs = refs[n_in:]

        def f(*diff):
            full_x = list(xs)
            full_p = list(ps)
            for n, ix in enumerate(bidx):
                full_x[ix] = diff[n]
            for n, ix in enumerate(pidx):
                full_p[ix] = diff[len(bidx) + n]
            return tuple(fn(*full_x, *full_p))

        _, vjp = jax.vjp(f, *[xs[ix] for ix in bidx], *[ps[ix] for ix in pidx])
        grads = vjp(tuple(gs))
        for n, ix in enumerate(bidx):
            g = grads[n]
            if ix in add_vals:
                g = g + add_vals[ix]
            outs[n][...] = g.astype(outs[n].dtype)
        for n in range(len(pidx)):
            o_ref = outs[len(bidx) + n]

            @pl.when(i == 0)
            def _(o_ref=o_ref):
                o_ref[...] = jnp.zeros_like(o_ref)

            o_ref[...] += grads[len(bidx) + n]

    in_specs = _row_specs(blocked, params, ts)
    in_specs += _row_specs(cots, [], ts)
    in_specs += [pl.BlockSpec((ts, adds[k].shape[1]), lambda i: (i, 0)) for k in add_keys]
    out_specs = [pl.BlockSpec((ts, blocked[ix][1]), lambda i: (i, 0)) for ix in bidx]
    out_specs += [pl.BlockSpec(params[ix].shape, lambda i: (0, 0)) for ix in pidx]
    out_dtypes = [F32] * len(bidx) if out_dtypes is None else out_dtypes
    out_shape = [jax.ShapeDtypeStruct((S, blocked[ix][1]), d) for ix, d in zip(bidx, out_dtypes)]
    out_shape += [jax.ShapeDtypeStruct(params[ix].shape, F32) for ix in pidx]
    return pl.pallas_call(
        kern, name=name, grid=(S // ts,), in_specs=in_specs, out_specs=out_specs, out_shape=out_shape,
        compiler_params=_cparams(("arbitrary",)))(*[b[0] for b in blocked], *params, *[c[0] for c in cots],
                                                    *[adds[k] for k in add_keys])


def _rms_fn(x, g):
    return (_rms(x, g),)


def _head_norm(o, n_heads, dh):
    outs = []
    for h in range(n_heads):
        oh = o[:, h * dh:(h + 1) * dh]
        outs.append(oh * lax.rsqrt(jnp.mean(oh * oh, axis=-1, keepdims=True) + EPS))
    return outs


def _ret_post_fn(o_raw, gate, ret_norm):
    o = jnp.concatenate(_head_norm(o_raw, 4, 128), axis=1)
    return (o * ret_norm * _silu(gate),)


def _s5_post_fn(y1, y2, u, d, w_glu, b_glu):
    y = y1 - y2 + d * u
    y = jax.nn.gelu(y)
    return (y * jax.nn.sigmoid(mm(y, w_glu) + b_glu),)


def _xattn_fn(q, kv):
    outs = []
    for h in range(4):
        qh = q[:, h * 256:(h + 1) * 256]
        kh = kv[:, h * 256:(h + 1) * 256]
        vh = kv[:, 1024 + h * 256:1024 + (h + 1) * 256]
        s = mm_nt(qh, kh) * (256 ** -0.5)
        s = s - lax.stop_gradient(jnp.max(s, axis=-1, keepdims=True))
        p = jnp.exp(s)
        p = p / jnp.sum(p, axis=-1, keepdims=True)
        outs.append(mm(p, vh))
    return (jnp.concatenate(outs, axis=1),)


def _softplus(x):
    return jnp.maximum(x, 0.0) + jnp.log1p(jnp.exp(-jnp.abs(x)))


def _gdn_gates_fn(pt, a_log_p, dtb_p):
    rows, cols = _iota((128, 1024), 0), _iota((128, 1024), 1)
    e_b = (rows == (cols >> 7)).astype(F32)
    e_a = (rows == (cols >> 7) + 8).astype(F32)
    beta = jax.nn.sigmoid(pt)
    g = -(jnp.exp(a_log_p) * _softplus(pt + dtb_p))
    return mmh(g, e_a), mmh(beta, e_b)


def _gdn_post_fn(o_raw, z, o_norm):
    outs = _head_norm(o_raw, 8, 128)
    o = jnp.concatenate([oh * o_norm for oh in outs], axis=1)
    return (o * _silu(z),)


def _ffn_post(up, gate):
    return _silu(gate) * up


def _shift_down(cur, prev8, sh, row8):
    if sh == 0:
        return cur
    r = pltpu.roll(cur, sh, 0)
    p = pltpu.roll(prev8, sh, 0)
    top = jnp.where(row8 < sh, p, r[0:8])
    if cur.shape[0] == 8:
        return top
    return jnp.concatenate([top, r[8:]], axis=0)


def _shift_up(cur, next8, sh, row8):
    if sh == 0:
        return cur
    ts = cur.shape[0]
    r = pltpu.roll(cur, ts - sh, 0)
    p = pltpu.roll(next8, 8 - sh, 0)
    bot = jnp.where(row8 >= 8 - sh, p, r[ts - 8:])
    return jnp.concatenate([r[:ts - 8], bot], axis=0)


def _conv_rows(cur, prev8, wrows, row8):
    k_w = len(wrows)
    out = None
    for j in range(k_w):
        t = _shift_down(cur, prev8, k_w - 1 - j, row8) * wrows[j]
        out = t if out is None else out + t
    return out


def _conv_specs(x, xoff, w, woff, ts, tc):
    r8 = ts // 8
    return [pl.BlockSpec((ts, tc), functools.partial(lambda i, j, o: (i, j + o), o=xoff)),
            pl.BlockSpec((8, tc), functools.partial(lambda i, j, o: (jnp.maximum(i * r8 - 1, 0), j + o), o=xoff)),
            pl.BlockSpec((w.shape[0], tc), functools.partial(lambda i, j, o: (0, j + o), o=woff))]


def _conv_post(srcs, post, ncol, tc, name, cots=None, ts=256, out_dtype=F32):
    S = srcs[0][0].shape[0]
    ns = len(srcs)
    bwd = cots is not None

    def kern(*refs):
        first = pl.program_id(0) == 0
        row8 = _iota((8, tc), 0)
        cs = []
        for s in range(ns):
            cur_ref, prev_ref, w_ref = refs[3 * s:3 * s + 3]
            prev = jnp.where(first, 0.0, prev_ref[...])
            wrows = [w_ref[j:j + 1, :] for j in range(w_ref.shape[0])]
            cs.append(_conv_rows(cur_ref[...], prev, wrows, row8))
        if bwd:
            g = refs[3 * ns][...]
            _, vjp = jax.vjp(lambda *c: post(*c), *cs)
            for o_ref, d in zip(refs[3 * ns + 1:], vjp(g)):
                o_ref[...] = d
        else:
            refs[3 * ns][...] = post(*cs).astype(refs[3 * ns].dtype)

    in_specs = []
    ops = []
    for (x, xoff, w, woff) in srcs:
        in_specs += _conv_specs(x, xoff, w, woff, ts, tc)
        ops += [x, x, w]
    o_spec = pl.BlockSpec((ts, tc), lambda i, j: (i, j))
    o_shape = jax.ShapeDtypeStruct((S, ncol * tc), F32)
    if bwd:
        in_specs.append(o_spec)
        ops.append(cots)
        out_specs, out_shape = [o_spec] * ns, [o_shape] * ns
    else:
        out_specs, out_shape = o_spec, jax.ShapeDtypeStruct((S, ncol * tc), out_dtype)
    return pl.pallas_call(
        kern, name=name, grid=(S // ts, ncol), in_specs=in_specs, out_specs=out_specs, out_shape=out_shape,
        compiler_params=_cparams(("parallel", "parallel")))(*ops)


def _conv_bwd(dc, x, xoff, w, woff, ncol, tc, name, ts=256):
    S = x.shape[0]
    k_w = w.shape[0]
    r8 = ts // 8
    nblk8 = S // 8
    nrow = S // ts

    def kern(dc_ref, dn_ref, x_ref, xp_ref, w_ref, dx_ref, dw_ref):
        i = pl.program_id(1)
        row8 = _iota((8, tc), 0)
        dcur = dc_ref[...]
        dnext = jnp.where(i == nrow - 1, 0.0, dn_ref[...])
        xcur = x_ref[...]
        xprev = jnp.where(i == 0, 0.0, xp_ref[...])

        @pl.when(i == 0)
        def _():
            dw_ref[...] = jnp.zeros_like(dw_ref)

        dx = None
        for j in range(k_w):
            sh = k_w - 1 - j
            wj = w_ref[j:j + 1, :]
            t = _shift_up(dcur, dnext, sh, row8) * wj
            dx = t if dx is None else dx + t
            dw_ref[j:j + 1, :] += jnp.sum(dcur * _shift_down(xcur, xprev, sh, row8), axis=0, keepdims=True)
        dx_ref[...] = dx.astype(dx_ref.dtype)

    in_specs = [pl.BlockSpec((ts, tc), lambda j, i: (i, j)),
                pl.BlockSpec((8, tc), lambda j, i: (jnp.minimum((i + 1) * r8, nblk8 - 1), j)),
                pl.BlockSpec((ts, tc), functools.partial(lambda j, i, o: (i, j + o), o=xoff)),
                pl.BlockSpec((8, tc), functools.partial(lambda j, i, o: (jnp.maximum(i * r8 - 1, 0), j + o), o=xoff)),
                pl.BlockSpec((k_w, tc), functools.partial(lambda j, i, o: (0, j + o), o=woff))]
    out_specs = [pl.BlockSpec((ts, tc), lambda j, i: (i, j)), pl.BlockSpec((k_w, tc), lambda j, i: (0, j))]
    out_shape = [jax.ShapeDtypeStruct((S, ncol * tc), BF16), jax.ShapeDtypeStruct((k_w, ncol * tc), F32)]
    return pl.pallas_call(
        kern, name=name, grid=(ncol, nrow), in_specs=in_specs, out_specs=out_specs, out_shape=out_shape,
        compiler_params=_cparams(("parallel", "arbitrary")))(dc, dc, x, x, w)


def _conv_post_bwd(srcs, post, ncol, tc, cot, name, ts=256):
    S = srcs[0][0].shape[0]
    ns = len(srcs)
    r8 = ts // 8
    nblk8 = S // 8
    nrow = S // ts

    def kern(*refs):
        i = pl.program_id(1)
        row8 = _iota((8, tc), 0)
        g_ref, gn_ref = refs[4 * ns:4 * ns + 2]
        outs = refs[4 * ns + 2:]
        xs, xps, ws, cs, cns = [], [], [], [], []
        for s in range(ns):
            cur_ref, prev_ref, next_ref, w_ref = refs[4 * s:4 * s + 4]
            xcur = cur_ref[...]
            xprev = jnp.where(i == 0, 0.0, prev_ref[...])
            wrows = [w_ref[j:j + 1, :] for j in range(w_ref.shape[0])]
            xs.append(xcur)
            xps.append(xprev)
            ws.append(wrows)
            cs.append(_conv_rows(xcur, xprev, wrows, row8))
            cns.append(_conv_rows(next_ref[...], xcur[ts - 8:], wrows, row8))
        _, vjp = jax.vjp(lambda *c: post(*c), *cs)
        dcs = vjp(g_ref[...])
        _, vjp_next = jax.vjp(lambda *c: post(*c), *cns)
        dcns = vjp_next(jnp.where(i == nrow - 1, 0.0, gn_ref[...]))
        for s in range(ns):
            dx_ref, dw_ref = outs[2 * s], outs[2 * s + 1]

            @pl.when(i == 0)
            def _(dw_ref=dw_ref):
                dw_ref[...] = jnp.zeros_like(dw_ref)

            k_w = len(ws[s])
            dx = None
            for j in range(k_w):
                sh = k_w - 1 - j
                t = _shift_up(dcs[s], dcns[s], sh, row8) * ws[s][j]
                dx = t if dx is None else dx + t
                dw_ref[j:j + 1, :] += jnp.sum(dcs[s] * _shift_down(xs[s], xps[s], sh, row8), axis=0, keepdims=True)
            dx_ref[...] = dx.astype(dx_ref.dtype)

    def nxt(i):
        return jnp.minimum((i + 1) * r8, nblk8 - 1)

    in_specs, ops = [], []
    for (x, xoff, w, woff) in srcs:
        in_specs += [pl.BlockSpec((ts, tc), functools.partial(lambda j, i, o: (i, j + o), o=xoff)),
                     pl.BlockSpec((8, tc), functools.partial(lambda j, i, o: (jnp.maximum(i * r8 - 1, 0), j + o),
                                                             o=xoff)),
                     pl.BlockSpec((8, tc), functools.partial(lambda j, i, o: (nxt(i), j + o), o=xoff)),
                     pl.BlockSpec((w.shape[0], tc), functools.partial(lambda j, i, o: (0, j + o), o=woff))]
        ops += [x, x, x, w]
    in_specs += [pl.BlockSpec((ts, tc), lambda j, i: (i, j)), pl.BlockSpec((8, tc), lambda j, i: (nxt(i), j))]
    ops += [cot, cot]
    out_specs, out_shape = [], []
    for (x, xoff, w, woff) in srcs:
        out_specs += [pl.BlockSpec((ts, tc), lambda j, i: (i, j)), pl.BlockSpec((w.shape[0], tc), lambda j, i: (0, j))]
        out_shape += [jax.ShapeDtypeStruct((S, ncol * tc), BF16), jax.ShapeDtypeStruct((w.shape[0], ncol * tc), F32)]
    return pl.pallas_call(
        kern, name=name, grid=(ncol, nrow), in_specs=in_specs, out_specs=out_specs, out_shape=out_shape,
        compiler_params=_cparams(("parallel", "arbitrary")))(*ops)


def _ret_tables(S):
    H, C, dh = 4, 128, 128
    lg = jnp.log1p(-jnp.exp2(-5.0 - jnp.arange(H, dtype=F32)))
    idx = jnp.arange(C, dtype=F32)
    diff = idx[:, None] - idx[None, :]
    causal = diff >= 0
    intra = jnp.where(causal, jnp.exp(lg[:, None, None] * jnp.where(causal, diff, 0.0)), 0.0)
    kdec = jnp.broadcast_to(jnp.exp(lg[:, None] * (C - 1 - idx))[:, :, None], (H, C, dh))
    qdec = jnp.broadcast_to(jnp.exp(lg[:, None] * (idx + 1))[:, :, None], (H, C, dh))
    cdec = jnp.broadcast_to(jnp.exp(lg * C)[:, None, None], (H, dh, dh))
    half = dh // 2
    inv = jnp.exp(-math.log(10000.0) * jnp.arange(half, dtype=F32) / half)
    ang = jnp.arange(S).astype(F32)[:, None] * inv[None, :]
    cos, sin = jnp.cos(ang), jnp.sin(ang)
    cosf = jnp.concatenate([cos, cos], axis=1)
    sinf = jnp.concatenate([-sin, sin], axis=1)
    return cosf, sinf, intra, kdec, qdec, cdec


def _ret_chunk(q, k, v, cosf, sinf, intra, kdec, qdec, cdec, state):
    hs = range(len(q))
    qr = [q[h] * cosf + _swap_halves(q[h]) * sinf for h in hs]
    kr = [(k[h] * cosf + _swap_halves(k[h]) * sinf) * (128 ** -0.5) for h in hs]
    scores = [mm_nt(qr[h], kr[h]) * intra[h] for h in hs]
    inner = [mm(scores[h], v[h]) for h in hs]
    kv = [mm_tn(kr[h] * kdec[h], v[h]) for h in hs]
    cross = [mm(qr[h] * qdec[h], state[h]) for h in hs]
    return [inner[h] + cross[h] for h in hs], [state[h] * cdec[h] + kv[h] for h in hs]


RET_H = 4


def _ret_call(proj, tabs, states=None, do=None):
    S = proj.shape[0]
    N = S // 128
    bwd = do is not None

    def nn(n):
        return N - 1 - n if bwd else n

    qkv_spec = pl.BlockSpec((128, 3 * 512), lambda n: (nn(n), 0))
    pos = pl.BlockSpec((128, 128), lambda n: (nn(n), 0))
    tab = pl.BlockSpec((RET_H, 128, 128), lambda n: (0, 0, 0))
    st_spec = pl.BlockSpec((None, RET_H, 128, 128), lambda n: (nn(n), 0, 0, 0))
    o_spec = pl.BlockSpec((128, 512), lambda n: (nn(n), 0))

    def kern(*refs):
        x_ref, c_ref, s_ref, i_ref, kd_ref, qd_ref, cd_ref = refs[:7]
        carry = refs[-1]
        heads = range(RET_H)

        @pl.when(pl.program_id(0) == 0)
        def _():
            carry[...] = jnp.zeros_like(carry)

        def cols(ref, off=0):
            return [ref[:, _hs(off + h)] for h in heads]

        def tabs_of(ref):
            return [ref[h] for h in heads]

        consts = (c_ref[...], s_ref[...], tabs_of(i_ref), tabs_of(kd_ref), tabs_of(qd_ref), tabs_of(cd_ref))
        qkv = (cols(x_ref), cols(x_ref, RET_H), cols(x_ref, 2 * RET_H))
        if bwd:
            sp_ref, do_ref = refs[7:9]
            outs = refs[9:12]
            _, vjp = jax.vjp(lambda q, k, v, s: _ret_chunk(q, k, v, *consts, s), *qkv, tabs_of(sp_ref))
            dq, dk, dv, ds = vjp((cols(do_ref), tabs_of(carry)))
            for h in heads:
                for o_ref, d in zip(outs, (dq[h], dk[h], dv[h])):
                    o_ref[:, _hs(h)] = d.astype(o_ref.dtype)
                carry[h] = ds[h]
        else:
            o_ref, sp_ref = refs[7:9]
            state = tabs_of(carry)
            out, new = _ret_chunk(*qkv, *consts, state)
            for h in heads:
                sp_ref[h] = state[h]
                o_ref[:, _hs(h)] = out[h]
                carry[h] = new[h]

    in_specs = [qkv_spec, pos, pos, tab, tab, tab, tab]
    if bwd:
        in_specs += [st_spec, o_spec]
        out_specs = [o_spec] * 3
        out_shape = [jax.ShapeDtypeStruct((S, 512), BF16)] * 3
        ops = (proj, *tabs, states, do)
    else:
        out_specs = [o_spec, st_spec]
        out_shape = [jax.ShapeDtypeStruct((S, 512), F32), jax.ShapeDtypeStruct((N, RET_H, 128, 128), F32)]
        ops = (proj, *tabs)
    return pl.pallas_call(
        kern, name="ret_bwd" if bwd else "ret_fwd", grid=(N,), in_specs=in_specs, out_specs=out_specs,
        out_shape=out_shape, scratch_shapes=[pltpu.VMEM((RET_H, 128, 128), F32)],
        compiler_params=_cparams(("arbitrary",)))(*ops)


GDN_C = 64
GDN_H = 8


def _unit_lower_inverse(a_mats, eye):
    p = [-a for a in a_mats]
    t = [eye + x for x in p]
    for _ in range(5):
        p = [mm3(x, x) for x in p]
        t = [mm3(y, eye + x) for y, x in zip(t, p)]
    return t


@jax.custom_vjp
def _known_inverse(a_mat, t_mat):
    return t_mat


def _known_inverse_f(a_mat, t_mat):
    return t_mat, t_mat


def _known_inverse_b(t_mat, g):
    return -mmh_tn(t_mat, mmh_nt(g, t_mat)), jnp.zeros_like(t_mat)


_known_inverse.defvjp(_known_inverse_f, _known_inverse_b)


def _gdn_intra(q, k, v, g_b, beta_b, t_known=None):
    c = GDN_C
    hs = range(len(q))
    q = [x * lax.rsqrt(jnp.sum(x * x, axis=-1, keepdims=True) + EPS) * (128 ** -0.5) for x in q]
    k = [x * lax.rsqrt(jnp.sum(x * x, axis=-1, keepdims=True) + EPS) for x in k]
    ri, ci = _iota((c, c), 0), _iota((c, c), 1)
    incl = ri >= ci
    strict = ri > ci
    eye = (ri == ci).astype(F32)
    lower = incl.astype(F32)
    gc_b = [mm3(lower, g) for g in g_b]
    gl_b = [jnp.sum(g, axis=0, keepdims=True) for g in g_b]
    kb = [k[h] * beta_b[h] for h in hs]
    vb = [v[h] * beta_b[h] for h in hs]
    gcc = [g[:, :c] for g in gc_b]
    decay = [jnp.where(incl, jnp.exp(jnp.where(incl, g - g.T, 0.0)), 0.0) for g in gcc]
    a_mat = [jnp.where(strict, mm_nt(kb[h], k[h]) * decay[h], 0.0) for h in hs]
    if t_known is None:
        t_mat = _unit_lower_inverse(a_mat, eye)
    else:
        t_mat = [_known_inverse(a_mat[h], t_known[h]) for h in hs]
    egc = [jnp.exp(g) for g in gc_b]
    w = [mm(t_mat[h], kb[h] * egc[h]) for h in hs]
    u = [mm(t_mat[h], vb[h]) for h in hs]
    qk = [jnp.where(incl, mm_nt(q[h], k[h]) * decay[h], 0.0) for h in hs]
    q_dec = [q[h] * egc[h] for h in hs]
    k_dec = [k[h] * jnp.exp(gl_b[h] - gc_b[h]) for h in hs]
    return w, u, q_dec, k_dec, qk, t_mat


def _gdn_step(w, u, q_dec, k_dec, qk, g_b, state):
    hs = range(len(w))
    gl_s = [jnp.sum(g, axis=0, keepdims=True) for g in g_b]
    ws = [mm(w[h], state[h]) for h in hs]
    qs = [mm(q_dec[h], state[h]) for h in hs]
    v_new = [u[h] - ws[h] for h in hs]
    o = [qs[h] + mm(qk[h], v_new[h]) for h in hs]
    new = [state[h] * jnp.exp(gl_s[h]) + mm_tn(k_dec[h], v_new[h]) for h in hs]
    return o, new


def _hs(h):
    return slice(h * 128, (h + 1) * 128)


def _gdn_intra_call(qkv, g_e, beta_e, cots=None):
    S = qkv.shape[0]
    N = S // GDN_C
    bwd = cots is not None
    row = pl.BlockSpec((GDN_C, 1024), lambda n: (n, 0))
    qkv_spec = pl.BlockSpec((GDN_C, 3072), lambda n: (n, 0))
    qk_spec = pl.BlockSpec((GDN_H, GDN_C, GDN_C), lambda n: (0, n, 0))

    def kern(*refs):
        x_ref, g_ref, b_ref = refs[:3]
        heads = range(GDN_H)

        def cols(ref, off=0):
            return [ref[:, _hs(off + h)] for h in heads]

        args = (cols(x_ref), cols(x_ref, 8), cols(x_ref, 16), cols(g_ref), cols(b_ref))
        if bwd:
            dw_ref, du_ref, dqd_ref, dkd_ref, dqk_ref, dgadd_ref, t_ref = refs[3:10]
            outs = refs[10:]
            t_known = [t_ref[h] for h in heads]
            _, vjp = jax.vjp(lambda *a: _gdn_intra(*a, t_known=t_known)[:5], *args)
            dq, dk, dv, dg, db = vjp((cols(dw_ref), cols(du_ref), cols(dqd_ref), cols(dkd_ref),
                                      [dqk_ref[h] for h in heads]))
            dgadd = cols(dgadd_ref)
            for h in heads:
                for o_ref, d in zip(outs, (dq[h], dk[h], dv[h], dg[h] + dgadd[h], db[h])):
                    o_ref[:, _hs(h)] = d
        else:
            w, u, qd, kd, qk, t_mat = _gdn_intra(*args)
            for h in heads:
                for o_ref, o in zip(refs[3:7], (w[h], u[h], qd[h], kd[h])):
                    o_ref[:, _hs(h)] = o
                refs[7][h] = qk[h]
                refs[8][h] = t_mat[h]

    big = jax.ShapeDtypeStruct((S, 1024), F32)
    sq = jax.ShapeDtypeStruct((GDN_H, S, GDN_C), F32)
    if bwd:
        in_specs = [qkv_spec, row, row, row, row, row, row, qk_spec, row, qk_spec]
        out_specs, out_shape = [row] * 5, [big] * 5
        ops = (qkv, g_e, beta_e) + tuple(cots)
    else:
        in_specs = [qkv_spec, row, row]
        out_specs = [row] * 4 + [qk_spec, qk_spec]
        out_shape = [big] * 4 + [sq, sq]
        ops = (qkv, g_e, beta_e)
    return pl.pallas_call(
        kern, name="gdn_intra_bwd" if bwd else "gdn_intra", grid=(N,), in_specs=in_specs, out_specs=out_specs,
        out_shape=out_shape, compiler_params=_cparams(("parallel",)))(*ops)


def _gdn_pass(w, u, qd, kd, qk, g_e, states=None, do=None):
    S = w.shape[0]
    N = S // GDN_C
    bwd = do is not None

    def nn(n):
        return N - 1 - n if bwd else n

    row = pl.BlockSpec((GDN_C, 1024), lambda n: (nn(n), 0))
    qk_spec = pl.BlockSpec((GDN_H, GDN_C, GDN_C), lambda n: (0, nn(n), 0))
    st_spec = pl.BlockSpec((None, GDN_H, 128, 128), lambda n: (nn(n), 0, 0, 0))

    def kern(*refs):
        w_ref, u_ref, qd_ref, kd_ref, qk_ref, g_ref = refs[:6]
        carry = refs[-1]

        @pl.when(pl.program_id(0) == 0)
        def _():
            carry[...] = jnp.zeros_like(carry)

        heads = range(GDN_H)

        def cols(ref):
            return [ref[:, _hs(h)] for h in heads]

        args = (cols(w_ref), cols(u_ref), cols(qd_ref), cols(kd_ref), [qk_ref[h] for h in heads], cols(g_ref))
        if bwd:
            sp_ref, do_ref = refs[6:8]
            outs = refs[8:14]
            _, vjp = jax.vjp(_gdn_step, *args, [sp_ref[h] for h in heads])
            dw, du, dqd, dkd, dqk, dg, ds = vjp((cols(do_ref), [carry[h] for h in heads]))
            for h in heads:
                for o_ref, d in zip(outs[:4], (dw[h], du[h], dqd[h], dkd[h])):
                    o_ref[:, _hs(h)] = d
                outs[4][h] = dqk[h]
                outs[5][:, _hs(h)] = dg[h]
                carry[h] = ds[h]
        else:
            o_ref, sp_ref = refs[6:8]
            state = [carry[h] for h in heads]
            o, new = _gdn_step(*args, state)
            for h in heads:
                sp_ref[h] = state[h]
                o_ref[:, _hs(h)] = o[h]
                carry[h] = new[h]

    big = jax.ShapeDtypeStruct((S, 1024), F32)
    in_specs = [row, row, row, row, qk_spec, row]
    if bwd:
        in_specs += [st_spec, row]
        out_specs = [row] * 4 + [qk_spec, row]
        out_shape = [big] * 4 + [jax.ShapeDtypeStruct((GDN_H, S, GDN_C), F32), big]
        ops = (w, u, qd, kd, qk, g_e, states, do)
    else:
        out_specs = [row, st_spec]
        out_shape = [big, jax.ShapeDtypeStruct((N, GDN_H, 128, 128), F32)]
        ops = (w, u, qd, kd, qk, g_e)
    return pl.pallas_call(
        kern, name="gdn_pass_bwd" if bwd else "gdn_pass", grid=(N,), in_specs=in_specs, out_specs=out_specs,
        out_shape=out_shape, scratch_shapes=[pltpu.VMEM((GDN_H, 128, 128), F32)],
        compiler_params=_cparams(("arbitrary",)))(*ops)


def _s5_prep_fn(lr, li, ldt, br, bi, cr, ci):
    dt = jnp.exp(ldt)
    mag = jnp.exp(lr * dt)
    a_re = mag * jnp.cos(li * dt)
    a_im = mag * jnp.sin(li * dt)
    den = lr * lr + li * li
    z_re = ((a_re - 1.0) * lr + a_im * li) / den
    z_im = (a_im * lr - (a_re - 1.0) * li) / den
    e1 = ((_iota((512, 32), 0) >> 4) == _iota((512, 32), 1)).astype(F32)
    zr_e = mmh(e1, z_re)
    zi_e = mmh(e1, z_im)
    bb_re = zr_e * br - zi_e * bi
    bb_im = zr_e * bi + zi_e * br
    t1 = ((_iota((64, 2048), 1) & 63) == _iota((64, 2048), 0)).astype(F32)
    m1 = (_iota((512, 2048), 0) >> 4) == (_iota((512, 2048), 1) >> 6)
    bd_re = jnp.where(m1, mmh(bb_re, t1), 0.0)
    bd_im = jnp.where(m1, mmh(bb_im, t1), 0.0)
    t2 = ((_iota((16, 512), 1) & 15) == _iota((16, 512), 0)).astype(F32)
    m2 = (_iota((2048, 512), 0) >> 6) == (_iota((2048, 512), 1) >> 4)
    cd_re = jnp.where(m2, mmh(cr, t2), 0.0)
    cd_im = jnp.where(m2, mmh(ci, t2), 0.0)
    return a_re, a_im, bd_re, bd_im, cd_re, cd_im


_PREP_OUT = [(32, 64), (32, 64), (512, 2048), (512, 2048), (2048, 512), (2048, 512)]


def _s5_prep(params, cots=None):
    bwd = cots is not None

    def kern(*refs):
        vals = [r[...] for r in refs[:7]]
        if bwd:
            gs = tuple(r[...] for r in refs[7:13])
            _, vjp = jax.vjp(_s5_prep_fn, *vals)
            for o_ref, d in zip(refs[13:], vjp(gs)):
                o_ref[...] = d
        else:
            for o_ref, o in zip(refs[7:], _s5_prep_fn(*vals)):
                o_ref[...] = o

    if bwd:
        out_shape = [jax.ShapeDtypeStruct(p.shape, F32) for p in params]
        ops = list(params) + list(cots)
    else:
        out_shape = [jax.ShapeDtypeStruct(s, F32) for s in _PREP_OUT]
        ops = list(params)
    return pl.pallas_call(kern, name="s5_prep_bwd" if bwd else "s5_prep", out_shape=out_shape,
                          compiler_params=_cparams())(*ops)


def _cmul(ar, ai, br, bi):
    return ar * br - ai * bi, ar * bi + ai * br


def _power_table(ar, ai, row8, descending):
    pr, pi = ar, ai
    tr = jnp.zeros(row8.shape, F32)
    ti = jnp.zeros(row8.shape, F32)
    for n in range(8):
        r = 7 - n if descending else n
        tr = jnp.where(row8 == r, pr, tr)
        ti = jnp.where(row8 == r, pi, ti)
        if n < 7:
            pr, pi = _cmul(pr, pi, ar, ai)
    return tr, ti


def _tile_scan(xr, xi, pows, row8, up):
    for d, (pr, pi) in zip((1, 2, 4), pows):
        if up:
            sr = jnp.where(row8 < 8 - d, pltpu.roll(xr, 8 - d, 0), 0.0)
            si = jnp.where(row8 < 8 - d, pltpu.roll(xi, 8 - d, 0), 0.0)
        else:
            sr = jnp.where(row8 >= d, pltpu.roll(xr, d, 0), 0.0)
            si = jnp.where(row8 >= d, pltpu.roll(xi, d, 0), 0.0)
        mr, mi = _cmul(pr, pi, sr, si)
        xr, xi = xr + mr, xi + mi
    return xr, xi


def _pick_row(x, row8, r):
    return jnp.sum(jnp.where(row8 == r, x, 0.0), axis=0, keepdims=True)


SCAN_LB = 512
SCAN_TS = 512


def _scan_fwd(bu_re, bu_im, a_re, a_im):
    S, L = bu_re.shape
    ts, lb = min(SCAN_TS, S), SCAN_LB
    nt = ts // 8

    def kern(br_ref, bi_ref, ar_ref, ai_ref, or_ref, oi_ref, cr_ref, ci_ref):
        @pl.when(pl.program_id(1) == 0)
        def _():
            cr_ref[...] = jnp.zeros_like(cr_ref)
            ci_ref[...] = jnp.zeros_like(ci_ref)

        row8 = _iota((8, lb), 0)
        ar, ai = ar_ref[...], ai_ref[...]
        a2 = _cmul(ar, ai, ar, ai)
        a4 = _cmul(*a2, *a2)
        pows = ((ar, ai), a2, a4)
        tr, ti = _power_table(ar, ai, row8, False)

        def body(i, carry):
            cr, ci = carry
            off = pl.multiple_of(i * 8, 8)
            xr, xi = _tile_scan(br_ref[pl.ds(off, 8), :], bi_ref[pl.ds(off, 8), :], pows, row8, False)
            mr, mi = _cmul(tr, ti, cr, ci)
            xr, xi = xr + mr, xi + mi
            or_ref[pl.ds(off, 8), :] = xr
            oi_ref[pl.ds(off, 8), :] = xi
            return _pick_row(xr, row8, 7), _pick_row(xi, row8, 7)

        cr, ci = lax.fori_loop(0, nt, body, (cr_ref[...], ci_ref[...]))
        cr_ref[...] = cr
        ci_ref[...] = ci

    blk = pl.BlockSpec((ts, lb), lambda j, i: (i, j))
    par = pl.BlockSpec((1, lb), lambda j, i: (0, j))
    return pl.pallas_call(
        kern, name="s5_scan_fwd", grid=(L // lb, S // ts), in_specs=[blk, blk, par, par], out_specs=[blk, blk],
        out_shape=[jax.ShapeDtypeStruct((S, L), F32)] * 2,
        scratch_shapes=[pltpu.VMEM((1, lb), F32), pltpu.VMEM((1, lb), F32)],
        compiler_params=_cparams(("parallel", "arbitrary")))(bu_re, bu_im, a_re, a_im)


def _scan_bwd(dst_re, dst_im, st_re, st_im, a_re, a_im):
    S, L = dst_re.shape
    ts, lb = min(SCAN_TS, S), SCAN_LB
    nt = ts // 8
    nb = S // ts
    r8 = ts // 8

    def kern(dr_ref, di_ref, sr_ref, si_ref, pr_ref, pi_ref, ar_ref, ai_ref, gr_ref, gi_ref, dar_ref, dai_ref,
             cr_ref, ci_ref):
        step = pl.program_id(1)
        blk = nb - 1 - step

        @pl.when(step == 0)
        def _():
            cr_ref[...] = jnp.zeros_like(cr_ref)
            ci_ref[...] = jnp.zeros_like(ci_ref)
            dar_ref[...] = jnp.zeros_like(dar_ref)
            dai_ref[...] = jnp.zeros_like(dai_ref)

        row8 = _iota((8, lb), 0)
        ar, ai = ar_ref[...], ai_ref[...]
        nai = -ai
        a2 = _cmul(ar, nai, ar, nai)
        a4 = _cmul(*a2, *a2)
        pows = ((ar, nai), a2, a4)
        tr, ti = _power_table(ar, nai, row8, True)
        halo_r = jnp.where(blk == 0, 0.0, pr_ref[...])
        halo_i = jnp.where(blk == 0, 0.0, pi_ref[...])

        def body(n, carry):
            cr, ci, acc_r, acc_i = carry
            i = nt - 1 - n
            off = pl.multiple_of(i * 8, 8)
            gr, gi = _tile_scan(dr_ref[pl.ds(off, 8), :], di_ref[pl.ds(off, 8), :], pows, row8, True)
            mr, mi = _cmul(tr, ti, cr, ci)
            gr, gi = gr + mr, gi + mi
            gr_ref[pl.ds(off, 8), :] = gr
            gi_ref[pl.ds(off, 8), :] = gi
            poff = pl.multiple_of(jnp.maximum(i - 1, 0) * 8, 8)
            before_r = jnp.where(i == 0, halo_r, sr_ref[pl.ds(poff, 8), :])
            before_i = jnp.where(i == 0, halo_i, si_ref[pl.ds(poff, 8), :])
            last_r = _pick_row(before_r, row8, 7)
            last_i = _pick_row(before_i, row8, 7)
            spr = jnp.where(row8 >= 1, pltpu.roll(sr_ref[pl.ds(off, 8), :], 1, 0), last_r)
            spi = jnp.where(row8 >= 1, pltpu.roll(si_ref[pl.ds(off, 8), :], 1, 0), last_i)
            acc_r = acc_r + gr * spr + gi * spi
            acc_i = acc_i + gi * spr - gr * spi
            return _pick_row(gr, row8, 0), _pick_row(gi, row8, 0), acc_r, acc_i

        zero = jnp.zeros((8, lb), F32)
        cr, ci, acc_r, acc_i = lax.fori_loop(0, nt, body, (cr_ref[...], ci_ref[...], zero, zero))
        cr_ref[...] = cr
        ci_ref[...] = ci
        dar_ref[...] += jnp.sum(acc_r, axis=0, keepdims=True)
        dai_ref[...] += jnp.sum(acc_i, axis=0, keepdims=True)

    blk = pl.BlockSpec((ts, lb), lambda j, i: (nb - 1 - i, j))
    halo = pl.BlockSpec((8, lb), lambda j, i: (jnp.maximum((nb - 1 - i) * r8 - 1, 0), j))
    par = pl.BlockSpec((1, lb), lambda j, i: (0, j))
    return pl.pallas_call(
        kern, name="s5_scan_bwd", grid=(L // lb, nb), in_specs=[blk, blk, blk, blk, halo, halo, par, par],
        out_specs=[blk, blk, par, par],
        out_shape=[jax.ShapeDtypeStruct((S, L), F32)] * 2 + [jax.ShapeDtypeStruct((1, L), F32)] * 2,
        scratch_shapes=[pltpu.VMEM((1, lb), F32), pltpu.VMEM((1, lb), F32)],
        compiler_params=_cparams(("parallel", "arbitrary")))(dst_re, dst_im, st_re, st_im, st_re, st_im, a_re, a_im)


def _loss_grad(x, target, gain, ts=256):
    S, D = x.shape

    def kern(x_ref, t_ref, g_ref, loss_ref, dx_ref, dg_ref):
        i = pl.program_id(0)
        tgt = t_ref[...]

        def f(xv, gv):
            err = _rms(xv, gv) - tgt
            return 0.5 * jnp.mean(err * err, axis=-1, keepdims=True)

        rowloss, vjp = jax.vjp(f, x_ref[...], g_ref[...])
        dx, dg = vjp(jnp.ones_like(rowloss))
        dx_ref[...] = dx

        @pl.when(i == 0)
        def _():
            loss_ref[...] = jnp.zeros_like(loss_ref)
            dg_ref[...] = jnp.zeros_like(dg_ref)

        loss_ref[...] += jnp.broadcast_to(jnp.sum(rowloss, axis=0, keepdims=True), loss_ref.shape)
        dg_ref[...] += dg

    row = pl.BlockSpec((ts, D), lambda i: (i, 0))
    return pl.pallas_call(
        kern, name="loss_grad", grid=(S // ts,), in_specs=[row, row, pl.BlockSpec((1, D), lambda i: (0, 0))],
        out_specs=[pl.BlockSpec((8, 128), lambda i: (0, 0)), row, pl.BlockSpec((1, D), lambda i: (0, 0))],
        out_shape=[jax.ShapeDtypeStruct((8, 128), F32), jax.ShapeDtypeStruct((S, D), F32),
                   jax.ShapeDtypeStruct((1, D), F32)],
        compiler_params=_cparams(("arbitrary",)))(x, target, gain)


def _rms_fwd(x, g, name):
    return _rowwise(_rms_fn, [_blk(x)], [g], [x.shape[1]], name, out_dtypes=[BF16])[0]


def _rms_bwd(x, g, dy, name, add=None):
    return _rowwise_bwd(_rms_fn, [_blk(x)], [g], [dy], name, adds=None if add is None else {0: add})


FFN_TC = 1408


def _common_fwd(x, mem, P, L):
    hx = _rms_fwd(x, P['xa_norm'], L + "xa_norm")
    q = _matmul(hx, P['xa_wq'], name=L + "xa_q")
    memn = _rms_fwd(mem, P['mem_norm'], L + "mem_norm")
    kv = _matmul(memn, P['xa_wkv'], name=L + "xa_kv")
    att = _rowwise(_xattn_fn, [_blk(q)], [kv], [1024], L + "xattn", out_dtypes=[BF16])[0]
    x2 = _matmul(att, P['xa_wo'], res=x, name=L + "xa_o")
    hf = _rms_fwd(x2, P['ffn_norm'], L + "ffn_norm")
    hu = _matmul(hf, P['ffn_w_up'], name=L + "ffn_up")
    cw = P['ffn_conv']
    act = _conv_post([(hu, 0, cw, 0), (hu, 2, cw, 2)], _ffn_post, 2, FFN_TC, L + "ffn_conv", out_dtype=BF16)
    x3 = _matmul(act, P['ffn_w_down'], res=x2, name=L + "ffn_down")
    return x3, (x, mem, hx, q, memn, kv, att, x2, hf, hu, act)


def _common_bwd(saved, dx3, P, L):
    x, mem, hx, q, memn, kv, att, x2, hf, hu, act = saved
    G = {}
    dact = _matmul(dx3, P['ffn_w_down'], "nt", name=L + "ffn_down_dx")
    G['ffn_w_down'] = _matmul(act, dx3, "tn", name=L + "ffn_down_dw")
    cw = P['ffn_conv']
    dhu_u, dcw_u, dhu_g, dcw_g = _conv_post_bwd([(hu, 0, cw, 0), (hu, 2, cw, 2)], _ffn_post, 2, FFN_TC, dact,
                                                L + "ffn_conv_bwd")
    G['ffn_conv'] = jnp.concatenate([dcw_u, dcw_g], axis=1)
    dhf = _matmul_cat([dhu_u, dhu_g], P['ffn_w_up'], "nt", name=L + "ffn_up_dx")
    G['ffn_w_up'] = jnp.concatenate([_matmul(hf, dhu_u, "tn", name=L + "ffn_up_dw_up"),
                                     _matmul(hf, dhu_g, "tn", name=L + "ffn_up_dw_gate")], axis=1)
    dx2, G['ffn_norm'] = _rms_bwd(x2, P['ffn_norm'], dhf, L + "ffn_norm_bwd", add=dx3)
    datt = _matmul(dx2, P['xa_wo'], "nt", name=L + "xa_o_dx")
    G['xa_wo'] = _matmul(att, dx2, "tn", name=L + "xa_o_dw")
    dq, dkv = _rowwise_bwd(_xattn_fn, [_blk(q)], [kv], [datt], L + "xattn_bwd", out_dtypes=[BF16])
    dhx = _matmul(dq, P['xa_wq'], "nt", name=L + "xa_q_dx")
    G['xa_wq'] = _matmul(hx, dq, "tn", name=L + "xa_q_dw")
    dmemn = _matmul(dkv, P['xa_wkv'], "nt", name=L + "xa_kv_dx")
    G['xa_wkv'] = _matmul(memn, dkv, "tn", name=L + "xa_kv_dw")
    _, G['mem_norm'] = _rms_bwd(mem, P['mem_norm'], dmemn, L + "mem_norm_bwd")
    dx, G['xa_norm'] = _rms_bwd(x, P['xa_norm'], dhx, L + "xa_norm_bwd", add=dx2)
    return dx, G


U_COLS = (2048, 512)


def _even_fwd(x, P):
    S = x.shape[0]
    h0 = _rms_fwd(x, P['mix_norm'], "l0_mix_norm")
    proj = _matmul(h0, P['w_in'], name="l0_in")
    tabs = _ret_tables(S)
    o_raw, rstates = _ret_call(proj, tabs)
    o = _rowwise(_ret_post_fn, [_blk(o_raw), _blk(proj, 512, 3)], [P['ret_norm']], [512], "l0_ret_post",
                 out_dtypes=[BF16])[0]
    prep_in = (P['s5_lambda_re'], P['s5_lambda_im'], P['s5_log_dt'], P['s5_b_re'], P['s5_b_im'], P['s5_c_re'],
               P['s5_c_im'])
    a_re, a_im, bd_re, bd_im, cd_re, cd_im = _s5_prep(prep_in)
    a_re_f, a_im_f = a_re.reshape(1, 2048), a_im.reshape(1, 2048)
    bu_re = _matmul(proj, bd_re, name="l0_s5_bu_re", a_cols=U_COLS)
    bu_im = _matmul(proj, bd_im, name="l0_s5_bu_im", a_cols=U_COLS)
    st_re, st_im = _scan_fwd(bu_re, bu_im, a_re_f, a_im_f)
    y1 = _matmul(st_re, cd_re, name="l0_s5_y_re")
    y2 = _matmul(st_im, cd_im, name="l0_s5_y_im")
    yg = _rowwise(_s5_post_fn, [_blk(y1), _blk(y2), _blk(proj, 512, 4)],
                  [P['s5_d'], P['s5_w_glu'], P['s5_b_glu']], [512], "l0_s5_post", out_dtypes=[BF16])[0]
    x1 = _matmul_cat([o, yg], P['w_out'], "nn", res=x, name="l0_out")
    saved = (x, h0, proj, tabs, o_raw, rstates, prep_in, a_re_f, a_im_f, bd_re, bd_im, cd_re, cd_im, st_re, st_im,
             y1, y2, o, yg)
    return x1, saved


def _even_bwd(saved, dx1, P):
    (x, h0, proj, tabs, o_raw, rstates, prep_in, a_re_f, a_im_f, bd_re, bd_im, cd_re, cd_im, st_re, st_im, y1, y2,
     o, yg) = saved
    G = {}
    dmerged = _matmul(dx1, P['w_out'], "nt", name="l0_out_dx")
    G['w_out'] = jnp.concatenate([_matmul(o, dx1, "tn", name="l0_out_dw_ret"),
                                  _matmul(yg, dx1, "tn", name="l0_out_dw_s5")], axis=0)
    do_raw, dgate, G['ret_norm'] = _rowwise_bwd(
        _ret_post_fn, [_blk(o_raw), _blk(proj, 512, 3)], [P['ret_norm']], [_blk(dmerged, 512, 0)], "l0_ret_post_bwd",
        out_dtypes=[F32, BF16])
    dq, dk, dv = _ret_call(proj, tabs, states=rstates, do=do_raw)
    dy1, dy2, du_a, G['s5_d'], G['s5_w_glu'], G['s5_b_glu'] = _rowwise_bwd(
        _s5_post_fn, [_blk(y1), _blk(y2), _blk(proj, 512, 4)], [P['s5_d'], P['s5_w_glu'], P['s5_b_glu']],
        [_blk(dmerged, 512, 1)], "l0_s5_post_bwd", out_dtypes=[BF16, BF16, F32])
    dst_re = _matmul(dy1, cd_re, "nt", name="l0_s5_y_re_dx")
    dcd_re = _matmul(st_re, dy1, "tn", name="l0_s5_y_re_dw")
    dst_im = _matmul(dy2, cd_im, "nt", name="l0_s5_y_im_dx")
    dcd_im = _matmul(st_im, dy2, "tn", name="l0_s5_y_im_dw")
    dbu_re, dbu_im, da_re, da_im = _scan_bwd(dst_re, dst_im, st_re, st_im, a_re_f, a_im_f)
    du = _matmul(dbu_re, bd_re, "nt", res=du_a, name="l0_s5_bu_re_dx")
    du = _matmul(dbu_im, bd_im, "nt", res=du, name="l0_s5_bu_im_dx", out_dtype=BF16)
    dbd_re = _matmul(proj, dbu_re, "tn", name="l0_s5_bu_re_dw", a_cols=U_COLS)
    dbd_im = _matmul(proj, dbu_im, "tn", name="l0_s5_bu_im_dw", a_cols=U_COLS)
    dprep = _s5_prep(prep_in, cots=(da_re.reshape(32, 64), da_im.reshape(32, 64), dbd_re, dbd_im, dcd_re, dcd_im))
    for n, d in zip(('s5_lambda_re', 's5_lambda_im', 's5_log_dt', 's5_b_re', 's5_b_im', 's5_c_re', 's5_c_im'), dprep):
        G[n] = d
    pieces = [dq, dk, dv, dgate, du]
    dh0 = _matmul_cat(pieces, P['w_in'], "nt", name="l0_in_dx")
    G['w_in'] = jnp.concatenate([_matmul(h0, p, "tn", name="l0_in_dw_%d" % n) for n, p in enumerate(pieces)], axis=1)
    dx, G['mix_norm'] = _rms_bwd(x, P['mix_norm'], dh0, "l0_mix_norm_bwd", add=dx1)
    return dx, G


def _odd_fwd(x, P):
    h1 = _rms_fwd(x, P['mix_norm'], "l1_mix_norm")
    pm = _matmul(h1, P['w_main'], name="l1_in_main")
    pt = _matmul(h1, P['w_tail'], name="l1_in_tail")
    qkv = _conv_post([(pm, 0, P['conv'], 0)], _silu, 3, 1024, "l1_conv")
    g_e, beta_e = _rowwise(_gdn_gates_fn, [_blk(pt)], [P['a_log_p'], P['dtb_p']], [1024, 1024], "l1_gdn_gates")
    w, u, qd, kd, qk, tinv = _gdn_intra_call(qkv, g_e, beta_e)
    o_raw, gstates = _gdn_pass(w, u, qd, kd, qk, g_e)
    og = _rowwise(_gdn_post_fn, [_blk(o_raw), _blk(pm, 1024, 3)], [P['o_norm']], [1024], "l1_gdn_post",
                  out_dtypes=[BF16])[0]
    x1 = _matmul(og, P['w_out'], res=x, name="l1_out")
    return x1, (x, h1, pm, pt, qkv, g_e, beta_e, w, u, qd, kd, qk, tinv, o_raw, gstates, og)


def _odd_bwd(saved, dx1, P):
    x, h1, pm, pt, qkv, g_e, beta_e, w, u, qd, kd, qk, tinv, o_raw, gstates, og = saved
    G = {}
    dog = _matmul(dx1, P['w_out'], "nt", name="l1_out_dx")
    G['w_out'] = _matmul(og, dx1, "tn", name="l1_out_dw")
    do_raw, dz, G['o_norm'] = _rowwise_bwd(_gdn_post_fn, [_blk(o_raw), _blk(pm, 1024, 3)], [P['o_norm']], [dog],
                                           "l1_gdn_post_bwd", out_dtypes=[F32, BF16])
    dw, du, dqd, dkd, dqk, dg_pass = _gdn_pass(w, u, qd, kd, qk, g_e, states=gstates, do=do_raw)
    dqkv = _gdn_intra_call(qkv, g_e, beta_e, cots=(dw, du, dqd, dkd, dqk, dg_pass, tinv))
    dg_e, dbeta_e = dqkv[3], dqkv[4]
    dpt, G['a_log_p'], G['dtb_p'] = _rowwise_bwd(_gdn_gates_fn, [_blk(pt)], [P['a_log_p'], P['dtb_p']],
                                                 [dg_e, dbeta_e], "l1_gdn_gates_bwd", out_dtypes=[BF16])
    pieces, dcw = [], []
    for part in range(3):
        dxp, dwp = _conv_post_bwd([(pm, part, P['conv'], part)], _silu, 1, 1024, dqkv[part],
                                  "l1_conv_bwd_%d" % part)
        pieces.append(dxp)
        dcw.append(dwp)
    G['conv'] = jnp.concatenate(dcw, axis=1)
    pieces += [dz, dpt]
    dh1 = _matmul_cat(pieces, P['w_all'], "nt", name="l1_in_dx")
    G['w_all'] = jnp.concatenate([_matmul(h1, p, "tn", name="l1_in_dw_%d" % n) for n, p in enumerate(pieces)], axis=1)
    dx, G['mix_norm'] = _rms_bwd(x, P['mix_norm'], dh1, "l1_mix_norm_bwd", add=dx1)
    return dx, G


def _row(v):
    return v.reshape(1, -1)


def _local_step(x, mem, target, W, later_weights=None, early_grads=None):
    P0 = {
        'mix_norm': _row(W['l0_mix_norm']), 'w_in': W['l0_w_in'], 'ret_norm': _row(W['l0_ret_norm']),
        's5_lambda_re': W['l0_s5_lambda_re'], 's5_lambda_im': W['l0_s5_lambda_im'],
        's5_log_dt': W['l0_s5_log_dt'].reshape(32, 1),
        's5_b_re': W['l0_s5_b_re'].reshape(512, 64), 's5_b_im': W['l0_s5_b_im'].reshape(512, 64),
        's5_c_re': W['l0_s5_c_re'].reshape(2048, 16), 's5_c_im': W['l0_s5_c_im'].reshape(2048, 16),
        's5_d': _row(W['l0_s5_d']), 's5_w_glu': W['l0_s5_w_glu'].astype(F32), 's5_b_glu': _row(W['l0_s5_b_glu']),
        'w_out': W['l0_w_out'],
    }
    def common(L):
        return {'xa_norm': _row(W[L + 'xa_norm']), 'mem_norm': _row(W[L + 'mem_norm']), 'xa_wq': W[L + 'xa_wq'],
                'xa_wkv': W[L + 'xa_wkv'], 'xa_wo': W[L + 'xa_wo'], 'ffn_norm': _row(W[L + 'ffn_norm']),
                'ffn_w_up': W[L + 'ffn_w_up'], 'ffn_conv': W[L + 'ffn_conv'], 'ffn_w_down': W[L + 'ffn_w_down']}

    x1, s_even = _even_fwd(x, P0)
    if later_weights is not None:
        W = dict(W, **later_weights('l0_common', x1))
    C0 = common('l0_')
    x3, s_c0 = _common_fwd(x1, mem, C0, "l0_")

    if later_weights is not None:
        W = dict(W, **later_weights('l1', x3))
    w_in1 = W['l1_w_in']
    pad8 = jnp.zeros((8,), F32)
    w_all = jnp.pad(w_in1, ((0, 0), (0, 112)))
    P1 = {
        'mix_norm': _row(W['l1_mix_norm']), 'w_main': w_in1[:, :4096], 'w_tail': w_all[:, 4096:], 'w_all': w_all,
        'conv': W['l1_conv'],
        'a_log_p': _row(jnp.concatenate([pad8, W['l1_a_log'], jnp.zeros((112,), F32)])),
        'dtb_p': _row(jnp.concatenate([pad8, W['l1_dt_bias'], jnp.zeros((112,), F32)])),
        'o_norm': _row(W['l1_o_norm']), 'w_out': W['l1_w_out'],
    }
    C1 = common('l1_')
    x4, s_odd = _odd_fwd(x3, P1)
    x6, s_c1 = _common_fwd(x4, mem, C1, "l1_")
    loss_tile, dx6, d_final = _loss_grad(x6, target, _row(W['final_norm']))

    G = {'final_norm': d_final.reshape(-1)}
    dx4, g = _common_bwd(s_c1, dx6, C1, "l1_")
    for k, v in g.items():
        G['l1_' + k] = v
    dx3, g = _odd_bwd(s_odd, dx4, P1)
    G['l1_mix_norm'] = g['mix_norm']
    G['l1_w_in'] = g['w_all'][:, :4112]
    G['l1_conv'] = g['conv']
    G['l1_a_log'] = g['a_log_p'][0, 8:16]
    G['l1_dt_bias'] = g['dtb_p'][0, 8:16]
    G['l1_o_norm'] = g['o_norm']
    G['l1_w_out'] = g['w_out']
    if early_grads is not None:
        zero = early_grads('l1', G)
        C0 = dict(C0, ffn_w_down=C0['ffn_w_down'] + zero.astype(C0['ffn_w_down'].dtype))
    dx1, g = _common_bwd(s_c0, dx3, C0, "l0_")
    for k, v in g.items():
        G['l0_' + k] = v
    if early_grads is not None:
        zero = early_grads('l0_common', G)
        P0 = dict(P0, w_out=P0['w_out'] + zero.astype(P0['w_out'].dtype))
    dx0, g = _even_bwd(s_even, dx1, P0)
    for k, v in g.items():
        G['l0_' + k] = v
    return loss_tile, dx0, G


ANY = pl.BlockSpec(memory_space=pl.ANY)


def _place():
    return lax.axis_index("x"), lax.axis_index("y"), lax.axis_index("c")


def _my_chip():
    return 2 * lax.axis_index("x") + lax.axis_index("y")


def _chip_peers(x, y):
    return [(1 - x, y), (x, 1 - y), (1 - x, 1 - y)]


def _half(ref, mode, shard, j, h, split):
    r, w = shard
    rh = r // 2 if split else r
    h = h if split else 0
    if mode == 'row':
        return ref.at[pl.ds(j * r + h * rh, rh), :]
    if mode == 'col':
        return ref.at[pl.ds(h * rh, rh), pl.ds(j * w, w)]
    return ref.at[j, pl.ds(h * rh, rh), :]


def _place_shard(shard, mode, name):
    r, w = shard.shape
    dtype = BF16 if mode != 'tap' else shard.dtype
    if mode == 'tap':
        mode = 'slab'
    tr = _row_tile(r, w)
    nb = r // tr

    def kern(s_ref, o_ref):
        o_ref[...] = s_ref[...].astype(o_ref.dtype)

    if mode == 'row':
        full, o_spec = (4 * r, w), pl.BlockSpec((tr, w), lambda i: (_my_chip() * nb + i, 0))
    elif mode == 'col':
        full, o_spec = (r, 4 * w), pl.BlockSpec((tr, w), lambda i: (i, _my_chip()))
    else:
        full, o_spec = (4, r, w), pl.BlockSpec((None, tr, w), lambda i: (_my_chip(), i, 0))
    return pl.pallas_call(kern, name=name, grid=(nb,), in_specs=[pl.BlockSpec((tr, w), lambda i: (i, 0))],
                          out_specs=o_spec, out_shape=jax.ShapeDtypeStruct(full, dtype),
                          compiler_params=_cparams(("parallel",)))(shard)


def _gather_placed(fulls, modes, shards, splits):
    n = len(fulls)

    def body(*refs):
        outs = refs[n:2 * n]
        send_sems, recv_sems = refs[2 * n:]
        x, y, c = _place()
        peers = _chip_peers(x, y)
        me = 2 * x + y

        def win(a, j, h):
            return _half(outs[a], modes[a], shards[a], j, h, splits[a])

        def copy(a, k, j, h, to):
            return pltpu.make_async_remote_copy(src_ref=win(a, j, h), dst_ref=win(a, j, h),
                                                send_sem=send_sems.at[6 * a + k], recv_sem=recv_sems.at[6 * a + k],
                                                device_id=to, device_id_type=MESH)

        over_ici = [copy(a, k, me, c, (p[0], p[1], c)) for a in range(n) for k, p in enumerate(peers)]
        for cp in over_ici:
            cp.start()
        passed = []
        for a in range(n):
            for k, p in enumerate(peers):
                j = 2 * p[0] + p[1]
                copy(a, k, j, c, (p[0], p[1], c)).wait_recv()
                if splits[a]:
                    fwd = copy(a, 3 + k, j, c, (x, y, 1 - c))
                    fwd.start()
                    passed.append(fwd)
        for a in range(n):
            if splits[a]:
                for k, p in enumerate(peers):
                    copy(a, 3 + k, 2 * p[0] + p[1], 1 - c, (x, y, 1 - c)).wait_recv()
        for cp in over_ici + passed:
            cp.wait_send()

    return pl.pallas_call(
        body, name="gather_weights", in_specs=[ANY] * n, out_specs=[ANY] * n,
        out_shape=[jax.ShapeDtypeStruct(f.shape, f.dtype) for f in fulls],
        input_output_aliases={a: a for a in range(n)},
        scratch_shapes=[pltpu.SemaphoreType.DMA((6 * n,)), pltpu.SemaphoreType.DMA((6 * n,))],
    )(*fulls)


_FLIPS = [(dx, dy, dc) for dx in (0, 1) for dy in (0, 1) for dc in (0, 1) if (dx, dy, dc) != (0, 0, 0)]


def _send_other_half(gs, small, name):
    n = len(gs)

    def body(*refs):
        ins, outs = refs[:n], refs[n + 1:2 * n + 1]
        small_ref = refs[2 * n + 1]
        send_sems, recv_sems, small_send, small_recv = refs[2 * n + 2:]
        x, y, c = _place()
        me = 4 * x + 2 * y + c

        def peer(f):
            return (x ^ f[0], y ^ f[1], c ^ f[2])

        def small_copy(k, slab, to):
            return pltpu.make_async_remote_copy(src_ref=small_ref.at[slab], dst_ref=small_ref.at[slab],
                                                send_sem=small_send.at[k], recv_sem=small_recv.at[k], device_id=to,
                                                device_id_type=MESH)

        cps = []
        for a in range(n):
            rh = gs[a].shape[1] // 2
            cps.append(pltpu.make_async_remote_copy(
                src_ref=ins[a].at[:, pl.ds((1 - c) * rh, rh), :], dst_ref=outs[a], send_sem=send_sems.at[a],
                recv_sem=recv_sems.at[a], device_id=(x, y, 1 - c), device_id_type=MESH))
        smalls = [small_copy(k, me, peer(f)) for k, f in enumerate(_FLIPS)]
        for cp in cps + smalls:
            cp.start()
        for cp in cps:
            cp.wait()
        for k, f in enumerate(_FLIPS):
            p = peer(f)
            small_copy(k, 4 * p[0] + 2 * p[1] + p[2], p).wait_recv()
        for cp in smalls:
            cp.wait_send()

    outs = pl.pallas_call(
        body, name=name, in_specs=[ANY] * (n + 1), out_specs=[ANY] * (n + 1),
        out_shape=[jax.ShapeDtypeStruct((g.shape[0], g.shape[1] // 2, g.shape[2]), g.dtype) for g in gs]
        + [jax.ShapeDtypeStruct(small.shape, small.dtype)],
        input_output_aliases={n: n},
        scratch_shapes=[pltpu.SemaphoreType.DMA((n,)), pltpu.SemaphoreType.DMA((n,)),
                        pltpu.SemaphoreType.DMA((7,)), pltpu.SemaphoreType.DMA((7,))],
    )(*gs, small)
    return outs[:n], outs[n]


def _send_to_chips(ps, widths):
    n = len(ps)

    def body(*refs):
        ins, outs = refs[:n], refs[n:2 * n]
        send_sems, recv_sems = refs[2 * n:]
        x, y, c = _place()
        peers = _chip_peers(x, y)
        me = 2 * x + y

        def src(a, j):
            if ps[a].shape[0] == 4:
                return ins[a].at[j]
            return ins[a].at[0, :, pl.ds(j * widths[a], widths[a])]

        def copy(a, k, j, dst_slab, to):
            return pltpu.make_async_remote_copy(src_ref=src(a, j), dst_ref=outs[a].at[dst_slab],
                                                send_sem=send_sems.at[3 * a + k], recv_sem=recv_sems.at[3 * a + k],
                                                device_id=(to[0], to[1], c), device_id_type=MESH)

        sends = [copy(a, k, 2 * p[0] + p[1], me, p) for a in range(n) for k, p in enumerate(peers)]
        for cp in sends:
            cp.start()
        for a in range(n):
            for k, p in enumerate(peers):
                copy(a, k, me, 2 * p[0] + p[1], p).wait_recv()
        for cp in sends:
            cp.wait_send()

    return pl.pallas_call(
        body, name="send_to_chips", in_specs=[ANY] * n, out_specs=[ANY] * n,
        out_shape=[jax.ShapeDtypeStruct((4, p.shape[1], w), p.dtype) for p, w in zip(ps, widths)],
        scratch_shapes=[pltpu.SemaphoreType.DMA((3 * n,)), pltpu.SemaphoreType.DMA((3 * n,))],
    )(*ps)


def _share_halves(bufs, name):
    n = len(bufs)

    def body(*refs):
        outs = refs[n:2 * n]
        send_sems, recv_sems = refs[2 * n:]
        x, y, c = _place()
        sends, waits = [], []
        for a in range(n):
            rh = bufs[a].shape[0] // 2
            mine = outs[a].at[pl.ds(c * rh, rh), :]
            other = outs[a].at[pl.ds((1 - c) * rh, rh), :]
            sends.append(pltpu.make_async_remote_copy(src_ref=mine, dst_ref=mine, send_sem=send_sems.at[a],
                                                      recv_sem=recv_sems.at[a], device_id=(x, y, 1 - c),
                                                      device_id_type=MESH))
            waits.append(pltpu.make_async_remote_copy(src_ref=mine, dst_ref=other, send_sem=send_sems.at[a],
                                                      recv_sem=recv_sems.at[a], device_id=(x, y, 1 - c),
                                                      device_id_type=MESH))
        for cp in sends:
            cp.start()
        for cp in waits:
            cp.wait()

    return pl.pallas_call(
        body, name=name, in_specs=[ANY] * n, out_specs=[ANY] * n,
        out_shape=[jax.ShapeDtypeStruct(b.shape, b.dtype) for b in bufs],
        input_output_aliases={a: a for a in range(n)},
        scratch_shapes=[pltpu.SemaphoreType.DMA((n,)), pltpu.SemaphoreType.DMA((n,))],
    )(*bufs)


def _gather_all(mine):
    flips = [(dx, dy, dc) for dx in (0, 1) for dy in (0, 1) for dc in (0, 1) if (dx, dy, dc) != (0, 0, 0)]

    def body(x_ref, out_ref, send_sems, recv_sems, local_sem):
        x, y, c = _place()
        me = 4 * x + 2 * y + c

        def peer(f):
            return (x ^ f[0], y ^ f[1], c ^ f[2])

        def copy(k, slab, to):
            return pltpu.make_async_remote_copy(src_ref=x_ref, dst_ref=out_ref.at[slab], send_sem=send_sems.at[k],
                                                recv_sem=recv_sems.at[k], device_id=to, device_id_type=MESH)

        own = pltpu.make_async_copy(x_ref, out_ref.at[me], local_sem)
        own.start()
        sends = [copy(k, me, peer(f)) for k, f in enumerate(flips)]
        for s in sends:
            s.start()
        for k, f in enumerate(flips):
            p = peer(f)
            copy(k, 4 * p[0] + 2 * p[1] + p[2], p).wait_recv()
        for s in sends:
            s.wait_send()
        own.wait()

    return pl.pallas_call(
        body, name="gather_all", in_specs=[ANY], out_specs=ANY,
        out_shape=jax.ShapeDtypeStruct((8,) + mine.shape, mine.dtype),
        scratch_shapes=[pltpu.SemaphoreType.DMA((7,)), pltpu.SemaphoreType.DMA((7,)), pltpu.SemaphoreType.DMA],
    )(mine)


TILE_BYTES = 2 * 1024 * 1024


def _row_tile(rows, width=1024):
    for t in (512, 352, 256, 176, 128, 64, 32, 16, 8):
        if rows % t == 0 and t * width * 4 <= TILE_BYTES:
            return t
    return rows


def _pair_sum(g, got, name):
    ns, r, w = g.shape
    rh = r // 2
    tr = _row_tile(rh, w)
    nb = rh // tr

    def kern(g_ref, o_ref, out_ref):
        out_ref[...] = (g_ref[...] + o_ref[...]).astype(BF16)

    return pl.pallas_call(
        kern, name=name, grid=(ns, nb),
        in_specs=[pl.BlockSpec((None, tr, w), lambda j, i: (j, lax.axis_index("c") * nb + i, 0)),
                  pl.BlockSpec((None, tr, w), lambda j, i: (j, i, 0))],
        out_specs=pl.BlockSpec((None, tr, w), lambda j, i: (j, i, 0)),
        out_shape=jax.ShapeDtypeStruct((ns, rh, w), BF16),
        compiler_params=_cparams(("parallel", "parallel")))(g, got)


def _chip_sum(pair, recv, w, name):
    rh = pair.shape[1]
    tr = _row_tile(rh, w)
    nb = rh // tr

    def kern(own_ref, r1_ref, r2_ref, r3_ref, out_ref):
        acc = own_ref[...].astype(F32)
        for r_ref in (r1_ref, r2_ref, r3_ref):
            acc = acc + r_ref[...].astype(F32)
        out_ref[...] = acc

    if pair.shape[0] == 4:
        own_spec = pl.BlockSpec((None, tr, w), lambda i: (_my_chip(), i, 0))
    else:
        own_spec = pl.BlockSpec((None, tr, w), lambda i: (0, i, _my_chip()))
    recv_specs = [pl.BlockSpec((None, tr, w), functools.partial(lambda i, d: ((_my_chip() + d) % 4, i, 0), d=d))
                  for d in (1, 2, 3)]
    return pl.pallas_call(
        kern, name=name, grid=(nb,), in_specs=[own_spec] + recv_specs,
        out_specs=pl.BlockSpec((tr, w), lambda i: (lax.axis_index("c") * nb + i, 0)),
        out_shape=jax.ShapeDtypeStruct((2 * rh, w), F32), compiler_params=_cparams(("parallel",)))(pair, recv, recv, recv)


def _slab_sum(slabs, name):
    n, R, w = slabs.shape
    tr = _row_tile(R)

    def kern(s_ref, o_ref):
        acc = s_ref[0].astype(F32)
        for k in range(1, n):
            acc = acc + s_ref[k].astype(F32)
        o_ref[...] = acc

    return pl.pallas_call(
        kern, name=name, grid=(R // tr,), in_specs=[pl.BlockSpec((n, tr, w), lambda i: (0, i, 0))],
        out_specs=pl.BlockSpec((tr, w), lambda i: (i, 0)), out_shape=jax.ShapeDtypeStruct((R, w), F32),
        compiler_params=_cparams(("parallel",)))(slabs)


def _adamw(w, g, m, v, name):
    R, C = w.shape
    tr = _pick(R, (256, 128, 64, 32, 16, 8))

    def kern(w_ref, g_ref, m_ref, v_ref, d_ref, nm_ref, nv_ref, g_out_ref):
        gv = g_ref[...]
        g_out_ref[...] = gv
        m2 = ADAM_B1 * m_ref[...] + (1.0 - ADAM_B1) * gv
        v2 = ADAM_B2 * v_ref[...] + (1.0 - ADAM_B2) * jnp.square(gv)
        m_hat = m2 / (1.0 - ADAM_B1 ** ADAM_STEP)
        v_hat = v2 / (1.0 - ADAM_B2 ** ADAM_STEP)
        d_ref[...] = -ADAM_LR * (m_hat / (jnp.sqrt(v_hat) + ADAM_EPS) + ADAM_WD * w_ref[...])
        nm_ref[...] = m2
        nv_ref[...] = v2

    spec = pl.BlockSpec((tr, C), lambda i: (i, 0))
    return pl.pallas_call(
        kern, name=name, grid=(R // tr,), in_specs=[spec] * 4, out_specs=[spec] * 4,
        out_shape=[jax.ShapeDtypeStruct((R, C), F32)] * 4, compiler_params=_cparams(("parallel",)))(w, g, m, v)


def _pack_small(vals):
    flat = jnp.concatenate([vals[n].astype(F32).reshape(-1) for n in SMALL_NAMES])
    rows = -(-flat.shape[0] // (8 * LANES)) * 8
    return jnp.pad(flat, (0, rows * LANES - flat.shape[0])).reshape(rows, LANES)


def _unpack_small(packed, shapes):
    flat = packed.reshape(-1)
    out = {}
    off = 0
    for n in SMALL_NAMES:
        size = int(np.prod(shapes[n]))
        out[n] = flat[off:off + size].reshape(shapes[n])
        off += size
    return out


HBM = pl.BlockSpec(memory_space=pltpu.HBM)
SEM = pl.BlockSpec(memory_space=pltpu.SEMAPHORE)
DATAFLOW = pltpu.SideEffectType.DATAFLOW_SIDE_EFFECTING


def _in_hbm(a):
    return pltpu.with_memory_space_constraint(a, pltpu.HBM)


def _split_copy_start(srcs, lands, copies, after, name):
    ns, nl = len(srcs), len(lands)
    ncopy = len(copies(list(srcs), list(lands), None, None, probe=True))

    def body(*refs):
        src_refs, land_refs = refs[:ns], refs[ns:ns + nl]
        send_sems, recv_sems = refs[ns + nl + 1:ns + nl + 3]
        token = refs[-1]
        for cp in copies(src_refs, land_refs, send_sems, recv_sems):
            cp.start()
        token[...] = jnp.zeros_like(token)

    outs = pl.pallas_call(
        body, name=name,
        out_shape=(pltpu.SemaphoreType.DMA((ncopy,)), pltpu.SemaphoreType.DMA((ncopy,)),
                   *[pltpu.HBM(a.shape, a.dtype) for a in srcs], *[pltpu.HBM(a.shape, a.dtype) for a in lands],
                   jax.ShapeDtypeStruct((8, 128), F32)),
        in_specs=[HBM] * (ns + nl) + [ANY],
        out_specs=(SEM, SEM, *[HBM] * (ns + nl), pl.BlockSpec(memory_space=pltpu.VMEM)),
        input_output_aliases={i: 2 + i for i in range(ns + nl)},
        compiler_params=pltpu.CompilerParams(has_side_effects=DATAFLOW),
    )(*[_in_hbm(a) for a in srcs], *[_in_hbm(a) for a in lands], after)
    return outs[0], outs[1], outs[2:2 + ns], outs[2 + ns:2 + ns + nl], outs[-1]


def _split_copy_wait(send_sems, recv_sems, srcs, lands, copies, after, name):
    ns, nl = len(srcs), len(lands)

    def body(*refs):
        src_refs, land_refs = refs[:ns], refs[ns:ns + nl]
        send_ref, recv_ref = refs[ns + nl:ns + nl + 2]
        for cp in copies(src_refs, land_refs, send_ref, recv_ref):
            cp.wait_send()
            cp.wait_recv()

    outs = pl.pallas_call(
        body, name=name,
        out_shape=tuple(pltpu.HBM(a.shape, a.dtype) for a in list(srcs) + list(lands)),
        in_specs=[HBM] * (ns + nl) + [SEM, SEM, ANY], out_specs=tuple([HBM] * (ns + nl)),
        input_output_aliases={i: i for i in range(ns + nl)},
        compiler_params=pltpu.CompilerParams(has_side_effects=DATAFLOW),
    )(*srcs, *lands, send_sems, recv_sems, after)
    return outs[:ns], outs[ns:]


def _matrix_mode(n):
    return 'slab' if n == 'l1_w_in' else ('row' if MATRICES[n] == 0 else 'col')


def _placed(A, names):
    modes = ['slab' if n in CONVS else _matrix_mode(n) for n in names]
    fulls = [_place_shard(A[n], 'tap' if n in CONVS else m, "place_" + n) for n, m in zip(names, modes)]
    return fulls, modes


def _assembled(names, modes, outs):
    return {n: jnp.concatenate([o[j] for j in range(4)], axis=1) if m == 'slab' else o
            for n, m, o in zip(names, modes, outs)}


def _gather_weights(A, names):
    fulls, modes = _placed(A, names)
    outs = _gather_placed(fulls, modes, [A[n].shape for n in names], [n not in CONVS for n in names])
    return _assembled(names, modes, outs)


def _whole_shard_copies(modes, shards):
    def copies(src_refs, land_refs, send_sems, recv_sems, probe=False):
        if probe:
            return [None] * (3 * len(land_refs))
        x, y, c = _place()
        me = 2 * x + y
        out = []
        for a, ref in enumerate(land_refs):
            for k, p in enumerate(_chip_peers(x, y)):
                out.append(pltpu.make_async_remote_copy(
                    src_ref=_half(ref, modes[a], shards[a], me, 0, False),
                    dst_ref=_half(ref, modes[a], shards[a], me, 0, False),
                    send_sem=send_sems.at[3 * a + k], recv_sem=recv_sems.at[3 * a + k],
                    device_id=(p[0], p[1], c), device_id_type=MESH))
        return out
    return copies


def _gather_weights_start(A, names, after, tag):
    fulls, modes = _placed(A, names)
    copies = _whole_shard_copies(modes, [A[n].shape for n in names])
    send_sems, recv_sems, _, lands, zeros = _split_copy_start([], fulls, copies, after, "gather_start_" + tag)
    return (send_sems, recv_sems, lands, copies, names, modes), zeros


def _gather_weights_wait(state, after, tag):
    send_sems, recv_sems, lands, copies, names, modes = state
    _, outs = _split_copy_wait(send_sems, recv_sems, [], lands, copies, after, "gather_wait_" + tag)
    return _assembled(names, modes, outs)


def _to_chips_copies(pair_shapes, widths):
    def copies(src_refs, land_refs, send_sems, recv_sems, probe=False):
        if probe:
            return [None] * (3 * len(land_refs))
        x, y, c = _place()
        me = 2 * x + y
        out = []
        for a, (src, land) in enumerate(zip(src_refs, land_refs)):
            for k, p in enumerate(_chip_peers(x, y)):
                j = 2 * p[0] + p[1]
                part = src.at[j] if pair_shapes[a][0] == 4 else src.at[0, :, pl.ds(j * widths[a], widths[a])]
                out.append(pltpu.make_async_remote_copy(
                    src_ref=part, dst_ref=land.at[me], send_sem=send_sems.at[3 * a + k],
                    recv_sem=recv_sems.at[3 * a + k], device_id=(p[0], p[1], c), device_id_type=MESH))
        return out
    return copies


def _reduce_begin(G, names, small, tag):
    gs, widths = [], []
    for n in names:
        g = G[n]
        mode = _matrix_mode(n)
        if mode == 'row':
            gs.append(g.reshape(4, g.shape[0] // 4, g.shape[1]))
            widths.append(g.shape[1])
        elif mode == 'col':
            gs.append(g[None])
            widths.append(g.shape[1] // 4)
        else:
            wd = g.shape[1] // 4
            gs.append(jnp.stack([g[:, j * wd:(j + 1) * wd] for j in range(4)]))
            widths.append(wd)
    got, small = _send_other_half(gs, small, "send_other_half_" + tag)
    pairs = [_pair_sum(g, o, "pair_sum_" + n) for n, g, o in zip(names, gs, got)]
    return pairs, widths, small


def _reduce_end(names, pairs, recv, widths, tag):
    halves = [_chip_sum(p, r, w, "chip_sum_" + n) for n, p, r, w in zip(names, pairs, recv, widths)]
    return dict(zip(names, _share_halves(halves, "share_halves_" + tag)))


def _small_slab(packed):
    me8 = 4 * lax.axis_index("x") + 2 * lax.axis_index("y") + lax.axis_index("c")
    return lax.dynamic_update_slice(jnp.zeros((8,) + packed.shape, F32), packed[None], (me8, 0, 0))


def kernel(*args):
    A = dict(zip(ARG_NAMES, args, strict=True))
    x, mem, target = A['x'][0], A['mem'][0], A['loss_target'][0]

    stages = {'l0_mixer': ['l0_w_in', 'l0_s5_w_glu', 'l0_w_out'],
              'l0_common': [n for n in MATRIX_NAMES if n.startswith(('l0_xa_', 'l0_ffn_'))],
              'l1': [n for n in MATRIX_NAMES if n.startswith('l1_')]}
    W = _gather_weights(A, stages['l0_mixer'] + list(CONVS))
    for n in SMALL_NAMES:
        if n not in CONVS:
            W[n] = A[n]
    flights = {}
    after = W['l0_w_in']
    for stage in ('l0_common', 'l1'):
        flights[stage], after = _gather_weights_start(A, stages[stage], after, stage)
    W['l0_mix_norm'] = W['l0_mix_norm'] + after[0, 0]

    reduce_state = {}

    def early_grads(stage, G):
        pairs, widths, _ = _reduce_begin(G, stages[stage], jnp.zeros((8, 8, LANES), F32), stage)
        copies = _to_chips_copies([p.shape for p in pairs], widths)
        lands = [lax.empty((4, p.shape[1], w), p.dtype) for p, w in zip(pairs, widths)]
        send_sems, recv_sems, pairs, lands, zeros = _split_copy_start(pairs, lands, copies, G['final_norm'],
                                                                      "reduce_start_" + stage)
        reduce_state[stage] = (send_sems, recv_sems, pairs, lands, copies, widths)
        return zeros[0, 0]

    loss_tile, grad_x, G = _local_step(
        x, mem, target, W, later_weights=lambda stage, after: _gather_weights_wait(flights[stage], after, stage),
        early_grads=early_grads)
    loss = lax.psum(loss_tile[0, 0], ("x", "y", "c"))

    g_mat = {}
    for stage in ('l1', 'l0_common'):
        send_sems, recv_sems, pairs, lands, copies, widths = reduce_state[stage]
        sent, recv = _split_copy_wait(send_sems, recv_sems, pairs, lands, copies, grad_x, "reduce_wait_" + stage)
        g_mat.update(_reduce_end(stages[stage], sent, recv, widths, stage))
    pairs, widths, g_small = _reduce_begin(G, stages['l0_mixer'],
                                           _small_slab(_pack_small({n: G[n] for n in SMALL_NAMES})), "l0_mixer")
    g_mat.update(_reduce_end(stages['l0_mixer'], pairs, _send_to_chips(pairs, widths), widths, "l0_mixer"))
    g_small = _unpack_small(_slab_sum(g_small, "sum_small"), {n: G[n].shape for n in SMALL_NAMES})
    me = 2 * lax.axis_index("x") + lax.axis_index("y")
    for n in CONVS:
        wd = A[n].shape[1]
        g_small[n] = lax.dynamic_slice_in_dim(g_small[n], me * wd, wd, axis=1)
    flat_names = [n for n in SMALL_NAMES if n not in CONVS]

    def pack_flat(prefix):
        return _pack_small_flat({n: A[prefix + n] for n in flat_names}, flat_names)

    shapes = {n: A[n].shape for n in flat_names}
    d_s, m_s, v_s, _ = _adamw(pack_flat(''), _pack_small_flat(g_small, flat_names), pack_flat('m_'), pack_flat('v_'),
                              "adamw_small")
    d_s, m_s, v_s = (_unpack_flat(p, shapes, flat_names) for p in (d_s, m_s, v_s))

    grads, deltas, new_m, new_v = {}, {}, {}, {}
    for n in WEIGHTS:
        if n in MATRICES or n in CONVS:
            g = g_mat[n] if n in MATRICES else g_small[n]
            deltas[n], new_m[n], new_v[n], grads[n] = _adamw(A[n], g, A['m_' + n], A['v_' + n], "adamw_" + n)
        else:
            grads[n] = g_small[n].reshape(A[n].shape)
            deltas[n], new_m[n], new_v[n] = d_s[n], m_s[n], v_s[n]
    return (loss, grad_x[None], *[grads[n] for n in WEIGHTS], *[deltas[n] for n in WEIGHTS],
            *[new_m[n] for n in WEIGHTS], *[new_v[n] for n in WEIGHTS])


def _pack_small_flat(vals, names):
    flat = jnp.concatenate([vals[n].astype(F32).reshape(-1) for n in names])
    rows = -(-flat.shape[0] // (8 * LANES)) * 8
    return jnp.pad(flat, (0, rows * LANES - flat.shape[0])).reshape(rows, LANES)


def _unpack_flat(packed, shapes, names):
    flat = packed.reshape(-1)
    out = {}
    off = 0
    for n in names:
        size = int(np.prod(shapes[n]))
        out[n] = flat[off:off + size].reshape(shapes[n])
        off += size
    return out
```

```python
import functools
import math

import numpy as np
import jax
import jax.numpy as jnp
from jax import lax
from jax.experimental import pallas as pl
from jax.experimental.pallas import tpu as pltpu

F32 = jnp.float32
BF16 = jnp.bfloat16
EPS = 1e-6
MESH = pl.DeviceIdType.MESH

ADAM_LR = 0.001
ADAM_B1 = 0.9
ADAM_B2 = 0.999
ADAM_EPS = 1e-08
ADAM_WD = 0.01
ADAM_STEP = 10

VMEM_LIMIT_BYTES = 56 * 1024 * 1024
MATMUL_VMEM_BYTES = 44 * 1024 * 1024
LANES = 1024

WEIGHTS = ['l0_mix_norm', 'l0_w_in', 'l0_ret_norm', 'l0_s5_lambda_re', 'l0_s5_lambda_im', 'l0_s5_b_re', 'l0_s5_b_im',
           'l0_s5_c_re', 'l0_s5_c_im', 'l0_s5_d', 'l0_s5_log_dt', 'l0_s5_w_glu', 'l0_s5_b_glu', 'l0_w_out',
           'l0_xa_norm', 'l0_mem_norm', 'l0_xa_wq', 'l0_xa_wkv', 'l0_xa_wo', 'l0_ffn_norm', 'l0_ffn_w_up',
           'l0_ffn_conv', 'l0_ffn_w_down', 'l1_mix_norm', 'l1_w_in', 'l1_conv', 'l1_a_log', 'l1_dt_bias',
           'l1_o_norm', 'l1_w_out', 'l1_xa_norm', 'l1_mem_norm', 'l1_xa_wq', 'l1_xa_wkv', 'l1_xa_wo',
           'l1_ffn_norm', 'l1_ffn_w_up', 'l1_ffn_conv', 'l1_ffn_w_down', 'final_norm']
ARG_NAMES = (['x', 'mem'] + WEIGHTS + ['loss_target'] + ['m_' + w for w in WEIGHTS] + ['v_' + w for w in WEIGHTS])

MATRICES = {
    'l0_w_in': 1, 'l0_s5_w_glu': 0, 'l0_w_out': 0, 'l0_xa_wq': 0, 'l0_xa_wkv': 1, 'l0_xa_wo': 0, 'l0_ffn_w_up': 1,
    'l0_ffn_w_down': 0, 'l1_w_in': 1, 'l1_w_out': 0, 'l1_xa_wq': 0, 'l1_xa_wkv': 1, 'l1_xa_wo': 0,
    'l1_ffn_w_up': 1, 'l1_ffn_w_down': 0,
}
CONVS = ('l0_ffn_conv', 'l1_conv', 'l1_ffn_conv')
MATRIX_NAMES = [w for w in WEIGHTS if w in MATRICES]
SMALL_NAMES = [w for w in WEIGHTS if w not in MATRICES]


def _cparams(sem=None):
    return pltpu.CompilerParams(dimension_semantics=sem, vmem_limit_bytes=VMEM_LIMIT_BYTES)


def _pick(n, cands):
    for c in cands:
        if n % c == 0:
            return c
    return n


_NN = ((1,), (0,))
_NT = ((1,), (1,))
_TN = ((0,), (0,))


def _dot(a, b, dims, hi):
    if hi is not None:
        return lax.dot_general(a.astype(F32), b.astype(F32), (dims, ((), ())), precision=hi,
                               preferred_element_type=F32)
    return lax.dot_general(a.astype(BF16), b.astype(BF16), (dims, ((), ())), preferred_element_type=F32)


def _make_mm(hi):
    @jax.custom_vjp
    def nn(a, b):
        return _dot(a, b, _NN, hi)

    def nn_f(a, b):
        return nn(a, b), (a, b)

    def nn_b(r, g):
        a, b = r
        return _dot(g, b, _NT, hi), _dot(a, g, _TN, hi)

    nn.defvjp(nn_f, nn_b)

    @jax.custom_vjp
    def nt(a, b):
        return _dot(a, b, _NT, hi)

    def nt_f(a, b):
        return nt(a, b), (a, b)

    def nt_b(r, g):
        a, b = r
        return _dot(g, b, _NN, hi), _dot(g, a, _TN, hi)

    nt.defvjp(nt_f, nt_b)

    @jax.custom_vjp
    def tn(a, b):
        return _dot(a, b, _TN, hi)

    def tn_f(a, b):
        return tn(a, b), (a, b)

    def tn_b(r, g):
        a, b = r
        return _dot(b, g, _NT, hi), _dot(a, g, _NN, hi)

    tn.defvjp(tn_f, tn_b)
    return nn, nt, tn


mm, mm_nt, mm_tn = _make_mm(None)
mmh, mmh_nt, mmh_tn = _make_mm(lax.Precision.HIGHEST)
mm3, _, _ = _make_mm(lax.Precision.HIGH)


@jax.custom_vjp
def _swap_halves(x):
    return pltpu.roll(x, 64, 1)


def _swap_f(x):
    return pltpu.roll(x, 64, 1), None


def _swap_b(_, g):
    return (pltpu.roll(g, 64, 1),)


_swap_halves.defvjp(_swap_f, _swap_b)


def _silu(x):
    return x * jax.nn.sigmoid(x)


def _rms(x, g):
    return x * lax.rsqrt(jnp.mean(x * x, axis=-1, keepdims=True) + EPS) * g


def _iota(shape, dim):
    return lax.broadcasted_iota(jnp.int32, shape, dim)


def _matmul_tiles(M, N, K, a_bytes, b_bytes, has_res, a_off):
    def divisors(n, cands):
        return [c for c in cands if n % c == 0] or [n]

    fallback = None
    for tk in divisors(K, (K, 2048, 1408, 1024, 512, 256, 128)):
        for tm in divisors(M, (1024, 512, 1408, 256, 128)):
            for tn in divisors(N, (1408, 1024, 512, 256, 128)):
                need = 2 * (tm * tk * a_bytes + tk * tn * b_bytes + (tm * tn * 4 if has_res else 0)) + 3 * tm * tn * 4
                if need > MATMUL_VMEM_BYTES or a_off % tk or a_off % tm:
                    continue
                if tm >= 256 and tn >= 256:
                    return tm, tn, tk
                fallback = fallback or (tm, tn, tk)
    return fallback


def _matmul(a, b, mode="nn", res=None, name="mm", a_cols=None, out_dtype=F32):
    a_off, a_w = (0, a.shape[1]) if a_cols is None else a_cols
    if mode == "nn":
        (M, K), (K2, N) = (a.shape[0], a_w), b.shape
    elif mode == "nt":
        (M, K), (N, K2) = (a.shape[0], a_w), b.shape
    else:
        (K, M), (K2, N) = (a.shape[0], a_w), b.shape
    assert K == K2, (a.shape, b.shape, mode)
    tm, tn, tk = _matmul_tiles(M, N, K, 2 if a.dtype == BF16 else 4, 2 if b.dtype == BF16 else 4, res is not None,
                               a_off)
    nk = K // tk
    dims = {"nn": _NN, "nt": _NT, "tn": _TN}[mode]
    ao = a_off // (tm if mode == "tn" else tk)
    assert ao * (tm if mode == "tn" else tk) == a_off
    if mode == "nn":
        a_spec = pl.BlockSpec((tm, tk), lambda i, j, k: (i, k + ao))
        b_spec = pl.BlockSpec((tk, tn), lambda i, j, k: (k, j))
    elif mode == "nt":
        a_spec = pl.BlockSpec((tm, tk), lambda i, j, k: (i, k + ao))
        b_spec = pl.BlockSpec((tn, tk), lambda i, j, k: (j, k))
    else:
        a_spec = pl.BlockSpec((tk, tm), lambda i, j, k: (k, i + ao))
        b_spec = pl.BlockSpec((tk, tn), lambda i, j, k: (k, j))
    o_spec = pl.BlockSpec((tm, tn), lambda i, j, k: (i, j))
    has_res = res is not None

    def kern(*refs):
        a_ref, b_ref = refs[:2]
        r_ref = refs[2] if has_res else None
        o_ref = refs[3] if has_res else refs[2]
        acc_ref = refs[-1] if nk > 1 else None
        k = pl.program_id(2)
        part = lax.dot_general(a_ref[...].astype(BF16), b_ref[...].astype(BF16), (dims, ((), ())),
                               preferred_element_type=F32)
        if nk == 1:
            o_ref[...] = (part + r_ref[...] if has_res else part).astype(o_ref.dtype)
            return

        @pl.when(k == 0)
        def _():
            acc_ref[...] = part

        @pl.when((k > 0) & (k < nk - 1))
        def _():
            acc_ref[...] += part

        @pl.when(k == nk - 1)
        def _():
            total = acc_ref[...] + part
            o_ref[...] = (total + r_ref[...] if has_res else total).astype(o_ref.dtype)

    in_specs = [a_spec, b_spec] + ([o_spec] if has_res else [])
    ops = (a, b) + ((res,) if has_res else ())
    return pl.pallas_call(
        kern, name=name, grid=(M // tm, N // tn, nk), in_specs=in_specs, out_specs=o_spec,
        out_shape=jax.ShapeDtypeStruct((M, N), out_dtype),
        scratch_shapes=[pltpu.VMEM((tm, tn), F32)] if nk > 1 else [],
        compiler_params=_cparams(("parallel", "parallel", "arbitrary")))(*ops)


def _matmul_cat(pieces, b, mode="nn", res=None, name="mmcat"):
    M = pieces[0].shape[0]
    widths = [p.shape[1] for p in pieces]
    K = sum(widths)
    N = b.shape[1] if mode == "nn" else b.shape[0]
    assert (b.shape[0] if mode == "nn" else b.shape[1]) == K
    tn = _pick(N, (1024, 512, 256, 128))
    a_bytes = 2 if pieces[0].dtype == BF16 else 4
    for tm in (1024, 512, 256, 128):
        need = 2 * (tm * K * a_bytes + K * tn * 2 + (tm * tn * 4 if res is not None else 0)) + 3 * tm * tn * 4
        if M % tm == 0 and need <= MATMUL_VMEM_BYTES:
            break
    npc = len(pieces)
    has_res = res is not None
    dims = _NN if mode == "nn" else _NT

    def kern(*refs):
        b_ref = refs[npc]
        o_ref = refs[-1]
        acc = refs[npc + 1][...] if has_res else None
        off = 0
        for p in range(npc):
            bp = b_ref[off:off + widths[p], :] if mode == "nn" else b_ref[:, off:off + widths[p]]
            t = lax.dot_general(refs[p][...].astype(BF16), bp.astype(BF16), (dims, ((), ())),
                                preferred_element_type=F32)
            acc = t if acc is None else acc + t
            off += widths[p]
        o_ref[...] = acc

    in_specs = [pl.BlockSpec((tm, w), lambda j, i: (i, 0)) for w in widths]
    in_specs.append(pl.BlockSpec((K, tn), lambda j, i: (0, j)) if mode == "nn"
                    else pl.BlockSpec((tn, K), lambda j, i: (j, 0)))
    o_spec = pl.BlockSpec((tm, tn), lambda j, i: (i, j))
    if has_res:
        in_specs.append(o_spec)
    ops = list(pieces) + [b] + ([res] if has_res else [])
    return pl.pallas_call(
        kern, name=name, grid=(N // tn, M // tm), in_specs=in_specs, out_specs=o_spec,
        out_shape=jax.ShapeDtypeStruct((M, N), F32), compiler_params=_cparams(("parallel", "parallel")))(*ops)


def _blk(a, width=None, colblk=0):
    return (a, a.shape[1] if width is None else width, colblk)


def _row_specs(blocked, params, ts):
    specs = []
    for (_, w, cb) in blocked:
        specs.append(pl.BlockSpec((ts, w), functools.partial(lambda i, cb: (i, cb), cb=cb)))
    for p in params:
        specs.append(pl.BlockSpec(p.shape, lambda i: (0, 0)))
    return specs


def _rowwise(fn, blocked, params, out_widths, name, ts=256, out_dtypes=None):
    S = blocked[0][0].shape[0]
    ts = min(ts, S)
    nb, npar = len(blocked), len(params)
    out_dtypes = [F32] * len(out_widths) if out_dtypes is None else out_dtypes

    def kern(*refs):
        vals = [r[...] for r in refs[:nb + npar]]
        outs = fn(*vals)
        for o_ref, o in zip(refs[nb + npar:], outs):
            o_ref[...] = o.astype(o_ref.dtype)

    return pl.pallas_call(
        kern, name=name, grid=(S // ts,), in_specs=_row_specs(blocked, params, ts),
        out_specs=[pl.BlockSpec((ts, w), lambda i: (i, 0)) for w in out_widths],
        out_shape=[jax.ShapeDtypeStruct((S, w), d) for w, d in zip(out_widths, out_dtypes)],
        compiler_params=_cparams(("parallel",)))(*[b[0] for b in blocked], *params)


def _rowwise_bwd(fn, blocked, params, cots, name, blocked_grad=None, param_grad=None, adds=None, ts=256,
                 out_dtypes=None):
    S = blocked[0][0].shape[0]
    ts = min(ts, S)
    cots = [c if isinstance(c, tuple) else _blk(c) for c in cots]
    nb, npar, nc = len(blocked), len(params), len(cots)
    blocked_grad = [True] * nb if blocked_grad is None else blocked_grad
    param_grad = [True] * npar if param_grad is None else param_grad
    adds = {} if adds is None else adds
    bidx = [i for i in range(nb) if blocked_grad[i]]
    pidx = [i for i in range(npar) if param_grad[i]]
    add_keys = sorted(adds)
    n_in = nb + npar + nc + len(add_keys)

    def kern(*refs):
        i = pl.program_id(0)
        xs = [r[...] for r in refs[:nb]]
        ps = [r[...] for r in refs[nb:nb + npar]]
        gs = [r[...] for r in refs[nb + npar:nb + npar + nc]]
        add_vals = {k: refs[nb + npar + nc + n][...] for n, k in enumerate(add_keys)}
        outs = refs[n_in:]

        def f(*diff):
            full_x = list(xs)
            full_p = list(ps)
            for n, ix in enumerate(bidx):
                full_x[ix] = diff[n]
            for n, ix in enumerate(pidx):
                full_p[ix] = diff[len(bidx) + n]
            return tuple(fn(*full_x, *full_p))

        _, vjp = jax.vjp(f, *[xs[ix] for ix in bidx], *[ps[ix] for ix in pidx])
        grads = vjp(tuple(gs))
        for n, ix in enumerate(bidx):
            g = grads[n]
            if ix in add_vals:
                g = g + add_vals[ix]
            outs[n][...] = g.astype(outs[n].dtype)
        for n in range(len(pidx)):
            o_ref = outs[len(bidx) + n]

            @pl.when(i == 0)
            def _(o_ref=o_ref):
                o_ref[...] = jnp.zeros_like(o_ref)

            o_ref[...] += grads[len(bidx) + n]

    in_specs = _row_specs(blocked, params, ts)
    in_specs += _row_specs(cots, [], ts)
    in_specs += [pl.BlockSpec((ts, adds[k].shape[1]), lambda i: (i, 0)) for k in add_keys]
    out_specs = [pl.BlockSpec((ts, blocked[ix][1]), lambda i: (i, 0)) for ix in bidx]
    out_specs += [pl.BlockSpec(params[ix].shape, lambda i: (0, 0)) for ix in pidx]
    out_dtypes = [F32] * len(bidx) if out_dtypes is None else out_dtypes
    out_shape = [jax.ShapeDtypeStruct((S, blocked[ix][1]), d) for ix, d in zip(bidx, out_dtypes)]
    out_shape += [jax.ShapeDtypeStruct(params[ix].shape, F32) for ix in pidx]
    return pl.pallas_call(
        kern, name=name, grid=(S // ts,), in_specs=in_specs, out_specs=out_specs, out_shape=out_shape,
        compiler_params=_cparams(("arbitrary",)))(*[b[0] for b in blocked], *params, *[c[0] for c in cots],
                                                    *[adds[k] for k in add_keys])


def _rms_fn(x, g):
    return (_rms(x, g),)


def _head_norm(o, n_heads, dh):
    outs = []
    for h in range(n_heads):
        oh = o[:, h * dh:(h + 1) * dh]
        outs.append(oh * lax.rsqrt(jnp.mean(oh * oh, axis=-1, keepdims=True) + EPS))
    return outs


def _ret_post_fn(o_raw, gate, ret_norm):
    o = jnp.concatenate(_head_norm(o_raw, 4, 128), axis=1)
    return (o * ret_norm * _silu(gate),)


def _s5_post_fn(y1, y2, u, d, w_glu, b_glu):
    y = y1 - y2 + d * u
    y = jax.nn.gelu(y)
    return (y * jax.nn.sigmoid(mm(y, w_glu) + b_glu),)


def _xattn_fn(q, kv):
    outs = []
    for h in range(4):
        qh = q[:, h * 256:(h + 1) * 256]
        kh = kv[:, h * 256:(h + 1) * 256]
        vh = kv[:, 1024 + h * 256:1024 + (h + 1) * 256]
        s = mm_nt(qh, kh) * (256 ** -0.5)
        s = s - lax.stop_gradient(jnp.max(s, axis=-1, keepdims=True))
        p = jnp.exp(s)
        p = p / jnp.sum(p, axis=-1, keepdims=True)
        outs.append(mm(p, vh))
    return (jnp.concatenate(outs, axis=1),)


def _softplus(x):
    return jnp.maximum(x, 0.0) + jnp.log1p(jnp.exp(-jnp.abs(x)))


def _gdn_gates_fn(pt, a_log_p, dtb_p):
    rows, cols = _iota((128, 1024), 0), _iota((128, 1024), 1)
    e_b = (rows == (cols >> 7)).astype(F32)
    e_a = (rows == (cols >> 7) + 8).astype(F32)
    beta = jax.nn.sigmoid(pt)
    g = -(jnp.exp(a_log_p) * _softplus(pt + dtb_p))
    return mmh(g, e_a), mmh(beta, e_b)


def _gdn_post_fn(o_raw, z, o_norm):
    outs = _head_norm(o_raw, 8, 128)
    o = jnp.concatenate([oh * o_norm for oh in outs], axis=1)
    return (o * _silu(z),)


def _ffn_post(up, gate):
    return _silu(gate) * up


def _shift_down(cur, prev8, sh, row8):
    if sh == 0:
        return cur
    r = pltpu.roll(cur, sh, 0)
    p = pltpu.roll(prev8, sh, 0)
    top = jnp.where(row8 < sh, p, r[0:8])
    if cur.shape[0] == 8:
        return top
    return jnp.concatenate([top, r[8:]], axis=0)


def _shift_up(cur, next8, sh, row8):
    if sh == 0:
        return cur
    ts = cur.shape[0]
    r = pltpu.roll(cur, ts - sh, 0)
    p = pltpu.roll(next8, 8 - sh, 0)
    bot = jnp.where(row8 >= 8 - sh, p, r[ts - 8:])
    return jnp.concatenate([r[:ts - 8], bot], axis=0)


def _conv_rows(cur, prev8, wrows, row8):
    k_w = len(wrows)
    out = None
    for j in range(k_w):
        t = _shift_down(cur, prev8, k_w - 1 - j, row8) * wrows[j]
        out = t if out is None else out + t
    return out


def _conv_specs(x, xoff, w, woff, ts, tc):
    r8 = ts // 8
    return [pl.BlockSpec((ts, tc), functools.partial(lambda i, j, o: (i, j + o), o=xoff)),
            pl.BlockSpec((8, tc), functools.partial(lambda i, j, o: (jnp.maximum(i * r8 - 1, 0), j + o), o=xoff)),
            pl.BlockSpec((w.shape[0], tc), functools.partial(lambda i, j, o: (0, j + o), o=woff))]


def _conv_post(srcs, post, ncol, tc, name, cots=None, ts=256, out_dtype=F32):
    S = srcs[0][0].shape[0]
    ns = len(srcs)
    bwd = cots is not None

    def kern(*refs):
        first = pl.program_id(0) == 0
        row8 = _iota((8, tc), 0)
        cs = []
        for s in range(ns):
            cur_ref, prev_ref, w_ref = refs[3 * s:3 * s + 3]
            prev = jnp.where(first, 0.0, prev_ref[...])
            wrows = [w_ref[j:j + 1, :] for j in range(w_ref.shape[0])]
            cs.append(_conv_rows(cur_ref[...], prev, wrows, row8))
        if bwd:
            g = refs[3 * ns][...]
            _, vjp = jax.vjp(lambda *c: post(*c), *cs)
            for o_ref, d in zip(refs[3 * ns + 1:], vjp(g)):
                o_ref[...] = d
        else:
            refs[3 * ns][...] = post(*cs).astype(refs[3 * ns].dtype)

    in_specs = []
    ops = []
    for (x, xoff, w, woff) in srcs:
        in_specs += _conv_specs(x, xoff, w, woff, ts, tc)
        ops += [x, x, w]
    o_spec = pl.BlockSpec((ts, tc), lambda i, j: (i, j))
    o_shape = jax.ShapeDtypeStruct((S, ncol * tc), F32)
    if bwd:
        in_specs.append(o_spec)
        ops.append(cots)
        out_specs, out_shape = [o_spec] * ns, [o_shape] * ns
    else:
        out_specs, out_shape = o_spec, jax.ShapeDtypeStruct((S, ncol * tc), out_dtype)
    return pl.pallas_call(
        kern, name=name, grid=(S // ts, ncol), in_specs=in_specs, out_specs=out_specs, out_shape=out_shape,
        compiler_params=_cparams(("parallel", "parallel")))(*ops)


def _conv_bwd(dc, x, xoff, w, woff, ncol, tc, name, ts=256):
    S = x.shape[0]
    k_w = w.shape[0]
    r8 = ts // 8
    nblk8 = S // 8
    nrow = S // ts

    def kern(dc_ref, dn_ref, x_ref, xp_ref, w_ref, dx_ref, dw_ref):
        i = pl.program_id(1)
        row8 = _iota((8, tc), 0)
        dcur = dc_ref[...]
        dnext = jnp.where(i == nrow - 1, 0.0, dn_ref[...])
        xcur = x_ref[...]
        xprev = jnp.where(i == 0, 0.0, xp_ref[...])

        @pl.when(i == 0)
        def _():
            dw_ref[...] = jnp.zeros_like(dw_ref)

        dx = None
        for j in range(k_w):
            sh = k_w - 1 - j
            wj = w_ref[j:j + 1, :]
            t = _shift_up(dcur, dnext, sh, row8) * wj
            dx = t if dx is None else dx + t
            dw_ref[j:j + 1, :] += jnp.sum(dcur * _shift_down(xcur, xprev, sh, row8), axis=0, keepdims=True)
        dx_ref[...] = dx.astype(dx_ref.dtype)

    in_specs = [pl.BlockSpec((ts, tc), lambda j, i: (i, j)),
                pl.BlockSpec((8, tc), lambda j, i: (jnp.minimum((i + 1) * r8, nblk8 - 1), j)),
                pl.BlockSpec((ts, tc), functools.partial(lambda j, i, o: (i, j + o), o=xoff)),
                pl.BlockSpec((8, tc), functools.partial(lambda j, i, o: (jnp.maximum(i * r8 - 1, 0), j + o), o=xoff)),
                pl.BlockSpec((k_w, tc), functools.partial(lambda j, i, o: (0, j + o), o=woff))]
    out_specs = [pl.BlockSpec((ts, tc), lambda j, i: (i, j)), pl.BlockSpec((k_w, tc), lambda j, i: (0, j))]
    out_shape = [jax.ShapeDtypeStruct((S, ncol * tc), BF16), jax.ShapeDtypeStruct((k_w, ncol * tc), F32)]
    return pl.pallas_call(
        kern, name=name, grid=(ncol, nrow), in_specs=in_specs, out_specs=out_specs, out_shape=out_shape,
        compiler_params=_cparams(("parallel", "arbitrary")))(dc, dc, x, x, w)


def _conv_post_bwd(srcs, post, ncol, tc, cot, name, ts=256):
    S = srcs[0][0].shape[0]
    ns = len(srcs)
    r8 = ts // 8
    nblk8 = S // 8
    nrow = S // ts

    def kern(*refs):
        i = pl.program_id(1)
        row8 = _iota((8, tc), 0)
        g_ref, gn_ref = refs[4 * ns:4 * ns + 2]
        outs = refs[4 * ns + 2:]
        xs, xps, ws, cs, cns = [], [], [], [], []
        for s in range(ns):
            cur_ref, prev_ref, next_ref, w_ref = refs[4 * s:4 * s + 4]
            xcur = cur_ref[...]
            xprev = jnp.where(i == 0, 0.0, prev_ref[...])
            wrows = [w_ref[j:j + 1, :] for j in range(w_ref.shape[0])]
            xs.append(xcur)
            xps.append(xprev)
            ws.append(wrows)
            cs.append(_conv_rows(xcur, xprev, wrows, row8))
            cns.append(_conv_rows(next_ref[...], xcur[ts - 8:], wrows, row8))
        _, vjp = jax.vjp(lambda *c: post(*c), *cs)
        dcs = vjp(g_ref[...])
        _, vjp_next = jax.vjp(lambda *c: post(*c), *cns)
        dcns = vjp_next(jnp.where(i == nrow - 1, 0.0, gn_ref[...]))
        for s in range(ns):
            dx_ref, dw_ref = outs[2 * s], outs[2 * s + 1]

            @pl.when(i == 0)
            def _(dw_ref=dw_ref):
                dw_ref[...] = jnp.zeros_like(dw_ref)

            k_w = len(ws[s])
            dx = None
            for j in range(k_w):
                sh = k_w - 1 - j
                t = _shift_up(dcs[s], dcns[s], sh, row8) * ws[s][j]
                dx = t if dx is None else dx + t
                dw_ref[j:j + 1, :] += jnp.sum(dcs[s] * _shift_down(xs[s], xps[s], sh, row8), axis=0, keepdims=True)
            dx_ref[...] = dx.astype(dx_ref.dtype)

    def nxt(i):
        return jnp.minimum((i + 1) * r8, nblk8 - 1)

    in_specs, ops = [], []
    for (x, xoff, w, woff) in srcs:
        in_specs += [pl.BlockSpec((ts, tc), functools.partial(lambda j, i, o: (i, j + o), o=xoff)),
                     pl.BlockSpec((8, tc), functools.partial(lambda j, i, o: (jnp.maximum(i * r8 - 1, 0), j + o),
                                                             o=xoff)),
                     pl.BlockSpec((8, tc), functools.partial(lambda j, i, o: (nxt(i), j + o), o=xoff)),
                     pl.BlockSpec((w.shape[0], tc), functools.partial(lambda j, i, o: (0, j + o), o=woff))]
        ops += [x, x, x, w]
    in_specs += [pl.BlockSpec((ts, tc), lambda j, i: (i, j)), pl.BlockSpec((8, tc), lambda j, i: (nxt(i), j))]
    ops += [cot, cot]
    out_specs, out_shape = [], []
    for (x, xoff, w, woff) in srcs:
        out_specs += [pl.BlockSpec((ts, tc), lambda j, i: (i, j)), pl.BlockSpec((w.shape[0], tc), lambda j, i: (0, j))]
        out_shape += [jax.ShapeDtypeStruct((S, ncol * tc), BF16), jax.ShapeDtypeStruct((w.shape[0], ncol * tc), F32)]
    return pl.pallas_call(
        kern, name=name, grid=(ncol, nrow), in_specs=in_specs, out_specs=out_specs, out_shape=out_shape,
        compiler_params=_cparams(("parallel", "arbitrary")))(*ops)


def _ret_tables(S):
    H, C, dh = 4, 128, 128
    lg = jnp.log1p(-jnp.exp2(-5.0 - jnp.arange(H, dtype=F32)))
    idx = jnp.arange(C, dtype=F32)
    diff = idx[:, None] - idx[None, :]
    causal = diff >= 0
    intra = jnp.where(causal, jnp.exp(lg[:, None, None] * jnp.where(causal, diff, 0.0)), 0.0)
    kdec = jnp.broadcast_to(jnp.exp(lg[:, None] * (C - 1 - idx))[:, :, None], (H, C, dh))
    qdec = jnp.broadcast_to(jnp.exp(lg[:, None] * (idx + 1))[:, :, None], (H, C, dh))
    cdec = jnp.broadcast_to(jnp.exp(lg * C)[:, None, None], (H, dh, dh))
    half = dh // 2
    inv = jnp.exp(-math.log(10000.0) * jnp.arange(half, dtype=F32) / half)
    ang = jnp.arange(S).astype(F32)[:, None] * inv[None, :]
    cos, sin = jnp.cos(ang), jnp.sin(ang)
    cosf = jnp.concatenate([cos, cos], axis=1)
    sinf = jnp.concatenate([-sin, sin], axis=1)
    return cosf, sinf, intra, kdec, qdec, cdec


def _ret_chunk(q, k, v, cosf, sinf, intra, kdec, qdec, cdec, state):
    hs = range(len(q))
    qr = [q[h] * cosf + _swap_halves(q[h]) * sinf for h in hs]
    kr = [(k[h] * cosf + _swap_halves(k[h]) * sinf) * (128 ** -0.5) for h in hs]
    scores = [mm_nt(qr[h], kr[h]) * intra[h] for h in hs]
    inner = [mm(scores[h], v[h]) for h in hs]
    kv = [mm_tn(kr[h] * kdec[h], v[h]) for h in hs]
    cross = [mm(qr[h] * qdec[h], state[h]) for h in hs]
    return [inner[h] + cross[h] for h in hs], [state[h] * cdec[h] + kv[h] for h in hs]


RET_H = 4


def _ret_call(proj, tabs, states=None, do=None):
    S = proj.shape[0]
    N = S // 128
    bwd = do is not None

    def nn(n):
        return N - 1 - n if bwd else n

    qkv_spec = pl.BlockSpec((128, 3 * 512), lambda n: (nn(n), 0))
    pos = pl.BlockSpec((128, 128), lambda n: (nn(n), 0))
    tab = pl.BlockSpec((RET_H, 128, 128), lambda n: (0, 0, 0))
    st_spec = pl.BlockSpec((None, RET_H, 128, 128), lambda n: (nn(n), 0, 0, 0))
    o_spec = pl.BlockSpec((128, 512), lambda n: (nn(n), 0))

    def kern(*refs):
        x_ref, c_ref, s_ref, i_ref, kd_ref, qd_ref, cd_ref = refs[:7]
        carry = refs[-1]
        heads = range(RET_H)

        @pl.when(pl.program_id(0) == 0)
        def _():
            carry[...] = jnp.zeros_like(carry)

        def cols(ref, off=0):
            return [ref[:, _hs(off + h)] for h in heads]

        def tabs_of(ref):
            return [ref[h] for h in heads]

        consts = (c_ref[...], s_ref[...], tabs_of(i_ref), tabs_of(kd_ref), tabs_of(qd_ref), tabs_of(cd_ref))
        qkv = (cols(x_ref), cols(x_ref, RET_H), cols(x_ref, 2 * RET_H))
        if bwd:
            sp_ref, do_ref = refs[7:9]
            outs = refs[9:12]
            _, vjp = jax.vjp(lambda q, k, v, s: _ret_chunk(q, k, v, *consts, s), *qkv, tabs_of(sp_ref))
            dq, dk, dv, ds = vjp((cols(do_ref), tabs_of(carry)))
            for h in heads:
                for o_ref, d in zip(outs, (dq[h], dk[h], dv[h])):
                    o_ref[:, _hs(h)] = d.astype(o_ref.dtype)
                carry[h] = ds[h]
        else:
            o_ref, sp_ref = refs[7:9]
            state = tabs_of(carry)
            out, new = _ret_chunk(*qkv, *consts, state)
            for h in heads:
                sp_ref[h] = state[h]
                o_ref[:, _hs(h)] = out[h]
                carry[h] = new[h]

    in_specs = [qkv_spec, pos, pos, tab, tab, tab, tab]
    if bwd:
        in_specs += [st_spec, o_spec]
        out_specs = [o_spec] * 3
        out_shape = [jax.ShapeDtypeStruct((S, 512), BF16)] * 3
        ops = (proj, *tabs, states, do)
    else:
        out_specs = [o_spec, st_spec]
        out_shape = [jax.ShapeDtypeStruct((S, 512), F32), jax.ShapeDtypeStruct((N, RET_H, 128, 128), F32)]
        ops = (proj, *tabs)
    return pl.pallas_call(
        kern, name="ret_bwd" if bwd else "ret_fwd", grid=(N,), in_specs=in_specs, out_specs=out_specs,
        out_shape=out_shape, scratch_shapes=[pltpu.VMEM((RET_H, 128, 128), F32)],
        compiler_params=_cparams(("arbitrary",)))(*ops)


GDN_C = 64
GDN_H = 8


def _unit_lower_inverse(a_mats, eye):
    p = [-a for a in a_mats]
    t = [eye + x for x in p]
    for _ in range(5):
        p = [mm3(x, x) for x in p]
        t = [mm3(y, eye + x) for y, x in zip(t, p)]
    return t


@jax.custom_vjp
def _known_inverse(a_mat, t_mat):
    return t_mat


def _known_inverse_f(a_mat, t_mat):
    return t_mat, t_mat


def _known_inverse_b(t_mat, g):
    return -mmh_tn(t_mat, mmh_nt(g, t_mat)), jnp.zeros_like(t_mat)


_known_inverse.defvjp(_known_inverse_f, _known_inverse_b)


def _gdn_intra(q, k, v, g_b, beta_b, t_known=None):
    c = GDN_C
    hs = range(len(q))
    q = [x * lax.rsqrt(jnp.sum(x * x, axis=-1, keepdims=True) + EPS) * (128 ** -0.5) for x in q]
    k = [x * lax.rsqrt(jnp.sum(x * x, axis=-1, keepdims=True) + EPS) for x in k]
    ri, ci = _iota((c, c), 0), _iota((c, c), 1)
    incl = ri >= ci
    strict = ri > ci
    eye = (ri == ci).astype(F32)
    lower = incl.astype(F32)
    gc_b = [mm3(lower, g) for g in g_b]
    gl_b = [jnp.sum(g, axis=0, keepdims=True) for g in g_b]
    kb = [k[h] * beta_b[h] for h in hs]
    vb = [v[h] * beta_b[h] for h in hs]
    gcc = [g[:, :c] for g in gc_b]
    decay = [jnp.where(incl, jnp.exp(jnp.where(incl, g - g.T, 0.0)), 0.0) for g in gcc]
    a_mat = [jnp.where(strict, mm_nt(kb[h], k[h]) * decay[h], 0.0) for h in hs]
    if t_known is None:
        t_mat = _unit_lower_inverse(a_mat, eye)
    else:
        t_mat = [_known_inverse(a_mat[h], t_known[h]) for h in hs]
    egc = [jnp.exp(g) for g in gc_b]
    w = [mm(t_mat[h], kb[h] * egc[h]) for h in hs]
    u = [mm(t_mat[h], vb[h]) for h in hs]
    qk = [jnp.where(incl, mm_nt(q[h], k[h]) * decay[h], 0.0) for h in hs]
    q_dec = [q[h] * egc[h] for h in hs]
    k_dec = [k[h] * jnp.exp(gl_b[h] - gc_b[h]) for h in hs]
    return w, u, q_dec, k_dec, qk, t_mat


def _gdn_step(w, u, q_dec, k_dec, qk, g_b, state):
    hs = range(len(w))
    gl_s = [jnp.sum(g, axis=0, keepdims=True) for g in g_b]
    ws = [mm(w[h], state[h]) for h in hs]
    qs = [mm(q_dec[h], state[h]) for h in hs]
    v_new = [u[h] - ws[h] for h in hs]
    o = [qs[h] + mm(qk[h], v_new[h]) for h in hs]
    new = [state[h] * jnp.exp(gl_s[h]) + mm_tn(k_dec[h], v_new[h]) for h in hs]
    return o, new


def _hs(h):
    return slice(h * 128, (h + 1) * 128)


def _gdn_intra_call(qkv, g_e, beta_e, cots=None):
    S = qkv.shape[0]
    N = S // GDN_C
    bwd = cots is not None
    row = pl.BlockSpec((GDN_C, 1024), lambda n: (n, 0))
    qkv_spec = pl.BlockSpec((GDN_C, 3072), lambda n: (n, 0))
    qk_spec = pl.BlockSpec((GDN_H, GDN_C, GDN_C), lambda n: (0, n, 0))

    def kern(*refs):
        x_ref, g_ref, b_ref = refs[:3]
        heads = range(GDN_H)

        def cols(ref, off=0):
            return [ref[:, _hs(off + h)] for h in heads]

        args = (cols(x_ref), cols(x_ref, 8), cols(x_ref, 16), cols(g_ref), cols(b_ref))
        if bwd:
            dw_ref, du_ref, dqd_ref, dkd_ref, dqk_ref, dgadd_ref, t_ref = refs[3:10]
            outs = refs[10:]
            t_known = [t_ref[h] for h in heads]
            _, vjp = jax.vjp(lambda *a: _gdn_intra(*a, t_known=t_known)[:5], *args)
            dq, dk, dv, dg, db = vjp((cols(dw_ref), cols(du_ref), cols(dqd_ref), cols(dkd_ref),
                                      [dqk_ref[h] for h in heads]))
            dgadd = cols(dgadd_ref)
            for h in heads:
                for o_ref, d in zip(outs, (dq[h], dk[h], dv[h], dg[h] + dgadd[h], db[h])):
                    o_ref[:, _hs(h)] = d
        else:
            w, u, qd, kd, qk, t_mat = _gdn_intra(*args)
            for h in heads:
                for o_ref, o in zip(refs[3:7], (w[h], u[h], qd[h], kd[h])):
                    o_ref[:, _hs(h)] = o
                refs[7][h] = qk[h]
                refs[8][h] = t_mat[h]

    big = jax.ShapeDtypeStruct((S, 1024), F32)
    sq = jax.ShapeDtypeStruct((GDN_H, S, GDN_C), F32)
    if bwd:
        in_specs = [qkv_spec, row, row, row, row, row, row, qk_spec, row, qk_spec]
        out_specs, out_shape = [row] * 5, [big] * 5
        ops = (qkv, g_e, beta_e) + tuple(cots)
    else:
        in_specs = [qkv_spec, row, row]
        out_specs = [row] * 4 + [qk_spec, qk_spec]
        out_shape = [big] * 4 + [sq, sq]
        ops = (qkv, g_e, beta_e)
    return pl.pallas_call(
        kern, name="gdn_intra_bwd" if bwd else "gdn_intra", grid=(N,), in_specs=in_specs, out_specs=out_specs,
        out_shape=out_shape, compiler_params=_cparams(("parallel",)))(*ops)


def _gdn_pass(w, u, qd, kd, qk, g_e, states=None, do=None):
    S = w.shape[0]
    N = S // GDN_C
    bwd = do is not None

    def nn(n):
        return N - 1 - n if bwd else n

    row = pl.BlockSpec((GDN_C, 1024), lambda n: (nn(n), 0))
    qk_spec = pl.BlockSpec((GDN_H, GDN_C, GDN_C), lambda n: (0, nn(n), 0))
    st_spec = pl.BlockSpec((None, GDN_H, 128, 128), lambda n: (nn(n), 0, 0, 0))

    def kern(*refs):
        w_ref, u_ref, qd_ref, kd_ref, qk_ref, g_ref = refs[:6]
        carry = refs[-1]

        @pl.when(pl.program_id(0) == 0)
        def _():
            carry[...] = jnp.zeros_like(carry)

        heads = range(GDN_H)

        def cols(ref):
            return [ref[:, _hs(h)] for h in heads]

        args = (cols(w_ref), cols(u_ref), cols(qd_ref), cols(kd_ref), [qk_ref[h] for h in heads], cols(g_ref))
        if bwd:
            sp_ref, do_ref = refs[6:8]
            outs = refs[8:14]
            _, vjp = jax.vjp(_gdn_step, *args, [sp_ref[h] for h in heads])
            dw, du, dqd, dkd, dqk, dg, ds = vjp((cols(do_ref), [carry[h] for h in heads]))
            for h in heads:
                for o_ref, d in zip(outs[:4], (dw[h], du[h], dqd[h], dkd[h])):
                    o_ref[:, _hs(h)] = d
                outs[4][h] = dqk[h]
                outs[5][:, _hs(h)] = dg[h]
                carry[h] = ds[h]
        else:
            o_ref, sp_ref = refs[6:8]
            state = [carry[h] for h in heads]
            o, new = _gdn_step(*args, state)
            for h in heads:
                sp_ref[h] = state[h]
                o_ref[:, _hs(h)] = o[h]
                carry[h] = new[h]

    big = jax.ShapeDtypeStruct((S, 1024), F32)
    in_specs = [row, row, row, row, qk_spec, row]
    if bwd:
        in_specs += [st_spec, row]
        out_specs = [row] * 4 + [qk_spec, row]
        out_shape = [big] * 4 + [jax.ShapeDtypeStruct((GDN_H, S, GDN_C), F32), big]
        ops = (w, u, qd, kd, qk, g_e, states, do)
    else:
        out_specs = [row, st_spec]
        out_shape = [big, jax.ShapeDtypeStruct((N, GDN_H, 128, 128), F32)]
        ops = (w, u, qd, kd, qk, g_e)
    return pl.pallas_call(
        kern, name="gdn_pass_bwd" if bwd else "gdn_pass", grid=(N,), in_specs=in_specs, out_specs=out_specs,
        out_shape=out_shape, scratch_shapes=[pltpu.VMEM((GDN_H, 128, 128), F32)],
        compiler_params=_cparams(("arbitrary",)))(*ops)


def _s5_prep_fn(lr, li, ldt, br, bi, cr, ci):
    dt = jnp.exp(ldt)
    mag = jnp.exp(lr * dt)
    a_re = mag * jnp.cos(li * dt)
    a_im = mag * jnp.sin(li * dt)
    den = lr * lr + li * li
    z_re = ((a_re - 1.0) * lr + a_im * li) / den
    z_im = (a_im * lr - (a_re - 1.0) * li) / den
    e1 = ((_iota((512, 32), 0) >> 4) == _iota((512, 32), 1)).astype(F32)
    zr_e = mmh(e1, z_re)
    zi_e = mmh(e1, z_im)
    bb_re = zr_e * br - zi_e * bi
    bb_im = zr_e * bi + zi_e * br
    t1 = ((_iota((64, 2048), 1) & 63) == _iota((64, 2048), 0)).astype(F32)
    m1 = (_iota((512, 2048), 0) >> 4) == (_iota((512, 2048), 1) >> 6)
    bd_re = jnp.where(m1, mmh(bb_re, t1), 0.0)
    bd_im = jnp.where(m1, mmh(bb_im, t1), 0.0)
    t2 = ((_iota((16, 512), 1) & 15) == _iota((16, 512), 0)).astype(F32)
    m2 = (_iota((2048, 512), 0) >> 6) == (_iota((2048, 512), 1) >> 4)
    cd_re = jnp.where(m2, mmh(cr, t2), 0.0)
    cd_im = jnp.where(m2, mmh(ci, t2), 0.0)
    return a_re, a_im, bd_re, bd_im, cd_re, cd_im


_PREP_OUT = [(32, 64), (32, 64), (512, 2048), (512, 2048), (2048, 512), (2048, 512)]


def _s5_prep(params, cots=None):
    bwd = cots is not None

    def kern(*refs):
        vals = [r[...] for r in refs[:7]]
        if bwd:
            gs = tuple(r[...] for r in refs[7:13])
            _, vjp = jax.vjp(_s5_prep_fn, *vals)
            for o_ref, d in zip(refs[13:], vjp(gs)):
                o_ref[...] = d
        else:
            for o_ref, o in zip(refs[7:], _s5_prep_fn(*vals)):
                o_ref[...] = o

    if bwd:
        out_shape = [jax.ShapeDtypeStruct(p.shape, F32) for p in params]
        ops = list(params) + list(cots)
    else:
        out_shape = [jax.ShapeDtypeStruct(s, F32) for s in _PREP_OUT]
        ops = list(params)
    return pl.pallas_call(kern, name="s5_prep_bwd" if bwd else "s5_prep", out_shape=out_shape,
                          compiler_params=_cparams())(*ops)


def _cmul(ar, ai, br, bi):
    return ar * br - ai * bi, ar * bi + ai * br


def _power_table(ar, ai, row8, descending):
    pr, pi = ar, ai
    tr = jnp.zeros(row8.shape, F32)
    ti = jnp.zeros(row8.shape, F32)
    for n in range(8):
        r = 7 - n if descending else n
        tr = jnp.where(row8 == r, pr, tr)
        ti = jnp.where(row8 == r, pi, ti)
        if n < 7:
            pr, pi = _cmul(pr, pi, ar, ai)
    return tr, ti


def _tile_scan(xr, xi, pows, row8, up):
    for d, (pr, pi) in zip((1, 2, 4), pows):
        if up:
            sr = jnp.where(row8 < 8 - d, pltpu.roll(xr, 8 - d, 0), 0.0)
            si = jnp.where(row8 < 8 - d, pltpu.roll(xi, 8 - d, 0), 0.0)
        else:
            sr = jnp.where(row8 >= d, pltpu.roll(xr, d, 0), 0.0)
            si = jnp.where(row8 >= d, pltpu.roll(xi, d, 0), 0.0)
        mr, mi = _cmul(pr, pi, sr, si)
        xr, xi = xr + mr, xi + mi
    return xr, xi


def _pick_row(x, row8, r):
    return jnp.sum(jnp.where(row8 == r, x, 0.0), axis=0, keepdims=True)


SCAN_LB = 512
SCAN_TS = 512


def _scan_fwd(bu_re, bu_im, a_re, a_im):
    S, L = bu_re.shape
    ts, lb = min(SCAN_TS, S), SCAN_LB
    nt = ts // 8

    def kern(br_ref, bi_ref, ar_ref, ai_ref, or_ref, oi_ref, cr_ref, ci_ref):
        @pl.when(pl.program_id(1) == 0)
        def _():
            cr_ref[...] = jnp.zeros_like(cr_ref)
            ci_ref[...] = jnp.zeros_like(ci_ref)

        row8 = _iota((8, lb), 0)
        ar, ai = ar_ref[...], ai_ref[...]
        a2 = _cmul(ar, ai, ar, ai)
        a4 = _cmul(*a2, *a2)
        pows = ((ar, ai), a2, a4)
        tr, ti = _power_table(ar, ai, row8, False)

        def body(i, carry):
            cr, ci = carry
            off = pl.multiple_of(i * 8, 8)
            xr, xi = _tile_scan(br_ref[pl.ds(off, 8), :], bi_ref[pl.ds(off, 8), :], pows, row8, False)
            mr, mi = _cmul(tr, ti, cr, ci)
            xr, xi = xr + mr, xi + mi
            or_ref[pl.ds(off, 8), :] = xr
            oi_ref[pl.ds(off, 8), :] = xi
            return _pick_row(xr, row8, 7), _pick_row(xi, row8, 7)

        cr, ci = lax.fori_loop(0, nt, body, (cr_ref[...], ci_ref[...]))
        cr_ref[...] = cr
        ci_ref[...] = ci

    blk = pl.BlockSpec((ts, lb), lambda j, i: (i, j))
    par = pl.BlockSpec((1, lb), lambda j, i: (0, j))
    return pl.pallas_call(
        kern, name="s5_scan_fwd", grid=(L // lb, S // ts), in_specs=[blk, blk, par, par], out_specs=[blk, blk],
        out_shape=[jax.ShapeDtypeStruct((S, L), F32)] * 2,
        scratch_shapes=[pltpu.VMEM((1, lb), F32), pltpu.VMEM((1, lb), F32)],
        compiler_params=_cparams(("parallel", "arbitrary")))(bu_re, bu_im, a_re, a_im)


def _scan_bwd(dst_re, dst_im, st_re, st_im, a_re, a_im):
    S, L = dst_re.shape
    ts, lb = min(SCAN_TS, S), SCAN_LB
    nt = ts // 8
    nb = S // ts
    r8 = ts // 8

    def kern(dr_ref, di_ref, sr_ref, si_ref, pr_ref, pi_ref, ar_ref, ai_ref, gr_ref, gi_ref, dar_ref, dai_ref,
             cr_ref, ci_ref):
        step = pl.program_id(1)
        blk = nb - 1 - step

        @pl.when(step == 0)
        def _():
            cr_ref[...] = jnp.zeros_like(cr_ref)
            ci_ref[...] = jnp.zeros_like(ci_ref)
            dar_ref[...] = jnp.zeros_like(dar_ref)
            dai_ref[...] = jnp.zeros_like(dai_ref)

        row8 = _iota((8, lb), 0)
        ar, ai = ar_ref[...], ai_ref[...]
        nai = -ai
        a2 = _cmul(ar, nai, ar, nai)
        a4 = _cmul(*a2, *a2)
        pows = ((ar, nai), a2, a4)
        tr, ti = _power_table(ar, nai, row8, True)
        halo_r = jnp.where(blk == 0, 0.0, pr_ref[...])
        halo_i = jnp.where(blk == 0, 0.0, pi_ref[...])

        def body(n, carry):
            cr, ci, acc_r, acc_i = carry
            i = nt - 1 - n
            off = pl.multiple_of(i * 8, 8)
            gr, gi = _tile_scan(dr_ref[pl.ds(off, 8), :], di_ref[pl.ds(off, 8), :], pows, row8, True)
            mr, mi = _cmul(tr, ti, cr, ci)
            gr, gi = gr + mr, gi + mi
            gr_ref[pl.ds(off, 8), :] = gr
            gi_ref[pl.ds(off, 8), :] = gi
            poff = pl.multiple_of(jnp.maximum(i - 1, 0) * 8, 8)
            before_r = jnp.where(i == 0, halo_r, sr_ref[pl.ds(poff, 8), :])
            before_i = jnp.where(i == 0, halo_i, si_ref[pl.ds(poff, 8), :])
            last_r = _pick_row(before_r, row8, 7)
            last_i = _pick_row(before_i, row8, 7)
            spr = jnp.where(row8 >= 1, pltpu.roll(sr_ref[pl.ds(off, 8), :], 1, 0), last_r)
            spi = jnp.where(row8 >= 1, pltpu.roll(si_ref[pl.ds(off, 8), :], 1, 0), last_i)
            acc_r = acc_r + gr * spr + gi * spi
            acc_i = acc_i + gi * spr - gr * spi
            return _pick_row(gr, row8, 0), _pick_row(gi, row8, 0), acc_r, acc_i

        zero = jnp.zeros((8, lb), F32)
        cr, ci, acc_r, acc_i = lax.fori_loop(0, nt, body, (cr_ref[...], ci_ref[...], zero, zero))
        cr_ref[...] = cr
        ci_ref[...] = ci
        dar_ref[...] += jnp.sum(acc_r, axis=0, keepdims=True)
        dai_ref[...] += jnp.sum(acc_i, axis=0, keepdims=True)

    blk = pl.BlockSpec((ts, lb), lambda j, i: (nb - 1 - i, j))
    halo = pl.BlockSpec((8, lb), lambda j, i: (jnp.maximum((nb - 1 - i) * r8 - 1, 0), j))
    par = pl.BlockSpec((1, lb), lambda j, i: (0, j))
    return pl.pallas_call(
        kern, name="s5_scan_bwd", grid=(L // lb, nb), in_specs=[blk, blk, blk, blk, halo, halo, par, par],
        out_specs=[blk, blk, par, par],
        out_shape=[jax.ShapeDtypeStruct((S, L), F32)] * 2 + [jax.ShapeDtypeStruct((1, L), F32)] * 2,
        scratch_shapes=[pltpu.VMEM((1, lb), F32), pltpu.VMEM((1, lb), F32)],
        compiler_params=_cparams(("parallel", "arbitrary")))(dst_re, dst_im, st_re, st_im, st_re, st_im, a_re, a_im)


def _loss_grad(x, target, gain, ts=256):
    S, D = x.shape

    def kern(x_ref, t_ref, g_ref, loss_ref, dx_ref, dg_ref):
        i = pl.program_id(0)
        tgt = t_ref[...]

        def f(xv, gv):
            err = _rms(xv, gv) - tgt
            return 0.5 * jnp.mean(err * err, axis=-1, keepdims=True)

        rowloss, vjp = jax.vjp(f, x_ref[...], g_ref[...])
        dx, dg = vjp(jnp.ones_like(rowloss))
        dx_ref[...] = dx

        @pl.when(i == 0)
        def _():
            loss_ref[...] = jnp.zeros_like(loss_ref)
            dg_ref[...] = jnp.zeros_like(dg_ref)

        loss_ref[...] += jnp.broadcast_to(jnp.sum(rowloss, axis=0, keepdims=True), loss_ref.shape)
        dg_ref[...] += dg

    row = pl.BlockSpec((ts, D), lambda i: (i, 0))
    return pl.pallas_call(
        kern, name="loss_grad", grid=(S // ts,), in_specs=[row, row, pl.BlockSpec((1, D), lambda i: (0, 0))],
        out_specs=[pl.BlockSpec((8, 128), lambda i: (0, 0)), row, pl.BlockSpec((1, D), lambda i: (0, 0))],
        out_shape=[jax.ShapeDtypeStruct((8, 128), F32), jax.ShapeDtypeStruct((S, D), F32),
                   jax.ShapeDtypeStruct((1, D), F32)],
        compiler_params=_cparams(("arbitrary",)))(x, target, gain)


def _rms_fwd(x, g, name):
    return _rowwise(_rms_fn, [_blk(x)], [g], [x.shape[1]], name, out_dtypes=[BF16])[0]


def _rms_bwd(x, g, dy, name, add=None):
    return _rowwise_bwd(_rms_fn, [_blk(x)], [g], [dy], name, adds=None if add is None else {0: add})


FFN_TC = 1408


def _common_fwd(x, mem, P, L):
    hx = _rms_fwd(x, P['xa_norm'], L + "xa_norm")
    q = _matmul(hx, P['xa_wq'], name=L + "xa_q")
    memn = _rms_fwd(mem, P['mem_norm'], L + "mem_norm")
    kv = _matmul(memn, P['xa_wkv'], name=L + "xa_kv")
    att = _rowwise(_xattn_fn, [_blk(q)], [kv], [1024], L + "xattn", out_dtypes=[BF16])[0]
    x2 = _matmul(att, P['xa_wo'], res=x, name=L + "xa_o")
    hf = _rms_fwd(x2, P['ffn_norm'], L + "ffn_norm")
    hu = _matmul(hf, P['ffn_w_up'], name=L + "ffn_up")
    cw = P['ffn_conv']
    act = _conv_post([(hu, 0, cw, 0), (hu, 2, cw, 2)], _ffn_post, 2, FFN_TC, L + "ffn_conv", out_dtype=BF16)
    x3 = _matmul(act, P['ffn_w_down'], res=x2, name=L + "ffn_down")
    return x3, (x, mem, hx, q, memn, kv, att, x2, hf, hu, act)


def _common_bwd(saved, dx3, P, L, midway=None):
    x, mem, hx, q, memn, kv, att, x2, hf, hu, act = saved
    G = {}
    dact = _matmul(dx3, P['ffn_w_down'], "nt", name=L + "ffn_down_dx")
    G['ffn_w_down'] = _matmul(act, dx3, "tn", name=L + "ffn_down_dw")
    cw = P['ffn_conv']
    dhu_u, dcw_u, dhu_g, dcw_g = _conv_post_bwd([(hu, 0, cw, 0), (hu, 2, cw, 2)], _ffn_post, 2, FFN_TC, dact,
                                                L + "ffn_conv_bwd")
    G['ffn_conv'] = jnp.concatenate([dcw_u, dcw_g], axis=1)
    dhf = _matmul_cat([dhu_u, dhu_g], P['ffn_w_up'], "nt", name=L + "ffn_up_dx")
    G['ffn_w_up'] = jnp.concatenate([_matmul(hf, dhu_u, "tn", name=L + "ffn_up_dw_up"),
                                     _matmul(hf, dhu_g, "tn", name=L + "ffn_up_dw_gate")], axis=1)
    dx2, G['ffn_norm'] = _rms_bwd(x2, P['ffn_norm'], dhf, L + "ffn_norm_bwd", add=dx3)
    if midway is not None:
        P = dict(P, xa_wo=P['xa_wo'] + midway(dx2).astype(P['xa_wo'].dtype))
    datt = _matmul(dx2, P['xa_wo'], "nt", name=L + "xa_o_dx")
    G['xa_wo'] = _matmul(att, dx2, "tn", name=L + "xa_o_dw")
    dq, dkv = _rowwise_bwd(_xattn_fn, [_blk(q)], [kv], [datt], L + "xattn_bwd", out_dtypes=[BF16])
    dhx = _matmul(dq, P['xa_wq'], "nt", name=L + "xa_q_dx")
    G['xa_wq'] = _matmul(hx, dq, "tn", name=L + "xa_q_dw")
    dmemn = _matmul(dkv, P['xa_wkv'], "nt", name=L + "xa_kv_dx")
    G['xa_wkv'] = _matmul(memn, dkv, "tn", name=L + "xa_kv_dw")
    _, G['mem_norm'] = _rms_bwd(mem, P['mem_norm'], dmemn, L + "mem_norm_bwd")
    dx, G['xa_norm'] = _rms_bwd(x, P['xa_norm'], dhx, L + "xa_norm_bwd", add=dx2)
    return dx, G


U_COLS = (2048, 512)


def _even_fwd(x, P):
    S = x.shape[0]
    h0 = _rms_fwd(x, P['mix_norm'], "l0_mix_norm")
    proj = _matmul(h0, P['w_in'], name="l0_in")
    tabs = _ret_tables(S)
    o_raw, rstates = _ret_call(proj, tabs)
    o = _rowwise(_ret_post_fn, [_blk(o_raw), _blk(proj, 512, 3)], [P['ret_norm']], [512], "l0_ret_post",
                 out_dtypes=[BF16])[0]
    prep_in = (P['s5_lambda_re'], P['s5_lambda_im'], P['s5_log_dt'], P['s5_b_re'], P['s5_b_im'], P['s5_c_re'],
               P['s5_c_im'])
    a_re, a_im, bd_re, bd_im, cd_re, cd_im = _s5_prep(prep_in)
    a_re_f, a_im_f = a_re.reshape(1, 2048), a_im.reshape(1, 2048)
    bu_re = _matmul(proj, bd_re, name="l0_s5_bu_re", a_cols=U_COLS)
    bu_im = _matmul(proj, bd_im, name="l0_s5_bu_im", a_cols=U_COLS)
    st_re, st_im = _scan_fwd(bu_re, bu_im, a_re_f, a_im_f)
    y1 = _matmul(st_re, cd_re, name="l0_s5_y_re")
    y2 = _matmul(st_im, cd_im, name="l0_s5_y_im")
    yg = _rowwise(_s5_post_fn, [_blk(y1), _blk(y2), _blk(proj, 512, 4)],
                  [P['s5_d'], P['s5_w_glu'], P['s5_b_glu']], [512], "l0_s5_post", out_dtypes=[BF16])[0]
    x1 = _matmul_cat([o, yg], P['w_out'], "nn", res=x, name="l0_out")
    saved = (x, h0, proj, tabs, o_raw, rstates, prep_in, a_re_f, a_im_f, bd_re, bd_im, cd_re, cd_im, st_re, st_im,
             y1, y2, o, yg)
    return x1, saved


def _even_bwd(saved, dx1, P, midway=None):
    (x, h0, proj, tabs, o_raw, rstates, prep_in, a_re_f, a_im_f, bd_re, bd_im, cd_re, cd_im, st_re, st_im, y1, y2,
     o, yg) = saved
    G = {}
    dmerged = _matmul(dx1, P['w_out'], "nt", name="l0_out_dx")
    G['w_out'] = jnp.concatenate([_matmul(o, dx1, "tn", name="l0_out_dw_ret"),
                                  _matmul(yg, dx1, "tn", name="l0_out_dw_s5")], axis=0)
    do_raw, dgate, G['ret_norm'] = _rowwise_bwd(
        _ret_post_fn, [_blk(o_raw), _blk(proj, 512, 3)], [P['ret_norm']], [_blk(dmerged, 512, 0)], "l0_ret_post_bwd",
        out_dtypes=[F32, BF16])
    dq, dk, dv = _ret_call(proj, tabs, states=rstates, do=do_raw)
    if midway is not None:
        P = dict(P, s5_w_glu=P['s5_w_glu'] + midway(dq))
    dy1, dy2, du_a, G['s5_d'], G['s5_w_glu'], G['s5_b_glu'] = _rowwise_bwd(
        _s5_post_fn, [_blk(y1), _blk(y2), _blk(proj, 512, 4)], [P['s5_d'], P['s5_w_glu'], P['s5_b_glu']],
        [_blk(dmerged, 512, 1)], "l0_s5_post_bwd", out_dtypes=[BF16, BF16, F32])
    dst_re = _matmul(dy1, cd_re, "nt", name="l0_s5_y_re_dx")
    dcd_re = _matmul(st_re, dy1, "tn", name="l0_s5_y_re_dw")
    dst_im = _matmul(dy2, cd_im, "nt", name="l0_s5_y_im_dx")
    dcd_im = _matmul(st_im, dy2, "tn", name="l0_s5_y_im_dw")
    dbu_re, dbu_im, da_re, da_im = _scan_bwd(dst_re, dst_im, st_re, st_im, a_re_f, a_im_f)
    du = _matmul(dbu_re, bd_re, "nt", res=du_a, name="l0_s5_bu_re_dx")
    du = _matmul(dbu_im, bd_im, "nt", res=du, name="l0_s5_bu_im_dx", out_dtype=BF16)
    dbd_re = _matmul(proj, dbu_re, "tn", name="l0_s5_bu_re_dw", a_cols=U_COLS)
    dbd_im = _matmul(proj, dbu_im, "tn", name="l0_s5_bu_im_dw", a_cols=U_COLS)
    dprep = _s5_prep(prep_in, cots=(da_re.reshape(32, 64), da_im.reshape(32, 64), dbd_re, dbd_im, dcd_re, dcd_im))
    for n, d in zip(('s5_lambda_re', 's5_lambda_im', 's5_log_dt', 's5_b_re', 's5_b_im', 's5_c_re', 's5_c_im'), dprep):
        G[n] = d
    pieces = [dq, dk, dv, dgate, du]
    dh0 = _matmul_cat(pieces, P['w_in'], "nt", name="l0_in_dx")
    G['w_in'] = jnp.concatenate([_matmul(h0, p, "tn", name="l0_in_dw_%d" % n) for n, p in enumerate(pieces)], axis=1)
    dx, G['mix_norm'] = _rms_bwd(x, P['mix_norm'], dh0, "l0_mix_norm_bwd", add=dx1)
    return dx, G


def _odd_fwd(x, P):
    h1 = _rms_fwd(x, P['mix_norm'], "l1_mix_norm")
    pm = _matmul(h1, P['w_main'], name="l1_in_main")
    pt = _matmul(h1, P['w_tail'], name="l1_in_tail")
    qkv = _conv_post([(pm, 0, P['conv'], 0)], _silu, 3, 1024, "l1_conv")
    g_e, beta_e = _rowwise(_gdn_gates_fn, [_blk(pt)], [P['a_log_p'], P['dtb_p']], [1024, 1024], "l1_gdn_gates")
    w, u, qd, kd, qk, tinv = _gdn_intra_call(qkv, g_e, beta_e)
    o_raw, gstates = _gdn_pass(w, u, qd, kd, qk, g_e)
    og = _rowwise(_gdn_post_fn, [_blk(o_raw), _blk(pm, 1024, 3)], [P['o_norm']], [1024], "l1_gdn_post",
                  out_dtypes=[BF16])[0]
    x1 = _matmul(og, P['w_out'], res=x, name="l1_out")
    return x1, (x, h1, pm, pt, qkv, g_e, beta_e, w, u, qd, kd, qk, tinv, o_raw, gstates, og)


def _odd_bwd(saved, dx1, P):
    x, h1, pm, pt, qkv, g_e, beta_e, w, u, qd, kd, qk, tinv, o_raw, gstates, og = saved
    G = {}
    dog = _matmul(dx1, P['w_out'], "nt", name="l1_out_dx")
    G['w_out'] = _matmul(og, dx1, "tn", name="l1_out_dw")
    do_raw, dz, G['o_norm'] = _rowwise_bwd(_gdn_post_fn, [_blk(o_raw), _blk(pm, 1024, 3)], [P['o_norm']], [dog],
                                           "l1_gdn_post_bwd", out_dtypes=[F32, BF16])
    dw, du, dqd, dkd, dqk, dg_pass = _gdn_pass(w, u, qd, kd, qk, g_e, states=gstates, do=do_raw)
    dqkv = _gdn_intra_call(qkv, g_e, beta_e, cots=(dw, du, dqd, dkd, dqk, dg_pass, tinv))
    dg_e, dbeta_e = dqkv[3], dqkv[4]
    dpt, G['a_log_p'], G['dtb_p'] = _rowwise_bwd(_gdn_gates_fn, [_blk(pt)], [P['a_log_p'], P['dtb_p']],
                                                 [dg_e, dbeta_e], "l1_gdn_gates_bwd", out_dtypes=[BF16])
    pieces, dcw = [], []
    for part in range(3):
        dxp, dwp = _conv_post_bwd([(pm, part, P['conv'], part)], _silu, 1, 1024, dqkv[part],
                                  "l1_conv_bwd_%d" % part)
        pieces.append(dxp)
        dcw.append(dwp)
    G['conv'] = jnp.concatenate(dcw, axis=1)
    pieces += [dz, dpt]
    dh1 = _matmul_cat(pieces, P['w_all'], "nt", name="l1_in_dx")
    G['w_all'] = jnp.concatenate([_matmul(h1, p, "tn", name="l1_in_dw_%d" % n) for n, p in enumerate(pieces)], axis=1)
    dx, G['mix_norm'] = _rms_bwd(x, P['mix_norm'], dh1, "l1_mix_norm_bwd", add=dx1)
    return dx, G


def _row(v):
    return v.reshape(1, -1)


def _local_step(x, mem, target, W, later_weights=None, early_grads=None):
    P0 = {
        'mix_norm': _row(W['l0_mix_norm']), 'w_in': W['l0_w_in'], 'ret_norm': _row(W['l0_ret_norm']),
        's5_lambda_re': W['l0_s5_lambda_re'], 's5_lambda_im': W['l0_s5_lambda_im'],
        's5_log_dt': W['l0_s5_log_dt'].reshape(32, 1),
        's5_b_re': W['l0_s5_b_re'].reshape(512, 64), 's5_b_im': W['l0_s5_b_im'].reshape(512, 64),
        's5_c_re': W['l0_s5_c_re'].reshape(2048, 16), 's5_c_im': W['l0_s5_c_im'].reshape(2048, 16),
        's5_d': _row(W['l0_s5_d']), 's5_w_glu': W['l0_s5_w_glu'].astype(F32), 's5_b_glu': _row(W['l0_s5_b_glu']),
        'w_out': W['l0_w_out'],
    }
    def common(L):
        return {'xa_norm': _row(W[L + 'xa_norm']), 'mem_norm': _row(W[L + 'mem_norm']), 'xa_wq': W[L + 'xa_wq'],
                'xa_wkv': W[L + 'xa_wkv'], 'xa_wo': W[L + 'xa_wo'], 'ffn_norm': _row(W[L + 'ffn_norm']),
                'ffn_w_up': W[L + 'ffn_w_up'], 'ffn_conv': W[L + 'ffn_conv'], 'ffn_w_down': W[L + 'ffn_w_down']}

    x1, s_even = _even_fwd(x, P0)
    if later_weights is not None:
        W = dict(W, **later_weights('l0_common', x1))
    C0 = common('l0_')
    x3, s_c0 = _common_fwd(x1, mem, C0, "l0_")

    if later_weights is not None:
        W = dict(W, **later_weights('l1', x3))
    w_in1 = W['l1_w_in']
    pad8 = jnp.zeros((8,), F32)
    w_all = jnp.pad(w_in1, ((0, 0), (0, 112)))
    P1 = {
        'mix_norm': _row(W['l1_mix_norm']), 'w_main': w_in1[:, :4096], 'w_tail': w_all[:, 4096:], 'w_all': w_all,
        'conv': W['l1_conv'],
        'a_log_p': _row(jnp.concatenate([pad8, W['l1_a_log'], jnp.zeros((112,), F32)])),
        'dtb_p': _row(jnp.concatenate([pad8, W['l1_dt_bias'], jnp.zeros((112,), F32)])),
        'o_norm': _row(W['l1_o_norm']), 'w_out': W['l1_w_out'],
    }
    C1 = common('l1_')
    x4, s_odd = _odd_fwd(x3, P1)
    x6, s_c1 = _common_fwd(x4, mem, C1, "l1_")
    loss_tile, dx6, d_final = _loss_grad(x6, target, _row(W['final_norm']))

    G = {'final_norm': d_final.reshape(-1)}
    dx4, g = _common_bwd(s_c1, dx6, C1, "l1_")
    for k, v in g.items():
        G['l1_' + k] = v
    dx3, g = _odd_bwd(s_odd, dx4, P1)
    G['l1_mix_norm'] = g['mix_norm']
    G['l1_w_in'] = g['w_all'][:, :4112]
    G['l1_conv'] = g['conv']
    G['l1_a_log'] = g['a_log_p'][0, 8:16]
    G['l1_dt_bias'] = g['dtb_p'][0, 8:16]
    G['l1_o_norm'] = g['o_norm']
    G['l1_w_out'] = g['w_out']
    midway = None
    if early_grads is not None:
        zero, midway = early_grads('l1', G)
        C0 = dict(C0, ffn_w_down=C0['ffn_w_down'] + zero.astype(C0['ffn_w_down'].dtype))
    dx1, g = _common_bwd(s_c0, dx3, C0, "l0_", midway=midway)
    for k, v in g.items():
        G['l0_' + k] = v
    if early_grads is not None:
        zero, midway = early_grads('l0_common', G)
        P0 = dict(P0, w_out=P0['w_out'] + zero.astype(P0['w_out'].dtype))
    dx0, g = _even_bwd(s_even, dx1, P0, midway=midway)
    for k, v in g.items():
        G['l0_' + k] = v
    return loss_tile, dx0, G


ANY = pl.BlockSpec(memory_space=pl.ANY)


def _place():
    return lax.axis_index("x"), lax.axis_index("y"), lax.axis_index("c")


def _my_chip():
    return 2 * lax.axis_index("x") + lax.axis_index("y")


def _chip_peers(x, y):
    return [(1 - x, y), (x, 1 - y), (1 - x, 1 - y)]


def _half(ref, mode, shard, j, h, split):
    r, w = shard
    rh = r // 2 if split else r
    h = h if split else 0
    if mode == 'row':
        return ref.at[pl.ds(j * r + h * rh, rh), :]
    if mode == 'col':
        return ref.at[pl.ds(h * rh, rh), pl.ds(j * w, w)]
    return ref.at[j, pl.ds(h * rh, rh), :]


def _place_shard(shard, mode, name):
    r, w = shard.shape
    dtype = BF16 if mode != 'tap' else shard.dtype
    if mode == 'tap':
        mode = 'slab'
    tr = _row_tile(r, w)
    nb = r // tr

    def kern(s_ref, o_ref):
        o_ref[...] = s_ref[...].astype(o_ref.dtype)

    if mode == 'row':
        full, o_spec = (4 * r, w), pl.BlockSpec((tr, w), lambda i: (_my_chip() * nb + i, 0))
    elif mode == 'col':
        full, o_spec = (r, 4 * w), pl.BlockSpec((tr, w), lambda i: (i, _my_chip()))
    else:
        full, o_spec = (4, r, w), pl.BlockSpec((None, tr, w), lambda i: (_my_chip(), i, 0))
    return pl.pallas_call(kern, name=name, grid=(nb,), in_specs=[pl.BlockSpec((tr, w), lambda i: (i, 0))],
                          out_specs=o_spec, out_shape=jax.ShapeDtypeStruct(full, dtype),
                          compiler_params=_cparams(("parallel",)))(shard)


def _gather_placed(fulls, modes, shards, splits):
    n = len(fulls)

    def body(*refs):
        outs = refs[n:2 * n]
        send_sems, recv_sems = refs[2 * n:]
        x, y, c = _place()
        peers = _chip_peers(x, y)
        me = 2 * x + y

        def win(a, j, h):
            return _half(outs[a], modes[a], shards[a], j, h, splits[a])

        def copy(a, k, j, h, to):
            return pltpu.make_async_remote_copy(src_ref=win(a, j, h), dst_ref=win(a, j, h),
                                                send_sem=send_sems.at[6 * a + k], recv_sem=recv_sems.at[6 * a + k],
                                                device_id=to, device_id_type=MESH)

        over_ici = [copy(a, k, me, c, (p[0], p[1], c)) for a in range(n) for k, p in enumerate(peers)]
        for cp in over_ici:
            cp.start()
        passed = []
        for a in range(n):
            for k, p in enumerate(peers):
                j = 2 * p[0] + p[1]
                copy(a, k, j, c, (p[0], p[1], c)).wait_recv()
                if splits[a]:
                    fwd = copy(a, 3 + k, j, c, (x, y, 1 - c))
                    fwd.start()
                    passed.append(fwd)
        for a in range(n):
            if splits[a]:
                for k, p in enumerate(peers):
                    copy(a, 3 + k, 2 * p[0] + p[1], 1 - c, (x, y, 1 - c)).wait_recv()
        for cp in over_ici + passed:
            cp.wait_send()

    return pl.pallas_call(
        body, name="gather_weights", in_specs=[ANY] * n, out_specs=[ANY] * n,
        out_shape=[jax.ShapeDtypeStruct(f.shape, f.dtype) for f in fulls],
        input_output_aliases={a: a for a in range(n)},
        scratch_shapes=[pltpu.SemaphoreType.DMA((6 * n,)), pltpu.SemaphoreType.DMA((6 * n,))],
    )(*fulls)


_FLIPS = [(dx, dy, dc) for dx in (0, 1) for dy in (0, 1) for dc in (0, 1) if (dx, dy, dc) != (0, 0, 0)]


def _send_other_half(gs, small, name):
    n = len(gs)

    def body(*refs):
        ins, outs = refs[:n], refs[n + 1:2 * n + 1]
        small_ref = refs[2 * n + 1]
        send_sems, recv_sems, small_send, small_recv = refs[2 * n + 2:]
        x, y, c = _place()
        me = 4 * x + 2 * y + c

        def peer(f):
            return (x ^ f[0], y ^ f[1], c ^ f[2])

        def small_copy(k, slab, to):
            return pltpu.make_async_remote_copy(src_ref=small_ref.at[slab], dst_ref=small_ref.at[slab],
                                                send_sem=small_send.at[k], recv_sem=small_recv.at[k], device_id=to,
                                                device_id_type=MESH)

        cps = []
        for a in range(n):
            rh = gs[a].shape[1] // 2
            cps.append(pltpu.make_async_remote_copy(
                src_ref=ins[a].at[:, pl.ds((1 - c) * rh, rh), :], dst_ref=outs[a], send_sem=send_sems.at[a],
                recv_sem=recv_sems.at[a], device_id=(x, y, 1 - c), device_id_type=MESH))
        smalls = [small_copy(k, me, peer(f)) for k, f in enumerate(_FLIPS)]
        for cp in cps + smalls:
            cp.start()
        for cp in cps:
            cp.wait()
        for k, f in enumerate(_FLIPS):
            p = peer(f)
            small_copy(k, 4 * p[0] + 2 * p[1] + p[2], p).wait_recv()
        for cp in smalls:
            cp.wait_send()

    outs = pl.pallas_call(
        body, name=name, in_specs=[ANY] * (n + 1), out_specs=[ANY] * (n + 1),
        out_shape=[jax.ShapeDtypeStruct((g.shape[0], g.shape[1] // 2, g.shape[2]), g.dtype) for g in gs]
        + [jax.ShapeDtypeStruct(small.shape, small.dtype)],
        input_output_aliases={n: n},
        scratch_shapes=[pltpu.SemaphoreType.DMA((n,)), pltpu.SemaphoreType.DMA((n,)),
                        pltpu.SemaphoreType.DMA((7,)), pltpu.SemaphoreType.DMA((7,))],
    )(*gs, small)
    return outs[:n], outs[n]


def _send_to_chips(ps, widths):
    n = len(ps)

    def body(*refs):
        ins, outs = refs[:n], refs[n:2 * n]
        send_sems, recv_sems = refs[2 * n:]
        x, y, c = _place()
        peers = _chip_peers(x, y)
        me = 2 * x + y

        def src(a, j):
            if ps[a].shape[0] == 4:
                return ins[a].at[j]
            return ins[a].at[0, :, pl.ds(j * widths[a], widths[a])]

        def copy(a, k, j, dst_slab, to):
            return pltpu.make_async_remote_copy(src_ref=src(a, j), dst_ref=outs[a].at[dst_slab],
                                                send_sem=send_sems.at[3 * a + k], recv_sem=recv_sems.at[3 * a + k],
                                                device_id=(to[0], to[1], c), device_id_type=MESH)

        sends = [copy(a, k, 2 * p[0] + p[1], me, p) for a in range(n) for k, p in enumerate(peers)]
        for cp in sends:
            cp.start()
        for a in range(n):
            for k, p in enumerate(peers):
                copy(a, k, me, 2 * p[0] + p[1], p).wait_recv()
        for cp in sends:
            cp.wait_send()

    return pl.pallas_call(
        body, name="send_to_chips", in_specs=[ANY] * n, out_specs=[ANY] * n,
        out_shape=[jax.ShapeDtypeStruct((4, p.shape[1], w), p.dtype) for p, w in zip(ps, widths)],
        scratch_shapes=[pltpu.SemaphoreType.DMA((3 * n,)), pltpu.SemaphoreType.DMA((3 * n,))],
    )(*ps)


def _share_halves(bufs, name):
    n = len(bufs)

    def body(*refs):
        outs = refs[n:2 * n]
        send_sems, recv_sems = refs[2 * n:]
        x, y, c = _place()
        sends, waits = [], []
        for a in range(n):
            rh = bufs[a].shape[0] // 2
            mine = outs[a].at[pl.ds(c * rh, rh), :]
            other = outs[a].at[pl.ds((1 - c) * rh, rh), :]
            sends.append(pltpu.make_async_remote_copy(src_ref=mine, dst_ref=mine, send_sem=send_sems.at[a],
                                                      recv_sem=recv_sems.at[a], device_id=(x, y, 1 - c),
                                                      device_id_type=MESH))
            waits.append(pltpu.make_async_remote_copy(src_ref=mine, dst_ref=other, send_sem=send_sems.at[a],
                                                      recv_sem=recv_sems.at[a], device_id=(x, y, 1 - c),
                                                      device_id_type=MESH))
        for cp in sends:
            cp.start()
        for cp in waits:
            cp.wait()

    return pl.pallas_call(
        body, name=name, in_specs=[ANY] * n, out_specs=[ANY] * n,
        out_shape=[jax.ShapeDtypeStruct(b.shape, b.dtype) for b in bufs],
        input_output_aliases={a: a for a in range(n)},
        scratch_shapes=[pltpu.SemaphoreType.DMA((n,)), pltpu.SemaphoreType.DMA((n,))],
    )(*bufs)


def _gather_all(mine):
    flips = [(dx, dy, dc) for dx in (0, 1) for dy in (0, 1) for dc in (0, 1) if (dx, dy, dc) != (0, 0, 0)]

    def body(x_ref, out_ref, send_sems, recv_sems, local_sem):
        x, y, c = _place()
        me = 4 * x + 2 * y + c

        def peer(f):
            return (x ^ f[0], y ^ f[1], c ^ f[2])

        def copy(k, slab, to):
            return pltpu.make_async_remote_copy(src_ref=x_ref, dst_ref=out_ref.at[slab], send_sem=send_sems.at[k],
                                                recv_sem=recv_sems.at[k], device_id=to, device_id_type=MESH)

        own = pltpu.make_async_copy(x_ref, out_ref.at[me], local_sem)
        own.start()
        sends = [copy(k, me, peer(f)) for k, f in enumerate(flips)]
        for s in sends:
            s.start()
        for k, f in enumerate(flips):
            p = peer(f)
            copy(k, 4 * p[0] + 2 * p[1] + p[2], p).wait_recv()
        for s in sends:
            s.wait_send()
        own.wait()

    return pl.pallas_call(
        body, name="gather_all", in_specs=[ANY], out_specs=ANY,
        out_shape=jax.ShapeDtypeStruct((8,) + mine.shape, mine.dtype),
        scratch_shapes=[pltpu.SemaphoreType.DMA((7,)), pltpu.SemaphoreType.DMA((7,)), pltpu.SemaphoreType.DMA],
    )(mine)


TILE_BYTES = 2 * 1024 * 1024


def _row_tile(rows, width=1024):
    for t in (512, 352, 256, 176, 128, 64, 32, 16, 8):
        if rows % t == 0 and t * width * 4 <= TILE_BYTES:
            return t
    return rows


def _pair_sum(g, got, name):
    ns, r, w = g.shape
    rh = r // 2
    tr = _row_tile(rh, w)
    nb = rh // tr

    def kern(g_ref, o_ref, out_ref):
        out_ref[...] = (g_ref[...] + o_ref[...]).astype(BF16)

    return pl.pallas_call(
        kern, name=name, grid=(ns, nb),
        in_specs=[pl.BlockSpec((None, tr, w), lambda j, i: (j, lax.axis_index("c") * nb + i, 0)),
                  pl.BlockSpec((None, tr, w), lambda j, i: (j, i, 0))],
        out_specs=pl.BlockSpec((None, tr, w), lambda j, i: (j, i, 0)),
        out_shape=jax.ShapeDtypeStruct((ns, rh, w), BF16),
        compiler_params=_cparams(("parallel", "parallel")))(g, got)


def _chip_sum(pair, recv, w, name):
    rh = pair.shape[1]
    tr = _row_tile(rh, w)
    nb = rh // tr

    def kern(own_ref, r1_ref, r2_ref, r3_ref, out_ref):
        acc = own_ref[...].astype(F32)
        for r_ref in (r1_ref, r2_ref, r3_ref):
            acc = acc + r_ref[...].astype(F32)
        out_ref[...] = acc

    if pair.shape[0] == 4:
        own_spec = pl.BlockSpec((None, tr, w), lambda i: (_my_chip(), i, 0))
    else:
        own_spec = pl.BlockSpec((None, tr, w), lambda i: (0, i, _my_chip()))
    recv_specs = [pl.BlockSpec((None, tr, w), functools.partial(lambda i, d: ((_my_chip() + d) % 4, i, 0), d=d))
                  for d in (1, 2, 3)]
    return pl.pallas_call(
        kern, name=name, grid=(nb,), in_specs=[own_spec] + recv_specs,
        out_specs=pl.BlockSpec((tr, w), lambda i: (lax.axis_index("c") * nb + i, 0)),
        out_shape=jax.ShapeDtypeStruct((2 * rh, w), F32), compiler_params=_cparams(("parallel",)))(pair, recv, recv, recv)


def _slab_sum(slabs, name):
    n, R, w = slabs.shape
    tr = _row_tile(R)

    def kern(s_ref, o_ref):
        acc = s_ref[0].astype(F32)
        for k in range(1, n):
            acc = acc + s_ref[k].astype(F32)
        o_ref[...] = acc

    return pl.pallas_call(
        kern, name=name, grid=(R // tr,), in_specs=[pl.BlockSpec((n, tr, w), lambda i: (0, i, 0))],
        out_specs=pl.BlockSpec((tr, w), lambda i: (i, 0)), out_shape=jax.ShapeDtypeStruct((R, w), F32),
        compiler_params=_cparams(("parallel",)))(slabs)


def _adamw(w, g, m, v, name):
    R, C = w.shape
    tr = _pick(R, (256, 128, 64, 32, 16, 8))

    def kern(w_ref, g_ref, m_ref, v_ref, d_ref, nm_ref, nv_ref, g_out_ref):
        gv = g_ref[...]
        g_out_ref[...] = gv
        m2 = ADAM_B1 * m_ref[...] + (1.0 - ADAM_B1) * gv
        v2 = ADAM_B2 * v_ref[...] + (1.0 - ADAM_B2) * jnp.square(gv)
        m_hat = m2 / (1.0 - ADAM_B1 ** ADAM_STEP)
        v_hat = v2 / (1.0 - ADAM_B2 ** ADAM_STEP)
        d_ref[...] = -ADAM_LR * (m_hat / (jnp.sqrt(v_hat) + ADAM_EPS) + ADAM_WD * w_ref[...])
        nm_ref[...] = m2
        nv_ref[...] = v2

    spec = pl.BlockSpec((tr, C), lambda i: (i, 0))
    return pl.pallas_call(
        kern, name=name, grid=(R // tr,), in_specs=[spec] * 4, out_specs=[spec] * 4,
        out_shape=[jax.ShapeDtypeStruct((R, C), F32)] * 4, compiler_params=_cparams(("parallel",)))(w, g, m, v)


def _pack_small(vals):
    flat = jnp.concatenate([vals[n].astype(F32).reshape(-1) for n in SMALL_NAMES])
    rows = -(-flat.shape[0] // (8 * LANES)) * 8
    return jnp.pad(flat, (0, rows * LANES - flat.shape[0])).reshape(rows, LANES)


def _unpack_small(packed, shapes):
    flat = packed.reshape(-1)
    out = {}
    off = 0
    for n in SMALL_NAMES:
        size = int(np.prod(shapes[n]))
        out[n] = flat[off:off + size].reshape(shapes[n])
        off += size
    return out


HBM = pl.BlockSpec(memory_space=pltpu.HBM)
SEM = pl.BlockSpec(memory_space=pltpu.SEMAPHORE)
DATAFLOW = pltpu.SideEffectType.DATAFLOW_SIDE_EFFECTING


def _in_hbm(a):
    return pltpu.with_memory_space_constraint(a, pltpu.HBM)


def _split_copy_start(srcs, lands, copies, after, name):
    ns, nl = len(srcs), len(lands)
    ncopy = len(copies(list(srcs), list(lands), None, None, probe=True))

    def body(*refs):
        src_refs, land_refs = refs[:ns], refs[ns:ns + nl]
        send_sems, recv_sems = refs[ns + nl + 1:ns + nl + 3]
        token = refs[-1]
        for cp in copies(src_refs, land_refs, send_sems, recv_sems):
            cp.start()
        token[...] = jnp.zeros_like(token)

    outs = pl.pallas_call(
        body, name=name,
        out_shape=(pltpu.SemaphoreType.DMA((ncopy,)), pltpu.SemaphoreType.DMA((ncopy,)),
                   *[pltpu.HBM(a.shape, a.dtype) for a in srcs], *[pltpu.HBM(a.shape, a.dtype) for a in lands],
                   jax.ShapeDtypeStruct((8, 128), F32)),
        in_specs=[HBM] * (ns + nl) + [ANY],
        out_specs=(SEM, SEM, *[HBM] * (ns + nl), pl.BlockSpec(memory_space=pltpu.VMEM)),
        input_output_aliases={i: 2 + i for i in range(ns + nl)},
        compiler_params=pltpu.CompilerParams(has_side_effects=DATAFLOW),
    )(*[_in_hbm(a) for a in srcs], *[_in_hbm(a) for a in lands], after)
    return outs[0], outs[1], outs[2:2 + ns], outs[2 + ns:2 + ns + nl], outs[-1]


def _split_copy_wait(send_sems, recv_sems, srcs, lands, copies, after, name):
    ns, nl = len(srcs), len(lands)

    def body(*refs):
        src_refs, land_refs = refs[:ns], refs[ns:ns + nl]
        send_ref, recv_ref = refs[ns + nl:ns + nl + 2]
        for cp in copies(src_refs, land_refs, send_ref, recv_ref):
            cp.wait_send()
            cp.wait_recv()

    outs = pl.pallas_call(
        body, name=name,
        out_shape=tuple(pltpu.HBM(a.shape, a.dtype) for a in list(srcs) + list(lands)),
        in_specs=[HBM] * (ns + nl) + [SEM, SEM, ANY], out_specs=tuple([HBM] * (ns + nl)),
        input_output_aliases={i: i for i in range(ns + nl)},
        compiler_params=pltpu.CompilerParams(has_side_effects=DATAFLOW),
    )(*srcs, *lands, send_sems, recv_sems, after)
    return outs[:ns], outs[ns:]


def _matrix_mode(n):
    return 'slab' if n == 'l1_w_in' else ('row' if MATRICES[n] == 0 else 'col')


def _placed(A, names):
    modes = ['slab' if n in CONVS else _matrix_mode(n) for n in names]
    fulls = [_place_shard(A[n], 'tap' if n in CONVS else m, "place_" + n) for n, m in zip(names, modes)]
    return fulls, modes


def _assembled(names, modes, outs):
    return {n: jnp.concatenate([o[j] for j in range(4)], axis=1) if m == 'slab' else o
            for n, m, o in zip(names, modes, outs)}


def _gather_weights(A, names):
    fulls, modes = _placed(A, names)
    outs = _gather_placed(fulls, modes, [A[n].shape for n in names], [n not in CONVS for n in names])
    return _assembled(names, modes, outs)


def _whole_shard_copies(modes, shards):
    def copies(src_refs, land_refs, send_sems, recv_sems, probe=False):
        if probe:
            return [None] * (3 * len(land_refs))
        x, y, c = _place()
        me = 2 * x + y
        out = []
        for a, ref in enumerate(land_refs):
            for k, p in enumerate(_chip_peers(x, y)):
                out.append(pltpu.make_async_remote_copy(
                    src_ref=_half(ref, modes[a], shards[a], me, 0, False),
                    dst_ref=_half(ref, modes[a], shards[a], me, 0, False),
                    send_sem=send_sems.at[3 * a + k], recv_sem=recv_sems.at[3 * a + k],
                    device_id=(p[0], p[1], c), device_id_type=MESH))
        return out
    return copies


def _gather_weights_start(A, names, after, tag):
    fulls, modes = _placed(A, names)
    copies = _whole_shard_copies(modes, [A[n].shape for n in names])
    send_sems, recv_sems, _, lands, zeros = _split_copy_start([], fulls, copies, after, "gather_start_" + tag)
    return (send_sems, recv_sems, lands, copies, names, modes), zeros


def _gather_weights_wait(state, after, tag):
    send_sems, recv_sems, lands, copies, names, modes = state
    _, outs = _split_copy_wait(send_sems, recv_sems, [], lands, copies, after, "gather_wait_" + tag)
    return _assembled(names, modes, outs)


def _to_chips_copies(pair_shapes, widths):
    def copies(src_refs, land_refs, send_sems, recv_sems, probe=False):
        if probe:
            return [None] * (3 * len(land_refs))
        x, y, c = _place()
        me = 2 * x + y
        out = []
        for a, (src, land) in enumerate(zip(src_refs, land_refs)):
            for k, p in enumerate(_chip_peers(x, y)):
                j = 2 * p[0] + p[1]
                part = src.at[j] if pair_shapes[a][0] == 4 else src.at[0, :, pl.ds(j * widths[a], widths[a])]
                out.append(pltpu.make_async_remote_copy(
                    src_ref=part, dst_ref=land.at[me], send_sem=send_sems.at[3 * a + k],
                    recv_sem=recv_sems.at[3 * a + k], device_id=(p[0], p[1], c), device_id_type=MESH))
        return out
    return copies


def _slabbed(G, names):
    gs, widths = [], []
    for n in names:
        g = G[n]
        mode = _matrix_mode(n)
        if mode == 'row':
            gs.append(g.reshape(4, g.shape[0] // 4, g.shape[1]))
            widths.append(g.shape[1])
        elif mode == 'col':
            gs.append(g[None])
            widths.append(g.shape[1] // 4)
        else:
            wd = g.shape[1] // 4
            gs.append(jnp.stack([g[:, j * wd:(j + 1) * wd] for j in range(4)]))
            widths.append(wd)
    return gs, widths


def _reduce_begin(G, names, small, tag):
    gs, widths = _slabbed(G, names)
    got, small = _send_other_half(gs, small, "send_other_half_" + tag)
    pairs = [_pair_sum(g, o, "pair_sum_" + n) for n, g, o in zip(names, gs, got)]
    return pairs, widths, small


def _other_half_copies(shapes):
    def copies(src_refs, land_refs, send_sems, recv_sems, probe=False):
        if probe:
            return [None] * len(land_refs)
        x, y, c = _place()
        out = []
        for a, (src, land) in enumerate(zip(src_refs, land_refs)):
            rh = shapes[a][1] // 2
            out.append(pltpu.make_async_remote_copy(
                src_ref=src.at[:, pl.ds((1 - c) * rh, rh), :], dst_ref=land, send_sem=send_sems.at[a],
                recv_sem=recv_sems.at[a], device_id=(x, y, 1 - c), device_id_type=MESH))
        return out
    return copies


def _reduce_end(names, pairs, recv, widths, tag):
    halves = [_chip_sum(p, r, w, "chip_sum_" + n) for n, p, r, w in zip(names, pairs, recv, widths)]
    return dict(zip(names, _share_halves(halves, "share_halves_" + tag)))


def _small_slab(packed):
    me8 = 4 * lax.axis_index("x") + 2 * lax.axis_index("y") + lax.axis_index("c")
    return lax.dynamic_update_slice(jnp.zeros((8,) + packed.shape, F32), packed[None], (me8, 0, 0))


def kernel(*args):
    A = dict(zip(ARG_NAMES, args, strict=True))
    x, mem, target = A['x'][0], A['mem'][0], A['loss_target'][0]

    stages = {'l0_mixer': ['l0_w_in', 'l0_s5_w_glu', 'l0_w_out'],
              'l0_common': [n for n in MATRIX_NAMES if n.startswith(('l0_xa_', 'l0_ffn_'))],
              'l1': [n for n in MATRIX_NAMES if n.startswith('l1_')]}
    W = _gather_weights(A, stages['l0_mixer'] + list(CONVS))
    for n in SMALL_NAMES:
        if n not in CONVS:
            W[n] = A[n]
    flights = {}
    after = W['l0_w_in']
    for stage in ('l0_common', 'l1'):
        flights[stage], after = _gather_weights_start(A, stages[stage], after, stage)
    W['l0_mix_norm'] = W['l0_mix_norm'] + after[0, 0]

    reduce_state = {}

    def early_grads(stage, G):
        names = stages[stage]
        gs, widths = _slabbed(G, names)
        d2d = _other_half_copies([g.shape for g in gs])
        lands = [lax.empty((g.shape[0], g.shape[1] // 2, g.shape[2]), g.dtype) for g in gs]
        d2d_send, d2d_recv, gs, lands, zeros = _split_copy_start(gs, lands, d2d, G['final_norm'], "d2d_start_" + stage)

        def midway(after):
            mine, got = _split_copy_wait(d2d_send, d2d_recv, gs, lands, d2d, after, "d2d_wait_" + stage)
            pairs = [_pair_sum(g, o, "pair_sum_" + n) for n, g, o in zip(names, mine, got)]
            copies = _to_chips_copies([p.shape for p in pairs], widths)
            recv = [lax.empty((4, p.shape[1], w), p.dtype) for p, w in zip(pairs, widths)]
            send_sems, recv_sems, pairs, recv, zeros2 = _split_copy_start(pairs, recv, copies, G['final_norm'],
                                                                          "reduce_start_" + stage)
            reduce_state[stage] = (send_sems, recv_sems, pairs, recv, copies, widths)
            return zeros2[0, 0]

        return zeros[0, 0], midway

    loss_tile, grad_x, G = _local_step(
        x, mem, target, W, later_weights=lambda stage, after: _gather_weights_wait(flights[stage], after, stage),
        early_grads=early_grads)
    loss = lax.psum(loss_tile[0, 0], ("x", "y", "c"))

    g_mat = {}
    for stage in ('l1', 'l0_common'):
        send_sems, recv_sems, pairs, lands, copies, widths = reduce_state[stage]
        sent, recv = _split_copy_wait(send_sems, recv_sems, pairs, lands, copies, grad_x, "reduce_wait_" + stage)
        g_mat.update(_reduce_end(stages[stage], sent, recv, widths, stage))
    pairs, widths, g_small = _reduce_begin(G, stages['l0_mixer'],
                                           _small_slab(_pack_small({n: G[n] for n in SMALL_NAMES})), "l0_mixer")
    g_mat.update(_reduce_end(stages['l0_mixer'], pairs, _send_to_chips(pairs, widths), widths, "l0_mixer"))
    g_small = _unpack_small(_slab_sum(g_small, "sum_small"), {n: G[n].shape for n in SMALL_NAMES})
    me = 2 * lax.axis_index("x") + lax.axis_index("y")
    for n in CONVS:
        wd = A[n].shape[1]
        g_small[n] = lax.dynamic_slice_in_dim(g_small[n], me * wd, wd, axis=1)
    flat_names = [n for n in SMALL_NAMES if n not in CONVS]

    def pack_flat(prefix):
        return _pack_small_flat({n: A[prefix + n] for n in flat_names}, flat_names)

    shapes = {n: A[n].shape for n in flat_names}
    d_s, m_s, v_s, _ = _adamw(pack_flat(''), _pack_small_flat(g_small, flat_names), pack_flat('m_'), pack_flat('v_'),
                              "adamw_small")
    d_s, m_s, v_s = (_unpack_flat(p, shapes, flat_names) for p in (d_s, m_s, v_s))

    grads, deltas, new_m, new_v = {}, {}, {}, {}
    for n in WEIGHTS:
        if n in MATRICES or n in CONVS:
            g = g_mat[n] if n in MATRICES else g_small[n]
            deltas[n], new_m[n], new_v[n], grads[n] = _adamw(A[n], g, A['m_' + n], A['v_' + n], "adamw_" + n)
        else:
            grads[n] = g_small[n].reshape(A[n].shape)
            deltas[n], new_m[n], new_v[n] = d_s[n], m_s[n], v_s[n]
    return (loss, grad_x[None], *[grads[n] for n in WEIGHTS], *[deltas[n] for n in WEIGHTS],
            *[new_m[n] for n in WEIGHTS], *[new_v[n] for n in WEIGHTS])


def _pack_small_flat(vals, names):
    flat = jnp.concatenate([vals[n].astype(F32).reshape(-1) for n in names])
    rows = -(-flat.shape[0] // (8 * LANES)) * 8
    return jnp.pad(flat, (0, rows * LANES - flat.shape[0])).reshape(rows, LANES)


def _unpack_flat(packed, shapes, names):
    flat = packed.reshape(-1)
    out = {}
    off = 0
    for n in names:
        size = int(np.prod(shapes[n]))
        out[n] = flat[off:off + size].reshape(shapes[n])
        off += size
    return out
```

```python
import functools
import math

import numpy as np
import jax
import jax.numpy as jnp
from jax import lax
from jax.experimental import pallas as pl
from jax.experimental.pallas import tpu as pltpu

F32 = jnp.float32
BF16 = jnp.bfloat16
EPS = 1e-6
MESH = pl.DeviceIdType.MESH

ADAM_LR = 0.001
ADAM_B1 = 0.9
ADAM_B2 = 0.999
ADAM_EPS = 1e-08
ADAM_WD = 0.01
ADAM_STEP = 10

VMEM_LIMIT_BYTES = 56 * 1024 * 1024
MATMUL_VMEM_BYTES = 44 * 1024 * 1024
LANES = 1024

WEIGHTS = ['l0_mix_norm', 'l0_w_in', 'l0_ret_norm', 'l0_s5_lambda_re', 'l0_s5_lambda_im', 'l0_s5_b_re', 'l0_s5_b_im',
           'l0_s5_c_re', 'l0_s5_c_im', 'l0_s5_d', 'l0_s5_log_dt', 'l0_s5_w_glu', 'l0_s5_b_glu', 'l0_w_out',
           'l0_xa_norm', 'l0_mem_norm', 'l0_xa_wq', 'l0_xa_wkv', 'l0_xa_wo', 'l0_ffn_norm', 'l0_ffn_w_up',
           'l0_ffn_conv', 'l0_ffn_w_down', 'l1_mix_norm', 'l1_w_in', 'l1_conv', 'l1_a_log', 'l1_dt_bias',
           'l1_o_norm', 'l1_w_out', 'l1_xa_norm', 'l1_mem_norm', 'l1_xa_wq', 'l1_xa_wkv', 'l1_xa_wo',
           'l1_ffn_norm', 'l1_ffn_w_up', 'l1_ffn_conv', 'l1_ffn_w_down', 'final_norm']
ARG_NAMES = (['x', 'mem'] + WEIGHTS + ['loss_target'] + ['m_' + w for w in WEIGHTS] + ['v_' + w for w in WEIGHTS])

MATRICES = {
    'l0_w_in': 1, 'l0_s5_w_glu': 0, 'l0_w_out': 0, 'l0_xa_wq': 0, 'l0_xa_wkv': 1, 'l0_xa_wo': 0, 'l0_ffn_w_up': 1,
    'l0_ffn_w_down': 0, 'l1_w_in': 1, 'l1_w_out': 0, 'l1_xa_wq': 0, 'l1_xa_wkv': 1, 'l1_xa_wo': 0,
    'l1_ffn_w_up': 1, 'l1_ffn_w_down': 0,
}
CONVS = ('l0_ffn_conv', 'l1_conv', 'l1_ffn_conv')
MATRIX_NAMES = [w for w in WEIGHTS if w in MATRICES]
SMALL_NAMES = [w for w in WEIGHTS if w not in MATRICES]


def _cparams(sem=None):
    return pltpu.CompilerParams(dimension_semantics=sem, vmem_limit_bytes=VMEM_LIMIT_BYTES)


def _pick(n, cands):
    for c in cands:
        if n % c == 0:
            return c
    return n


_NN = ((1,), (0,))
_NT = ((1,), (1,))
_TN = ((0,), (0,))


def _dot(a, b, dims, hi):
    if hi is not None:
        return lax.dot_general(a.astype(F32), b.astype(F32), (dims, ((), ())), precision=hi,
                               preferred_element_type=F32)
    return lax.dot_general(a.astype(BF16), b.astype(BF16), (dims, ((), ())), preferred_element_type=F32)


def _make_mm(hi):
    @jax.custom_vjp
    def nn(a, b):
        return _dot(a, b, _NN, hi)

    def nn_f(a, b):
        return nn(a, b), (a, b)

    def nn_b(r, g):
        a, b = r
        return _dot(g, b, _NT, hi), _dot(a, g, _TN, hi)

    nn.defvjp(nn_f, nn_b)

    @jax.custom_vjp
    def nt(a, b):
        return _dot(a, b, _NT, hi)

    def nt_f(a, b):
        return nt(a, b), (a, b)

    def nt_b(r, g):
        a, b = r
        return _dot(g, b, _NN, hi), _dot(g, a, _TN, hi)

    nt.defvjp(nt_f, nt_b)

    @jax.custom_vjp
    def tn(a, b):
        return _dot(a, b, _TN, hi)

    def tn_f(a, b):
        return tn(a, b), (a, b)

    def tn_b(r, g):
        a, b = r
        return _dot(b, g, _NT, hi), _dot(a, g, _NN, hi)

    tn.defvjp(tn_f, tn_b)
    return nn, nt, tn


mm, mm_nt, mm_tn = _make_mm(None)
mmh, mmh_nt, mmh_tn = _make_mm(lax.Precision.HIGHEST)
mm3, _, _ = _make_mm(lax.Precision.HIGH)


@jax.custom_vjp
def _swap_halves(x):
    return pltpu.roll(x, 64, 1)


def _swap_f(x):
    return pltpu.roll(x, 64, 1), None


def _swap_b(_, g):
    return (pltpu.roll(g, 64, 1),)


_swap_halves.defvjp(_swap_f, _swap_b)


def _silu(x):
    return x * jax.nn.sigmoid(x)


def _rms(x, g):
    return x * lax.rsqrt(jnp.mean(x * x, axis=-1, keepdims=True) + EPS) * g


def _iota(shape, dim):
    return lax.broadcasted_iota(jnp.int32, shape, dim)


def _matmul_tiles(M, N, K, a_bytes, b_bytes, has_res, a_off):
    def divisors(n, cands):
        return [c for c in cands if n % c == 0] or [n]

    fallback = None
    for tk in divisors(K, (K, 2048, 1408, 1024, 512, 256, 128)):
        for tm in divisors(M, (1024, 512, 1408, 256, 128)):
            for tn in divisors(N, (1408, 1024, 512, 256, 128)):
                need = 2 * (tm * tk * a_bytes + tk * tn * b_bytes + (tm * tn * 4 if has_res else 0)) + 3 * tm * tn * 4
                if need > MATMUL_VMEM_BYTES or a_off % tk or a_off % tm:
                    continue
                if tm >= 256 and tn >= 256:
                    return tm, tn, tk
                fallback = fallback or (tm, tn, tk)
    return fallback


def _matmul(a, b, mode="nn", res=None, name="mm", a_cols=None, out_dtype=F32):
    a_off, a_w = (0, a.shape[1]) if a_cols is None else a_cols
    if mode == "nn":
        (M, K), (K2, N) = (a.shape[0], a_w), b.shape
    elif mode == "nt":
        (M, K), (N, K2) = (a.shape[0], a_w), b.shape
    else:
        (K, M), (K2, N) = (a.shape[0], a_w), b.shape
    assert K == K2, (a.shape, b.shape, mode)
    tm, tn, tk = _matmul_tiles(M, N, K, 2 if a.dtype == BF16 else 4, 2 if b.dtype == BF16 else 4, res is not None,
                               a_off)
    nk = K // tk
    dims = {"nn": _NN, "nt": _NT, "tn": _TN}[mode]
    ao = a_off // (tm if mode == "tn" else tk)
    assert ao * (tm if mode == "tn" else tk) == a_off
    if mode == "nn":
        a_spec = pl.BlockSpec((tm, tk), lambda i, j, k: (i, k + ao))
        b_spec = pl.BlockSpec((tk, tn), lambda i, j, k: (k, j))
    elif mode == "nt":
        a_spec = pl.BlockSpec((tm, tk), lambda i, j, k: (i, k + ao))
        b_spec = pl.BlockSpec((tn, tk), lambda i, j, k: (j, k))
    else:
        a_spec = pl.BlockSpec((tk, tm), lambda i, j, k: (k, i + ao))
        b_spec = pl.BlockSpec((tk, tn), lambda i, j, k: (k, j))
    o_spec = pl.BlockSpec((tm, tn), lambda i, j, k: (i, j))
    has_res = res is not None

    def kern(*refs):
        a_ref, b_ref = refs[:2]
        r_ref = refs[2] if has_res else None
        o_ref = refs[3] if has_res else refs[2]
        acc_ref = refs[-1] if nk > 1 else None
        k = pl.program_id(2)
        part = lax.dot_general(a_ref[...].astype(BF16), b_ref[...].astype(BF16), (dims, ((), ())),
                               preferred_element_type=F32)
        if nk == 1:
            o_ref[...] = (part + r_ref[...] if has_res else part).astype(o_ref.dtype)
            return

        @pl.when(k == 0)
        def _():
            acc_ref[...] = part

        @pl.when((k > 0) & (k < nk - 1))
        def _():
            acc_ref[...] += part

        @pl.when(k == nk - 1)
        def _():
            total = acc_ref[...] + part
            o_ref[...] = (total + r_ref[...] if has_res else total).astype(o_ref.dtype)

    in_specs = [a_spec, b_spec] + ([o_spec] if has_res else [])
    ops = (a, b) + ((res,) if has_res else ())
    return pl.pallas_call(
        kern, name=name, grid=(M // tm, N // tn, nk), in_specs=in_specs, out_specs=o_spec,
        out_shape=jax.ShapeDtypeStruct((M, N), out_dtype),
        scratch_shapes=[pltpu.VMEM((tm, tn), F32)] if nk > 1 else [],
        compiler_params=_cparams(("parallel", "parallel", "arbitrary")))(*ops)


def _matmul_cat(pieces, b, mode="nn", res=None, name="mmcat"):
    M = pieces[0].shape[0]
    widths = [p.shape[1] for p in pieces]
    K = sum(widths)
    N = b.shape[1] if mode == "nn" else b.shape[0]
    assert (b.shape[0] if mode == "nn" else b.shape[1]) == K
    tn = _pick(N, (1024, 512, 256, 128))
    a_bytes = 2 if pieces[0].dtype == BF16 else 4
    for tm in (1024, 512, 256, 128):
        need = 2 * (tm * K * a_bytes + K * tn * 2 + (tm * tn * 4 if res is not None else 0)) + 3 * tm * tn * 4
        if M % tm == 0 and need <= MATMUL_VMEM_BYTES:
            break
    npc = len(pieces)
    has_res = res is not None
    dims = _NN if mode == "nn" else _NT

    def kern(*refs):
        b_ref = refs[npc]
        o_ref = refs[-1]
        acc = refs[npc + 1][...] if has_res else None
        off = 0
        for p in range(npc):
            bp = b_ref[off:off + widths[p], :] if mode == "nn" else b_ref[:, off:off + widths[p]]
            t = lax.dot_general(refs[p][...].astype(BF16), bp.astype(BF16), (dims, ((), ())),
                                preferred_element_type=F32)
            acc = t if acc is None else acc + t
            off += widths[p]
        o_ref[...] = acc

    in_specs = [pl.BlockSpec((tm, w), lambda j, i: (i, 0)) for w in widths]
    in_specs.append(pl.BlockSpec((K, tn), lambda j, i: (0, j)) if mode == "nn"
                    else pl.BlockSpec((tn, K), lambda j, i: (j, 0)))
    o_spec = pl.BlockSpec((tm, tn), lambda j, i: (i, j))
    if has_res:
        in_specs.append(o_spec)
    ops = list(pieces) + [b] + ([res] if has_res else [])
    return pl.pallas_call(
        kern, name=name, grid=(N // tn, M // tm), in_specs=in_specs, out_specs=o_spec,
        out_shape=jax.ShapeDtypeStruct((M, N), F32), compiler_params=_cparams(("parallel", "parallel")))(*ops)


def _blk(a, width=None, colblk=0):
    return (a, a.shape[1] if width is None else width, colblk)


def _row_specs(blocked, params, ts):
    specs = []
    for (_, w, cb) in blocked:
        specs.append(pl.BlockSpec((ts, w), functools.partial(lambda i, cb: (i, cb), cb=cb)))
    for p in params:
        specs.append(pl.BlockSpec(p.shape, lambda i: (0, 0)))
    return specs


def _rowwise(fn, blocked, params, out_widths, name, ts=256, out_dtypes=None):
    S = blocked[0][0].shape[0]
    ts = min(ts, S)
    nb, npar = len(blocked), len(params)
    out_dtypes = [F32] * len(out_widths) if out_dtypes is None else out_dtypes

    def kern(*refs):
        vals = [r[...] for r in refs[:nb + npar]]
        outs = fn(*vals)
        for o_ref, o in zip(refs[nb + npar:], outs):
            o_ref[...] = o.astype(o_ref.dtype)

    return pl.pallas_call(
        kern, name=name, grid=(S // ts,), in_specs=_row_specs(blocked, params, ts),
        out_specs=[pl.BlockSpec((ts, w), lambda i: (i, 0)) for w in out_widths],
        out_shape=[jax.ShapeDtypeStruct((S, w), d) for w, d in zip(out_widths, out_dtypes)],
        compiler_params=_cparams(("parallel",)))(*[b[0] for b in blocked], *params)


def _rowwise_bwd(fn, blocked, params, cots, name, blocked_grad=None, param_grad=None, adds=None, ts=256,
                 out_dtypes=None):
    S = blocked[0][0].shape[0]
    ts = min(ts, S)
    cots = [c if isinstance(c, tuple) else _blk(c) for c in cots]
    nb, npar, nc = len(blocked), len(params), len(cots)
    blocked_grad = [True] * nb if blocked_grad is None else blocked_grad
    param_grad = [True] * npar if param_grad is None else param_grad
    adds = {} if adds is None else adds
    bidx = [i for i in range(nb) if blocked_grad[i]]
    pidx = [i for i in range(npar) if param_grad[i]]
    add_keys = sorted(adds)
    n_in = nb + npar + nc + len(add_keys)

    def kern(*refs):
        i = pl.program_id(0)
        xs = [r[...] for r in refs[:nb]]
        ps = [r[...] for r in refs[nb:nb + npar]]
        gs = [r[...] for r in refs[nb + npar:nb + npar + nc]]
        add_vals = {k: refs[nb + npar + nc + n][...] for n, k in enumerate(add_keys)}
        outs = refs[n_in:]

        def f(*diff):
            full_x = list(xs)
            full_p = list(ps)
            for n, ix in enumerate(bidx):
                full_x[ix] = diff[n]
            for n, ix in enumerate(pidx):
                full_p[ix] = diff[len(bidx) + n]
            return tuple(fn(*full_x, *full_p))

        _, vjp = jax.vjp(f, *[xs[ix] for ix in bidx], *[ps[ix] for ix in pidx])
        grads = vjp(tuple(gs))
        for n, ix in enumerate(bidx):
            g = grads[n]
            if ix in add_vals:
                g = g + add_vals[ix]
            outs[n][...] = g.astype(outs[n].dtype)
        for n in range(len(pidx)):
            o_ref = outs[len(bidx) + n]

            @pl.when(i == 0)
            def _(o_ref=o_ref):
                o_ref[...] = jnp.zeros_like(o_ref)

            o_ref[...] += grads[len(bidx) + n]

    in_specs = _row_specs(blocked, params, ts)
    in_specs += _row_specs(cots, [], ts)
    in_specs += [pl.BlockSpec((ts, adds[k].shape[1]), lambda i: (i, 0)) for k in add_keys]
    out_specs = [pl.BlockSpec((ts, blocked[ix][1]), lambda i: (i, 0)) for ix in bidx]
    out_specs += [pl.BlockSpec(params[ix].shape, lambda i: (0, 0)) for ix in pidx]
    out_dtypes = [F32] * len(bidx) if out_dtypes is None else out_dtypes
    out_shape = [jax.ShapeDtypeStruct((S, blocked[ix][1]), d) for ix, d in zip(bidx, out_dtypes)]
    out_shape += [jax.ShapeDtypeStruct(params[ix].shape, F32) for ix in pidx]
    return pl.pallas_call(
        kern, name=name, grid=(S // ts,), in_specs=in_specs, out_specs=out_specs, out_shape=out_shape,
        compiler_params=_cparams(("arbitrary",)))(*[b[0] for b in blocked], *params, *[c[0] for c in cots],
                                                    *[adds[k] for k in add_keys])


def _rms_fn(x, g):
    return (_rms(x, g),)


def _head_norm(o, n_heads, dh):
    outs = []
    for h in range(n_heads):
        oh = o[:, h * dh:(h + 1) * dh]
        outs.append(oh * lax.rsqrt(jnp.mean(oh * oh, axis=-1, keepdims=True) + EPS))
    return outs


def _ret_post_fn(o_raw, gate, ret_norm):
    o = jnp.concatenate(_head_norm(o_raw, 4, 128), axis=1)
    return (o * ret_norm * _silu(gate),)


def _s5_post_fn(y1, y2, u, d, w_glu, b_glu):
    y = y1 - y2 + d * u
    y = jax.nn.gelu(y)
    return (y * jax.nn.sigmoid(mm(y, w_glu) + b_glu),)


def _xattn_fn(q, kv):
    outs = []
    for h in range(4):
        qh = q[:, h * 256:(h + 1) * 256]
        kh = kv[:, h * 256:(h + 1) * 256]
        vh = kv[:, 1024 + h * 256:1024 + (h + 1) * 256]
        s = mm_nt(qh, kh) * (256 ** -0.5)
        s = s - lax.stop_gradient(jnp.max(s, axis=-1, keepdims=True))
        p = jnp.exp(s)
        p = p / jnp.sum(p, axis=-1, keepdims=True)
        outs.append(mm(p, vh))
    return (jnp.concatenate(outs, axis=1),)


def _softplus(x):
    return jnp.maximum(x, 0.0) + jnp.log1p(jnp.exp(-jnp.abs(x)))


def _gdn_gates_fn(pt, a_log_p, dtb_p):
    rows, cols = _iota((128, 1024), 0), _iota((128, 1024), 1)
    e_b = (rows == (cols >> 7)).astype(F32)
    e_a = (rows == (cols >> 7) + 8).astype(F32)
    beta = jax.nn.sigmoid(pt)
    g = -(jnp.exp(a_log_p) * _softplus(pt + dtb_p))
    return mmh(g, e_a), mmh(beta, e_b)


def _gdn_post_fn(o_raw, z, o_norm):
    outs = _head_norm(o_raw, 8, 128)
    o = jnp.concatenate([oh * o_norm for oh in outs], axis=1)
    return (o * _silu(z),)


def _ffn_post(up, gate):
    return _silu(gate) * up


def _shift_down(cur, prev8, sh, row8):
    if sh == 0:
        return cur
    r = pltpu.roll(cur, sh, 0)
    p = pltpu.roll(prev8, sh, 0)
    top = jnp.where(row8 < sh, p, r[0:8])
    if cur.shape[0] == 8:
        return top
    return jnp.concatenate([top, r[8:]], axis=0)


def _shift_up(cur, next8, sh, row8):
    if sh == 0:
        return cur
    ts = cur.shape[0]
    r = pltpu.roll(cur, ts - sh, 0)
    p = pltpu.roll(next8, 8 - sh, 0)
    bot = jnp.where(row8 >= 8 - sh, p, r[ts - 8:])
    return jnp.concatenate([r[:ts - 8], bot], axis=0)


def _conv_rows(cur, prev8, wrows, row8):
    k_w = len(wrows)
    out = None
    for j in range(k_w):
        t = _shift_down(cur, prev8, k_w - 1 - j, row8) * wrows[j]
        out = t if out is None else out + t
    return out


def _conv_specs(x, xoff, w, woff, ts, tc):
    r8 = ts // 8
    return [pl.BlockSpec((ts, tc), functools.partial(lambda i, j, o: (i, j + o), o=xoff)),
            pl.BlockSpec((8, tc), functools.partial(lambda i, j, o: (jnp.maximum(i * r8 - 1, 0), j + o), o=xoff)),
            pl.BlockSpec((w.shape[0], tc), functools.partial(lambda i, j, o: (0, j + o), o=woff))]


def _conv_post(srcs, post, ncol, tc, name, cots=None, ts=256, out_dtype=F32):
    S = srcs[0][0].shape[0]
    ns = len(srcs)
    bwd = cots is not None

    def kern(*refs):
        first = pl.program_id(0) == 0
        row8 = _iota((8, tc), 0)
        cs = []
        for s in range(ns):
            cur_ref, prev_ref, w_ref = refs[3 * s:3 * s + 3]
            prev = jnp.where(first, 0.0, prev_ref[...])
            wrows = [w_ref[j:j + 1, :] for j in range(w_ref.shape[0])]
            cs.append(_conv_rows(cur_ref[...], prev, wrows, row8))
        if bwd:
            g = refs[3 * ns][...]
            _, vjp = jax.vjp(lambda *c: post(*c), *cs)
            for o_ref, d in zip(refs[3 * ns + 1:], vjp(g)):
                o_ref[...] = d
        else:
            refs[3 * ns][...] = post(*cs).astype(refs[3 * ns].dtype)

    in_specs = []
    ops = []
    for (x, xoff, w, woff) in srcs:
        in_specs += _conv_specs(x, xoff, w, woff, ts, tc)
        ops += [x, x, w]
    o_spec = pl.BlockSpec((ts, tc), lambda i, j: (i, j))
    o_shape = jax.ShapeDtypeStruct((S, ncol * tc), F32)
    if bwd:
        in_specs.append(o_spec)
        ops.append(cots)
        out_specs, out_shape = [o_spec] * ns, [o_shape] * ns
    else:
        out_specs, out_shape = o_spec, jax.ShapeDtypeStruct((S, ncol * tc), out_dtype)
    return pl.pallas_call(
        kern, name=name, grid=(S // ts, ncol), in_specs=in_specs, out_specs=out_specs, out_shape=out_shape,
        compiler_params=_cparams(("parallel", "parallel")))(*ops)


def _conv_bwd(dc, x, xoff, w, woff, ncol, tc, name, ts=256):
    S = x.shape[0]
    k_w = w.shape[0]
    r8 = ts // 8
    nblk8 = S // 8
    nrow = S // ts

    def kern(dc_ref, dn_ref, x_ref, xp_ref, w_ref, dx_ref, dw_ref):
        i = pl.program_id(1)
        row8 = _iota((8, tc), 0)
        dcur = dc_ref[...]
        dnext = jnp.where(i == nrow - 1, 0.0, dn_ref[...])
        xcur = x_ref[...]
        xprev = jnp.where(i == 0, 0.0, xp_ref[...])

        @pl.when(i == 0)
        def _():
            dw_ref[...] = jnp.zeros_like(dw_ref)

        dx = None
        for j in range(k_w):
            sh = k_w - 1 - j
            wj = w_ref[j:j + 1, :]
            t = _shift_up(dcur, dnext, sh, row8) * wj
            dx = t if dx is None else dx + t
            dw_ref[j:j + 1, :] += jnp.sum(dcur * _shift_down(xcur, xprev, sh, row8), axis=0, keepdims=True)
        dx_ref[...] = dx.astype(dx_ref.dtype)

    in_specs = [pl.BlockSpec((ts, tc), lambda j, i: (i, j)),
                pl.BlockSpec((8, tc), lambda j, i: (jnp.minimum((i + 1) * r8, nblk8 - 1), j)),
                pl.BlockSpec((ts, tc), functools.partial(lambda j, i, o: (i, j + o), o=xoff)),
                pl.BlockSpec((8, tc), functools.partial(lambda j, i, o: (jnp.maximum(i * r8 - 1, 0), j + o), o=xoff)),
                pl.BlockSpec((k_w, tc), functools.partial(lambda j, i, o: (0, j + o), o=woff))]
    out_specs = [pl.BlockSpec((ts, tc), lambda j, i: (i, j)), pl.BlockSpec((k_w, tc), lambda j, i: (0, j))]
    out_shape = [jax.ShapeDtypeStruct((S, ncol * tc), BF16), jax.ShapeDtypeStruct((k_w, ncol * tc), F32)]
    return pl.pallas_call(
        kern, name=name, grid=(ncol, nrow), in_specs=in_specs, out_specs=out_specs, out_shape=out_shape,
        compiler_params=_cparams(("parallel", "arbitrary")))(dc, dc, x, x, w)


def _conv_post_bwd(srcs, post, ncol, tc, cot, name, ts=256):
    S = srcs[0][0].shape[0]
    ns = len(srcs)
    r8 = ts // 8
    nblk8 = S // 8
    nrow = S // ts

    def kern(*refs):
        i = pl.program_id(1)
        row8 = _iota((8, tc), 0)
        g_ref, gn_ref = refs[4 * ns:4 * ns + 2]
        outs = refs[4 * ns + 2:]
        xs, xps, ws, cs, cns = [], [], [], [], []
        for s in range(ns):
            cur_ref, prev_ref, next_ref, w_ref = refs[4 * s:4 * s + 4]
            xcur = cur_ref[...]
            xprev = jnp.where(i == 0, 0.0, prev_ref[...])
            wrows = [w_ref[j:j + 1, :] for j in range(w_ref.shape[0])]
            xs.append(xcur)
            xps.append(xprev)
            ws.append(wrows)
            cs.append(_conv_rows(xcur, xprev, wrows, row8))
            cns.append(_conv_rows(next_ref[...], xcur[ts - 8:], wrows, row8))
        _, vjp = jax.vjp(lambda *c: post(*c), *cs)
        dcs = vjp(g_ref[...])
        _, vjp_next = jax.vjp(lambda *c: post(*c), *cns)
        dcns = vjp_next(jnp.where(i == nrow - 1, 0.0, gn_ref[...]))
        for s in range(ns):
            dx_ref, dw_ref = outs[2 * s], outs[2 * s + 1]

            @pl.when(i == 0)
            def _(dw_ref=dw_ref):
                dw_ref[...] = jnp.zeros_like(dw_ref)

            k_w = len(ws[s])
            dx = None
            for j in range(k_w):
                sh = k_w - 1 - j
                t = _shift_up(dcs[s], dcns[s], sh, row8) * ws[s][j]
                dx = t if dx is None else dx + t
                dw_ref[j:j + 1, :] += jnp.sum(dcs[s] * _shift_down(xs[s], xps[s], sh, row8), axis=0, keepdims=True)
            dx_ref[...] = dx.astype(dx_ref.dtype)

    def nxt(i):
        return jnp.minimum((i + 1) * r8, nblk8 - 1)

    in_specs, ops = [], []
    for (x, xoff, w, woff) in srcs:
        in_specs += [pl.BlockSpec((ts, tc), functools.partial(lambda j, i, o: (i, j + o), o=xoff)),
                     pl.BlockSpec((8, tc), functools.partial(lambda j, i, o: (jnp.maximum(i * r8 - 1, 0), j + o),
                                                             o=xoff)),
                     pl.BlockSpec((8, tc), functools.partial(lambda j, i, o: (nxt(i), j + o), o=xoff)),
                     pl.BlockSpec((w.shape[0], tc), functools.partial(lambda j, i, o: (0, j + o), o=woff))]
        ops += [x, x, x, w]
    in_specs += [pl.BlockSpec((ts, tc), lambda j, i: (i, j)), pl.BlockSpec((8, tc), lambda j, i: (nxt(i), j))]
    ops += [cot, cot]
    out_specs, out_shape = [], []
    for (x, xoff, w, woff) in srcs:
        out_specs += [pl.BlockSpec((ts, tc), lambda j, i: (i, j)), pl.BlockSpec((w.shape[0], tc), lambda j, i: (0, j))]
        out_shape += [jax.ShapeDtypeStruct((S, ncol * tc), BF16), jax.ShapeDtypeStruct((w.shape[0], ncol * tc), F32)]
    return pl.pallas_call(
        kern, name=name, grid=(ncol, nrow), in_specs=in_specs, out_specs=out_specs, out_shape=out_shape,
        compiler_params=_cparams(("parallel", "arbitrary")))(*ops)


def _ret_tables(S):
    H, C, dh = 4, 128, 128
    lg = jnp.log1p(-jnp.exp2(-5.0 - jnp.arange(H, dtype=F32)))
    idx = jnp.arange(C, dtype=F32)
    diff = idx[:, None] - idx[None, :]
    causal = diff >= 0
    intra = jnp.where(causal, jnp.exp(lg[:, None, None] * jnp.where(causal, diff, 0.0)), 0.0)
    kdec = jnp.broadcast_to(jnp.exp(lg[:, None] * (C - 1 - idx))[:, :, None], (H, C, dh))
    qdec = jnp.broadcast_to(jnp.exp(lg[:, None] * (idx + 1))[:, :, None], (H, C, dh))
    cdec = jnp.broadcast_to(jnp.exp(lg * C)[:, None, None], (H, dh, dh))
    half = dh // 2
    inv = jnp.exp(-math.log(10000.0) * jnp.arange(half, dtype=F32) / half)
    ang = jnp.arange(S).astype(F32)[:, None] * inv[None, :]
    cos, sin = jnp.cos(ang), jnp.sin(ang)
    cosf = jnp.concatenate([cos, cos], axis=1)
    sinf = jnp.concatenate([-sin, sin], axis=1)
    return cosf, sinf, intra, kdec, qdec, cdec


def _ret_chunk(q, k, v, cosf, sinf, intra, kdec, qdec, cdec, state):
    hs = range(len(q))
    qr = [q[h] * cosf + _swap_halves(q[h]) * sinf for h in hs]
    kr = [(k[h] * cosf + _swap_halves(k[h]) * sinf) * (128 ** -0.5) for h in hs]
    scores = [mm_nt(qr[h], kr[h]) * intra[h] for h in hs]
    inner = [mm(scores[h], v[h]) for h in hs]
    kv = [mm_tn(kr[h] * kdec[h], v[h]) for h in hs]
    cross = [mm(qr[h] * qdec[h], state[h]) for h in hs]
    return [inner[h] + cross[h] for h in hs], [state[h] * cdec[h] + kv[h] for h in hs]


RET_H = 4


def _ret_call(proj, tabs, states=None, do=None):
    S = proj.shape[0]
    N = S // 128
    bwd = do is not None

    def nn(n):
        return N - 1 - n if bwd else n

    qkv_spec = pl.BlockSpec((128, 3 * 512), lambda n: (nn(n), 0))
    pos = pl.BlockSpec((128, 128), lambda n: (nn(n), 0))
    tab = pl.BlockSpec((RET_H, 128, 128), lambda n: (0, 0, 0))
    st_spec = pl.BlockSpec((None, RET_H, 128, 128), lambda n: (nn(n), 0, 0, 0))
    o_spec = pl.BlockSpec((128, 512), lambda n: (nn(n), 0))

    def kern(*refs):
        x_ref, c_ref, s_ref, i_ref, kd_ref, qd_ref, cd_ref = refs[:7]
        carry = refs[-1]
        heads = range(RET_H)

        @pl.when(pl.program_id(0) == 0)
        def _():
            carry[...] = jnp.zeros_like(carry)

        def cols(ref, off=0):
            return [ref[:, _hs(off + h)] for h in heads]

        def tabs_of(ref):
            return [ref[h] for h in heads]

        consts = (c_ref[...], s_ref[...], tabs_of(i_ref), tabs_of(kd_ref), tabs_of(qd_ref), tabs_of(cd_ref))
        qkv = (cols(x_ref), cols(x_ref, RET_H), cols(x_ref, 2 * RET_H))
        if bwd:
            sp_ref, do_ref = refs[7:9]
            outs = refs[9:12]
            _, vjp = jax.vjp(lambda q, k, v, s: _ret_chunk(q, k, v, *consts, s), *qkv, tabs_of(sp_ref))
            dq, dk, dv, ds = vjp((cols(do_ref), tabs_of(carry)))
            for h in heads:
                for o_ref, d in zip(outs, (dq[h], dk[h], dv[h])):
                    o_ref[:, _hs(h)] = d.astype(o_ref.dtype)
                carry[h] = ds[h]
        else:
            o_ref, sp_ref = refs[7:9]
            state = tabs_of(carry)
            out, new = _ret_chunk(*qkv, *consts, state)
            for h in heads:
                sp_ref[h] = state[h]
                o_ref[:, _hs(h)] = out[h]
                carry[h] = new[h]

    in_specs = [qkv_spec, pos, pos, tab, tab, tab, tab]
    if bwd:
        in_specs += [st_spec, o_spec]
        out_specs = [o_spec] * 3
        out_shape = [jax.ShapeDtypeStruct((S, 512), BF16)] * 3
        ops = (proj, *tabs, states, do)
    else:
        out_specs = [o_spec, st_spec]
        out_shape = [jax.ShapeDtypeStruct((S, 512), F32), jax.ShapeDtypeStruct((N, RET_H, 128, 128), F32)]
        ops = (proj, *tabs)
    return pl.pallas_call(
        kern, name="ret_bwd" if bwd else "ret_fwd", grid=(N,), in_specs=in_specs, out_specs=out_specs,
        out_shape=out_shape, scratch_shapes=[pltpu.VMEM((RET_H, 128, 128), F32)],
        compiler_params=_cparams(("arbitrary",)))(*ops)


GDN_C = 64
GDN_H = 8


def _unit_lower_inverse(a_mats, eye):
    p = [-a for a in a_mats]
    t = [eye + x for x in p]
    for _ in range(5):
        p = [mm3(x, x) for x in p]
        t = [mm3(y, eye + x) for y, x in zip(t, p)]
    return t


@jax.custom_vjp
def _known_inverse(a_mat, t_mat):
    return t_mat


def _known_inverse_f(a_mat, t_mat):
    return t_mat, t_mat


def _known_inverse_b(t_mat, g):
    return -mmh_tn(t_mat, mmh_nt(g, t_mat)), jnp.zeros_like(t_mat)


_known_inverse.defvjp(_known_inverse_f, _known_inverse_b)


def _gdn_intra(q, k, v, g_b, beta_b, t_known=None):
    c = GDN_C
    hs = range(len(q))
    q = [x * lax.rsqrt(jnp.sum(x * x, axis=-1, keepdims=True) + EPS) * (128 ** -0.5) for x in q]
    k = [x * lax.rsqrt(jnp.sum(x * x, axis=-1, keepdims=True) + EPS) for x in k]
    ri, ci = _iota((c, c), 0), _iota((c, c), 1)
    incl = ri >= ci
    strict = ri > ci
    eye = (ri == ci).astype(F32)
    lower = incl.astype(F32)
    gc_b = [mm3(lower, g) for g in g_b]
    gl_b = [jnp.sum(g, axis=0, keepdims=True) for g in g_b]
    kb = [k[h] * beta_b[h] for h in hs]
    vb = [v[h] * beta_b[h] for h in hs]
    gcc = [g[:, :c] for g in gc_b]
    decay = [jnp.where(incl, jnp.exp(jnp.where(incl, g - g.T, 0.0)), 0.0) for g in gcc]
    a_mat = [jnp.where(strict, mm_nt(kb[h], k[h]) * decay[h], 0.0) for h in hs]
    if t_known is None:
        t_mat = _unit_lower_inverse(a_mat, eye)
    else:
        t_mat = [_known_inverse(a_mat[h], t_known[h]) for h in hs]
    egc = [jnp.exp(g) for g in gc_b]
    w = [mm(t_mat[h], kb[h] * egc[h]) for h in hs]
    u = [mm(t_mat[h], vb[h]) for h in hs]
    qk = [jnp.where(incl, mm_nt(q[h], k[h]) * decay[h], 0.0) for h in hs]
    q_dec = [q[h] * egc[h] for h in hs]
    k_dec = [k[h] * jnp.exp(gl_b[h] - gc_b[h]) for h in hs]
    return w, u, q_dec, k_dec, qk, t_mat


def _gdn_step(w, u, q_dec, k_dec, qk, g_b, state):
    hs = range(len(w))
    gl_s = [jnp.sum(g, axis=0, keepdims=True) for g in g_b]
    ws = [mm(w[h], state[h]) for h in hs]
    qs = [mm(q_dec[h], state[h]) for h in hs]
    v_new = [u[h] - ws[h] for h in hs]
    o = [qs[h] + mm(qk[h], v_new[h]) for h in hs]
    new = [state[h] * jnp.exp(gl_s[h]) + mm_tn(k_dec[h], v_new[h]) for h in hs]
    return o, new


def _hs(h):
    return slice(h * 128, (h + 1) * 128)


def _gdn_intra_call(qkv, g_e, beta_e, cots=None):
    S = qkv.shape[0]
    N = S // GDN_C
    bwd = cots is not None
    row = pl.BlockSpec((GDN_C, 1024), lambda n: (n, 0))
    qkv_spec = pl.BlockSpec((GDN_C, 3072), lambda n: (n, 0))
    qk_spec = pl.BlockSpec((GDN_H, GDN_C, GDN_C), lambda n: (0, n, 0))

    def kern(*refs):
        x_ref, g_ref, b_ref = refs[:3]
        heads = range(GDN_H)

        def cols(ref, off=0):
            return [ref[:, _hs(off + h)] for h in heads]

        args = (cols(x_ref), cols(x_ref, 8), cols(x_ref, 16), cols(g_ref), cols(b_ref))
        if bwd:
            dw_ref, du_ref, dqd_ref, dkd_ref, dqk_ref, dgadd_ref, t_ref = refs[3:10]
            outs = refs[10:]
            t_known = [t_ref[h] for h in heads]
            _, vjp = jax.vjp(lambda *a: _gdn_intra(*a, t_known=t_known)[:5], *args)
            dq, dk, dv, dg, db = vjp((cols(dw_ref), cols(du_ref), cols(dqd_ref), cols(dkd_ref),
                                      [dqk_ref[h] for h in heads]))
            dgadd = cols(dgadd_ref)
            for h in heads:
                for o_ref, d in zip(outs, (dq[h], dk[h], dv[h], dg[h] + dgadd[h], db[h])):
                    o_ref[:, _hs(h)] = d
        else:
            w, u, qd, kd, qk, t_mat = _gdn_intra(*args)
            for h in heads:
                for o_ref, o in zip(refs[3:7], (w[h], u[h], qd[h], kd[h])):
                    o_ref[:, _hs(h)] = o
                refs[7][h] = qk[h]
                refs[8][h] = t_mat[h]

    big = jax.ShapeDtypeStruct((S, 1024), F32)
    sq = jax.ShapeDtypeStruct((GDN_H, S, GDN_C), F32)
    if bwd:
        in_specs = [qkv_spec, row, row, row, row, row, row, qk_spec, row, qk_spec]
        out_specs, out_shape = [row] * 5, [big] * 5
        ops = (qkv, g_e, beta_e) + tuple(cots)
    else:
        in_specs = [qkv_spec, row, row]
        out_specs = [row] * 4 + [qk_spec, qk_spec]
        out_shape = [big] * 4 + [sq, sq]
        ops = (qkv, g_e, beta_e)
    return pl.pallas_call(
        kern, name="gdn_intra_bwd" if bwd else "gdn_intra", grid=(N,), in_specs=in_specs, out_specs=out_specs,
        out_shape=out_shape, compiler_params=_cparams(("parallel",)))(*ops)


def _gdn_pass(w, u, qd, kd, qk, g_e, states=None, do=None):
    S = w.shape[0]
    N = S // GDN_C
    bwd = do is not None

    def nn(n):
        return N - 1 - n if bwd else n

    row = pl.BlockSpec((GDN_C, 1024), lambda n: (nn(n), 0))
    qk_spec = pl.BlockSpec((GDN_H, GDN_C, GDN_C), lambda n: (0, nn(n), 0))
    st_spec = pl.BlockSpec((None, GDN_H, 128, 128), lambda n: (nn(n), 0, 0, 0))

    def kern(*refs):
        w_ref, u_ref, qd_ref, kd_ref, qk_ref, g_ref = refs[:6]
        carry = refs[-1]

        @pl.when(pl.program_id(0) == 0)
        def _():
            carry[...] = jnp.zeros_like(carry)

        heads = range(GDN_H)

        def cols(ref):
            return [ref[:, _hs(h)] for h in heads]

        args = (cols(w_ref), cols(u_ref), cols(qd_ref), cols(kd_ref), [qk_ref[h] for h in heads], cols(g_ref))
        if bwd:
            sp_ref, do_ref = refs[6:8]
            outs = refs[8:14]
            _, vjp = jax.vjp(_gdn_step, *args, [sp_ref[h] for h in heads])
            dw, du, dqd, dkd, dqk, dg, ds = vjp((cols(do_ref), [carry[h] for h in heads]))
            for h in heads:
                for o_ref, d in zip(outs[:4], (dw[h], du[h], dqd[h], dkd[h])):
                    o_ref[:, _hs(h)] = d
                outs[4][h] = dqk[h]
                outs[5][:, _hs(h)] = dg[h]
                carry[h] = ds[h]
        else:
            o_ref, sp_ref = refs[6:8]
            state = [carry[h] for h in heads]
            o, new = _gdn_step(*args, state)
            for h in heads:
                sp_ref[h] = state[h]
                o_ref[:, _hs(h)] = o[h]
                carry[h] = new[h]

    big = jax.ShapeDtypeStruct((S, 1024), F32)
    in_specs = [row, row, row, row, qk_spec, row]
    if bwd:
        in_specs += [st_spec, row]
        out_specs = [row] * 4 + [qk_spec, row]
        out_shape = [big] * 4 + [jax.ShapeDtypeStruct((GDN_H, S, GDN_C), F32), big]
        ops = (w, u, qd, kd, qk, g_e, states, do)
    else:
        out_specs = [row, st_spec]
        out_shape = [big, jax.ShapeDtypeStruct((N, GDN_H, 128, 128), F32)]
        ops = (w, u, qd, kd, qk, g_e)
    return pl.pallas_call(
        kern, name="gdn_pass_bwd" if bwd else "gdn_pass", grid=(N,), in_specs=in_specs, out_specs=out_specs,
        out_shape=out_shape, scratch_shapes=[pltpu.VMEM((GDN_H, 128, 128), F32)],
        compiler_params=_cparams(("arbitrary",)))(*ops)


def _s5_prep_fn(lr, li, ldt, br, bi, cr, ci):
    dt = jnp.exp(ldt)
    mag = jnp.exp(lr * dt)
    a_re = mag * jnp.cos(li * dt)
    a_im = mag * jnp.sin(li * dt)
    den = lr * lr + li * li
    z_re = ((a_re - 1.0) * lr + a_im * li) / den
    z_im = (a_im * lr - (a_re - 1.0) * li) / den
    e1 = ((_iota((512, 32), 0) >> 4) == _iota((512, 32), 1)).astype(F32)
    zr_e = mmh(e1, z_re)
    zi_e = mmh(e1, z_im)
    bb_re = zr_e * br - zi_e * bi
    bb_im = zr_e * bi + zi_e * br
    t1 = ((_iota((64, 2048), 1) & 63) == _iota((64, 2048), 0)).astype(F32)
    m1 = (_iota((512, 2048), 0) >> 4) == (_iota((512, 2048), 1) >> 6)
    bd_re = jnp.where(m1, mmh(bb_re, t1), 0.0)
    bd_im = jnp.where(m1, mmh(bb_im, t1), 0.0)
    t2 = ((_iota((16, 512), 1) & 15) == _iota((16, 512), 0)).astype(F32)
    m2 = (_iota((2048, 512), 0) >> 6) == (_iota((2048, 512), 1) >> 4)
    cd_re = jnp.where(m2, mmh(cr, t2), 0.0)
    cd_im = jnp.where(m2, mmh(ci, t2), 0.0)
    return a_re, a_im, bd_re, bd_im, cd_re, cd_im


_PREP_OUT = [(32, 64), (32, 64), (512, 2048), (512, 2048), (2048, 512), (2048, 512)]


def _s5_prep(params, cots=None):
    bwd = cots is not None

    def kern(*refs):
        vals = [r[...] for r in refs[:7]]
        if bwd:
            gs = tuple(r[...] for r in refs[7:13])
            _, vjp = jax.vjp(_s5_prep_fn, *vals)
            for o_ref, d in zip(refs[13:], vjp(gs)):
                o_ref[...] = d
        else:
            for o_ref, o in zip(refs[7:], _s5_prep_fn(*vals)):
                o_ref[...] = o

    if bwd:
        out_shape = [jax.ShapeDtypeStruct(p.shape, F32) for p in params]
        ops = list(params) + list(cots)
    else:
        out_shape = [jax.ShapeDtypeStruct(s, F32) for s in _PREP_OUT]
        ops = list(params)
    return pl.pallas_call(kern, name="s5_prep_bwd" if bwd else "s5_prep", out_shape=out_shape,
                          compiler_params=_cparams())(*ops)


def _cmul(ar, ai, br, bi):
    return ar * br - ai * bi, ar * bi + ai * br


def _power_table(ar, ai, row8, descending):
    pr, pi = ar, ai
    tr = jnp.zeros(row8.shape, F32)
    ti = jnp.zeros(row8.shape, F32)
    for n in range(8):
        r = 7 - n if descending else n
        tr = jnp.where(row8 == r, pr, tr)
        ti = jnp.where(row8 == r, pi, ti)
        if n < 7:
            pr, pi = _cmul(pr, pi, ar, ai)
    return tr, ti


def _tile_scan(xr, xi, pows, row8, up):
    for d, (pr, pi) in zip((1, 2, 4), pows):
        if up:
            sr = jnp.where(row8 < 8 - d, pltpu.roll(xr, 8 - d, 0), 0.0)
            si = jnp.where(row8 < 8 - d, pltpu.roll(xi, 8 - d, 0), 0.0)
        else:
            sr = jnp.where(row8 >= d, pltpu.roll(xr, d, 0), 0.0)
            si = jnp.where(row8 >= d, pltpu.roll(xi, d, 0), 0.0)
        mr, mi = _cmul(pr, pi, sr, si)
        xr, xi = xr + mr, xi + mi
    return xr, xi


def _pick_row(x, row8, r):
    return jnp.sum(jnp.where(row8 == r, x, 0.0), axis=0, keepdims=True)


SCAN_LB = 512
SCAN_TS = 512


def _scan_fwd(bu_re, bu_im, a_re, a_im):
    S, L = bu_re.shape
    ts, lb = min(SCAN_TS, S), SCAN_LB
    nt = ts // 8

    def kern(br_ref, bi_ref, ar_ref, ai_ref, or_ref, oi_ref, cr_ref, ci_ref):
        @pl.when(pl.program_id(1) == 0)
        def _():
            cr_ref[...] = jnp.zeros_like(cr_ref)
            ci_ref[...] = jnp.zeros_like(ci_ref)

        row8 = _iota((8, lb), 0)
        ar, ai = ar_ref[...], ai_ref[...]
        a2 = _cmul(ar, ai, ar, ai)
        a4 = _cmul(*a2, *a2)
        pows = ((ar, ai), a2, a4)
        tr, ti = _power_table(ar, ai, row8, False)

        def body(i, carry):
            cr, ci = carry
            off = pl.multiple_of(i * 8, 8)
            xr, xi = _tile_scan(br_ref[pl.ds(off, 8), :], bi_ref[pl.ds(off, 8), :], pows, row8, False)
            mr, mi = _cmul(tr, ti, cr, ci)
            xr, xi = xr + mr, xi + mi
            or_ref[pl.ds(off, 8), :] = xr
            oi_ref[pl.ds(off, 8), :] = xi
            return _pick_row(xr, row8, 7), _pick_row(xi, row8, 7)

        cr, ci = lax.fori_loop(0, nt, body, (cr_ref[...], ci_ref[...]))
        cr_ref[...] = cr
        ci_ref[...] = ci

    blk = pl.BlockSpec((ts, lb), lambda j, i: (i, j))
    par = pl.BlockSpec((1, lb), lambda j, i: (0, j))
    return pl.pallas_call(
        kern, name="s5_scan_fwd", grid=(L // lb, S // ts), in_specs=[blk, blk, par, par], out_specs=[blk, blk],
        out_shape=[jax.ShapeDtypeStruct((S, L), F32)] * 2,
        scratch_shapes=[pltpu.VMEM((1, lb), F32), pltpu.VMEM((1, lb), F32)],
        compiler_params=_cparams(("parallel", "arbitrary")))(bu_re, bu_im, a_re, a_im)


def _scan_bwd(dst_re, dst_im, st_re, st_im, a_re, a_im):
    S, L = dst_re.shape
    ts, lb = min(SCAN_TS, S), SCAN_LB
    nt = ts // 8
    nb = S // ts
    r8 = ts // 8

    def kern(dr_ref, di_ref, sr_ref, si_ref, pr_ref, pi_ref, ar_ref, ai_ref, gr_ref, gi_ref, dar_ref, dai_ref,
             cr_ref, ci_ref):
        step = pl.program_id(1)
        blk = nb - 1 - step

        @pl.when(step == 0)
        def _():
            cr_ref[...] = jnp.zeros_like(cr_ref)
            ci_ref[...] = jnp.zeros_like(ci_ref)
            dar_ref[...] = jnp.zeros_like(dar_ref)
            dai_ref[...] = jnp.zeros_like(dai_ref)

        row8 = _iota((8, lb), 0)
        ar, ai = ar_ref[...], ai_ref[...]
        nai = -ai
        a2 = _cmul(ar, nai, ar, nai)
        a4 = _cmul(*a2, *a2)
        pows = ((ar, nai), a2, a4)
        tr, ti = _power_table(ar, nai, row8, True)
        halo_r = jnp.where(blk == 0, 0.0, pr_ref[...])
        halo_i = jnp.where(blk == 0, 0.0, pi_ref[...])

        def body(n, carry):
            cr, ci, acc_r, acc_i = carry
            i = nt - 1 - n
            off = pl.multiple_of(i * 8, 8)
            gr, gi = _tile_scan(dr_ref[pl.ds(off, 8), :], di_ref[pl.ds(off, 8), :], pows, row8, True)
            mr, mi = _cmul(tr, ti, cr, ci)
            gr, gi = gr + mr, gi + mi
            gr_ref[pl.ds(off, 8), :] = gr
            gi_ref[pl.ds(off, 8), :] = gi
            poff = pl.multiple_of(jnp.maximum(i - 1, 0) * 8, 8)
            before_r = jnp.where(i == 0, halo_r, sr_ref[pl.ds(poff, 8), :])
            before_i = jnp.where(i == 0, halo_i, si_ref[pl.ds(poff, 8), :])
            last_r = _pick_row(before_r, row8, 7)
            last_i = _pick_row(before_i, row8, 7)
            spr = jnp.where(row8 >= 1, pltpu.roll(sr_ref[pl.ds(off, 8), :], 1, 0), last_r)
            spi = jnp.where(row8 >= 1, pltpu.roll(si_ref[pl.ds(off, 8), :], 1, 0), last_i)
            acc_r = acc_r + gr * spr + gi * spi
            acc_i = acc_i + gi * spr - gr * spi
            return _pick_row(gr, row8, 0), _pick_row(gi, row8, 0), acc_r, acc_i

        zero = jnp.zeros((8, lb), F32)
        cr, ci, acc_r, acc_i = lax.fori_loop(0, nt, body, (cr_ref[...], ci_ref[...], zero, zero))
        cr_ref[...] = cr
        ci_ref[...] = ci
        dar_ref[...] += jnp.sum(acc_r, axis=0, keepdims=True)
        dai_ref[...] += jnp.sum(acc_i, axis=0, keepdims=True)

    blk = pl.BlockSpec((ts, lb), lambda j, i: (nb - 1 - i, j))
    halo = pl.BlockSpec((8, lb), lambda j, i: (jnp.maximum((nb - 1 - i) * r8 - 1, 0), j))
    par = pl.BlockSpec((1, lb), lambda j, i: (0, j))
    return pl.pallas_call(
        kern, name="s5_scan_bwd", grid=(L // lb, nb), in_specs=[blk, blk, blk, blk, halo, halo, par, par],
        out_specs=[blk, blk, par, par],
        out_shape=[jax.ShapeDtypeStruct((S, L), F32)] * 2 + [jax.ShapeDtypeStruct((1, L), F32)] * 2,
        scratch_shapes=[pltpu.VMEM((1, lb), F32), pltpu.VMEM((1, lb), F32)],
        compiler_params=_cparams(("parallel", "arbitrary")))(dst_re, dst_im, st_re, st_im, st_re, st_im, a_re, a_im)


def _loss_grad(x, target, gain, ts=256):
    S, D = x.shape

    def kern(x_ref, t_ref, g_ref, loss_ref, dx_ref, dg_ref):
        i = pl.program_id(0)
        tgt = t_ref[...]

        def f(xv, gv):
            err = _rms(xv, gv) - tgt
            return 0.5 * jnp.mean(err * err, axis=-1, keepdims=True)

        rowloss, vjp = jax.vjp(f, x_ref[...], g_ref[...])
        dx, dg = vjp(jnp.ones_like(rowloss))
        dx_ref[...] = dx

        @pl.when(i == 0)
        def _():
            loss_ref[...] = jnp.zeros_like(loss_ref)
            dg_ref[...] = jnp.zeros_like(dg_ref)

        loss_ref[...] += jnp.broadcast_to(jnp.sum(rowloss, axis=0, keepdims=True), loss_ref.shape)
        dg_ref[...] += dg

    row = pl.BlockSpec((ts, D), lambda i: (i, 0))
    return pl.pallas_call(
        kern, name="loss_grad", grid=(S // ts,), in_specs=[row, row, pl.BlockSpec((1, D), lambda i: (0, 0))],
        out_specs=[pl.BlockSpec((8, 128), lambda i: (0, 0)), row, pl.BlockSpec((1, D), lambda i: (0, 0))],
        out_shape=[jax.ShapeDtypeStruct((8, 128), F32), jax.ShapeDtypeStruct((S, D), F32),
                   jax.ShapeDtypeStruct((1, D), F32)],
        compiler_params=_cparams(("arbitrary",)))(x, target, gain)


def _rms_fwd(x, g, name):
    return _rowwise(_rms_fn, [_blk(x)], [g], [x.shape[1]], name, out_dtypes=[BF16])[0]


def _rms_bwd(x, g, dy, name, add=None):
    return _rowwise_bwd(_rms_fn, [_blk(x)], [g], [dy], name, adds=None if add is None else {0: add})


FFN_TC = 1408


def _common_fwd(x, mem, P, L):
    hx = _rms_fwd(x, P['xa_norm'], L + "xa_norm")
    q = _matmul(hx, P['xa_wq'], name=L + "xa_q")
    memn = _rms_fwd(mem, P['mem_norm'], L + "mem_norm")
    kv = _matmul(memn, P['xa_wkv'], name=L + "xa_kv")
    att = _rowwise(_xattn_fn, [_blk(q)], [kv], [1024], L + "xattn", out_dtypes=[BF16])[0]
    x2 = _matmul(att, P['xa_wo'], res=x, name=L + "xa_o")
    hf = _rms_fwd(x2, P['ffn_norm'], L + "ffn_norm")
    hu = _matmul(hf, P['ffn_w_up'], name=L + "ffn_up")
    cw = P['ffn_conv']
    act = _conv_post([(hu, 0, cw, 0), (hu, 2, cw, 2)], _ffn_post, 2, FFN_TC, L + "ffn_conv", out_dtype=BF16)
    x3 = _matmul(act, P['ffn_w_down'], res=x2, name=L + "ffn_down")
    return x3, (x, mem, hx, q, memn, kv, att, x2, hf, hu, act)


def _common_bwd(saved, dx3, P, L, midway=None):
    x, mem, hx, q, memn, kv, att, x2, hf, hu, act = saved
    G = {}
    dact = _matmul(dx3, P['ffn_w_down'], "nt", name=L + "ffn_down_dx")
    G['ffn_w_down'] = _matmul(act, dx3, "tn", name=L + "ffn_down_dw")
    cw = P['ffn_conv']
    dhu_u, dcw_u, dhu_g, dcw_g = _conv_post_bwd([(hu, 0, cw, 0), (hu, 2, cw, 2)], _ffn_post, 2, FFN_TC, dact,
                                                L + "ffn_conv_bwd")
    G['ffn_conv'] = jnp.concatenate([dcw_u, dcw_g], axis=1)
    dhf = _matmul_cat([dhu_u, dhu_g], P['ffn_w_up'], "nt", name=L + "ffn_up_dx")
    G['ffn_w_up'] = jnp.concatenate([_matmul(hf, dhu_u, "tn", name=L + "ffn_up_dw_up"),
                                     _matmul(hf, dhu_g, "tn", name=L + "ffn_up_dw_gate")], axis=1)
    dx2, G['ffn_norm'] = _rms_bwd(x2, P['ffn_norm'], dhf, L + "ffn_norm_bwd", add=dx3)
    if midway is not None:
        P = dict(P, xa_wo=P['xa_wo'] + midway(dx2).astype(P['xa_wo'].dtype))
    datt = _matmul(dx2, P['xa_wo'], "nt", name=L + "xa_o_dx")
    G['xa_wo'] = _matmul(att, dx2, "tn", name=L + "xa_o_dw")
    dq, dkv = _rowwise_bwd(_xattn_fn, [_blk(q)], [kv], [datt], L + "xattn_bwd", out_dtypes=[BF16])
    dhx = _matmul(dq, P['xa_wq'], "nt", name=L + "xa_q_dx")
    G['xa_wq'] = _matmul(hx, dq, "tn", name=L + "xa_q_dw")
    dmemn = _matmul(dkv, P['xa_wkv'], "nt", name=L + "xa_kv_dx")
    G['xa_wkv'] = _matmul(memn, dkv, "tn", name=L + "xa_kv_dw")
    _, G['mem_norm'] = _rms_bwd(mem, P['mem_norm'], dmemn, L + "mem_norm_bwd")
    dx, G['xa_norm'] = _rms_bwd(x, P['xa_norm'], dhx, L + "xa_norm_bwd", add=dx2)
    return dx, G


U_COLS = (2048, 512)


def _even_fwd(x, P):
    S = x.shape[0]
    h0 = _rms_fwd(x, P['mix_norm'], "l0_mix_norm")
    proj = _matmul(h0, P['w_in'], name="l0_in")
    tabs = _ret_tables(S)
    o_raw, rstates = _ret_call(proj, tabs)
    o = _rowwise(_ret_post_fn, [_blk(o_raw), _blk(proj, 512, 3)], [P['ret_norm']], [512], "l0_ret_post",
                 out_dtypes=[BF16])[0]
    prep_in = (P['s5_lambda_re'], P['s5_lambda_im'], P['s5_log_dt'], P['s5_b_re'], P['s5_b_im'], P['s5_c_re'],
               P['s5_c_im'])
    a_re, a_im, bd_re, bd_im, cd_re, cd_im = _s5_prep(prep_in)
    a_re_f, a_im_f = a_re.reshape(1, 2048), a_im.reshape(1, 2048)
    bu_re = _matmul(proj, bd_re, name="l0_s5_bu_re", a_cols=U_COLS)
    bu_im = _matmul(proj, bd_im, name="l0_s5_bu_im", a_cols=U_COLS)
    st_re, st_im = _scan_fwd(bu_re, bu_im, a_re_f, a_im_f)
    y1 = _matmul(st_re, cd_re, name="l0_s5_y_re")
    y2 = _matmul(st_im, cd_im, name="l0_s5_y_im")
    yg = _rowwise(_s5_post_fn, [_blk(y1), _blk(y2), _blk(proj, 512, 4)],
                  [P['s5_d'], P['s5_w_glu'], P['s5_b_glu']], [512], "l0_s5_post", out_dtypes=[BF16])[0]
    x1 = _matmul_cat([o, yg], P['w_out'], "nn", res=x, name="l0_out")
    saved = (x, h0, proj, tabs, o_raw, rstates, prep_in, a_re_f, a_im_f, bd_re, bd_im, cd_re, cd_im, st_re, st_im,
             y1, y2, o, yg)
    return x1, saved


def _even_bwd(saved, dx1, P, midway=None):
    (x, h0, proj, tabs, o_raw, rstates, prep_in, a_re_f, a_im_f, bd_re, bd_im, cd_re, cd_im, st_re, st_im, y1, y2,
     o, yg) = saved
    G = {}
    dmerged = _matmul(dx1, P['w_out'], "nt", name="l0_out_dx")
    G['w_out'] = jnp.concatenate([_matmul(o, dx1, "tn", name="l0_out_dw_ret"),
                                  _matmul(yg, dx1, "tn", name="l0_out_dw_s5")], axis=0)
    do_raw, dgate, G['ret_norm'] = _rowwise_bwd(
        _ret_post_fn, [_blk(o_raw), _blk(proj, 512, 3)], [P['ret_norm']], [_blk(dmerged, 512, 0)], "l0_ret_post_bwd",
        out_dtypes=[F32, BF16])
    dq, dk, dv = _ret_call(proj, tabs, states=rstates, do=do_raw)
    if midway is not None:
        P = dict(P, s5_w_glu=P['s5_w_glu'] + midway(dq))
    dy1, dy2, du_a, G['s5_d'], G['s5_w_glu'], G['s5_b_glu'] = _rowwise_bwd(
        _s5_post_fn, [_blk(y1), _blk(y2), _blk(proj, 512, 4)], [P['s5_d'], P['s5_w_glu'], P['s5_b_glu']],
        [_blk(dmerged, 512, 1)], "l0_s5_post_bwd", out_dtypes=[BF16, BF16, F32])
    dst_re = _matmul(dy1, cd_re, "nt", name="l0_s5_y_re_dx")
    dcd_re = _matmul(st_re, dy1, "tn", name="l0_s5_y_re_dw")
    dst_im = _matmul(dy2, cd_im, "nt", name="l0_s5_y_im_dx")
    dcd_im = _matmul(st_im, dy2, "tn", name="l0_s5_y_im_dw")
    dbu_re, dbu_im, da_re, da_im = _scan_bwd(dst_re, dst_im, st_re, st_im, a_re_f, a_im_f)
    du = _matmul(dbu_re, bd_re, "nt", res=du_a, name="l0_s5_bu_re_dx")
    du = _matmul(dbu_im, bd_im, "nt", res=du, name="l0_s5_bu_im_dx", out_dtype=BF16)
    dbd_re = _matmul(proj, dbu_re, "tn", name="l0_s5_bu_re_dw", a_cols=U_COLS)
    dbd_im = _matmul(proj, dbu_im, "tn", name="l0_s5_bu_im_dw", a_cols=U_COLS)
    dprep = _s5_prep(prep_in, cots=(da_re.reshape(32, 64), da_im.reshape(32, 64), dbd_re, dbd_im, dcd_re, dcd_im))
    for n, d in zip(('s5_lambda_re', 's5_lambda_im', 's5_log_dt', 's5_b_re', 's5_b_im', 's5_c_re', 's5_c_im'), dprep):
        G[n] = d
    pieces = [dq, dk, dv, dgate, du]
    dh0 = _matmul_cat(pieces, P['w_in'], "nt", name="l0_in_dx")
    G['w_in'] = jnp.concatenate([_matmul(h0, p, "tn", name="l0_in_dw_%d" % n) for n, p in enumerate(pieces)], axis=1)
    dx, G['mix_norm'] = _rms_bwd(x, P['mix_norm'], dh0, "l0_mix_norm_bwd", add=dx1)
    return dx, G


def _odd_fwd(x, P):
    h1 = _rms_fwd(x, P['mix_norm'], "l1_mix_norm")
    pm = _matmul(h1, P['w_main'], name="l1_in_main")
    pt = _matmul(h1, P['w_tail'], name="l1_in_tail")
    qkv = _conv_post([(pm, 0, P['conv'], 0)], _silu, 3, 1024, "l1_conv")
    g_e, beta_e = _rowwise(_gdn_gates_fn, [_blk(pt)], [P['a_log_p'], P['dtb_p']], [1024, 1024], "l1_gdn_gates")
    w, u, qd, kd, qk, tinv = _gdn_intra_call(qkv, g_e, beta_e)
    o_raw, gstates = _gdn_pass(w, u, qd, kd, qk, g_e)
    og = _rowwise(_gdn_post_fn, [_blk(o_raw), _blk(pm, 1024, 3)], [P['o_norm']], [1024], "l1_gdn_post",
                  out_dtypes=[BF16])[0]
    x1 = _matmul(og, P['w_out'], res=x, name="l1_out")
    return x1, (x, h1, pm, pt, qkv, g_e, beta_e, w, u, qd, kd, qk, tinv, o_raw, gstates, og)


def _odd_bwd(saved, dx1, P):
    x, h1, pm, pt, qkv, g_e, beta_e, w, u, qd, kd, qk, tinv, o_raw, gstates, og = saved
    G = {}
    dog = _matmul(dx1, P['w_out'], "nt", name="l1_out_dx")
    G['w_out'] = _matmul(og, dx1, "tn", name="l1_out_dw")
    do_raw, dz, G['o_norm'] = _rowwise_bwd(_gdn_post_fn, [_blk(o_raw), _blk(pm, 1024, 3)], [P['o_norm']], [dog],
                                           "l1_gdn_post_bwd", out_dtypes=[F32, BF16])
    dw, du, dqd, dkd, dqk, dg_pass = _gdn_pass(w, u, qd, kd, qk, g_e, states=gstates, do=do_raw)
    dqkv = _gdn_intra_call(qkv, g_e, beta_e, cots=(dw, du, dqd, dkd, dqk, dg_pass, tinv))
    dg_e, dbeta_e = dqkv[3], dqkv[4]
    dpt, G['a_log_p'], G['dtb_p'] = _rowwise_bwd(_gdn_gates_fn, [_blk(pt)], [P['a_log_p'], P['dtb_p']],
                                                 [dg_e, dbeta_e], "l1_gdn_gates_bwd", out_dtypes=[BF16])
    pieces, dcw = [], []
    for part in range(3):
        dxp, dwp = _conv_post_bwd([(pm, part, P['conv'], part)], _silu, 1, 1024, dqkv[part],
                                  "l1_conv_bwd_%d" % part)
        pieces.append(dxp)
        dcw.append(dwp)
    G['conv'] = jnp.concatenate(dcw, axis=1)
    pieces += [dz, dpt]
    dh1 = _matmul_cat(pieces, P['w_all'], "nt", name="l1_in_dx")
    G['w_all'] = jnp.concatenate([_matmul(h1, p, "tn", name="l1_in_dw_%d" % n) for n, p in enumerate(pieces)], axis=1)
    dx, G['mix_norm'] = _rms_bwd(x, P['mix_norm'], dh1, "l1_mix_norm_bwd", add=dx1)
    return dx, G


def _row(v):
    return v.reshape(1, -1)


def _local_step(x, mem, target, W, later_weights=None, early_grads=None):
    P0 = {
        'mix_norm': _row(W['l0_mix_norm']), 'w_in': W['l0_w_in'], 'ret_norm': _row(W['l0_ret_norm']),
        's5_lambda_re': W['l0_s5_lambda_re'], 's5_lambda_im': W['l0_s5_lambda_im'],
        's5_log_dt': W['l0_s5_log_dt'].reshape(32, 1),
        's5_b_re': W['l0_s5_b_re'].reshape(512, 64), 's5_b_im': W['l0_s5_b_im'].reshape(512, 64),
        's5_c_re': W['l0_s5_c_re'].reshape(2048, 16), 's5_c_im': W['l0_s5_c_im'].reshape(2048, 16),
        's5_d': _row(W['l0_s5_d']), 's5_w_glu': W['l0_s5_w_glu'].astype(F32), 's5_b_glu': _row(W['l0_s5_b_glu']),
        'w_out': W['l0_w_out'],
    }
    def common(L):
        return {'xa_norm': _row(W[L + 'xa_norm']), 'mem_norm': _row(W[L + 'mem_norm']), 'xa_wq': W[L + 'xa_wq'],
                'xa_wkv': W[L + 'xa_wkv'], 'xa_wo': W[L + 'xa_wo'], 'ffn_norm': _row(W[L + 'ffn_norm']),
                'ffn_w_up': W[L + 'ffn_w_up'], 'ffn_conv': W[L + 'ffn_conv'], 'ffn_w_down': W[L + 'ffn_w_down']}

    x1, s_even = _even_fwd(x, P0)
    if later_weights is not None:
        W = dict(W, **later_weights('l0_common', x1))
    C0 = common('l0_')
    x3, s_c0 = _common_fwd(x1, mem, C0, "l0_")

    if later_weights is not None:
        W = dict(W, **later_weights('l1', x3))
    w_in1 = W['l1_w_in']
    pad8 = jnp.zeros((8,), F32)
    w_all = jnp.pad(w_in1, ((0, 0), (0, 112)))
    P1 = {
        'mix_norm': _row(W['l1_mix_norm']), 'w_main': w_in1[:, :4096], 'w_tail': w_all[:, 4096:], 'w_all': w_all,
        'conv': W['l1_conv'],
        'a_log_p': _row(jnp.concatenate([pad8, W['l1_a_log'], jnp.zeros((112,), F32)])),
        'dtb_p': _row(jnp.concatenate([pad8, W['l1_dt_bias'], jnp.zeros((112,), F32)])),
        'o_norm': _row(W['l1_o_norm']), 'w_out': W['l1_w_out'],
    }
    C1 = common('l1_')
    x4, s_odd = _odd_fwd(x3, P1)
    x6, s_c1 = _common_fwd(x4, mem, C1, "l1_")
    loss_tile, dx6, d_final = _loss_grad(x6, target, _row(W['final_norm']))

    G = {'final_norm': d_final.reshape(-1)}
    dx4, g = _common_bwd(s_c1, dx6, C1, "l1_")
    for k, v in g.items():
        G['l1_' + k] = v
    dx3, g = _odd_bwd(s_odd, dx4, P1)
    G['l1_mix_norm'] = g['mix_norm']
    G['l1_w_in'] = g['w_all'][:, :4112]
    G['l1_conv'] = g['conv']
    G['l1_a_log'] = g['a_log_p'][0, 8:16]
    G['l1_dt_bias'] = g['dtb_p'][0, 8:16]
    G['l1_o_norm'] = g['o_norm']
    G['l1_w_out'] = g['w_out']
    midway = None
    if early_grads is not None:
        zero, midway = early_grads('l1', G)
        C0 = dict(C0, ffn_w_down=C0['ffn_w_down'] + zero.astype(C0['ffn_w_down'].dtype))
    dx1, g = _common_bwd(s_c0, dx3, C0, "l0_", midway=midway)
    for k, v in g.items():
        G['l0_' + k] = v
    if early_grads is not None:
        zero, midway = early_grads('l0_common', G)
        P0 = dict(P0, w_out=P0['w_out'] + zero.astype(P0['w_out'].dtype))
    dx0, g = _even_bwd(s_even, dx1, P0, midway=midway)
    for k, v in g.items():
        G['l0_' + k] = v
    return loss_tile, dx0, G


ANY = pl.BlockSpec(memory_space=pl.ANY)


def _place():
    return lax.axis_index("x"), lax.axis_index("y"), lax.axis_index("c")


def _my_chip():
    return 2 * lax.axis_index("x") + lax.axis_index("y")


def _chip_peers(x, y):
    return [(1 - x, y), (x, 1 - y), (1 - x, 1 - y)]


def _half(ref, mode, shard, j, h, split):
    r, w = shard
    rh = r // 2 if split else r
    h = h if split else 0
    if mode == 'row':
        return ref.at[pl.ds(j * r + h * rh, rh), :]
    if mode == 'col':
        return ref.at[pl.ds(h * rh, rh), pl.ds(j * w, w)]
    return ref.at[j, pl.ds(h * rh, rh), :]


def _place_shard(shard, mode, name):
    r, w = shard.shape
    dtype = BF16 if mode != 'tap' else shard.dtype
    if mode == 'tap':
        mode = 'slab'
    tr = _row_tile(r, w)
    nb = r // tr

    def kern(s_ref, o_ref):
        o_ref[...] = s_ref[...].astype(o_ref.dtype)

    if mode == 'row':
        full, o_spec = (4 * r, w), pl.BlockSpec((tr, w), lambda i: (_my_chip() * nb + i, 0))
    elif mode == 'col':
        full, o_spec = (r, 4 * w), pl.BlockSpec((tr, w), lambda i: (i, _my_chip()))
    else:
        full, o_spec = (4, r, w), pl.BlockSpec((None, tr, w), lambda i: (_my_chip(), i, 0))
    return pl.pallas_call(kern, name=name, grid=(nb,), in_specs=[pl.BlockSpec((tr, w), lambda i: (i, 0))],
                          out_specs=o_spec, out_shape=jax.ShapeDtypeStruct(full, dtype),
                          compiler_params=_cparams(("parallel",)))(shard)


def _gather_placed(fulls, modes, shards, splits):
    n = len(fulls)

    def body(*refs):
        outs = refs[n:2 * n]
        send_sems, recv_sems = refs[2 * n:]
        x, y, c = _place()
        peers = _chip_peers(x, y)
        me = 2 * x + y

        def win(a, j, h):
            return _half(outs[a], modes[a], shards[a], j, h, splits[a])

        def copy(a, k, j, h, to):
            return pltpu.make_async_remote_copy(src_ref=win(a, j, h), dst_ref=win(a, j, h),
                                                send_sem=send_sems.at[6 * a + k], recv_sem=recv_sems.at[6 * a + k],
                                                device_id=to, device_id_type=MESH)

        over_ici = [copy(a, k, me, c, (p[0], p[1], c)) for a in range(n) for k, p in enumerate(peers)]
        for cp in over_ici:
            cp.start()
        passed = []
        for a in range(n):
            for k, p in enumerate(peers):
                j = 2 * p[0] + p[1]
                copy(a, k, j, c, (p[0], p[1], c)).wait_recv()
                if splits[a]:
                    fwd = copy(a, 3 + k, j, c, (x, y, 1 - c))
                    fwd.start()
                    passed.append(fwd)
        for a in range(n):
            if splits[a]:
                for k, p in enumerate(peers):
                    copy(a, 3 + k, 2 * p[0] + p[1], 1 - c, (x, y, 1 - c)).wait_recv()
        for cp in over_ici + passed:
            cp.wait_send()

    return pl.pallas_call(
        body, name="gather_weights", in_specs=[ANY] * n, out_specs=[ANY] * n,
        out_shape=[jax.ShapeDtypeStruct(f.shape, f.dtype) for f in fulls],
        input_output_aliases={a: a for a in range(n)},
        scratch_shapes=[pltpu.SemaphoreType.DMA((6 * n,)), pltpu.SemaphoreType.DMA((6 * n,))],
    )(*fulls)


_FLIPS = [(dx, dy, dc) for dx in (0, 1) for dy in (0, 1) for dc in (0, 1) if (dx, dy, dc) != (0, 0, 0)]


def _send_other_half(gs, small, name):
    n = len(gs)

    def body(*refs):
        ins, outs = refs[:n], refs[n + 1:2 * n + 1]
        small_ref = refs[2 * n + 1]
        send_sems, recv_sems, small_send, small_recv = refs[2 * n + 2:]
        x, y, c = _place()
        me = 4 * x + 2 * y + c

        def peer(f):
            return (x ^ f[0], y ^ f[1], c ^ f[2])

        def small_copy(k, slab, to):
            return pltpu.make_async_remote_copy(src_ref=small_ref.at[slab], dst_ref=small_ref.at[slab],
                                                send_sem=small_send.at[k], recv_sem=small_recv.at[k], device_id=to,
                                                device_id_type=MESH)

        cps = []
        for a in range(n):
            rh = gs[a].shape[1] // 2
            cps.append(pltpu.make_async_remote_copy(
                src_ref=ins[a].at[:, pl.ds((1 - c) * rh, rh), :], dst_ref=outs[a], send_sem=send_sems.at[a],
                recv_sem=recv_sems.at[a], device_id=(x, y, 1 - c), device_id_type=MESH))
        smalls = [small_copy(k, me, peer(f)) for k, f in enumerate(_FLIPS)]
        for cp in cps + smalls:
            cp.start()
        for cp in cps:
            cp.wait()
        for k, f in enumerate(_FLIPS):
            p = peer(f)
            small_copy(k, 4 * p[0] + 2 * p[1] + p[2], p).wait_recv()
        for cp in smalls:
            cp.wait_send()

    outs = pl.pallas_call(
        body, name=name, in_specs=[ANY] * (n + 1), out_specs=[ANY] * (n + 1),
        out_shape=[jax.ShapeDtypeStruct((g.shape[0], g.shape[1] // 2, g.shape[2]), g.dtype) for g in gs]
        + [jax.ShapeDtypeStruct(small.shape, small.dtype)],
        input_output_aliases={n: n},
        scratch_shapes=[pltpu.SemaphoreType.DMA((n,)), pltpu.SemaphoreType.DMA((n,)),
                        pltpu.SemaphoreType.DMA((7,)), pltpu.SemaphoreType.DMA((7,))],
    )(*gs, small)
    return outs[:n], outs[n]


def _send_to_chips(ps, widths):
    n = len(ps)

    def body(*refs):
        ins, outs = refs[:n], refs[n:2 * n]
        send_sems, recv_sems = refs[2 * n:]
        x, y, c = _place()
        peers = _chip_peers(x, y)
        me = 2 * x + y

        def src(a, j):
            if ps[a].shape[0] == 4:
                return ins[a].at[j]
            return ins[a].at[0, :, pl.ds(j * widths[a], widths[a])]

        def copy(a, k, j, dst_slab, to):
            return pltpu.make_async_remote_copy(src_ref=src(a, j), dst_ref=outs[a].at[dst_slab],
                                                send_sem=send_sems.at[3 * a + k], recv_sem=recv_sems.at[3 * a + k],
                                                device_id=(to[0], to[1], c), device_id_type=MESH)

        sends = [copy(a, k, 2 * p[0] + p[1], me, p) for a in range(n) for k, p in enumerate(peers)]
        for cp in sends:
            cp.start()
        for a in range(n):
            for k, p in enumerate(peers):
                copy(a, k, me, 2 * p[0] + p[1], p).wait_recv()
        for cp in sends:
            cp.wait_send()

    return pl.pallas_call(
        body, name="send_to_chips", in_specs=[ANY] * n, out_specs=[ANY] * n,
        out_shape=[jax.ShapeDtypeStruct((4, p.shape[1], w), p.dtype) for p, w in zip(ps, widths)],
        scratch_shapes=[pltpu.SemaphoreType.DMA((3 * n,)), pltpu.SemaphoreType.DMA((3 * n,))],
    )(*ps)


def _share_halves(bufs, name):
    n = len(bufs)

    def body(*refs):
        outs = refs[n:2 * n]
        send_sems, recv_sems = refs[2 * n:]
        x, y, c = _place()
        sends, waits = [], []
        for a in range(n):
            rh = bufs[a].shape[0] // 2
            mine = outs[a].at[pl.ds(c * rh, rh), :]
            other = outs[a].at[pl.ds((1 - c) * rh, rh), :]
            sends.append(pltpu.make_async_remote_copy(src_ref=mine, dst_ref=mine, send_sem=send_sems.at[a],
                                                      recv_sem=recv_sems.at[a], device_id=(x, y, 1 - c),
                                                      device_id_type=MESH))
            waits.append(pltpu.make_async_remote_copy(src_ref=mine, dst_ref=other, send_sem=send_sems.at[a],
                                                      recv_sem=recv_sems.at[a], device_id=(x, y, 1 - c),
                                                      device_id_type=MESH))
        for cp in sends:
            cp.start()
        for cp in waits:
            cp.wait()

    return pl.pallas_call(
        body, name=name, in_specs=[ANY] * n, out_specs=[ANY] * n,
        out_shape=[jax.ShapeDtypeStruct(b.shape, b.dtype) for b in bufs],
        input_output_aliases={a: a for a in range(n)},
        scratch_shapes=[pltpu.SemaphoreType.DMA((n,)), pltpu.SemaphoreType.DMA((n,))],
    )(*bufs)


def _gather_all(mine):
    flips = [(dx, dy, dc) for dx in (0, 1) for dy in (0, 1) for dc in (0, 1) if (dx, dy, dc) != (0, 0, 0)]

    def body(x_ref, out_ref, send_sems, recv_sems, local_sem):
        x, y, c = _place()
        me = 4 * x + 2 * y + c

        def peer(f):
            return (x ^ f[0], y ^ f[1], c ^ f[2])

        def copy(k, slab, to):
            return pltpu.make_async_remote_copy(src_ref=x_ref, dst_ref=out_ref.at[slab], send_sem=send_sems.at[k],
                                                recv_sem=recv_sems.at[k], device_id=to, device_id_type=MESH)

        own = pltpu.make_async_copy(x_ref, out_ref.at[me], local_sem)
        own.start()
        sends = [copy(k, me, peer(f)) for k, f in enumerate(flips)]
        for s in sends:
            s.start()
        for k, f in enumerate(flips):
            p = peer(f)
            copy(k, 4 * p[0] + 2 * p[1] + p[2], p).wait_recv()
        for s in sends:
            s.wait_send()
        own.wait()

    return pl.pallas_call(
        body, name="gather_all", in_specs=[ANY], out_specs=ANY,
        out_shape=jax.ShapeDtypeStruct((8,) + mine.shape, mine.dtype),
        scratch_shapes=[pltpu.SemaphoreType.DMA((7,)), pltpu.SemaphoreType.DMA((7,)), pltpu.SemaphoreType.DMA],
    )(mine)


TILE_BYTES = 2 * 1024 * 1024


def _row_tile(rows, width=1024):
    for t in (512, 352, 256, 176, 128, 64, 32, 16, 8):
        if rows % t == 0 and t * width * 4 <= TILE_BYTES:
            return t
    return rows


def _pair_sum(g, got, name):
    ns, r, w = g.shape
    rh = r // 2
    tr = _row_tile(rh, w)
    nb = rh // tr

    def kern(g_ref, o_ref, out_ref):
        out_ref[...] = (g_ref[...] + o_ref[...]).astype(BF16)

    return pl.pallas_call(
        kern, name=name, grid=(ns, nb),
        in_specs=[pl.BlockSpec((None, tr, w), lambda j, i: (j, lax.axis_index("c") * nb + i, 0)),
                  pl.BlockSpec((None, tr, w), lambda j, i: (j, i, 0))],
        out_specs=pl.BlockSpec((None, tr, w), lambda j, i: (j, i, 0)),
        out_shape=jax.ShapeDtypeStruct((ns, rh, w), BF16),
        compiler_params=_cparams(("parallel", "parallel")))(g, got)


def _chip_sum(pair, recv, w, name):
    rh = pair.shape[1]
    tr = _row_tile(rh, w)
    nb = rh // tr

    def kern(own_ref, r1_ref, r2_ref, r3_ref, out_ref):
        acc = own_ref[...].astype(F32)
        for r_ref in (r1_ref, r2_ref, r3_ref):
            acc = acc + r_ref[...].astype(F32)
        out_ref[...] = acc

    if pair.shape[0] == 4:
        own_spec = pl.BlockSpec((None, tr, w), lambda i: (_my_chip(), i, 0))
    else:
        own_spec = pl.BlockSpec((None, tr, w), lambda i: (0, i, _my_chip()))
    recv_specs = [pl.BlockSpec((None, tr, w), functools.partial(lambda i, d: ((_my_chip() + d) % 4, i, 0), d=d))
                  for d in (1, 2, 3)]
    return pl.pallas_call(
        kern, name=name, grid=(nb,), in_specs=[own_spec] + recv_specs,
        out_specs=pl.BlockSpec((tr, w), lambda i: (lax.axis_index("c") * nb + i, 0)),
        out_shape=jax.ShapeDtypeStruct((2 * rh, w), F32), compiler_params=_cparams(("parallel",)))(pair, recv, recv, recv)


def _slab_sum(slabs, name):
    n, R, w = slabs.shape
    tr = _row_tile(R)

    def kern(s_ref, o_ref):
        acc = s_ref[0].astype(F32)
        for k in range(1, n):
            acc = acc + s_ref[k].astype(F32)
        o_ref[...] = acc

    return pl.pallas_call(
        kern, name=name, grid=(R // tr,), in_specs=[pl.BlockSpec((n, tr, w), lambda i: (0, i, 0))],
        out_specs=pl.BlockSpec((tr, w), lambda i: (i, 0)), out_shape=jax.ShapeDtypeStruct((R, w), F32),
        compiler_params=_cparams(("parallel",)))(slabs)


def _adamw(w, g, m, v, name):
    R, C = w.shape
    tr = _pick(R, (256, 128, 64, 32, 16, 8))

    def kern(w_ref, g_ref, m_ref, v_ref, d_ref, nm_ref, nv_ref, g_out_ref):
        gv = g_ref[...]
        g_out_ref[...] = gv
        m2 = ADAM_B1 * m_ref[...] + (1.0 - ADAM_B1) * gv
        v2 = ADAM_B2 * v_ref[...] + (1.0 - ADAM_B2) * jnp.square(gv)
        m_hat = m2 / (1.0 - ADAM_B1 ** ADAM_STEP)
        v_hat = v2 / (1.0 - ADAM_B2 ** ADAM_STEP)
        d_ref[...] = -ADAM_LR * (m_hat / (jnp.sqrt(v_hat) + ADAM_EPS) + ADAM_WD * w_ref[...])
        nm_ref[...] = m2
        nv_ref[...] = v2

    spec = pl.BlockSpec((tr, C), lambda i: (i, 0))
    return pl.pallas_call(
        kern, name=name, grid=(R // tr,), in_specs=[spec] * 4, out_specs=[spec] * 4,
        out_shape=[jax.ShapeDtypeStruct((R, C), F32)] * 4, compiler_params=_cparams(("parallel",)))(w, g, m, v)


def _pack_small(vals):
    flat = jnp.concatenate([vals[n].astype(F32).reshape(-1) for n in SMALL_NAMES])
    rows = -(-flat.shape[0] // (8 * LANES)) * 8
    return jnp.pad(flat, (0, rows * LANES - flat.shape[0])).reshape(rows, LANES)


def _unpack_small(packed, shapes):
    flat = packed.reshape(-1)
    out = {}
    off = 0
    for n in SMALL_NAMES:
        size = int(np.prod(shapes[n]))
        out[n] = flat[off:off + size].reshape(shapes[n])
        off += size
    return out


HBM = pl.BlockSpec(memory_space=pltpu.HBM)
SEM = pl.BlockSpec(memory_space=pltpu.SEMAPHORE)
DATAFLOW = pltpu.SideEffectType.DATAFLOW_SIDE_EFFECTING


def _in_hbm(a):
    return pltpu.with_memory_space_constraint(a, pltpu.HBM)


def _split_copy_start(srcs, lands, copies, after, name):
    ns, nl = len(srcs), len(lands)
    ncopy = len(copies(list(srcs), list(lands), None, None, probe=True))

    def body(*refs):
        src_refs, land_refs = refs[:ns], refs[ns:ns + nl]
        send_sems, recv_sems = refs[ns + nl + 1:ns + nl + 3]
        token = refs[-1]
        for cp in copies(src_refs, land_refs, send_sems, recv_sems):
            cp.start()
        token[...] = jnp.zeros_like(token)

    outs = pl.pallas_call(
        body, name=name,
        out_shape=(pltpu.SemaphoreType.DMA((ncopy,)), pltpu.SemaphoreType.DMA((ncopy,)),
                   *[pltpu.HBM(a.shape, a.dtype) for a in srcs], *[pltpu.HBM(a.shape, a.dtype) for a in lands],
                   jax.ShapeDtypeStruct((8, 128), F32)),
        in_specs=[HBM] * (ns + nl) + [ANY],
        out_specs=(SEM, SEM, *[HBM] * (ns + nl), pl.BlockSpec(memory_space=pltpu.VMEM)),
        input_output_aliases={i: 2 + i for i in range(ns + nl)},
        compiler_params=pltpu.CompilerParams(has_side_effects=DATAFLOW),
    )(*[_in_hbm(a) for a in srcs], *[_in_hbm(a) for a in lands], after)
    return outs[0], outs[1], outs[2:2 + ns], outs[2 + ns:2 + ns + nl], outs[-1]


def _split_copy_wait(send_sems, recv_sems, srcs, lands, copies, after, name):
    ns, nl = len(srcs), len(lands)

    def body(*refs):
        src_refs, land_refs = refs[:ns], refs[ns:ns + nl]
        send_ref, recv_ref = refs[ns + nl:ns + nl + 2]
        for cp in copies(src_refs, land_refs, send_ref, recv_ref):
            cp.wait_send()
            cp.wait_recv()

    outs = pl.pallas_call(
        body, name=name,
        out_shape=tuple(pltpu.HBM(a.shape, a.dtype) for a in list(srcs) + list(lands)),
        in_specs=[HBM] * (ns + nl) + [SEM, SEM, ANY], out_specs=tuple([HBM] * (ns + nl)),
        input_output_aliases={i: i for i in range(ns + nl)},
        compiler_params=pltpu.CompilerParams(has_side_effects=DATAFLOW),
    )(*srcs, *lands, send_sems, recv_sems, after)
    return outs[:ns], outs[ns:]


def _matrix_mode(n):
    return 'slab' if n == 'l1_w_in' else ('row' if MATRICES[n] == 0 else 'col')


def _placed(A, names):
    modes = ['slab' if n in CONVS else _matrix_mode(n) for n in names]
    fulls = [_place_shard(A[n], 'tap' if n in CONVS else m, "place_" + n) for n, m in zip(names, modes)]
    return fulls, modes


def _assembled(names, modes, outs):
    return {n: jnp.concatenate([o[j] for j in range(4)], axis=1) if m == 'slab' else o
            for n, m, o in zip(names, modes, outs)}


def _gather_weights(A, names):
    fulls, modes = _placed(A, names)
    outs = _gather_placed(fulls, modes, [A[n].shape for n in names], [n not in CONVS for n in names])
    return _assembled(names, modes, outs)


def _whole_shard_copies(modes, shards):
    def copies(src_refs, land_refs, send_sems, recv_sems, probe=False):
        if probe:
            return [None] * (3 * len(land_refs))
        x, y, c = _place()
        me = 2 * x + y
        out = []
        for a, ref in enumerate(land_refs):
            for k, p in enumerate(_chip_peers(x, y)):
                out.append(pltpu.make_async_remote_copy(
                    src_ref=_half(ref, modes[a], shards[a], me, 0, False),
                    dst_ref=_half(ref, modes[a], shards[a], me, 0, False),
                    send_sem=send_sems.at[3 * a + k], recv_sem=recv_sems.at[3 * a + k],
                    device_id=(p[0], p[1], c), device_id_type=MESH))
        return out
    return copies


def _gather_weights_start(A, names, after, tag):
    fulls, modes = _placed(A, names)
    copies = _whole_shard_copies(modes, [A[n].shape for n in names])
    send_sems, recv_sems, _, lands, zeros = _split_copy_start([], fulls, copies, after, "gather_start_" + tag)
    return (send_sems, recv_sems, lands, copies, names, modes), zeros


def _gather_weights_wait(state, after, tag):
    send_sems, recv_sems, lands, copies, names, modes = state
    _, outs = _split_copy_wait(send_sems, recv_sems, [], lands, copies, after, "gather_wait_" + tag)
    return _assembled(names, modes, outs)


def _to_chips_copies(pair_shapes, widths):
    def copies(src_refs, land_refs, send_sems, recv_sems, probe=False):
        if probe:
            return [None] * (3 * len(land_refs))
        x, y, c = _place()
        me = 2 * x + y
        out = []
        for a, (src, land) in enumerate(zip(src_refs, land_refs)):
            for k, p in enumerate(_chip_peers(x, y)):
                j = 2 * p[0] + p[1]
                part = src.at[j] if pair_shapes[a][0] == 4 else src.at[0, :, pl.ds(j * widths[a], widths[a])]
                out.append(pltpu.make_async_remote_copy(
                    src_ref=part, dst_ref=land.at[me], send_sem=send_sems.at[3 * a + k],
                    recv_sem=recv_sems.at[3 * a + k], device_id=(p[0], p[1], c), device_id_type=MESH))
        return out
    return copies


def _slabbed(G, names):
    gs, widths = [], []
    for n in names:
        g = G[n]
        mode = _matrix_mode(n)
        if mode == 'row':
            gs.append(g.reshape(4, g.shape[0] // 4, g.shape[1]))
            widths.append(g.shape[1])
        elif mode == 'col':
            gs.append(g[None])
            widths.append(g.shape[1] // 4)
        else:
            wd = g.shape[1] // 4
            gs.append(jnp.stack([g[:, j * wd:(j + 1) * wd] for j in range(4)]))
            widths.append(wd)
    return gs, widths


def _reduce_begin(G, names, small, tag):
    gs, widths = _slabbed(G, names)
    got, small = _send_other_half(gs, small, "send_other_half_" + tag)
    pairs = [_pair_sum(g, o, "pair_sum_" + n) for n, g, o in zip(names, gs, got)]
    return pairs, widths, small


def _other_half_copies(shapes):
    def copies(src_refs, land_refs, send_sems, recv_sems, probe=False):
        if probe:
            return [None] * len(land_refs)
        x, y, c = _place()
        out = []
        for a, (src, land) in enumerate(zip(src_refs, land_refs)):
            rh = shapes[a][1] // 2
            out.append(pltpu.make_async_remote_copy(
                src_ref=src.at[:, pl.ds((1 - c) * rh, rh), :], dst_ref=land, send_sem=send_sems.at[a],
                recv_sem=recv_sems.at[a], device_id=(x, y, 1 - c), device_id_type=MESH))
        return out
    return copies


def _reduce_end(names, pairs, recv, widths, tag):
    halves = [_chip_sum(p, r, w, "chip_sum_" + n) for n, p, r, w in zip(names, pairs, recv, widths)]
    return dict(zip(names, _share_halves(halves, "share_halves_" + tag)))


def _small_copies():
    def copies(src_refs, land_refs, send_sems, recv_sems, probe=False):
        if probe:
            return [None] * len(_FLIPS)
        x, y, c = _place()
        mine = land_refs[0].at[4 * x + 2 * y + c]
        return [pltpu.make_async_remote_copy(src_ref=mine, dst_ref=mine, send_sem=send_sems.at[k],
                                             recv_sem=recv_sems.at[k], device_id=(x ^ f[0], y ^ f[1], c ^ f[2]),
                                             device_id_type=MESH) for k, f in enumerate(_FLIPS)]
    return copies


def _small_slab(packed):
    me8 = 4 * lax.axis_index("x") + 2 * lax.axis_index("y") + lax.axis_index("c")
    return lax.dynamic_update_slice(jnp.zeros((8,) + packed.shape, F32), packed[None], (me8, 0, 0))


def kernel(*args):
    A = dict(zip(ARG_NAMES, args, strict=True))
    x, mem, target = A['x'][0], A['mem'][0], A['loss_target'][0]

    stages = {'l0_mixer': ['l0_w_in', 'l0_s5_w_glu', 'l0_w_out'],
              'l0_common': [n for n in MATRIX_NAMES if n.startswith(('l0_xa_', 'l0_ffn_'))],
              'l1': [n for n in MATRIX_NAMES if n.startswith('l1_')]}
    W = _gather_weights(A, stages['l0_mixer'] + list(CONVS))
    for n in SMALL_NAMES:
        if n not in CONVS:
            W[n] = A[n]
    flights = {}
    after = W['l0_w_in']
    for stage in ('l0_common', 'l1'):
        flights[stage], after = _gather_weights_start(A, stages[stage], after, stage)
    W['l0_mix_norm'] = W['l0_mix_norm'] + after[0, 0]

    reduce_state = {}

    def early_grads(stage, G):
        names = stages[stage]
        gs, widths = _slabbed(G, names)
        d2d = _other_half_copies([g.shape for g in gs])
        lands = [lax.empty((g.shape[0], g.shape[1] // 2, g.shape[2]), g.dtype) for g in gs]
        d2d_send, d2d_recv, gs, lands, zeros = _split_copy_start(gs, lands, d2d, G['final_norm'], "d2d_start_" + stage)

        def midway(after):
            mine, got = _split_copy_wait(d2d_send, d2d_recv, gs, lands, d2d, after, "d2d_wait_" + stage)
            pairs = [_pair_sum(g, o, "pair_sum_" + n) for n, g, o in zip(names, mine, got)]
            copies = _to_chips_copies([p.shape for p in pairs], widths)
            recv = [lax.empty((4, p.shape[1], w), p.dtype) for p, w in zip(pairs, widths)]
            send_sems, recv_sems, pairs, recv, zeros2 = _split_copy_start(pairs, recv, copies, G['final_norm'],
                                                                          "reduce_start_" + stage)
            reduce_state[stage] = (send_sems, recv_sems, pairs, recv, copies, widths)
            return zeros2[0, 0]

        return zeros[0, 0], midway

    loss_tile, grad_x, G = _local_step(
        x, mem, target, W, later_weights=lambda stage, after: _gather_weights_wait(flights[stage], after, stage),
        early_grads=early_grads)
    loss = lax.psum(loss_tile[0, 0], ("x", "y", "c"))

    small_copies = _small_copies()
    small_send, small_recv, _, small_land, _ = _split_copy_start(
        [], [_small_slab(_pack_small({n: G[n] for n in SMALL_NAMES}))], small_copies, grad_x, "small_start")
    _, mixer_midway = early_grads('l0_mixer', G)
    grads, deltas, new_m, new_v = {}, {}, {}, {}

    def finish(stage, after):
        send_sems, recv_sems, pairs, lands, copies, widths = reduce_state[stage]
        sent, recv = _split_copy_wait(send_sems, recv_sems, pairs, lands, copies, after, "reduce_wait_" + stage)
        g_stage = _reduce_end(stages[stage], sent, recv, widths, stage)
        for n in stages[stage]:
            deltas[n], new_m[n], new_v[n], grads[n] = _adamw(A[n], g_stage[n], A['m_' + n], A['v_' + n], "adamw_" + n)
        return deltas[stages[stage][-1]]

    after = finish('l1', grad_x)
    mixer_midway(after)
    after = finish('l0_common', after)
    after = finish('l0_mixer', after)
    _, (g_small,) = _split_copy_wait(small_send, small_recv, [], small_land, small_copies, after, "small_wait")
    g_small = _unpack_small(_slab_sum(g_small, "sum_small"), {n: G[n].shape for n in SMALL_NAMES})
    me = 2 * lax.axis_index("x") + lax.axis_index("y")
    for n in CONVS:
        wd = A[n].shape[1]
        g_small[n] = lax.dynamic_slice_in_dim(g_small[n], me * wd, wd, axis=1)
    flat_names = [n for n in SMALL_NAMES if n not in CONVS]

    def pack_flat(prefix):
        return _pack_small_flat({n: A[prefix + n] for n in flat_names}, flat_names)

    shapes = {n: A[n].shape for n in flat_names}
    d_s, m_s, v_s, _ = _adamw(pack_flat(''), _pack_small_flat(g_small, flat_names), pack_flat('m_'), pack_flat('v_'),
                              "adamw_small")
    d_s, m_s, v_s = (_unpack_flat(p, shapes, flat_names) for p in (d_s, m_s, v_s))

    for n in WEIGHTS:
        if n in MATRICES:
            continue
        if n in CONVS:
            deltas[n], new_m[n], new_v[n], grads[n] = _adamw(A[n], g_small[n], A['m_' + n], A['v_' + n],
                                                             "adamw_" + n)
        else:
            grads[n] = g_small[n].reshape(A[n].shape)
            deltas[n], new_m[n], new_v[n] = d_s[n], m_s[n], v_s[n]
    return (loss, grad_x[None], *[grads[n] for n in WEIGHTS], *[deltas[n] for n in WEIGHTS],
            *[new_m[n] for n in WEIGHTS], *[new_v[n] for n in WEIGHTS])


def _pack_small_flat(vals, names):
    flat = jnp.concatenate([vals[n].astype(F32).reshape(-1) for n in names])
    rows = -(-flat.shape[0] // (8 * LANES)) * 8
    return jnp.pad(flat, (0, rows * LANES - flat.shape[0])).reshape(rows, LANES)


def _unpack_flat(packed, shapes, names):
    flat = packed.reshape(-1)
    out = {}
    off = 0
    for n in names:
        size = int(np.prod(shapes[n]))
        out[n] = flat[off:off + size].reshape(shapes[n])
        off += size
    return out
```

```python
import functools
import math

import numpy as np
import jax
import jax.numpy as jnp
from jax import lax
from jax.experimental import pallas as pl
from jax.experimental.pallas import tpu as pltpu

F32 = jnp.float32
BF16 = jnp.bfloat16
EPS = 1e-6
MESH = pl.DeviceIdType.MESH

ADAM_LR = 0.001
ADAM_B1 = 0.9
ADAM_B2 = 0.999
ADAM_EPS = 1e-08
ADAM_WD = 0.01
ADAM_STEP = 10

VMEM_LIMIT_BYTES = 56 * 1024 * 1024
MATMUL_VMEM_BYTES = 44 * 1024 * 1024
LANES = 1024

WEIGHTS = ['l0_mix_norm', 'l0_w_in', 'l0_ret_norm', 'l0_s5_lambda_re', 'l0_s5_lambda_im', 'l0_s5_b_re', 'l0_s5_b_im',
           'l0_s5_c_re', 'l0_s5_c_im', 'l0_s5_d', 'l0_s5_log_dt', 'l0_s5_w_glu', 'l0_s5_b_glu', 'l0_w_out',
           'l0_xa_norm', 'l0_mem_norm', 'l0_xa_wq', 'l0_xa_wkv', 'l0_xa_wo', 'l0_ffn_norm', 'l0_ffn_w_up',
           'l0_ffn_conv', 'l0_ffn_w_down', 'l1_mix_norm', 'l1_w_in', 'l1_conv', 'l1_a_log', 'l1_dt_bias',
           'l1_o_norm', 'l1_w_out', 'l1_xa_norm', 'l1_mem_norm', 'l1_xa_wq', 'l1_xa_wkv', 'l1_xa_wo',
           'l1_ffn_norm', 'l1_ffn_w_up', 'l1_ffn_conv', 'l1_ffn_w_down', 'final_norm']
ARG_NAMES = (['x', 'mem'] + WEIGHTS + ['loss_target'] + ['m_' + w for w in WEIGHTS] + ['v_' + w for w in WEIGHTS])

MATRICES = {
    'l0_w_in': 1, 'l0_s5_w_glu': 0, 'l0_w_out': 0, 'l0_xa_wq': 0, 'l0_xa_wkv': 1, 'l0_xa_wo': 0, 'l0_ffn_w_up': 1,
    'l0_ffn_w_down': 0, 'l1_w_in': 1, 'l1_w_out': 0, 'l1_xa_wq': 0, 'l1_xa_wkv': 1, 'l1_xa_wo': 0,
    'l1_ffn_w_up': 1, 'l1_ffn_w_down': 0,
}
CONVS = ('l0_ffn_conv', 'l1_conv', 'l1_ffn_conv')
MATRIX_NAMES = [w for w in WEIGHTS if w in MATRICES]
SMALL_NAMES = [w for w in WEIGHTS if w not in MATRICES]


def _cparams(sem=None):
    return pltpu.CompilerParams(dimension_semantics=sem, vmem_limit_bytes=VMEM_LIMIT_BYTES)


def _pick(n, cands):
    for c in cands:
        if n % c == 0:
            return c
    return n


_NN = ((1,), (0,))
_NT = ((1,), (1,))
_TN = ((0,), (0,))


def _dot(a, b, dims, hi):
    if hi is not None:
        return lax.dot_general(a.astype(F32), b.astype(F32), (dims, ((), ())), precision=hi,
                               preferred_element_type=F32)
    return lax.dot_general(a.astype(BF16), b.astype(BF16), (dims, ((), ())), preferred_element_type=F32)


def _make_mm(hi):
    @jax.custom_vjp
    def nn(a, b):
        return _dot(a, b, _NN, hi)

    def nn_f(a, b):
        return nn(a, b), (a, b)

    def nn_b(r, g):
        a, b = r
        return _dot(g, b, _NT, hi), _dot(a, g, _TN, hi)

    nn.defvjp(nn_f, nn_b)

    @jax.custom_vjp
    def nt(a, b):
        return _dot(a, b, _NT, hi)

    def nt_f(a, b):
        return nt(a, b), (a, b)

    def nt_b(r, g):
        a, b = r
        return _dot(g, b, _NN, hi), _dot(g, a, _TN, hi)

    nt.defvjp(nt_f, nt_b)

    @jax.custom_vjp
    def tn(a, b):
        return _dot(a, b, _TN, hi)

    def tn_f(a, b):
        return tn(a, b), (a, b)

    def tn_b(r, g):
        a, b = r
        return _dot(b, g, _NT, hi), _dot(a, g, _NN, hi)

    tn.defvjp(tn_f, tn_b)
    return nn, nt, tn


mm, mm_nt, mm_tn = _make_mm(None)
mmh, mmh_nt, mmh_tn = _make_mm(lax.Precision.HIGHEST)
mm3, _, _ = _make_mm(lax.Precision.HIGH)


@jax.custom_vjp
def _swap_halves(x):
    return pltpu.roll(x, 64, 1)


def _swap_f(x):
    return pltpu.roll(x, 64, 1), None


def _swap_b(_, g):
    return (pltpu.roll(g, 64, 1),)


_swap_halves.defvjp(_swap_f, _swap_b)


def _silu(x):
    return x * jax.nn.sigmoid(x)


def _rms(x, g):
    return x * lax.rsqrt(jnp.mean(x * x, axis=-1, keepdims=True) + EPS) * g


def _iota(shape, dim):
    return lax.broadcasted_iota(jnp.int32, shape, dim)


def _matmul_tiles(M, N, K, a_bytes, b_bytes, has_res, a_off):
    def divisors(n, cands):
        return [c for c in cands if n % c == 0] or [n]

    fallback = None
    for tk in divisors(K, (K, 2048, 1408, 1024, 512, 256, 128)):
        for tm in divisors(M, (1024, 512, 1408, 256, 128)):
            for tn in divisors(N, (1408, 1024, 512, 256, 128)):
                need = 2 * (tm * tk * a_bytes + tk * tn * b_bytes + (tm * tn * 4 if has_res else 0)) + 3 * tm * tn * 4
                if need > MATMUL_VMEM_BYTES or a_off % tk or a_off % tm:
                    continue
                if tm >= 256 and tn >= 256:
                    return tm, tn, tk
                fallback = fallback or (tm, tn, tk)
    return fallback


def _matmul(a, b, mode="nn", res=None, name="mm", a_cols=None, out_dtype=F32):
    a_off, a_w = (0, a.shape[1]) if a_cols is None else a_cols
    if mode == "nn":
        (M, K), (K2, N) = (a.shape[0], a_w), b.shape
    elif mode == "nt":
        (M, K), (N, K2) = (a.shape[0], a_w), b.shape
    else:
        (K, M), (K2, N) = (a.shape[0], a_w), b.shape
    assert K == K2, (a.shape, b.shape, mode)
    tm, tn, tk = _matmul_tiles(M, N, K, 2 if a.dtype == BF16 else 4, 2 if b.dtype == BF16 else 4, res is not None,
                               a_off)
    nk = K // tk
    dims = {"nn": _NN, "nt": _NT, "tn": _TN}[mode]
    ao = a_off // (tm if mode == "tn" else tk)
    assert ao * (tm if mode == "tn" else tk) == a_off
    if mode == "nn":
        a_spec = pl.BlockSpec((tm, tk), lambda i, j, k: (i, k + ao))
        b_spec = pl.BlockSpec((tk, tn), lambda i, j, k: (k, j))
    elif mode == "nt":
        a_spec = pl.BlockSpec((tm, tk), lambda i, j, k: (i, k + ao))
        b_spec = pl.BlockSpec((tn, tk), lambda i, j, k: (j, k))
    else:
        a_spec = pl.BlockSpec((tk, tm), lambda i, j, k: (k, i + ao))
        b_spec = pl.BlockSpec((tk, tn), lambda i, j, k: (k, j))
    o_spec = pl.BlockSpec((tm, tn), lambda i, j, k: (i, j))
    has_res = res is not None

    def kern(*refs):
        a_ref, b_ref = refs[:2]
        r_ref = refs[2] if has_res else None
        o_ref = refs[3] if has_res else refs[2]
        acc_ref = refs[-1] if nk > 1 else None
        k = pl.program_id(2)
        part = lax.dot_general(a_ref[...].astype(BF16), b_ref[...].astype(BF16), (dims, ((), ())),
                               preferred_element_type=F32)
        if nk == 1:
            o_ref[...] = (part + r_ref[...] if has_res else part).astype(o_ref.dtype)
            return

        @pl.when(k == 0)
        def _():
            acc_ref[...] = part

        @pl.when((k > 0) & (k < nk - 1))
        def _():
            acc_ref[...] += part

        @pl.when(k == nk - 1)
        def _():
            total = acc_ref[...] + part
            o_ref[...] = (total + r_ref[...] if has_res else total).astype(o_ref.dtype)

    in_specs = [a_spec, b_spec] + ([o_spec] if has_res else [])
    ops = (a, b) + ((res,) if has_res else ())
    return pl.pallas_call(
        kern, name=name, grid=(M // tm, N // tn, nk), in_specs=in_specs, out_specs=o_spec,
        out_shape=jax.ShapeDtypeStruct((M, N), out_dtype),
        scratch_shapes=[pltpu.VMEM((tm, tn), F32)] if nk > 1 else [],
        compiler_params=_cparams(("parallel", "parallel", "arbitrary")))(*ops)


def _matmul_cat(pieces, b, mode="nn", res=None, name="mmcat"):
    M = pieces[0].shape[0]
    widths = [p.shape[1] for p in pieces]
    K = sum(widths)
    N = b.shape[1] if mode == "nn" else b.shape[0]
    assert (b.shape[0] if mode == "nn" else b.shape[1]) == K
    tn = _pick(N, (1024, 512, 256, 128))
    a_bytes = 2 if pieces[0].dtype == BF16 else 4
    for tm in (1024, 512, 256, 128):
        need = 2 * (tm * K * a_bytes + K * tn * 2 + (tm * tn * 4 if res is not None else 0)) + 3 * tm * tn * 4
        if M % tm == 0 and need <= MATMUL_VMEM_BYTES:
            break
    npc = len(pieces)
    has_res = res is not None
    dims = _NN if mode == "nn" else _NT

    def kern(*refs):
        b_ref = refs[npc]
        o_ref = refs[-1]
        acc = refs[npc + 1][...] if has_res else None
        off = 0
        for p in range(npc):
            bp = b_ref[off:off + widths[p], :] if mode == "nn" else b_ref[:, off:off + widths[p]]
            t = lax.dot_general(refs[p][...].astype(BF16), bp.astype(BF16), (dims, ((), ())),
                                preferred_element_type=F32)
            acc = t if acc is None else acc + t
            off += widths[p]
        o_ref[...] = acc

    in_specs = [pl.BlockSpec((tm, w), lambda j, i: (i, 0)) for w in widths]
    in_specs.append(pl.BlockSpec((K, tn), lambda j, i: (0, j)) if mode == "nn"
                    else pl.BlockSpec((tn, K), lambda j, i: (j, 0)))
    o_spec = pl.BlockSpec((tm, tn), lambda j, i: (i, j))
    if has_res:
        in_specs.append(o_spec)
    ops = list(pieces) + [b] + ([res] if has_res else [])
    return pl.pallas_call(
        kern, name=name, grid=(N // tn, M // tm), in_specs=in_specs, out_specs=o_spec,
        out_shape=jax.ShapeDtypeStruct((M, N), F32), compiler_params=_cparams(("parallel", "parallel")))(*ops)


def _blk(a, width=None, colblk=0):
    return (a, a.shape[1] if width is None else width, colblk)


def _row_specs(blocked, params, ts):
    specs = []
    for (_, w, cb) in blocked:
        specs.append(pl.BlockSpec((ts, w), functools.partial(lambda i, cb: (i, cb), cb=cb)))
    for p in params:
        specs.append(pl.BlockSpec(p.shape, lambda i: (0, 0)))
    return specs


def _rowwise(fn, blocked, params, out_widths, name, ts=256, out_dtypes=None):
    S = blocked[0][0].shape[0]
    ts = min(ts, S)
    nb, npar = len(blocked), len(params)
    out_dtypes = [F32] * len(out_widths) if out_dtypes is None else out_dtypes

    def kern(*refs):
        vals = [r[...] for r in refs[:nb + npar]]
        outs = fn(*vals)
        for o_ref, o in zip(refs[nb + npar:], outs):
            o_ref[...] = o.astype(o_ref.dtype)

    return pl.pallas_call(
        kern, name=name, grid=(S // ts,), in_specs=_row_specs(blocked, params, ts),
        out_specs=[pl.BlockSpec((ts, w), lambda i: (i, 0)) for w in out_widths],
        out_shape=[jax.ShapeDtypeStruct((S, w), d) for w, d in zip(out_widths, out_dtypes)],
        compiler_params=_cparams(("parallel",)))(*[b[0] for b in blocked], *params)


def _rowwise_bwd(fn, blocked, params, cots, name, blocked_grad=None, param_grad=None, adds=None, ts=256,
                 out_dtypes=None):
    S = blocked[0][0].shape[0]
    ts = min(ts, S)
    cots = [c if isinstance(c, tuple) else _blk(c) for c in cots]
    nb, npar, nc = len(blocked), len(params), len(cots)
    blocked_grad = [True] * nb if blocked_grad is None else blocked_grad
    param_grad = [True] * npar if param_grad is None else param_grad
    adds = {} if adds is None else adds
    bidx = [i for i in range(nb) if blocked_grad[i]]
    pidx = [i for i in range(npar) if param_grad[i]]
    add_keys = sorted(adds)
    n_in = nb + npar + nc + len(add_keys)

    def kern(*refs):
        i = pl.program_id(0)
        xs = [r[...] for r in refs[:nb]]
        ps = [r[...] for r in refs[nb:nb + npar]]
        gs = [r[...] for r in refs[nb + npar:nb + npar + nc]]
        add_vals = {k: refs[nb + npar + nc + n][...] for n, k in enumerate(add_keys)}
        outs = refs[n_in:]

        def f(*diff):
            full_x = list(xs)
            full_p = list(ps)
            for n, ix in enumerate(bidx):
                full_x[ix] = diff[n]
            for n, ix in enumerate(pidx):
                full_p[ix] = diff[len(bidx) + n]
            return tuple(fn(*full_x, *full_p))

        _, vjp = jax.vjp(f, *[xs[ix] for ix in bidx], *[ps[ix] for ix in pidx])
        grads = vjp(tuple(gs))
        for n, ix in enumerate(bidx):
            g = grads[n]
            if ix in add_vals:
                g = g + add_vals[ix]
            outs[n][...] = g.astype(outs[n].dtype)
        for n in range(len(pidx)):
            o_ref = outs[len(bidx) + n]

            @pl.when(i == 0)
            def _(o_ref=o_ref):
                o_ref[...] = jnp.zeros_like(o_ref)

            o_ref[...] += grads[len(bidx) + n]

    in_specs = _row_specs(blocked, params, ts)
    in_specs += _row_specs(cots, [], ts)
    in_specs += [pl.BlockSpec((ts, adds[k].shape[1]), lambda i: (i, 0)) for k in add_keys]
    out_specs = [pl.BlockSpec((ts, blocked[ix][1]), lambda i: (i, 0)) for ix in bidx]
    out_specs += [pl.BlockSpec(params[ix].shape, lambda i: (0, 0)) for ix in pidx]
    out_dtypes = [F32] * len(bidx) if out_dtypes is None else out_dtypes
    out_shape = [jax.ShapeDtypeStruct((S, blocked[ix][1]), d) for ix, d in zip(bidx, out_dtypes)]
    out_shape += [jax.ShapeDtypeStruct(params[ix].shape, F32) for ix in pidx]
    return pl.pallas_call(
        kern, name=name, grid=(S // ts,), in_specs=in_specs, out_specs=out_specs, out_shape=out_shape,
        compiler_params=_cparams(("arbitrary",)))(*[b[0] for b in blocked], *params, *[c[0] for c in cots],
                                                    *[adds[k] for k in add_keys])


def _rms_fn(x, g):
    return (_rms(x, g),)


def _head_norm(o, n_heads, dh):
    outs = []
    for h in range(n_heads):
        oh = o[:, h * dh:(h + 1) * dh]
        outs.append(oh * lax.rsqrt(jnp.mean(oh * oh, axis=-1, keepdims=True) + EPS))
    return outs


def _ret_post_fn(o_raw, gate, ret_norm):
    o = jnp.concatenate(_head_norm(o_raw, 4, 128), axis=1)
    return (o * ret_norm * _silu(gate),)


def _s5_post_fn(y1, y2, u, d, w_glu, b_glu):
    y = y1 - y2 + d * u
    y = jax.nn.gelu(y)
    return (y * jax.nn.sigmoid(mm(y, w_glu) + b_glu),)


def _xattn_fn(q, kv):
    outs = []
    for h in range(4):
        qh = q[:, h * 256:(h + 1) * 256]
        kh = kv[:, h * 256:(h + 1) * 256]
        vh = kv[:, 1024 + h * 256:1024 + (h + 1) * 256]
        s = mm_nt(qh, kh) * (256 ** -0.5)
        s = s - lax.stop_gradient(jnp.max(s, axis=-1, keepdims=True))
        p = jnp.exp(s)
        p = p / jnp.sum(p, axis=-1, keepdims=True)
        outs.append(mm(p, vh))
    return (jnp.concatenate(outs, axis=1),)


def _softplus(x):
    return jnp.maximum(x, 0.0) + jnp.log1p(jnp.exp(-jnp.abs(x)))


def _gdn_gates_fn(pt, a_log_p, dtb_p):
    rows, cols = _iota((128, 1024), 0), _iota((128, 1024), 1)
    e_b = (rows == (cols >> 7)).astype(F32)
    e_a = (rows == (cols >> 7) + 8).astype(F32)
    beta = jax.nn.sigmoid(pt)
    g = -(jnp.exp(a_log_p) * _softplus(pt + dtb_p))
    return mmh(g, e_a), mmh(beta, e_b)


def _gdn_post_fn(o_raw, z, o_norm):
    outs = _head_norm(o_raw, 8, 128)
    o = jnp.concatenate([oh * o_norm for oh in outs], axis=1)
    return (o * _silu(z),)


def _ffn_post(up, gate):
    return _silu(gate) * up


def _shift_down(cur, prev8, sh, row8):
    if sh == 0:
        return cur
    r = pltpu.roll(cur, sh, 0)
    p = pltpu.roll(prev8, sh, 0)
    top = jnp.where(row8 < sh, p, r[0:8])
    if cur.shape[0] == 8:
        return top
    return jnp.concatenate([top, r[8:]], axis=0)


def _shift_up(cur, next8, sh, row8):
    if sh == 0:
        return cur
    ts = cur.shape[0]
    r = pltpu.roll(cur, ts - sh, 0)
    p = pltpu.roll(next8, 8 - sh, 0)
    bot = jnp.where(row8 >= 8 - sh, p, r[ts - 8:])
    return jnp.concatenate([r[:ts - 8], bot], axis=0)


def _conv_rows(cur, prev8, wrows, row8):
    k_w = len(wrows)
    out = None
    for j in range(k_w):
        t = _shift_down(cur, prev8, k_w - 1 - j, row8) * wrows[j]
        out = t if out is None else out + t
    return out


def _conv_specs(x, xoff, w, woff, ts, tc):
    r8 = ts // 8
    return [pl.BlockSpec((ts, tc), functools.partial(lambda i, j, o: (i, j + o), o=xoff)),
            pl.BlockSpec((8, tc), functools.partial(lambda i, j, o: (jnp.maximum(i * r8 - 1, 0), j + o), o=xoff)),
            pl.BlockSpec((w.shape[0], tc), functools.partial(lambda i, j, o: (0, j + o), o=woff))]


def _conv_post(srcs, post, ncol, tc, name, cots=None, ts=256, out_dtype=F32):
    S = srcs[0][0].shape[0]
    ns = len(srcs)
    bwd = cots is not None

    def kern(*refs):
        first = pl.program_id(0) == 0
        row8 = _iota((8, tc), 0)
        cs = []
        for s in range(ns):
            cur_ref, prev_ref, w_ref = refs[3 * s:3 * s + 3]
            prev = jnp.where(first, 0.0, prev_ref[...])
            wrows = [w_ref[j:j + 1, :] for j in range(w_ref.shape[0])]
            cs.append(_conv_rows(cur_ref[...], prev, wrows, row8))
        if bwd:
            g = refs[3 * ns][...]
            _, vjp = jax.vjp(lambda *c: post(*c), *cs)
            for o_ref, d in zip(refs[3 * ns + 1:], vjp(g)):
                o_ref[...] = d
        else:
            refs[3 * ns][...] = post(*cs).astype(refs[3 * ns].dtype)

    in_specs = []
    ops = []
    for (x, xoff, w, woff) in srcs:
        in_specs += _conv_specs(x, xoff, w, woff, ts, tc)
        ops += [x, x, w]
    o_spec = pl.BlockSpec((ts, tc), lambda i, j: (i, j))
    o_shape = jax.ShapeDtypeStruct((S, ncol * tc), F32)
    if bwd:
        in_specs.append(o_spec)
        ops.append(cots)
        out_specs, out_shape = [o_spec] * ns, [o_shape] * ns
    else:
        out_specs, out_shape = o_spec, jax.ShapeDtypeStruct((S, ncol * tc), out_dtype)
    return pl.pallas_call(
        kern, name=name, grid=(S // ts, ncol), in_specs=in_specs, out_specs=out_specs, out_shape=out_shape,
        compiler_params=_cparams(("parallel", "parallel")))(*ops)


def _conv_bwd(dc, x, xoff, w, woff, ncol, tc, name, ts=256):
    S = x.shape[0]
    k_w = w.shape[0]
    r8 = ts // 8
    nblk8 = S // 8
    nrow = S // ts

    def kern(dc_ref, dn_ref, x_ref, xp_ref, w_ref, dx_ref, dw_ref):
        i = pl.program_id(1)
        row8 = _iota((8, tc), 0)
        dcur = dc_ref[...]
        dnext = jnp.where(i == nrow - 1, 0.0, dn_ref[...])
        xcur = x_ref[...]
        xprev = jnp.where(i == 0, 0.0, xp_ref[...])

        @pl.when(i == 0)
        def _():
            dw_ref[...] = jnp.zeros_like(dw_ref)

        dx = None
        for j in range(k_w):
            sh = k_w - 1 - j
            wj = w_ref[j:j + 1, :]
            t = _shift_up(dcur, dnext, sh, row8) * wj
            dx = t if dx is None else dx + t
            dw_ref[j:j + 1, :] += jnp.sum(dcur * _shift_down(xcur, xprev, sh, row8), axis=0, keepdims=True)
        dx_ref[...] = dx.astype(dx_ref.dtype)

    in_specs = [pl.BlockSpec((ts, tc), lambda j, i: (i, j)),
                pl.BlockSpec((8, tc), lambda j, i: (jnp.minimum((i + 1) * r8, nblk8 - 1), j)),
                pl.BlockSpec((ts, tc), functools.partial(lambda j, i, o: (i, j + o), o=xoff)),
                pl.BlockSpec((8, tc), functools.partial(lambda j, i, o: (jnp.maximum(i * r8 - 1, 0), j + o), o=xoff)),
                pl.BlockSpec((k_w, tc), functools.partial(lambda j, i, o: (0, j + o), o=woff))]
    out_specs = [pl.BlockSpec((ts, tc), lambda j, i: (i, j)), pl.BlockSpec((k_w, tc), lambda j, i: (0, j))]
    out_shape = [jax.ShapeDtypeStruct((S, ncol * tc), BF16), jax.ShapeDtypeStruct((k_w, ncol * tc), F32)]
    return pl.pallas_call(
        kern, name=name, grid=(ncol, nrow), in_specs=in_specs, out_specs=out_specs, out_shape=out_shape,
        compiler_params=_cparams(("parallel", "arbitrary")))(dc, dc, x, x, w)


def _conv_post_bwd(srcs, post, ncol, tc, cot, name, ts=256):
    S = srcs[0][0].shape[0]
    ns = len(srcs)
    r8 = ts // 8
    nblk8 = S // 8
    nrow = S // ts

    def kern(*refs):
        i = pl.program_id(1)
        row8 = _iota((8, tc), 0)
        g_ref, gn_ref = refs[4 * ns:4 * ns + 2]
        outs = refs[4 * ns + 2:]
        xs, xps, ws, cs, cns = [], [], [], [], []
        for s in range(ns):
            cur_ref, prev_ref, next_ref, w_ref = refs[4 * s:4 * s + 4]
            xcur = cur_ref[...]
            xprev = jnp.where(i == 0, 0.0, prev_ref[...])
            wrows = [w_ref[j:j + 1, :] for j in range(w_ref.shape[0])]
            xs.append(xcur)
            xps.append(xprev)
            ws.append(wrows)
            cs.append(_conv_rows(xcur, xprev, wrows, row8))
            cns.append(_conv_rows(next_ref[...], xcur[ts - 8:], wrows, row8))
        _, vjp = jax.vjp(lambda *c: post(*c), *cs)
        dcs = vjp(g_ref[...])
        _, vjp_next = jax.vjp(lambda *c: post(*c), *cns)
        dcns = vjp_next(jnp.where(i == nrow - 1, 0.0, gn_ref[...]))
        for s in range(ns):
            dx_ref, dw_ref = outs[2 * s], outs[2 * s + 1]

            @pl.when(i == 0)
            def _(dw_ref=dw_ref):
                dw_ref[...] = jnp.zeros_like(dw_ref)

            k_w = len(ws[s])
            dx = None
            for j in range(k_w):
                sh = k_w - 1 - j
                t = _shift_up(dcs[s], dcns[s], sh, row8) * ws[s][j]
                dx = t if dx is None else dx + t
                dw_ref[j:j + 1, :] += jnp.sum(dcs[s] * _shift_down(xs[s], xps[s], sh, row8), axis=0, keepdims=True)
            dx_ref[...] = dx.astype(dx_ref.dtype)

    def nxt(i):
        return jnp.minimum((i + 1) * r8, nblk8 - 1)

    in_specs, ops = [], []
    for (x, xoff, w, woff) in srcs:
        in_specs += [pl.BlockSpec((ts, tc), functools.partial(lambda j, i, o: (i, j + o), o=xoff)),
                     pl.BlockSpec((8, tc), functools.partial(lambda j, i, o: (jnp.maximum(i * r8 - 1, 0), j + o),
                                                             o=xoff)),
                     pl.BlockSpec((8, tc), functools.partial(lambda j, i, o: (nxt(i), j + o), o=xoff)),
                     pl.BlockSpec((w.shape[0], tc), functools.partial(lambda j, i, o: (0, j + o), o=woff))]
        ops += [x, x, x, w]
    in_specs += [pl.BlockSpec((ts, tc), lambda j, i: (i, j)), pl.BlockSpec((8, tc), lambda j, i: (nxt(i), j))]
    ops += [cot, cot]
    out_specs, out_shape = [], []
    for (x, xoff, w, woff) in srcs:
        out_specs += [pl.BlockSpec((ts, tc), lambda j, i: (i, j)), pl.BlockSpec((w.shape[0], tc), lambda j, i: (0, j))]
        out_shape += [jax.ShapeDtypeStruct((S, ncol * tc), BF16), jax.ShapeDtypeStruct((w.shape[0], ncol * tc), F32)]
    return pl.pallas_call(
        kern, name=name, grid=(ncol, nrow), in_specs=in_specs, out_specs=out_specs, out_shape=out_shape,
        compiler_params=_cparams(("parallel", "arbitrary")))(*ops)


def _ret_tables(S):
    H, C, dh = 4, 128, 128
    lg = jnp.log1p(-jnp.exp2(-5.0 - jnp.arange(H, dtype=F32)))
    idx = jnp.arange(C, dtype=F32)
    diff = idx[:, None] - idx[None, :]
    causal = diff >= 0
    intra = jnp.where(causal, jnp.exp(lg[:, None, None] * jnp.where(causal, diff, 0.0)), 0.0)
    kdec = jnp.broadcast_to(jnp.exp(lg[:, None] * (C - 1 - idx))[:, :, None], (H, C, dh))
    qdec = jnp.broadcast_to(jnp.exp(lg[:, None] * (idx + 1))[:, :, None], (H, C, dh))
    cdec = jnp.broadcast_to(jnp.exp(lg * C)[:, None, None], (H, dh, dh))
    half = dh // 2
    inv = jnp.exp(-math.log(10000.0) * jnp.arange(half, dtype=F32) / half)
    ang = jnp.arange(S).astype(F32)[:, None] * inv[None, :]
    cos, sin = jnp.cos(ang), jnp.sin(ang)
    cosf = jnp.concatenate([cos, cos], axis=1)
    sinf = jnp.concatenate([-sin, sin], axis=1)
    return cosf, sinf, intra, kdec, qdec, cdec


def _ret_chunk(q, k, v, cosf, sinf, intra, kdec, qdec, cdec, state):
    hs = range(len(q))
    qr = [q[h] * cosf + _swap_halves(q[h]) * sinf for h in hs]
    kr = [(k[h] * cosf + _swap_halves(k[h]) * sinf) * (128 ** -0.5) for h in hs]
    scores = [mm_nt(qr[h], kr[h]) * intra[h] for h in hs]
    inner = [mm(scores[h], v[h]) for h in hs]
    kv = [mm_tn(kr[h] * kdec[h], v[h]) for h in hs]
    cross = [mm(qr[h] * qdec[h], state[h]) for h in hs]
    return [inner[h] + cross[h] for h in hs], [state[h] * cdec[h] + kv[h] for h in hs]


RET_H = 4


def _ret_call(proj, tabs, states=None, do=None):
    S = proj.shape[0]
    N = S // 128
    bwd = do is not None

    def nn(n):
        return N - 1 - n if bwd else n

    qkv_spec = pl.BlockSpec((128, 3 * 512), lambda n: (nn(n), 0))
    pos = pl.BlockSpec((128, 128), lambda n: (nn(n), 0))
    tab = pl.BlockSpec((RET_H, 128, 128), lambda n: (0, 0, 0))
    st_spec = pl.BlockSpec((None, RET_H, 128, 128), lambda n: (nn(n), 0, 0, 0))
    o_spec = pl.BlockSpec((128, 512), lambda n: (nn(n), 0))

    def kern(*refs):
        x_ref, c_ref, s_ref, i_ref, kd_ref, qd_ref, cd_ref = refs[:7]
        carry = refs[-1]
        heads = range(RET_H)

        @pl.when(pl.program_id(0) == 0)
        def _():
            carry[...] = jnp.zeros_like(carry)

        def cols(ref, off=0):
            return [ref[:, _hs(off + h)] for h in heads]

        def tabs_of(ref):
            return [ref[h] for h in heads]

        consts = (c_ref[...], s_ref[...], tabs_of(i_ref), tabs_of(kd_ref), tabs_of(qd_ref), tabs_of(cd_ref))
        qkv = (cols(x_ref), cols(x_ref, RET_H), cols(x_ref, 2 * RET_H))
        if bwd:
            sp_ref, do_ref = refs[7:9]
            outs = refs[9:12]
            _, vjp = jax.vjp(lambda q, k, v, s: _ret_chunk(q, k, v, *consts, s), *qkv, tabs_of(sp_ref))
            dq, dk, dv, ds = vjp((cols(do_ref), tabs_of(carry)))
            for h in heads:
                for o_ref, d in zip(outs, (dq[h], dk[h], dv[h])):
                    o_ref[:, _hs(h)] = d.astype(o_ref.dtype)
                carry[h] = ds[h]
        else:
            o_ref, sp_ref = refs[7:9]
            state = tabs_of(carry)
            out, new = _ret_chunk(*qkv, *consts, state)
            for h in heads:
                sp_ref[h] = state[h]
                o_ref[:, _hs(h)] = out[h]
                carry[h] = new[h]

    in_specs = [qkv_spec, pos, pos, tab, tab, tab, tab]
    if bwd:
        in_specs += [st_spec, o_spec]
        out_specs = [o_spec] * 3
        out_shape = [jax.ShapeDtypeStruct((S, 512), BF16)] * 3
        ops = (proj, *tabs, states, do)
    else:
        out_specs = [o_spec, st_spec]
        out_shape = [jax.ShapeDtypeStruct((S, 512), F32), jax.ShapeDtypeStruct((N, RET_H, 128, 128), F32)]
        ops = (proj, *tabs)
    return pl.pallas_call(
        kern, name="ret_bwd" if bwd else "ret_fwd", grid=(N,), in_specs=in_specs, out_specs=out_specs,
        out_shape=out_shape, scratch_shapes=[pltpu.VMEM((RET_H, 128, 128), F32)],
        compiler_params=_cparams(("arbitrary",)))(*ops)


GDN_C = 64
GDN_H = 8


def _unit_lower_inverse(a_mats, eye):
    p = [-a for a in a_mats]
    t = [eye + x for x in p]
    for _ in range(5):
        p = [mm3(x, x) for x in p]
        t = [mm3(y, eye + x) for y, x in zip(t, p)]
    return t


@jax.custom_vjp
def _known_inverse(a_mat, t_mat):
    return t_mat


def _known_inverse_f(a_mat, t_mat):
    return t_mat, t_mat


def _known_inverse_b(t_mat, g):
    return -mmh_tn(t_mat, mmh_nt(g, t_mat)), jnp.zeros_like(t_mat)


_known_inverse.defvjp(_known_inverse_f, _known_inverse_b)


def _gdn_intra(q, k, v, g_b, beta_b, t_known=None):
    c = GDN_C
    hs = range(len(q))
    q = [x * lax.rsqrt(jnp.sum(x * x, axis=-1, keepdims=True) + EPS) * (128 ** -0.5) for x in q]
    k = [x * lax.rsqrt(jnp.sum(x * x, axis=-1, keepdims=True) + EPS) for x in k]
    ri, ci = _iota((c, c), 0), _iota((c, c), 1)
    incl = ri >= ci
    strict = ri > ci
    eye = (ri == ci).astype(F32)
    lower = incl.astype(F32)
    gc_b = [mm3(lower, g) for g in g_b]
    gl_b = [jnp.sum(g, axis=0, keepdims=True) for g in g_b]
    kb = [k[h] * beta_b[h] for h in hs]
    vb = [v[h] * beta_b[h] for h in hs]
    gcc = [g[:, :c] for g in gc_b]
    decay = [jnp.where(incl, jnp.exp(jnp.where(incl, g - g.T, 0.0)), 0.0) for g in gcc]
    a_mat = [jnp.where(strict, mm_nt(kb[h], k[h]) * decay[h], 0.0) for h in hs]
    if t_known is None:
        t_mat = _unit_lower_inverse(a_mat, eye)
    else:
        t_mat = [_known_inverse(a_mat[h], t_known[h]) for h in hs]
    egc = [jnp.exp(g) for g in gc_b]
    w = [mm(t_mat[h], kb[h] * egc[h]) for h in hs]
    u = [mm(t_mat[h], vb[h]) for h in hs]
    qk = [jnp.where(incl, mm_nt(q[h], k[h]) * decay[h], 0.0) for h in hs]
    q_dec = [q[h] * egc[h] for h in hs]
    k_dec = [k[h] * jnp.exp(gl_b[h] - gc_b[h]) for h in hs]
    return w, u, q_dec, k_dec, qk, t_mat


def _gdn_step(w, u, q_dec, k_dec, qk, g_b, state):
    hs = range(len(w))
    gl_s = [jnp.sum(g, axis=0, keepdims=True) for g in g_b]
    ws = [mm(w[h], state[h]) for h in hs]
    qs = [mm(q_dec[h], state[h]) for h in hs]
    v_new = [u[h] - ws[h] for h in hs]
    o = [qs[h] + mm(qk[h], v_new[h]) for h in hs]
    new = [state[h] * jnp.exp(gl_s[h]) + mm_tn(k_dec[h], v_new[h]) for h in hs]
    return o, new


def _hs(h):
    return slice(h * 128, (h + 1) * 128)


def _gdn_intra_call(qkv, g_e, beta_e, cots=None):
    S = qkv.shape[0]
    N = S // GDN_C
    bwd = cots is not None
    row = pl.BlockSpec((GDN_C, 1024), lambda n: (n, 0))
    qkv_spec = pl.BlockSpec((GDN_C, 3072), lambda n: (n, 0))
    qk_spec = pl.BlockSpec((GDN_H, GDN_C, GDN_C), lambda n: (0, n, 0))

    def kern(*refs):
        x_ref, g_ref, b_ref = refs[:3]
        heads = range(GDN_H)

        def cols(ref, off=0):
            return [ref[:, _hs(off + h)] for h in heads]

        args = (cols(x_ref), cols(x_ref, 8), cols(x_ref, 16), cols(g_ref), cols(b_ref))
        if bwd:
            dw_ref, du_ref, dqd_ref, dkd_ref, dqk_ref, dgadd_ref, t_ref = refs[3:10]
            outs = refs[10:]
            t_known = [t_ref[h] for h in heads]
            _, vjp = jax.vjp(lambda *a: _gdn_intra(*a, t_known=t_known)[:5], *args)
            dq, dk, dv, dg, db = vjp((cols(dw_ref), cols(du_ref), cols(dqd_ref), cols(dkd_ref),
                                      [dqk_ref[h] for h in heads]))
            dgadd = cols(dgadd_ref)
            for h in heads:
                for o_ref, d in zip(outs, (dq[h], dk[h], dv[h], dg[h] + dgadd[h], db[h])):
                    o_ref[:, _hs(h)] = d
        else:
            w, u, qd, kd, qk, t_mat = _gdn_intra(*args)
            for h in heads:
                for o_ref, o in zip(refs[3:7], (w[h], u[h], qd[h], kd[h])):
                    o_ref[:, _hs(h)] = o
                refs[7][h] = qk[h]
                refs[8][h] = t_mat[h]

    big = jax.ShapeDtypeStruct((S, 1024), F32)
    sq = jax.ShapeDtypeStruct((GDN_H, S, GDN_C), F32)
    if bwd:
        in_specs = [qkv_spec, row, row, row, row, row, row, qk_spec, row, qk_spec]
        out_specs, out_shape = [row] * 5, [big] * 5
        ops = (qkv, g_e, beta_e) + tuple(cots)
    else:
        in_specs = [qkv_spec, row, row]
        out_specs = [row] * 4 + [qk_spec, qk_spec]
        out_shape = [big] * 4 + [sq, sq]
        ops = (qkv, g_e, beta_e)
    return pl.pallas_call(
        kern, name="gdn_intra_bwd" if bwd else "gdn_intra", grid=(N,), in_specs=in_specs, out_specs=out_specs,
        out_shape=out_shape, compiler_params=_cparams(("parallel",)))(*ops)


def _gdn_pass(w, u, qd, kd, qk, g_e, states=None, do=None):
    S = w.shape[0]
    N = S // GDN_C
    bwd = do is not None

    def nn(n):
        return N - 1 - n if bwd else n

    row = pl.BlockSpec((GDN_C, 1024), lambda n: (nn(n), 0))
    qk_spec = pl.BlockSpec((GDN_H, GDN_C, GDN_C), lambda n: (0, nn(n), 0))
    st_spec = pl.BlockSpec((None, GDN_H, 128, 128), lambda n: (nn(n), 0, 0, 0))

    def kern(*refs):
        w_ref, u_ref, qd_ref, kd_ref, qk_ref, g_ref = refs[:6]
        carry = refs[-1]

        @pl.when(pl.program_id(0) == 0)
        def _():
            carry[...] = jnp.zeros_like(carry)

        heads = range(GDN_H)

        def cols(ref):
            return [ref[:, _hs(h)] for h in heads]

        args = (cols(w_ref), cols(u_ref), cols(qd_ref), cols(kd_ref), [qk_ref[h] for h in heads], cols(g_ref))
        if bwd:
            sp_ref, do_ref = refs[6:8]
            outs = refs[8:14]
            _, vjp = jax.vjp(_gdn_step, *args, [sp_ref[h] for h in heads])
            dw, du, dqd, dkd, dqk, dg, ds = vjp((cols(do_ref), [carry[h] for h in heads]))
            for h in heads:
                for o_ref, d in zip(outs[:4], (dw[h], du[h], dqd[h], dkd[h])):
                    o_ref[:, _hs(h)] = d
                outs[4][h] = dqk[h]
                outs[5][:, _hs(h)] = dg[h]
                carry[h] = ds[h]
        else:
            o_ref, sp_ref = refs[6:8]
            state = [carry[h] for h in heads]
            o, new = _gdn_step(*args, state)
            for h in heads:
                sp_ref[h] = state[h]
                o_ref[:, _hs(h)] = o[h]
                carry[h] = new[h]

    big = jax.ShapeDtypeStruct((S, 1024), F32)
    in_specs = [row, row, row, row, qk_spec, row]
    if bwd:
        in_specs += [st_spec, row]
        out_specs = [row] * 4 + [qk_spec, row]
        out_shape = [big] * 4 + [jax.ShapeDtypeStruct((GDN_H, S, GDN_C), F32), big]
        ops = (w, u, qd, kd, qk, g_e, states, do)
    else:
        out_specs = [row, st_spec]
        out_shape = [big, jax.ShapeDtypeStruct((N, GDN_H, 128, 128), F32)]
        ops = (w, u, qd, kd, qk, g_e)
    return pl.pallas_call(
        kern, name="gdn_pass_bwd" if bwd else "gdn_pass", grid=(N,), in_specs=in_specs, out_specs=out_specs,
        out_shape=out_shape, scratch_shapes=[pltpu.VMEM((GDN_H, 128, 128), F32)],
        compiler_params=_cparams(("arbitrary",)))(*ops)


def _s5_prep_fn(lr, li, ldt, br, bi, cr, ci):
    dt = jnp.exp(ldt)
    mag = jnp.exp(lr * dt)
    a_re = mag * jnp.cos(li * dt)
    a_im = mag * jnp.sin(li * dt)
    den = lr * lr + li * li
    z_re = ((a_re - 1.0) * lr + a_im * li) / den
    z_im = (a_im * lr - (a_re - 1.0) * li) / den
    e1 = ((_iota((512, 32), 0) >> 4) == _iota((512, 32), 1)).astype(F32)
    zr_e = mmh(e1, z_re)
    zi_e = mmh(e1, z_im)
    bb_re = zr_e * br - zi_e * bi
    bb_im = zr_e * bi + zi_e * br
    t1 = ((_iota((64, 2048), 1) & 63) == _iota((64, 2048), 0)).astype(F32)
    m1 = (_iota((512, 2048), 0) >> 4) == (_iota((512, 2048), 1) >> 6)
    bd_re = jnp.where(m1, mmh(bb_re, t1), 0.0)
    bd_im = jnp.where(m1, mmh(bb_im, t1), 0.0)
    t2 = ((_iota((16, 512), 1) & 15) == _iota((16, 512), 0)).astype(F32)
    m2 = (_iota((2048, 512), 0) >> 6) == (_iota((2048, 512), 1) >> 4)
    cd_re = jnp.where(m2, mmh(cr, t2), 0.0)
    cd_im = jnp.where(m2, mmh(ci, t2), 0.0)
    return a_re, a_im, bd_re, bd_im, cd_re, cd_im


_PREP_OUT = [(32, 64), (32, 64), (512, 2048), (512, 2048), (2048, 512), (2048, 512)]


def _s5_prep(params, cots=None):
    bwd = cots is not None

    def kern(*refs):
        vals = [r[...] for r in refs[:7]]
        if bwd:
            gs = tuple(r[...] for r in refs[7:13])
            _, vjp = jax.vjp(_s5_prep_fn, *vals)
            for o_ref, d in zip(refs[13:], vjp(gs)):
                o_ref[...] = d
        else:
            for o_ref, o in zip(refs[7:], _s5_prep_fn(*vals)):
                o_ref[...] = o

    if bwd:
        out_shape = [jax.ShapeDtypeStruct(p.shape, F32) for p in params]
        ops = list(params) + list(cots)
    else:
        out_shape = [jax.ShapeDtypeStruct(s, F32) for s in _PREP_OUT]
        ops = list(params)
    return pl.pallas_call(kern, name="s5_prep_bwd" if bwd else "s5_prep", out_shape=out_shape,
                          compiler_params=_cparams())(*ops)


def _cmul(ar, ai, br, bi):
    return ar * br - ai * bi, ar * bi + ai * br


def _power_table(ar, ai, row8, descending):
    pr, pi = ar, ai
    tr = jnp.zeros(row8.shape, F32)
    ti = jnp.zeros(row8.shape, F32)
    for n in range(8):
        r = 7 - n if descending else n
        tr = jnp.where(row8 == r, pr, tr)
        ti = jnp.where(row8 == r, pi, ti)
        if n < 7:
            pr, pi = _cmul(pr, pi, ar, ai)
    return tr, ti


def _tile_scan(xr, xi, pows, row8, up):
    for d, (pr, pi) in zip((1, 2, 4), pows):
        if up:
            sr = jnp.where(row8 < 8 - d, pltpu.roll(xr, 8 - d, 0), 0.0)
            si = jnp.where(row8 < 8 - d, pltpu.roll(xi, 8 - d, 0), 0.0)
        else:
            sr = jnp.where(row8 >= d, pltpu.roll(xr, d, 0), 0.0)
            si = jnp.where(row8 >= d, pltpu.roll(xi, d, 0), 0.0)
        mr, mi = _cmul(pr, pi, sr, si)
        xr, xi = xr + mr, xi + mi
    return xr, xi


def _pick_row(x, row8, r):
    return jnp.sum(jnp.where(row8 == r, x, 0.0), axis=0, keepdims=True)


SCAN_LB = 512
SCAN_TS = 512


def _scan_fwd(bu_re, bu_im, a_re, a_im):
    S, L = bu_re.shape
    ts, lb = min(SCAN_TS, S), SCAN_LB
    nt = ts // 8

    def kern(br_ref, bi_ref, ar_ref, ai_ref, or_ref, oi_ref, cr_ref, ci_ref):
        @pl.when(pl.program_id(1) == 0)
        def _():
            cr_ref[...] = jnp.zeros_like(cr_ref)
            ci_ref[...] = jnp.zeros_like(ci_ref)

        row8 = _iota((8, lb), 0)
        ar, ai = ar_ref[...], ai_ref[...]
        a2 = _cmul(ar, ai, ar, ai)
        a4 = _cmul(*a2, *a2)
        pows = ((ar, ai), a2, a4)
        tr, ti = _power_table(ar, ai, row8, False)

        def body(i, carry):
            cr, ci = carry
            off = pl.multiple_of(i * 8, 8)
            xr, xi = _tile_scan(br_ref[pl.ds(off, 8), :], bi_ref[pl.ds(off, 8), :], pows, row8, False)
            mr, mi = _cmul(tr, ti, cr, ci)
            xr, xi = xr + mr, xi + mi
            or_ref[pl.ds(off, 8), :] = xr
            oi_ref[pl.ds(off, 8), :] = xi
            return _pick_row(xr, row8, 7), _pick_row(xi, row8, 7)

        cr, ci = lax.fori_loop(0, nt, body, (cr_ref[...], ci_ref[...]))
        cr_ref[...] = cr
        ci_ref[...] = ci

    blk = pl.BlockSpec((ts, lb), lambda j, i: (i, j))
    par = pl.BlockSpec((1, lb), lambda j, i: (0, j))
    return pl.pallas_call(
        kern, name="s5_scan_fwd", grid=(L // lb, S // ts), in_specs=[blk, blk, par, par], out_specs=[blk, blk],
        out_shape=[jax.ShapeDtypeStruct((S, L), F32)] * 2,
        scratch_shapes=[pltpu.VMEM((1, lb), F32), pltpu.VMEM((1, lb), F32)],
        compiler_params=_cparams(("parallel", "arbitrary")))(bu_re, bu_im, a_re, a_im)


def _scan_bwd(dst_re, dst_im, st_re, st_im, a_re, a_im):
    S, L = dst_re.shape
    ts, lb = min(SCAN_TS, S), SCAN_LB
    nt = ts // 8
    nb = S // ts
    r8 = ts // 8

    def kern(dr_ref, di_ref, sr_ref, si_ref, pr_ref, pi_ref, ar_ref, ai_ref, gr_ref, gi_ref, dar_ref, dai_ref,
             cr_ref, ci_ref):
        step = pl.program_id(1)
        blk = nb - 1 - step

        @pl.when(step == 0)
        def _():
            cr_ref[...] = jnp.zeros_like(cr_ref)
            ci_ref[...] = jnp.zeros_like(ci_ref)
            dar_ref[...] = jnp.zeros_like(dar_ref)
            dai_ref[...] = jnp.zeros_like(dai_ref)

        row8 = _iota((8, lb), 0)
        ar, ai = ar_ref[...], ai_ref[...]
        nai = -ai
        a2 = _cmul(ar, nai, ar, nai)
        a4 = _cmul(*a2, *a2)
        pows = ((ar, nai), a2, a4)
        tr, ti = _power_table(ar, nai, row8, True)
        halo_r = jnp.where(blk == 0, 0.0, pr_ref[...])
        halo_i = jnp.where(blk == 0, 0.0, pi_ref[...])

        def body(n, carry):
            cr, ci, acc_r, acc_i = carry
            i = nt - 1 - n
            off = pl.multiple_of(i * 8, 8)
            gr, gi = _tile_scan(dr_ref[pl.ds(off, 8), :], di_ref[pl.ds(off, 8), :], pows, row8, True)
            mr, mi = _cmul(tr, ti, cr, ci)
            gr, gi = gr + mr, gi + mi
            gr_ref[pl.ds(off, 8), :] = gr
            gi_ref[pl.ds(off, 8), :] = gi
            poff = pl.multiple_of(jnp.maximum(i - 1, 0) * 8, 8)
            before_r = jnp.where(i == 0, halo_r, sr_ref[pl.ds(poff, 8), :])
            before_i = jnp.where(i == 0, halo_i, si_ref[pl.ds(poff, 8), :])
            last_r = _pick_row(before_r, row8, 7)
            last_i = _pick_row(before_i, row8, 7)
            spr = jnp.where(row8 >= 1, pltpu.roll(sr_ref[pl.ds(off, 8), :], 1, 0), last_r)
            spi = jnp.where(row8 >= 1, pltpu.roll(si_ref[pl.ds(off, 8), :], 1, 0), last_i)
            acc_r = acc_r + gr * spr + gi * spi
            acc_i = acc_i + gi * spr - gr * spi
            return _pick_row(gr, row8, 0), _pick_row(gi, row8, 0), acc_r, acc_i

        zero = jnp.zeros((8, lb), F32)
        cr, ci, acc_r, acc_i = lax.fori_loop(0, nt, body, (cr_ref[...], ci_ref[...], zero, zero))
        cr_ref[...] = cr
        ci_ref[...] = ci
        dar_ref[...] += jnp.sum(acc_r, axis=0, keepdims=True)
        dai_ref[...] += jnp.sum(acc_i, axis=0, keepdims=True)

    blk = pl.BlockSpec((ts, lb), lambda j, i: (nb - 1 - i, j))
    halo = pl.BlockSpec((8, lb), lambda j, i: (jnp.maximum((nb - 1 - i) * r8 - 1, 0), j))
    par = pl.BlockSpec((1, lb), lambda j, i: (0, j))
    return pl.pallas_call(
        kern, name="s5_scan_bwd", grid=(L // lb, nb), in_specs=[blk, blk, blk, blk, halo, halo, par, par],
        out_specs=[blk, blk, par, par],
        out_shape=[jax.ShapeDtypeStruct((S, L), F32)] * 2 + [jax.ShapeDtypeStruct((1, L), F32)] * 2,
        scratch_shapes=[pltpu.VMEM((1, lb), F32), pltpu.VMEM((1, lb), F32)],
        compiler_params=_cparams(("parallel", "arbitrary")))(dst_re, dst_im, st_re, st_im, st_re, st_im, a_re, a_im)


def _loss_grad(x, target, gain, ts=256):
    S, D = x.shape

    def kern(x_ref, t_ref, g_ref, loss_ref, dx_ref, dg_ref):
        i = pl.program_id(0)
        tgt = t_ref[...]

        def f(xv, gv):
            err = _rms(xv, gv) - tgt
            return 0.5 * jnp.mean(err * err, axis=-1, keepdims=True)

        rowloss, vjp = jax.vjp(f, x_ref[...], g_ref[...])
        dx, dg = vjp(jnp.ones_like(rowloss))
        dx_ref[...] = dx

        @pl.when(i == 0)
        def _():
            loss_ref[...] = jnp.zeros_like(loss_ref)
            dg_ref[...] = jnp.zeros_like(dg_ref)

        loss_ref[...] += jnp.broadcast_to(jnp.sum(rowloss, axis=0, keepdims=True), loss_ref.shape)
        dg_ref[...] += dg

    row = pl.BlockSpec((ts, D), lambda i: (i, 0))
    return pl.pallas_call(
        kern, name="loss_grad", grid=(S // ts,), in_specs=[row, row, pl.BlockSpec((1, D), lambda i: (0, 0))],
        out_specs=[pl.BlockSpec((8, 128), lambda i: (0, 0)), row, pl.BlockSpec((1, D), lambda i: (0, 0))],
        out_shape=[jax.ShapeDtypeStruct((8, 128), F32), jax.ShapeDtypeStruct((S, D), F32),
                   jax.ShapeDtypeStruct((1, D), F32)],
        compiler_params=_cparams(("arbitrary",)))(x, target, gain)


def _rms_fwd(x, g, name):
    return _rowwise(_rms_fn, [_blk(x)], [g], [x.shape[1]], name, out_dtypes=[BF16])[0]


def _rms_bwd(x, g, dy, name, add=None):
    return _rowwise_bwd(_rms_fn, [_blk(x)], [g], [dy], name, adds=None if add is None else {0: add})


FFN_TC = 1408


def _common_fwd(x, mem, P, L):
    hx = _rms_fwd(x, P['xa_norm'], L + "xa_norm")
    q = _matmul(hx, P['xa_wq'], name=L + "xa_q")
    memn = _rms_fwd(mem, P['mem_norm'], L + "mem_norm")
    kv = _matmul(memn, P['xa_wkv'], name=L + "xa_kv")
    att = _rowwise(_xattn_fn, [_blk(q)], [kv], [1024], L + "xattn", out_dtypes=[BF16])[0]
    x2 = _matmul(att, P['xa_wo'], res=x, name=L + "xa_o")
    hf = _rms_fwd(x2, P['ffn_norm'], L + "ffn_norm")
    hu = _matmul(hf, P['ffn_w_up'], name=L + "ffn_up")
    cw = P['ffn_conv']
    act = _conv_post([(hu, 0, cw, 0), (hu, 2, cw, 2)], _ffn_post, 2, FFN_TC, L + "ffn_conv", out_dtype=BF16)
    x3 = _matmul(act, P['ffn_w_down'], res=x2, name=L + "ffn_down")
    return x3, (x, mem, hx, q, memn, kv, att, x2, hf, hu, act)


def _common_bwd(saved, dx3, P, L, midway=None):
    x, mem, hx, q, memn, kv, att, x2, hf, hu, act = saved
    G = {}
    dact = _matmul(dx3, P['ffn_w_down'], "nt", name=L + "ffn_down_dx")
    G['ffn_w_down'] = _matmul(act, dx3, "tn", name=L + "ffn_down_dw")
    cw = P['ffn_conv']
    dhu_u, dcw_u, dhu_g, dcw_g = _conv_post_bwd([(hu, 0, cw, 0), (hu, 2, cw, 2)], _ffn_post, 2, FFN_TC, dact,
                                                L + "ffn_conv_bwd")
    G['ffn_conv'] = jnp.concatenate([dcw_u, dcw_g], axis=1)
    dhf = _matmul_cat([dhu_u, dhu_g], P['ffn_w_up'], "nt", name=L + "ffn_up_dx")
    G['ffn_w_up'] = jnp.concatenate([_matmul(hf, dhu_u, "tn", name=L + "ffn_up_dw_up"),
                                     _matmul(hf, dhu_g, "tn", name=L + "ffn_up_dw_gate")], axis=1)
    dx2, G['ffn_norm'] = _rms_bwd(x2, P['ffn_norm'], dhf, L + "ffn_norm_bwd", add=dx3)
    if midway is not None:
        P = dict(P, xa_wo=P['xa_wo'] + midway(dx2).astype(P['xa_wo'].dtype))
    datt = _matmul(dx2, P['xa_wo'], "nt", name=L + "xa_o_dx")
    G['xa_wo'] = _matmul(att, dx2, "tn", name=L + "xa_o_dw")
    dq, dkv = _rowwise_bwd(_xattn_fn, [_blk(q)], [kv], [datt], L + "xattn_bwd", out_dtypes=[BF16])
    dhx = _matmul(dq, P['xa_wq'], "nt", name=L + "xa_q_dx")
    G['xa_wq'] = _matmul(hx, dq, "tn", name=L + "xa_q_dw")
    dmemn = _matmul(dkv, P['xa_wkv'], "nt", name=L + "xa_kv_dx")
    G['xa_wkv'] = _matmul(memn, dkv, "tn", name=L + "xa_kv_dw")
    _, G['mem_norm'] = _rms_bwd(mem, P['mem_norm'], dmemn, L + "mem_norm_bwd")
    dx, G['xa_norm'] = _rms_bwd(x, P['xa_norm'], dhx, L + "xa_norm_bwd", add=dx2)
    return dx, G


U_COLS = (2048, 512)


def _even_fwd(x, P):
    S = x.shape[0]
    h0 = _rms_fwd(x, P['mix_norm'], "l0_mix_norm")
    proj = _matmul(h0, P['w_in'], name="l0_in")
    tabs = _ret_tables(S)
    o_raw, rstates = _ret_call(proj, tabs)
    o = _rowwise(_ret_post_fn, [_blk(o_raw), _blk(proj, 512, 3)], [P['ret_norm']], [512], "l0_ret_post",
                 out_dtypes=[BF16])[0]
    prep_in = (P['s5_lambda_re'], P['s5_lambda_im'], P['s5_log_dt'], P['s5_b_re'], P['s5_b_im'], P['s5_c_re'],
               P['s5_c_im'])
    a_re, a_im, bd_re, bd_im, cd_re, cd_im = _s5_prep(prep_in)
    a_re_f, a_im_f = a_re.reshape(1, 2048), a_im.reshape(1, 2048)
    bu_re = _matmul(proj, bd_re, name="l0_s5_bu_re", a_cols=U_COLS)
    bu_im = _matmul(proj, bd_im, name="l0_s5_bu_im", a_cols=U_COLS)
    st_re, st_im = _scan_fwd(bu_re, bu_im, a_re_f, a_im_f)
    y1 = _matmul(st_re, cd_re, name="l0_s5_y_re")
    y2 = _matmul(st_im, cd_im, name="l0_s5_y_im")
    yg = _rowwise(_s5_post_fn, [_blk(y1), _blk(y2), _blk(proj, 512, 4)],
                  [P['s5_d'], P['s5_w_glu'], P['s5_b_glu']], [512], "l0_s5_post", out_dtypes=[BF16])[0]
    x1 = _matmul_cat([o, yg], P['w_out'], "nn", res=x, name="l0_out")
    saved = (x, h0, proj, tabs, o_raw, rstates, prep_in, a_re_f, a_im_f, bd_re, bd_im, cd_re, cd_im, st_re, st_im,
             y1, y2, o, yg)
    return x1, saved


def _even_bwd(saved, dx1, P, midway=None):
    (x, h0, proj, tabs, o_raw, rstates, prep_in, a_re_f, a_im_f, bd_re, bd_im, cd_re, cd_im, st_re, st_im, y1, y2,
     o, yg) = saved
    G = {}
    dmerged = _matmul(dx1, P['w_out'], "nt", name="l0_out_dx")
    G['w_out'] = jnp.concatenate([_matmul(o, dx1, "tn", name="l0_out_dw_ret"),
                                  _matmul(yg, dx1, "tn", name="l0_out_dw_s5")], axis=0)
    do_raw, dgate, G['ret_norm'] = _rowwise_bwd(
        _ret_post_fn, [_blk(o_raw), _blk(proj, 512, 3)], [P['ret_norm']], [_blk(dmerged, 512, 0)], "l0_ret_post_bwd",
        out_dtypes=[F32, BF16])
    dq, dk, dv = _ret_call(proj, tabs, states=rstates, do=do_raw)
    if midway is not None:
        P = dict(P, s5_w_glu=P['s5_w_glu'] + midway(dq))
    dy1, dy2, du_a, G['s5_d'], G['s5_w_glu'], G['s5_b_glu'] = _rowwise_bwd(
        _s5_post_fn, [_blk(y1), _blk(y2), _blk(proj, 512, 4)], [P['s5_d'], P['s5_w_glu'], P['s5_b_glu']],
        [_blk(dmerged, 512, 1)], "l0_s5_post_bwd", out_dtypes=[BF16, BF16, F32])
    dst_re = _matmul(dy1, cd_re, "nt", name="l0_s5_y_re_dx")
    dcd_re = _matmul(st_re, dy1, "tn", name="l0_s5_y_re_dw")
    dst_im = _matmul(dy2, cd_im, "nt", name="l0_s5_y_im_dx")
    dcd_im = _matmul(st_im, dy2, "tn", name="l0_s5_y_im_dw")
    dbu_re, dbu_im, da_re, da_im = _scan_bwd(dst_re, dst_im, st_re, st_im, a_re_f, a_im_f)
    du = _matmul(dbu_re, bd_re, "nt", res=du_a, name="l0_s5_bu_re_dx")
    du = _matmul(dbu_im, bd_im, "nt", res=du, name="l0_s5_bu_im_dx", out_dtype=BF16)
    dbd_re = _matmul(proj, dbu_re, "tn", name="l0_s5_bu_re_dw", a_cols=U_COLS)
    dbd_im = _matmul(proj, dbu_im, "tn", name="l0_s5_bu_im_dw", a_cols=U_COLS)
    dprep = _s5_prep(prep_in, cots=(da_re.reshape(32, 64), da_im.reshape(32, 64), dbd_re, dbd_im, dcd_re, dcd_im))
    for n, d in zip(('s5_lambda_re', 's5_lambda_im', 's5_log_dt', 's5_b_re', 's5_b_im', 's5_c_re', 's5_c_im'), dprep):
        G[n] = d
    pieces = [dq, dk, dv, dgate, du]
    dh0 = _matmul_cat(pieces, P['w_in'], "nt", name="l0_in_dx")
    G['w_in'] = jnp.concatenate([_matmul(h0, p, "tn", name="l0_in_dw_%d" % n) for n, p in enumerate(pieces)], axis=1)
    dx, G['mix_norm'] = _rms_bwd(x, P['mix_norm'], dh0, "l0_mix_norm_bwd", add=dx1)
    return dx, G


def _odd_fwd(x, P):
    h1 = _rms_fwd(x, P['mix_norm'], "l1_mix_norm")
    pm = _matmul(h1, P['w_main'], name="l1_in_main")
    pt = _matmul(h1, P['w_tail'], name="l1_in_tail")
    qkv = _conv_post([(pm, 0, P['conv'], 0)], _silu, 3, 1024, "l1_conv")
    g_e, beta_e = _rowwise(_gdn_gates_fn, [_blk(pt)], [P['a_log_p'], P['dtb_p']], [1024, 1024], "l1_gdn_gates")
    w, u, qd, kd, qk, tinv = _gdn_intra_call(qkv, g_e, beta_e)
    o_raw, gstates = _gdn_pass(w, u, qd, kd, qk, g_e)
    og = _rowwise(_gdn_post_fn, [_blk(o_raw), _blk(pm, 1024, 3)], [P['o_norm']], [1024], "l1_gdn_post",
                  out_dtypes=[BF16])[0]
    x1 = _matmul(og, P['w_out'], res=x, name="l1_out")
    return x1, (x, h1, pm, pt, qkv, g_e, beta_e, w, u, qd, kd, qk, tinv, o_raw, gstates, og)


def _odd_bwd(saved, dx1, P):
    x, h1, pm, pt, qkv, g_e, beta_e, w, u, qd, kd, qk, tinv, o_raw, gstates, og = saved
    G = {}
    dog = _matmul(dx1, P['w_out'], "nt", name="l1_out_dx")
    G['w_out'] = _matmul(og, dx1, "tn", name="l1_out_dw")
    do_raw, dz, G['o_norm'] = _rowwise_bwd(_gdn_post_fn, [_blk(o_raw), _blk(pm, 1024, 3)], [P['o_norm']], [dog],
                                           "l1_gdn_post_bwd", out_dtypes=[F32, BF16])
    dw, du, dqd, dkd, dqk, dg_pass = _gdn_pass(w, u, qd, kd, qk, g_e, states=gstates, do=do_raw)
    dqkv = _gdn_intra_call(qkv, g_e, beta_e, cots=(dw, du, dqd, dkd, dqk, dg_pass, tinv))
    dg_e, dbeta_e = dqkv[3], dqkv[4]
    dpt, G['a_log_p'], G['dtb_p'] = _rowwise_bwd(_gdn_gates_fn, [_blk(pt)], [P['a_log_p'], P['dtb_p']],
                                                 [dg_e, dbeta_e], "l1_gdn_gates_bwd", out_dtypes=[BF16])
    pieces, dcw = [], []
    for part in range(3):
        dxp, dwp = _conv_post_bwd([(pm, part, P['conv'], part)], _silu, 1, 1024, dqkv[part],
                                  "l1_conv_bwd_%d" % part)
        pieces.append(dxp)
        dcw.append(dwp)
    G['conv'] = jnp.concatenate(dcw, axis=1)
    pieces += [dz, dpt]
    dh1 = _matmul_cat(pieces, P['w_all'], "nt", name="l1_in_dx")
    G['w_all'] = jnp.concatenate([_matmul(h1, p, "tn", name="l1_in_dw_%d" % n) for n, p in enumerate(pieces)], axis=1)
    dx, G['mix_norm'] = _rms_bwd(x, P['mix_norm'], dh1, "l1_mix_norm_bwd", add=dx1)
    return dx, G


def _row(v):
    return v.reshape(1, -1)


def _local_step(x, mem, target, W, later_weights=None, early_grads=None):
    P0 = {
        'mix_norm': _row(W['l0_mix_norm']), 'w_in': W['l0_w_in'], 'ret_norm': _row(W['l0_ret_norm']),
        's5_lambda_re': W['l0_s5_lambda_re'], 's5_lambda_im': W['l0_s5_lambda_im'],
        's5_log_dt': W['l0_s5_log_dt'].reshape(32, 1),
        's5_b_re': W['l0_s5_b_re'].reshape(512, 64), 's5_b_im': W['l0_s5_b_im'].reshape(512, 64),
        's5_c_re': W['l0_s5_c_re'].reshape(2048, 16), 's5_c_im': W['l0_s5_c_im'].reshape(2048, 16),
        's5_d': _row(W['l0_s5_d']), 's5_w_glu': W['l0_s5_w_glu'].astype(F32), 's5_b_glu': _row(W['l0_s5_b_glu']),
        'w_out': W['l0_w_out'],
    }
    def common(L):
        return {'xa_norm': _row(W[L + 'xa_norm']), 'mem_norm': _row(W[L + 'mem_norm']), 'xa_wq': W[L + 'xa_wq'],
                'xa_wkv': W[L + 'xa_wkv'], 'xa_wo': W[L + 'xa_wo'], 'ffn_norm': _row(W[L + 'ffn_norm']),
                'ffn_w_up': W[L + 'ffn_w_up'], 'ffn_conv': W[L + 'ffn_conv'], 'ffn_w_down': W[L + 'ffn_w_down']}

    x1, s_even = _even_fwd(x, P0)
    if later_weights is not None:
        W = dict(W, **later_weights('l0_common', x1))
    C0 = common('l0_')
    x3, s_c0 = _common_fwd(x1, mem, C0, "l0_")

    if later_weights is not None:
        W = dict(W, **later_weights('l1', x3))
    w_in1 = W['l1_w_in']
    pad8 = jnp.zeros((8,), F32)
    w_all = jnp.pad(w_in1, ((0, 0), (0, 112)))
    P1 = {
        'mix_norm': _row(W['l1_mix_norm']), 'w_main': w_in1[:, :4096], 'w_tail': w_all[:, 4096:], 'w_all': w_all,
        'conv': W['l1_conv'],
        'a_log_p': _row(jnp.concatenate([pad8, W['l1_a_log'], jnp.zeros((112,), F32)])),
        'dtb_p': _row(jnp.concatenate([pad8, W['l1_dt_bias'], jnp.zeros((112,), F32)])),
        'o_norm': _row(W['l1_o_norm']), 'w_out': W['l1_w_out'],
    }
    C1 = common('l1_')
    x4, s_odd = _odd_fwd(x3, P1)
    x6, s_c1 = _common_fwd(x4, mem, C1, "l1_")
    loss_tile, dx6, d_final = _loss_grad(x6, target, _row(W['final_norm']))

    G = {'final_norm': d_final.reshape(-1)}
    dx4, g = _common_bwd(s_c1, dx6, C1, "l1_")
    for k, v in g.items():
        G['l1_' + k] = v
    dx3, g = _odd_bwd(s_odd, dx4, P1)
    G['l1_mix_norm'] = g['mix_norm']
    G['l1_w_in'] = g['w_all'][:, :4112]
    G['l1_conv'] = g['conv']
    G['l1_a_log'] = g['a_log_p'][0, 8:16]
    G['l1_dt_bias'] = g['dtb_p'][0, 8:16]
    G['l1_o_norm'] = g['o_norm']
    G['l1_w_out'] = g['w_out']
    midway = None
    if early_grads is not None:
        zero, midway = early_grads('l1', G)
        C0 = dict(C0, ffn_w_down=C0['ffn_w_down'] + zero.astype(C0['ffn_w_down'].dtype))
    dx1, g = _common_bwd(s_c0, dx3, C0, "l0_", midway=midway)
    for k, v in g.items():
        G['l0_' + k] = v
    if early_grads is not None:
        zero, midway = early_grads('l0_common', G)
        P0 = dict(P0, w_out=P0['w_out'] + zero.astype(P0['w_out'].dtype))
    dx0, g = _even_bwd(s_even, dx1, P0, midway=midway)
    for k, v in g.items():
        G['l0_' + k] = v
    return loss_tile, dx0, G


ANY = pl.BlockSpec(memory_space=pl.ANY)


def _place():
    return lax.axis_index("x"), lax.axis_index("y"), lax.axis_index("c")


def _my_chip():
    return 2 * lax.axis_index("x") + lax.axis_index("y")


def _chip_peers(x, y):
    return [(1 - x, y), (x, 1 - y), (1 - x, 1 - y)]


def _half(ref, mode, shard, j, h, split):
    r, w = shard
    rh = r // 2 if split else r
    h = h if split else 0
    if mode == 'row':
        return ref.at[pl.ds(j * r + h * rh, rh), :]
    if mode == 'col':
        return ref.at[pl.ds(h * rh, rh), pl.ds(j * w, w)]
    return ref.at[j, pl.ds(h * rh, rh), :]


def _place_shard(shard, mode, name):
    r, w = shard.shape
    dtype = BF16 if mode != 'tap' else shard.dtype
    if mode == 'tap':
        mode = 'slab'
    tr = _row_tile(r, w)
    nb = r // tr

    def kern(s_ref, o_ref):
        o_ref[...] = s_ref[...].astype(o_ref.dtype)

    if mode == 'row':
        full, o_spec = (4 * r, w), pl.BlockSpec((tr, w), lambda i: (_my_chip() * nb + i, 0))
    elif mode == 'col':
        full, o_spec = (r, 4 * w), pl.BlockSpec((tr, w), lambda i: (i, _my_chip()))
    else:
        full, o_spec = (4, r, w), pl.BlockSpec((None, tr, w), lambda i: (_my_chip(), i, 0))
    return pl.pallas_call(kern, name=name, grid=(nb,), in_specs=[pl.BlockSpec((tr, w), lambda i: (i, 0))],
                          out_specs=o_spec, out_shape=jax.ShapeDtypeStruct(full, dtype),
                          compiler_params=_cparams(("parallel",)))(shard)


def _gather_placed(fulls, modes, shards, splits):
    n = len(fulls)

    def body(*refs):
        outs = refs[n:2 * n]
        send_sems, recv_sems = refs[2 * n:]
        x, y, c = _place()
        peers = _chip_peers(x, y)
        me = 2 * x + y

        def win(a, j, h):
            return _half(outs[a], modes[a], shards[a], j, h, splits[a])

        def copy(a, k, j, h, to):
            return pltpu.make_async_remote_copy(src_ref=win(a, j, h), dst_ref=win(a, j, h),
                                                send_sem=send_sems.at[6 * a + k], recv_sem=recv_sems.at[6 * a + k],
                                                device_id=to, device_id_type=MESH)

        over_ici = [copy(a, k, me, c, (p[0], p[1], c)) for a in range(n) for k, p in enumerate(peers)]
        for cp in over_ici:
            cp.start()
        passed = []
        for a in range(n):
            for k, p in enumerate(peers):
                j = 2 * p[0] + p[1]
                copy(a, k, j, c, (p[0], p[1], c)).wait_recv()
                if splits[a]:
                    fwd = copy(a, 3 + k, j, c, (x, y, 1 - c))
                    fwd.start()
                    passed.append(fwd)
        for a in range(n):
            if splits[a]:
                for k, p in enumerate(peers):
                    copy(a, 3 + k, 2 * p[0] + p[1], 1 - c, (x, y, 1 - c)).wait_recv()
        for cp in over_ici + passed:
            cp.wait_send()

    return pl.pallas_call(
        body, name="gather_weights", in_specs=[ANY] * n, out_specs=[ANY] * n,
        out_shape=[jax.ShapeDtypeStruct(f.shape, f.dtype) for f in fulls],
        input_output_aliases={a: a for a in range(n)},
        scratch_shapes=[pltpu.SemaphoreType.DMA((6 * n,)), pltpu.SemaphoreType.DMA((6 * n,))],
    )(*fulls)


_FLIPS = [(dx, dy, dc) for dx in (0, 1) for dy in (0, 1) for dc in (0, 1) if (dx, dy, dc) != (0, 0, 0)]


def _send_other_half(gs, small, name):
    n = len(gs)

    def body(*refs):
        ins, outs = refs[:n], refs[n + 1:2 * n + 1]
        small_ref = refs[2 * n + 1]
        send_sems, recv_sems, small_send, small_recv = refs[2 * n + 2:]
        x, y, c = _place()
        me = 4 * x + 2 * y + c

        def peer(f):
            return (x ^ f[0], y ^ f[1], c ^ f[2])

        def small_copy(k, slab, to):
            return pltpu.make_async_remote_copy(src_ref=small_ref.at[slab], dst_ref=small_ref.at[slab],
                                                send_sem=small_send.at[k], recv_sem=small_recv.at[k], device_id=to,
                                                device_id_type=MESH)

        cps = []
        for a in range(n):
            rh = gs[a].shape[1] // 2
            cps.append(pltpu.make_async_remote_copy(
                src_ref=ins[a].at[:, pl.ds((1 - c) * rh, rh), :], dst_ref=outs[a], send_sem=send_sems.at[a],
                recv_sem=recv_sems.at[a], device_id=(x, y, 1 - c), device_id_type=MESH))
        smalls = [small_copy(k, me, peer(f)) for k, f in enumerate(_FLIPS)]
        for cp in cps + smalls:
            cp.start()
        for cp in cps:
            cp.wait()
        for k, f in enumerate(_FLIPS):
            p = peer(f)
            small_copy(k, 4 * p[0] + 2 * p[1] + p[2], p).wait_recv()
        for cp in smalls:
            cp.wait_send()

    outs = pl.pallas_call(
        body, name=name, in_specs=[ANY] * (n + 1), out_specs=[ANY] * (n + 1),
        out_shape=[jax.ShapeDtypeStruct((g.shape[0], g.shape[1] // 2, g.shape[2]), g.dtype) for g in gs]
        + [jax.ShapeDtypeStruct(small.shape, small.dtype)],
        input_output_aliases={n: n},
        scratch_shapes=[pltpu.SemaphoreType.DMA((n,)), pltpu.SemaphoreType.DMA((n,)),
                        pltpu.SemaphoreType.DMA((7,)), pltpu.SemaphoreType.DMA((7,))],
    )(*gs, small)
    return outs[:n], outs[n]


def _send_to_chips(ps, widths):
    n = len(ps)

    def body(*refs):
        ins, outs = refs[:n], refs[n:2 * n]
        send_sems, recv_sems = refs[2 * n:]
        x, y, c = _place()
        peers = _chip_peers(x, y)
        me = 2 * x + y

        def src(a, j):
            if ps[a].shape[0] == 4:
                return ins[a].at[j]
            return ins[a].at[0, :, pl.ds(j * widths[a], widths[a])]

        def copy(a, k, j, dst_slab, to):
            return pltpu.make_async_remote_copy(src_ref=src(a, j), dst_ref=outs[a].at[dst_slab],
                                                send_sem=send_sems.at[3 * a + k], recv_sem=recv_sems.at[3 * a + k],
                                                device_id=(to[0], to[1], c), device_id_type=MESH)

        sends = [copy(a, k, 2 * p[0] + p[1], me, p) for a in range(n) for k, p in enumerate(peers)]
        for cp in sends:
            cp.start()
        for a in range(n):
            for k, p in enumerate(peers):
                copy(a, k, me, 2 * p[0] + p[1], p).wait_recv()
        for cp in sends:
            cp.wait_send()

    return pl.pallas_call(
        body, name="send_to_chips", in_specs=[ANY] * n, out_specs=[ANY] * n,
        out_shape=[jax.ShapeDtypeStruct((4, p.shape[1], w), p.dtype) for p, w in zip(ps, widths)],
        scratch_shapes=[pltpu.SemaphoreType.DMA((3 * n,)), pltpu.SemaphoreType.DMA((3 * n,))],
    )(*ps)


def _share_halves(bufs, name):
    n = len(bufs)

    def body(*refs):
        outs = refs[n:2 * n]
        send_sems, recv_sems = refs[2 * n:]
        x, y, c = _place()
        sends, waits = [], []
        for a in range(n):
            rh = bufs[a].shape[0] // 2
            mine = outs[a].at[pl.ds(c * rh, rh), :]
            other = outs[a].at[pl.ds((1 - c) * rh, rh), :]
            sends.append(pltpu.make_async_remote_copy(src_ref=mine, dst_ref=mine, send_sem=send_sems.at[a],
                                                      recv_sem=recv_sems.at[a], device_id=(x, y, 1 - c),
                                                      device_id_type=MESH))
            waits.append(pltpu.make_async_remote_copy(src_ref=mine, dst_ref=other, send_sem=send_sems.at[a],
                                                      recv_sem=recv_sems.at[a], device_id=(x, y, 1 - c),
                                                      device_id_type=MESH))
        for cp in sends:
            cp.start()
        for cp in waits:
            cp.wait()

    return pl.pallas_call(
        body, name=name, in_specs=[ANY] * n, out_specs=[ANY] * n,
        out_shape=[jax.ShapeDtypeStruct(b.shape, b.dtype) for b in bufs],
        input_output_aliases={a: a for a in range(n)},
        scratch_shapes=[pltpu.SemaphoreType.DMA((n,)), pltpu.SemaphoreType.DMA((n,))],
    )(*bufs)


def _gather_all(mine):
    flips = [(dx, dy, dc) for dx in (0, 1) for dy in (0, 1) for dc in (0, 1) if (dx, dy, dc) != (0, 0, 0)]

    def body(x_ref, out_ref, send_sems, recv_sems, local_sem):
        x, y, c = _place()
        me = 4 * x + 2 * y + c

        def peer(f):
            return (x ^ f[0], y ^ f[1], c ^ f[2])

        def copy(k, slab, to):
            return pltpu.make_async_remote_copy(src_ref=x_ref, dst_ref=out_ref.at[slab], send_sem=send_sems.at[k],
                                                recv_sem=recv_sems.at[k], device_id=to, device_id_type=MESH)

        own = pltpu.make_async_copy(x_ref, out_ref.at[me], local_sem)
        own.start()
        sends = [copy(k, me, peer(f)) for k, f in enumerate(flips)]
        for s in sends:
            s.start()
        for k, f in enumerate(flips):
            p = peer(f)
            copy(k, 4 * p[0] + 2 * p[1] + p[2], p).wait_recv()
        for s in sends:
            s.wait_send()
        own.wait()

    return pl.pallas_call(
        body, name="gather_all", in_specs=[ANY], out_specs=ANY,
        out_shape=jax.ShapeDtypeStruct((8,) + mine.shape, mine.dtype),
        scratch_shapes=[pltpu.SemaphoreType.DMA((7,)), pltpu.SemaphoreType.DMA((7,)), pltpu.SemaphoreType.DMA],
    )(mine)


TILE_BYTES = 2 * 1024 * 1024


def _row_tile(rows, width=1024):
    for t in (512, 352, 256, 176, 128, 64, 32, 16, 8):
        if rows % t == 0 and t * width * 4 <= TILE_BYTES:
            return t
    return rows


def _pair_sum(g, got, name):
    ns, r, w = g.shape
    rh = r // 2
    tr = _row_tile(rh, w)
    nb = rh // tr

    def kern(g_ref, o_ref, out_ref):
        out_ref[...] = (g_ref[...] + o_ref[...]).astype(BF16)

    return pl.pallas_call(
        kern, name=name, grid=(ns, nb),
        in_specs=[pl.BlockSpec((None, tr, w), lambda j, i: (j, lax.axis_index("c") * nb + i, 0)),
                  pl.BlockSpec((None, tr, w), lambda j, i: (j, i, 0))],
        out_specs=pl.BlockSpec((None, tr, w), lambda j, i: (j, i, 0)),
        out_shape=jax.ShapeDtypeStruct((ns, rh, w), BF16),
        compiler_params=_cparams(("parallel", "parallel")))(g, got)


def _chip_sum(pair, recv, w, name):
    rh = pair.shape[1]
    tr = _row_tile(rh, w)
    nb = rh // tr

    def kern(own_ref, r1_ref, r2_ref, r3_ref, out_ref):
        acc = own_ref[...].astype(F32)
        for r_ref in (r1_ref, r2_ref, r3_ref):
            acc = acc + r_ref[...].astype(F32)
        out_ref[...] = acc

    if pair.shape[0] == 4:
        own_spec = pl.BlockSpec((None, tr, w), lambda i: (_my_chip(), i, 0))
    else:
        own_spec = pl.BlockSpec((None, tr, w), lambda i: (0, i, _my_chip()))
    recv_specs = [pl.BlockSpec((None, tr, w), functools.partial(lambda i, d: ((_my_chip() + d) % 4, i, 0), d=d))
                  for d in (1, 2, 3)]
    return pl.pallas_call(
        kern, name=name, grid=(nb,), in_specs=[own_spec] + recv_specs,
        out_specs=pl.BlockSpec((tr, w), lambda i: (lax.axis_index("c") * nb + i, 0)),
        out_shape=jax.ShapeDtypeStruct((2 * rh, w), F32), compiler_params=_cparams(("parallel",)))(pair, recv, recv, recv)


def _slab_sum(slabs, name):
    n, R, w = slabs.shape
    tr = _row_tile(R)

    def kern(s_ref, o_ref):
        acc = s_ref[0].astype(F32)
        for k in range(1, n):
            acc = acc + s_ref[k].astype(F32)
        o_ref[...] = acc

    return pl.pallas_call(
        kern, name=name, grid=(R // tr,), in_specs=[pl.BlockSpec((n, tr, w), lambda i: (0, i, 0))],
        out_specs=pl.BlockSpec((tr, w), lambda i: (i, 0)), out_shape=jax.ShapeDtypeStruct((R, w), F32),
        compiler_params=_cparams(("parallel",)))(slabs)


def _adamw(w, g, m, v, name):
    R, C = w.shape
    tr = _pick(R, (256, 128, 64, 32, 16, 8))

    def kern(w_ref, g_ref, m_ref, v_ref, d_ref, nm_ref, nv_ref, g_out_ref):
        gv = g_ref[...]
        g_out_ref[...] = gv
        m2 = ADAM_B1 * m_ref[...] + (1.0 - ADAM_B1) * gv
        v2 = ADAM_B2 * v_ref[...] + (1.0 - ADAM_B2) * jnp.square(gv)
        m_hat = m2 / (1.0 - ADAM_B1 ** ADAM_STEP)
        v_hat = v2 / (1.0 - ADAM_B2 ** ADAM_STEP)
        d_ref[...] = -ADAM_LR * (m_hat / (jnp.sqrt(v_hat) + ADAM_EPS) + ADAM_WD * w_ref[...])
        nm_ref[...] = m2
        nv_ref[...] = v2

    spec = pl.BlockSpec((tr, C), lambda i: (i, 0))
    return pl.pallas_call(
        kern, name=name, grid=(R // tr,), in_specs=[spec] * 4, out_specs=[spec] * 4,
        out_shape=[jax.ShapeDtypeStruct((R, C), F32)] * 4, compiler_params=_cparams(("parallel",)))(w, g, m, v)


def _pack_small(vals):
    flat = jnp.concatenate([vals[n].astype(F32).reshape(-1) for n in SMALL_NAMES])
    rows = -(-flat.shape[0] // (8 * LANES)) * 8
    return jnp.pad(flat, (0, rows * LANES - flat.shape[0])).reshape(rows, LANES)


def _unpack_small(packed, shapes):
    flat = packed.reshape(-1)
    out = {}
    off = 0
    for n in SMALL_NAMES:
        size = int(np.prod(shapes[n]))
        out[n] = flat[off:off + size].reshape(shapes[n])
        off += size
    return out


HBM = pl.BlockSpec(memory_space=pltpu.HBM)
SEM = pl.BlockSpec(memory_space=pltpu.SEMAPHORE)
DATAFLOW = pltpu.SideEffectType.DATAFLOW_SIDE_EFFECTING


def _in_hbm(a):
    return pltpu.with_memory_space_constraint(a, pltpu.HBM)


def _split_copy_start(srcs, lands, copies, after, name):
    ns, nl = len(srcs), len(lands)
    ncopy = len(copies(list(srcs), list(lands), None, None, probe=True))

    def body(*refs):
        src_refs, land_refs = refs[:ns], refs[ns:ns + nl]
        send_sems, recv_sems = refs[ns + nl + 1:ns + nl + 3]
        token = refs[-1]
        for cp in copies(src_refs, land_refs, send_sems, recv_sems):
            cp.start()
        token[...] = jnp.zeros_like(token)

    outs = pl.pallas_call(
        body, name=name,
        out_shape=(pltpu.SemaphoreType.DMA((ncopy,)), pltpu.SemaphoreType.DMA((ncopy,)),
                   *[pltpu.HBM(a.shape, a.dtype) for a in srcs], *[pltpu.HBM(a.shape, a.dtype) for a in lands],
                   jax.ShapeDtypeStruct((8, 128), F32)),
        in_specs=[HBM] * (ns + nl) + [ANY],
        out_specs=(SEM, SEM, *[HBM] * (ns + nl), pl.BlockSpec(memory_space=pltpu.VMEM)),
        input_output_aliases={i: 2 + i for i in range(ns + nl)},
        compiler_params=pltpu.CompilerParams(has_side_effects=DATAFLOW),
    )(*[_in_hbm(a) for a in srcs], *[_in_hbm(a) for a in lands], after)
    return outs[0], outs[1], outs[2:2 + ns], outs[2 + ns:2 + ns + nl], outs[-1]


def _split_copy_wait(send_sems, recv_sems, srcs, lands, copies, after, name):
    ns, nl = len(srcs), len(lands)

    def body(*refs):
        src_refs, land_refs = refs[:ns], refs[ns:ns + nl]
        send_ref, recv_ref = refs[ns + nl:ns + nl + 2]
        for cp in copies(src_refs, land_refs, send_ref, recv_ref):
            cp.wait_send()
            cp.wait_recv()

    outs = pl.pallas_call(
        body, name=name,
        out_shape=tuple(pltpu.HBM(a.shape, a.dtype) for a in list(srcs) + list(lands)),
        in_specs=[HBM] * (ns + nl) + [SEM, SEM, ANY], out_specs=tuple([HBM] * (ns + nl)),
        input_output_aliases={i: i for i in range(ns + nl)},
        compiler_params=pltpu.CompilerParams(has_side_effects=DATAFLOW),
    )(*srcs, *lands, send_sems, recv_sems, after)
    return outs[:ns], outs[ns:]


def _matrix_mode(n):
    return 'slab' if n == 'l1_w_in' else ('row' if MATRICES[n] == 0 else 'col')


def _placed(A, names):
    modes = ['slab' if n in CONVS else _matrix_mode(n) for n in names]
    fulls = [_place_shard(A[n], 'tap' if n in CONVS else m, "place_" + n) for n, m in zip(names, modes)]
    return fulls, modes


def _assembled(names, modes, outs):
    return {n: jnp.concatenate([o[j] for j in range(4)], axis=1) if m == 'slab' else o
            for n, m, o in zip(names, modes, outs)}


def _gather_weights(A, names):
    fulls, modes = _placed(A, names)
    outs = _gather_placed(fulls, modes, [A[n].shape for n in names], [n not in CONVS for n in names])
    return _assembled(names, modes, outs)


def _whole_shard_copies(modes, shards):
    def copies(src_refs, land_refs, send_sems, recv_sems, probe=False):
        if probe:
            return [None] * (3 * len(land_refs))
        x, y, c = _place()
        me = 2 * x + y
        out = []
        for a, ref in enumerate(land_refs):
            for k, p in enumerate(_chip_peers(x, y)):
                out.append(pltpu.make_async_remote_copy(
                    src_ref=_half(ref, modes[a], shards[a], me, 0, False),
                    dst_ref=_half(ref, modes[a], shards[a], me, 0, False),
                    send_sem=send_sems.at[3 * a + k], recv_sem=recv_sems.at[3 * a + k],
                    device_id=(p[0], p[1], c), device_id_type=MESH))
        return out
    return copies


def _gather_weights_start(A, names, after, tag):
    fulls, modes = _placed(A, names)
    copies = _whole_shard_copies(modes, [A[n].shape for n in names])
    send_sems, recv_sems, _, lands, zeros = _split_copy_start([], fulls, copies, after, "gather_start_" + tag)
    return (send_sems, recv_sems, lands, copies, names, modes), zeros


def _gather_weights_wait(state, after, tag):
    send_sems, recv_sems, lands, copies, names, modes = state
    _, outs = _split_copy_wait(send_sems, recv_sems, [], lands, copies, after, "gather_wait_" + tag)
    return _assembled(names, modes, outs)


def _to_chips_copies(pair_shapes, widths):
    def copies(src_refs, land_refs, send_sems, recv_sems, probe=False):
        if probe:
            return [None] * (3 * len(land_refs))
        x, y, c = _place()
        me = 2 * x + y
        out = []
        for a, (src, land) in enumerate(zip(src_refs, land_refs)):
            for k, p in enumerate(_chip_peers(x, y)):
                j = 2 * p[0] + p[1]
                part = src.at[j] if pair_shapes[a][0] == 4 else src.at[0, :, pl.ds(j * widths[a], widths[a])]
                out.append(pltpu.make_async_remote_copy(
                    src_ref=part, dst_ref=land.at[me], send_sem=send_sems.at[3 * a + k],
                    recv_sem=recv_sems.at[3 * a + k], device_id=(p[0], p[1], c), device_id_type=MESH))
        return out
    return copies


def _slabbed(G, names):
    gs, widths = [], []
    for n in names:
        g = G[n]
        mode = _matrix_mode(n)
        if mode == 'row':
            gs.append(g.reshape(4, g.shape[0] // 4, g.shape[1]))
            widths.append(g.shape[1])
        elif mode == 'col':
            gs.append(g[None])
            widths.append(g.shape[1] // 4)
        else:
            wd = g.shape[1] // 4
            gs.append(jnp.stack([g[:, j * wd:(j + 1) * wd] for j in range(4)]))
            widths.append(wd)
    return gs, widths


def _reduce_begin(G, names, small, tag):
    gs, widths = _slabbed(G, names)
    got, small = _send_other_half(gs, small, "send_other_half_" + tag)
    pairs = [_pair_sum(g, o, "pair_sum_" + n) for n, g, o in zip(names, gs, got)]
    return pairs, widths, small


def _other_half_copies(shapes):
    def copies(src_refs, land_refs, send_sems, recv_sems, probe=False):
        if probe:
            return [None] * len(land_refs)
        x, y, c = _place()
        out = []
        for a, (src, land) in enumerate(zip(src_refs, land_refs)):
            rh = shapes[a][1] // 2
            out.append(pltpu.make_async_remote_copy(
                src_ref=src.at[:, pl.ds((1 - c) * rh, rh), :], dst_ref=land, send_sem=send_sems.at[a],
                recv_sem=recv_sems.at[a], device_id=(x, y, 1 - c), device_id_type=MESH))
        return out
    return copies


def _reduce_end(names, pairs, recv, widths, tag):
    halves = [_chip_sum(p, r, w, "chip_sum_" + n) for n, p, r, w in zip(names, pairs, recv, widths)]
    return dict(zip(names, _share_halves(halves, "share_halves_" + tag)))


def _small_copies():
    def copies(src_refs, land_refs, send_sems, recv_sems, probe=False):
        if probe:
            return [None] * len(_FLIPS)
        x, y, c = _place()
        mine = land_refs[0].at[4 * x + 2 * y + c]
        return [pltpu.make_async_remote_copy(src_ref=mine, dst_ref=mine, send_sem=send_sems.at[k],
                                             recv_sem=recv_sems.at[k], device_id=(x ^ f[0], y ^ f[1], c ^ f[2]),
                                             device_id_type=MESH) for k, f in enumerate(_FLIPS)]
    return copies


def _small_slab(packed):
    me8 = 4 * lax.axis_index("x") + 2 * lax.axis_index("y") + lax.axis_index("c")
    return lax.dynamic_update_slice(jnp.zeros((8,) + packed.shape, F32), packed[None], (me8, 0, 0))


def kernel(*args):
    A = dict(zip(ARG_NAMES, args, strict=True))
    x, mem, target = A['x'][0], A['mem'][0], A['loss_target'][0]

    stages = {'l0_mixer': ['l0_w_in', 'l0_s5_w_glu', 'l0_w_out'],
              'l0_common': [n for n in MATRIX_NAMES if n.startswith(('l0_xa_', 'l0_ffn_'))],
              'l1': [n for n in MATRIX_NAMES if n.startswith('l1_')]}
    W = _gather_weights(A, stages['l0_mixer'] + list(CONVS))
    for n in SMALL_NAMES:
        if n not in CONVS:
            W[n] = A[n]
    flights = {}
    after = W['l0_w_in']
    for stage in ('l0_common', 'l1'):
        flights[stage], after = _gather_weights_start(A, stages[stage], after, stage)
    W['l0_mix_norm'] = W['l0_mix_norm'] + after[0, 0]

    reduce_state, start_tokens = {}, {}

    def early_grads(stage, G, after=None):
        names = stages[stage]
        gs, widths = _slabbed(G, names)
        d2d = _other_half_copies([g.shape for g in gs])
        lands = [lax.empty((g.shape[0], g.shape[1] // 2, g.shape[2]), g.dtype) for g in gs]
        d2d_send, d2d_recv, gs, lands, zeros = _split_copy_start(gs, lands, d2d, G['final_norm'] if after is None else after,
                                                                 "d2d_start_" + stage)
        start_tokens[stage] = zeros

        def midway(after):
            mine, got = _split_copy_wait(d2d_send, d2d_recv, gs, lands, d2d, after, "d2d_wait_" + stage)
            pairs = [_pair_sum(g, o, "pair_sum_" + n) for n, g, o in zip(names, mine, got)]
            copies = _to_chips_copies([p.shape for p in pairs], widths)
            recv = [lax.empty((4, p.shape[1], w), p.dtype) for p, w in zip(pairs, widths)]
            send_sems, recv_sems, pairs, recv, zeros2 = _split_copy_start(pairs, recv, copies, G['final_norm'],
                                                                          "reduce_start_" + stage)
            reduce_state[stage] = (send_sems, recv_sems, pairs, recv, copies, widths)
            start_tokens[stage] = zeros2
            return zeros2[0, 0]

        return zeros[0, 0], midway

    loss_tile, grad_x, G = _local_step(
        x, mem, target, W, later_weights=lambda stage, after: _gather_weights_wait(flights[stage], after, stage),
        early_grads=early_grads)
    loss = lax.psum(loss_tile[0, 0], ("x", "y", "c"))

    small_copies = _small_copies()
    small_send, small_recv, _, small_land, started = _split_copy_start(
        [], [_small_slab(_pack_small({n: G[n] for n in SMALL_NAMES}))], small_copies, grad_x, "small_start")
    _, mixer_midway = early_grads('l0_mixer', G, after=started)
    grads, deltas, new_m, new_v = {}, {}, {}, {}

    def finish(stage, after):
        send_sems, recv_sems, pairs, lands, copies, widths = reduce_state[stage]
        sent, recv = _split_copy_wait(send_sems, recv_sems, pairs, lands, copies, after, "reduce_wait_" + stage)
        g_stage = _reduce_end(stages[stage], sent, recv, widths, stage)
        for n in stages[stage]:
            deltas[n], new_m[n], new_v[n], grads[n] = _adamw(A[n], g_stage[n], A['m_' + n], A['v_' + n], "adamw_" + n)
        return deltas[stages[stage][-1]]

    after = finish('l1', start_tokens['l0_mixer'])
    mixer_midway(after)
    finish('l0_common', start_tokens['l0_mixer'])
    after = finish('l0_mixer', deltas[stages['l0_common'][-1]])
    _, (g_small,) = _split_copy_wait(small_send, small_recv, [], small_land, small_copies, after, "small_wait")
    g_small = _unpack_small(_slab_sum(g_small, "sum_small"), {n: G[n].shape for n in SMALL_NAMES})
    me = 2 * lax.axis_index("x") + lax.axis_index("y")
    for n in CONVS:
        wd = A[n].shape[1]
        g_small[n] = lax.dynamic_slice_in_dim(g_small[n], me * wd, wd, axis=1)
    flat_names = [n for n in SMALL_NAMES if n not in CONVS]

    def pack_flat(prefix):
        return _pack_small_flat({n: A[prefix + n] for n in flat_names}, flat_names)

    shapes = {n: A[n].shape for n in flat_names}
    d_s, m_s, v_s, _ = _adamw(pack_flat(''), _pack_small_flat(g_small, flat_names), pack_flat('m_'), pack_flat('v_'),
                              "adamw_small")
    d_s, m_s, v_s = (_unpack_flat(p, shapes, flat_names) for p in (d_s, m_s, v_s))

    for n in WEIGHTS:
        if n in MATRICES:
            continue
        if n in CONVS:
            deltas[n], new_m[n], new_v[n], grads[n] = _adamw(A[n], g_small[n], A['m_' + n], A['v_' + n],
                                                             "adamw_" + n)
        else:
            grads[n] = g_small[n].reshape(A[n].shape)
            deltas[n], new_m[n], new_v[n] = d_s[n], m_s[n], v_s[n]
    return (loss, grad_x[None], *[grads[n] for n in WEIGHTS], *[deltas[n] for n in WEIGHTS],
            *[new_m[n] for n in WEIGHTS], *[new_v[n] for n in WEIGHTS])


def _pack_small_flat(vals, names):
    flat = jnp.concatenate([vals[n].astype(F32).reshape(-1) for n in names])
    rows = -(-flat.shape[0] // (8 * LANES)) * 8
    return jnp.pad(flat, (0, rows * LANES - flat.shape[0])).reshape(rows, LANES)


def _unpack_flat(packed, shapes, names):
    flat = packed.reshape(-1)
    out = {}
    off = 0
    for n in names:
        size = int(np.prod(shapes[n]))
        out[n] = flat[off:off + size].reshape(shapes[n])
        off += size
    return out
```

```python
import functools
import math

import numpy as np
import jax
import jax.numpy as jnp
from jax import lax
from jax.experimental import pallas as pl
from jax.experimental.pallas import tpu as pltpu

F32 = jnp.float32
BF16 = jnp.bfloat16
EPS = 1e-6
MESH = pl.DeviceIdType.MESH

ADAM_LR = 0.001
ADAM_B1 = 0.9
ADAM_B2 = 0.999
ADAM_EPS = 1e-08
ADAM_WD = 0.01
ADAM_STEP = 10

VMEM_LIMIT_BYTES = 56 * 1024 * 1024
MATMUL_VMEM_BYTES = 44 * 1024 * 1024
LANES = 1024

WEIGHTS = ['l0_mix_norm', 'l0_w_in', 'l0_ret_norm', 'l0_s5_lambda_re', 'l0_s5_lambda_im', 'l0_s5_b_re', 'l0_s5_b_im',
           'l0_s5_c_re', 'l0_s5_c_im', 'l0_s5_d', 'l0_s5_log_dt', 'l0_s5_w_glu', 'l0_s5_b_glu', 'l0_w_out',
           'l0_xa_norm', 'l0_mem_norm', 'l0_xa_wq', 'l0_xa_wkv', 'l0_xa_wo', 'l0_ffn_norm', 'l0_ffn_w_up',
           'l0_ffn_conv', 'l0_ffn_w_down', 'l1_mix_norm', 'l1_w_in', 'l1_conv', 'l1_a_log', 'l1_dt_bias',
           'l1_o_norm', 'l1_w_out', 'l1_xa_norm', 'l1_mem_norm', 'l1_xa_wq', 'l1_xa_wkv', 'l1_xa_wo',
           'l1_ffn_norm', 'l1_ffn_w_up', 'l1_ffn_conv', 'l1_ffn_w_down', 'final_norm']
ARG_NAMES = (['x', 'mem'] + WEIGHTS + ['loss_target'] + ['m_' + w for w in WEIGHTS] + ['v_' + w for w in WEIGHTS])

MATRICES = {
    'l0_w_in': 1, 'l0_s5_w_glu': 0, 'l0_w_out': 0, 'l0_xa_wq': 0, 'l0_xa_wkv': 1, 'l0_xa_wo': 0, 'l0_ffn_w_up': 1,
    'l0_ffn_w_down': 0, 'l1_w_in': 1, 'l1_w_out': 0, 'l1_xa_wq': 0, 'l1_xa_wkv': 1, 'l1_xa_wo': 0,
    'l1_ffn_w_up': 1, 'l1_ffn_w_down': 0,
}
CONVS = ('l0_ffn_conv', 'l1_conv', 'l1_ffn_conv')
MATRIX_NAMES = [w for w in WEIGHTS if w in MATRICES]
SMALL_NAMES = [w for w in WEIGHTS if w not in MATRICES]


def _cparams(sem=None):
    return pltpu.CompilerParams(dimension_semantics=sem, vmem_limit_bytes=VMEM_LIMIT_BYTES)


def _pick(n, cands):
    for c in cands:
        if n % c == 0:
            return c
    return n


_NN = ((1,), (0,))
_NT = ((1,), (1,))
_TN = ((0,), (0,))


def _dot(a, b, dims, hi):
    if hi is not None:
        return lax.dot_general(a.astype(F32), b.astype(F32), (dims, ((), ())), precision=hi,
                               preferred_element_type=F32)
    return lax.dot_general(a.astype(BF16), b.astype(BF16), (dims, ((), ())), preferred_element_type=F32)


def _make_mm(hi):
    @jax.custom_vjp
    def nn(a, b):
        return _dot(a, b, _NN, hi)

    def nn_f(a, b):
        return nn(a, b), (a, b)

    def nn_b(r, g):
        a, b = r
        return _dot(g, b, _NT, hi), _dot(a, g, _TN, hi)

    nn.defvjp(nn_f, nn_b)

    @jax.custom_vjp
    def nt(a, b):
        return _dot(a, b, _NT, hi)

    def nt_f(a, b):
        return nt(a, b), (a, b)

    def nt_b(r, g):
        a, b = r
        return _dot(g, b, _NN, hi), _dot(g, a, _TN, hi)

    nt.defvjp(nt_f, nt_b)

    @jax.custom_vjp
    def tn(a, b):
        return _dot(a, b, _TN, hi)

    def tn_f(a, b):
        return tn(a, b), (a, b)

    def tn_b(r, g):
        a, b = r
        return _dot(b, g, _NT, hi), _dot(a, g, _NN, hi)

    tn.defvjp(tn_f, tn_b)
    return nn, nt, tn


mm, mm_nt, mm_tn = _make_mm(None)
mmh, mmh_nt, mmh_tn = _make_mm(lax.Precision.HIGHEST)
mm3, mm3_nt, mm3_tn = _make_mm(lax.Precision.HIGH)


@jax.custom_vjp
def _swap_halves(x):
    return pltpu.roll(x, 64, 1)


def _swap_f(x):
    return pltpu.roll(x, 64, 1), None


def _swap_b(_, g):
    return (pltpu.roll(g, 64, 1),)


_swap_halves.defvjp(_swap_f, _swap_b)


def _silu(x):
    return x * jax.nn.sigmoid(x)


def _rms(x, g):
    return x * lax.rsqrt(jnp.mean(x * x, axis=-1, keepdims=True) + EPS) * g


def _iota(shape, dim):
    return lax.broadcasted_iota(jnp.int32, shape, dim)


def _matmul_tiles(M, N, K, a_bytes, b_bytes, has_res, a_off):
    def divisors(n, cands):
        return [c for c in cands if n % c == 0] or [n]

    fallback = None
    for tk in divisors(K, (K, 2048, 1408, 1024, 512, 256, 128)):
        for tm in divisors(M, (1024, 512, 1408, 256, 128)):
            for tn in divisors(N, (1408, 1024, 512, 256, 128)):
                need = 2 * (tm * tk * a_bytes + tk * tn * b_bytes + (tm * tn * 4 if has_res else 0)) + 3 * tm * tn * 4
                if need > MATMUL_VMEM_BYTES or a_off % tk or a_off % tm:
                    continue
                if tm >= 256 and tn >= 256:
                    return tm, tn, tk
                fallback = fallback or (tm, tn, tk)
    return fallback


def _matmul(a, b, mode="nn", res=None, name="mm", a_cols=None, out_dtype=F32):
    a_off, a_w = (0, a.shape[1]) if a_cols is None else a_cols
    if mode == "nn":
        (M, K), (K2, N) = (a.shape[0], a_w), b.shape
    elif mode == "nt":
        (M, K), (N, K2) = (a.shape[0], a_w), b.shape
    else:
        (K, M), (K2, N) = (a.shape[0], a_w), b.shape
    assert K == K2, (a.shape, b.shape, mode)
    tm, tn, tk = _matmul_tiles(M, N, K, 2 if a.dtype == BF16 else 4, 2 if b.dtype == BF16 else 4, res is not None,
                               a_off)
    nk = K // tk
    dims = {"nn": _NN, "nt": _NT, "tn": _TN}[mode]
    ao = a_off // (tm if mode == "tn" else tk)
    assert ao * (tm if mode == "tn" else tk) == a_off
    if mode == "nn":
        a_spec = pl.BlockSpec((tm, tk), lambda i, j, k: (i, k + ao))
        b_spec = pl.BlockSpec((tk, tn), lambda i, j, k: (k, j))
    elif mode == "nt":
        a_spec = pl.BlockSpec((tm, tk), lambda i, j, k: (i, k + ao))
        b_spec = pl.BlockSpec((tn, tk), lambda i, j, k: (j, k))
    else:
        a_spec = pl.BlockSpec((tk, tm), lambda i, j, k: (k, i + ao))
        b_spec = pl.BlockSpec((tk, tn), lambda i, j, k: (k, j))
    o_spec = pl.BlockSpec((tm, tn), lambda i, j, k: (i, j))
    has_res = res is not None

    def kern(*refs):
        a_ref, b_ref = refs[:2]
        r_ref = refs[2] if has_res else None
        o_ref = refs[3] if has_res else refs[2]
        acc_ref = refs[-1] if nk > 1 else None
        k = pl.program_id(2)
        part = lax.dot_general(a_ref[...].astype(BF16), b_ref[...].astype(BF16), (dims, ((), ())),
                               preferred_element_type=F32)
        if nk == 1:
            o_ref[...] = (part + r_ref[...] if has_res else part).astype(o_ref.dtype)
            return

        @pl.when(k == 0)
        def _():
            acc_ref[...] = part

        @pl.when((k > 0) & (k < nk - 1))
        def _():
            acc_ref[...] += part

        @pl.when(k == nk - 1)
        def _():
            total = acc_ref[...] + part
            o_ref[...] = (total + r_ref[...] if has_res else total).astype(o_ref.dtype)

    in_specs = [a_spec, b_spec] + ([o_spec] if has_res else [])
    ops = (a, b) + ((res,) if has_res else ())
    return pl.pallas_call(
        kern, name=name, grid=(M // tm, N // tn, nk), in_specs=in_specs, out_specs=o_spec,
        out_shape=jax.ShapeDtypeStruct((M, N), out_dtype),
        scratch_shapes=[pltpu.VMEM((tm, tn), F32)] if nk > 1 else [],
        compiler_params=_cparams(("parallel", "parallel", "arbitrary")))(*ops)


def _matmul_cat(pieces, b, mode="nn", res=None, name="mmcat"):
    M = pieces[0].shape[0]
    widths = [p.shape[1] for p in pieces]
    K = sum(widths)
    N = b.shape[1] if mode == "nn" else b.shape[0]
    assert (b.shape[0] if mode == "nn" else b.shape[1]) == K
    tn = _pick(N, (1024, 512, 256, 128))
    a_bytes = 2 if pieces[0].dtype == BF16 else 4
    for tm in (1024, 512, 256, 128):
        need = 2 * (tm * K * a_bytes + K * tn * 2 + (tm * tn * 4 if res is not None else 0)) + 3 * tm * tn * 4
        if M % tm == 0 and need <= MATMUL_VMEM_BYTES:
            break
    npc = len(pieces)
    has_res = res is not None
    dims = _NN if mode == "nn" else _NT

    def kern(*refs):
        b_ref = refs[npc]
        o_ref = refs[-1]
        acc = refs[npc + 1][...] if has_res else None
        off = 0
        for p in range(npc):
            bp = b_ref[off:off + widths[p], :] if mode == "nn" else b_ref[:, off:off + widths[p]]
            t = lax.dot_general(refs[p][...].astype(BF16), bp.astype(BF16), (dims, ((), ())),
                                preferred_element_type=F32)
            acc = t if acc is None else acc + t
            off += widths[p]
        o_ref[...] = acc

    in_specs = [pl.BlockSpec((tm, w), lambda j, i: (i, 0)) for w in widths]
    in_specs.append(pl.BlockSpec((K, tn), lambda j, i: (0, j)) if mode == "nn"
                    else pl.BlockSpec((tn, K), lambda j, i: (j, 0)))
    o_spec = pl.BlockSpec((tm, tn), lambda j, i: (i, j))
    if has_res:
        in_specs.append(o_spec)
    ops = list(pieces) + [b] + ([res] if has_res else [])
    return pl.pallas_call(
        kern, name=name, grid=(N // tn, M // tm), in_specs=in_specs, out_specs=o_spec,
        out_shape=jax.ShapeDtypeStruct((M, N), F32), compiler_params=_cparams(("parallel", "parallel")))(*ops)


def _blk(a, width=None, colblk=0):
    return (a, a.shape[1] if width is None else width, colblk)


def _row_specs(blocked, params, ts):
    specs = []
    for (_, w, cb) in blocked:
        specs.append(pl.BlockSpec((ts, w), functools.partial(lambda i, cb: (i, cb), cb=cb)))
    for p in params:
        specs.append(pl.BlockSpec(p.shape, lambda i: (0, 0)))
    return specs


def _rowwise(fn, blocked, params, out_widths, name, ts=256, out_dtypes=None):
    S = blocked[0][0].shape[0]
    ts = min(ts, S)
    nb, npar = len(blocked), len(params)
    out_dtypes = [F32] * len(out_widths) if out_dtypes is None else out_dtypes

    def kern(*refs):
        vals = [r[...] for r in refs[:nb + npar]]
        outs = fn(*vals)
        for o_ref, o in zip(refs[nb + npar:], outs):
            o_ref[...] = o.astype(o_ref.dtype)

    return pl.pallas_call(
        kern, name=name, grid=(S // ts,), in_specs=_row_specs(blocked, params, ts),
        out_specs=[pl.BlockSpec((ts, w), lambda i: (i, 0)) for w in out_widths],
        out_shape=[jax.ShapeDtypeStruct((S, w), d) for w, d in zip(out_widths, out_dtypes)],
        compiler_params=_cparams(("parallel",)))(*[b[0] for b in blocked], *params)


def _rowwise_bwd(fn, blocked, params, cots, name, blocked_grad=None, param_grad=None, adds=None, ts=256,
                 out_dtypes=None):
    S = blocked[0][0].shape[0]
    ts = min(ts, S)
    cots = [c if isinstance(c, tuple) else _blk(c) for c in cots]
    nb, npar, nc = len(blocked), len(params), len(cots)
    blocked_grad = [True] * nb if blocked_grad is None else blocked_grad
    param_grad = [True] * npar if param_grad is None else param_grad
    adds = {} if adds is None else adds
    bidx = [i for i in range(nb) if blocked_grad[i]]
    pidx = [i for i in range(npar) if param_grad[i]]
    add_keys = sorted(adds)
    n_in = nb + npar + nc + len(add_keys)

    def kern(*refs):
        i = pl.program_id(0)
        xs = [r[...] for r in refs[:nb]]
        ps = [r[...] for r in refs[nb:nb + npar]]
        gs = [r[...] for r in refs[nb + npar:nb + npar + nc]]
        add_vals = {k: refs[nb + npar + nc + n][...] for n, k in enumerate(add_keys)}
        outs = refs[n_in:]

        def f(*diff):
            full_x = list(xs)
            full_p = list(ps)
            for n, ix in enumerate(bidx):
                full_x[ix] = diff[n]
            for n, ix in enumerate(pidx):
                full_p[ix] = diff[len(bidx) + n]
            return tuple(fn(*full_x, *full_p))

        _, vjp = jax.vjp(f, *[xs[ix] for ix in bidx], *[ps[ix] for ix in pidx])
        grads = vjp(tuple(gs))
        for n, ix in enumerate(bidx):
            g = grads[n]
            if ix in add_vals:
                g = g + add_vals[ix]
            outs[n][...] = g.astype(outs[n].dtype)
        for n in range(len(pidx)):
            o_ref = outs[len(bidx) + n]

            @pl.when(i == 0)
            def _(o_ref=o_ref):
                o_ref[...] = jnp.zeros_like(o_ref)

            o_ref[...] += grads[len(bidx) + n]

    in_specs = _row_specs(blocked, params, ts)
    in_specs += _row_specs(cots, [], ts)
    in_specs += [pl.BlockSpec((ts, adds[k].shape[1]), lambda i: (i, 0)) for k in add_keys]
    out_specs = [pl.BlockSpec((ts, blocked[ix][1]), lambda i: (i, 0)) for ix in bidx]
    out_specs += [pl.BlockSpec(params[ix].shape, lambda i: (0, 0)) for ix in pidx]
    out_dtypes = [F32] * len(bidx) if out_dtypes is None else out_dtypes
    out_shape = [jax.ShapeDtypeStruct((S, blocked[ix][1]), d) for ix, d in zip(bidx, out_dtypes)]
    out_shape += [jax.ShapeDtypeStruct(params[ix].shape, F32) for ix in pidx]
    return pl.pallas_call(
        kern, name=name, grid=(S // ts,), in_specs=in_specs, out_specs=out_specs, out_shape=out_shape,
        compiler_params=_cparams(("arbitrary",)))(*[b[0] for b in blocked], *params, *[c[0] for c in cots],
                                                    *[adds[k] for k in add_keys])


def _rms_fn(x, g):
    return (_rms(x, g),)


def _head_norm(o, n_heads, dh):
    outs = []
    for h in range(n_heads):
        oh = o[:, h * dh:(h + 1) * dh]
        outs.append(oh * lax.rsqrt(jnp.mean(oh * oh, axis=-1, keepdims=True) + EPS))
    return outs


def _ret_post_fn(o_raw, gate, ret_norm):
    o = jnp.concatenate(_head_norm(o_raw, 4, 128), axis=1)
    return (o * ret_norm * _silu(gate),)


def _s5_post_fn(y1, y2, u, d, w_glu, b_glu):
    y = y1 - y2 + d * u
    y = jax.nn.gelu(y)
    return (y * jax.nn.sigmoid(mm(y, w_glu) + b_glu),)


def _xattn_fn(q, kv):
    outs = []
    for h in range(4):
        qh = q[:, h * 256:(h + 1) * 256]
        kh = kv[:, h * 256:(h + 1) * 256]
        vh = kv[:, 1024 + h * 256:1024 + (h + 1) * 256]
        s = mm_nt(qh, kh) * (256 ** -0.5)
        s = s - lax.stop_gradient(jnp.max(s, axis=-1, keepdims=True))
        p = jnp.exp(s)
        p = p / jnp.sum(p, axis=-1, keepdims=True)
        outs.append(mm(p, vh))
    return (jnp.concatenate(outs, axis=1),)


def _softplus(x):
    return jnp.maximum(x, 0.0) + jnp.log1p(jnp.exp(-jnp.abs(x)))


def _gdn_gates_fn(pt, a_log_p, dtb_p):
    rows, cols = _iota((128, 1024), 0), _iota((128, 1024), 1)
    e_b = (rows == (cols >> 7)).astype(F32)
    e_a = (rows == (cols >> 7) + 8).astype(F32)
    beta = jax.nn.sigmoid(pt)
    g = -(jnp.exp(a_log_p) * _softplus(pt + dtb_p))
    return mmh(g, e_a), mmh(beta, e_b)


def _gdn_post_fn(o_raw, z, o_norm):
    outs = _head_norm(o_raw, 8, 128)
    o = jnp.concatenate([oh * o_norm for oh in outs], axis=1)
    return (o * _silu(z),)


def _ffn_post(up, gate):
    return _silu(gate) * up


def _shift_down(cur, prev8, sh, row8):
    if sh == 0:
        return cur
    r = pltpu.roll(cur, sh, 0)
    p = pltpu.roll(prev8, sh, 0)
    top = jnp.where(row8 < sh, p, r[0:8])
    if cur.shape[0] == 8:
        return top
    return jnp.concatenate([top, r[8:]], axis=0)


def _shift_up(cur, next8, sh, row8):
    if sh == 0:
        return cur
    ts = cur.shape[0]
    r = pltpu.roll(cur, ts - sh, 0)
    p = pltpu.roll(next8, 8 - sh, 0)
    bot = jnp.where(row8 >= 8 - sh, p, r[ts - 8:])
    return jnp.concatenate([r[:ts - 8], bot], axis=0)


def _conv_rows(cur, prev8, wrows, row8):
    k_w = len(wrows)
    out = None
    for j in range(k_w):
        t = _shift_down(cur, prev8, k_w - 1 - j, row8) * wrows[j]
        out = t if out is None else out + t
    return out


def _conv_specs(x, xoff, w, woff, ts, tc):
    r8 = ts // 8
    return [pl.BlockSpec((ts, tc), functools.partial(lambda i, j, o: (i, j + o), o=xoff)),
            pl.BlockSpec((8, tc), functools.partial(lambda i, j, o: (jnp.maximum(i * r8 - 1, 0), j + o), o=xoff)),
            pl.BlockSpec((w.shape[0], tc), functools.partial(lambda i, j, o: (0, j + o), o=woff))]


def _conv_post(srcs, post, ncol, tc, name, cots=None, ts=256, out_dtype=F32):
    S = srcs[0][0].shape[0]
    ns = len(srcs)
    bwd = cots is not None

    def kern(*refs):
        first = pl.program_id(0) == 0
        row8 = _iota((8, tc), 0)
        cs = []
        for s in range(ns):
            cur_ref, prev_ref, w_ref = refs[3 * s:3 * s + 3]
            prev = jnp.where(first, 0.0, prev_ref[...])
            wrows = [w_ref[j:j + 1, :] for j in range(w_ref.shape[0])]
            cs.append(_conv_rows(cur_ref[...], prev, wrows, row8))
        if bwd:
            g = refs[3 * ns][...]
            _, vjp = jax.vjp(lambda *c: post(*c), *cs)
            for o_ref, d in zip(refs[3 * ns + 1:], vjp(g)):
                o_ref[...] = d
        else:
            refs[3 * ns][...] = post(*cs).astype(refs[3 * ns].dtype)

    in_specs = []
    ops = []
    for (x, xoff, w, woff) in srcs:
        in_specs += _conv_specs(x, xoff, w, woff, ts, tc)
        ops += [x, x, w]
    o_spec = pl.BlockSpec((ts, tc), lambda i, j: (i, j))
    o_shape = jax.ShapeDtypeStruct((S, ncol * tc), F32)
    if bwd:
        in_specs.append(o_spec)
        ops.append(cots)
        out_specs, out_shape = [o_spec] * ns, [o_shape] * ns
    else:
        out_specs, out_shape = o_spec, jax.ShapeDtypeStruct((S, ncol * tc), out_dtype)
    return pl.pallas_call(
        kern, name=name, grid=(S // ts, ncol), in_specs=in_specs, out_specs=out_specs, out_shape=out_shape,
        compiler_params=_cparams(("parallel", "parallel")))(*ops)


def _conv_bwd(dc, x, xoff, w, woff, ncol, tc, name, ts=256):
    S = x.shape[0]
    k_w = w.shape[0]
    r8 = ts // 8
    nblk8 = S // 8
    nrow = S // ts

    def kern(dc_ref, dn_ref, x_ref, xp_ref, w_ref, dx_ref, dw_ref):
        i = pl.program_id(1)
        row8 = _iota((8, tc), 0)
        dcur = dc_ref[...]
        dnext = jnp.where(i == nrow - 1, 0.0, dn_ref[...])
        xcur = x_ref[...]
        xprev = jnp.where(i == 0, 0.0, xp_ref[...])

        @pl.when(i == 0)
        def _():
            dw_ref[...] = jnp.zeros_like(dw_ref)

        dx = None
        for j in range(k_w):
            sh = k_w - 1 - j
            wj = w_ref[j:j + 1, :]
            t = _shift_up(dcur, dnext, sh, row8) * wj
            dx = t if dx is None else dx + t
            dw_ref[j:j + 1, :] += jnp.sum(dcur * _shift_down(xcur, xprev, sh, row8), axis=0, keepdims=True)
        dx_ref[...] = dx.astype(dx_ref.dtype)

    in_specs = [pl.BlockSpec((ts, tc), lambda j, i: (i, j)),
                pl.BlockSpec((8, tc), lambda j, i: (jnp.minimum((i + 1) * r8, nblk8 - 1), j)),
                pl.BlockSpec((ts, tc), functools.partial(lambda j, i, o: (i, j + o), o=xoff)),
                pl.BlockSpec((8, tc), functools.partial(lambda j, i, o: (jnp.maximum(i * r8 - 1, 0), j + o), o=xoff)),
                pl.BlockSpec((k_w, tc), functools.partial(lambda j, i, o: (0, j + o), o=woff))]
    out_specs = [pl.BlockSpec((ts, tc), lambda j, i: (i, j)), pl.BlockSpec((k_w, tc), lambda j, i: (0, j))]
    out_shape = [jax.ShapeDtypeStruct((S, ncol * tc), BF16), jax.ShapeDtypeStruct((k_w, ncol * tc), F32)]
    return pl.pallas_call(
        kern, name=name, grid=(ncol, nrow), in_specs=in_specs, out_specs=out_specs, out_shape=out_shape,
        compiler_params=_cparams(("parallel", "arbitrary")))(dc, dc, x, x, w)


def _conv_post_bwd(srcs, post, ncol, tc, cot, name, ts=256):
    S = srcs[0][0].shape[0]
    ns = len(srcs)
    r8 = ts // 8
    nblk8 = S // 8
    nrow = S // ts

    def kern(*refs):
        i = pl.program_id(1)
        row8 = _iota((8, tc), 0)
        g_ref, gn_ref = refs[4 * ns:4 * ns + 2]
        outs = refs[4 * ns + 2:]
        xs, xps, ws, cs, cns = [], [], [], [], []
        for s in range(ns):
            cur_ref, prev_ref, next_ref, w_ref = refs[4 * s:4 * s + 4]
            xcur = cur_ref[...]
            xprev = jnp.where(i == 0, 0.0, prev_ref[...])
            wrows = [w_ref[j:j + 1, :] for j in range(w_ref.shape[0])]
            xs.append(xcur)
            xps.append(xprev)
            ws.append(wrows)
            cs.append(_conv_rows(xcur, xprev, wrows, row8))
            cns.append(_conv_rows(next_ref[...], xcur[ts - 8:], wrows, row8))
        _, vjp = jax.vjp(lambda *c: post(*c), *cs)
        dcs = vjp(g_ref[...])
        _, vjp_next = jax.vjp(lambda *c: post(*c), *cns)
        dcns = vjp_next(jnp.where(i == nrow - 1, 0.0, gn_ref[...]))
        for s in range(ns):
            dx_ref, dw_ref = outs[2 * s], outs[2 * s + 1]

            @pl.when(i == 0)
            def _(dw_ref=dw_ref):
                dw_ref[...] = jnp.zeros_like(dw_ref)

            k_w = len(ws[s])
            dx = None
            for j in range(k_w):
                sh = k_w - 1 - j
                t = _shift_up(dcs[s], dcns[s], sh, row8) * ws[s][j]
                dx = t if dx is None else dx + t
                dw_ref[j:j + 1, :] += jnp.sum(dcs[s] * _shift_down(xs[s], xps[s], sh, row8), axis=0, keepdims=True)
            dx_ref[...] = dx.astype(dx_ref.dtype)

    def nxt(i):
        return jnp.minimum((i + 1) * r8, nblk8 - 1)

    in_specs, ops = [], []
    for (x, xoff, w, woff) in srcs:
        in_specs += [pl.BlockSpec((ts, tc), functools.partial(lambda j, i, o: (i, j + o), o=xoff)),
                     pl.BlockSpec((8, tc), functools.partial(lambda j, i, o: (jnp.maximum(i * r8 - 1, 0), j + o),
                                                             o=xoff)),
                     pl.BlockSpec((8, tc), functools.partial(lambda j, i, o: (nxt(i), j + o), o=xoff)),
                     pl.BlockSpec((w.shape[0], tc), functools.partial(lambda j, i, o: (0, j + o), o=woff))]
        ops += [x, x, x, w]
    in_specs += [pl.BlockSpec((ts, tc), lambda j, i: (i, j)), pl.BlockSpec((8, tc), lambda j, i: (nxt(i), j))]
    ops += [cot, cot]
    out_specs, out_shape = [], []
    for (x, xoff, w, woff) in srcs:
        out_specs += [pl.BlockSpec((ts, tc), lambda j, i: (i, j)), pl.BlockSpec((w.shape[0], tc), lambda j, i: (0, j))]
        out_shape += [jax.ShapeDtypeStruct((S, ncol * tc), BF16), jax.ShapeDtypeStruct((w.shape[0], ncol * tc), F32)]
    return pl.pallas_call(
        kern, name=name, grid=(ncol, nrow), in_specs=in_specs, out_specs=out_specs, out_shape=out_shape,
        compiler_params=_cparams(("parallel", "arbitrary")))(*ops)


def _ret_tables(S):
    H, C, dh = 4, 128, 128
    lg = jnp.log1p(-jnp.exp2(-5.0 - jnp.arange(H, dtype=F32)))
    idx = jnp.arange(C, dtype=F32)
    diff = idx[:, None] - idx[None, :]
    causal = diff >= 0
    intra = jnp.where(causal, jnp.exp(lg[:, None, None] * jnp.where(causal, diff, 0.0)), 0.0)
    kdec = jnp.broadcast_to(jnp.exp(lg[:, None] * (C - 1 - idx))[:, :, None], (H, C, dh))
    qdec = jnp.broadcast_to(jnp.exp(lg[:, None] * (idx + 1))[:, :, None], (H, C, dh))
    cdec = jnp.broadcast_to(jnp.exp(lg * C)[:, None, None], (H, dh, dh))
    half = dh // 2
    inv = jnp.exp(-math.log(10000.0) * jnp.arange(half, dtype=F32) / half)
    ang = jnp.arange(S).astype(F32)[:, None] * inv[None, :]
    cos, sin = jnp.cos(ang), jnp.sin(ang)
    cosf = jnp.concatenate([cos, cos], axis=1)
    sinf = jnp.concatenate([-sin, sin], axis=1)
    return cosf, sinf, intra, kdec, qdec, cdec


def _ret_chunk(q, k, v, cosf, sinf, intra, kdec, qdec, cdec, state):
    hs = range(len(q))
    qr = [q[h] * cosf + _swap_halves(q[h]) * sinf for h in hs]
    kr = [(k[h] * cosf + _swap_halves(k[h]) * sinf) * (128 ** -0.5) for h in hs]
    scores = [mm_nt(qr[h], kr[h]) * intra[h] for h in hs]
    inner = [mm(scores[h], v[h]) for h in hs]
    kv = [mm_tn(kr[h] * kdec[h], v[h]) for h in hs]
    cross = [mm(qr[h] * qdec[h], state[h]) for h in hs]
    return [inner[h] + cross[h] for h in hs], [state[h] * cdec[h] + kv[h] for h in hs]


RET_H = 4


def _ret_call(proj, tabs, states=None, do=None):
    S = proj.shape[0]
    N = S // 128
    bwd = do is not None

    def nn(n):
        return N - 1 - n if bwd else n

    qkv_spec = pl.BlockSpec((128, 3 * 512), lambda n: (nn(n), 0))
    pos = pl.BlockSpec((128, 128), lambda n: (nn(n), 0))
    tab = pl.BlockSpec((RET_H, 128, 128), lambda n: (0, 0, 0))
    st_spec = pl.BlockSpec((None, RET_H, 128, 128), lambda n: (nn(n), 0, 0, 0))
    o_spec = pl.BlockSpec((128, 512), lambda n: (nn(n), 0))

    def kern(*refs):
        x_ref, c_ref, s_ref, i_ref, kd_ref, qd_ref, cd_ref = refs[:7]
        carry = refs[-1]
        heads = range(RET_H)

        @pl.when(pl.program_id(0) == 0)
        def _():
            carry[...] = jnp.zeros_like(carry)

        def cols(ref, off=0):
            return [ref[:, _hs(off + h)] for h in heads]

        def tabs_of(ref):
            return [ref[h] for h in heads]

        consts = (c_ref[...], s_ref[...], tabs_of(i_ref), tabs_of(kd_ref), tabs_of(qd_ref), tabs_of(cd_ref))
        qkv = (cols(x_ref), cols(x_ref, RET_H), cols(x_ref, 2 * RET_H))
        if bwd:
            sp_ref, do_ref = refs[7:9]
            outs = refs[9:12]
            _, vjp = jax.vjp(lambda q, k, v, s: _ret_chunk(q, k, v, *consts, s), *qkv, tabs_of(sp_ref))
            dq, dk, dv, ds = vjp((cols(do_ref), tabs_of(carry)))
            for h in heads:
                for o_ref, d in zip(outs, (dq[h], dk[h], dv[h])):
                    o_ref[:, _hs(h)] = d.astype(o_ref.dtype)
                carry[h] = ds[h]
        else:
            o_ref, sp_ref = refs[7:9]
            state = tabs_of(carry)
            out, new = _ret_chunk(*qkv, *consts, state)
            for h in heads:
                sp_ref[h] = state[h]
                o_ref[:, _hs(h)] = out[h]
                carry[h] = new[h]

    in_specs = [qkv_spec, pos, pos, tab, tab, tab, tab]
    if bwd:
        in_specs += [st_spec, o_spec]
        out_specs = [o_spec] * 3
        out_shape = [jax.ShapeDtypeStruct((S, 512), BF16)] * 3
        ops = (proj, *tabs, states, do)
    else:
        out_specs = [o_spec, st_spec]
        out_shape = [jax.ShapeDtypeStruct((S, 512), F32), jax.ShapeDtypeStruct((N, RET_H, 128, 128), F32)]
        ops = (proj, *tabs)
    return pl.pallas_call(
        kern, name="ret_bwd" if bwd else "ret_fwd", grid=(N,), in_specs=in_specs, out_specs=out_specs,
        out_shape=out_shape, scratch_shapes=[pltpu.VMEM((RET_H, 128, 128), F32)],
        compiler_params=_cparams(("arbitrary",)))(*ops)


GDN_C = 64
GDN_H = 8


def _unit_lower_inverse(a_mats, eye):
    p = [-a for a in a_mats]
    t = [eye + x for x in p]
    for _ in range(5):
        p = [mm3(x, x) for x in p]
        t = [mm3(y, eye + x) for y, x in zip(t, p)]
    return t


@jax.custom_vjp
def _known_inverse(a_mat, t_mat):
    return t_mat


def _known_inverse_f(a_mat, t_mat):
    return t_mat, t_mat


def _known_inverse_b(t_mat, g):
    return -mm3_tn(t_mat, mm3_nt(g, t_mat)), jnp.zeros_like(t_mat)


_known_inverse.defvjp(_known_inverse_f, _known_inverse_b)


def _gdn_intra(q, k, v, g_b, beta_b, t_known=None):
    c = GDN_C
    hs = range(len(q))
    q = [x * lax.rsqrt(jnp.sum(x * x, axis=-1, keepdims=True) + EPS) * (128 ** -0.5) for x in q]
    k = [x * lax.rsqrt(jnp.sum(x * x, axis=-1, keepdims=True) + EPS) for x in k]
    ri, ci = _iota((c, c), 0), _iota((c, c), 1)
    incl = ri >= ci
    strict = ri > ci
    eye = (ri == ci).astype(F32)
    lower = incl.astype(F32)
    gc_b = [mm3(lower, g) for g in g_b]
    gl_b = [jnp.sum(g, axis=0, keepdims=True) for g in g_b]
    kb = [k[h] * beta_b[h] for h in hs]
    vb = [v[h] * beta_b[h] for h in hs]
    gcc = [g[:, :c] for g in gc_b]
    decay = [jnp.where(incl, jnp.exp(jnp.where(incl, g - g.T, 0.0)), 0.0) for g in gcc]
    a_mat = [jnp.where(strict, mm_nt(kb[h], k[h]) * decay[h], 0.0) for h in hs]
    if t_known is None:
        t_mat = _unit_lower_inverse(a_mat, eye)
    else:
        t_mat = [_known_inverse(a_mat[h], t_known[h]) for h in hs]
    egc = [jnp.exp(g) for g in gc_b]
    w = [mm(t_mat[h], kb[h] * egc[h]) for h in hs]
    u = [mm(t_mat[h], vb[h]) for h in hs]
    qk = [jnp.where(incl, mm_nt(q[h], k[h]) * decay[h], 0.0) for h in hs]
    q_dec = [q[h] * egc[h] for h in hs]
    k_dec = [k[h] * jnp.exp(gl_b[h] - gc_b[h]) for h in hs]
    return w, u, q_dec, k_dec, qk, t_mat


def _gdn_step(w, u, q_dec, k_dec, qk, g_b, state):
    hs = range(len(w))
    gl_s = [jnp.sum(g, axis=0, keepdims=True) for g in g_b]
    ws = [mm(w[h], state[h]) for h in hs]
    qs = [mm(q_dec[h], state[h]) for h in hs]
    v_new = [u[h] - ws[h] for h in hs]
    o = [qs[h] + mm(qk[h], v_new[h]) for h in hs]
    new = [state[h] * jnp.exp(gl_s[h]) + mm_tn(k_dec[h], v_new[h]) for h in hs]
    return o, new


def _hs(h):
    return slice(h * 128, (h + 1) * 128)


def _gdn_intra_call(qkv, g_e, beta_e, cots=None):
    S = qkv.shape[0]
    N = S // GDN_C
    bwd = cots is not None
    row = pl.BlockSpec((GDN_C, 1024), lambda n: (n, 0))
    qkv_spec = pl.BlockSpec((GDN_C, 3072), lambda n: (n, 0))
    qk_spec = pl.BlockSpec((GDN_H, GDN_C, GDN_C), lambda n: (0, n, 0))

    def kern(*refs):
        x_ref, g_ref, b_ref = refs[:3]
        heads = range(GDN_H)

        def cols(ref, off=0):
            return [ref[:, _hs(off + h)] for h in heads]

        args = (cols(x_ref), cols(x_ref, 8), cols(x_ref, 16), cols(g_ref), cols(b_ref))
        if bwd:
            dw_ref, du_ref, dqd_ref, dkd_ref, dqk_ref, dgadd_ref, t_ref = refs[3:10]
            outs = refs[10:]
            t_known = [t_ref[h] for h in heads]
            _, vjp = jax.vjp(lambda *a: _gdn_intra(*a, t_known=t_known)[:5], *args)
            dq, dk, dv, dg, db = vjp((cols(dw_ref), cols(du_ref), cols(dqd_ref), cols(dkd_ref),
                                      [dqk_ref[h] for h in heads]))
            dgadd = cols(dgadd_ref)
            for h in heads:
                for o_ref, d in zip(outs, (dq[h], dk[h], dv[h], dg[h] + dgadd[h], db[h])):
                    o_ref[:, _hs(h)] = d
        else:
            w, u, qd, kd, qk, t_mat = _gdn_intra(*args)
            for h in heads:
                for o_ref, o in zip(refs[3:7], (w[h], u[h], qd[h], kd[h])):
                    o_ref[:, _hs(h)] = o
                refs[7][h] = qk[h]
                refs[8][h] = t_mat[h]

    big = jax.ShapeDtypeStruct((S, 1024), F32)
    sq = jax.ShapeDtypeStruct((GDN_H, S, GDN_C), F32)
    if bwd:
        in_specs = [qkv_spec, row, row, row, row, row, row, qk_spec, row, qk_spec]
        out_specs, out_shape = [row] * 5, [big] * 5
        ops = (qkv, g_e, beta_e) + tuple(cots)
    else:
        in_specs = [qkv_spec, row, row]
        out_specs = [row] * 4 + [qk_spec, qk_spec]
        out_shape = [big] * 4 + [sq, sq]
        ops = (qkv, g_e, beta_e)
    return pl.pallas_call(
        kern, name="gdn_intra_bwd" if bwd else "gdn_intra", grid=(N,), in_specs=in_specs, out_specs=out_specs,
        out_shape=out_shape, compiler_params=_cparams(("parallel",)))(*ops)


def _gdn_pass(w, u, qd, kd, qk, g_e, states=None, do=None):
    S = w.shape[0]
    N = S // GDN_C
    bwd = do is not None

    def nn(n):
        return N - 1 - n if bwd else n

    row = pl.BlockSpec((GDN_C, 1024), lambda n: (nn(n), 0))
    qk_spec = pl.BlockSpec((GDN_H, GDN_C, GDN_C), lambda n: (0, nn(n), 0))
    st_spec = pl.BlockSpec((None, GDN_H, 128, 128), lambda n: (nn(n), 0, 0, 0))

    def kern(*refs):
        w_ref, u_ref, qd_ref, kd_ref, qk_ref, g_ref = refs[:6]
        carry = refs[-1]

        @pl.when(pl.program_id(0) == 0)
        def _():
            carry[...] = jnp.zeros_like(carry)

        heads = range(GDN_H)

        def cols(ref):
            return [ref[:, _hs(h)] for h in heads]

        args = (cols(w_ref), cols(u_ref), cols(qd_ref), cols(kd_ref), [qk_ref[h] for h in heads], cols(g_ref))
        if bwd:
            sp_ref, do_ref = refs[6:8]
            outs = refs[8:14]
            _, vjp = jax.vjp(_gdn_step, *args, [sp_ref[h] for h in heads])
            dw, du, dqd, dkd, dqk, dg, ds = vjp((cols(do_ref), [carry[h] for h in heads]))
            for h in heads:
                for o_ref, d in zip(outs[:4], (dw[h], du[h], dqd[h], dkd[h])):
                    o_ref[:, _hs(h)] = d
                outs[4][h] = dqk[h]
                outs[5][:, _hs(h)] = dg[h]
                carry[h] = ds[h]
        else:
            o_ref, sp_ref = refs[6:8]
            state = [carry[h] for h in heads]
            o, new = _gdn_step(*args, state)
            for h in heads:
                sp_ref[h] = state[h]
                o_ref[:, _hs(h)] = o[h]
                carry[h] = new[h]

    big = jax.ShapeDtypeStruct((S, 1024), F32)
    in_specs = [row, row, row, row, qk_spec, row]
    if bwd:
        in_specs += [st_spec, row]
        out_specs = [row] * 4 + [qk_spec, row]
        out_shape = [big] * 4 + [jax.ShapeDtypeStruct((GDN_H, S, GDN_C), F32), big]
        ops = (w, u, qd, kd, qk, g_e, states, do)
    else:
        out_specs = [row, st_spec]
        out_shape = [big, jax.ShapeDtypeStruct((N, GDN_H, 128, 128), F32)]
        ops = (w, u, qd, kd, qk, g_e)
    return pl.pallas_call(
        kern, name="gdn_pass_bwd" if bwd else "gdn_pass", grid=(N,), in_specs=in_specs, out_specs=out_specs,
        out_shape=out_shape, scratch_shapes=[pltpu.VMEM((GDN_H, 128, 128), F32)],
        compiler_params=_cparams(("arbitrary",)))(*ops)


def _s5_prep_fn(lr, li, ldt, br, bi, cr, ci):
    dt = jnp.exp(ldt)
    mag = jnp.exp(lr * dt)
    a_re = mag * jnp.cos(li * dt)
    a_im = mag * jnp.sin(li * dt)
    den = lr * lr + li * li
    z_re = ((a_re - 1.0) * lr + a_im * li) / den
    z_im = (a_im * lr - (a_re - 1.0) * li) / den
    e1 = ((_iota((512, 32), 0) >> 4) == _iota((512, 32), 1)).astype(F32)
    zr_e = mmh(e1, z_re)
    zi_e = mmh(e1, z_im)
    bb_re = zr_e * br - zi_e * bi
    bb_im = zr_e * bi + zi_e * br
    t1 = ((_iota((64, 2048), 1) & 63) == _iota((64, 2048), 0)).astype(F32)
    m1 = (_iota((512, 2048), 0) >> 4) == (_iota((512, 2048), 1) >> 6)
    bd_re = jnp.where(m1, mmh(bb_re, t1), 0.0)
    bd_im = jnp.where(m1, mmh(bb_im, t1), 0.0)
    t2 = ((_iota((16, 512), 1) & 15) == _iota((16, 512), 0)).astype(F32)
    m2 = (_iota((2048, 512), 0) >> 6) == (_iota((2048, 512), 1) >> 4)
    cd_re = jnp.where(m2, mmh(cr, t2), 0.0)
    cd_im = jnp.where(m2, mmh(ci, t2), 0.0)
    return a_re, a_im, bd_re, bd_im, cd_re, cd_im


_PREP_OUT = [(32, 64), (32, 64), (512, 2048), (512, 2048), (2048, 512), (2048, 512)]


def _s5_prep(params, cots=None):
    bwd = cots is not None

    def kern(*refs):
        vals = [r[...] for r in refs[:7]]
        if bwd:
            gs = tuple(r[...] for r in refs[7:13])
            _, vjp = jax.vjp(_s5_prep_fn, *vals)
            for o_ref, d in zip(refs[13:], vjp(gs)):
                o_ref[...] = d
        else:
            for o_ref, o in zip(refs[7:], _s5_prep_fn(*vals)):
                o_ref[...] = o

    if bwd:
        out_shape = [jax.ShapeDtypeStruct(p.shape, F32) for p in params]
        ops = list(params) + list(cots)
    else:
        out_shape = [jax.ShapeDtypeStruct(s, F32) for s in _PREP_OUT]
        ops = list(params)
    return pl.pallas_call(kern, name="s5_prep_bwd" if bwd else "s5_prep", out_shape=out_shape,
                          compiler_params=_cparams())(*ops)


def _cmul(ar, ai, br, bi):
    return ar * br - ai * bi, ar * bi + ai * br


def _power_table(ar, ai, row8, descending):
    pr, pi = ar, ai
    tr = jnp.zeros(row8.shape, F32)
    ti = jnp.zeros(row8.shape, F32)
    for n in range(8):
        r = 7 - n if descending else n
        tr = jnp.where(row8 == r, pr, tr)
        ti = jnp.where(row8 == r, pi, ti)
        if n < 7:
            pr, pi = _cmul(pr, pi, ar, ai)
    return tr, ti


def _tile_scan(xr, xi, pows, row8, up):
    for d, (pr, pi) in zip((1, 2, 4), pows):
        if up:
            sr = jnp.where(row8 < 8 - d, pltpu.roll(xr, 8 - d, 0), 0.0)
            si = jnp.where(row8 < 8 - d, pltpu.roll(xi, 8 - d, 0), 0.0)
        else:
            sr = jnp.where(row8 >= d, pltpu.roll(xr, d, 0), 0.0)
            si = jnp.where(row8 >= d, pltpu.roll(xi, d, 0), 0.0)
        mr, mi = _cmul(pr, pi, sr, si)
        xr, xi = xr + mr, xi + mi
    return xr, xi


def _pick_row(x, row8, r):
    return jnp.sum(jnp.where(row8 == r, x, 0.0), axis=0, keepdims=True)


SCAN_LB = 512
SCAN_TS = 512


def _scan_fwd(bu_re, bu_im, a_re, a_im):
    S, L = bu_re.shape
    ts, lb = min(SCAN_TS, S), SCAN_LB
    nt = ts // 8

    def kern(br_ref, bi_ref, ar_ref, ai_ref, or_ref, oi_ref, cr_ref, ci_ref):
        @pl.when(pl.program_id(1) == 0)
        def _():
            cr_ref[...] = jnp.zeros_like(cr_ref)
            ci_ref[...] = jnp.zeros_like(ci_ref)

        row8 = _iota((8, lb), 0)
        ar, ai = ar_ref[...], ai_ref[...]
        a2 = _cmul(ar, ai, ar, ai)
        a4 = _cmul(*a2, *a2)
        pows = ((ar, ai), a2, a4)
        tr, ti = _power_table(ar, ai, row8, False)

        def body(i, carry):
            cr, ci = carry
            off = pl.multiple_of(i * 8, 8)
            xr, xi = _tile_scan(br_ref[pl.ds(off, 8), :], bi_ref[pl.ds(off, 8), :], pows, row8, False)
            mr, mi = _cmul(tr, ti, cr, ci)
            xr, xi = xr + mr, xi + mi
            or_ref[pl.ds(off, 8), :] = xr
            oi_ref[pl.ds(off, 8), :] = xi
            return _pick_row(xr, row8, 7), _pick_row(xi, row8, 7)

        cr, ci = lax.fori_loop(0, nt, body, (cr_ref[...], ci_ref[...]))
        cr_ref[...] = cr
        ci_ref[...] = ci

    blk = pl.BlockSpec((ts, lb), lambda j, i: (i, j))
    par = pl.BlockSpec((1, lb), lambda j, i: (0, j))
    return pl.pallas_call(
        kern, name="s5_scan_fwd", grid=(L // lb, S // ts), in_specs=[blk, blk, par, par], out_specs=[blk, blk],
        out_shape=[jax.ShapeDtypeStruct((S, L), F32)] * 2,
        scratch_shapes=[pltpu.VMEM((1, lb), F32), pltpu.VMEM((1, lb), F32)],
        compiler_params=_cparams(("parallel", "arbitrary")))(bu_re, bu_im, a_re, a_im)


def _scan_bwd(dst_re, dst_im, st_re, st_im, a_re, a_im):
    S, L = dst_re.shape
    ts, lb = min(SCAN_TS, S), SCAN_LB
    nt = ts // 8
    nb = S // ts
    r8 = ts // 8

    def kern(dr_ref, di_ref, sr_ref, si_ref, pr_ref, pi_ref, ar_ref, ai_ref, gr_ref, gi_ref, dar_ref, dai_ref,
             cr_ref, ci_ref):
        step = pl.program_id(1)
        blk = nb - 1 - step

        @pl.when(step == 0)
        def _():
            cr_ref[...] = jnp.zeros_like(cr_ref)
            ci_ref[...] = jnp.zeros_like(ci_ref)
            dar_ref[...] = jnp.zeros_like(dar_ref)
            dai_ref[...] = jnp.zeros_like(dai_ref)

        row8 = _iota((8, lb), 0)
        ar, ai = ar_ref[...], ai_ref[...]
        nai = -ai
        a2 = _cmul(ar, nai, ar, nai)
        a4 = _cmul(*a2, *a2)
        pows = ((ar, nai), a2, a4)
        tr, ti = _power_table(ar, nai, row8, True)
        halo_r = jnp.where(blk == 0, 0.0, pr_ref[...])
        halo_i = jnp.where(blk == 0, 0.0, pi_ref[...])

        def body(n, carry):
            cr, ci, acc_r, acc_i = carry
            i = nt - 1 - n
            off = pl.multiple_of(i * 8, 8)
            gr, gi = _tile_scan(dr_ref[pl.ds(off, 8), :], di_ref[pl.ds(off, 8), :], pows, row8, True)
            mr, mi = _cmul(tr, ti, cr, ci)
            gr, gi = gr + mr, gi + mi
            gr_ref[pl.ds(off, 8), :] = gr
            gi_ref[pl.ds(off, 8), :] = gi
            poff = pl.multiple_of(jnp.maximum(i - 1, 0) * 8, 8)
            before_r = jnp.where(i == 0, halo_r, sr_ref[pl.ds(poff, 8), :])
            before_i = jnp.where(i == 0, halo_i, si_ref[pl.ds(poff, 8), :])
            last_r = _pick_row(before_r, row8, 7)
            last_i = _pick_row(before_i, row8, 7)
            spr = jnp.where(row8 >= 1, pltpu.roll(sr_ref[pl.ds(off, 8), :], 1, 0), last_r)
            spi = jnp.where(row8 >= 1, pltpu.roll(si_ref[pl.ds(off, 8), :], 1, 0), last_i)
            acc_r = acc_r + gr * spr + gi * spi
            acc_i = acc_i + gi * spr - gr * spi
            return _pick_row(gr, row8, 0), _pick_row(gi, row8, 0), acc_r, acc_i

        zero = jnp.zeros((8, lb), F32)
        cr, ci, acc_r, acc_i = lax.fori_loop(0, nt, body, (cr_ref[...], ci_ref[...], zero, zero))
        cr_ref[...] = cr
        ci_ref[...] = ci
        dar_ref[...] += jnp.sum(acc_r, axis=0, keepdims=True)
        dai_ref[...] += jnp.sum(acc_i, axis=0, keepdims=True)

    blk = pl.BlockSpec((ts, lb), lambda j, i: (nb - 1 - i, j))
    halo = pl.BlockSpec((8, lb), lambda j, i: (jnp.maximum((nb - 1 - i) * r8 - 1, 0), j))
    par = pl.BlockSpec((1, lb), lambda j, i: (0, j))
    return pl.pallas_call(
        kern, name="s5_scan_bwd", grid=(L // lb, nb), in_specs=[blk, blk, blk, blk, halo, halo, par, par],
        out_specs=[blk, blk, par, par],
        out_shape=[jax.ShapeDtypeStruct((S, L), F32)] * 2 + [jax.ShapeDtypeStruct((1, L), F32)] * 2,
        scratch_shapes=[pltpu.VMEM((1, lb), F32), pltpu.VMEM((1, lb), F32)],
        compiler_params=_cparams(("parallel", "arbitrary")))(dst_re, dst_im, st_re, st_im, st_re, st_im, a_re, a_im)


def _loss_grad(x, target, gain, ts=256):
    S, D = x.shape

    def kern(x_ref, t_ref, g_ref, loss_ref, dx_ref, dg_ref):
        i = pl.program_id(0)
        tgt = t_ref[...]

        def f(xv, gv):
            err = _rms(xv, gv) - tgt
            return 0.5 * jnp.mean(err * err, axis=-1, keepdims=True)

        rowloss, vjp = jax.vjp(f, x_ref[...], g_ref[...])
        dx, dg = vjp(jnp.ones_like(rowloss))
        dx_ref[...] = dx

        @pl.when(i == 0)
        def _():
            loss_ref[...] = jnp.zeros_like(loss_ref)
            dg_ref[...] = jnp.zeros_like(dg_ref)

        loss_ref[...] += jnp.broadcast_to(jnp.sum(rowloss, axis=0, keepdims=True), loss_ref.shape)
        dg_ref[...] += dg

    row = pl.BlockSpec((ts, D), lambda i: (i, 0))
    return pl.pallas_call(
        kern, name="loss_grad", grid=(S // ts,), in_specs=[row, row, pl.BlockSpec((1, D), lambda i: (0, 0))],
        out_specs=[pl.BlockSpec((8, 128), lambda i: (0, 0)), row, pl.BlockSpec((1, D), lambda i: (0, 0))],
        out_shape=[jax.ShapeDtypeStruct((8, 128), F32), jax.ShapeDtypeStruct((S, D), F32),
                   jax.ShapeDtypeStruct((1, D), F32)],
        compiler_params=_cparams(("arbitrary",)))(x, target, gain)


def _rms_fwd(x, g, name):
    return _rowwise(_rms_fn, [_blk(x)], [g], [x.shape[1]], name, out_dtypes=[BF16])[0]


def _rms_bwd(x, g, dy, name, add=None):
    return _rowwise_bwd(_rms_fn, [_blk(x)], [g], [dy], name, adds=None if add is None else {0: add})


FFN_TC = 1408


def _common_fwd(x, mem, P, L):
    hx = _rms_fwd(x, P['xa_norm'], L + "xa_norm")
    q = _matmul(hx, P['xa_wq'], name=L + "xa_q")
    memn = _rms_fwd(mem, P['mem_norm'], L + "mem_norm")
    kv = _matmul(memn, P['xa_wkv'], name=L + "xa_kv")
    att = _rowwise(_xattn_fn, [_blk(q)], [kv], [1024], L + "xattn", out_dtypes=[BF16])[0]
    x2 = _matmul(att, P['xa_wo'], res=x, name=L + "xa_o")
    hf = _rms_fwd(x2, P['ffn_norm'], L + "ffn_norm")
    hu = _matmul(hf, P['ffn_w_up'], name=L + "ffn_up")
    cw = P['ffn_conv']
    act = _conv_post([(hu, 0, cw, 0), (hu, 2, cw, 2)], _ffn_post, 2, FFN_TC, L + "ffn_conv", out_dtype=BF16)
    x3 = _matmul(act, P['ffn_w_down'], res=x2, name=L + "ffn_down")
    return x3, (x, mem, hx, q, memn, kv, att, x2, hf, hu, act)


def _common_bwd(saved, dx3, P, L, midway=None):
    x, mem, hx, q, memn, kv, att, x2, hf, hu, act = saved
    G = {}
    dact = _matmul(dx3, P['ffn_w_down'], "nt", name=L + "ffn_down_dx")
    G['ffn_w_down'] = _matmul(act, dx3, "tn", name=L + "ffn_down_dw")
    cw = P['ffn_conv']
    dhu_u, dcw_u, dhu_g, dcw_g = _conv_post_bwd([(hu, 0, cw, 0), (hu, 2, cw, 2)], _ffn_post, 2, FFN_TC, dact,
                                                L + "ffn_conv_bwd")
    G['ffn_conv'] = jnp.concatenate([dcw_u, dcw_g], axis=1)
    dhf = _matmul_cat([dhu_u, dhu_g], P['ffn_w_up'], "nt", name=L + "ffn_up_dx")
    G['ffn_w_up'] = jnp.concatenate([_matmul(hf, dhu_u, "tn", name=L + "ffn_up_dw_up"),
                                     _matmul(hf, dhu_g, "tn", name=L + "ffn_up_dw_gate")], axis=1)
    dx2, G['ffn_norm'] = _rms_bwd(x2, P['ffn_norm'], dhf, L + "ffn_norm_bwd", add=dx3)
    if midway is not None:
        P = dict(P, xa_wo=P['xa_wo'] + midway(dx2).astype(P['xa_wo'].dtype))
    datt = _matmul(dx2, P['xa_wo'], "nt", name=L + "xa_o_dx")
    G['xa_wo'] = _matmul(att, dx2, "tn", name=L + "xa_o_dw")
    dq, dkv = _rowwise_bwd(_xattn_fn, [_blk(q)], [kv], [datt], L + "xattn_bwd", out_dtypes=[BF16])
    dhx = _matmul(dq, P['xa_wq'], "nt", name=L + "xa_q_dx")
    G['xa_wq'] = _matmul(hx, dq, "tn", name=L + "xa_q_dw")
    dmemn = _matmul(dkv, P['xa_wkv'], "nt", name=L + "xa_kv_dx")
    G['xa_wkv'] = _matmul(memn, dkv, "tn", name=L + "xa_kv_dw")
    _, G['mem_norm'] = _rms_bwd(mem, P['mem_norm'], dmemn, L + "mem_norm_bwd")
    dx, G['xa_norm'] = _rms_bwd(x, P['xa_norm'], dhx, L + "xa_norm_bwd", add=dx2)
    return dx, G


U_COLS = (2048, 512)


def _even_fwd(x, P):
    S = x.shape[0]
    h0 = _rms_fwd(x, P['mix_norm'], "l0_mix_norm")
    proj = _matmul(h0, P['w_in'], name="l0_in")
    tabs = _ret_tables(S)
    o_raw, rstates = _ret_call(proj, tabs)
    o = _rowwise(_ret_post_fn, [_blk(o_raw), _blk(proj, 512, 3)], [P['ret_norm']], [512], "l0_ret_post",
                 out_dtypes=[BF16])[0]
    prep_in = (P['s5_lambda_re'], P['s5_lambda_im'], P['s5_log_dt'], P['s5_b_re'], P['s5_b_im'], P['s5_c_re'],
               P['s5_c_im'])
    a_re, a_im, bd_re, bd_im, cd_re, cd_im = _s5_prep(prep_in)
    a_re_f, a_im_f = a_re.reshape(1, 2048), a_im.reshape(1, 2048)
    bu_re = _matmul(proj, bd_re, name="l0_s5_bu_re", a_cols=U_COLS)
    bu_im = _matmul(proj, bd_im, name="l0_s5_bu_im", a_cols=U_COLS)
    st_re, st_im = _scan_fwd(bu_re, bu_im, a_re_f, a_im_f)
    y1 = _matmul(st_re, cd_re, name="l0_s5_y_re")
    y2 = _matmul(st_im, cd_im, name="l0_s5_y_im")
    yg = _rowwise(_s5_post_fn, [_blk(y1), _blk(y2), _blk(proj, 512, 4)],
                  [P['s5_d'], P['s5_w_glu'], P['s5_b_glu']], [512], "l0_s5_post", out_dtypes=[BF16])[0]
    x1 = _matmul_cat([o, yg], P['w_out'], "nn", res=x, name="l0_out")
    saved = (x, h0, proj, tabs, o_raw, rstates, prep_in, a_re_f, a_im_f, bd_re, bd_im, cd_re, cd_im, st_re, st_im,
             y1, y2, o, yg)
    return x1, saved


def _even_bwd(saved, dx1, P, midway=None):
    (x, h0, proj, tabs, o_raw, rstates, prep_in, a_re_f, a_im_f, bd_re, bd_im, cd_re, cd_im, st_re, st_im, y1, y2,
     o, yg) = saved
    G = {}
    dmerged = _matmul(dx1, P['w_out'], "nt", name="l0_out_dx")
    G['w_out'] = jnp.concatenate([_matmul(o, dx1, "tn", name="l0_out_dw_ret"),
                                  _matmul(yg, dx1, "tn", name="l0_out_dw_s5")], axis=0)
    do_raw, dgate, G['ret_norm'] = _rowwise_bwd(
        _ret_post_fn, [_blk(o_raw), _blk(proj, 512, 3)], [P['ret_norm']], [_blk(dmerged, 512, 0)], "l0_ret_post_bwd",
        out_dtypes=[F32, BF16])
    dq, dk, dv = _ret_call(proj, tabs, states=rstates, do=do_raw)
    if midway is not None:
        P = dict(P, s5_w_glu=P['s5_w_glu'] + midway(dq))
    dy1, dy2, du_a, G['s5_d'], G['s5_w_glu'], G['s5_b_glu'] = _rowwise_bwd(
        _s5_post_fn, [_blk(y1), _blk(y2), _blk(proj, 512, 4)], [P['s5_d'], P['s5_w_glu'], P['s5_b_glu']],
        [_blk(dmerged, 512, 1)], "l0_s5_post_bwd", out_dtypes=[BF16, BF16, F32])
    dst_re = _matmul(dy1, cd_re, "nt", name="l0_s5_y_re_dx")
    dcd_re = _matmul(st_re, dy1, "tn", name="l0_s5_y_re_dw")
    dst_im = _matmul(dy2, cd_im, "nt", name="l0_s5_y_im_dx")
    dcd_im = _matmul(st_im, dy2, "tn", name="l0_s5_y_im_dw")
    dbu_re, dbu_im, da_re, da_im = _scan_bwd(dst_re, dst_im, st_re, st_im, a_re_f, a_im_f)
    du = _matmul(dbu_re, bd_re, "nt", res=du_a, name="l0_s5_bu_re_dx")
    du = _matmul(dbu_im, bd_im, "nt", res=du, name="l0_s5_bu_im_dx", out_dtype=BF16)
    dbd_re = _matmul(proj, dbu_re, "tn", name="l0_s5_bu_re_dw", a_cols=U_COLS)
    dbd_im = _matmul(proj, dbu_im, "tn", name="l0_s5_bu_im_dw", a_cols=U_COLS)
    dprep = _s5_prep(prep_in, cots=(da_re.reshape(32, 64), da_im.reshape(32, 64), dbd_re, dbd_im, dcd_re, dcd_im))
    for n, d in zip(('s5_lambda_re', 's5_lambda_im', 's5_log_dt', 's5_b_re', 's5_b_im', 's5_c_re', 's5_c_im'), dprep):
        G[n] = d
    pieces = [dq, dk, dv, dgate, du]
    dh0 = _matmul_cat(pieces, P['w_in'], "nt", name="l0_in_dx")
    G['w_in'] = jnp.concatenate([_matmul(h0, p, "tn", name="l0_in_dw_%d" % n) for n, p in enumerate(pieces)], axis=1)
    dx, G['mix_norm'] = _rms_bwd(x, P['mix_norm'], dh0, "l0_mix_norm_bwd", add=dx1)
    return dx, G


def _odd_fwd(x, P):
    h1 = _rms_fwd(x, P['mix_norm'], "l1_mix_norm")
    pm = _matmul(h1, P['w_main'], name="l1_in_main")
    pt = _matmul(h1, P['w_tail'], name="l1_in_tail")
    qkv = _conv_post([(pm, 0, P['conv'], 0)], _silu, 3, 1024, "l1_conv")
    g_e, beta_e = _rowwise(_gdn_gates_fn, [_blk(pt)], [P['a_log_p'], P['dtb_p']], [1024, 1024], "l1_gdn_gates")
    w, u, qd, kd, qk, tinv = _gdn_intra_call(qkv, g_e, beta_e)
    o_raw, gstates = _gdn_pass(w, u, qd, kd, qk, g_e)
    og = _rowwise(_gdn_post_fn, [_blk(o_raw), _blk(pm, 1024, 3)], [P['o_norm']], [1024], "l1_gdn_post",
                  out_dtypes=[BF16])[0]
    x1 = _matmul(og, P['w_out'], res=x, name="l1_out")
    return x1, (x, h1, pm, pt, qkv, g_e, beta_e, w, u, qd, kd, qk, tinv, o_raw, gstates, og)


def _odd_bwd(saved, dx1, P):
    x, h1, pm, pt, qkv, g_e, beta_e, w, u, qd, kd, qk, tinv, o_raw, gstates, og = saved
    G = {}
    dog = _matmul(dx1, P['w_out'], "nt", name="l1_out_dx")
    G['w_out'] = _matmul(og, dx1, "tn", name="l1_out_dw")
    do_raw, dz, G['o_norm'] = _rowwise_bwd(_gdn_post_fn, [_blk(o_raw), _blk(pm, 1024, 3)], [P['o_norm']], [dog],
                                           "l1_gdn_post_bwd", out_dtypes=[F32, BF16])
    dw, du, dqd, dkd, dqk, dg_pass = _gdn_pass(w, u, qd, kd, qk, g_e, states=gstates, do=do_raw)
    dqkv = _gdn_intra_call(qkv, g_e, beta_e, cots=(dw, du, dqd, dkd, dqk, dg_pass, tinv))
    dg_e, dbeta_e = dqkv[3], dqkv[4]
    dpt, G['a_log_p'], G['dtb_p'] = _rowwise_bwd(_gdn_gates_fn, [_blk(pt)], [P['a_log_p'], P['dtb_p']],
                                                 [dg_e, dbeta_e], "l1_gdn_gates_bwd", out_dtypes=[BF16])
    pieces, dcw = [], []
    for part in range(3):
        dxp, dwp = _conv_post_bwd([(pm, part, P['conv'], part)], _silu, 1, 1024, dqkv[part],
                                  "l1_conv_bwd_%d" % part)
        pieces.append(dxp)
        dcw.append(dwp)
    G['conv'] = jnp.concatenate(dcw, axis=1)
    pieces += [dz, dpt]
    dh1 = _matmul_cat(pieces, P['w_all'], "nt", name="l1_in_dx")
    G['w_all'] = jnp.concatenate([_matmul(h1, p, "tn", name="l1_in_dw_%d" % n) for n, p in enumerate(pieces)], axis=1)
    dx, G['mix_norm'] = _rms_bwd(x, P['mix_norm'], dh1, "l1_mix_norm_bwd", add=dx1)
    return dx, G


def _row(v):
    return v.reshape(1, -1)


def _local_step(x, mem, target, W, later_weights=None, early_grads=None):
    P0 = {
        'mix_norm': _row(W['l0_mix_norm']), 'w_in': W['l0_w_in'], 'ret_norm': _row(W['l0_ret_norm']),
        's5_lambda_re': W['l0_s5_lambda_re'], 's5_lambda_im': W['l0_s5_lambda_im'],
        's5_log_dt': W['l0_s5_log_dt'].reshape(32, 1),
        's5_b_re': W['l0_s5_b_re'].reshape(512, 64), 's5_b_im': W['l0_s5_b_im'].reshape(512, 64),
        's5_c_re': W['l0_s5_c_re'].reshape(2048, 16), 's5_c_im': W['l0_s5_c_im'].reshape(2048, 16),
        's5_d': _row(W['l0_s5_d']), 's5_w_glu': W['l0_s5_w_glu'].astype(F32), 's5_b_glu': _row(W['l0_s5_b_glu']),
        'w_out': W['l0_w_out'],
    }
    def common(L):
        return {'xa_norm': _row(W[L + 'xa_norm']), 'mem_norm': _row(W[L + 'mem_norm']), 'xa_wq': W[L + 'xa_wq'],
                'xa_wkv': W[L + 'xa_wkv'], 'xa_wo': W[L + 'xa_wo'], 'ffn_norm': _row(W[L + 'ffn_norm']),
                'ffn_w_up': W[L + 'ffn_w_up'], 'ffn_conv': W[L + 'ffn_conv'], 'ffn_w_down': W[L + 'ffn_w_down']}

    x1, s_even = _even_fwd(x, P0)
    if later_weights is not None:
        W = dict(W, **later_weights('l0_common', x1))
    C0 = common('l0_')
    x3, s_c0 = _common_fwd(x1, mem, C0, "l0_")

    if later_weights is not None:
        W = dict(W, **later_weights('l1', x3))
    w_in1 = W['l1_w_in']
    pad8 = jnp.zeros((8,), F32)
    w_all = jnp.pad(w_in1, ((0, 0), (0, 112)))
    P1 = {
        'mix_norm': _row(W['l1_mix_norm']), 'w_main': w_in1[:, :4096], 'w_tail': w_all[:, 4096:], 'w_all': w_all,
        'conv': W['l1_conv'],
        'a_log_p': _row(jnp.concatenate([pad8, W['l1_a_log'], jnp.zeros((112,), F32)])),
        'dtb_p': _row(jnp.concatenate([pad8, W['l1_dt_bias'], jnp.zeros((112,), F32)])),
        'o_norm': _row(W['l1_o_norm']), 'w_out': W['l1_w_out'],
    }
    C1 = common('l1_')
    x4, s_odd = _odd_fwd(x3, P1)
    x6, s_c1 = _common_fwd(x4, mem, C1, "l1_")
    loss_tile, dx6, d_final = _loss_grad(x6, target, _row(W['final_norm']))

    G = {'final_norm': d_final.reshape(-1)}
    dx4, g = _common_bwd(s_c1, dx6, C1, "l1_")
    for k, v in g.items():
        G['l1_' + k] = v
    dx3, g = _odd_bwd(s_odd, dx4, P1)
    G['l1_mix_norm'] = g['mix_norm']
    G['l1_w_in'] = g['w_all'][:, :4112]
    G['l1_conv'] = g['conv']
    G['l1_a_log'] = g['a_log_p'][0, 8:16]
    G['l1_dt_bias'] = g['dtb_p'][0, 8:16]
    G['l1_o_norm'] = g['o_norm']
    G['l1_w_out'] = g['w_out']
    midway = None
    if early_grads is not None:
        zero, midway = early_grads('l1', G)
        C0 = dict(C0, ffn_w_down=C0['ffn_w_down'] + zero.astype(C0['ffn_w_down'].dtype))
    dx1, g = _common_bwd(s_c0, dx3, C0, "l0_", midway=midway)
    for k, v in g.items():
        G['l0_' + k] = v
    if early_grads is not None:
        zero, midway = early_grads('l0_common', G)
        P0 = dict(P0, w_out=P0['w_out'] + zero.astype(P0['w_out'].dtype))
    dx0, g = _even_bwd(s_even, dx1, P0, midway=midway)
    for k, v in g.items():
        G['l0_' + k] = v
    return loss_tile, dx0, G


ANY = pl.BlockSpec(memory_space=pl.ANY)


def _place():
    return lax.axis_index("x"), lax.axis_index("y"), lax.axis_index("c")


def _my_chip():
    return 2 * lax.axis_index("x") + lax.axis_index("y")


def _chip_peers(x, y):
    return [(1 - x, y), (x, 1 - y), (1 - x, 1 - y)]


def _half(ref, mode, shard, j, h, split):
    r, w = shard
    rh = r // 2 if split else r
    h = h if split else 0
    if mode == 'row':
        return ref.at[pl.ds(j * r + h * rh, rh), :]
    if mode == 'col':
        return ref.at[pl.ds(h * rh, rh), pl.ds(j * w, w)]
    return ref.at[j, pl.ds(h * rh, rh), :]


def _place_shard(shard, mode, name):
    r, w = shard.shape
    dtype = BF16 if mode != 'tap' else shard.dtype
    if mode == 'tap':
        mode = 'slab'
    tr = _row_tile(r, w)
    nb = r // tr

    def kern(s_ref, o_ref):
        o_ref[...] = s_ref[...].astype(o_ref.dtype)

    if mode == 'row':
        full, o_spec = (4 * r, w), pl.BlockSpec((tr, w), lambda i: (_my_chip() * nb + i, 0))
    elif mode == 'col':
        full, o_spec = (r, 4 * w), pl.BlockSpec((tr, w), lambda i: (i, _my_chip()))
    else:
        full, o_spec = (4, r, w), pl.BlockSpec((None, tr, w), lambda i: (_my_chip(), i, 0))
    return pl.pallas_call(kern, name=name, grid=(nb,), in_specs=[pl.BlockSpec((tr, w), lambda i: (i, 0))],
                          out_specs=o_spec, out_shape=jax.ShapeDtypeStruct(full, dtype),
                          compiler_params=_cparams(("parallel",)))(shard)


def _gather_placed(fulls, modes, shards, splits):
    n = len(fulls)

    def body(*refs):
        outs = refs[n:2 * n]
        send_sems, recv_sems = refs[2 * n:]
        x, y, c = _place()
        peers = _chip_peers(x, y)
        me = 2 * x + y

        def win(a, j, h):
            return _half(outs[a], modes[a], shards[a], j, h, splits[a])

        def copy(a, k, j, h, to):
            return pltpu.make_async_remote_copy(src_ref=win(a, j, h), dst_ref=win(a, j, h),
                                                send_sem=send_sems.at[6 * a + k], recv_sem=recv_sems.at[6 * a + k],
                                                device_id=to, device_id_type=MESH)

        over_ici = [copy(a, k, me, c, (p[0], p[1], c)) for a in range(n) for k, p in enumerate(peers)]
        for cp in over_ici:
            cp.start()
        passed = []
        for a in range(n):
            for k, p in enumerate(peers):
                j = 2 * p[0] + p[1]
                copy(a, k, j, c, (p[0], p[1], c)).wait_recv()
                if splits[a]:
                    fwd = copy(a, 3 + k, j, c, (x, y, 1 - c))
                    fwd.start()
                    passed.append(fwd)
        for a in range(n):
            if splits[a]:
                for k, p in enumerate(peers):
                    copy(a, 3 + k, 2 * p[0] + p[1], 1 - c, (x, y, 1 - c)).wait_recv()
        for cp in over_ici + passed:
            cp.wait_send()

    return pl.pallas_call(
        body, name="gather_weights", in_specs=[ANY] * n, out_specs=[ANY] * n,
        out_shape=[jax.ShapeDtypeStruct(f.shape, f.dtype) for f in fulls],
        input_output_aliases={a: a for a in range(n)},
        scratch_shapes=[pltpu.SemaphoreType.DMA((6 * n,)), pltpu.SemaphoreType.DMA((6 * n,))],
    )(*fulls)


_FLIPS = [(dx, dy, dc) for dx in (0, 1) for dy in (0, 1) for dc in (0, 1) if (dx, dy, dc) != (0, 0, 0)]


def _send_other_half(gs, small, name):
    n = len(gs)

    def body(*refs):
        ins, outs = refs[:n], refs[n + 1:2 * n + 1]
        small_ref = refs[2 * n + 1]
        send_sems, recv_sems, small_send, small_recv = refs[2 * n + 2:]
        x, y, c = _place()
        me = 4 * x + 2 * y + c

        def peer(f):
            return (x ^ f[0], y ^ f[1], c ^ f[2])

        def small_copy(k, slab, to):
            return pltpu.make_async_remote_copy(src_ref=small_ref.at[slab], dst_ref=small_ref.at[slab],
                                                send_sem=small_send.at[k], recv_sem=small_recv.at[k], device_id=to,
                                                device_id_type=MESH)

        cps = []
        for a in range(n):
            rh = gs[a].shape[1] // 2
            cps.append(pltpu.make_async_remote_copy(
                src_ref=ins[a].at[:, pl.ds((1 - c) * rh, rh), :], dst_ref=outs[a], send_sem=send_sems.at[a],
                recv_sem=recv_sems.at[a], device_id=(x, y, 1 - c), device_id_type=MESH))
        smalls = [small_copy(k, me, peer(f)) for k, f in enumerate(_FLIPS)]
        for cp in cps + smalls:
            cp.start()
        for cp in cps:
            cp.wait()
        for k, f in enumerate(_FLIPS):
            p = peer(f)
            small_copy(k, 4 * p[0] + 2 * p[1] + p[2], p).wait_recv()
        for cp in smalls:
            cp.wait_send()

    outs = pl.pallas_call(
        body, name=name, in_specs=[ANY] * (n + 1), out_specs=[ANY] * (n + 1),
        out_shape=[jax.ShapeDtypeStruct((g.shape[0], g.shape[1] // 2, g.shape[2]), g.dtype) for g in gs]
        + [jax.ShapeDtypeStruct(small.shape, small.dtype)],
        input_output_aliases={n: n},
        scratch_shapes=[pltpu.SemaphoreType.DMA((n,)), pltpu.SemaphoreType.DMA((n,)),
                        pltpu.SemaphoreType.DMA((7,)), pltpu.SemaphoreType.DMA((7,))],
    )(*gs, small)
    return outs[:n], outs[n]


def _send_to_chips(ps, widths):
    n = len(ps)

    def body(*refs):
        ins, outs = refs[:n], refs[n:2 * n]
        send_sems, recv_sems = refs[2 * n:]
        x, y, c = _place()
        peers = _chip_peers(x, y)
        me = 2 * x + y

        def src(a, j):
            if ps[a].shape[0] == 4:
                return ins[a].at[j]
            return ins[a].at[0, :, pl.ds(j * widths[a], widths[a])]

        def copy(a, k, j, dst_slab, to):
            return pltpu.make_async_remote_copy(src_ref=src(a, j), dst_ref=outs[a].at[dst_slab],
                                                send_sem=send_sems.at[3 * a + k], recv_sem=recv_sems.at[3 * a + k],
                                                device_id=(to[0], to[1], c), device_id_type=MESH)

        sends = [copy(a, k, 2 * p[0] + p[1], me, p) for a in range(n) for k, p in enumerate(peers)]
        for cp in sends:
            cp.start()
        for a in range(n):
            for k, p in enumerate(peers):
                copy(a, k, me, 2 * p[0] + p[1], p).wait_recv()
        for cp in sends:
            cp.wait_send()

    return pl.pallas_call(
        body, name="send_to_chips", in_specs=[ANY] * n, out_specs=[ANY] * n,
        out_shape=[jax.ShapeDtypeStruct((4, p.shape[1], w), p.dtype) for p, w in zip(ps, widths)],
        scratch_shapes=[pltpu.SemaphoreType.DMA((3 * n,)), pltpu.SemaphoreType.DMA((3 * n,))],
    )(*ps)


def _share_halves(bufs, name):
    n = len(bufs)

    def body(*refs):
        outs = refs[n:2 * n]
        send_sems, recv_sems = refs[2 * n:]
        x, y, c = _place()
        sends, waits = [], []
        for a in range(n):
            rh = bufs[a].shape[0] // 2
            mine = outs[a].at[pl.ds(c * rh, rh), :]
            other = outs[a].at[pl.ds((1 - c) * rh, rh), :]
            sends.append(pltpu.make_async_remote_copy(src_ref=mine, dst_ref=mine, send_sem=send_sems.at[a],
                                                      recv_sem=recv_sems.at[a], device_id=(x, y, 1 - c),
                                                      device_id_type=MESH))
            waits.append(pltpu.make_async_remote_copy(src_ref=mine, dst_ref=other, send_sem=send_sems.at[a],
                                                      recv_sem=recv_sems.at[a], device_id=(x, y, 1 - c),
                                                      device_id_type=MESH))
        for cp in sends:
            cp.start()
        for cp in waits:
            cp.wait()

    return pl.pallas_call(
        body, name=name, in_specs=[ANY] * n, out_specs=[ANY] * n,
        out_shape=[jax.ShapeDtypeStruct(b.shape, b.dtype) for b in bufs],
        input_output_aliases={a: a for a in range(n)},
        scratch_shapes=[pltpu.SemaphoreType.DMA((n,)), pltpu.SemaphoreType.DMA((n,))],
    )(*bufs)


def _gather_all(mine):
    flips = [(dx, dy, dc) for dx in (0, 1) for dy in (0, 1) for dc in (0, 1) if (dx, dy, dc) != (0, 0, 0)]

    def body(x_ref, out_ref, send_sems, recv_sems, local_sem):
        x, y, c = _place()
        me = 4 * x + 2 * y + c

        def peer(f):
            return (x ^ f[0], y ^ f[1], c ^ f[2])

        def copy(k, slab, to):
            return pltpu.make_async_remote_copy(src_ref=x_ref, dst_ref=out_ref.at[slab], send_sem=send_sems.at[k],
                                                recv_sem=recv_sems.at[k], device_id=to, device_id_type=MESH)

        own = pltpu.make_async_copy(x_ref, out_ref.at[me], local_sem)
        own.start()
        sends = [copy(k, me, peer(f)) for k, f in enumerate(flips)]
        for s in sends:
            s.start()
        for k, f in enumerate(flips):
            p = peer(f)
            copy(k, 4 * p[0] + 2 * p[1] + p[2], p).wait_recv()
        for s in sends:
            s.wait_send()
        own.wait()

    return pl.pallas_call(
        body, name="gather_all", in_specs=[ANY], out_specs=ANY,
        out_shape=jax.ShapeDtypeStruct((8,) + mine.shape, mine.dtype),
        scratch_shapes=[pltpu.SemaphoreType.DMA((7,)), pltpu.SemaphoreType.DMA((7,)), pltpu.SemaphoreType.DMA],
    )(mine)


TILE_BYTES = 2 * 1024 * 1024


def _row_tile(rows, width=1024):
    for t in (512, 352, 256, 176, 128, 64, 32, 16, 8):
        if rows % t == 0 and t * width * 4 <= TILE_BYTES:
            return t
    return rows


def _pair_sum(g, got, name):
    ns, r, w = g.shape
    rh = r // 2
    tr = _row_tile(rh, w)
    nb = rh // tr

    def kern(g_ref, o_ref, out_ref):
        out_ref[...] = (g_ref[...] + o_ref[...]).astype(BF16)

    return pl.pallas_call(
        kern, name=name, grid=(ns, nb),
        in_specs=[pl.BlockSpec((None, tr, w), lambda j, i: (j, lax.axis_index("c") * nb + i, 0)),
                  pl.BlockSpec((None, tr, w), lambda j, i: (j, i, 0))],
        out_specs=pl.BlockSpec((None, tr, w), lambda j, i: (j, i, 0)),
        out_shape=jax.ShapeDtypeStruct((ns, rh, w), BF16),
        compiler_params=_cparams(("parallel", "parallel")))(g, got)


def _chip_sum(pair, recv, w, name):
    rh = pair.shape[1]
    tr = _row_tile(rh, w)
    nb = rh // tr

    def kern(own_ref, r1_ref, r2_ref, r3_ref, out_ref):
        acc = own_ref[...].astype(F32)
        for r_ref in (r1_ref, r2_ref, r3_ref):
            acc = acc + r_ref[...].astype(F32)
        out_ref[...] = acc

    if pair.shape[0] == 4:
        own_spec = pl.BlockSpec((None, tr, w), lambda i: (_my_chip(), i, 0))
    else:
        own_spec = pl.BlockSpec((None, tr, w), lambda i: (0, i, _my_chip()))
    recv_specs = [pl.BlockSpec((None, tr, w), functools.partial(lambda i, d: ((_my_chip() + d) % 4, i, 0), d=d))
                  for d in (1, 2, 3)]
    return pl.pallas_call(
        kern, name=name, grid=(nb,), in_specs=[own_spec] + recv_specs,
        out_specs=pl.BlockSpec((tr, w), lambda i: (lax.axis_index("c") * nb + i, 0)),
        out_shape=jax.ShapeDtypeStruct((2 * rh, w), F32), compiler_params=_cparams(("parallel",)))(pair, recv, recv, recv)


def _slab_sum(slabs, name):
    n, R, w = slabs.shape
    tr = _row_tile(R)

    def kern(s_ref, o_ref):
        acc = s_ref[0].astype(F32)
        for k in range(1, n):
            acc = acc + s_ref[k].astype(F32)
        o_ref[...] = acc

    return pl.pallas_call(
        kern, name=name, grid=(R // tr,), in_specs=[pl.BlockSpec((n, tr, w), lambda i: (0, i, 0))],
        out_specs=pl.BlockSpec((tr, w), lambda i: (i, 0)), out_shape=jax.ShapeDtypeStruct((R, w), F32),
        compiler_params=_cparams(("parallel",)))(slabs)


def _adamw(w, g, m, v, name):
    R, C = w.shape
    tr = _pick(R, (256, 128, 64, 32, 16, 8))

    def kern(w_ref, g_ref, m_ref, v_ref, d_ref, nm_ref, nv_ref, g_out_ref):
        gv = g_ref[...]
        g_out_ref[...] = gv
        m2 = ADAM_B1 * m_ref[...] + (1.0 - ADAM_B1) * gv
        v2 = ADAM_B2 * v_ref[...] + (1.0 - ADAM_B2) * jnp.square(gv)
        m_hat = m2 / (1.0 - ADAM_B1 ** ADAM_STEP)
        v_hat = v2 / (1.0 - ADAM_B2 ** ADAM_STEP)
        d_ref[...] = -ADAM_LR * (m_hat / (jnp.sqrt(v_hat) + ADAM_EPS) + ADAM_WD * w_ref[...])
        nm_ref[...] = m2
        nv_ref[...] = v2

    spec = pl.BlockSpec((tr, C), lambda i: (i, 0))
    return pl.pallas_call(
        kern, name=name, grid=(R // tr,), in_specs=[spec] * 4, out_specs=[spec] * 4,
        out_shape=[jax.ShapeDtypeStruct((R, C), F32)] * 4, compiler_params=_cparams(("parallel",)))(w, g, m, v)


def _pack_small(vals):
    flat = jnp.concatenate([vals[n].astype(F32).reshape(-1) for n in SMALL_NAMES])
    rows = -(-flat.shape[0] // (8 * LANES)) * 8
    return jnp.pad(flat, (0, rows * LANES - flat.shape[0])).reshape(rows, LANES)


def _unpack_small(packed, shapes):
    flat = packed.reshape(-1)
    out = {}
    off = 0
    for n in SMALL_NAMES:
        size = int(np.prod(shapes[n]))
        out[n] = flat[off:off + size].reshape(shapes[n])
        off += size
    return out


HBM = pl.BlockSpec(memory_space=pltpu.HBM)
SEM = pl.BlockSpec(memory_space=pltpu.SEMAPHORE)
DATAFLOW = pltpu.SideEffectType.DATAFLOW_SIDE_EFFECTING


def _in_hbm(a):
    return pltpu.with_memory_space_constraint(a, pltpu.HBM)


def _split_copy_start(srcs, lands, copies, after, name):
    ns, nl = len(srcs), len(lands)
    ncopy = len(copies(list(srcs), list(lands), None, None, probe=True))

    def body(*refs):
        src_refs, land_refs = refs[:ns], refs[ns:ns + nl]
        send_sems, recv_sems = refs[ns + nl + 1:ns + nl + 3]
        token = refs[-1]
        for cp in copies(src_refs, land_refs, send_sems, recv_sems):
            cp.start()
        token[...] = jnp.zeros_like(token)

    outs = pl.pallas_call(
        body, name=name,
        out_shape=(pltpu.SemaphoreType.DMA((ncopy,)), pltpu.SemaphoreType.DMA((ncopy,)),
                   *[pltpu.HBM(a.shape, a.dtype) for a in srcs], *[pltpu.HBM(a.shape, a.dtype) for a in lands],
                   jax.ShapeDtypeStruct((8, 128), F32)),
        in_specs=[HBM] * (ns + nl) + [ANY],
        out_specs=(SEM, SEM, *[HBM] * (ns + nl), pl.BlockSpec(memory_space=pltpu.VMEM)),
        input_output_aliases={i: 2 + i for i in range(ns + nl)},
        compiler_params=pltpu.CompilerParams(has_side_effects=DATAFLOW),
    )(*[_in_hbm(a) for a in srcs], *[_in_hbm(a) for a in lands], after)
    return outs[0], outs[1], outs[2:2 + ns], outs[2 + ns:2 + ns + nl], outs[-1]


def _split_copy_wait(send_sems, recv_sems, srcs, lands, copies, after, name):
    ns, nl = len(srcs), len(lands)

    def body(*refs):
        src_refs, land_refs = refs[:ns], refs[ns:ns + nl]
        send_ref, recv_ref = refs[ns + nl:ns + nl + 2]
        for cp in copies(src_refs, land_refs, send_ref, recv_ref):
            cp.wait_send()
            cp.wait_recv()

    outs = pl.pallas_call(
        body, name=name,
        out_shape=tuple(pltpu.HBM(a.shape, a.dtype) for a in list(srcs) + list(lands)),
        in_specs=[HBM] * (ns + nl) + [SEM, SEM, ANY], out_specs=tuple([HBM] * (ns + nl)),
        input_output_aliases={i: i for i in range(ns + nl)},
        compiler_params=pltpu.CompilerParams(has_side_effects=DATAFLOW),
    )(*srcs, *lands, send_sems, recv_sems, after)
    return outs[:ns], outs[ns:]


def _matrix_mode(n):
    return 'slab' if n == 'l1_w_in' else ('row' if MATRICES[n] == 0 else 'col')


def _placed(A, names):
    modes = ['slab' if n in CONVS else _matrix_mode(n) for n in names]
    fulls = [_place_shard(A[n], 'tap' if n in CONVS else m, "place_" + n) for n, m in zip(names, modes)]
    return fulls, modes


def _assembled(names, modes, outs):
    return {n: jnp.concatenate([o[j] for j in range(4)], axis=1) if m == 'slab' else o
            for n, m, o in zip(names, modes, outs)}


def _gather_weights(A, names):
    fulls, modes = _placed(A, names)
    outs = _gather_placed(fulls, modes, [A[n].shape for n in names], [n not in CONVS for n in names])
    return _assembled(names, modes, outs)


def _whole_shard_copies(modes, shards):
    def copies(src_refs, land_refs, send_sems, recv_sems, probe=False):
        if probe:
            return [None] * (3 * len(land_refs))
        x, y, c = _place()
        me = 2 * x + y
        out = []
        for a, ref in enumerate(land_refs):
            for k, p in enumerate(_chip_peers(x, y)):
                out.append(pltpu.make_async_remote_copy(
                    src_ref=_half(ref, modes[a], shards[a], me, 0, False),
                    dst_ref=_half(ref, modes[a], shards[a], me, 0, False),
                    send_sem=send_sems.at[3 * a + k], recv_sem=recv_sems.at[3 * a + k],
                    device_id=(p[0], p[1], c), device_id_type=MESH))
        return out
    return copies


def _gather_weights_start(A, names, after, tag):
    fulls, modes = _placed(A, names)
    copies = _whole_shard_copies(modes, [A[n].shape for n in names])
    send_sems, recv_sems, _, lands, zeros = _split_copy_start([], fulls, copies, after, "gather_start_" + tag)
    return (send_sems, recv_sems, lands, copies, names, modes), zeros


def _gather_weights_wait(state, after, tag):
    send_sems, recv_sems, lands, copies, names, modes = state
    _, outs = _split_copy_wait(send_sems, recv_sems, [], lands, copies, after, "gather_wait_" + tag)
    return _assembled(names, modes, outs)


def _to_chips_copies(pair_shapes, widths):
    def copies(src_refs, land_refs, send_sems, recv_sems, probe=False):
        if probe:
            return [None] * (3 * len(land_refs))
        x, y, c = _place()
        me = 2 * x + y
        out = []
        for a, (src, land) in enumerate(zip(src_refs, land_refs)):
            for k, p in enumerate(_chip_peers(x, y)):
                j = 2 * p[0] + p[1]
                part = src.at[j] if pair_shapes[a][0] == 4 else src.at[0, :, pl.ds(j * widths[a], widths[a])]
                out.append(pltpu.make_async_remote_copy(
                    src_ref=part, dst_ref=land.at[me], send_sem=send_sems.at[3 * a + k],
                    recv_sem=recv_sems.at[3 * a + k], device_id=(p[0], p[1], c), device_id_type=MESH))
        return out
    return copies


def _slabbed(G, names):
    gs, widths = [], []
    for n in names:
        g = G[n]
        mode = _matrix_mode(n)
        if mode == 'row':
            gs.append(g.reshape(4, g.shape[0] // 4, g.shape[1]))
            widths.append(g.shape[1])
        elif mode == 'col':
            gs.append(g[None])
            widths.append(g.shape[1] // 4)
        else:
            wd = g.shape[1] // 4
            gs.append(jnp.stack([g[:, j * wd:(j + 1) * wd] for j in range(4)]))
            widths.append(wd)
    return gs, widths


def _reduce_begin(G, names, small, tag):
    gs, widths = _slabbed(G, names)
    got, small = _send_other_half(gs, small, "send_other_half_" + tag)
    pairs = [_pair_sum(g, o, "pair_sum_" + n) for n, g, o in zip(names, gs, got)]
    return pairs, widths, small


def _other_half_copies(shapes):
    def copies(src_refs, land_refs, send_sems, recv_sems, probe=False):
        if probe:
            return [None] * len(land_refs)
        x, y, c = _place()
        out = []
        for a, (src, land) in enumerate(zip(src_refs, land_refs)):
            rh = shapes[a][1] // 2
            out.append(pltpu.make_async_remote_copy(
                src_ref=src.at[:, pl.ds((1 - c) * rh, rh), :], dst_ref=land, send_sem=send_sems.at[a],
                recv_sem=recv_sems.at[a], device_id=(x, y, 1 - c), device_id_type=MESH))
        return out
    return copies


def _reduce_end(names, pairs, recv, widths, tag):
    halves = [_chip_sum(p, r, w, "chip_sum_" + n) for n, p, r, w in zip(names, pairs, recv, widths)]
    return dict(zip(names, _share_halves(halves, "share_halves_" + tag)))


def _small_copies():
    def copies(src_refs, land_refs, send_sems, recv_sems, probe=False):
        if probe:
            return [None] * len(_FLIPS)
        x, y, c = _place()
        mine = land_refs[0].at[4 * x + 2 * y + c]
        return [pltpu.make_async_remote_copy(src_ref=mine, dst_ref=mine, send_sem=send_sems.at[k],
                                             recv_sem=recv_sems.at[k], device_id=(x ^ f[0], y ^ f[1], c ^ f[2]),
                                             device_id_type=MESH) for k, f in enumerate(_FLIPS)]
    return copies


def _small_slab(packed):
    me8 = 4 * lax.axis_index("x") + 2 * lax.axis_index("y") + lax.axis_index("c")
    return lax.dynamic_update_slice(jnp.zeros((8,) + packed.shape, F32), packed[None], (me8, 0, 0))


def kernel(*args):
    A = dict(zip(ARG_NAMES, args, strict=True))
    x, mem, target = A['x'][0], A['mem'][0], A['loss_target'][0]

    stages = {'l0_mixer': ['l0_w_in', 'l0_s5_w_glu', 'l0_w_out'],
              'l0_common': [n for n in MATRIX_NAMES if n.startswith(('l0_xa_', 'l0_ffn_'))],
              'l1': [n for n in MATRIX_NAMES if n.startswith('l1_')]}
    W = _gather_weights(A, stages['l0_mixer'] + list(CONVS))
    for n in SMALL_NAMES:
        if n not in CONVS:
            W[n] = A[n]
    flights = {}
    after = W['l0_w_in']
    for stage in ('l0_common', 'l1'):
        flights[stage], after = _gather_weights_start(A, stages[stage], after, stage)
    W['l0_mix_norm'] = W['l0_mix_norm'] + after[0, 0]

    reduce_state, start_tokens = {}, {}

    def early_grads(stage, G, after=None):
        names = stages[stage]
        gs, widths = _slabbed(G, names)
        d2d = _other_half_copies([g.shape for g in gs])
        lands = [lax.empty((g.shape[0], g.shape[1] // 2, g.shape[2]), g.dtype) for g in gs]
        d2d_send, d2d_recv, gs, lands, zeros = _split_copy_start(gs, lands, d2d, G['final_norm'] if after is None else after,
                                                                 "d2d_start_" + stage)
        start_tokens[stage] = zeros

        def midway(after):
            mine, got = _split_copy_wait(d2d_send, d2d_recv, gs, lands, d2d, after, "d2d_wait_" + stage)
            pairs = [_pair_sum(g, o, "pair_sum_" + n) for n, g, o in zip(names, mine, got)]
            copies = _to_chips_copies([p.shape for p in pairs], widths)
            recv = [lax.empty((4, p.shape[1], w), p.dtype) for p, w in zip(pairs, widths)]
            send_sems, recv_sems, pairs, recv, zeros2 = _split_copy_start(pairs, recv, copies, G['final_norm'],
                                                                          "reduce_start_" + stage)
            reduce_state[stage] = (send_sems, recv_sems, pairs, recv, copies, widths)
            start_tokens[stage] = zeros2
            return zeros2[0, 0]

        return zeros[0, 0], midway

    loss_tile, grad_x, G = _local_step(
        x, mem, target, W, later_weights=lambda stage, after: _gather_weights_wait(flights[stage], after, stage),
        early_grads=early_grads)
    loss = lax.psum(loss_tile[0, 0], ("x", "y", "c"))

    small_copies = _small_copies()
    small_send, small_recv, _, small_land, started = _split_copy_start(
        [], [_small_slab(_pack_small({n: G[n] for n in SMALL_NAMES}))], small_copies, grad_x, "small_start")
    _, mixer_midway = early_grads('l0_mixer', G, after=started)
    grads, deltas, new_m, new_v = {}, {}, {}, {}

    def finish(stage, after):
        send_sems, recv_sems, pairs, lands, copies, widths = reduce_state[stage]
        sent, recv = _split_copy_wait(send_sems, recv_sems, pairs, lands, copies, after, "reduce_wait_" + stage)
        g_stage = _reduce_end(stages[stage], sent, recv, widths, stage)
        for n in stages[stage]:
            deltas[n], new_m[n], new_v[n], grads[n] = _adamw(A[n], g_stage[n], A['m_' + n], A['v_' + n], "adamw_" + n)
        return deltas[stages[stage][-1]]

    after = finish('l1', start_tokens['l0_mixer'])
    mixer_midway(after)
    finish('l0_common', start_tokens['l0_mixer'])
    after = finish('l0_mixer', deltas[stages['l0_common'][-1]])
    _, (g_small,) = _split_copy_wait(small_send, small_recv, [], small_land, small_copies, after, "small_wait")
    g_small = _unpack_small(_slab_sum(g_small, "sum_small"), {n: G[n].shape for n in SMALL_NAMES})
    me = 2 * lax.axis_index("x") + lax.axis_index("y")
    for n in CONVS:
        wd = A[n].shape[1]
        g_small[n] = lax.dynamic_slice_in_dim(g_small[n], me * wd, wd, axis=1)
    flat_names = [n for n in SMALL_NAMES if n not in CONVS]

    def pack_flat(prefix):
        return _pack_small_flat({n: A[prefix + n] for n in flat_names}, flat_names)

    shapes = {n: A[n].shape for n in flat_names}
    d_s, m_s, v_s, _ = _adamw(pack_flat(''), _pack_small_flat(g_small, flat_names), pack_flat('m_'), pack_flat('v_'),
                              "adamw_small")
    d_s, m_s, v_s = (_unpack_flat(p, shapes, flat_names) for p in (d_s, m_s, v_s))

    for n in WEIGHTS:
        if n in MATRICES:
            continue
        if n in CONVS:
            deltas[n], new_m[n], new_v[n], grads[n] = _adamw(A[n], g_small[n], A['m_' + n], A['v_' + n],
                                                             "adamw_" + n)
        else:
            grads[n] = g_small[n].reshape(A[n].shape)
            deltas[n], new_m[n], new_v[n] = d_s[n], m_s[n], v_s[n]
    return (loss, grad_x[None], *[grads[n] for n in WEIGHTS], *[deltas[n] for n in WEIGHTS],
            *[new_m[n] for n in WEIGHTS], *[new_v[n] for n in WEIGHTS])


def _pack_small_flat(vals, names):
    flat = jnp.concatenate([vals[n].astype(F32).reshape(-1) for n in names])
    rows = -(-flat.shape[0] // (8 * LANES)) * 8
    return jnp.pad(flat, (0, rows * LANES - flat.shape[0])).reshape(rows, LANES)


def _unpack_flat(packed, shapes, names):
    flat = packed.reshape(-1)
    out = {}
    off = 0
    for n in names:
        size = int(np.prod(shapes[n]))
        out[n] = flat[off:off + size].reshape(shapes[n])
        off += size
    return out
```

```python
import functools
import math

import numpy as np
import jax
import jax.numpy as jnp
from jax import lax
from jax.experimental import pallas as pl
from jax.experimental.pallas import tpu as pltpu

F32 = jnp.float32
BF16 = jnp.bfloat16
EPS = 1e-6
MESH = pl.DeviceIdType.MESH

ADAM_LR = 0.001
ADAM_B1 = 0.9
ADAM_B2 = 0.999
ADAM_EPS = 1e-08
ADAM_WD = 0.01
ADAM_STEP = 10

VMEM_LIMIT_BYTES = 56 * 1024 * 1024
MATMUL_VMEM_BYTES = 44 * 1024 * 1024
LANES = 1024

WEIGHTS = ['l0_mix_norm', 'l0_w_in', 'l0_ret_norm', 'l0_s5_lambda_re', 'l0_s5_lambda_im', 'l0_s5_b_re', 'l0_s5_b_im',
           'l0_s5_c_re', 'l0_s5_c_im', 'l0_s5_d', 'l0_s5_log_dt', 'l0_s5_w_glu', 'l0_s5_b_glu', 'l0_w_out',
           'l0_xa_norm', 'l0_mem_norm', 'l0_xa_wq', 'l0_xa_wkv', 'l0_xa_wo', 'l0_ffn_norm', 'l0_ffn_w_up',
           'l0_ffn_conv', 'l0_ffn_w_down', 'l1_mix_norm', 'l1_w_in', 'l1_conv', 'l1_a_log', 'l1_dt_bias',
           'l1_o_norm', 'l1_w_out', 'l1_xa_norm', 'l1_mem_norm', 'l1_xa_wq', 'l1_xa_wkv', 'l1_xa_wo',
           'l1_ffn_norm', 'l1_ffn_w_up', 'l1_ffn_conv', 'l1_ffn_w_down', 'final_norm']
ARG_NAMES = (['x', 'mem'] + WEIGHTS + ['loss_target'] + ['m_' + w for w in WEIGHTS] + ['v_' + w for w in WEIGHTS])

MATRICES = {
    'l0_w_in': 1, 'l0_s5_w_glu': 0, 'l0_w_out': 0, 'l0_xa_wq': 0, 'l0_xa_wkv': 1, 'l0_xa_wo': 0, 'l0_ffn_w_up': 1,
    'l0_ffn_w_down': 0, 'l1_w_in': 1, 'l1_w_out': 0, 'l1_xa_wq': 0, 'l1_xa_wkv': 1, 'l1_xa_wo': 0,
    'l1_ffn_w_up': 1, 'l1_ffn_w_down': 0,
}
CONVS = ('l0_ffn_conv', 'l1_conv', 'l1_ffn_conv')
MATRIX_NAMES = [w for w in WEIGHTS if w in MATRICES]
SMALL_NAMES = [w for w in WEIGHTS if w not in MATRICES]


def _cparams(sem=None):
    return pltpu.CompilerParams(dimension_semantics=sem, vmem_limit_bytes=VMEM_LIMIT_BYTES)


def _pick(n, cands):
    for c in cands:
        if n % c == 0:
            return c
    return n


_NN = ((1,), (0,))
_NT = ((1,), (1,))
_TN = ((0,), (0,))


def _dot(a, b, dims, hi):
    if hi is not None:
        return lax.dot_general(a.astype(F32), b.astype(F32), (dims, ((), ())), precision=hi,
                               preferred_element_type=F32)
    return lax.dot_general(a.astype(BF16), b.astype(BF16), (dims, ((), ())), preferred_element_type=F32)


def _make_mm(hi):
    @jax.custom_vjp
    def nn(a, b):
        return _dot(a, b, _NN, hi)

    def nn_f(a, b):
        return nn(a, b), (a, b)

    def nn_b(r, g):
        a, b = r
        return _dot(g, b, _NT, hi), _dot(a, g, _TN, hi)

    nn.defvjp(nn_f, nn_b)

    @jax.custom_vjp
    def nt(a, b):
        return _dot(a, b, _NT, hi)

    def nt_f(a, b):
        return nt(a, b), (a, b)

    def nt_b(r, g):
        a, b = r
        return _dot(g, b, _NN, hi), _dot(g, a, _TN, hi)

    nt.defvjp(nt_f, nt_b)

    @jax.custom_vjp
    def tn(a, b):
        return _dot(a, b, _TN, hi)

    def tn_f(a, b):
        return tn(a, b), (a, b)

    def tn_b(r, g):
        a, b = r
        return _dot(b, g, _NT, hi), _dot(a, g, _NN, hi)

    tn.defvjp(tn_f, tn_b)
    return nn, nt, tn


mm, mm_nt, mm_tn = _make_mm(None)
mmh, mmh_nt, mmh_tn = _make_mm(lax.Precision.HIGHEST)
mm3, mm3_nt, mm3_tn = _make_mm(lax.Precision.HIGH)


@jax.custom_vjp
def _swap_halves(x):
    return pltpu.roll(x, 64, 1)


def _swap_f(x):
    return pltpu.roll(x, 64, 1), None


def _swap_b(_, g):
    return (pltpu.roll(g, 64, 1),)


_swap_halves.defvjp(_swap_f, _swap_b)


def _silu(x):
    return x * jax.nn.sigmoid(x)


def _rms(x, g):
    return x * lax.rsqrt(jnp.mean(x * x, axis=-1, keepdims=True) + EPS) * g


def _iota(shape, dim):
    return lax.broadcasted_iota(jnp.int32, shape, dim)


def _matmul_tiles(M, N, K, a_bytes, b_bytes, has_res, a_off):
    def divisors(n, cands):
        return [c for c in cands if n % c == 0] or [n]

    fallback = None
    for tk in divisors(K, (K, 2048, 1408, 1024, 512, 256, 128)):
        for tm in divisors(M, (1024, 512, 1408, 256, 128)):
            for tn in divisors(N, (1408, 1024, 512, 256, 128)):
                need = 2 * (tm * tk * a_bytes + tk * tn * b_bytes + (tm * tn * 4 if has_res else 0)) + 3 * tm * tn * 4
                if need > MATMUL_VMEM_BYTES or a_off % tk or a_off % tm:
                    continue
                if tm >= 256 and tn >= 256:
                    return tm, tn, tk
                fallback = fallback or (tm, tn, tk)
    return fallback


def _matmul(a, b, mode="nn", res=None, name="mm", a_cols=None, out_dtype=F32):
    a_off, a_w = (0, a.shape[1]) if a_cols is None else a_cols
    if mode == "nn":
        (M, K), (K2, N) = (a.shape[0], a_w), b.shape
    elif mode == "nt":
        (M, K), (N, K2) = (a.shape[0], a_w), b.shape
    else:
        (K, M), (K2, N) = (a.shape[0], a_w), b.shape
    assert K == K2, (a.shape, b.shape, mode)
    tm, tn, tk = _matmul_tiles(M, N, K, 2 if a.dtype == BF16 else 4, 2 if b.dtype == BF16 else 4, res is not None,
                               a_off)
    nk = K // tk
    dims = {"nn": _NN, "nt": _NT, "tn": _TN}[mode]
    ao = a_off // (tm if mode == "tn" else tk)
    assert ao * (tm if mode == "tn" else tk) == a_off
    if mode == "nn":
        a_spec = pl.BlockSpec((tm, tk), lambda i, j, k: (i, k + ao))
        b_spec = pl.BlockSpec((tk, tn), lambda i, j, k: (k, j))
    elif mode == "nt":
        a_spec = pl.BlockSpec((tm, tk), lambda i, j, k: (i, k + ao))
        b_spec = pl.BlockSpec((tn, tk), lambda i, j, k: (j, k))
    else:
        a_spec = pl.BlockSpec((tk, tm), lambda i, j, k: (k, i + ao))
        b_spec = pl.BlockSpec((tk, tn), lambda i, j, k: (k, j))
    o_spec = pl.BlockSpec((tm, tn), lambda i, j, k: (i, j))
    has_res = res is not None

    def kern(*refs):
        a_ref, b_ref = refs[:2]
        r_ref = refs[2] if has_res else None
        o_ref = refs[3] if has_res else refs[2]
        acc_ref = refs[-1] if nk > 1 else None
        k = pl.program_id(2)
        part = lax.dot_general(a_ref[...].astype(BF16), b_ref[...].astype(BF16), (dims, ((), ())),
                               preferred_element_type=F32)
        if nk == 1:
            o_ref[...] = (part + r_ref[...] if has_res else part).astype(o_ref.dtype)
            return

        @pl.when(k == 0)
        def _():
            acc_ref[...] = part

        @pl.when((k > 0) & (k < nk - 1))
        def _():
            acc_ref[...] += part

        @pl.when(k == nk - 1)
        def _():
            total = acc_ref[...] + part
            o_ref[...] = (total + r_ref[...] if has_res else total).astype(o_ref.dtype)

    in_specs = [a_spec, b_spec] + ([o_spec] if has_res else [])
    ops = (a, b) + ((res,) if has_res else ())
    return pl.pallas_call(
        kern, name=name, grid=(M // tm, N // tn, nk), in_specs=in_specs, out_specs=o_spec,
        out_shape=jax.ShapeDtypeStruct((M, N), out_dtype),
        scratch_shapes=[pltpu.VMEM((tm, tn), F32)] if nk > 1 else [],
        compiler_params=_cparams(("parallel", "parallel", "arbitrary")))(*ops)


def _matmul_cat(pieces, b, mode="nn", res=None, name="mmcat"):
    M = pieces[0].shape[0]
    widths = [p.shape[1] for p in pieces]
    K = sum(widths)
    N = b.shape[1] if mode == "nn" else b.shape[0]
    assert (b.shape[0] if mode == "nn" else b.shape[1]) == K
    tn = _pick(N, (1024, 512, 256, 128))
    a_bytes = 2 if pieces[0].dtype == BF16 else 4
    for tm in (1024, 512, 256, 128):
        need = 2 * (tm * K * a_bytes + K * tn * 2 + (tm * tn * 4 if res is not None else 0)) + 3 * tm * tn * 4
        if M % tm == 0 and need <= MATMUL_VMEM_BYTES:
            break
    npc = len(pieces)
    has_res = res is not None
    dims = _NN if mode == "nn" else _NT

    def kern(*refs):
        b_ref = refs[npc]
        o_ref = refs[-1]
        acc = refs[npc + 1][...] if has_res else None
        off = 0
        for p in range(npc):
            bp = b_ref[off:off + widths[p], :] if mode == "nn" else b_ref[:, off:off + widths[p]]
            t = lax.dot_general(refs[p][...].astype(BF16), bp.astype(BF16), (dims, ((), ())),
                                preferred_element_type=F32)
            acc = t if acc is None else acc + t
            off += widths[p]
        o_ref[...] = acc

    in_specs = [pl.BlockSpec((tm, w), lambda j, i: (i, 0)) for w in widths]
    in_specs.append(pl.BlockSpec((K, tn), lambda j, i: (0, j)) if mode == "nn"
                    else pl.BlockSpec((tn, K), lambda j, i: (j, 0)))
    o_spec = pl.BlockSpec((tm, tn), lambda j, i: (i, j))
    if has_res:
        in_specs.append(o_spec)
    ops = list(pieces) + [b] + ([res] if has_res else [])
    return pl.pallas_call(
        kern, name=name, grid=(N // tn, M // tm), in_specs=in_specs, out_specs=o_spec,
        out_shape=jax.ShapeDtypeStruct((M, N), F32), compiler_params=_cparams(("parallel", "parallel")))(*ops)


def _blk(a, width=None, colblk=0):
    return (a, a.shape[1] if width is None else width, colblk)


def _row_specs(blocked, params, ts):
    specs = []
    for (_, w, cb) in blocked:
        specs.append(pl.BlockSpec((ts, w), functools.partial(lambda i, cb: (i, cb), cb=cb)))
    for p in params:
        specs.append(pl.BlockSpec(p.shape, lambda i: (0, 0)))
    return specs


def _rowwise(fn, blocked, params, out_widths, name, ts=256, out_dtypes=None):
    S = blocked[0][0].shape[0]
    ts = min(ts, S)
    nb, npar = len(blocked), len(params)
    out_dtypes = [F32] * len(out_widths) if out_dtypes is None else out_dtypes

    def kern(*refs):
        vals = [r[...] for r in refs[:nb + npar]]
        outs = fn(*vals)
        for o_ref, o in zip(refs[nb + npar:], outs):
            o_ref[...] = o.astype(o_ref.dtype)

    return pl.pallas_call(
        kern, name=name, grid=(S // ts,), in_specs=_row_specs(blocked, params, ts),
        out_specs=[pl.BlockSpec((ts, w), lambda i: (i, 0)) for w in out_widths],
        out_shape=[jax.ShapeDtypeStruct((S, w), d) for w, d in zip(out_widths, out_dtypes)],
        compiler_params=_cparams(("parallel",)))(*[b[0] for b in blocked], *params)


def _rowwise_bwd(fn, blocked, params, cots, name, blocked_grad=None, param_grad=None, adds=None, ts=256,
                 out_dtypes=None):
    S = blocked[0][0].shape[0]
    ts = min(ts, S)
    cots = [c if isinstance(c, tuple) else _blk(c) for c in cots]
    nb, npar, nc = len(blocked), len(params), len(cots)
    blocked_grad = [True] * nb if blocked_grad is None else blocked_grad
    param_grad = [True] * npar if param_grad is None else param_grad
    adds = {} if adds is None else adds
    bidx = [i for i in range(nb) if blocked_grad[i]]
    pidx = [i for i in range(npar) if param_grad[i]]
    add_keys = sorted(adds)
    n_in = nb + npar + nc + len(add_keys)

    def kern(*refs):
        i = pl.program_id(0)
        xs = [r[...] for r in refs[:nb]]
        ps = [r[...] for r in refs[nb:nb + npar]]
        gs = [r[...] for r in refs[nb + npar:nb + npar + nc]]
        add_vals = {k: refs[nb + npar + nc + n][...] for n, k in enumerate(add_keys)}
        outs = refs[n_in:]

        def f(*diff):
            full_x = list(xs)
            full_p = list(ps)
            for n, ix in enumerate(bidx):
                full_x[ix] = diff[n]
            for n, ix in enumerate(pidx):
                full_p[ix] = diff[len(bidx) + n]
            return tuple(fn(*full_x, *full_p))

        _, vjp = jax.vjp(f, *[xs[ix] for ix in bidx], *[ps[ix] for ix in pidx])
        grads = vjp(tuple(gs))
        for n, ix in enumerate(bidx):
            g = grads[n]
            if ix in add_vals:
                g = g + add_vals[ix]
            outs[n][...] = g.astype(outs[n].dtype)
        for n in range(len(pidx)):
            o_ref = outs[len(bidx) + n]

            @pl.when(i == 0)
            def _(o_ref=o_ref):
                o_ref[...] = jnp.zeros_like(o_ref)

            o_ref[...] += grads[len(bidx) + n]

    in_specs = _row_specs(blocked, params, ts)
    in_specs += _row_specs(cots, [], ts)
    in_specs += [pl.BlockSpec((ts, adds[k].shape[1]), lambda i: (i, 0)) for k in add_keys]
    out_specs = [pl.BlockSpec((ts, blocked[ix][1]), lambda i: (i, 0)) for ix in bidx]
    out_specs += [pl.BlockSpec(params[ix].shape, lambda i: (0, 0)) for ix in pidx]
    out_dtypes = [F32] * len(bidx) if out_dtypes is None else out_dtypes
    out_shape = [jax.ShapeDtypeStruct((S, blocked[ix][1]), d) for ix, d in zip(bidx, out_dtypes)]
    out_shape += [jax.ShapeDtypeStruct(params[ix].shape, F32) for ix in pidx]
    return pl.pallas_call(
        kern, name=name, grid=(S // ts,), in_specs=in_specs, out_specs=out_specs, out_shape=out_shape,
        compiler_params=_cparams(("arbitrary",)))(*[b[0] for b in blocked], *params, *[c[0] for c in cots],
                                                    *[adds[k] for k in add_keys])


def _rms_fn(x, g):
    return (_rms(x, g),)


def _head_norm(o, n_heads, dh):
    outs = []
    for h in range(n_heads):
        oh = o[:, h * dh:(h + 1) * dh]
        outs.append(oh * lax.rsqrt(jnp.mean(oh * oh, axis=-1, keepdims=True) + EPS))
    return outs


def _ret_post_fn(o_raw, gate, ret_norm):
    o = jnp.concatenate(_head_norm(o_raw, 4, 128), axis=1)
    return (o * ret_norm * _silu(gate),)


def _s5_post_fn(y1, y2, u, d, w_glu, b_glu):
    y = y1 - y2 + d * u
    y = jax.nn.gelu(y)
    return (y * jax.nn.sigmoid(mm(y, w_glu) + b_glu),)


def _xattn_fn(q, kv):
    outs = []
    for h in range(4):
        qh = q[:, h * 256:(h + 1) * 256]
        kh = kv[:, h * 256:(h + 1) * 256]
        vh = kv[:, 1024 + h * 256:1024 + (h + 1) * 256]
        s = mm_nt(qh, kh) * (256 ** -0.5)
        s = s - lax.stop_gradient(jnp.max(s, axis=-1, keepdims=True))
        p = jnp.exp(s)
        p = p / jnp.sum(p, axis=-1, keepdims=True)
        outs.append(mm(p, vh))
    return (jnp.concatenate(outs, axis=1),)


def _softplus(x):
    return jnp.maximum(x, 0.0) + jnp.log1p(jnp.exp(-jnp.abs(x)))


def _gdn_gates_fn(pt, a_log_p, dtb_p):
    rows, cols = _iota((128, 1024), 0), _iota((128, 1024), 1)
    e_b = (rows == (cols >> 7)).astype(F32)
    e_a = (rows == (cols >> 7) + 8).astype(F32)
    beta = jax.nn.sigmoid(pt)
    g = -(jnp.exp(a_log_p) * _softplus(pt + dtb_p))
    return mmh(g, e_a), mmh(beta, e_b)


def _gdn_post_fn(o_raw, z, o_norm):
    outs = _head_norm(o_raw, 8, 128)
    o = jnp.concatenate([oh * o_norm for oh in outs], axis=1)
    return (o * _silu(z),)


def _ffn_post(up, gate):
    return _silu(gate) * up


def _shift_down(cur, prev8, sh, row8):
    if sh == 0:
        return cur
    r = pltpu.roll(cur, sh, 0)
    p = pltpu.roll(prev8, sh, 0)
    top = jnp.where(row8 < sh, p, r[0:8])
    if cur.shape[0] == 8:
        return top
    return jnp.concatenate([top, r[8:]], axis=0)


def _shift_up(cur, next8, sh, row8):
    if sh == 0:
        return cur
    ts = cur.shape[0]
    r = pltpu.roll(cur, ts - sh, 0)
    p = pltpu.roll(next8, 8 - sh, 0)
    bot = jnp.where(row8 >= 8 - sh, p, r[ts - 8:])
    return jnp.concatenate([r[:ts - 8], bot], axis=0)


def _conv_rows(cur, prev8, wrows, row8):
    k_w = len(wrows)
    out = None
    for j in range(k_w):
        t = _shift_down(cur, prev8, k_w - 1 - j, row8) * wrows[j]
        out = t if out is None else out + t
    return out


def _conv_specs(x, xoff, w, woff, ts, tc):
    r8 = ts // 8
    return [pl.BlockSpec((ts, tc), functools.partial(lambda i, j, o: (i, j + o), o=xoff)),
            pl.BlockSpec((8, tc), functools.partial(lambda i, j, o: (jnp.maximum(i * r8 - 1, 0), j + o), o=xoff)),
            pl.BlockSpec((w.shape[0], tc), functools.partial(lambda i, j, o: (0, j + o), o=woff))]


def _conv_post(srcs, post, ncol, tc, name, ts=256, out_dtype=F32):
    S = srcs[0][0].shape[0]
    ns = len(srcs)

    def kern(*refs):
        first = pl.program_id(0) == 0
        row8 = _iota((8, tc), 0)
        cs = []
        for s in range(ns):
            cur_ref, prev_ref, w_ref = refs[3 * s:3 * s + 3]
            prev = jnp.where(first, 0.0, prev_ref[...])
            wrows = [w_ref[j:j + 1, :] for j in range(w_ref.shape[0])]
            cs.append(_conv_rows(cur_ref[...], prev, wrows, row8))
        refs[3 * ns][...] = post(*cs).astype(refs[3 * ns].dtype)

    in_specs = []
    ops = []
    for (x, xoff, w, woff) in srcs:
        in_specs += _conv_specs(x, xoff, w, woff, ts, tc)
        ops += [x, x, w]
    return pl.pallas_call(
        kern, name=name, grid=(S // ts, ncol), in_specs=in_specs, out_specs=pl.BlockSpec((ts, tc), lambda i, j: (i, j)),
        out_shape=jax.ShapeDtypeStruct((S, ncol * tc), out_dtype),
        compiler_params=_cparams(("parallel", "parallel")))(*ops)


def _conv_post_bwd(srcs, post, ncol, tc, cot, name, ts=256):
    S = srcs[0][0].shape[0]
    ns = len(srcs)
    r8 = ts // 8
    nblk8 = S // 8
    nrow = S // ts

    def kern(*refs):
        i = pl.program_id(1)
        row8 = _iota((8, tc), 0)
        g_ref, gn_ref = refs[4 * ns:4 * ns + 2]
        outs = refs[4 * ns + 2:]
        xs, xps, ws, cs, cns = [], [], [], [], []
        for s in range(ns):
            cur_ref, prev_ref, next_ref, w_ref = refs[4 * s:4 * s + 4]
            xcur = cur_ref[...]
            xprev = jnp.where(i == 0, 0.0, prev_ref[...])
            wrows = [w_ref[j:j + 1, :] for j in range(w_ref.shape[0])]
            xs.append(xcur)
            xps.append(xprev)
            ws.append(wrows)
            cs.append(_conv_rows(xcur, xprev, wrows, row8))
            cns.append(_conv_rows(next_ref[...], xcur[ts - 8:], wrows, row8))
        _, vjp = jax.vjp(lambda *c: post(*c), *cs)
        dcs = vjp(g_ref[...])
        _, vjp_next = jax.vjp(lambda *c: post(*c), *cns)
        dcns = vjp_next(jnp.where(i == nrow - 1, 0.0, gn_ref[...]))
        for s in range(ns):
            dx_ref, dw_ref = outs[2 * s], outs[2 * s + 1]

            @pl.when(i == 0)
            def _(dw_ref=dw_ref):
                dw_ref[...] = jnp.zeros_like(dw_ref)

            k_w = len(ws[s])
            dx = None
            for j in range(k_w):
                sh = k_w - 1 - j
                t = _shift_up(dcs[s], dcns[s], sh, row8) * ws[s][j]
                dx = t if dx is None else dx + t
                dw_ref[j:j + 1, :] += jnp.sum(dcs[s] * _shift_down(xs[s], xps[s], sh, row8), axis=0, keepdims=True)
            dx_ref[...] = dx.astype(dx_ref.dtype)

    def nxt(i):
        return jnp.minimum((i + 1) * r8, nblk8 - 1)

    in_specs, ops = [], []
    for (x, xoff, w, woff) in srcs:
        in_specs += [pl.BlockSpec((ts, tc), functools.partial(lambda j, i, o: (i, j + o), o=xoff)),
                     pl.BlockSpec((8, tc), functools.partial(lambda j, i, o: (jnp.maximum(i * r8 - 1, 0), j + o),
                                                             o=xoff)),
                     pl.BlockSpec((8, tc), functools.partial(lambda j, i, o: (nxt(i), j + o), o=xoff)),
                     pl.BlockSpec((w.shape[0], tc), functools.partial(lambda j, i, o: (0, j + o), o=woff))]
        ops += [x, x, x, w]
    in_specs += [pl.BlockSpec((ts, tc), lambda j, i: (i, j)), pl.BlockSpec((8, tc), lambda j, i: (nxt(i), j))]
    ops += [cot, cot]
    out_specs, out_shape = [], []
    for (x, xoff, w, woff) in srcs:
        out_specs += [pl.BlockSpec((ts, tc), lambda j, i: (i, j)), pl.BlockSpec((w.shape[0], tc), lambda j, i: (0, j))]
        out_shape += [jax.ShapeDtypeStruct((S, ncol * tc), BF16), jax.ShapeDtypeStruct((w.shape[0], ncol * tc), F32)]
    return pl.pallas_call(
        kern, name=name, grid=(ncol, nrow), in_specs=in_specs, out_specs=out_specs, out_shape=out_shape,
        compiler_params=_cparams(("parallel", "arbitrary")))(*ops)


def _ret_tables(S):
    H, C, dh = 4, 128, 128
    lg = jnp.log1p(-jnp.exp2(-5.0 - jnp.arange(H, dtype=F32)))
    idx = jnp.arange(C, dtype=F32)
    diff = idx[:, None] - idx[None, :]
    causal = diff >= 0
    intra = jnp.where(causal, jnp.exp(lg[:, None, None] * jnp.where(causal, diff, 0.0)), 0.0)
    kdec = jnp.broadcast_to(jnp.exp(lg[:, None] * (C - 1 - idx))[:, :, None], (H, C, dh))
    qdec = jnp.broadcast_to(jnp.exp(lg[:, None] * (idx + 1))[:, :, None], (H, C, dh))
    cdec = jnp.broadcast_to(jnp.exp(lg * C)[:, None, None], (H, dh, dh))
    half = dh // 2
    inv = jnp.exp(-math.log(10000.0) * jnp.arange(half, dtype=F32) / half)
    ang = jnp.arange(S).astype(F32)[:, None] * inv[None, :]
    cos, sin = jnp.cos(ang), jnp.sin(ang)
    cosf = jnp.concatenate([cos, cos], axis=1)
    sinf = jnp.concatenate([-sin, sin], axis=1)
    return cosf, sinf, intra, kdec, qdec, cdec


def _ret_chunk(q, k, v, cosf, sinf, intra, kdec, qdec, cdec, state):
    hs = range(len(q))
    qr = [q[h] * cosf + _swap_halves(q[h]) * sinf for h in hs]
    kr = [(k[h] * cosf + _swap_halves(k[h]) * sinf) * (128 ** -0.5) for h in hs]
    scores = [mm_nt(qr[h], kr[h]) * intra[h] for h in hs]
    inner = [mm(scores[h], v[h]) for h in hs]
    kv = [mm_tn(kr[h] * kdec[h], v[h]) for h in hs]
    cross = [mm(qr[h] * qdec[h], state[h]) for h in hs]
    return [inner[h] + cross[h] for h in hs], [state[h] * cdec[h] + kv[h] for h in hs]


RET_H = 4


def _ret_call(proj, tabs, states=None, do=None):
    S = proj.shape[0]
    N = S // 128
    bwd = do is not None

    def nn(n):
        return N - 1 - n if bwd else n

    qkv_spec = pl.BlockSpec((128, 3 * 512), lambda n: (nn(n), 0))
    pos = pl.BlockSpec((128, 128), lambda n: (nn(n), 0))
    tab = pl.BlockSpec((RET_H, 128, 128), lambda n: (0, 0, 0))
    st_spec = pl.BlockSpec((None, RET_H, 128, 128), lambda n: (nn(n), 0, 0, 0))
    o_spec = pl.BlockSpec((128, 512), lambda n: (nn(n), 0))

    def kern(*refs):
        x_ref, c_ref, s_ref, i_ref, kd_ref, qd_ref, cd_ref = refs[:7]
        carry = refs[-1]
        heads = range(RET_H)

        @pl.when(pl.program_id(0) == 0)
        def _():
            carry[...] = jnp.zeros_like(carry)

        def cols(ref, off=0):
            return [ref[:, _hs(off + h)] for h in heads]

        def tabs_of(ref):
            return [ref[h] for h in heads]

        consts = (c_ref[...], s_ref[...], tabs_of(i_ref), tabs_of(kd_ref), tabs_of(qd_ref), tabs_of(cd_ref))
        qkv = (cols(x_ref), cols(x_ref, RET_H), cols(x_ref, 2 * RET_H))
        if bwd:
            sp_ref, do_ref = refs[7:9]
            outs = refs[9:12]
            _, vjp = jax.vjp(lambda q, k, v, s: _ret_chunk(q, k, v, *consts, s), *qkv, tabs_of(sp_ref))
            dq, dk, dv, ds = vjp((cols(do_ref), tabs_of(carry)))
            for h in heads:
                for o_ref, d in zip(outs, (dq[h], dk[h], dv[h])):
                    o_ref[:, _hs(h)] = d.astype(o_ref.dtype)
                carry[h] = ds[h]
        else:
            o_ref, sp_ref = refs[7:9]
            state = tabs_of(carry)
            out, new = _ret_chunk(*qkv, *consts, state)
            for h in heads:
                sp_ref[h] = state[h]
                o_ref[:, _hs(h)] = out[h]
                carry[h] = new[h]

    in_specs = [qkv_spec, pos, pos, tab, tab, tab, tab]
    if bwd:
        in_specs += [st_spec, o_spec]
        out_specs = [o_spec] * 3
        out_shape = [jax.ShapeDtypeStruct((S, 512), BF16)] * 3
        ops = (proj, *tabs, states, do)
    else:
        out_specs = [o_spec, st_spec]
        out_shape = [jax.ShapeDtypeStruct((S, 512), F32), jax.ShapeDtypeStruct((N, RET_H, 128, 128), F32)]
        ops = (proj, *tabs)
    return pl.pallas_call(
        kern, name="ret_bwd" if bwd else "ret_fwd", grid=(N,), in_specs=in_specs, out_specs=out_specs,
        out_shape=out_shape, scratch_shapes=[pltpu.VMEM((RET_H, 128, 128), F32)],
        compiler_params=_cparams(("arbitrary",)))(*ops)


GDN_C = 64
GDN_H = 8


def _unit_lower_inverse(a_mats, eye):
    p = [-a for a in a_mats]
    t = [eye + x for x in p]
    for _ in range(5):
        p = [mm3(x, x) for x in p]
        t = [mm3(y, eye + x) for y, x in zip(t, p)]
    return t


@jax.custom_vjp
def _known_inverse(a_mat, t_mat):
    return t_mat


def _known_inverse_f(a_mat, t_mat):
    return t_mat, t_mat


def _known_inverse_b(t_mat, g):
    return -mm3_tn(t_mat, mm3_nt(g, t_mat)), jnp.zeros_like(t_mat)


_known_inverse.defvjp(_known_inverse_f, _known_inverse_b)


def _gdn_intra(q, k, v, g_b, beta_b, t_known=None):
    c = GDN_C
    hs = range(len(q))
    q = [x * lax.rsqrt(jnp.sum(x * x, axis=-1, keepdims=True) + EPS) * (128 ** -0.5) for x in q]
    k = [x * lax.rsqrt(jnp.sum(x * x, axis=-1, keepdims=True) + EPS) for x in k]
    ri, ci = _iota((c, c), 0), _iota((c, c), 1)
    incl = ri >= ci
    strict = ri > ci
    eye = (ri == ci).astype(F32)
    lower = incl.astype(F32)
    gc_b = [mm3(lower, g) for g in g_b]
    gl_b = [jnp.sum(g, axis=0, keepdims=True) for g in g_b]
    kb = [k[h] * beta_b[h] for h in hs]
    vb = [v[h] * beta_b[h] for h in hs]
    gcc = [g[:, :c] for g in gc_b]
    decay = [jnp.where(incl, jnp.exp(jnp.where(incl, g - g.T, 0.0)), 0.0) for g in gcc]
    a_mat = [jnp.where(strict, mm_nt(kb[h], k[h]) * decay[h], 0.0) for h in hs]
    if t_known is None:
        t_mat = _unit_lower_inverse(a_mat, eye)
    else:
        t_mat = [_known_inverse(a_mat[h], t_known[h]) for h in hs]
    egc = [jnp.exp(g) for g in gc_b]
    w = [mm(t_mat[h], kb[h] * egc[h]) for h in hs]
    u = [mm(t_mat[h], vb[h]) for h in hs]
    qk = [jnp.where(incl, mm_nt(q[h], k[h]) * decay[h], 0.0) for h in hs]
    q_dec = [q[h] * egc[h] for h in hs]
    k_dec = [k[h] * jnp.exp(gl_b[h] - gc_b[h]) for h in hs]
    return w, u, q_dec, k_dec, qk, t_mat


def _gdn_step(w, u, q_dec, k_dec, qk, g_b, state):
    hs = range(len(w))
    gl_s = [jnp.sum(g, axis=0, keepdims=True) for g in g_b]
    ws = [mm(w[h], state[h]) for h in hs]
    qs = [mm(q_dec[h], state[h]) for h in hs]
    v_new = [u[h] - ws[h] for h in hs]
    o = [qs[h] + mm(qk[h], v_new[h]) for h in hs]
    new = [state[h] * jnp.exp(gl_s[h]) + mm_tn(k_dec[h], v_new[h]) for h in hs]
    return o, new


def _hs(h):
    return slice(h * 128, (h + 1) * 128)


def _gdn_intra_call(qkv, g_e, beta_e, cots=None):
    S = qkv.shape[0]
    N = S // GDN_C
    bwd = cots is not None
    row = pl.BlockSpec((GDN_C, 1024), lambda n: (n, 0))
    qkv_spec = pl.BlockSpec((GDN_C, 3072), lambda n: (n, 0))
    qk_spec = pl.BlockSpec((GDN_H, GDN_C, GDN_C), lambda n: (0, n, 0))

    def kern(*refs):
        x_ref, g_ref, b_ref = refs[:3]
        heads = range(GDN_H)

        def cols(ref, off=0):
            return [ref[:, _hs(off + h)] for h in heads]

        args = (cols(x_ref), cols(x_ref, 8), cols(x_ref, 16), cols(g_ref), cols(b_ref))
        if bwd:
            dw_ref, du_ref, dqd_ref, dkd_ref, dqk_ref, dgadd_ref, t_ref = refs[3:10]
            outs = refs[10:]
            t_known = [t_ref[h] for h in heads]
            _, vjp = jax.vjp(lambda *a: _gdn_intra(*a, t_known=t_known)[:5], *args)
            dq, dk, dv, dg, db = vjp((cols(dw_ref), cols(du_ref), cols(dqd_ref), cols(dkd_ref),
                                      [dqk_ref[h] for h in heads]))
            dgadd = cols(dgadd_ref)
            for h in heads:
                for o_ref, d in zip(outs, (dq[h], dk[h], dv[h], dg[h] + dgadd[h], db[h])):
                    o_ref[:, _hs(h)] = d
        else:
            w, u, qd, kd, qk, t_mat = _gdn_intra(*args)
            for h in heads:
                for o_ref, o in zip(refs[3:7], (w[h], u[h], qd[h], kd[h])):
                    o_ref[:, _hs(h)] = o
                refs[7][h] = qk[h]
                refs[8][h] = t_mat[h]

    big = jax.ShapeDtypeStruct((S, 1024), F32)
    sq = jax.ShapeDtypeStruct((GDN_H, S, GDN_C), F32)
    if bwd:
        in_specs = [qkv_spec, row, row, row, row, row, row, qk_spec, row, qk_spec]
        out_specs, out_shape = [row] * 5, [big] * 5
        ops = (qkv, g_e, beta_e) + tuple(cots)
    else:
        in_specs = [qkv_spec, row, row]
        out_specs = [row] * 4 + [qk_spec, qk_spec]
        out_shape = [big] * 4 + [sq, sq]
        ops = (qkv, g_e, beta_e)
    return pl.pallas_call(
        kern, name="gdn_intra_bwd" if bwd else "gdn_intra", grid=(N,), in_specs=in_specs, out_specs=out_specs,
        out_shape=out_shape, compiler_params=_cparams(("parallel",)))(*ops)


def _gdn_pass(w, u, qd, kd, qk, g_e, states=None, do=None):
    S = w.shape[0]
    N = S // GDN_C
    bwd = do is not None

    def nn(n):
        return N - 1 - n if bwd else n

    row = pl.BlockSpec((GDN_C, 1024), lambda n: (nn(n), 0))
    qk_spec = pl.BlockSpec((GDN_H, GDN_C, GDN_C), lambda n: (0, nn(n), 0))
    st_spec = pl.BlockSpec((None, GDN_H, 128, 128), lambda n: (nn(n), 0, 0, 0))

    def kern(*refs):
        w_ref, u_ref, qd_ref, kd_ref, qk_ref, g_ref = refs[:6]
        carry = refs[-1]

        @pl.when(pl.program_id(0) == 0)
        def _():
            carry[...] = jnp.zeros_like(carry)

        heads = range(GDN_H)

        def cols(ref):
            return [ref[:, _hs(h)] for h in heads]

        args = (cols(w_ref), cols(u_ref), cols(qd_ref), cols(kd_ref), [qk_ref[h] for h in heads], cols(g_ref))
        if bwd:
            sp_ref, do_ref = refs[6:8]
            outs = refs[8:14]
            _, vjp = jax.vjp(_gdn_step, *args, [sp_ref[h] for h in heads])
            dw, du, dqd, dkd, dqk, dg, ds = vjp((cols(do_ref), [carry[h] for h in heads]))
            for h in heads:
                for o_ref, d in zip(outs[:4], (dw[h], du[h], dqd[h], dkd[h])):
                    o_ref[:, _hs(h)] = d
                outs[4][h] = dqk[h]
                outs[5][:, _hs(h)] = dg[h]
                carry[h] = ds[h]
        else:
            o_ref, sp_ref = refs[6:8]
            state = [carry[h] for h in heads]
            o, new = _gdn_step(*args, state)
            for h in heads:
                sp_ref[h] = state[h]
                o_ref[:, _hs(h)] = o[h]
                carry[h] = new[h]

    big = jax.ShapeDtypeStruct((S, 1024), F32)
    in_specs = [row, row, row, row, qk_spec, row]
    if bwd:
        in_specs += [st_spec, row]
        out_specs = [row] * 4 + [qk_spec, row]
        out_shape = [big] * 4 + [jax.ShapeDtypeStruct((GDN_H, S, GDN_C), F32), big]
        ops = (w, u, qd, kd, qk, g_e, states, do)
    else:
        out_specs = [row, st_spec]
        out_shape = [big, jax.ShapeDtypeStruct((N, GDN_H, 128, 128), F32)]
        ops = (w, u, qd, kd, qk, g_e)
    return pl.pallas_call(
        kern, name="gdn_pass_bwd" if bwd else "gdn_pass", grid=(N,), in_specs=in_specs, out_specs=out_specs,
        out_shape=out_shape, scratch_shapes=[pltpu.VMEM((GDN_H, 128, 128), F32)],
        compiler_params=_cparams(("arbitrary",)))(*ops)


def _s5_prep_fn(lr, li, ldt, br, bi, cr, ci):
    dt = jnp.exp(ldt)
    mag = jnp.exp(lr * dt)
    a_re = mag * jnp.cos(li * dt)
    a_im = mag * jnp.sin(li * dt)
    den = lr * lr + li * li
    z_re = ((a_re - 1.0) * lr + a_im * li) / den
    z_im = (a_im * lr - (a_re - 1.0) * li) / den
    e1 = ((_iota((512, 32), 0) >> 4) == _iota((512, 32), 1)).astype(F32)
    zr_e = mmh(e1, z_re)
    zi_e = mmh(e1, z_im)
    bb_re = zr_e * br - zi_e * bi
    bb_im = zr_e * bi + zi_e * br
    t1 = ((_iota((64, 2048), 1) & 63) == _iota((64, 2048), 0)).astype(F32)
    m1 = (_iota((512, 2048), 0) >> 4) == (_iota((512, 2048), 1) >> 6)
    bd_re = jnp.where(m1, mmh(bb_re, t1), 0.0)
    bd_im = jnp.where(m1, mmh(bb_im, t1), 0.0)
    t2 = ((_iota((16, 512), 1) & 15) == _iota((16, 512), 0)).astype(F32)
    m2 = (_iota((2048, 512), 0) >> 6) == (_iota((2048, 512), 1) >> 4)
    cd_re = jnp.where(m2, mmh(cr, t2), 0.0)
    cd_im = jnp.where(m2, mmh(ci, t2), 0.0)
    return a_re, a_im, bd_re, bd_im, cd_re, cd_im


_PREP_OUT = [(32, 64), (32, 64), (512, 2048), (512, 2048), (2048, 512), (2048, 512)]


def _s5_prep(params, cots=None):
    bwd = cots is not None

    def kern(*refs):
        vals = [r[...] for r in refs[:7]]
        if bwd:
            gs = tuple(r[...] for r in refs[7:13])
            _, vjp = jax.vjp(_s5_prep_fn, *vals)
            for o_ref, d in zip(refs[13:], vjp(gs)):
                o_ref[...] = d
        else:
            for o_ref, o in zip(refs[7:], _s5_prep_fn(*vals)):
                o_ref[...] = o

    if bwd:
        out_shape = [jax.ShapeDtypeStruct(p.shape, F32) for p in params]
        ops = list(params) + list(cots)
    else:
        out_shape = [jax.ShapeDtypeStruct(s, F32) for s in _PREP_OUT]
        ops = list(params)
    return pl.pallas_call(kern, name="s5_prep_bwd" if bwd else "s5_prep", out_shape=out_shape,
                          compiler_params=_cparams())(*ops)


def _cmul(ar, ai, br, bi):
    return ar * br - ai * bi, ar * bi + ai * br


def _power_table(ar, ai, row8, descending):
    pr, pi = ar, ai
    tr = jnp.zeros(row8.shape, F32)
    ti = jnp.zeros(row8.shape, F32)
    for n in range(8):
        r = 7 - n if descending else n
        tr = jnp.where(row8 == r, pr, tr)
        ti = jnp.where(row8 == r, pi, ti)
        if n < 7:
            pr, pi = _cmul(pr, pi, ar, ai)
    return tr, ti


def _tile_scan(xr, xi, pows, row8, up):
    for d, (pr, pi) in zip((1, 2, 4), pows):
        if up:
            sr = jnp.where(row8 < 8 - d, pltpu.roll(xr, 8 - d, 0), 0.0)
            si = jnp.where(row8 < 8 - d, pltpu.roll(xi, 8 - d, 0), 0.0)
        else:
            sr = jnp.where(row8 >= d, pltpu.roll(xr, d, 0), 0.0)
            si = jnp.where(row8 >= d, pltpu.roll(xi, d, 0), 0.0)
        mr, mi = _cmul(pr, pi, sr, si)
        xr, xi = xr + mr, xi + mi
    return xr, xi


def _pick_row(x, row8, r):
    return jnp.sum(jnp.where(row8 == r, x, 0.0), axis=0, keepdims=True)


SCAN_LB = 512
SCAN_TS = 512
SCAN_UNROLL = 4


def _scan_fwd(bu_re, bu_im, a_re, a_im):
    S, L = bu_re.shape
    ts, lb = min(SCAN_TS, S), SCAN_LB
    nt = ts // 8

    def kern(br_ref, bi_ref, ar_ref, ai_ref, or_ref, oi_ref, cr_ref, ci_ref):
        @pl.when(pl.program_id(1) == 0)
        def _():
            cr_ref[...] = jnp.zeros_like(cr_ref)
            ci_ref[...] = jnp.zeros_like(ci_ref)

        row8 = _iota((8, lb), 0)
        ar, ai = ar_ref[...], ai_ref[...]
        a2 = _cmul(ar, ai, ar, ai)
        a4 = _cmul(*a2, *a2)
        pows = ((ar, ai), a2, a4)
        tr, ti = _power_table(ar, ai, row8, False)

        def body(i, carry):
            cr, ci = carry
            off = pl.multiple_of(i * 8, 8)
            xr, xi = _tile_scan(br_ref[pl.ds(off, 8), :], bi_ref[pl.ds(off, 8), :], pows, row8, False)
            mr, mi = _cmul(tr, ti, cr, ci)
            xr, xi = xr + mr, xi + mi
            or_ref[pl.ds(off, 8), :] = xr
            oi_ref[pl.ds(off, 8), :] = xi
            return _pick_row(xr, row8, 7), _pick_row(xi, row8, 7)

        cr, ci = lax.fori_loop(0, nt, body, (cr_ref[...], ci_ref[...]), unroll=SCAN_UNROLL)
        cr_ref[...] = cr
        ci_ref[...] = ci

    blk = pl.BlockSpec((ts, lb), lambda j, i: (i, j))
    par = pl.BlockSpec((1, lb), lambda j, i: (0, j))
    return pl.pallas_call(
        kern, name="s5_scan_fwd", grid=(L // lb, S // ts), in_specs=[blk, blk, par, par], out_specs=[blk, blk],
        out_shape=[jax.ShapeDtypeStruct((S, L), F32)] * 2,
        scratch_shapes=[pltpu.VMEM((1, lb), F32), pltpu.VMEM((1, lb), F32)],
        compiler_params=_cparams(("parallel", "arbitrary")))(bu_re, bu_im, a_re, a_im)


def _scan_bwd(dst_re, dst_im, st_re, st_im, a_re, a_im):
    S, L = dst_re.shape
    ts, lb = min(SCAN_TS, S), SCAN_LB
    nt = ts // 8
    nb = S // ts
    r8 = ts // 8

    def kern(dr_ref, di_ref, sr_ref, si_ref, pr_ref, pi_ref, ar_ref, ai_ref, gr_ref, gi_ref, dar_ref, dai_ref,
             cr_ref, ci_ref):
        step = pl.program_id(1)
        blk = nb - 1 - step

        @pl.when(step == 0)
        def _():
            cr_ref[...] = jnp.zeros_like(cr_ref)
            ci_ref[...] = jnp.zeros_like(ci_ref)
            dar_ref[...] = jnp.zeros_like(dar_ref)
            dai_ref[...] = jnp.zeros_like(dai_ref)

        row8 = _iota((8, lb), 0)
        ar, ai = ar_ref[...], ai_ref[...]
        nai = -ai
        a2 = _cmul(ar, nai, ar, nai)
        a4 = _cmul(*a2, *a2)
        pows = ((ar, nai), a2, a4)
        tr, ti = _power_table(ar, nai, row8, True)
        halo_r = jnp.where(blk == 0, 0.0, pr_ref[...])
        halo_i = jnp.where(blk == 0, 0.0, pi_ref[...])

        def body(n, carry):
            cr, ci, acc_r, acc_i = carry
            i = nt - 1 - n
            off = pl.multiple_of(i * 8, 8)
            gr, gi = _tile_scan(dr_ref[pl.ds(off, 8), :], di_ref[pl.ds(off, 8), :], pows, row8, True)
            mr, mi = _cmul(tr, ti, cr, ci)
            gr, gi = gr + mr, gi + mi
            gr_ref[pl.ds(off, 8), :] = gr
            gi_ref[pl.ds(off, 8), :] = gi
            poff = pl.multiple_of(jnp.maximum(i - 1, 0) * 8, 8)
            before_r = jnp.where(i == 0, halo_r, sr_ref[pl.ds(poff, 8), :])
            before_i = jnp.where(i == 0, halo_i, si_ref[pl.ds(poff, 8), :])
            last_r = _pick_row(before_r, row8, 7)
            last_i = _pick_row(before_i, row8, 7)
            spr = jnp.where(row8 >= 1, pltpu.roll(sr_ref[pl.ds(off, 8), :], 1, 0), last_r)
            spi = jnp.where(row8 >= 1, pltpu.roll(si_ref[pl.ds(off, 8), :], 1, 0), last_i)
            acc_r = acc_r + gr * spr + gi * spi
            acc_i = acc_i + gi * spr - gr * spi
            return _pick_row(gr, row8, 0), _pick_row(gi, row8, 0), acc_r, acc_i

        zero = jnp.zeros((8, lb), F32)
        cr, ci, acc_r, acc_i = lax.fori_loop(0, nt, body, (cr_ref[...], ci_ref[...], zero, zero), unroll=SCAN_UNROLL)
        cr_ref[...] = cr
        ci_ref[...] = ci
        dar_ref[...] += jnp.sum(acc_r, axis=0, keepdims=True)
        dai_ref[...] += jnp.sum(acc_i, axis=0, keepdims=True)

    blk = pl.BlockSpec((ts, lb), lambda j, i: (nb - 1 - i, j))
    halo = pl.BlockSpec((8, lb), lambda j, i: (jnp.maximum((nb - 1 - i) * r8 - 1, 0), j))
    par = pl.BlockSpec((1, lb), lambda j, i: (0, j))
    return pl.pallas_call(
        kern, name="s5_scan_bwd", grid=(L // lb, nb), in_specs=[blk, blk, blk, blk, halo, halo, par, par],
        out_specs=[blk, blk, par, par],
        out_shape=[jax.ShapeDtypeStruct((S, L), F32)] * 2 + [jax.ShapeDtypeStruct((1, L), F32)] * 2,
        scratch_shapes=[pltpu.VMEM((1, lb), F32), pltpu.VMEM((1, lb), F32)],
        compiler_params=_cparams(("parallel", "arbitrary")))(dst_re, dst_im, st_re, st_im, st_re, st_im, a_re, a_im)


def _loss_grad(x, target, gain, ts=256):
    S, D = x.shape

    def kern(x_ref, t_ref, g_ref, loss_ref, dx_ref, dg_ref):
        i = pl.program_id(0)
        tgt = t_ref[...]

        def f(xv, gv):
            err = _rms(xv, gv) - tgt
            return 0.5 * jnp.mean(err * err, axis=-1, keepdims=True)

        rowloss, vjp = jax.vjp(f, x_ref[...], g_ref[...])
        dx, dg = vjp(jnp.ones_like(rowloss))
        dx_ref[...] = dx

        @pl.when(i == 0)
        def _():
            loss_ref[...] = jnp.zeros_like(loss_ref)
            dg_ref[...] = jnp.zeros_like(dg_ref)

        loss_ref[...] += jnp.broadcast_to(jnp.sum(rowloss, axis=0, keepdims=True), loss_ref.shape)
        dg_ref[...] += dg

    row = pl.BlockSpec((ts, D), lambda i: (i, 0))
    return pl.pallas_call(
        kern, name="loss_grad", grid=(S // ts,), in_specs=[row, row, pl.BlockSpec((1, D), lambda i: (0, 0))],
        out_specs=[pl.BlockSpec((8, 128), lambda i: (0, 0)), row, pl.BlockSpec((1, D), lambda i: (0, 0))],
        out_shape=[jax.ShapeDtypeStruct((8, 128), F32), jax.ShapeDtypeStruct((S, D), F32),
                   jax.ShapeDtypeStruct((1, D), F32)],
        compiler_params=_cparams(("arbitrary",)))(x, target, gain)


def _rms_fwd(x, g, name):
    return _rowwise(_rms_fn, [_blk(x)], [g], [x.shape[1]], name, out_dtypes=[BF16])[0]


def _rms_bwd(x, g, dy, name, add=None):
    return _rowwise_bwd(_rms_fn, [_blk(x)], [g], [dy], name, adds=None if add is None else {0: add})


FFN_TC = 1408


def _common_fwd(x, mem, P, L):
    hx = _rms_fwd(x, P['xa_norm'], L + "xa_norm")
    q = _matmul(hx, P['xa_wq'], name=L + "xa_q")
    memn = _rms_fwd(mem, P['mem_norm'], L + "mem_norm")
    kv = _matmul(memn, P['xa_wkv'], name=L + "xa_kv")
    att = _rowwise(_xattn_fn, [_blk(q)], [kv], [1024], L + "xattn", out_dtypes=[BF16])[0]
    x2 = _matmul(att, P['xa_wo'], res=x, name=L + "xa_o")
    hf = _rms_fwd(x2, P['ffn_norm'], L + "ffn_norm")
    hu = _matmul(hf, P['ffn_w_up'], name=L + "ffn_up")
    cw = P['ffn_conv']
    act = _conv_post([(hu, 0, cw, 0), (hu, 2, cw, 2)], _ffn_post, 2, FFN_TC, L + "ffn_conv", out_dtype=BF16)
    x3 = _matmul(act, P['ffn_w_down'], res=x2, name=L + "ffn_down")
    return x3, (x, mem, hx, q, memn, kv, att, x2, hf, hu, act)


def _common_bwd(saved, dx3, P, L, midway=None):
    x, mem, hx, q, memn, kv, att, x2, hf, hu, act = saved
    G = {}
    dact = _matmul(dx3, P['ffn_w_down'], "nt", name=L + "ffn_down_dx")
    G['ffn_w_down'] = _matmul(act, dx3, "tn", name=L + "ffn_down_dw")
    cw = P['ffn_conv']
    dhu_u, dcw_u, dhu_g, dcw_g = _conv_post_bwd([(hu, 0, cw, 0), (hu, 2, cw, 2)], _ffn_post, 2, FFN_TC, dact,
                                                L + "ffn_conv_bwd")
    G['ffn_conv'] = jnp.concatenate([dcw_u, dcw_g], axis=1)
    dhf = _matmul_cat([dhu_u, dhu_g], P['ffn_w_up'], "nt", name=L + "ffn_up_dx")
    G['ffn_w_up'] = jnp.concatenate([_matmul(hf, dhu_u, "tn", name=L + "ffn_up_dw_up"),
                                     _matmul(hf, dhu_g, "tn", name=L + "ffn_up_dw_gate")], axis=1)
    dx2, G['ffn_norm'] = _rms_bwd(x2, P['ffn_norm'], dhf, L + "ffn_norm_bwd", add=dx3)
    if midway is not None:
        P = dict(P, xa_wo=P['xa_wo'] + midway(dx2).astype(P['xa_wo'].dtype))
    datt = _matmul(dx2, P['xa_wo'], "nt", name=L + "xa_o_dx")
    G['xa_wo'] = _matmul(att, dx2, "tn", name=L + "xa_o_dw")
    dq, dkv = _rowwise_bwd(_xattn_fn, [_blk(q)], [kv], [datt], L + "xattn_bwd", out_dtypes=[BF16])
    dhx = _matmul(dq, P['xa_wq'], "nt", name=L + "xa_q_dx")
    G['xa_wq'] = _matmul(hx, dq, "tn", name=L + "xa_q_dw")
    dmemn = _matmul(dkv, P['xa_wkv'], "nt", name=L + "xa_kv_dx")
    G['xa_wkv'] = _matmul(memn, dkv, "tn", name=L + "xa_kv_dw")
    _, G['mem_norm'] = _rms_bwd(mem, P['mem_norm'], dmemn, L + "mem_norm_bwd")
    dx, G['xa_norm'] = _rms_bwd(x, P['xa_norm'], dhx, L + "xa_norm_bwd", add=dx2)
    return dx, G


U_COLS = (2048, 512)


def _even_fwd(x, P):
    S = x.shape[0]
    h0 = _rms_fwd(x, P['mix_norm'], "l0_mix_norm")
    proj = _matmul(h0, P['w_in'], name="l0_in")
    tabs = _ret_tables(S)
    o_raw, rstates = _ret_call(proj, tabs)
    o = _rowwise(_ret_post_fn, [_blk(o_raw), _blk(proj, 512, 3)], [P['ret_norm']], [512], "l0_ret_post",
                 out_dtypes=[BF16])[0]
    prep_in = (P['s5_lambda_re'], P['s5_lambda_im'], P['s5_log_dt'], P['s5_b_re'], P['s5_b_im'], P['s5_c_re'],
               P['s5_c_im'])
    a_re, a_im, bd_re, bd_im, cd_re, cd_im = _s5_prep(prep_in)
    a_re_f, a_im_f = a_re.reshape(1, 2048), a_im.reshape(1, 2048)
    bu_re = _matmul(proj, bd_re, name="l0_s5_bu_re", a_cols=U_COLS)
    bu_im = _matmul(proj, bd_im, name="l0_s5_bu_im", a_cols=U_COLS)
    st_re, st_im = _scan_fwd(bu_re, bu_im, a_re_f, a_im_f)
    y1 = _matmul(st_re, cd_re, name="l0_s5_y_re")
    y2 = _matmul(st_im, cd_im, name="l0_s5_y_im")
    yg = _rowwise(_s5_post_fn, [_blk(y1), _blk(y2), _blk(proj, 512, 4)],
                  [P['s5_d'], P['s5_w_glu'], P['s5_b_glu']], [512], "l0_s5_post", out_dtypes=[BF16])[0]
    x1 = _matmul_cat([o, yg], P['w_out'], "nn", res=x, name="l0_out")
    saved = (x, h0, proj, tabs, o_raw, rstates, prep_in, a_re_f, a_im_f, bd_re, bd_im, cd_re, cd_im, st_re, st_im,
             y1, y2, o, yg)
    return x1, saved


def _even_bwd(saved, dx1, P, midway=None):
    (x, h0, proj, tabs, o_raw, rstates, prep_in, a_re_f, a_im_f, bd_re, bd_im, cd_re, cd_im, st_re, st_im, y1, y2,
     o, yg) = saved
    G = {}
    dmerged = _matmul(dx1, P['w_out'], "nt", name="l0_out_dx")
    G['w_out'] = jnp.concatenate([_matmul(o, dx1, "tn", name="l0_out_dw_ret"),
                                  _matmul(yg, dx1, "tn", name="l0_out_dw_s5")], axis=0)
    do_raw, dgate, G['ret_norm'] = _rowwise_bwd(
        _ret_post_fn, [_blk(o_raw), _blk(proj, 512, 3)], [P['ret_norm']], [_blk(dmerged, 512, 0)], "l0_ret_post_bwd",
        out_dtypes=[F32, BF16])
    dq, dk, dv = _ret_call(proj, tabs, states=rstates, do=do_raw)
    if midway is not None:
        P = dict(P, s5_w_glu=P['s5_w_glu'] + midway(dq))
    dy1, dy2, du_a, G['s5_d'], G['s5_w_glu'], G['s5_b_glu'] = _rowwise_bwd(
        _s5_post_fn, [_blk(y1), _blk(y2), _blk(proj, 512, 4)], [P['s5_d'], P['s5_w_glu'], P['s5_b_glu']],
        [_blk(dmerged, 512, 1)], "l0_s5_post_bwd", out_dtypes=[BF16, BF16, F32])
    dst_re = _matmul(dy1, cd_re, "nt", name="l0_s5_y_re_dx")
    dcd_re = _matmul(st_re, dy1, "tn", name="l0_s5_y_re_dw")
    dst_im = _matmul(dy2, cd_im, "nt", name="l0_s5_y_im_dx")
    dcd_im = _matmul(st_im, dy2, "tn", name="l0_s5_y_im_dw")
    dbu_re, dbu_im, da_re, da_im = _scan_bwd(dst_re, dst_im, st_re, st_im, a_re_f, a_im_f)
    du = _matmul(dbu_re, bd_re, "nt", res=du_a, name="l0_s5_bu_re_dx")
    du = _matmul(dbu_im, bd_im, "nt", res=du, name="l0_s5_bu_im_dx", out_dtype=BF16)
    dbd_re = _matmul(proj, dbu_re, "tn", name="l0_s5_bu_re_dw", a_cols=U_COLS)
    dbd_im = _matmul(proj, dbu_im, "tn", name="l0_s5_bu_im_dw", a_cols=U_COLS)
    dprep = _s5_prep(prep_in, cots=(da_re.reshape(32, 64), da_im.reshape(32, 64), dbd_re, dbd_im, dcd_re, dcd_im))
    for n, d in zip(('s5_lambda_re', 's5_lambda_im', 's5_log_dt', 's5_b_re', 's5_b_im', 's5_c_re', 's5_c_im'), dprep):
        G[n] = d
    pieces = [dq, dk, dv, dgate, du]
    dh0 = _matmul_cat(pieces, P['w_in'], "nt", name="l0_in_dx")
    G['w_in'] = jnp.concatenate([_matmul(h0, p, "tn", name="l0_in_dw_%d" % n) for n, p in enumerate(pieces)], axis=1)
    dx, G['mix_norm'] = _rms_bwd(x, P['mix_norm'], dh0, "l0_mix_norm_bwd", add=dx1)
    return dx, G


def _odd_fwd(x, P):
    h1 = _rms_fwd(x, P['mix_norm'], "l1_mix_norm")
    pm = _matmul(h1, P['w_main'], name="l1_in_main")
    pt = _matmul(h1, P['w_tail'], name="l1_in_tail")
    qkv = _conv_post([(pm, 0, P['conv'], 0)], _silu, 3, 1024, "l1_conv")
    g_e, beta_e = _rowwise(_gdn_gates_fn, [_blk(pt)], [P['a_log_p'], P['dtb_p']], [1024, 1024], "l1_gdn_gates")
    w, u, qd, kd, qk, tinv = _gdn_intra_call(qkv, g_e, beta_e)
    o_raw, gstates = _gdn_pass(w, u, qd, kd, qk, g_e)
    og = _rowwise(_gdn_post_fn, [_blk(o_raw), _blk(pm, 1024, 3)], [P['o_norm']], [1024], "l1_gdn_post",
                  out_dtypes=[BF16])[0]
    x1 = _matmul(og, P['w_out'], res=x, name="l1_out")
    return x1, (x, h1, pm, pt, qkv, g_e, beta_e, w, u, qd, kd, qk, tinv, o_raw, gstates, og)


def _odd_bwd(saved, dx1, P):
    x, h1, pm, pt, qkv, g_e, beta_e, w, u, qd, kd, qk, tinv, o_raw, gstates, og = saved
    G = {}
    dog = _matmul(dx1, P['w_out'], "nt", name="l1_out_dx")
    G['w_out'] = _matmul(og, dx1, "tn", name="l1_out_dw")
    do_raw, dz, G['o_norm'] = _rowwise_bwd(_gdn_post_fn, [_blk(o_raw), _blk(pm, 1024, 3)], [P['o_norm']], [dog],
                                           "l1_gdn_post_bwd", out_dtypes=[F32, BF16])
    dw, du, dqd, dkd, dqk, dg_pass = _gdn_pass(w, u, qd, kd, qk, g_e, states=gstates, do=do_raw)
    dqkv = _gdn_intra_call(qkv, g_e, beta_e, cots=(dw, du, dqd, dkd, dqk, dg_pass, tinv))
    dg_e, dbeta_e = dqkv[3], dqkv[4]
    dpt, G['a_log_p'], G['dtb_p'] = _rowwise_bwd(_gdn_gates_fn, [_blk(pt)], [P['a_log_p'], P['dtb_p']],
                                                 [dg_e, dbeta_e], "l1_gdn_gates_bwd", out_dtypes=[BF16])
    pieces, dcw = [], []
    for part in range(3):
        dxp, dwp = _conv_post_bwd([(pm, part, P['conv'], part)], _silu, 1, 1024, dqkv[part],
                                  "l1_conv_bwd_%d" % part)
        pieces.append(dxp)
        dcw.append(dwp)
    G['conv'] = jnp.concatenate(dcw, axis=1)
    pieces += [dz, dpt]
    dh1 = _matmul_cat(pieces, P['w_all'], "nt", name="l1_in_dx")
    G['w_all'] = jnp.concatenate([_matmul(h1, p, "tn", name="l1_in_dw_%d" % n) for n, p in enumerate(pieces)], axis=1)
    dx, G['mix_norm'] = _rms_bwd(x, P['mix_norm'], dh1, "l1_mix_norm_bwd", add=dx1)
    return dx, G


def _row(v):
    return v.reshape(1, -1)


def _local_step(x, mem, target, W, later_weights=None, early_grads=None):
    P0 = {
        'mix_norm': _row(W['l0_mix_norm']), 'w_in': W['l0_w_in'], 'ret_norm': _row(W['l0_ret_norm']),
        's5_lambda_re': W['l0_s5_lambda_re'], 's5_lambda_im': W['l0_s5_lambda_im'],
        's5_log_dt': W['l0_s5_log_dt'].reshape(32, 1),
        's5_b_re': W['l0_s5_b_re'].reshape(512, 64), 's5_b_im': W['l0_s5_b_im'].reshape(512, 64),
        's5_c_re': W['l0_s5_c_re'].reshape(2048, 16), 's5_c_im': W['l0_s5_c_im'].reshape(2048, 16),
        's5_d': _row(W['l0_s5_d']), 's5_w_glu': W['l0_s5_w_glu'].astype(F32), 's5_b_glu': _row(W['l0_s5_b_glu']),
        'w_out': W['l0_w_out'],
    }
    def common(L):
        return {'xa_norm': _row(W[L + 'xa_norm']), 'mem_norm': _row(W[L + 'mem_norm']), 'xa_wq': W[L + 'xa_wq'],
                'xa_wkv': W[L + 'xa_wkv'], 'xa_wo': W[L + 'xa_wo'], 'ffn_norm': _row(W[L + 'ffn_norm']),
                'ffn_w_up': W[L + 'ffn_w_up'], 'ffn_conv': W[L + 'ffn_conv'], 'ffn_w_down': W[L + 'ffn_w_down']}

    x1, s_even = _even_fwd(x, P0)
    if later_weights is not None:
        W = dict(W, **later_weights('l0_common', x1))
    C0 = common('l0_')
    x3, s_c0 = _common_fwd(x1, mem, C0, "l0_")

    if later_weights is not None:
        W = dict(W, **later_weights('l1', x3))
    w_in1 = W['l1_w_in']
    pad8 = jnp.zeros((8,), F32)
    w_all = jnp.pad(w_in1, ((0, 0), (0, 112)))
    P1 = {
        'mix_norm': _row(W['l1_mix_norm']), 'w_main': w_in1[:, :4096], 'w_tail': w_all[:, 4096:], 'w_all': w_all,
        'conv': W['l1_conv'],
        'a_log_p': _row(jnp.concatenate([pad8, W['l1_a_log'], jnp.zeros((112,), F32)])),
        'dtb_p': _row(jnp.concatenate([pad8, W['l1_dt_bias'], jnp.zeros((112,), F32)])),
        'o_norm': _row(W['l1_o_norm']), 'w_out': W['l1_w_out'],
    }
    C1 = common('l1_')
    x4, s_odd = _odd_fwd(x3, P1)
    x6, s_c1 = _common_fwd(x4, mem, C1, "l1_")
    loss_tile, dx6, d_final = _loss_grad(x6, target, _row(W['final_norm']))

    G = {'final_norm': d_final.reshape(-1)}
    dx4, g = _common_bwd(s_c1, dx6, C1, "l1_")
    for k, v in g.items():
        G['l1_' + k] = v
    dx3, g = _odd_bwd(s_odd, dx4, P1)
    G['l1_mix_norm'] = g['mix_norm']
    G['l1_w_in'] = g['w_all'][:, :4112]
    G['l1_conv'] = g['conv']
    G['l1_a_log'] = g['a_log_p'][0, 8:16]
    G['l1_dt_bias'] = g['dtb_p'][0, 8:16]
    G['l1_o_norm'] = g['o_norm']
    G['l1_w_out'] = g['w_out']
    midway = None
    if early_grads is not None:
        zero, midway = early_grads('l1', G)
        C0 = dict(C0, ffn_w_down=C0['ffn_w_down'] + zero.astype(C0['ffn_w_down'].dtype))
    dx1, g = _common_bwd(s_c0, dx3, C0, "l0_", midway=midway)
    for k, v in g.items():
        G['l0_' + k] = v
    if early_grads is not None:
        zero, midway = early_grads('l0_common', G)
        P0 = dict(P0, w_out=P0['w_out'] + zero.astype(P0['w_out'].dtype))
    dx0, g = _even_bwd(s_even, dx1, P0, midway=midway)
    for k, v in g.items():
        G['l0_' + k] = v
    return loss_tile, dx0, G


ANY = pl.BlockSpec(memory_space=pl.ANY)


def _place():
    return lax.axis_index("x"), lax.axis_index("y"), lax.axis_index("c")


def _my_chip():
    return 2 * lax.axis_index("x") + lax.axis_index("y")


def _chip_peers(x, y):
    return [(1 - x, y), (x, 1 - y), (1 - x, 1 - y)]


def _half(ref, mode, shard, j, h, split):
    r, w = shard
    rh = r // 2 if split else r
    h = h if split else 0
    if mode == 'row':
        return ref.at[pl.ds(j * r + h * rh, rh), :]
    if mode == 'col':
        return ref.at[pl.ds(h * rh, rh), pl.ds(j * w, w)]
    return ref.at[j, pl.ds(h * rh, rh), :]


def _place_shard(shard, mode, name):
    r, w = shard.shape
    dtype = BF16 if mode != 'tap' else shard.dtype
    if mode == 'tap':
        mode = 'slab'
    tr = _row_tile(r, w)
    nb = r // tr

    def kern(s_ref, o_ref):
        o_ref[...] = s_ref[...].astype(o_ref.dtype)

    if mode == 'row':
        full, o_spec = (4 * r, w), pl.BlockSpec((tr, w), lambda i: (_my_chip() * nb + i, 0))
    elif mode == 'col':
        full, o_spec = (r, 4 * w), pl.BlockSpec((tr, w), lambda i: (i, _my_chip()))
    else:
        full, o_spec = (4, r, w), pl.BlockSpec((None, tr, w), lambda i: (_my_chip(), i, 0))
    return pl.pallas_call(kern, name=name, grid=(nb,), in_specs=[pl.BlockSpec((tr, w), lambda i: (i, 0))],
                          out_specs=o_spec, out_shape=jax.ShapeDtypeStruct(full, dtype),
                          compiler_params=_cparams(("parallel",)))(shard)


def _gather_placed(fulls, modes, shards, splits):
    n = len(fulls)

    def body(*refs):
        outs = refs[n:2 * n]
        send_sems, recv_sems = refs[2 * n:]
        x, y, c = _place()
        peers = _chip_peers(x, y)
        me = 2 * x + y

        def win(a, j, h):
            return _half(outs[a], modes[a], shards[a], j, h, splits[a])

        def copy(a, k, j, h, to):
            return pltpu.make_async_remote_copy(src_ref=win(a, j, h), dst_ref=win(a, j, h),
                                                send_sem=send_sems.at[6 * a + k], recv_sem=recv_sems.at[6 * a + k],
                                                device_id=to, device_id_type=MESH)

        over_ici = [copy(a, k, me, c, (p[0], p[1], c)) for a in range(n) for k, p in enumerate(peers)]
        for cp in over_ici:
            cp.start()
        passed = []
        for a in range(n):
            for k, p in enumerate(peers):
                j = 2 * p[0] + p[1]
                copy(a, k, j, c, (p[0], p[1], c)).wait_recv()
                if splits[a]:
                    fwd = copy(a, 3 + k, j, c, (x, y, 1 - c))
                    fwd.start()
                    passed.append(fwd)
        for a in range(n):
            if splits[a]:
                for k, p in enumerate(peers):
                    copy(a, 3 + k, 2 * p[0] + p[1], 1 - c, (x, y, 1 - c)).wait_recv()
        for cp in over_ici + passed:
            cp.wait_send()

    return pl.pallas_call(
        body, name="gather_weights", in_specs=[ANY] * n, out_specs=[ANY] * n,
        out_shape=[jax.ShapeDtypeStruct(f.shape, f.dtype) for f in fulls],
        input_output_aliases={a: a for a in range(n)},
        scratch_shapes=[pltpu.SemaphoreType.DMA((6 * n,)), pltpu.SemaphoreType.DMA((6 * n,))],
    )(*fulls)


_FLIPS = [(dx, dy, dc) for dx in (0, 1) for dy in (0, 1) for dc in (0, 1) if (dx, dy, dc) != (0, 0, 0)]


def _share_halves(bufs, name):
    n = len(bufs)

    def body(*refs):
        outs = refs[n:2 * n]
        send_sems, recv_sems = refs[2 * n:]
        x, y, c = _place()
        sends, waits = [], []
        for a in range(n):
            rh = bufs[a].shape[0] // 2
            mine = outs[a].at[pl.ds(c * rh, rh), :]
            other = outs[a].at[pl.ds((1 - c) * rh, rh), :]
            sends.append(pltpu.make_async_remote_copy(src_ref=mine, dst_ref=mine, send_sem=send_sems.at[a],
                                                      recv_sem=recv_sems.at[a], device_id=(x, y, 1 - c),
                                                      device_id_type=MESH))
            waits.append(pltpu.make_async_remote_copy(src_ref=mine, dst_ref=other, send_sem=send_sems.at[a],
                                                      recv_sem=recv_sems.at[a], device_id=(x, y, 1 - c),
                                                      device_id_type=MESH))
        for cp in sends:
            cp.start()
        for cp in waits:
            cp.wait()

    return pl.pallas_call(
        body, name=name, in_specs=[ANY] * n, out_specs=[ANY] * n,
        out_shape=[jax.ShapeDtypeStruct(b.shape, b.dtype) for b in bufs],
        input_output_aliases={a: a for a in range(n)},
        scratch_shapes=[pltpu.SemaphoreType.DMA((n,)), pltpu.SemaphoreType.DMA((n,))],
    )(*bufs)


TILE_BYTES = 2 * 1024 * 1024


def _row_tile(rows, width=1024):
    for t in (512, 352, 256, 176, 128, 64, 32, 16, 8):
        if rows % t == 0 and t * width * 4 <= TILE_BYTES:
            return t
    return rows


def _pair_sum(g, got, name):
    ns, r, w = g.shape
    rh = r // 2
    tr = _row_tile(rh, w)
    nb = rh // tr

    def kern(g_ref, o_ref, out_ref):
        out_ref[...] = (g_ref[...] + o_ref[...]).astype(BF16)

    return pl.pallas_call(
        kern, name=name, grid=(ns, nb),
        in_specs=[pl.BlockSpec((None, tr, w), lambda j, i: (j, lax.axis_index("c") * nb + i, 0)),
                  pl.BlockSpec((None, tr, w), lambda j, i: (j, i, 0))],
        out_specs=pl.BlockSpec((None, tr, w), lambda j, i: (j, i, 0)),
        out_shape=jax.ShapeDtypeStruct((ns, rh, w), BF16),
        compiler_params=_cparams(("parallel", "parallel")))(g, got)


def _chip_sum(pair, recv, w, name):
    rh = pair.shape[1]
    tr = _row_tile(rh, w)
    nb = rh // tr

    def kern(own_ref, r1_ref, r2_ref, r3_ref, out_ref):
        acc = own_ref[...].astype(F32)
        for r_ref in (r1_ref, r2_ref, r3_ref):
            acc = acc + r_ref[...].astype(F32)
        out_ref[...] = acc

    if pair.shape[0] == 4:
        own_spec = pl.BlockSpec((None, tr, w), lambda i: (_my_chip(), i, 0))
    else:
        own_spec = pl.BlockSpec((None, tr, w), lambda i: (0, i, _my_chip()))
    recv_specs = [pl.BlockSpec((None, tr, w), functools.partial(lambda i, d: ((_my_chip() + d) % 4, i, 0), d=d))
                  for d in (1, 2, 3)]
    return pl.pallas_call(
        kern, name=name, grid=(nb,), in_specs=[own_spec] + recv_specs,
        out_specs=pl.BlockSpec((tr, w), lambda i: (lax.axis_index("c") * nb + i, 0)),
        out_shape=jax.ShapeDtypeStruct((2 * rh, w), F32), compiler_params=_cparams(("parallel",)))(pair, recv, recv, recv)


def _slab_sum(slabs, name):
    n, R, w = slabs.shape
    tr = _row_tile(R)

    def kern(s_ref, o_ref):
        acc = s_ref[0].astype(F32)
        for k in range(1, n):
            acc = acc + s_ref[k].astype(F32)
        o_ref[...] = acc

    return pl.pallas_call(
        kern, name=name, grid=(R // tr,), in_specs=[pl.BlockSpec((n, tr, w), lambda i: (0, i, 0))],
        out_specs=pl.BlockSpec((tr, w), lambda i: (i, 0)), out_shape=jax.ShapeDtypeStruct((R, w), F32),
        compiler_params=_cparams(("parallel",)))(slabs)


def _adamw(w, g, m, v, name):
    R, C = w.shape
    tr = _pick(R, (256, 128, 64, 32, 16, 8))

    def kern(w_ref, g_ref, m_ref, v_ref, d_ref, nm_ref, nv_ref, g_out_ref):
        gv = g_ref[...]
        g_out_ref[...] = gv
        m2 = ADAM_B1 * m_ref[...] + (1.0 - ADAM_B1) * gv
        v2 = ADAM_B2 * v_ref[...] + (1.0 - ADAM_B2) * jnp.square(gv)
        m_hat = m2 / (1.0 - ADAM_B1 ** ADAM_STEP)
        v_hat = v2 / (1.0 - ADAM_B2 ** ADAM_STEP)
        d_ref[...] = -ADAM_LR * (m_hat / (jnp.sqrt(v_hat) + ADAM_EPS) + ADAM_WD * w_ref[...])
        nm_ref[...] = m2
        nv_ref[...] = v2

    spec = pl.BlockSpec((tr, C), lambda i: (i, 0))
    return pl.pallas_call(
        kern, name=name, grid=(R // tr,), in_specs=[spec] * 4, out_specs=[spec] * 4,
        out_shape=[jax.ShapeDtypeStruct((R, C), F32)] * 4, compiler_params=_cparams(("parallel",)))(w, g, m, v)


def _pack_small(vals):
    flat = jnp.concatenate([vals[n].astype(F32).reshape(-1) for n in SMALL_NAMES])
    rows = -(-flat.shape[0] // (8 * LANES)) * 8
    return jnp.pad(flat, (0, rows * LANES - flat.shape[0])).reshape(rows, LANES)


def _unpack_small(packed, shapes):
    flat = packed.reshape(-1)
    out = {}
    off = 0
    for n in SMALL_NAMES:
        size = int(np.prod(shapes[n]))
        out[n] = flat[off:off + size].reshape(shapes[n])
        off += size
    return out


HBM = pl.BlockSpec(memory_space=pltpu.HBM)
SEM = pl.BlockSpec(memory_space=pltpu.SEMAPHORE)
DATAFLOW = pltpu.SideEffectType.DATAFLOW_SIDE_EFFECTING


def _in_hbm(a):
    return pltpu.with_memory_space_constraint(a, pltpu.HBM)


def _split_copy_start(srcs, lands, copies, after, name):
    ns, nl = len(srcs), len(lands)
    ncopy = len(copies(list(srcs), list(lands), None, None, probe=True))

    def body(*refs):
        src_refs, land_refs = refs[:ns], refs[ns:ns + nl]
        send_sems, recv_sems = refs[ns + nl + 1:ns + nl + 3]
        token = refs[-1]
        for cp in copies(src_refs, land_refs, send_sems, recv_sems):
            cp.start()
        token[...] = jnp.zeros_like(token)

    outs = pl.pallas_call(
        body, name=name,
        out_shape=(pltpu.SemaphoreType.DMA((ncopy,)), pltpu.SemaphoreType.DMA((ncopy,)),
                   *[pltpu.HBM(a.shape, a.dtype) for a in srcs], *[pltpu.HBM(a.shape, a.dtype) for a in lands],
                   jax.ShapeDtypeStruct((8, 128), F32)),
        in_specs=[HBM] * (ns + nl) + [ANY],
        out_specs=(SEM, SEM, *[HBM] * (ns + nl), pl.BlockSpec(memory_space=pltpu.VMEM)),
        input_output_aliases={i: 2 + i for i in range(ns + nl)},
        compiler_params=pltpu.CompilerParams(has_side_effects=DATAFLOW),
    )(*[_in_hbm(a) for a in srcs], *[_in_hbm(a) for a in lands], after)
    return outs[0], outs[1], outs[2:2 + ns], outs[2 + ns:2 + ns + nl], outs[-1]


def _split_copy_wait(send_sems, recv_sems, srcs, lands, copies, after, name):
    ns, nl = len(srcs), len(lands)

    def body(*refs):
        src_refs, land_refs = refs[:ns], refs[ns:ns + nl]
        send_ref, recv_ref = refs[ns + nl:ns + nl + 2]
        for cp in copies(src_refs, land_refs, send_ref, recv_ref):
            cp.wait_send()
            cp.wait_recv()

    outs = pl.pallas_call(
        body, name=name,
        out_shape=tuple(pltpu.HBM(a.shape, a.dtype) for a in list(srcs) + list(lands)),
        in_specs=[HBM] * (ns + nl) + [SEM, SEM, ANY], out_specs=tuple([HBM] * (ns + nl)),
        input_output_aliases={i: i for i in range(ns + nl)},
        compiler_params=pltpu.CompilerParams(has_side_effects=DATAFLOW),
    )(*srcs, *lands, send_sems, recv_sems, after)
    return outs[:ns], outs[ns:]


def _matrix_mode(n):
    return 'slab' if n == 'l1_w_in' else ('row' if MATRICES[n] == 0 else 'col')


def _placed(A, names):
    modes = ['slab' if n in CONVS else _matrix_mode(n) for n in names]
    fulls = [_place_shard(A[n], 'tap' if n in CONVS else m, "place_" + n) for n, m in zip(names, modes)]
    return fulls, modes


def _assembled(names, modes, outs):
    return {n: jnp.concatenate([o[j] for j in range(4)], axis=1) if m == 'slab' else o
            for n, m, o in zip(names, modes, outs)}


def _gather_weights(A, names):
    fulls, modes = _placed(A, names)
    outs = _gather_placed(fulls, modes, [A[n].shape for n in names], [n not in CONVS for n in names])
    return _assembled(names, modes, outs)


def _whole_shard_copies(modes, shards):
    def copies(src_refs, land_refs, send_sems, recv_sems, probe=False):
        if probe:
            return [None] * (3 * len(land_refs))
        x, y, c = _place()
        me = 2 * x + y
        out = []
        for a, ref in enumerate(land_refs):
            for k, p in enumerate(_chip_peers(x, y)):
                out.append(pltpu.make_async_remote_copy(
                    src_ref=_half(ref, modes[a], shards[a], me, 0, False),
                    dst_ref=_half(ref, modes[a], shards[a], me, 0, False),
                    send_sem=send_sems.at[3 * a + k], recv_sem=recv_sems.at[3 * a + k],
                    device_id=(p[0], p[1], c), device_id_type=MESH))
        return out
    return copies


def _gather_weights_start(A, names, after, tag):
    fulls, modes = _placed(A, names)
    copies = _whole_shard_copies(modes, [A[n].shape for n in names])
    send_sems, recv_sems, _, lands, zeros = _split_copy_start([], fulls, copies, after, "gather_start_" + tag)
    return (send_sems, recv_sems, lands, copies, names, modes), zeros


def _gather_weights_wait(state, after, tag):
    send_sems, recv_sems, lands, copies, names, modes = state
    _, outs = _split_copy_wait(send_sems, recv_sems, [], lands, copies, after, "gather_wait_" + tag)
    return _assembled(names, modes, outs)


def _to_chips_copies(pair_shapes, widths):
    def copies(src_refs, land_refs, send_sems, recv_sems, probe=False):
        if probe:
            return [None] * (3 * len(land_refs))
        x, y, c = _place()
        me = 2 * x + y
        out = []
        for a, (src, land) in enumerate(zip(src_refs, land_refs)):
            for k, p in enumerate(_chip_peers(x, y)):
                j = 2 * p[0] + p[1]
                part = src.at[j] if pair_shapes[a][0] == 4 else src.at[0, :, pl.ds(j * widths[a], widths[a])]
                out.append(pltpu.make_async_remote_copy(
                    src_ref=part, dst_ref=land.at[me], send_sem=send_sems.at[3 * a + k],
                    recv_sem=recv_sems.at[3 * a + k], device_id=(p[0], p[1], c), device_id_type=MESH))
        return out
    return copies


def _slabbed(G, names):
    gs, widths = [], []
    for n in names:
        g = G[n]
        mode = _matrix_mode(n)
        if mode == 'row':
            gs.append(g.reshape(4, g.shape[0] // 4, g.shape[1]))
            widths.append(g.shape[1])
        elif mode == 'col':
            gs.append(g[None])
            widths.append(g.shape[1] // 4)
        else:
            wd = g.shape[1] // 4
            gs.append(jnp.stack([g[:, j * wd:(j + 1) * wd] for j in range(4)]))
            widths.append(wd)
    return gs, widths


def _other_half_copies(shapes):
    def copies(src_refs, land_refs, send_sems, recv_sems, probe=False):
        if probe:
            return [None] * len(land_refs)
        x, y, c = _place()
        out = []
        for a, (src, land) in enumerate(zip(src_refs, land_refs)):
            rh = shapes[a][1] // 2
            out.append(pltpu.make_async_remote_copy(
                src_ref=src.at[:, pl.ds((1 - c) * rh, rh), :], dst_ref=land, send_sem=send_sems.at[a],
                recv_sem=recv_sems.at[a], device_id=(x, y, 1 - c), device_id_type=MESH))
        return out
    return copies


def _reduce_end(names, pairs, recv, widths, tag):
    halves = [_chip_sum(p, r, w, "chip_sum_" + n) for n, p, r, w in zip(names, pairs, recv, widths)]
    return dict(zip(names, _share_halves(halves, "share_halves_" + tag)))


def _small_copies():
    def copies(src_refs, land_refs, send_sems, recv_sems, probe=False):
        if probe:
            return [None] * len(_FLIPS)
        x, y, c = _place()
        mine = land_refs[0].at[4 * x + 2 * y + c]
        return [pltpu.make_async_remote_copy(src_ref=mine, dst_ref=mine, send_sem=send_sems.at[k],
                                             recv_sem=recv_sems.at[k], device_id=(x ^ f[0], y ^ f[1], c ^ f[2]),
                                             device_id_type=MESH) for k, f in enumerate(_FLIPS)]
    return copies


def _small_slab(packed):
    me8 = 4 * lax.axis_index("x") + 2 * lax.axis_index("y") + lax.axis_index("c")
    return lax.dynamic_update_slice(jnp.zeros((8,) + packed.shape, F32), packed[None], (me8, 0, 0))


def kernel(*args):
    A = dict(zip(ARG_NAMES, args, strict=True))
    x, mem, target = A['x'][0], A['mem'][0], A['loss_target'][0]

    stages = {'l0_mixer': ['l0_w_in', 'l0_s5_w_glu', 'l0_w_out'],
              'l0_common': [n for n in MATRIX_NAMES if n.startswith(('l0_xa_', 'l0_ffn_'))],
              'l1': [n for n in MATRIX_NAMES if n.startswith('l1_')]}
    W = _gather_weights(A, stages['l0_mixer'] + list(CONVS))
    for n in SMALL_NAMES:
        if n not in CONVS:
            W[n] = A[n]
    flights = {}
    after = W['l0_w_in']
    for stage in ('l0_common', 'l1'):
        flights[stage], after = _gather_weights_start(A, stages[stage], after, stage)
    W['l0_mix_norm'] = W['l0_mix_norm'] + after[0, 0]

    reduce_state, start_tokens = {}, {}

    def early_grads(stage, G, after=None):
        names = stages[stage]
        gs, widths = _slabbed(G, names)
        d2d = _other_half_copies([g.shape for g in gs])
        lands = [lax.empty((g.shape[0], g.shape[1] // 2, g.shape[2]), g.dtype) for g in gs]
        d2d_send, d2d_recv, gs, lands, zeros = _split_copy_start(gs, lands, d2d, G['final_norm'] if after is None else after,
                                                                 "d2d_start_" + stage)
        start_tokens[stage] = zeros

        def midway(after):
            mine, got = _split_copy_wait(d2d_send, d2d_recv, gs, lands, d2d, after, "d2d_wait_" + stage)
            pairs = [_pair_sum(g, o, "pair_sum_" + n) for n, g, o in zip(names, mine, got)]
            copies = _to_chips_copies([p.shape for p in pairs], widths)
            recv = [lax.empty((4, p.shape[1], w), p.dtype) for p, w in zip(pairs, widths)]
            send_sems, recv_sems, pairs, recv, zeros2 = _split_copy_start(pairs, recv, copies, G['final_norm'],
                                                                          "reduce_start_" + stage)
            reduce_state[stage] = (send_sems, recv_sems, pairs, recv, copies, widths)
            start_tokens[stage] = zeros2
            return zeros2[0, 0]

        return zeros[0, 0], midway

    loss_tile, grad_x, G = _local_step(
        x, mem, target, W, later_weights=lambda stage, after: _gather_weights_wait(flights[stage], after, stage),
        early_grads=early_grads)
    loss = lax.psum(loss_tile[0, 0], ("x", "y", "c"))

    small_copies = _small_copies()
    small_send, small_recv, _, small_land, started = _split_copy_start(
        [], [_small_slab(_pack_small({n: G[n] for n in SMALL_NAMES}))], small_copies, grad_x, "small_start")
    _, mixer_midway = early_grads('l0_mixer', G, after=started)
    grads, deltas, new_m, new_v = {}, {}, {}, {}

    def finish(stage, after):
        send_sems, recv_sems, pairs, lands, copies, widths = reduce_state[stage]
        sent, recv = _split_copy_wait(send_sems, recv_sems, pairs, lands, copies, after, "reduce_wait_" + stage)
        g_stage = _reduce_end(stages[stage], sent, recv, widths, stage)
        for n in stages[stage]:
            deltas[n], new_m[n], new_v[n], grads[n] = _adamw(A[n], g_stage[n], A['m_' + n], A['v_' + n], "adamw_" + n)
        return deltas[stages[stage][-1]]

    after = finish('l1', start_tokens['l0_mixer'])
    mixer_midway(after)
    finish('l0_common', start_tokens['l0_mixer'])
    after = finish('l0_mixer', deltas[stages['l0_common'][-1]])
    _, (g_small,) = _split_copy_wait(small_send, small_recv, [], small_land, small_copies, after, "small_wait")
    g_small = _unpack_small(_slab_sum(g_small, "sum_small"), {n: G[n].shape for n in SMALL_NAMES})
    me = 2 * lax.axis_index("x") + lax.axis_index("y")
    for n in CONVS:
        wd = A[n].shape[1]
        g_small[n] = lax.dynamic_slice_in_dim(g_small[n], me * wd, wd, axis=1)
    flat_names = [n for n in SMALL_NAMES if n not in CONVS]

    def pack_flat(prefix):
        return _pack_small_flat({n: A[prefix + n] for n in flat_names}, flat_names)

    shapes = {n: A[n].shape for n in flat_names}
    d_s, m_s, v_s, _ = _adamw(pack_flat(''), _pack_small_flat(g_small, flat_names), pack_flat('m_'), pack_flat('v_'),
                              "adamw_small")
    d_s, m_s, v_s = (_unpack_flat(p, shapes, flat_names) for p in (d_s, m_s, v_s))

    for n in WEIGHTS:
        if n in MATRICES:
            continue
        if n in CONVS:
            deltas[n], new_m[n], new_v[n], grads[n] = _adamw(A[n], g_small[n], A['m_' + n], A['v_' + n],
                                                             "adamw_" + n)
        else:
            grads[n] = g_small[n].reshape(A[n].shape)
            deltas[n], new_m[n], new_v[n] = d_s[n], m_s[n], v_s[n]
    return (loss, grad_x[None], *[grads[n] for n in WEIGHTS], *[deltas[n] for n in WEIGHTS],
            *[new_m[n] for n in WEIGHTS], *[new_v[n] for n in WEIGHTS])


def _pack_small_flat(vals, names):
    flat = jnp.concatenate([vals[n].astype(F32).reshape(-1) for n in names])
    rows = -(-flat.shape[0] // (8 * LANES)) * 8
    return jnp.pad(flat, (0, rows * LANES - flat.shape[0])).reshape(rows, LANES)


def _unpack_flat(packed, shapes, names):
    flat = packed.reshape(-1)
    out = {}
    off = 0
    for n in names:
        size = int(np.prod(shapes[n]))
        out[n] = flat[off:off + size].reshape(shapes[n])
        off += size
    return out
```

```python
import functools
import math

import numpy as np
import jax
import jax.numpy as jnp
from jax import lax
from jax.experimental import pallas as pl
from jax.experimental.pallas import tpu as pltpu

F32 = jnp.float32
BF16 = jnp.bfloat16
EPS = 1e-6
MESH = pl.DeviceIdType.MESH

ADAM_LR = 0.001
ADAM_B1 = 0.9
ADAM_B2 = 0.999
ADAM_EPS = 1e-08
ADAM_WD = 0.01
ADAM_STEP = 10

VMEM_LIMIT_BYTES = 56 * 1024 * 1024
MATMUL_VMEM_BYTES = 44 * 1024 * 1024
LANES = 1024

WEIGHTS = ['l0_mix_norm', 'l0_w_in', 'l0_ret_norm', 'l0_s5_lambda_re', 'l0_s5_lambda_im', 'l0_s5_b_re', 'l0_s5_b_im',
           'l0_s5_c_re', 'l0_s5_c_im', 'l0_s5_d', 'l0_s5_log_dt', 'l0_s5_w_glu', 'l0_s5_b_glu', 'l0_w_out',
           'l0_xa_norm', 'l0_mem_norm', 'l0_xa_wq', 'l0_xa_wkv', 'l0_xa_wo', 'l0_ffn_norm', 'l0_ffn_w_up',
           'l0_ffn_conv', 'l0_ffn_w_down', 'l1_mix_norm', 'l1_w_in', 'l1_conv', 'l1_a_log', 'l1_dt_bias',
           'l1_o_norm', 'l1_w_out', 'l1_xa_norm', 'l1_mem_norm', 'l1_xa_wq', 'l1_xa_wkv', 'l1_xa_wo',
           'l1_ffn_norm', 'l1_ffn_w_up', 'l1_ffn_conv', 'l1_ffn_w_down', 'final_norm']
ARG_NAMES = (['x', 'mem'] + WEIGHTS + ['loss_target'] + ['m_' + w for w in WEIGHTS] + ['v_' + w for w in WEIGHTS])

MATRICES = {
    'l0_w_in': 1, 'l0_s5_w_glu': 0, 'l0_w_out': 0, 'l0_xa_wq': 0, 'l0_xa_wkv': 1, 'l0_xa_wo': 0, 'l0_ffn_w_up': 1,
    'l0_ffn_w_down': 0, 'l1_w_in': 1, 'l1_w_out': 0, 'l1_xa_wq': 0, 'l1_xa_wkv': 1, 'l1_xa_wo': 0,
    'l1_ffn_w_up': 1, 'l1_ffn_w_down': 0,
}
CONVS = ('l0_ffn_conv', 'l1_conv', 'l1_ffn_conv')
MATRIX_NAMES = [w for w in WEIGHTS if w in MATRICES]
SMALL_NAMES = [w for w in WEIGHTS if w not in MATRICES]


def _cparams(sem=None):
    return pltpu.CompilerParams(dimension_semantics=sem, vmem_limit_bytes=VMEM_LIMIT_BYTES)


def _pick(n, cands):
    for c in cands:
        if n % c == 0:
            return c
    return n


_NN = ((1,), (0,))
_NT = ((1,), (1,))
_TN = ((0,), (0,))


def _dot(a, b, dims, hi):
    if hi is not None:
        return lax.dot_general(a.astype(F32), b.astype(F32), (dims, ((), ())), precision=hi,
                               preferred_element_type=F32)
    return lax.dot_general(a.astype(BF16), b.astype(BF16), (dims, ((), ())), preferred_element_type=F32)


def _make_mm(hi):
    @jax.custom_vjp
    def nn(a, b):
        return _dot(a, b, _NN, hi)

    def nn_f(a, b):
        return nn(a, b), (a, b)

    def nn_b(r, g):
        a, b = r
        return _dot(g, b, _NT, hi), _dot(a, g, _TN, hi)

    nn.defvjp(nn_f, nn_b)

    @jax.custom_vjp
    def nt(a, b):
        return _dot(a, b, _NT, hi)

    def nt_f(a, b):
        return nt(a, b), (a, b)

    def nt_b(r, g):
        a, b = r
        return _dot(g, b, _NN, hi), _dot(g, a, _TN, hi)

    nt.defvjp(nt_f, nt_b)

    @jax.custom_vjp
    def tn(a, b):
        return _dot(a, b, _TN, hi)

    def tn_f(a, b):
        return tn(a, b), (a, b)

    def tn_b(r, g):
        a, b = r
        return _dot(b, g, _NT, hi), _dot(a, g, _NN, hi)

    tn.defvjp(tn_f, tn_b)
    return nn, nt, tn


mm, mm_nt, mm_tn = _make_mm(None)
mmh, mmh_nt, mmh_tn = _make_mm(lax.Precision.HIGHEST)
mm3, mm3_nt, mm3_tn = _make_mm(lax.Precision.HIGH)


@jax.custom_vjp
def _swap_halves(x):
    return pltpu.roll(x, 64, 1)


def _swap_f(x):
    return pltpu.roll(x, 64, 1), None


def _swap_b(_, g):
    return (pltpu.roll(g, 64, 1),)


_swap_halves.defvjp(_swap_f, _swap_b)


def _silu(x):
    return x * jax.nn.sigmoid(x)


def _rms(x, g):
    return x * lax.rsqrt(jnp.mean(x * x, axis=-1, keepdims=True) + EPS) * g


def _iota(shape, dim):
    return lax.broadcasted_iota(jnp.int32, shape, dim)


def _matmul_tiles(M, N, K, a_bytes, b_bytes, has_res, a_off):
    def divisors(n, cands):
        return [c for c in cands if n % c == 0] or [n]

    fallback = None
    for tk in divisors(K, (K, 2048, 1408, 1024, 512, 256, 128)):
        for tm in divisors(M, (1024, 512, 1408, 256, 128)):
            for tn in divisors(N, (1408, 1024, 512, 256, 128)):
                need = 2 * (tm * tk * a_bytes + tk * tn * b_bytes + (tm * tn * 4 if has_res else 0)) + 3 * tm * tn * 4
                if need > MATMUL_VMEM_BYTES or a_off % tk or a_off % tm:
                    continue
                if tm >= 256 and tn >= 256:
                    return tm, tn, tk
                fallback = fallback or (tm, tn, tk)
    return fallback


def _matmul(a, b, mode="nn", res=None, name="mm", a_cols=None, out_dtype=F32):
    a_off, a_w = (0, a.shape[1]) if a_cols is None else a_cols
    if mode == "nn":
        (M, K), (K2, N) = (a.shape[0], a_w), b.shape
    elif mode == "nt":
        (M, K), (N, K2) = (a.shape[0], a_w), b.shape
    else:
        (K, M), (K2, N) = (a.shape[0], a_w), b.shape
    assert K == K2, (a.shape, b.shape, mode)
    tm, tn, tk = _matmul_tiles(M, N, K, 2 if a.dtype == BF16 else 4, 2 if b.dtype == BF16 else 4, res is not None,
                               a_off)
    nk = K // tk
    dims = {"nn": _NN, "nt": _NT, "tn": _TN}[mode]
    ao = a_off // (tm if mode == "tn" else tk)
    assert ao * (tm if mode == "tn" else tk) == a_off
    if mode == "nn":
        a_spec = pl.BlockSpec((tm, tk), lambda i, j, k: (i, k + ao))
        b_spec = pl.BlockSpec((tk, tn), lambda i, j, k: (k, j))
    elif mode == "nt":
        a_spec = pl.BlockSpec((tm, tk), lambda i, j, k: (i, k + ao))
        b_spec = pl.BlockSpec((tn, tk), lambda i, j, k: (j, k))
    else:
        a_spec = pl.BlockSpec((tk, tm), lambda i, j, k: (k, i + ao))
        b_spec = pl.BlockSpec((tk, tn), lambda i, j, k: (k, j))
    o_spec = pl.BlockSpec((tm, tn), lambda i, j, k: (i, j))
    has_res = res is not None

    def kern(*refs):
        a_ref, b_ref = refs[:2]
        r_ref = refs[2] if has_res else None
        o_ref = refs[3] if has_res else refs[2]
        acc_ref = refs[-1] if nk > 1 else None
        k = pl.program_id(2)
        part = lax.dot_general(a_ref[...].astype(BF16), b_ref[...].astype(BF16), (dims, ((), ())),
                               preferred_element_type=F32)
        if nk == 1:
            o_ref[...] = (part + r_ref[...] if has_res else part).astype(o_ref.dtype)
            return

        @pl.when(k == 0)
        def _():
            acc_ref[...] = part

        @pl.when((k > 0) & (k < nk - 1))
        def _():
            acc_ref[...] += part

        @pl.when(k == nk - 1)
        def _():
            total = acc_ref[...] + part
            o_ref[...] = (total + r_ref[...] if has_res else total).astype(o_ref.dtype)

    in_specs = [a_spec, b_spec] + ([o_spec] if has_res else [])
    ops = (a, b) + ((res,) if has_res else ())
    return pl.pallas_call(
        kern, name=name, grid=(M // tm, N // tn, nk), in_specs=in_specs, out_specs=o_spec,
        out_shape=jax.ShapeDtypeStruct((M, N), out_dtype),
        scratch_shapes=[pltpu.VMEM((tm, tn), F32)] if nk > 1 else [],
        compiler_params=_cparams(("parallel", "parallel", "arbitrary")))(*ops)


def _matmul_cat(pieces, b, mode="nn", res=None, name="mmcat"):
    M = pieces[0].shape[0]
    widths = [p.shape[1] for p in pieces]
    K = sum(widths)
    N = b.shape[1] if mode == "nn" else b.shape[0]
    assert (b.shape[0] if mode == "nn" else b.shape[1]) == K
    tn = _pick(N, (1024, 512, 256, 128))
    a_bytes = 2 if pieces[0].dtype == BF16 else 4
    for tm in (1024, 512, 256, 128):
        need = 2 * (tm * K * a_bytes + K * tn * 2 + (tm * tn * 4 if res is not None else 0)) + 3 * tm * tn * 4
        if M % tm == 0 and need <= MATMUL_VMEM_BYTES:
            break
    npc = len(pieces)
    has_res = res is not None
    dims = _NN if mode == "nn" else _NT

    def kern(*refs):
        b_ref = refs[npc]
        o_ref = refs[-1]
        acc = refs[npc + 1][...] if has_res else None
        off = 0
        for p in range(npc):
            bp = b_ref[off:off + widths[p], :] if mode == "nn" else b_ref[:, off:off + widths[p]]
            t = lax.dot_general(refs[p][...].astype(BF16), bp.astype(BF16), (dims, ((), ())),
                                preferred_element_type=F32)
            acc = t if acc is None else acc + t
            off += widths[p]
        o_ref[...] = acc

    in_specs = [pl.BlockSpec((tm, w), lambda j, i: (i, 0)) for w in widths]
    in_specs.append(pl.BlockSpec((K, tn), lambda j, i: (0, j)) if mode == "nn"
                    else pl.BlockSpec((tn, K), lambda j, i: (j, 0)))
    o_spec = pl.BlockSpec((tm, tn), lambda j, i: (i, j))
    if has_res:
        in_specs.append(o_spec)
    ops = list(pieces) + [b] + ([res] if has_res else [])
    return pl.pallas_call(
        kern, name=name, grid=(N // tn, M // tm), in_specs=in_specs, out_specs=o_spec,
        out_shape=jax.ShapeDtypeStruct((M, N), F32), compiler_params=_cparams(("parallel", "parallel")))(*ops)


def _blk(a, width=None, colblk=0):
    return (a, a.shape[1] if width is None else width, colblk)


def _row_specs(blocked, params, ts):
    specs = []
    for (_, w, cb) in blocked:
        specs.append(pl.BlockSpec((ts, w), functools.partial(lambda i, cb: (i, cb), cb=cb)))
    for p in params:
        specs.append(pl.BlockSpec(p.shape, lambda i: (0, 0)))
    return specs


def _rowwise(fn, blocked, params, out_widths, name, ts=256, out_dtypes=None):
    S = blocked[0][0].shape[0]
    ts = min(ts, S)
    nb, npar = len(blocked), len(params)
    out_dtypes = [F32] * len(out_widths) if out_dtypes is None else out_dtypes

    def kern(*refs):
        vals = [r[...] for r in refs[:nb + npar]]
        outs = fn(*vals)
        for o_ref, o in zip(refs[nb + npar:], outs):
            o_ref[...] = o.astype(o_ref.dtype)

    return pl.pallas_call(
        kern, name=name, grid=(S // ts,), in_specs=_row_specs(blocked, params, ts),
        out_specs=[pl.BlockSpec((ts, w), lambda i: (i, 0)) for w in out_widths],
        out_shape=[jax.ShapeDtypeStruct((S, w), d) for w, d in zip(out_widths, out_dtypes)],
        compiler_params=_cparams(("parallel",)))(*[b[0] for b in blocked], *params)


def _rowwise_bwd(fn, blocked, params, cots, name, blocked_grad=None, param_grad=None, adds=None, ts=256,
                 out_dtypes=None):
    S = blocked[0][0].shape[0]
    ts = min(ts, S)
    cots = [c if isinstance(c, tuple) else _blk(c) for c in cots]
    nb, npar, nc = len(blocked), len(params), len(cots)
    blocked_grad = [True] * nb if blocked_grad is None else blocked_grad
    param_grad = [True] * npar if param_grad is None else param_grad
    adds = {} if adds is None else adds
    bidx = [i for i in range(nb) if blocked_grad[i]]
    pidx = [i for i in range(npar) if param_grad[i]]
    add_keys = sorted(adds)
    n_in = nb + npar + nc + len(add_keys)

    def kern(*refs):
        i = pl.program_id(0)
        xs = [r[...] for r in refs[:nb]]
        ps = [r[...] for r in refs[nb:nb + npar]]
        gs = [r[...] for r in refs[nb + npar:nb + npar + nc]]
        add_vals = {k: refs[nb + npar + nc + n][...] for n, k in enumerate(add_keys)}
        outs = refs[n_in:]

        def f(*diff):
            full_x = list(xs)
            full_p = list(ps)
            for n, ix in enumerate(bidx):
                full_x[ix] = diff[n]
            for n, ix in enumerate(pidx):
                full_p[ix] = diff[len(bidx) + n]
            return tuple(fn(*full_x, *full_p))

        _, vjp = jax.vjp(f, *[xs[ix] for ix in bidx], *[ps[ix] for ix in pidx])
        grads = vjp(tuple(gs))
        for n, ix in enumerate(bidx):
            g = grads[n]
            if ix in add_vals:
                g = g + add_vals[ix]
            outs[n][...] = g.astype(outs[n].dtype)
        for n in range(len(pidx)):
            o_ref = outs[len(bidx) + n]

            @pl.when(i == 0)
            def _(o_ref=o_ref):
                o_ref[...] = jnp.zeros_like(o_ref)

            o_ref[...] += grads[len(bidx) + n]

    in_specs = _row_specs(blocked, params, ts)
    in_specs += _row_specs(cots, [], ts)
    in_specs += [pl.BlockSpec((ts, adds[k].shape[1]), lambda i: (i, 0)) for k in add_keys]
    out_specs = [pl.BlockSpec((ts, blocked[ix][1]), lambda i: (i, 0)) for ix in bidx]
    out_specs += [pl.BlockSpec(params[ix].shape, lambda i: (0, 0)) for ix in pidx]
    out_dtypes = [F32] * len(bidx) if out_dtypes is None else out_dtypes
    out_shape = [jax.ShapeDtypeStruct((S, blocked[ix][1]), d) for ix, d in zip(bidx, out_dtypes)]
    out_shape += [jax.ShapeDtypeStruct(params[ix].shape, F32) for ix in pidx]
    return pl.pallas_call(
        kern, name=name, grid=(S // ts,), in_specs=in_specs, out_specs=out_specs, out_shape=out_shape,
        compiler_params=_cparams(("arbitrary",)))(*[b[0] for b in blocked], *params, *[c[0] for c in cots],
                                                    *[adds[k] for k in add_keys])


def _rms_fn(x, g):
    return (_rms(x, g),)


def _head_norm(o, n_heads, dh):
    outs = []
    for h in range(n_heads):
        oh = o[:, h * dh:(h + 1) * dh]
        outs.append(oh * lax.rsqrt(jnp.mean(oh * oh, axis=-1, keepdims=True) + EPS))
    return outs


def _ret_post_fn(o_raw, gate, ret_norm):
    o = jnp.concatenate(_head_norm(o_raw, 4, 128), axis=1)
    return (o * ret_norm * _silu(gate),)


def _s5_post_fn(y1, y2, u, d, w_glu, b_glu):
    y = y1 - y2 + d * u
    y = jax.nn.gelu(y)
    return (y * jax.nn.sigmoid(mm(y, w_glu) + b_glu),)


def _xattn_fn(q, kv):
    outs = []
    for h in range(4):
        qh = q[:, h * 256:(h + 1) * 256]
        kh = kv[:, h * 256:(h + 1) * 256]
        vh = kv[:, 1024 + h * 256:1024 + (h + 1) * 256]
        s = mm_nt(qh, kh) * (256 ** -0.5)
        s = s - lax.stop_gradient(jnp.max(s, axis=-1, keepdims=True))
        p = jnp.exp(s)
        p = p / jnp.sum(p, axis=-1, keepdims=True)
        outs.append(mm(p, vh))
    return (jnp.concatenate(outs, axis=1),)


def _softplus(x):
    return jnp.maximum(x, 0.0) + jnp.log1p(jnp.exp(-jnp.abs(x)))


def _gdn_gates_fn(pt, a_log_p, dtb_p):
    rows, cols = _iota((128, 1024), 0), _iota((128, 1024), 1)
    e_b = (rows == (cols >> 7)).astype(F32)
    e_a = (rows == (cols >> 7) + 8).astype(F32)
    beta = jax.nn.sigmoid(pt)
    g = -(jnp.exp(a_log_p) * _softplus(pt + dtb_p))
    return mmh(g, e_a), mmh(beta, e_b)


def _gdn_post_fn(o_raw, z, o_norm):
    outs = _head_norm(o_raw, 8, 128)
    o = jnp.concatenate([oh * o_norm for oh in outs], axis=1)
    return (o * _silu(z),)


def _ffn_post(up, gate):
    return _silu(gate) * up


def _shift_down(cur, prev8, sh, row8):
    if sh == 0:
        return cur
    r = pltpu.roll(cur, sh, 0)
    p = pltpu.roll(prev8, sh, 0)
    top = jnp.where(row8 < sh, p, r[0:8])
    if cur.shape[0] == 8:
        return top
    return jnp.concatenate([top, r[8:]], axis=0)


def _shift_up(cur, next8, sh, row8):
    if sh == 0:
        return cur
    ts = cur.shape[0]
    r = pltpu.roll(cur, ts - sh, 0)
    p = pltpu.roll(next8, 8 - sh, 0)
    bot = jnp.where(row8 >= 8 - sh, p, r[ts - 8:])
    return jnp.concatenate([r[:ts - 8], bot], axis=0)


def _conv_rows(cur, prev8, wrows, row8):
    k_w = len(wrows)
    out = None
    for j in range(k_w):
        t = _shift_down(cur, prev8, k_w - 1 - j, row8) * wrows[j]
        out = t if out is None else out + t
    return out


def _conv_specs(x, xoff, w, woff, ts, tc):
    r8 = ts // 8
    return [pl.BlockSpec((ts, tc), functools.partial(lambda i, j, o: (i, j + o), o=xoff)),
            pl.BlockSpec((8, tc), functools.partial(lambda i, j, o: (jnp.maximum(i * r8 - 1, 0), j + o), o=xoff)),
            pl.BlockSpec((w.shape[0], tc), functools.partial(lambda i, j, o: (0, j + o), o=woff))]


def _conv_post(srcs, post, ncol, tc, name, ts=256, out_dtype=F32):
    S = srcs[0][0].shape[0]
    ns = len(srcs)

    def kern(*refs):
        first = pl.program_id(0) == 0
        row8 = _iota((8, tc), 0)
        cs = []
        for s in range(ns):
            cur_ref, prev_ref, w_ref = refs[3 * s:3 * s + 3]
            prev = jnp.where(first, 0.0, prev_ref[...])
            wrows = [w_ref[j:j + 1, :] for j in range(w_ref.shape[0])]
            cs.append(_conv_rows(cur_ref[...], prev, wrows, row8))
        refs[3 * ns][...] = post(*cs).astype(refs[3 * ns].dtype)

    in_specs = []
    ops = []
    for (x, xoff, w, woff) in srcs:
        in_specs += _conv_specs(x, xoff, w, woff, ts, tc)
        ops += [x, x, w]
    return pl.pallas_call(
        kern, name=name, grid=(S // ts, ncol), in_specs=in_specs, out_specs=pl.BlockSpec((ts, tc), lambda i, j: (i, j)),
        out_shape=jax.ShapeDtypeStruct((S, ncol * tc), out_dtype),
        compiler_params=_cparams(("parallel", "parallel")))(*ops)


def _conv_post_bwd(srcs, post, ncol, tc, cot, name, ts=256):
    S = srcs[0][0].shape[0]
    ns = len(srcs)
    r8 = ts // 8
    nblk8 = S // 8
    nrow = S // ts

    def kern(*refs):
        i = pl.program_id(1)
        row8 = _iota((8, tc), 0)
        g_ref, gn_ref = refs[4 * ns:4 * ns + 2]
        outs = refs[4 * ns + 2:]
        xs, xps, ws, cs, cns = [], [], [], [], []
        for s in range(ns):
            cur_ref, prev_ref, next_ref, w_ref = refs[4 * s:4 * s + 4]
            xcur = cur_ref[...]
            xprev = jnp.where(i == 0, 0.0, prev_ref[...])
            wrows = [w_ref[j:j + 1, :] for j in range(w_ref.shape[0])]
            xs.append(xcur)
            xps.append(xprev)
            ws.append(wrows)
            cs.append(_conv_rows(xcur, xprev, wrows, row8))
            cns.append(_conv_rows(next_ref[...], xcur[ts - 8:], wrows, row8))
        _, vjp = jax.vjp(lambda *c: post(*c), *cs)
        dcs = vjp(g_ref[...])
        _, vjp_next = jax.vjp(lambda *c: post(*c), *cns)
        dcns = vjp_next(jnp.where(i == nrow - 1, 0.0, gn_ref[...]))
        for s in range(ns):
            dx_ref, dw_ref = outs[2 * s], outs[2 * s + 1]

            @pl.when(i == 0)
            def _(dw_ref=dw_ref):
                dw_ref[...] = jnp.zeros_like(dw_ref)

            k_w = len(ws[s])
            dx = None
            for j in range(k_w):
                sh = k_w - 1 - j
                t = _shift_up(dcs[s], dcns[s], sh, row8) * ws[s][j]
                dx = t if dx is None else dx + t
                dw_ref[j:j + 1, :] += jnp.sum(dcs[s] * _shift_down(xs[s], xps[s], sh, row8), axis=0, keepdims=True)
            dx_ref[...] = dx.astype(dx_ref.dtype)

    def nxt(i):
        return jnp.minimum((i + 1) * r8, nblk8 - 1)

    in_specs, ops = [], []
    for (x, xoff, w, woff) in srcs:
        in_specs += [pl.BlockSpec((ts, tc), functools.partial(lambda j, i, o: (i, j + o), o=xoff)),
                     pl.BlockSpec((8, tc), functools.partial(lambda j, i, o: (jnp.maximum(i * r8 - 1, 0), j + o),
                                                             o=xoff)),
                     pl.BlockSpec((8, tc), functools.partial(lambda j, i, o: (nxt(i), j + o), o=xoff)),
                     pl.BlockSpec((w.shape[0], tc), functools.partial(lambda j, i, o: (0, j + o), o=woff))]
        ops += [x, x, x, w]
    in_specs += [pl.BlockSpec((ts, tc), lambda j, i: (i, j)), pl.BlockSpec((8, tc), lambda j, i: (nxt(i), j))]
    ops += [cot, cot]
    out_specs, out_shape = [], []
    for (x, xoff, w, woff) in srcs:
        out_specs += [pl.BlockSpec((ts, tc), lambda j, i: (i, j)), pl.BlockSpec((w.shape[0], tc), lambda j, i: (0, j))]
        out_shape += [jax.ShapeDtypeStruct((S, ncol * tc), BF16), jax.ShapeDtypeStruct((w.shape[0], ncol * tc), F32)]
    return pl.pallas_call(
        kern, name=name, grid=(ncol, nrow), in_specs=in_specs, out_specs=out_specs, out_shape=out_shape,
        compiler_params=_cparams(("parallel", "arbitrary")))(*ops)


def _ret_tables(S):
    H, C, dh = 4, 128, 128
    lg = jnp.log1p(-jnp.exp2(-5.0 - jnp.arange(H, dtype=F32)))
    idx = jnp.arange(C, dtype=F32)
    diff = idx[:, None] - idx[None, :]
    causal = diff >= 0
    intra = jnp.where(causal, jnp.exp(lg[:, None, None] * jnp.where(causal, diff, 0.0)), 0.0)
    kdec = jnp.broadcast_to(jnp.exp(lg[:, None] * (C - 1 - idx))[:, :, None], (H, C, dh))
    qdec = jnp.broadcast_to(jnp.exp(lg[:, None] * (idx + 1))[:, :, None], (H, C, dh))
    cdec = jnp.broadcast_to(jnp.exp(lg * C)[:, None, None], (H, dh, dh))
    half = dh // 2
    inv = jnp.exp(-math.log(10000.0) * jnp.arange(half, dtype=F32) / half)
    ang = jnp.arange(S).astype(F32)[:, None] * inv[None, :]
    cos, sin = jnp.cos(ang), jnp.sin(ang)
    cosf = jnp.concatenate([cos, cos], axis=1)
    sinf = jnp.concatenate([-sin, sin], axis=1)
    return cosf, sinf, intra, kdec, qdec, cdec


def _ret_chunk(q, k, v, cosf, sinf, intra, kdec, qdec, cdec, state):
    hs = range(len(q))
    qr = [q[h] * cosf + _swap_halves(q[h]) * sinf for h in hs]
    kr = [(k[h] * cosf + _swap_halves(k[h]) * sinf) * (128 ** -0.5) for h in hs]
    scores = [mm_nt(qr[h], kr[h]) * intra[h] for h in hs]
    inner = [mm(scores[h], v[h]) for h in hs]
    kv = [mm_tn(kr[h] * kdec[h], v[h]) for h in hs]
    cross = [mm(qr[h] * qdec[h], state[h]) for h in hs]
    return [inner[h] + cross[h] for h in hs], [state[h] * cdec[h] + kv[h] for h in hs]


RET_H = 4


def _ret_call(proj, tabs, states=None, do=None):
    S = proj.shape[0]
    N = S // 128
    bwd = do is not None

    def nn(n):
        return N - 1 - n if bwd else n

    qkv_spec = pl.BlockSpec((128, 3 * 512), lambda n: (nn(n), 0))
    pos = pl.BlockSpec((128, 128), lambda n: (nn(n), 0))
    tab = pl.BlockSpec((RET_H, 128, 128), lambda n: (0, 0, 0))
    st_spec = pl.BlockSpec((None, RET_H, 128, 128), lambda n: (nn(n), 0, 0, 0))
    o_spec = pl.BlockSpec((128, 512), lambda n: (nn(n), 0))

    def kern(*refs):
        x_ref, c_ref, s_ref, i_ref, kd_ref, qd_ref, cd_ref = refs[:7]
        carry = refs[-1]
        heads = range(RET_H)

        @pl.when(pl.program_id(0) == 0)
        def _():
            carry[...] = jnp.zeros_like(carry)

        def cols(ref, off=0):
            return [ref[:, _hs(off + h)] for h in heads]

        def tabs_of(ref):
            return [ref[h] for h in heads]

        consts = (c_ref[...], s_ref[...], tabs_of(i_ref), tabs_of(kd_ref), tabs_of(qd_ref), tabs_of(cd_ref))
        qkv = (cols(x_ref), cols(x_ref, RET_H), cols(x_ref, 2 * RET_H))
        if bwd:
            sp_ref, do_ref = refs[7:9]
            outs = refs[9:12]
            _, vjp = jax.vjp(lambda q, k, v, s: _ret_chunk(q, k, v, *consts, s), *qkv, tabs_of(sp_ref))
            dq, dk, dv, ds = vjp((cols(do_ref), tabs_of(carry)))
            for h in heads:
                for o_ref, d in zip(outs, (dq[h], dk[h], dv[h])):
                    o_ref[:, _hs(h)] = d.astype(o_ref.dtype)
                carry[h] = ds[h]
        else:
            o_ref, sp_ref = refs[7:9]
            state = tabs_of(carry)
            out, new = _ret_chunk(*qkv, *consts, state)
            for h in heads:
                sp_ref[h] = state[h]
                o_ref[:, _hs(h)] = out[h]
                carry[h] = new[h]

    in_specs = [qkv_spec, pos, pos, tab, tab, tab, tab]
    if bwd:
        in_specs += [st_spec, o_spec]
        out_specs = [o_spec] * 3
        out_shape = [jax.ShapeDtypeStruct((S, 512), BF16)] * 3
        ops = (proj, *tabs, states, do)
    else:
        out_specs = [o_spec, st_spec]
        out_shape = [jax.ShapeDtypeStruct((S, 512), F32), jax.ShapeDtypeStruct((N, RET_H, 128, 128), F32)]
        ops = (proj, *tabs)
    return pl.pallas_call(
        kern, name="ret_bwd" if bwd else "ret_fwd", grid=(N,), in_specs=in_specs, out_specs=out_specs,
        out_shape=out_shape, scratch_shapes=[pltpu.VMEM((RET_H, 128, 128), F32)],
        compiler_params=_cparams(("arbitrary",)))(*ops)


GDN_C = 64
GDN_H = 8


def _unit_lower_inverse(a_mats, eye):
    p = [-a for a in a_mats]
    t = [eye + x for x in p]
    for _ in range(5):
        p = [mm3(x, x) for x in p]
        t = [mm3(y, eye + x) for y, x in zip(t, p)]
    return t


@jax.custom_vjp
def _known_inverse(a_mat, t_mat):
    return t_mat


def _known_inverse_f(a_mat, t_mat):
    return t_mat, t_mat


def _known_inverse_b(t_mat, g):
    return -mm3_tn(t_mat, mm3_nt(g, t_mat)), jnp.zeros_like(t_mat)


_known_inverse.defvjp(_known_inverse_f, _known_inverse_b)


def _gdn_intra(q, k, v, g_b, beta_b, t_known=None):
    c = GDN_C
    hs = range(len(q))
    q = [x * lax.rsqrt(jnp.sum(x * x, axis=-1, keepdims=True) + EPS) * (128 ** -0.5) for x in q]
    k = [x * lax.rsqrt(jnp.sum(x * x, axis=-1, keepdims=True) + EPS) for x in k]
    ri, ci = _iota((c, c), 0), _iota((c, c), 1)
    incl = ri >= ci
    strict = ri > ci
    eye = (ri == ci).astype(F32)
    lower = incl.astype(F32)
    gc_b = [mm3(lower, g) for g in g_b]
    gl_b = [jnp.sum(g, axis=0, keepdims=True) for g in g_b]
    kb = [k[h] * beta_b[h] for h in hs]
    vb = [v[h] * beta_b[h] for h in hs]
    gcc = [g[:, :c] for g in gc_b]
    decay = [jnp.where(incl, jnp.exp(jnp.where(incl, g - g.T, 0.0)), 0.0) for g in gcc]
    a_mat = [jnp.where(strict, mm_nt(kb[h], k[h]) * decay[h], 0.0) for h in hs]
    if t_known is None:
        t_mat = _unit_lower_inverse(a_mat, eye)
    else:
        t_mat = [_known_inverse(a_mat[h], t_known[h]) for h in hs]
    egc = [jnp.exp(g) for g in gc_b]
    w = [mm(t_mat[h], kb[h] * egc[h]) for h in hs]
    u = [mm(t_mat[h], vb[h]) for h in hs]
    qk = [jnp.where(incl, mm_nt(q[h], k[h]) * decay[h], 0.0) for h in hs]
    q_dec = [q[h] * egc[h] for h in hs]
    k_dec = [k[h] * jnp.exp(gl_b[h] - gc_b[h]) for h in hs]
    return w, u, q_dec, k_dec, qk, t_mat


def _gdn_step(w, u, q_dec, k_dec, qk, g_b, state):
    hs = range(len(w))
    gl_s = [jnp.sum(g, axis=0, keepdims=True) for g in g_b]
    ws = [mm(w[h], state[h]) for h in hs]
    qs = [mm(q_dec[h], state[h]) for h in hs]
    v_new = [u[h] - ws[h] for h in hs]
    o = [qs[h] + mm(qk[h], v_new[h]) for h in hs]
    new = [state[h] * jnp.exp(gl_s[h]) + mm_tn(k_dec[h], v_new[h]) for h in hs]
    return o, new


def _hs(h):
    return slice(h * 128, (h + 1) * 128)


def _gdn_intra_call(qkv, g_e, beta_e, cots=None):
    S = qkv.shape[0]
    N = S // GDN_C
    bwd = cots is not None
    row = pl.BlockSpec((GDN_C, 1024), lambda n: (n, 0))
    qkv_spec = pl.BlockSpec((GDN_C, 3072), lambda n: (n, 0))
    qk_spec = pl.BlockSpec((GDN_H, GDN_C, GDN_C), lambda n: (0, n, 0))

    def kern(*refs):
        x_ref, g_ref, b_ref = refs[:3]
        heads = range(GDN_H)

        def cols(ref, off=0):
            return [ref[:, _hs(off + h)] for h in heads]

        args = (cols(x_ref), cols(x_ref, 8), cols(x_ref, 16), cols(g_ref), cols(b_ref))
        if bwd:
            dw_ref, du_ref, dqd_ref, dkd_ref, dqk_ref, dgadd_ref, t_ref = refs[3:10]
            outs = refs[10:]
            t_known = [t_ref[h] for h in heads]
            _, vjp = jax.vjp(lambda *a: _gdn_intra(*a, t_known=t_known)[:5], *args)
            dq, dk, dv, dg, db = vjp((cols(dw_ref), cols(du_ref), cols(dqd_ref), cols(dkd_ref),
                                      [dqk_ref[h] for h in heads]))
            dgadd = cols(dgadd_ref)
            for h in heads:
                for o_ref, d in zip(outs, (dq[h], dk[h], dv[h], dg[h] + dgadd[h], db[h])):
                    o_ref[:, _hs(h)] = d
        else:
            w, u, qd, kd, qk, t_mat = _gdn_intra(*args)
            for h in heads:
                for o_ref, o in zip(refs[3:7], (w[h], u[h], qd[h], kd[h])):
                    o_ref[:, _hs(h)] = o
                refs[7][h] = qk[h]
                refs[8][h] = t_mat[h]

    big = jax.ShapeDtypeStruct((S, 1024), F32)
    sq = jax.ShapeDtypeStruct((GDN_H, S, GDN_C), F32)
    if bwd:
        in_specs = [qkv_spec, row, row, row, row, row, row, qk_spec, row, qk_spec]
        out_specs, out_shape = [row] * 5, [big] * 5
        ops = (qkv, g_e, beta_e) + tuple(cots)
    else:
        in_specs = [qkv_spec, row, row]
        out_specs = [row] * 4 + [qk_spec, qk_spec]
        out_shape = [big] * 4 + [sq, sq]
        ops = (qkv, g_e, beta_e)
    return pl.pallas_call(
        kern, name="gdn_intra_bwd" if bwd else "gdn_intra", grid=(N,), in_specs=in_specs, out_specs=out_specs,
        out_shape=out_shape, compiler_params=_cparams(("parallel",)))(*ops)


def _gdn_pass(w, u, qd, kd, qk, g_e, states=None, do=None):
    S = w.shape[0]
    N = S // GDN_C
    bwd = do is not None

    def nn(n):
        return N - 1 - n if bwd else n

    row = pl.BlockSpec((GDN_C, 1024), lambda n: (nn(n), 0))
    qk_spec = pl.BlockSpec((GDN_H, GDN_C, GDN_C), lambda n: (0, nn(n), 0))
    st_spec = pl.BlockSpec((None, GDN_H, 128, 128), lambda n: (nn(n), 0, 0, 0))

    def kern(*refs):
        w_ref, u_ref, qd_ref, kd_ref, qk_ref, g_ref = refs[:6]
        carry = refs[-1]

        @pl.when(pl.program_id(0) == 0)
        def _():
            carry[...] = jnp.zeros_like(carry)

        heads = range(GDN_H)

        def cols(ref):
            return [ref[:, _hs(h)] for h in heads]

        args = (cols(w_ref), cols(u_ref), cols(qd_ref), cols(kd_ref), [qk_ref[h] for h in heads], cols(g_ref))
        if bwd:
            sp_ref, do_ref = refs[6:8]
            outs = refs[8:14]
            _, vjp = jax.vjp(_gdn_step, *args, [sp_ref[h] for h in heads])
            dw, du, dqd, dkd, dqk, dg, ds = vjp((cols(do_ref), [carry[h] for h in heads]))
            for h in heads:
                for o_ref, d in zip(outs[:4], (dw[h], du[h], dqd[h], dkd[h])):
                    o_ref[:, _hs(h)] = d
                outs[4][h] = dqk[h]
                outs[5][:, _hs(h)] = dg[h]
                carry[h] = ds[h]
        else:
            o_ref, sp_ref = refs[6:8]
            state = [carry[h] for h in heads]
            o, new = _gdn_step(*args, state)
            for h in heads:
                sp_ref[h] = state[h]
                o_ref[:, _hs(h)] = o[h]
                carry[h] = new[h]

    big = jax.ShapeDtypeStruct((S, 1024), F32)
    in_specs = [row, row, row, row, qk_spec, row]
    if bwd:
        in_specs += [st_spec, row]
        out_specs = [row] * 4 + [qk_spec, row]
        out_shape = [big] * 4 + [jax.ShapeDtypeStruct((GDN_H, S, GDN_C), F32), big]
        ops = (w, u, qd, kd, qk, g_e, states, do)
    else:
        out_specs = [row, st_spec]
        out_shape = [big, jax.ShapeDtypeStruct((N, GDN_H, 128, 128), F32)]
        ops = (w, u, qd, kd, qk, g_e)
    return pl.pallas_call(
        kern, name="gdn_pass_bwd" if bwd else "gdn_pass", grid=(N,), in_specs=in_specs, out_specs=out_specs,
        out_shape=out_shape, scratch_shapes=[pltpu.VMEM((GDN_H, 128, 128), F32)],
        compiler_params=_cparams(("arbitrary",)))(*ops)


def _s5_prep_fn(lr, li, ldt, br, bi, cr, ci):
    dt = jnp.exp(ldt)
    mag = jnp.exp(lr * dt)
    a_re = mag * jnp.cos(li * dt)
    a_im = mag * jnp.sin(li * dt)
    den = lr * lr + li * li
    z_re = ((a_re - 1.0) * lr + a_im * li) / den
    z_im = (a_im * lr - (a_re - 1.0) * li) / den
    e1 = ((_iota((512, 32), 0) >> 4) == _iota((512, 32), 1)).astype(F32)
    zr_e = mmh(e1, z_re)
    zi_e = mmh(e1, z_im)
    bb_re = zr_e * br - zi_e * bi
    bb_im = zr_e * bi + zi_e * br
    t1 = ((_iota((64, 2048), 1) & 63) == _iota((64, 2048), 0)).astype(F32)
    m1 = (_iota((512, 2048), 0) >> 4) == (_iota((512, 2048), 1) >> 6)
    bd_re = jnp.where(m1, mmh(bb_re, t1), 0.0)
    bd_im = jnp.where(m1, mmh(bb_im, t1), 0.0)
    t2 = ((_iota((16, 512), 1) & 15) == _iota((16, 512), 0)).astype(F32)
    m2 = (_iota((2048, 512), 0) >> 6) == (_iota((2048, 512), 1) >> 4)
    cd_re = jnp.where(m2, mmh(cr, t2), 0.0)
    cd_im = jnp.where(m2, mmh(ci, t2), 0.0)
    return a_re, a_im, bd_re, bd_im, cd_re, cd_im


_PREP_OUT = [(32, 64), (32, 64), (512, 2048), (512, 2048), (2048, 512), (2048, 512)]


def _s5_prep(params, cots=None):
    bwd = cots is not None

    def kern(*refs):
        vals = [r[...] for r in refs[:7]]
        if bwd:
            gs = tuple(r[...] for r in refs[7:13])
            _, vjp = jax.vjp(_s5_prep_fn, *vals)
            for o_ref, d in zip(refs[13:], vjp(gs)):
                o_ref[...] = d
        else:
            for o_ref, o in zip(refs[7:], _s5_prep_fn(*vals)):
                o_ref[...] = o

    if bwd:
        out_shape = [jax.ShapeDtypeStruct(p.shape, F32) for p in params]
        ops = list(params) + list(cots)
    else:
        out_shape = [jax.ShapeDtypeStruct(s, F32) for s in _PREP_OUT]
        ops = list(params)
    return pl.pallas_call(kern, name="s5_prep_bwd" if bwd else "s5_prep", out_shape=out_shape,
                          compiler_params=_cparams())(*ops)


def _cmul(ar, ai, br, bi):
    return ar * br - ai * bi, ar * bi + ai * br


def _power_table(ar, ai, row8, descending):
    pr, pi = ar, ai
    tr = jnp.zeros(row8.shape, F32)
    ti = jnp.zeros(row8.shape, F32)
    for n in range(8):
        r = 7 - n if descending else n
        tr = jnp.where(row8 == r, pr, tr)
        ti = jnp.where(row8 == r, pi, ti)
        if n < 7:
            pr, pi = _cmul(pr, pi, ar, ai)
    return tr, ti


def _tile_scan(xr, xi, pows, row8, up):
    for d, (pr, pi) in zip((1, 2, 4), pows):
        if up:
            sr = jnp.where(row8 < 8 - d, pltpu.roll(xr, 8 - d, 0), 0.0)
            si = jnp.where(row8 < 8 - d, pltpu.roll(xi, 8 - d, 0), 0.0)
        else:
            sr = jnp.where(row8 >= d, pltpu.roll(xr, d, 0), 0.0)
            si = jnp.where(row8 >= d, pltpu.roll(xi, d, 0), 0.0)
        mr, mi = _cmul(pr, pi, sr, si)
        xr, xi = xr + mr, xi + mi
    return xr, xi


def _pick_row(x, row8, r):
    return jnp.sum(jnp.where(row8 == r, x, 0.0), axis=0, keepdims=True)


SCAN_LB = 512
SCAN_TS = 512


def _scan_fwd(bu_re, bu_im, a_re, a_im):
    S, L = bu_re.shape
    ts, lb = min(SCAN_TS, S), SCAN_LB
    nt = ts // 8

    def kern(br_ref, bi_ref, ar_ref, ai_ref, or_ref, oi_ref, cr_ref, ci_ref):
        @pl.when(pl.program_id(1) == 0)
        def _():
            cr_ref[...] = jnp.zeros_like(cr_ref)
            ci_ref[...] = jnp.zeros_like(ci_ref)

        row8 = _iota((8, lb), 0)
        ar, ai = ar_ref[...], ai_ref[...]
        a2 = _cmul(ar, ai, ar, ai)
        a4 = _cmul(*a2, *a2)
        pows = ((ar, ai), a2, a4)
        tr, ti = _power_table(ar, ai, row8, False)

        def body(i, carry):
            cr, ci = carry
            off = pl.multiple_of(i * 8, 8)
            xr, xi = _tile_scan(br_ref[pl.ds(off, 8), :], bi_ref[pl.ds(off, 8), :], pows, row8, False)
            mr, mi = _cmul(tr, ti, cr, ci)
            xr, xi = xr + mr, xi + mi
            or_ref[pl.ds(off, 8), :] = xr
            oi_ref[pl.ds(off, 8), :] = xi
            return _pick_row(xr, row8, 7), _pick_row(xi, row8, 7)

        cr, ci = lax.fori_loop(0, nt, body, (cr_ref[...], ci_ref[...]))
        cr_ref[...] = cr
        ci_ref[...] = ci

    blk = pl.BlockSpec((ts, lb), lambda j, i: (i, j))
    par = pl.BlockSpec((1, lb), lambda j, i: (0, j))
    return pl.pallas_call(
        kern, name="s5_scan_fwd", grid=(L // lb, S // ts), in_specs=[blk, blk, par, par], out_specs=[blk, blk],
        out_shape=[jax.ShapeDtypeStruct((S, L), F32)] * 2,
        scratch_shapes=[pltpu.VMEM((1, lb), F32), pltpu.VMEM((1, lb), F32)],
        compiler_params=_cparams(("parallel", "arbitrary")))(bu_re, bu_im, a_re, a_im)


def _scan_bwd(dst_re, dst_im, st_re, st_im, a_re, a_im):
    S, L = dst_re.shape
    ts, lb = min(SCAN_TS, S), SCAN_LB
    nt = ts // 8
    nb = S // ts
    r8 = ts // 8

    def kern(dr_ref, di_ref, sr_ref, si_ref, pr_ref, pi_ref, ar_ref, ai_ref, gr_ref, gi_ref, dar_ref, dai_ref,
             cr_ref, ci_ref):
        step = pl.program_id(1)
        blk = nb - 1 - step

        @pl.when(step == 0)
        def _():
            cr_ref[...] = jnp.zeros_like(cr_ref)
            ci_ref[...] = jnp.zeros_like(ci_ref)
            dar_ref[...] = jnp.zeros_like(dar_ref)
            dai_ref[...] = jnp.zeros_like(dai_ref)

        row8 = _iota((8, lb), 0)
        ar, ai = ar_ref[...], ai_ref[...]
        nai = -ai
        a2 = _cmul(ar, nai, ar, nai)
        a4 = _cmul(*a2, *a2)
        pows = ((ar, nai), a2, a4)
        tr, ti = _power_table(ar, nai, row8, True)
        halo_r = jnp.where(blk == 0, 0.0, pr_ref[...])
        halo_i = jnp.where(blk == 0, 0.0, pi_ref[...])

        def body(n, carry):
            cr, ci, acc_r, acc_i = carry
            i = nt - 1 - n
            off = pl.multiple_of(i * 8, 8)
            gr, gi = _tile_scan(dr_ref[pl.ds(off, 8), :], di_ref[pl.ds(off, 8), :], pows, row8, True)
            mr, mi = _cmul(tr, ti, cr, ci)
            gr, gi = gr + mr, gi + mi
            gr_ref[pl.ds(off, 8), :] = gr
            gi_ref[pl.ds(off, 8), :] = gi
            poff = pl.multiple_of(jnp.maximum(i - 1, 0) * 8, 8)
            before_r = jnp.where(i == 0, halo_r, sr_ref[pl.ds(poff, 8), :])
            before_i = jnp.where(i == 0, halo_i, si_ref[pl.ds(poff, 8), :])
            last_r = _pick_row(before_r, row8, 7)
            last_i = _pick_row(before_i, row8, 7)
            spr = jnp.where(row8 >= 1, pltpu.roll(sr_ref[pl.ds(off, 8), :], 1, 0), last_r)
            spi = jnp.where(row8 >= 1, pltpu.roll(si_ref[pl.ds(off, 8), :], 1, 0), last_i)
            acc_r = acc_r + gr * spr + gi * spi
            acc_i = acc_i + gi * spr - gr * spi
            return _pick_row(gr, row8, 0), _pick_row(gi, row8, 0), acc_r, acc_i

        zero = jnp.zeros((8, lb), F32)
        cr, ci, acc_r, acc_i = lax.fori_loop(0, nt, body, (cr_ref[...], ci_ref[...], zero, zero))
        cr_ref[...] = cr
        ci_ref[...] = ci
        dar_ref[...] += jnp.sum(acc_r, axis=0, keepdims=True)
        dai_ref[...] += jnp.sum(acc_i, axis=0, keepdims=True)

    blk = pl.BlockSpec((ts, lb), lambda j, i: (nb - 1 - i, j))
    halo = pl.BlockSpec((8, lb), lambda j, i: (jnp.maximum((nb - 1 - i) * r8 - 1, 0), j))
    par = pl.BlockSpec((1, lb), lambda j, i: (0, j))
    return pl.pallas_call(
        kern, name="s5_scan_bwd", grid=(L // lb, nb), in_specs=[blk, blk, blk, blk, halo, halo, par, par],
        out_specs=[blk, blk, par, par],
        out_shape=[jax.ShapeDtypeStruct((S, L), F32)] * 2 + [jax.ShapeDtypeStruct((1, L), F32)] * 2,
        scratch_shapes=[pltpu.VMEM((1, lb), F32), pltpu.VMEM((1, lb), F32)],
        compiler_params=_cparams(("parallel", "arbitrary")))(dst_re, dst_im, st_re, st_im, st_re, st_im, a_re, a_im)


def _loss_grad(x, target, gain, ts=256):
    S, D = x.shape

    def kern(x_ref, t_ref, g_ref, loss_ref, dx_ref, dg_ref):
        i = pl.program_id(0)
        tgt = t_ref[...]

        def f(xv, gv):
            err = _rms(xv, gv) - tgt
            return 0.5 * jnp.mean(err * err, axis=-1, keepdims=True)

        rowloss, vjp = jax.vjp(f, x_ref[...], g_ref[...])
        dx, dg = vjp(jnp.ones_like(rowloss))
        dx_ref[...] = dx

        @pl.when(i == 0)
        def _():
            loss_ref[...] = jnp.zeros_like(loss_ref)
            dg_ref[...] = jnp.zeros_like(dg_ref)

        loss_ref[...] += jnp.broadcast_to(jnp.sum(rowloss, axis=0, keepdims=True), loss_ref.shape)
        dg_ref[...] += dg

    row = pl.BlockSpec((ts, D), lambda i: (i, 0))
    return pl.pallas_call(
        kern, name="loss_grad", grid=(S // ts,), in_specs=[row, row, pl.BlockSpec((1, D), lambda i: (0, 0))],
        out_specs=[pl.BlockSpec((8, 128), lambda i: (0, 0)), row, pl.BlockSpec((1, D), lambda i: (0, 0))],
        out_shape=[jax.ShapeDtypeStruct((8, 128), F32), jax.ShapeDtypeStruct((S, D), F32),
                   jax.ShapeDtypeStruct((1, D), F32)],
        compiler_params=_cparams(("arbitrary",)))(x, target, gain)


def _rms_fwd(x, g, name):
    return _rowwise(_rms_fn, [_blk(x)], [g], [x.shape[1]], name, out_dtypes=[BF16])[0]


def _rms_bwd(x, g, dy, name, add=None):
    return _rowwise_bwd(_rms_fn, [_blk(x)], [g], [dy], name, adds=None if add is None else {0: add})


FFN_TC = 1408


def _common_fwd(x, mem, P, L):
    hx = _rms_fwd(x, P['xa_norm'], L + "xa_norm")
    q = _matmul(hx, P['xa_wq'], name=L + "xa_q")
    memn = _rms_fwd(mem, P['mem_norm'], L + "mem_norm")
    kv = _matmul(memn, P['xa_wkv'], name=L + "xa_kv")
    att = _rowwise(_xattn_fn, [_blk(q)], [kv], [1024], L + "xattn", out_dtypes=[BF16])[0]
    x2 = _matmul(att, P['xa_wo'], res=x, name=L + "xa_o")
    hf = _rms_fwd(x2, P['ffn_norm'], L + "ffn_norm")
    hu = _matmul(hf, P['ffn_w_up'], name=L + "ffn_up")
    cw = P['ffn_conv']
    act = _conv_post([(hu, 0, cw, 0), (hu, 2, cw, 2)], _ffn_post, 2, FFN_TC, L + "ffn_conv", out_dtype=BF16)
    x3 = _matmul(act, P['ffn_w_down'], res=x2, name=L + "ffn_down")
    return x3, (x, mem, hx, q, memn, kv, att, x2, hf, hu, act)


def _common_bwd(saved, dx3, P, L, midway=None):
    x, mem, hx, q, memn, kv, att, x2, hf, hu, act = saved
    G = {}
    dact = _matmul(dx3, P['ffn_w_down'], "nt", name=L + "ffn_down_dx")
    G['ffn_w_down'] = _matmul(act, dx3, "tn", name=L + "ffn_down_dw")
    cw = P['ffn_conv']
    dhu_u, dcw_u, dhu_g, dcw_g = _conv_post_bwd([(hu, 0, cw, 0), (hu, 2, cw, 2)], _ffn_post, 2, FFN_TC, dact,
                                                L + "ffn_conv_bwd")
    G['ffn_conv'] = jnp.concatenate([dcw_u, dcw_g], axis=1)
    dhf = _matmul_cat([dhu_u, dhu_g], P['ffn_w_up'], "nt", name=L + "ffn_up_dx")
    G['ffn_w_up'] = jnp.concatenate([_matmul(hf, dhu_u, "tn", name=L + "ffn_up_dw_up"),
                                     _matmul(hf, dhu_g, "tn", name=L + "ffn_up_dw_gate")], axis=1)
    dx2, G['ffn_norm'] = _rms_bwd(x2, P['ffn_norm'], dhf, L + "ffn_norm_bwd", add=dx3)
    if midway is not None:
        P = dict(P, xa_wo=P['xa_wo'] + midway(dx2).astype(P['xa_wo'].dtype))
    datt = _matmul(dx2, P['xa_wo'], "nt", name=L + "xa_o_dx")
    G['xa_wo'] = _matmul(att, dx2, "tn", name=L + "xa_o_dw")
    dq, dkv = _rowwise_bwd(_xattn_fn, [_blk(q)], [kv], [datt], L + "xattn_bwd", out_dtypes=[BF16])
    dhx = _matmul(dq, P['xa_wq'], "nt", name=L + "xa_q_dx")
    G['xa_wq'] = _matmul(hx, dq, "tn", name=L + "xa_q_dw")
    dmemn = _matmul(dkv, P['xa_wkv'], "nt", name=L + "xa_kv_dx")
    G['xa_wkv'] = _matmul(memn, dkv, "tn", name=L + "xa_kv_dw")
    _, G['mem_norm'] = _rms_bwd(mem, P['mem_norm'], dmemn, L + "mem_norm_bwd")
    dx, G['xa_norm'] = _rms_bwd(x, P['xa_norm'], dhx, L + "xa_norm_bwd", add=dx2)
    return dx, G


U_COLS = (2048, 512)


def _even_fwd(x, P):
    S = x.shape[0]
    h0 = _rms_fwd(x, P['mix_norm'], "l0_mix_norm")
    proj = _matmul(h0, P['w_in'], name="l0_in")
    tabs = _ret_tables(S)
    o_raw, rstates = _ret_call(proj, tabs)
    o = _rowwise(_ret_post_fn, [_blk(o_raw), _blk(proj, 512, 3)], [P['ret_norm']], [512], "l0_ret_post",
                 out_dtypes=[BF16])[0]
    prep_in = (P['s5_lambda_re'], P['s5_lambda_im'], P['s5_log_dt'], P['s5_b_re'], P['s5_b_im'], P['s5_c_re'],
               P['s5_c_im'])
    a_re, a_im, bd_re, bd_im, cd_re, cd_im = _s5_prep(prep_in)
    a_re_f, a_im_f = a_re.reshape(1, 2048), a_im.reshape(1, 2048)
    bu_re = _matmul(proj, bd_re, name="l0_s5_bu_re", a_cols=U_COLS)
    bu_im = _matmul(proj, bd_im, name="l0_s5_bu_im", a_cols=U_COLS)
    st_re, st_im = _scan_fwd(bu_re, bu_im, a_re_f, a_im_f)
    y1 = _matmul(st_re, cd_re, name="l0_s5_y_re")
    y2 = _matmul(st_im, cd_im, name="l0_s5_y_im")
    yg = _rowwise(_s5_post_fn, [_blk(y1), _blk(y2), _blk(proj, 512, 4)],
                  [P['s5_d'], P['s5_w_glu'], P['s5_b_glu']], [512], "l0_s5_post", out_dtypes=[BF16])[0]
    x1 = _matmul_cat([o, yg], P['w_out'], "nn", res=x, name="l0_out")
    saved = (x, h0, proj, tabs, o_raw, rstates, prep_in, a_re_f, a_im_f, bd_re, bd_im, cd_re, cd_im, st_re, st_im,
             y1, y2, o, yg)
    return x1, saved


def _even_bwd(saved, dx1, P, midway=None):
    (x, h0, proj, tabs, o_raw, rstates, prep_in, a_re_f, a_im_f, bd_re, bd_im, cd_re, cd_im, st_re, st_im, y1, y2,
     o, yg) = saved
    G = {}
    dmerged = _matmul(dx1, P['w_out'], "nt", name="l0_out_dx")
    G['w_out'] = jnp.concatenate([_matmul(o, dx1, "tn", name="l0_out_dw_ret"),
                                  _matmul(yg, dx1, "tn", name="l0_out_dw_s5")], axis=0)
    do_raw, dgate, G['ret_norm'] = _rowwise_bwd(
        _ret_post_fn, [_blk(o_raw), _blk(proj, 512, 3)], [P['ret_norm']], [_blk(dmerged, 512, 0)], "l0_ret_post_bwd",
        out_dtypes=[F32, BF16])
    dq, dk, dv = _ret_call(proj, tabs, states=rstates, do=do_raw)
    if midway is not None:
        P = dict(P, s5_w_glu=P['s5_w_glu'] + midway(dq))
    dy1, dy2, du_a, G['s5_d'], G['s5_w_glu'], G['s5_b_glu'] = _rowwise_bwd(
        _s5_post_fn, [_blk(y1), _blk(y2), _blk(proj, 512, 4)], [P['s5_d'], P['s5_w_glu'], P['s5_b_glu']],
        [_blk(dmerged, 512, 1)], "l0_s5_post_bwd", out_dtypes=[BF16, BF16, F32])
    dst_re = _matmul(dy1, cd_re, "nt", name="l0_s5_y_re_dx")
    dcd_re = _matmul(st_re, dy1, "tn", name="l0_s5_y_re_dw")
    dst_im = _matmul(dy2, cd_im, "nt", name="l0_s5_y_im_dx")
    dcd_im = _matmul(st_im, dy2, "tn", name="l0_s5_y_im_dw")
    dbu_re, dbu_im, da_re, da_im = _scan_bwd(dst_re, dst_im, st_re, st_im, a_re_f, a_im_f)
    du = _matmul(dbu_re, bd_re, "nt", res=du_a, name="l0_s5_bu_re_dx")
    du = _matmul(dbu_im, bd_im, "nt", res=du, name="l0_s5_bu_im_dx", out_dtype=BF16)
    dbd_re = _matmul(proj, dbu_re, "tn", name="l0_s5_bu_re_dw", a_cols=U_COLS)
    dbd_im = _matmul(proj, dbu_im, "tn", name="l0_s5_bu_im_dw", a_cols=U_COLS)
    dprep = _s5_prep(prep_in, cots=(da_re.reshape(32, 64), da_im.reshape(32, 64), dbd_re, dbd_im, dcd_re, dcd_im))
    for n, d in zip(('s5_lambda_re', 's5_lambda_im', 's5_log_dt', 's5_b_re', 's5_b_im', 's5_c_re', 's5_c_im'), dprep):
        G[n] = d
    pieces = [dq, dk, dv, dgate, du]
    dh0 = _matmul_cat(pieces, P['w_in'], "nt", name="l0_in_dx")
    G['w_in'] = jnp.concatenate([_matmul(h0, p, "tn", name="l0_in_dw_%d" % n) for n, p in enumerate(pieces)], axis=1)
    dx, G['mix_norm'] = _rms_bwd(x, P['mix_norm'], dh0, "l0_mix_norm_bwd", add=dx1)
    return dx, G


def _odd_fwd(x, P):
    h1 = _rms_fwd(x, P['mix_norm'], "l1_mix_norm")
    pm = _matmul(h1, P['w_all'], name="l1_in")
    pt = _blk(pm, 128, 32)
    qkv = _conv_post([(pm, 0, P['conv'], 0)], _silu, 3, 1024, "l1_conv")
    g_e, beta_e = _rowwise(_gdn_gates_fn, [pt], [P['a_log_p'], P['dtb_p']], [1024, 1024], "l1_gdn_gates")
    w, u, qd, kd, qk, tinv = _gdn_intra_call(qkv, g_e, beta_e)
    o_raw, gstates = _gdn_pass(w, u, qd, kd, qk, g_e)
    og = _rowwise(_gdn_post_fn, [_blk(o_raw), _blk(pm, 1024, 3)], [P['o_norm']], [1024], "l1_gdn_post",
                  out_dtypes=[BF16])[0]
    x1 = _matmul(og, P['w_out'], res=x, name="l1_out")
    return x1, (x, h1, pm, pt, qkv, g_e, beta_e, w, u, qd, kd, qk, tinv, o_raw, gstates, og)


def _odd_bwd(saved, dx1, P):
    x, h1, pm, pt, qkv, g_e, beta_e, w, u, qd, kd, qk, tinv, o_raw, gstates, og = saved
    G = {}
    dog = _matmul(dx1, P['w_out'], "nt", name="l1_out_dx")
    G['w_out'] = _matmul(og, dx1, "tn", name="l1_out_dw")
    do_raw, dz, G['o_norm'] = _rowwise_bwd(_gdn_post_fn, [_blk(o_raw), _blk(pm, 1024, 3)], [P['o_norm']], [dog],
                                           "l1_gdn_post_bwd", out_dtypes=[F32, BF16])
    dw, du, dqd, dkd, dqk, dg_pass = _gdn_pass(w, u, qd, kd, qk, g_e, states=gstates, do=do_raw)
    dqkv = _gdn_intra_call(qkv, g_e, beta_e, cots=(dw, du, dqd, dkd, dqk, dg_pass, tinv))
    dg_e, dbeta_e = dqkv[3], dqkv[4]
    dpt, G['a_log_p'], G['dtb_p'] = _rowwise_bwd(_gdn_gates_fn, [pt], [P['a_log_p'], P['dtb_p']],
                                                 [dg_e, dbeta_e], "l1_gdn_gates_bwd", out_dtypes=[BF16])
    pieces, dcw = [], []
    for part in range(3):
        dxp, dwp = _conv_post_bwd([(pm, part, P['conv'], part)], _silu, 1, 1024, dqkv[part],
                                  "l1_conv_bwd_%d" % part)
        pieces.append(dxp)
        dcw.append(dwp)
    G['conv'] = jnp.concatenate(dcw, axis=1)
    pieces += [dz, dpt]
    dh1 = _matmul_cat(pieces, P['w_all'], "nt", name="l1_in_dx")
    G['w_all'] = jnp.concatenate([_matmul(h1, p, "tn", name="l1_in_dw_%d" % n) for n, p in enumerate(pieces)], axis=1)
    dx, G['mix_norm'] = _rms_bwd(x, P['mix_norm'], dh1, "l1_mix_norm_bwd", add=dx1)
    return dx, G


def _row(v):
    return v.reshape(1, -1)


def _local_step(x, mem, target, W, later_weights=None, early_grads=None):
    P0 = {
        'mix_norm': _row(W['l0_mix_norm']), 'w_in': W['l0_w_in'], 'ret_norm': _row(W['l0_ret_norm']),
        's5_lambda_re': W['l0_s5_lambda_re'], 's5_lambda_im': W['l0_s5_lambda_im'],
        's5_log_dt': W['l0_s5_log_dt'].reshape(32, 1),
        's5_b_re': W['l0_s5_b_re'].reshape(512, 64), 's5_b_im': W['l0_s5_b_im'].reshape(512, 64),
        's5_c_re': W['l0_s5_c_re'].reshape(2048, 16), 's5_c_im': W['l0_s5_c_im'].reshape(2048, 16),
        's5_d': _row(W['l0_s5_d']), 's5_w_glu': W['l0_s5_w_glu'].astype(F32), 's5_b_glu': _row(W['l0_s5_b_glu']),
        'w_out': W['l0_w_out'],
    }
    def common(L):
        return {'xa_norm': _row(W[L + 'xa_norm']), 'mem_norm': _row(W[L + 'mem_norm']), 'xa_wq': W[L + 'xa_wq'],
                'xa_wkv': W[L + 'xa_wkv'], 'xa_wo': W[L + 'xa_wo'], 'ffn_norm': _row(W[L + 'ffn_norm']),
                'ffn_w_up': W[L + 'ffn_w_up'], 'ffn_conv': W[L + 'ffn_conv'], 'ffn_w_down': W[L + 'ffn_w_down']}

    x1, s_even = _even_fwd(x, P0)
    if later_weights is not None:
        W = dict(W, **later_weights('l0_common', x1))
    C0 = common('l0_')
    x3, s_c0 = _common_fwd(x1, mem, C0, "l0_")

    if later_weights is not None:
        W = dict(W, **later_weights('l1', x3))
    w_in1 = W['l1_w_in']
    pad8 = jnp.zeros((8,), F32)
    w_all = jnp.pad(w_in1, ((0, 0), (0, 112)))
    P1 = {
        'mix_norm': _row(W['l1_mix_norm']), 'w_all': w_all,
        'conv': W['l1_conv'],
        'a_log_p': _row(jnp.concatenate([pad8, W['l1_a_log'], jnp.zeros((112,), F32)])),
        'dtb_p': _row(jnp.concatenate([pad8, W['l1_dt_bias'], jnp.zeros((112,), F32)])),
        'o_norm': _row(W['l1_o_norm']), 'w_out': W['l1_w_out'],
    }
    C1 = common('l1_')
    x4, s_odd = _odd_fwd(x3, P1)
    x6, s_c1 = _common_fwd(x4, mem, C1, "l1_")
    loss_tile, dx6, d_final = _loss_grad(x6, target, _row(W['final_norm']))

    G = {'final_norm': d_final.reshape(-1)}
    dx4, g = _common_bwd(s_c1, dx6, C1, "l1_")
    for k, v in g.items():
        G['l1_' + k] = v
    dx3, g = _odd_bwd(s_odd, dx4, P1)
    G['l1_mix_norm'] = g['mix_norm']
    G['l1_w_in'] = g['w_all'][:, :4112]
    G['l1_conv'] = g['conv']
    G['l1_a_log'] = g['a_log_p'][0, 8:16]
    G['l1_dt_bias'] = g['dtb_p'][0, 8:16]
    G['l1_o_norm'] = g['o_norm']
    G['l1_w_out'] = g['w_out']
    midway = None
    if early_grads is not None:
        zero, midway = early_grads('l1', G)
        C0 = dict(C0, ffn_w_down=C0['ffn_w_down'] + zero.astype(C0['ffn_w_down'].dtype))
    dx1, g = _common_bwd(s_c0, dx3, C0, "l0_", midway=midway)
    for k, v in g.items():
        G['l0_' + k] = v
    if early_grads is not None:
        zero, midway = early_grads('l0_common', G)
        P0 = dict(P0, w_out=P0['w_out'] + zero.astype(P0['w_out'].dtype))
    dx0, g = _even_bwd(s_even, dx1, P0, midway=midway)
    for k, v in g.items():
        G['l0_' + k] = v
    return loss_tile, dx0, G


ANY = pl.BlockSpec(memory_space=pl.ANY)


def _place():
    return lax.axis_index("x"), lax.axis_index("y"), lax.axis_index("c")


def _my_chip():
    return 2 * lax.axis_index("x") + lax.axis_index("y")


def _chip_peers(x, y):
    return [(1 - x, y), (x, 1 - y), (1 - x, 1 - y)]


def _half(ref, mode, shard, j, h, split):
    r, w = shard
    rh = r // 2 if split else r
    h = h if split else 0
    if mode == 'row':
        return ref.at[pl.ds(j * r + h * rh, rh), :]
    if mode == 'col':
        return ref.at[pl.ds(h * rh, rh), pl.ds(j * w, w)]
    return ref.at[j, pl.ds(h * rh, rh), :]


def _place_shard(shard, mode, name):
    r, w = shard.shape
    dtype = BF16 if mode != 'tap' else shard.dtype
    if mode == 'tap':
        mode = 'slab'
    tr = _row_tile(r, w)
    nb = r // tr

    def kern(s_ref, o_ref):
        o_ref[...] = s_ref[...].astype(o_ref.dtype)

    if mode == 'row':
        full, o_spec = (4 * r, w), pl.BlockSpec((tr, w), lambda i: (_my_chip() * nb + i, 0))
    elif mode == 'col':
        full, o_spec = (r, 4 * w), pl.BlockSpec((tr, w), lambda i: (i, _my_chip()))
    else:
        full, o_spec = (4, r, w), pl.BlockSpec((None, tr, w), lambda i: (_my_chip(), i, 0))
    return pl.pallas_call(kern, name=name, grid=(nb,), in_specs=[pl.BlockSpec((tr, w), lambda i: (i, 0))],
                          out_specs=o_spec, out_shape=jax.ShapeDtypeStruct(full, dtype),
                          compiler_params=_cparams(("parallel",)))(shard)


def _gather_placed(fulls, modes, shards, splits):
    n = len(fulls)

    def body(*refs):
        outs = refs[n:2 * n]
        send_sems, recv_sems = refs[2 * n:]
        x, y, c = _place()
        peers = _chip_peers(x, y)
        me = 2 * x + y

        def win(a, j, h):
            return _half(outs[a], modes[a], shards[a], j, h, splits[a])

        def copy(a, k, j, h, to):
            return pltpu.make_async_remote_copy(src_ref=win(a, j, h), dst_ref=win(a, j, h),
                                                send_sem=send_sems.at[6 * a + k], recv_sem=recv_sems.at[6 * a + k],
                                                device_id=to, device_id_type=MESH)

        over_ici = [copy(a, k, me, c, (p[0], p[1], c)) for a in range(n) for k, p in enumerate(peers)]
        for cp in over_ici:
            cp.start()
        passed = []
        for a in range(n):
            for k, p in enumerate(peers):
                j = 2 * p[0] + p[1]
                copy(a, k, j, c, (p[0], p[1], c)).wait_recv()
                if splits[a]:
                    fwd = copy(a, 3 + k, j, c, (x, y, 1 - c))
                    fwd.start()
                    passed.append(fwd)
        for a in range(n):
            if splits[a]:
                for k, p in enumerate(peers):
                    copy(a, 3 + k, 2 * p[0] + p[1], 1 - c, (x, y, 1 - c)).wait_recv()
        for cp in over_ici + passed:
            cp.wait_send()

    return pl.pallas_call(
        body, name="gather_weights", in_specs=[ANY] * n, out_specs=[ANY] * n,
        out_shape=[jax.ShapeDtypeStruct(f.shape, f.dtype) for f in fulls],
        input_output_aliases={a: a for a in range(n)},
        scratch_shapes=[pltpu.SemaphoreType.DMA((6 * n,)), pltpu.SemaphoreType.DMA((6 * n,))],
    )(*fulls)


_FLIPS = [(dx, dy, dc) for dx in (0, 1) for dy in (0, 1) for dc in (0, 1) if (dx, dy, dc) != (0, 0, 0)]


def _share_halves(bufs, name):
    n = len(bufs)

    def body(*refs):
        outs = refs[n:2 * n]
        send_sems, recv_sems = refs[2 * n:]
        x, y, c = _place()
        sends, waits = [], []
        for a in range(n):
            rh = bufs[a].shape[0] // 2
            mine = outs[a].at[pl.ds(c * rh, rh), :]
            other = outs[a].at[pl.ds((1 - c) * rh, rh), :]
            sends.append(pltpu.make_async_remote_copy(src_ref=mine, dst_ref=mine, send_sem=send_sems.at[a],
                                                      recv_sem=recv_sems.at[a], device_id=(x, y, 1 - c),
                                                      device_id_type=MESH))
            waits.append(pltpu.make_async_remote_copy(src_ref=mine, dst_ref=other, send_sem=send_sems.at[a],
                                                      recv_sem=recv_sems.at[a], device_id=(x, y, 1 - c),
                                                      device_id_type=MESH))
        for cp in sends:
            cp.start()
        for cp in waits:
            cp.wait()

    return pl.pallas_call(
        body, name=name, in_specs=[ANY] * n, out_specs=[ANY] * n,
        out_shape=[jax.ShapeDtypeStruct(b.shape, b.dtype) for b in bufs],
        input_output_aliases={a: a for a in range(n)},
        scratch_shapes=[pltpu.SemaphoreType.DMA((n,)), pltpu.SemaphoreType.DMA((n,))],
    )(*bufs)


TILE_BYTES = 2 * 1024 * 1024


def _row_tile(rows, width=1024):
    for t in (512, 352, 256, 176, 128, 64, 32, 16, 8):
        if rows % t == 0 and t * width * 4 <= TILE_BYTES:
            return t
    return rows


def _pair_sum(g, got, name):
    ns, r, w = g.shape
    rh = r // 2
    tr = _row_tile(rh, w)
    nb = rh // tr

    def kern(g_ref, o_ref, out_ref):
        out_ref[...] = (g_ref[...] + o_ref[...]).astype(BF16)

    return pl.pallas_call(
        kern, name=name, grid=(ns, nb),
        in_specs=[pl.BlockSpec((None, tr, w), lambda j, i: (j, lax.axis_index("c") * nb + i, 0)),
                  pl.BlockSpec((None, tr, w), lambda j, i: (j, i, 0))],
        out_specs=pl.BlockSpec((None, tr, w), lambda j, i: (j, i, 0)),
        out_shape=jax.ShapeDtypeStruct((ns, rh, w), BF16),
        compiler_params=_cparams(("parallel", "parallel")))(g, got)


def _chip_sum(pair, recv, w, name):
    rh = pair.shape[1]
    tr = _row_tile(rh, w)
    nb = rh // tr

    def kern(own_ref, r1_ref, r2_ref, r3_ref, out_ref):
        acc = own_ref[...].astype(F32)
        for r_ref in (r1_ref, r2_ref, r3_ref):
            acc = acc + r_ref[...].astype(F32)
        out_ref[...] = acc

    if pair.shape[0] == 4:
        own_spec = pl.BlockSpec((None, tr, w), lambda i: (_my_chip(), i, 0))
    else:
        own_spec = pl.BlockSpec((None, tr, w), lambda i: (0, i, _my_chip()))
    recv_specs = [pl.BlockSpec((None, tr, w), functools.partial(lambda i, d: ((_my_chip() + d) % 4, i, 0), d=d))
                  for d in (1, 2, 3)]
    return pl.pallas_call(
        kern, name=name, grid=(nb,), in_specs=[own_spec] + recv_specs,
        out_specs=pl.BlockSpec((tr, w), lambda i: (lax.axis_index("c") * nb + i, 0)),
        out_shape=jax.ShapeDtypeStruct((2 * rh, w), F32), compiler_params=_cparams(("parallel",)))(pair, recv, recv, recv)


def _slab_sum(slabs, name):
    n, R, w = slabs.shape
    tr = _row_tile(R)

    def kern(s_ref, o_ref):
        acc = s_ref[0].astype(F32)
        for k in range(1, n):
            acc = acc + s_ref[k].astype(F32)
        o_ref[...] = acc

    return pl.pallas_call(
        kern, name=name, grid=(R // tr,), in_specs=[pl.BlockSpec((n, tr, w), lambda i: (0, i, 0))],
        out_specs=pl.BlockSpec((tr, w), lambda i: (i, 0)), out_shape=jax.ShapeDtypeStruct((R, w), F32),
        compiler_params=_cparams(("parallel",)))(slabs)


def _adamw(w, g, m, v, name):
    R, C = w.shape
    tr = _pick(R, (256, 128, 64, 32, 16, 8))

    def kern(w_ref, g_ref, m_ref, v_ref, d_ref, nm_ref, nv_ref, g_out_ref):
        gv = g_ref[...]
        g_out_ref[...] = gv
        m2 = ADAM_B1 * m_ref[...] + (1.0 - ADAM_B1) * gv
        v2 = ADAM_B2 * v_ref[...] + (1.0 - ADAM_B2) * jnp.square(gv)
        m_hat = m2 / (1.0 - ADAM_B1 ** ADAM_STEP)
        v_hat = v2 / (1.0 - ADAM_B2 ** ADAM_STEP)
        d_ref[...] = -ADAM_LR * (m_hat / (jnp.sqrt(v_hat) + ADAM_EPS) + ADAM_WD * w_ref[...])
        nm_ref[...] = m2
        nv_ref[...] = v2

    spec = pl.BlockSpec((tr, C), lambda i: (i, 0))
    return pl.pallas_call(
        kern, name=name, grid=(R // tr,), in_specs=[spec] * 4, out_specs=[spec] * 4,
        out_shape=[jax.ShapeDtypeStruct((R, C), F32)] * 4, compiler_params=_cparams(("parallel",)))(w, g, m, v)


def _pack_small(vals):
    flat = jnp.concatenate([vals[n].astype(F32).reshape(-1) for n in SMALL_NAMES])
    rows = -(-flat.shape[0] // (8 * LANES)) * 8
    return jnp.pad(flat, (0, rows * LANES - flat.shape[0])).reshape(rows, LANES)


def _unpack_small(packed, shapes):
    flat = packed.reshape(-1)
    out = {}
    off = 0
    for n in SMALL_NAMES:
        size = int(np.prod(shapes[n]))
        out[n] = flat[off:off + size].reshape(shapes[n])
        off += size
    return out


HBM = pl.BlockSpec(memory_space=pltpu.HBM)
SEM = pl.BlockSpec(memory_space=pltpu.SEMAPHORE)
DATAFLOW = pltpu.SideEffectType.DATAFLOW_SIDE_EFFECTING


def _in_hbm(a):
    return pltpu.with_memory_space_constraint(a, pltpu.HBM)


def _split_copy_start(srcs, lands, copies, after, name):
    ns, nl = len(srcs), len(lands)
    ncopy = len(copies(list(srcs), list(lands), None, None, probe=True))

    def body(*refs):
        src_refs, land_refs = refs[:ns], refs[ns:ns + nl]
        send_sems, recv_sems = refs[ns + nl + 1:ns + nl + 3]
        token = refs[-1]
        for cp in copies(src_refs, land_refs, send_sems, recv_sems):
            cp.start()
        token[...] = jnp.zeros_like(token)

    outs = pl.pallas_call(
        body, name=name,
        out_shape=(pltpu.SemaphoreType.DMA((ncopy,)), pltpu.SemaphoreType.DMA((ncopy,)),
                   *[pltpu.HBM(a.shape, a.dtype) for a in srcs], *[pltpu.HBM(a.shape, a.dtype) for a in lands],
                   jax.ShapeDtypeStruct((8, 128), F32)),
        in_specs=[HBM] * (ns + nl) + [ANY],
        out_specs=(SEM, SEM, *[HBM] * (ns + nl), pl.BlockSpec(memory_space=pltpu.VMEM)),
        input_output_aliases={i: 2 + i for i in range(ns + nl)},
        compiler_params=pltpu.CompilerParams(has_side_effects=DATAFLOW),
    )(*[_in_hbm(a) for a in srcs], *[_in_hbm(a) for a in lands], after)
    return outs[0], outs[1], outs[2:2 + ns], outs[2 + ns:2 + ns + nl], outs[-1]


def _split_copy_wait(send_sems, recv_sems, srcs, lands, copies, after, name):
    ns, nl = len(srcs), len(lands)

    def body(*refs):
        src_refs, land_refs = refs[:ns], refs[ns:ns + nl]
        send_ref, recv_ref = refs[ns + nl:ns + nl + 2]
        for cp in copies(src_refs, land_refs, send_ref, recv_ref):
            cp.wait_send()
            cp.wait_recv()

    outs = pl.pallas_call(
        body, name=name,
        out_shape=tuple(pltpu.HBM(a.shape, a.dtype) for a in list(srcs) + list(lands)),
        in_specs=[HBM] * (ns + nl) + [SEM, SEM, ANY], out_specs=tuple([HBM] * (ns + nl)),
        input_output_aliases={i: i for i in range(ns + nl)},
        compiler_params=pltpu.CompilerParams(has_side_effects=DATAFLOW),
    )(*srcs, *lands, send_sems, recv_sems, after)
    return outs[:ns], outs[ns:]


def _matrix_mode(n):
    return 'slab' if n == 'l1_w_in' else ('row' if MATRICES[n] == 0 else 'col')


def _placed(A, names):
    modes = ['slab' if n in CONVS else _matrix_mode(n) for n in names]
    fulls = [_place_shard(A[n], 'tap' if n in CONVS else m, "place_" + n) for n, m in zip(names, modes)]
    return fulls, modes


def _assembled(names, modes, outs):
    return {n: jnp.concatenate([o[j] for j in range(4)], axis=1) if m == 'slab' else o
            for n, m, o in zip(names, modes, outs)}


def _gather_weights(A, names):
    fulls, modes = _placed(A, names)
    outs = _gather_placed(fulls, modes, [A[n].shape for n in names], [n not in CONVS for n in names])
    return _assembled(names, modes, outs)


def _whole_shard_copies(modes, shards):
    def copies(src_refs, land_refs, send_sems, recv_sems, probe=False):
        if probe:
            return [None] * (3 * len(land_refs))
        x, y, c = _place()
        me = 2 * x + y
        out = []
        for a, ref in enumerate(land_refs):
            for k, p in enumerate(_chip_peers(x, y)):
                out.append(pltpu.make_async_remote_copy(
                    src_ref=_half(ref, modes[a], shards[a], me, 0, False),
                    dst_ref=_half(ref, modes[a], shards[a], me, 0, False),
                    send_sem=send_sems.at[3 * a + k], recv_sem=recv_sems.at[3 * a + k],
                    device_id=(p[0], p[1], c), device_id_type=MESH))
        return out
    return copies


def _gather_weights_start(A, names, after, tag):
    fulls, modes = _placed(A, names)
    copies = _whole_shard_copies(modes, [A[n].shape for n in names])
    send_sems, recv_sems, _, lands, zeros = _split_copy_start([], fulls, copies, after, "gather_start_" + tag)
    return (send_sems, recv_sems, lands, copies, names, modes), zeros


def _gather_weights_wait(state, after, tag):
    send_sems, recv_sems, lands, copies, names, modes = state
    _, outs = _split_copy_wait(send_sems, recv_sems, [], lands, copies, after, "gather_wait_" + tag)
    return _assembled(names, modes, outs)


def _to_chips_copies(pair_shapes, widths):
    def copies(src_refs, land_refs, send_sems, recv_sems, probe=False):
        if probe:
            return [None] * (3 * len(land_refs))
        x, y, c = _place()
        me = 2 * x + y
        out = []
        for a, (src, land) in enumerate(zip(src_refs, land_refs)):
            for k, p in enumerate(_chip_peers(x, y)):
                j = 2 * p[0] + p[1]
                part = src.at[j] if pair_shapes[a][0] == 4 else src.at[0, :, pl.ds(j * widths[a], widths[a])]
                out.append(pltpu.make_async_remote_copy(
                    src_ref=part, dst_ref=land.at[me], send_sem=send_sems.at[3 * a + k],
                    recv_sem=recv_sems.at[3 * a + k], device_id=(p[0], p[1], c), device_id_type=MESH))
        return out
    return copies


def _slabbed(G, names):
    gs, widths = [], []
    for n in names:
        g = G[n]
        mode = _matrix_mode(n)
        if mode == 'row':
            gs.append(g.reshape(4, g.shape[0] // 4, g.shape[1]))
            widths.append(g.shape[1])
        elif mode == 'col':
            gs.append(g[None])
            widths.append(g.shape[1] // 4)
        else:
            wd = g.shape[1] // 4
            gs.append(jnp.stack([g[:, j * wd:(j + 1) * wd] for j in range(4)]))
            widths.append(wd)
    return gs, widths


def _other_half_copies(shapes):
    def copies(src_refs, land_refs, send_sems, recv_sems, probe=False):
        if probe:
            return [None] * len(land_refs)
        x, y, c = _place()
        out = []
        for a, (src, land) in enumerate(zip(src_refs, land_refs)):
            rh = shapes[a][1] // 2
            out.append(pltpu.make_async_remote_copy(
                src_ref=src.at[:, pl.ds((1 - c) * rh, rh), :], dst_ref=land, send_sem=send_sems.at[a],
                recv_sem=recv_sems.at[a], device_id=(x, y, 1 - c), device_id_type=MESH))
        return out
    return copies


def _reduce_end(names, pairs, recv, widths, tag):
    halves = [_chip_sum(p, r, w, "chip_sum_" + n) for n, p, r, w in zip(names, pairs, recv, widths)]
    return dict(zip(names, _share_halves(halves, "share_halves_" + tag)))


def _small_copies():
    def copies(src_refs, land_refs, send_sems, recv_sems, probe=False):
        if probe:
            return [None] * len(_FLIPS)
        x, y, c = _place()
        mine = land_refs[0].at[4 * x + 2 * y + c]
        return [pltpu.make_async_remote_copy(src_ref=mine, dst_ref=mine, send_sem=send_sems.at[k],
                                             recv_sem=recv_sems.at[k], device_id=(x ^ f[0], y ^ f[1], c ^ f[2]),
                                             device_id_type=MESH) for k, f in enumerate(_FLIPS)]
    return copies


def _small_slab(packed):
    me8 = 4 * lax.axis_index("x") + 2 * lax.axis_index("y") + lax.axis_index("c")
    return lax.dynamic_update_slice(jnp.zeros((8,) + packed.shape, F32), packed[None], (me8, 0, 0))


def kernel(*args):
    A = dict(zip(ARG_NAMES, args, strict=True))
    x, mem, target = A['x'][0], A['mem'][0], A['loss_target'][0]

    stages = {'l0_mixer': ['l0_w_in', 'l0_s5_w_glu', 'l0_w_out'],
              'l0_common': [n for n in MATRIX_NAMES if n.startswith(('l0_xa_', 'l0_ffn_'))],
              'l1': [n for n in MATRIX_NAMES if n.startswith('l1_')]}
    W = _gather_weights(A, stages['l0_mixer'] + list(CONVS))
    for n in SMALL_NAMES:
        if n not in CONVS:
            W[n] = A[n]
    flights = {}
    after = W['l0_w_in']
    for stage in ('l0_common', 'l1'):
        flights[stage], after = _gather_weights_start(A, stages[stage], after, stage)
    W['l0_mix_norm'] = W['l0_mix_norm'] + after[0, 0]

    reduce_state, start_tokens = {}, {}

    def early_grads(stage, G, after=None):
        names = stages[stage]
        gs, widths = _slabbed(G, names)
        d2d = _other_half_copies([g.shape for g in gs])
        lands = [lax.empty((g.shape[0], g.shape[1] // 2, g.shape[2]), g.dtype) for g in gs]
        d2d_send, d2d_recv, gs, lands, zeros = _split_copy_start(gs, lands, d2d, G['final_norm'] if after is None else after,
                                                                 "d2d_start_" + stage)
        start_tokens[stage] = zeros

        def midway(after):
            mine, got = _split_copy_wait(d2d_send, d2d_recv, gs, lands, d2d, after, "d2d_wait_" + stage)
            pairs = [_pair_sum(g, o, "pair_sum_" + n) for n, g, o in zip(names, mine, got)]
            copies = _to_chips_copies([p.shape for p in pairs], widths)
            recv = [lax.empty((4, p.shape[1], w), p.dtype) for p, w in zip(pairs, widths)]
            send_sems, recv_sems, pairs, recv, zeros2 = _split_copy_start(pairs, recv, copies, G['final_norm'],
                                                                          "reduce_start_" + stage)
            reduce_state[stage] = (send_sems, recv_sems, pairs, recv, copies, widths)
            start_tokens[stage] = zeros2
            return zeros2[0, 0]

        return zeros[0, 0], midway

    loss_tile, grad_x, G = _local_step(
        x, mem, target, W, later_weights=lambda stage, after: _gather_weights_wait(flights[stage], after, stage),
        early_grads=early_grads)
    loss = lax.psum(loss_tile[0, 0], ("x", "y", "c"))

    small_copies = _small_copies()
    small_send, small_recv, _, small_land, started = _split_copy_start(
        [], [_small_slab(_pack_small({n: G[n] for n in SMALL_NAMES}))], small_copies, grad_x, "small_start")
    _, mixer_midway = early_grads('l0_mixer', G, after=started)
    grads, deltas, new_m, new_v = {}, {}, {}, {}

    def finish(stage, after):
        send_sems, recv_sems, pairs, lands, copies, widths = reduce_state[stage]
        sent, recv = _split_copy_wait(send_sems, recv_sems, pairs, lands, copies, after, "reduce_wait_" + stage)
        g_stage = _reduce_end(stages[stage], sent, recv, widths, stage)
        for n in stages[stage]:
            deltas[n], new_m[n], new_v[n], grads[n] = _adamw(A[n], g_stage[n], A['m_' + n], A['v_' + n], "adamw_" + n)
        return deltas[stages[stage][-1]]

    after = finish('l1', start_tokens['l0_mixer'])
    mixer_midway(after)
    finish('l0_common', start_tokens['l0_mixer'])
    after = finish('l0_mixer', deltas[stages['l0_common'][-1]])
    _, (g_small,) = _split_copy_wait(small_send, small_recv, [], small_land, small_copies, after, "small_wait")
    g_small = _unpack_small(_slab_sum(g_small, "sum_small"), {n: G[n].shape for n in SMALL_NAMES})
    me = 2 * lax.axis_index("x") + lax.axis_index("y")
    for n in CONVS:
        wd = A[n].shape[1]
        g_small[n] = lax.dynamic_slice_in_dim(g_small[n], me * wd, wd, axis=1)
    flat_names = [n for n in SMALL_NAMES if n not in CONVS]

    def pack_flat(prefix):
        return _pack_small_flat({n: A[prefix + n] for n in flat_names}, flat_names)

    shapes = {n: A[n].shape for n in flat_names}
    d_s, m_s, v_s, _ = _adamw(pack_flat(''), _pack_small_flat(g_small, flat_names), pack_flat('m_'), pack_flat('v_'),
                              "adamw_small")
    d_s, m_s, v_s = (_unpack_flat(p, shapes, flat_names) for p in (d_s, m_s, v_s))

    for n in WEIGHTS:
        if n in MATRICES:
            continue
        if n in CONVS:
            deltas[n], new_m[n], new_v[n], grads[n] = _adamw(A[n], g_small[n], A['m_' + n], A['v_' + n],
                                                             "adamw_" + n)
        else:
            grads[n] = g_small[n].reshape(A[n].shape)
            deltas[n], new_m[n], new_v[n] = d_s[n], m_s[n], v_s[n]
    return (loss, grad_x[None], *[grads[n] for n in WEIGHTS], *[deltas[n] for n in WEIGHTS],
            *[new_m[n] for n in WEIGHTS], *[new_v[n] for n in WEIGHTS])


def _pack_small_flat(vals, names):
    flat = jnp.concatenate([vals[n].astype(F32).reshape(-1) for n in names])
    rows = -(-flat.shape[0] // (8 * LANES)) * 8
    return jnp.pad(flat, (0, rows * LANES - flat.shape[0])).reshape(rows, LANES)


def _unpack_flat(packed, shapes, names):
    flat = packed.reshape(-1)
    out = {}
    off = 0
    for n in names:
        size = int(np.prod(shapes[n]))
        out[n] = flat[off:off + size].reshape(shapes[n])
        off += size
    return out
```
